```python
import jax, jax.numpy as jnp
from jax import lax
import numpy as np

D_MODEL = 1024
BATCH = 8
SEQ = 4096
DEPTH = 2

N_MIXERS = 2
D_TOK = 3 * D_MODEL // 4
D_MEM = D_MODEL - D_TOK
HG_EXPAND = 128
HG_HEADS = D_TOK // HG_EXPAND
HG_VDIM = D_TOK // HG_HEADS
HG_CHUNK = 64
GM_CHUNK = 128
GM_GROUPS = 6
GM_GDIM = D_TOK // GM_GROUPS
MEM_LEN = 256
MEM_HEADS = 4
MEM_HDIM = D_MEM // MEM_HEADS
D_FF = -(-8 * D_MODEL // (3 * 256)) * 256
N_A = (DEPTH + 1) // 2
N_B = DEPTH // 2
EPS = 1e-6

kernel_name = "hybrid_hgrn2_gmlp_memxattn"


def rmsnorm(x, g):
    xf = x.astype(jnp.float32)
    y = xf * lax.rsqrt(jnp.mean(xf * xf, axis=-1, keepdims=True) + EPS)
    return (y * g.astype(jnp.float32)).astype(x.dtype)


def hgrn2_mix(p, lb):
    B, S, _ = p.shape
    n = S // HG_CHUNK
    q, fz, iv, g = jnp.split(p.astype(jnp.float32), 4, axis=-1)
    lbf = lb.astype(jnp.float32)
    log_f = jnp.log(lbf + (1.0 - lbf) * jax.nn.sigmoid(fz))
    k = -jnp.expm1(log_f)

    def to_chunks(t, d):
        return t.reshape(B, n, HG_CHUNK, HG_HEADS, d).transpose(1, 0, 3, 2, 4)

    qc = to_chunks(q, HG_EXPAND)
    kc = to_chunks(k, HG_EXPAND)
    lc = to_chunks(log_f, HG_EXPAND)
    vc = to_chunks(iv, HG_VDIM)
    causal = jnp.tril(jnp.ones((HG_CHUNK, HG_CHUNK), dtype=bool))[None, None, :, :, None]

    def body(state, inp):
        qb, kb, vb, lg = inp
        b = jnp.cumsum(lg, axis=2)
        inter = jnp.einsum('bhtk,bhkv->bhtv', qb * jnp.exp(b), state)
        diff = b[:, :, :, None, :] - b[:, :, None, :, :]
        decay = jnp.where(causal, jnp.exp(jnp.minimum(diff, 0.0)), 0.0)
        scores = jnp.einsum('bhtk,bhsk,bhtsk->bhts', qb, kb, decay)
        intra = jnp.einsum('bhts,bhsv->bhtv', scores, vb)
        b_last = b[:, :, -1:, :]
        new_state = (jnp.exp(b_last[:, :, 0, :])[..., None] * state
                     + jnp.einsum('bhsk,bhsv->bhkv', kb * jnp.exp(b_last - b), vb))
        return new_state, inter + intra

    s0 = jnp.zeros((B, HG_HEADS, HG_EXPAND, HG_VDIM), jnp.float32)
    _, o = lax.scan(body, s0, (qc, kc, vc, lc))
    o = o.transpose(1, 0, 3, 2, 4).reshape(B, S, HG_HEADS, HG_VDIM)
    o = o * lax.rsqrt(jnp.mean(o * o, axis=-1, keepdims=True) + EPS)
    return o.reshape(B, S, D_TOK) * jax.nn.silu(g)


def gmlp_mix(p, ln_g, ln_b, ws, bs):
    B, S, _ = p.shape
    n = S // GM_CHUNK
    z = jax.nn.gelu(p.astype(jnp.float32), approximate=False)
    u, v = jnp.split(z, 2, axis=-1)
    mu = jnp.mean(v, axis=-1, keepdims=True)
    var = jnp.mean(jnp.square(v - mu), axis=-1, keepdims=True)
    v = (v - mu) * lax.rsqrt(var + EPS) * ln_g.astype(jnp.float32) + ln_b.astype(jnp.float32)
    v = v.reshape(B, n, GM_CHUNK, GM_GROUPS, GM_GDIM)
    w = ws.astype(jnp.float32) * jnp.tril(jnp.ones((GM_CHUNK, GM_CHUNK), jnp.float32))[None]
    sv = jnp.einsum('gts,bnsgc->bntgc', w, v) + bs.astype(jnp.float32).T[None, None, :, :, None]
    return u * sv.reshape(B, S, D_TOK)


def mem_attn(qm, mem, g, w_kv):
    B, S, _ = qm.shape
    m = rmsnorm(mem, g)
    kv = m @ w_kv
    k, v = jnp.split(kv, 2, axis=-1)
    k = k.reshape(B, MEM_LEN, MEM_HEADS, MEM_HDIM)
    v = v.reshape(B, MEM_LEN, MEM_HEADS, MEM_HDIM)
    q = qm.reshape(B, S, MEM_HEADS, MEM_HDIM)
    s = jnp.einsum('bshd,bmhd->bhsm', q, k).astype(jnp.float32) * (MEM_HDIM ** -0.5)
    pr = jax.nn.softmax(s, axis=-1)
    o = jnp.einsum('bhsm,bmhd->bshd', pr, v.astype(jnp.float32))
    return o.reshape(B, S, D_MEM)


def swiglu(h, w_in, w_out):
    a = h @ w_in
    gate, up = jnp.split(a, 2, axis=-1)
    return (jax.nn.silu(gate) * up) @ w_out


def _fwd_setup_inputs(seed: int = 0) -> dict:
    key = jax.random.key(seed)
    ks = jax.random.split(key, 20)
    f32 = jnp.float32

    def nrm(k, shape, s):
        return jax.random.normal(k, shape, f32) * s

    return {
        "x": nrm(ks[0], (BATCH, SEQ, D_MODEL), 1.0),
        "mem": nrm(ks[1], (BATCH, MEM_LEN, D_MODEL), 1.0),
        "mix_norm": 1.0 + nrm(ks[2], (DEPTH, D_MODEL), 0.02),
        "mem_norm": 1.0 + nrm(ks[3], (DEPTH, D_MODEL), 0.02),
        "w_mem_kv": nrm(ks[4], (DEPTH, D_MODEL, 2 * D_MEM), D_MODEL ** -0.5),
        "w_out": nrm(ks[5], (DEPTH, D_TOK + D_MEM, D_MODEL), (D_TOK + D_MEM) ** -0.5),
        "hg_w_in": nrm(ks[6], (N_A, D_MODEL, 4 * D_TOK + D_MEM), D_MODEL ** -0.5),
        "hg_lb": nrm(ks[7], (DEPTH + 1, D_TOK), 0.5),
        "hg_onorm": 1.0 + nrm(ks[8], (N_A, D_TOK), 0.02),
        "gm_w_in": nrm(ks[9], (N_B, D_MODEL, 2 * D_TOK + D_MEM), D_MODEL ** -0.5),
        "gm_ln_g": 1.0 + nrm(ks[10], (N_B, D_TOK), 0.02),
        "gm_ln_b": nrm(ks[11], (N_B, D_TOK), 0.02),
        "gm_ws": nrm(ks[12], (N_B, GM_GROUPS, GM_CHUNK, GM_CHUNK), GM_CHUNK ** -0.5),
        "gm_bs": 1.0 + nrm(ks[13], (N_B, GM_GROUPS, GM_CHUNK), 0.02),
        "ffn_norm": 1.0 + nrm(ks[14], (DEPTH, D_MODEL), 0.02),
        "w_ffn_in": nrm(ks[15], (DEPTH, D_MODEL, 2 * D_FF), D_MODEL ** -0.5),
        "w_ffn_out": nrm(ks[16], (DEPTH, D_FF, D_MODEL), D_FF ** -0.5),
        "final_norm": 1.0 + nrm(ks[17], (D_MODEL,), 0.02),
    }


def _fwd_reference(x, mem, mix_norm, mem_norm, w_mem_kv, w_out, hg_w_in, hg_lb, hg_onorm,
              gm_w_in, gm_ln_g, gm_ln_b, gm_ws, gm_bs, ffn_norm, w_ffn_in, w_ffn_out,
              final_norm):
    lb_all = jnp.cumsum(jax.nn.softmax(hg_lb.astype(jnp.float32), axis=0), axis=0)
    for i in range(DEPTH):
        h = rmsnorm(x, mix_norm[i])
        j = i // N_MIXERS
        if i % N_MIXERS == 0:
            p = h @ hg_w_in[j]
            tok = hgrn2_mix(p[..., :4 * D_TOK], lb_all[i]) * hg_onorm[j].astype(jnp.float32)
            qm = p[..., 4 * D_TOK:]
        else:
            p = h @ gm_w_in[j]
            tok = gmlp_mix(p[..., :2 * D_TOK], gm_ln_g[j], gm_ln_b[j], gm_ws[j], gm_bs[j])
            qm = p[..., 2 * D_TOK:]
        mo = mem_attn(qm, mem, mem_norm[i], w_mem_kv[i])
        heads = jnp.concatenate([tok, mo], axis=-1).astype(x.dtype)
        x = x + heads @ w_out[i]
        x = x + swiglu(rmsnorm(x, ffn_norm[i]), w_ffn_in[i], w_ffn_out[i])
    return rmsnorm(x, final_norm)


import jax as _jax
import jax.numpy as _jnp

TWIN_FORMAT = 'train_step'
FWD_PARAMS = ['x', 'mem', 'mix_norm', 'mem_norm', 'w_mem_kv', 'w_out', 'hg_w_in', 'hg_lb', 'hg_onorm', 'gm_w_in', 'gm_ln_g', 'gm_ln_b', 'gm_ws', 'gm_bs', 'ffn_norm', 'w_ffn_in', 'w_ffn_out', 'final_norm']
TWIN_WEIGHTS = ['mix_norm', 'mem_norm', 'w_mem_kv', 'w_out', 'hg_w_in', 'hg_lb', 'hg_onorm', 'gm_w_in', 'gm_ln_g', 'gm_ln_b', 'gm_ws', 'gm_bs', 'ffn_norm', 'w_ffn_in', 'w_ffn_out', 'final_norm']
TWIN_DIFF_INPUT = 'x'
TWIN_INPUTS = ['x', 'mem', 'mix_norm', 'mem_norm', 'w_mem_kv', 'w_out', 'hg_w_in', 'hg_lb', 'hg_onorm', 'gm_w_in', 'gm_ln_g', 'gm_ln_b', 'gm_ws', 'gm_bs', 'ffn_norm', 'w_ffn_in', 'w_ffn_out', 'final_norm', 'loss_target', 'm_mix_norm', 'm_mem_norm', 'm_w_mem_kv', 'm_w_out', 'm_hg_w_in', 'm_hg_lb', 'm_hg_onorm', 'm_gm_w_in', 'm_gm_ln_g', 'm_gm_ln_b', 'm_gm_ws', 'm_gm_bs', 'm_ffn_norm', 'm_w_ffn_in', 'm_w_ffn_out', 'm_final_norm', 'v_mix_norm', 'v_mem_norm', 'v_w_mem_kv', 'v_w_out', 'v_hg_w_in', 'v_hg_lb', 'v_hg_onorm', 'v_gm_w_in', 'v_gm_ln_g', 'v_gm_ln_b', 'v_gm_ws', 'v_gm_bs', 'v_ffn_norm', 'v_w_ffn_in', 'v_w_ffn_out', 'v_final_norm']
TWIN_OUTPUTS = ['loss', 'grad_x', 'grad_mix_norm', 'grad_mem_norm', 'grad_w_mem_kv', 'grad_w_out', 'grad_hg_w_in', 'grad_hg_lb', 'grad_hg_onorm', 'grad_gm_w_in', 'grad_gm_ln_g', 'grad_gm_ln_b', 'grad_gm_ws', 'grad_gm_bs', 'grad_ffn_norm', 'grad_w_ffn_in', 'grad_w_ffn_out', 'grad_final_norm', 'delta_mix_norm', 'delta_mem_norm', 'delta_w_mem_kv', 'delta_w_out', 'delta_hg_w_in', 'delta_hg_lb', 'delta_hg_onorm', 'delta_gm_w_in', 'delta_gm_ln_g', 'delta_gm_ln_b', 'delta_gm_ws', 'delta_gm_bs', 'delta_ffn_norm', 'delta_w_ffn_in', 'delta_w_ffn_out', 'delta_final_norm', 'new_m_mix_norm', 'new_m_mem_norm', 'new_m_w_mem_kv', 'new_m_w_out', 'new_m_hg_w_in', 'new_m_hg_lb', 'new_m_hg_onorm', 'new_m_gm_w_in', 'new_m_gm_ln_g', 'new_m_gm_ln_b', 'new_m_gm_ws', 'new_m_gm_bs', 'new_m_ffn_norm', 'new_m_w_ffn_in', 'new_m_w_ffn_out', 'new_m_final_norm', 'new_v_mix_norm', 'new_v_mem_norm', 'new_v_w_mem_kv', 'new_v_w_out', 'new_v_hg_w_in', 'new_v_hg_lb', 'new_v_hg_onorm', 'new_v_gm_w_in', 'new_v_gm_ln_g', 'new_v_gm_ln_b', 'new_v_gm_ws', 'new_v_gm_bs', 'new_v_ffn_norm', 'new_v_w_ffn_in', 'new_v_w_ffn_out', 'new_v_final_norm']
TWIN_LEAF_KINDS = {'loss': 'loss', 'grad_x': 'grad_x', 'grad_mix_norm': 'grad_w', 'grad_mem_norm': 'grad_w', 'grad_w_mem_kv': 'grad_w', 'grad_w_out': 'grad_w', 'grad_hg_w_in': 'grad_w', 'grad_hg_lb': 'grad_w', 'grad_hg_onorm': 'grad_w', 'grad_gm_w_in': 'grad_w', 'grad_gm_ln_g': 'grad_w', 'grad_gm_ln_b': 'grad_w', 'grad_gm_ws': 'grad_w', 'grad_gm_bs': 'grad_w', 'grad_ffn_norm': 'grad_w', 'grad_w_ffn_in': 'grad_w', 'grad_w_ffn_out': 'grad_w', 'grad_final_norm': 'grad_w', 'delta_mix_norm': 'delta_w', 'delta_mem_norm': 'delta_w', 'delta_w_mem_kv': 'delta_w', 'delta_w_out': 'delta_w', 'delta_hg_w_in': 'delta_w', 'delta_hg_lb': 'delta_w', 'delta_hg_onorm': 'delta_w', 'delta_gm_w_in': 'delta_w', 'delta_gm_ln_g': 'delta_w', 'delta_gm_ln_b': 'delta_w', 'delta_gm_ws': 'delta_w', 'delta_gm_bs': 'delta_w', 'delta_ffn_norm': 'delta_w', 'delta_w_ffn_in': 'delta_w', 'delta_w_ffn_out': 'delta_w', 'delta_final_norm': 'delta_w', 'new_m_mix_norm': 'new_m', 'new_m_mem_norm': 'new_m', 'new_m_w_mem_kv': 'new_m', 'new_m_w_out': 'new_m', 'new_m_hg_w_in': 'new_m', 'new_m_hg_lb': 'new_m', 'new_m_hg_onorm': 'new_m', 'new_m_gm_w_in': 'new_m', 'new_m_gm_ln_g': 'new_m', 'new_m_gm_ln_b': 'new_m', 'new_m_gm_ws': 'new_m', 'new_m_gm_bs': 'new_m', 'new_m_ffn_norm': 'new_m', 'new_m_w_ffn_in': 'new_m', 'new_m_w_ffn_out': 'new_m', 'new_m_final_norm': 'new_m', 'new_v_mix_norm': 'new_v', 'new_v_mem_norm': 'new_v', 'new_v_w_mem_kv': 'new_v', 'new_v_w_out': 'new_v', 'new_v_hg_w_in': 'new_v', 'new_v_hg_lb': 'new_v', 'new_v_hg_onorm': 'new_v', 'new_v_gm_w_in': 'new_v', 'new_v_gm_ln_g': 'new_v', 'new_v_gm_ln_b': 'new_v', 'new_v_gm_ws': 'new_v', 'new_v_gm_bs': 'new_v', 'new_v_ffn_norm': 'new_v', 'new_v_w_ffn_in': 'new_v', 'new_v_w_ffn_out': 'new_v', 'new_v_final_norm': 'new_v'}


def _forward(args):
    return _fwd_reference(*[args[k] for k in FWD_PARAMS])


def _output_shape():
    out = _jax.eval_shape(lambda: _forward(_fwd_setup_inputs(0)))
    return out.shape, out.dtype

N_MICROBATCH = 1
ADAM_LR = 0.001
ADAM_B1 = 0.9
ADAM_B2 = 0.999
ADAM_EPS = 1e-08
ADAM_WD = 0.01
ADAM_STEP = 10
PER_EXAMPLE_BATCH_AXIS = {'x': 0, 'mem': 0, 'loss_target': 0}
SHARED_INPUTS = []
_WEIGHT_DTYPES = {'mix_norm': _jnp.float32, 'mem_norm': _jnp.float32, 'w_mem_kv': _jnp.float32, 'w_out': _jnp.float32, 'hg_w_in': _jnp.float32, 'hg_lb': _jnp.float32, 'hg_onorm': _jnp.float32, 'gm_w_in': _jnp.float32, 'gm_ln_g': _jnp.float32, 'gm_ln_b': _jnp.float32, 'gm_ws': _jnp.float32, 'gm_bs': _jnp.float32, 'ffn_norm': _jnp.float32, 'w_ffn_in': _jnp.float32, 'w_ffn_out': _jnp.float32, 'final_norm': _jnp.float32}
MOMENT_SCALE = {'mix_norm': 1.794843e-01, 'mem_norm': 1.306103e-02, 'w_mem_kv': 1.785907e-02, 'w_out': 9.781423e-02, 'hg_w_in': 1.259099e-01, 'hg_lb': 3.707796e-02, 'hg_onorm': 1.299755e-01, 'gm_w_in': 8.434993e-02, 'gm_ln_g': 6.391010e-02, 'gm_ln_b': 6.054005e-02, 'gm_ws': 6.018730e-02, 'gm_bs': 8.458703e-02, 'ffn_norm': 1.199005e-01, 'w_ffn_in': 5.022717e-02, 'w_ffn_out': 8.208416e-02, 'final_norm': 3.204872e+01}


def _to_microbatches(a, axis):
    t = _jnp.moveaxis(a, axis, 0)
    t = t.reshape((N_MICROBATCH, t.shape[0] // N_MICROBATCH) + t.shape[1:])
    return _jnp.moveaxis(t, 1, axis + 1)


def setup_inputs(seed: int = 0) -> dict:
    inp = _fwd_setup_inputs(seed)
    key = _jax.random.fold_in(_jax.random.key(seed), 7919)
    shape, _ = _output_shape()
    out = dict(inp)
    out["loss_target"] = _jax.random.normal(_jax.random.fold_in(key, 0), shape, _jnp.float32)
    for i, name in enumerate(TWIN_WEIGHTS):
        w = inp[name].astype(_jnp.float32)
        if MOMENT_SCALE is None:
            s = _jnp.sqrt(_jnp.mean(_jnp.square(w)) + 1e-30)
        else:
            s = MOMENT_SCALE[name]
        km, kv = _jax.random.split(_jax.random.fold_in(key, i + 1))
        out[name] = w
        out["m_" + name] = s * _jax.random.normal(km, w.shape, _jnp.float32)
        out["v_" + name] = (s * s) * _jax.random.uniform(kv, w.shape, _jnp.float32, 0.5, 1.5)
    if N_MICROBATCH > 1:
        for name, axis in PER_EXAMPLE_BATCH_AXIS.items():
            out[name] = _to_microbatches(out[name], axis)
    return {'x': out['x'], 'mem': out['mem'], 'mix_norm': out['mix_norm'], 'mem_norm': out['mem_norm'], 'w_mem_kv': out['w_mem_kv'], 'w_out': out['w_out'], 'hg_w_in': out['hg_w_in'], 'hg_lb': out['hg_lb'], 'hg_onorm': out['hg_onorm'], 'gm_w_in': out['gm_w_in'], 'gm_ln_g': out['gm_ln_g'], 'gm_ln_b': out['gm_ln_b'], 'gm_ws': out['gm_ws'], 'gm_bs': out['gm_bs'], 'ffn_norm': out['ffn_norm'], 'w_ffn_in': out['w_ffn_in'], 'w_ffn_out': out['w_ffn_out'], 'final_norm': out['final_norm'], 'loss_target': out['loss_target'], 'm_mix_norm': out['m_mix_norm'], 'm_mem_norm': out['m_mem_norm'], 'm_w_mem_kv': out['m_w_mem_kv'], 'm_w_out': out['m_w_out'], 'm_hg_w_in': out['m_hg_w_in'], 'm_hg_lb': out['m_hg_lb'], 'm_hg_onorm': out['m_hg_onorm'], 'm_gm_w_in': out['m_gm_w_in'], 'm_gm_ln_g': out['m_gm_ln_g'], 'm_gm_ln_b': out['m_gm_ln_b'], 'm_gm_ws': out['m_gm_ws'], 'm_gm_bs': out['m_gm_bs'], 'm_ffn_norm': out['m_ffn_norm'], 'm_w_ffn_in': out['m_w_ffn_in'], 'm_w_ffn_out': out['m_w_ffn_out'], 'm_final_norm': out['m_final_norm'], 'v_mix_norm': out['v_mix_norm'], 'v_mem_norm': out['v_mem_norm'], 'v_w_mem_kv': out['v_w_mem_kv'], 'v_w_out': out['v_w_out'], 'v_hg_w_in': out['v_hg_w_in'], 'v_hg_lb': out['v_hg_lb'], 'v_hg_onorm': out['v_hg_onorm'], 'v_gm_w_in': out['v_gm_w_in'], 'v_gm_ln_g': out['v_gm_ln_g'], 'v_gm_ln_b': out['v_gm_ln_b'], 'v_gm_ws': out['v_gm_ws'], 'v_gm_bs': out['v_gm_bs'], 'v_ffn_norm': out['v_ffn_norm'], 'v_w_ffn_in': out['v_w_ffn_in'], 'v_w_ffn_out': out['v_w_ffn_out'], 'v_final_norm': out['v_final_norm']}


def _loss(weights, diff, rest, loss_target):
    with _jax.named_scope("forward"):
        args = {**rest, TWIN_DIFF_INPUT: diff, **{k: w.astype(_WEIGHT_DTYPES[k]) for k, w in weights.items()}}
        y = _forward(args)
    with _jax.named_scope("loss_head"):
        err = _jnp.square(y.astype(_jnp.float32) - loss_target)
        return 0.5 * _jnp.sum(_jnp.mean(err, axis=-1)) if err.ndim else 0.5 * err


def _adamw(w, g, m, v):
    m = ADAM_B1 * m + (1.0 - ADAM_B1) * g
    v = ADAM_B2 * v + (1.0 - ADAM_B2) * _jnp.square(g)
    m_hat = m / (1.0 - ADAM_B1 ** ADAM_STEP)
    v_hat = v / (1.0 - ADAM_B2 ** ADAM_STEP)
    delta = -ADAM_LR * (m_hat / (_jnp.sqrt(v_hat) + ADAM_EPS) + ADAM_WD * w)
    return delta, m, v


def reference(x, mem, mix_norm, mem_norm, w_mem_kv, w_out, hg_w_in, hg_lb, hg_onorm, gm_w_in, gm_ln_g, gm_ln_b, gm_ws, gm_bs, ffn_norm, w_ffn_in, w_ffn_out, final_norm, loss_target, m_mix_norm, m_mem_norm, m_w_mem_kv, m_w_out, m_hg_w_in, m_hg_lb, m_hg_onorm, m_gm_w_in, m_gm_ln_g, m_gm_ln_b, m_gm_ws, m_gm_bs, m_ffn_norm, m_w_ffn_in, m_w_ffn_out, m_final_norm, v_mix_norm, v_mem_norm, v_w_mem_kv, v_w_out, v_hg_w_in, v_hg_lb, v_hg_onorm, v_gm_w_in, v_gm_ln_g, v_gm_ln_b, v_gm_ws, v_gm_bs, v_ffn_norm, v_w_ffn_in, v_w_ffn_out, v_final_norm):
    given = dict(x=x, mem=mem, mix_norm=mix_norm, mem_norm=mem_norm, w_mem_kv=w_mem_kv, w_out=w_out, hg_w_in=hg_w_in, hg_lb=hg_lb, hg_onorm=hg_onorm, gm_w_in=gm_w_in, gm_ln_g=gm_ln_g, gm_ln_b=gm_ln_b, gm_ws=gm_ws, gm_bs=gm_bs, ffn_norm=ffn_norm, w_ffn_in=w_ffn_in, w_ffn_out=w_ffn_out, final_norm=final_norm, loss_target=loss_target, m_mix_norm=m_mix_norm, m_mem_norm=m_mem_norm, m_w_mem_kv=m_w_mem_kv, m_w_out=m_w_out, m_hg_w_in=m_hg_w_in, m_hg_lb=m_hg_lb, m_hg_onorm=m_hg_onorm, m_gm_w_in=m_gm_w_in, m_gm_ln_g=m_gm_ln_g, m_gm_ln_b=m_gm_ln_b, m_gm_ws=m_gm_ws, m_gm_bs=m_gm_bs, m_ffn_norm=m_ffn_norm, m_w_ffn_in=m_w_ffn_in, m_w_ffn_out=m_w_ffn_out, m_final_norm=m_final_norm, v_mix_norm=v_mix_norm, v_mem_norm=v_mem_norm, v_w_mem_kv=v_w_mem_kv, v_w_out=v_w_out, v_hg_w_in=v_hg_w_in, v_hg_lb=v_hg_lb, v_hg_onorm=v_hg_onorm, v_gm_w_in=v_gm_w_in, v_gm_ln_g=v_gm_ln_g, v_gm_ln_b=v_gm_ln_b, v_gm_ws=v_gm_ws, v_gm_bs=v_gm_bs, v_ffn_norm=v_ffn_norm, v_w_ffn_in=v_w_ffn_in, v_w_ffn_out=v_w_ffn_out, v_final_norm=v_final_norm)
    weights = {n: given[n] for n in TWIN_WEIGHTS}
    shared = {n: given[n] for n in SHARED_INPUTS}
    per_example = {n: given[n] for n in ['x', 'mem']}
    grad_fn = _jax.value_and_grad(_loss, argnums=(0, 1))

    def one_microbatch(ex, loss_target):
        ex = dict(ex)
        diff = ex.pop(TWIN_DIFF_INPUT)
        return grad_fn(weights, diff, {**shared, **ex}, loss_target)

    if N_MICROBATCH == 1:
        loss, (grad_w, grad_x) = one_microbatch(per_example, given["loss_target"])
    else:
        def body(carry, xs):
            loss_sum, grad_sum = carry
            l_k, (gw_k, gx_k) = one_microbatch(xs[0], xs[1])
            with _jax.named_scope("update"):
                return (loss_sum + l_k, _jax.tree.map(_jnp.add, grad_sum, gw_k)), gx_k

        init = (_jnp.zeros((), _jnp.float32), _jax.tree.map(_jnp.zeros_like, weights))
        (loss, grad_w), grad_x = _jax.lax.scan(body, init, (per_example, given["loss_target"]))
    with _jax.named_scope("update"):
        delta_w, new_m, new_v = {}, {}, {}
        for n in TWIN_WEIGHTS:
            delta_w[n], new_m[n], new_v[n] = _adamw(weights[n], grad_w[n], given["m_" + n], given["v_" + n])
    return (loss, grad_x, *[grad_w[n] for n in TWIN_WEIGHTS], *[delta_w[n] for n in TWIN_WEIGHTS],
            *[new_m[n] for n in TWIN_WEIGHTS], *[new_v[n] for n in TWIN_WEIGHTS])
```

```python
import functools

import jax
import jax.numpy as jnp
from jax import lax
from jax.experimental import pallas as pl
from jax.experimental.pallas import tpu as pltpu

F32 = jnp.float32
BF16 = jnp.bfloat16
MXU_DTYPE = jnp.bfloat16
MESH_ID = pl.DeviceIdType.MESH

N_DEV = 8
EPS = 1e-6
D_MODEL = 1024
D_TOK = 768
D_MEM = 256
N_HEADS = 6
HEAD = 128
MEM_HEADS = 4
MEM_HDIM = 64
GM_CHUNK = 128
D_FF = 2816
HG_SUB = 16
LANE = 128

ADAM_LR = 0.001
ADAM_B1 = 0.9
ADAM_B2 = 0.999
ADAM_EPS = 1e-08
ADAM_WD = 0.01
ADAM_STEP = 10

VMEM_LIMIT = 48 * 2 ** 20


def _params(sem=None):
    return pltpu.CompilerParams(dimension_semantics=sem, vmem_limit_bytes=VMEM_LIMIT)


def _tile(n, cap, q=LANE):
    if n <= cap:
        return n
    best = None
    for t in range(q, cap + 1, q):
        if n % t == 0:
            best = t
    assert best is not None, (n, cap, q)
    return best


def _sigmoid(x):
    return 1.0 / (1.0 + jnp.exp(-x))


def _gelu(x):
    return 0.5 * x * (1.0 + lax.erf(x * 0.7071067811865476))


def _gelu_grad(x):
    return 0.5 * (1.0 + lax.erf(x * 0.7071067811865476)) + x * jnp.exp(-0.5 * x * x) * 0.3989422804014327


def _matmul(a, b, *, name, ta=False, tb=False, res=None, out_dtype=F32):
    K, M = a.shape if ta else a.shape[::-1]
    N = b.shape[0] if tb else b.shape[1]
    assert (b.shape[1] if tb else b.shape[0]) == K
    tm = _tile(M, 512)
    tn = _tile(N, 1792)
    tk = _tile(K, 1408)
    nk = K // tk
    dims = (((0 if ta else 1,), (1 if tb else 0,)), ((), ()))

    def body(*refs):
        if res is None:
            a_ref, b_ref, o_ref, acc = refs
        else:
            a_ref, b_ref, r_ref, o_ref, acc = refs
        k = pl.program_id(2)

        @pl.when(k == 0)
        def _():
            acc[...] = jnp.zeros_like(acc)

        acc[...] += lax.dot_general(a_ref[...].astype(MXU_DTYPE), b_ref[...].astype(MXU_DTYPE), dims,
                                    preferred_element_type=F32)

        @pl.when(k == nk - 1)
        def _():
            r = acc[...]
            if res is not None:
                r = r + r_ref[...].astype(F32)
            o_ref[...] = r.astype(out_dtype)

    a_spec = pl.BlockSpec((tk, tm), lambda i, j, k: (k, i)) if ta else pl.BlockSpec((tm, tk), lambda i, j, k: (i, k))
    b_spec = pl.BlockSpec((tn, tk), lambda i, j, k: (j, k)) if tb else pl.BlockSpec((tk, tn), lambda i, j, k: (k, j))
    o_spec = pl.BlockSpec((tm, tn), lambda i, j, k: (i, j))
    in_specs = [a_spec, b_spec] + ([o_spec] if res is not None else [])
    args = (a, b) + ((res,) if res is not None else ())
    return pl.pallas_call(
        body, name=name, grid=(M // tm, N // tn, nk), in_specs=in_specs, out_specs=o_spec,
        out_shape=jax.ShapeDtypeStruct((M, N), out_dtype), scratch_shapes=[pltpu.VMEM((tm, tn), F32)],
        compiler_params=_params(("parallel", "parallel", "arbitrary")))(*args)


def _rms_fwd(x, g, *, name):
    R, Dm = x.shape
    tr = _tile(R, 512, 8)

    def body(x_ref, g_ref, o_ref):
        xv = x_ref[...]
        r = lax.rsqrt(jnp.mean(xv * xv, axis=-1, keepdims=True) + EPS)
        o_ref[...] = (xv * r * g_ref[...]).astype(o_ref.dtype)

    return pl.pallas_call(
        body, name=name, grid=(R // tr,),
        in_specs=[pl.BlockSpec((tr, Dm), lambda i: (i, 0)), pl.BlockSpec((1, Dm), lambda i: (0, 0))],
        out_specs=pl.BlockSpec((tr, Dm), lambda i: (i, 0)), out_shape=jax.ShapeDtypeStruct((R, Dm), BF16),
        compiler_params=_params(("parallel",)))(x, g)


def _rms_bwd(x, g, dh, dres, *, name):
    R, Dm = x.shape
    tr = _tile(R, 256, 8)

    def body(x_ref, g_ref, dh_ref, dres_ref, dx_ref, dg_ref):
        @pl.when(pl.program_id(0) == 0)
        def _():
            dg_ref[...] = jnp.zeros_like(dg_ref)

        xv = x_ref[...]
        r = lax.rsqrt(jnp.mean(xv * xv, axis=-1, keepdims=True) + EPS)
        xh = xv * r
        dhv = dh_ref[...].astype(F32)
        dg_ref[...] += jnp.sum(dhv * xh, axis=0, keepdims=True)
        u = dhv * g_ref[...]
        dx = r * (u - xh * jnp.mean(u * xh, axis=-1, keepdims=True))
        dx_ref[...] = dres_ref[...] + dx

    row = pl.BlockSpec((tr, Dm), lambda i: (i, 0))
    vec = pl.BlockSpec((1, Dm), lambda i: (0, 0))
    return pl.pallas_call(
        body, name=name, grid=(R // tr,), in_specs=[row, vec, row, row], out_specs=[row, vec],
        out_shape=[jax.ShapeDtypeStruct((R, Dm), F32), jax.ShapeDtypeStruct((1, Dm), F32)],
        compiler_params=_params(("arbitrary",)))(x, g, dh, dres)


def _final_loss(x, g, tgt, *, name):
    R, Dm = x.shape
    tr = _tile(R, 256, 8)

    def body(x_ref, g_ref, t_ref, dx_ref, dg_ref, loss_ref):
        @pl.when(pl.program_id(0) == 0)
        def _():
            dg_ref[...] = jnp.zeros_like(dg_ref)
            loss_ref[...] = jnp.zeros_like(loss_ref)

        xv = x_ref[...]
        r = lax.rsqrt(jnp.mean(xv * xv, axis=-1, keepdims=True) + EPS)
        xh = xv * r
        gv = g_ref[...]
        err = xh * gv - t_ref[...]
        part = 0.5 * jnp.sum(jnp.mean(err * err, axis=-1, keepdims=True), axis=0, keepdims=True)
        loss_ref[...] += jnp.broadcast_to(part, loss_ref.shape)
        dy = err * (1.0 / Dm)
        dg_ref[...] += jnp.sum(dy * xh, axis=0, keepdims=True)
        u = dy * gv
        dx_ref[...] = r * (u - xh * jnp.mean(u * xh, axis=-1, keepdims=True))

    row = pl.BlockSpec((tr, Dm), lambda i: (i, 0))
    vec = pl.BlockSpec((1, Dm), lambda i: (0, 0))
    one = pl.BlockSpec((1, LANE), lambda i: (0, 0))
    return pl.pallas_call(
        body, name=name, grid=(R // tr,), in_specs=[row, vec, row], out_specs=[row, vec, one],
        out_shape=[jax.ShapeDtypeStruct((R, Dm), F32), jax.ShapeDtypeStruct((1, Dm), F32),
                   jax.ShapeDtypeStruct((1, LANE), F32)],
        compiler_params=_params(("arbitrary",)))(x, g, tgt)


def _swiglu_fwd(gu, *, name):
    R = gu.shape[0]
    tr = _tile(R, 256, 8)

    def body(gu_ref, o_ref):
        gate = gu_ref[:, :D_FF]
        up = gu_ref[:, D_FF:]
        o_ref[...] = (gate * _sigmoid(gate) * up).astype(o_ref.dtype)

    return pl.pallas_call(
        body, name=name, grid=(R // tr,), in_specs=[pl.BlockSpec((tr, 2 * D_FF), lambda i: (i, 0))],
        out_specs=pl.BlockSpec((tr, D_FF), lambda i: (i, 0)), out_shape=jax.ShapeDtypeStruct((R, D_FF), BF16),
        compiler_params=_params(("parallel",)))(gu)


def _swiglu_bwd(gu, dact, *, name):
    R = gu.shape[0]
    tr = _tile(R, 256, 8)

    def body(gu_ref, da_ref, o_ref):
        gate = gu_ref[:, :D_FF]
        up = gu_ref[:, D_FF:]
        da = da_ref[...]
        sg = _sigmoid(gate)
        o_ref[:, :D_FF] = (da * up * sg * (1.0 + gate * (1.0 - sg))).astype(o_ref.dtype)
        o_ref[:, D_FF:] = (da * gate * sg).astype(o_ref.dtype)

    return pl.pallas_call(
        body, name=name, grid=(R // tr,),
        in_specs=[pl.BlockSpec((tr, 2 * D_FF), lambda i: (i, 0)), pl.BlockSpec((tr, D_FF), lambda i: (i, 0))],
        out_specs=pl.BlockSpec((tr, 2 * D_FF), lambda i: (i, 0)),
        out_shape=jax.ShapeDtypeStruct((R, 2 * D_FF), BF16), compiler_params=_params(("parallel",)))(gu, dact)


def _head_mask(h):
    lane = lax.broadcasted_iota(jnp.int32, (1, D_MEM), 1)
    return (lane >= h * MEM_HDIM) & (lane < (h + 1) * MEM_HDIM)


def _attn_probs(qv, k_mx, mask):
    s = lax.dot_general(jnp.where(mask, qv, 0.0).astype(MXU_DTYPE), k_mx, (((1,), (1,)), ((), ())),
                        preferred_element_type=F32) * (MEM_HDIM ** -0.5)
    e = jnp.exp(s - jnp.max(s, axis=-1, keepdims=True))
    return e / jnp.sum(e, axis=-1, keepdims=True)


def _attn_fwd(p, qcol, kv, *, name):
    S = p.shape[0]
    M = kv.shape[0]
    ts = _tile(S, 512, 8)

    def body(q_ref, k_ref, v_ref, o_ref):
        qv = q_ref[...]
        kx = k_ref[...].astype(MXU_DTYPE)
        vv = v_ref[...]
        out = jnp.zeros((ts, D_MEM), F32)
        for h in range(MEM_HEADS):
            mask = _head_mask(h)
            pr = _attn_probs(qv, kx, mask)
            out = out + jnp.dot(pr.astype(MXU_DTYPE), jnp.where(mask, vv, 0.0).astype(MXU_DTYPE),
                                preferred_element_type=F32)
        o_ref[...] = out.astype(o_ref.dtype)

    return pl.pallas_call(
        body, name=name, grid=(S // ts,),
        in_specs=[pl.BlockSpec((ts, D_MEM), lambda i: (i, qcol)), pl.BlockSpec((M, D_MEM), lambda i: (0, 0)),
                  pl.BlockSpec((M, D_MEM), lambda i: (0, 1))],
        out_specs=pl.BlockSpec((ts, D_MEM), lambda i: (i, 0)), out_shape=jax.ShapeDtypeStruct((S, D_MEM), BF16),
        compiler_params=_params(("parallel",)))(p, kv, kv)


def _attn_bwd(p, qcol, kv, dheads, *, name):
    S = p.shape[0]
    M = kv.shape[0]
    ts = _tile(S, 512, 8)
    scale = MEM_HDIM ** -0.5

    def body(q_ref, k_ref, v_ref, do_ref, dq_ref, dk_ref, dv_ref):
        @pl.when(pl.program_id(0) == 0)
        def _():
            dk_ref[...] = jnp.zeros_like(dk_ref)
            dv_ref[...] = jnp.zeros_like(dv_ref)

        qv = q_ref[...]
        kv_ = k_ref[...]
        kx = kv_.astype(MXU_DTYPE)
        vv = v_ref[...]
        do = do_ref[...].astype(F32)
        dox = do.astype(MXU_DTYPE)
        qx = qv.astype(MXU_DTYPE)
        dq = jnp.zeros((ts, D_MEM), F32)
        for h in range(MEM_HEADS):
            mask = _head_mask(h)
            pr = _attn_probs(qv, kx, mask)
            vh = jnp.where(mask, vv, 0.0).astype(MXU_DTYPE)
            dp = lax.dot_general(dox, vh, (((1,), (1,)), ((), ())), preferred_element_type=F32)
            ds = (pr * (dp - jnp.sum(dp * pr, axis=-1, keepdims=True)) * scale).astype(MXU_DTYPE)
            dq = dq + jnp.dot(ds, jnp.where(mask, kv_, 0.0).astype(MXU_DTYPE), preferred_element_type=F32)
            dk_h = lax.dot_general(ds, qx, (((0,), (0,)), ((), ())), preferred_element_type=F32)
            dv_h = lax.dot_general(pr.astype(MXU_DTYPE), dox, (((0,), (0,)), ((), ())), preferred_element_type=F32)
            dk_ref[...] += jnp.where(mask, dk_h, 0.0)
            dv_ref[...] += jnp.where(mask, dv_h, 0.0)
        dq_ref[...] = dq.astype(dq_ref.dtype)

    return pl.pallas_call(
        body, name=name, grid=(S // ts,),
        in_specs=[pl.BlockSpec((ts, D_MEM), lambda i: (i, qcol)), pl.BlockSpec((M, D_MEM), lambda i: (0, 0)),
                  pl.BlockSpec((M, D_MEM), lambda i: (0, 1)), pl.BlockSpec((ts, D_MEM), lambda i: (i, 3))],
        out_specs=[pl.BlockSpec((ts, D_MEM), lambda i: (i, 0)), pl.BlockSpec((M, D_MEM), lambda i: (0, 0)),
                   pl.BlockSpec((M, D_MEM), lambda i: (0, 0))],
        out_shape=[jax.ShapeDtypeStruct((S, D_MEM), BF16), jax.ShapeDtypeStruct((M, D_MEM), F32),
                   jax.ShapeDtypeStruct((M, D_MEM), F32)],
        compiler_params=_params(("arbitrary",)))(p, kv, kv, dheads)


def _gm_forward_parts(u_ref, v_ref, lng_ref, lnb_ref, w_ref, bsb_ref):
    zu = _gelu(u_ref[...])
    zv = _gelu(v_ref[...])
    mu = jnp.mean(zv, axis=-1, keepdims=True)
    cen = zv - mu
    rs = lax.rsqrt(jnp.mean(cen * cen, axis=-1, keepdims=True) + EPS)
    vh = cen * rs
    vn = vh * lng_ref[...] + lnb_ref[...]
    row = lax.broadcasted_iota(jnp.int32, (GM_CHUNK, GM_CHUNK), 0)
    col = lax.broadcasted_iota(jnp.int32, (GM_CHUNK, GM_CHUNK), 1)
    tril = row >= col
    wm = [jnp.where(tril, w_ref[g], 0.0).astype(MXU_DTYPE) for g in range(N_HEADS)]
    vnx = [vn[:, g * HEAD:(g + 1) * HEAD].astype(MXU_DTYPE) for g in range(N_HEADS)]
    sv = [jnp.dot(wm[g], vnx[g], preferred_element_type=F32) + bsb_ref[g] for g in range(N_HEADS)]
    return zu, vh, rs, wm, vnx, sv, tril


def _gmlp_fwd(p, lng, lnb, ws, bsb, *, name):
    S = p.shape[0]

    def body(u_ref, v_ref, lng_ref, lnb_ref, w_ref, bsb_ref, o_ref):
        zu, _, _, _, _, sv, _ = _gm_forward_parts(u_ref, v_ref, lng_ref, lnb_ref, w_ref, bsb_ref)
        for g in range(N_HEADS):
            o_ref[:, g * HEAD:(g + 1) * HEAD] = (zu[:, g * HEAD:(g + 1) * HEAD] * sv[g]).astype(o_ref.dtype)

    blk = lambda c: pl.BlockSpec((GM_CHUNK, D_TOK), lambda i: (i, c))
    vec = pl.BlockSpec((1, D_TOK), lambda i: (0, 0))
    cube = pl.BlockSpec((N_HEADS, GM_CHUNK, GM_CHUNK), lambda i: (0, 0, 0))
    return pl.pallas_call(
        body, name=name, grid=(S // GM_CHUNK,), in_specs=[blk(0), blk(1), vec, vec, cube, cube],
        out_specs=blk(0), out_shape=jax.ShapeDtypeStruct((S, D_TOK), BF16),
        compiler_params=_params(("parallel",)))(p, p, lng, lnb, ws, bsb)


def _gmlp_bwd(p, lng, lnb, ws, bsb, dheads, *, name):
    S = p.shape[0]

    def body(u_ref, v_ref, lng_ref, lnb_ref, w_ref, bsb_ref, dt_ref, dpu_ref, dpv_ref, dw_ref, dbs_ref, dlg_ref,
             dlb_ref):
        @pl.when(pl.program_id(0) == 0)
        def _():
            dw_ref[...] = jnp.zeros_like(dw_ref)
            dbs_ref[...] = jnp.zeros_like(dbs_ref)
            dlg_ref[...] = jnp.zeros_like(dlg_ref)
            dlb_ref[...] = jnp.zeros_like(dlb_ref)

        zu, vh, rs, wm, vnx, sv, tril = _gm_forward_parts(u_ref, v_ref, lng_ref, lnb_ref, w_ref, bsb_ref)
        dt = dt_ref[...].astype(F32)
        dvn_parts = []
        for g in range(N_HEADS):
            sl = slice(g * HEAD, (g + 1) * HEAD)
            dsv = dt[:, sl] * zu[:, sl]
            dpu_ref[:, sl] = (dt[:, sl] * sv[g] * _gelu_grad(u_ref[:, sl])).astype(dpu_ref.dtype)
            dsx = dsv.astype(MXU_DTYPE)
            dw = lax.dot_general(dsx, vnx[g], (((1,), (1,)), ((), ())), preferred_element_type=F32)
            dw_ref[g] += jnp.where(tril, dw, 0.0)
            dbs_ref[g] += jnp.sum(dsv, axis=-1, keepdims=True)
            dvn_parts.append(lax.dot_general(wm[g], dsx, (((0,), (0,)), ((), ())), preferred_element_type=F32))
        dvn = jnp.concatenate(dvn_parts, axis=-1)
        dlg_ref[...] += jnp.sum(dvn * vh, axis=0, keepdims=True)
        dlb_ref[...] += jnp.sum(dvn, axis=0, keepdims=True)
        dvh = dvn * lng_ref[...]
        dzv = rs * (dvh - jnp.mean(dvh, axis=-1, keepdims=True) - vh * jnp.mean(dvh * vh, axis=-1, keepdims=True))
        dpv_ref[...] = (dzv * _gelu_grad(v_ref[...])).astype(dpv_ref.dtype)

    blk = lambda c: pl.BlockSpec((GM_CHUNK, D_TOK), lambda i: (i, c))
    vec = pl.BlockSpec((1, D_TOK), lambda i: (0, 0))
    cube = pl.BlockSpec((N_HEADS, GM_CHUNK, GM_CHUNK), lambda i: (0, 0, 0))
    col = pl.BlockSpec((N_HEADS, GM_CHUNK, 1), lambda i: (0, 0, 0))
    return pl.pallas_call(
        body, name=name, grid=(S // GM_CHUNK,), in_specs=[blk(0), blk(1), vec, vec, cube, cube, blk(0)],
        out_specs=[blk(0), blk(0), cube, col, vec, vec],
        out_shape=[jax.ShapeDtypeStruct((S, D_TOK), BF16), jax.ShapeDtypeStruct((S, D_TOK), BF16),
                   jax.ShapeDtypeStruct((N_HEADS, GM_CHUNK, GM_CHUNK), F32),
                   jax.ShapeDtypeStruct((N_HEADS, GM_CHUNK, 1), F32), jax.ShapeDtypeStruct((1, D_TOK), F32),
                   jax.ShapeDtypeStruct((1, D_TOK), F32)],
        compiler_params=_params(("arbitrary",)))(p, p, lng, lnb, ws, bsb, dheads)


def _tri(n, upper):
    r = lax.broadcasted_iota(jnp.int32, (n, n), 0)
    c = lax.broadcasted_iota(jnp.int32, (n, n), 1)
    return jnp.where((r <= c) if upper else (r >= c), 1.0, 0.0).astype(F32)


def _hg_gates(fz, lb):
    sg = _sigmoid(fz)
    f = lb + (1.0 - lb) * sg
    kk = (1.0 - lb) * (1.0 - sg)
    return sg, f, jnp.log(f), kk


def _hgrn2_fwd(p, lb, onorm, *, name):
    S = p.shape[0]
    C = HG_SUB
    tb = _tile(S, 512, C)
    nsub = tb // C

    def body(q_ref, fz_ref, v_ref, g_ref, lb_ref, on_ref, tok_ref, o_ref, st_ref, state, bsc, ksc, vsc):
        @pl.when(pl.program_id(1) == 0)
        def _():
            state[...] = jnp.zeros_like(state)

        lbv = lb_ref[...]
        onv = on_ref[...]
        lower = _tri(C, False)
        tt = lax.broadcasted_iota(jnp.int32, (C, HEAD), 0)

        def sub(c, carry):
            rows = pl.ds(pl.multiple_of(c * C, C), C)
            qv = q_ref[rows, :]
            vv = v_ref[rows, :]
            gv = g_ref[rows, :]
            _, _, lg, kk = _hg_gates(fz_ref[rows, :], lbv)
            b = jnp.dot(lower, lg, precision=lax.Precision.HIGHEST, preferred_element_type=F32)
            st0 = state[...]
            st_ref[c] = st0
            inter = lax.dot_general((qv * jnp.exp(b)).astype(MXU_DTYPE), st0.astype(MXU_DTYPE),
                                    (((1,), (1,)), ((), ())), preferred_element_type=F32)
            bsc[...] = b
            ksc[...] = kk
            vsc[...] = vv
            intra = jnp.zeros((C, HEAD), F32)
            for s in range(C):
                dec = jnp.where(tt >= s, jnp.exp(jnp.minimum(b - bsc[pl.ds(s, 1), :], 0.0)), 0.0)
                a_s = jnp.sum(qv * ksc[pl.ds(s, 1), :] * dec, axis=-1, keepdims=True)
                intra = intra + a_s * vsc[pl.ds(s, 1), :]
            o = inter + intra
            b_last = bsc[pl.ds(C - 1, 1), :]
            ke = kk * jnp.exp(b_last - b)
            state[...] = st0 * jnp.exp(b_last) + lax.dot_general(
                vv.astype(MXU_DTYPE), ke.astype(MXU_DTYPE), (((0,), (0,)), ((), ())), preferred_element_type=F32)
            o_ref[rows, :] = o
            n = o * lax.rsqrt(jnp.mean(o * o, axis=-1, keepdims=True) + EPS)
            tok_ref[rows, :] = (n * (gv * _sigmoid(gv)) * onv).astype(tok_ref.dtype)
            return carry

        lax.fori_loop(0, nsub, sub, 0)

    blk = lambda c: pl.BlockSpec((tb, HEAD), lambda h, i, c=c: (i, c * N_HEADS + h))
    vec = pl.BlockSpec((1, HEAD), lambda h, i: (0, h))
    outb = pl.BlockSpec((tb, HEAD), lambda h, i: (i, h))
    stb = pl.BlockSpec((None, nsub, HEAD, HEAD), lambda h, i: (h, i, 0, 0))
    return pl.pallas_call(
        body, name=name, grid=(N_HEADS, S // tb), in_specs=[blk(0), blk(1), blk(2), blk(3), vec, vec],
        out_specs=[outb, outb, stb],
        out_shape=[jax.ShapeDtypeStruct((S, D_TOK), BF16), jax.ShapeDtypeStruct((S, D_TOK), F32),
                   jax.ShapeDtypeStruct((N_HEADS, S // C, HEAD, HEAD), F32)],
        scratch_shapes=[pltpu.VMEM((HEAD, HEAD), F32)] + [pltpu.VMEM((C, HEAD), F32)] * 3,
        compiler_params=_params(("parallel", "arbitrary")))(p, p, p, p, lb, onorm)


def _hgrn2_bwd(p, lb, onorm, o, states, dheads, *, name):
    S = p.shape[0]
    C = HG_SUB
    tb = _tile(S, 512, C)
    nsub = tb // C
    nblk = S // tb

    def body(q_ref, fz_ref, v_ref, g_ref, lb_ref, on_ref, o_ref, st_ref, dt_ref, dq_ref, dfz_ref, dv_ref, dg_ref,
             dlb_ref, don_ref, dstate, bsc, ksc, vsc, qsc, dosc):
        @pl.when(pl.program_id(1) == 0)
        def _():
            dstate[...] = jnp.zeros_like(dstate)
            dlb_ref[...] = jnp.zeros_like(dlb_ref)
            don_ref[...] = jnp.zeros_like(don_ref)

        lbv = lb_ref[...]
        onv = on_ref[...]
        lower = _tri(C, False)
        upper = _tri(C, True)
        tt = lax.broadcasted_iota(jnp.int32, (C, HEAD), 0)

        def sub(j, carry):
            c = nsub - 1 - j
            rows = pl.ds(pl.multiple_of(c * C, C), C)
            qv = q_ref[rows, :]
            vv = v_ref[rows, :]
            gv = g_ref[rows, :]
            ov = o_ref[rows, :]
            dt = dt_ref[rows, :].astype(F32)
            sgg = _sigmoid(gv)
            sil = gv * sgg
            rinv = lax.rsqrt(jnp.mean(ov * ov, axis=-1, keepdims=True) + EPS)
            n = ov * rinv
            don_ref[...] += jnp.sum(dt * n * sil, axis=0, keepdims=True)
            dn = dt * sil * onv
            dg_ref[rows, :] = (dt * n * onv * sgg * (1.0 + gv * (1.0 - sgg))).astype(dg_ref.dtype)
            do = rinv * (dn - n * jnp.mean(dn * n, axis=-1, keepdims=True))
            sg, f, lg, kk = _hg_gates(fz_ref[rows, :], lbv)
            b = jnp.dot(lower, lg, precision=lax.Precision.HIGHEST, preferred_element_type=F32)
            bsc[...] = b
            ksc[...] = kk
            vsc[...] = vv
            qsc[...] = qv
            dosc[...] = do
            b_last = bsc[pl.ds(C - 1, 1), :]
            eb = jnp.exp(b)
            qe = qv * eb
            ebb = jnp.exp(b_last - b)
            ke = kk * ebb
            e_last = jnp.exp(b_last)
            st0 = st_ref[c]
            dst1 = dstate[...]
            st0x = st0.astype(MXU_DTYPE)
            dst1x = dst1.astype(MXU_DTYPE)
            dox = do.astype(MXU_DTYPE)
            dqe = jnp.dot(dox, st0x, preferred_element_type=F32)
            dke = jnp.dot(vv.astype(MXU_DTYPE), dst1x, preferred_element_type=F32)
            dv = lax.dot_general(ke.astype(MXU_DTYPE), dst1x, (((1,), (1,)), ((), ())), preferred_element_type=F32)
            db_last = e_last * jnp.sum(st0 * dst1, axis=0, keepdims=True) + jnp.sum(dke * ke, axis=0, keepdims=True)
            dstate[...] = dst1 * e_last + lax.dot_general(dox, qe.astype(MXU_DTYPE), (((0,), (0,)), ((), ())),
                                                          preferred_element_type=F32)
            dq = dqe * eb
            db = dqe * qe - dke * ke
            dkk = dke * ebb
            for s in range(C):
                dec = jnp.where(tt >= s, jnp.exp(jnp.minimum(b - bsc[pl.ds(s, 1), :], 0.0)), 0.0)
                da_s = jnp.sum(do * vsc[pl.ds(s, 1), :], axis=-1, keepdims=True)
                pq = da_s * ksc[pl.ds(s, 1), :] * dec
                dq = dq + pq
                db = db + pq * qv
            for t in range(C):
                q_t = qsc[pl.ds(t, 1), :]
                do_t = dosc[pl.ds(t, 1), :]
                dec = jnp.where(tt <= t, jnp.exp(jnp.minimum(bsc[pl.ds(t, 1), :] - b, 0.0)), 0.0)
                da_t = jnp.sum(vv * do_t, axis=-1, keepdims=True)
                pk = da_t * q_t * dec
                dkk = dkk + pk
                db = db - pk * kk
                a_t = jnp.sum(q_t * kk * dec, axis=-1, keepdims=True)
                dv = dv + a_t * do_t
            db = db + jnp.where(tt == C - 1, db_last, 0.0)
            dlg = jnp.dot(upper, db, precision=lax.Precision.HIGHEST, preferred_element_type=F32)
            w = dlg / f - dkk
            dq_ref[rows, :] = dq.astype(dq_ref.dtype)
            dv_ref[rows, :] = dv.astype(dv_ref.dtype)
            dfz_ref[rows, :] = (w * (1.0 - lbv) * sg * (1.0 - sg)).astype(dfz_ref.dtype)
            dlb_ref[...] += jnp.sum(w * (1.0 - sg), axis=0, keepdims=True)
            return carry

        lax.fori_loop(0, nsub, sub, 0)

    blk = lambda c: pl.BlockSpec((tb, HEAD), lambda h, i, c=c: (nblk - 1 - i, c * N_HEADS + h))
    vec = pl.BlockSpec((1, HEAD), lambda h, i: (0, h))
    rowb = pl.BlockSpec((tb, HEAD), lambda h, i: (nblk - 1 - i, h))
    stb = pl.BlockSpec((None, nsub, HEAD, HEAD), lambda h, i: (h, nblk - 1 - i, 0, 0))
    grad = jax.ShapeDtypeStruct((S, D_TOK), BF16)
    small = jax.ShapeDtypeStruct((1, D_TOK), F32)
    return pl.pallas_call(
        body, name=name, grid=(N_HEADS, nblk), in_specs=[blk(0), blk(1), blk(2), blk(3), vec, vec, rowb, stb, rowb],
        out_specs=[rowb, rowb, rowb, rowb, vec, vec], out_shape=[grad, grad, grad, grad, small, small],
        scratch_shapes=[pltpu.VMEM((HEAD, HEAD), F32)] + [pltpu.VMEM((C, HEAD), F32)] * 5,
        compiler_params=_params(("parallel", "arbitrary")))(p, p, p, p, lb, onorm, o, states, dheads)


def _adamw(w, g, m, v, *, name):
    shape = w.shape
    cols = shape[-1]
    w2, g2, m2, v2 = (t.reshape(-1, cols) for t in (w, g, m, v))
    R = w2.shape[0]
    tr = _tile(R, 512, 8)

    def body(w_ref, g_ref, m_ref, v_ref, d_ref, nm_ref, nv_ref):
        gv = g_ref[...]
        nm = ADAM_B1 * m_ref[...] + (1.0 - ADAM_B1) * gv
        nv = ADAM_B2 * v_ref[...] + (1.0 - ADAM_B2) * (gv * gv)
        m_hat = nm / (1.0 - ADAM_B1 ** ADAM_STEP)
        v_hat = nv / (1.0 - ADAM_B2 ** ADAM_STEP)
        d_ref[...] = -ADAM_LR * (m_hat / (jnp.sqrt(v_hat) + ADAM_EPS) + ADAM_WD * w_ref[...])
        nm_ref[...] = nm
        nv_ref[...] = nv

    spec = pl.BlockSpec((tr, cols), lambda i: (i, 0))
    out = jax.ShapeDtypeStruct((R, cols), F32)
    d, nm, nv = pl.pallas_call(body, name=name, grid=(R // tr,), in_specs=[spec] * 4, out_specs=[spec] * 3,
                               out_shape=[out] * 3, compiler_params=_params(("parallel",)))(w2, g2, m2, v2)
    return d.reshape(shape), nm.reshape(shape), nv.reshape(shape)


def _add_pairs(a, b, *, name, out_dtype):
    R = a.shape[0]
    tr = _tile(R, 2048, 16)

    def body(a_ref, b_ref, o_ref):
        o_ref[...] = (a_ref[...].astype(F32) + b_ref[...].astype(F32)).astype(out_dtype)

    spec = pl.BlockSpec((tr, LANE), lambda i: (i, 0))
    return pl.pallas_call(body, name=name, grid=(R // tr,), in_specs=[spec, spec], out_specs=spec,
                          out_shape=jax.ShapeDtypeStruct((R, LANE), out_dtype),
                          compiler_params=_params(("parallel",)))(a, b)


def _add_four(a, b3, *, name):
    R = a.shape[0]
    tr = _tile(R, 2048, 16)

    def body(a_ref, b_ref, o_ref):
        acc = a_ref[...].astype(F32)
        for k in range(3):
            acc = acc + b_ref[k].astype(F32)
        o_ref[...] = acc

    return pl.pallas_call(
        body, name=name, grid=(R // tr,),
        in_specs=[pl.BlockSpec((tr, LANE), lambda i: (i, 0)), pl.BlockSpec((3, tr, LANE), lambda i: (0, i, 0))],
        out_specs=pl.BlockSpec((tr, LANE), lambda i: (i, 0)), out_shape=jax.ShapeDtypeStruct((R, LANE), F32),
        compiler_params=_params(("parallel",)))(a, b3)


def _place():
    return lax.axis_index("x"), lax.axis_index("y"), lax.axis_index("c")


def _all_gather(x, *, name, in_vmem, reduce_sum=False):
    R, Cc = x.shape
    space = pltpu.VMEM if in_vmem else pl.ANY

    def body(x_ref, out_ref, *scratch):
        if reduce_sum:
            gat_ref, send_sems, recv_sems, local_sem = scratch
        else:
            gat_ref = out_ref
            send_sems, recv_sems, local_sem = scratch
        mx, my, mc = _place()
        me, sibling = (mx, my, mc), (mx, my, 1 - mc)
        chips = [(1 - mx, my), (mx, 1 - my), (1 - mx, 1 - my)]

        def rows(px, py, pc):
            return gat_ref.at[pl.ds((4 * px + 2 * py + pc) * R, R), :]

        def copy(k, block, to, src=None):
            return pltpu.make_async_remote_copy(
                src_ref=rows(*block) if src is None else src, dst_ref=rows(*block), send_sem=send_sems.at[k],
                recv_sem=recv_sems.at[k], device_id=to, device_id_type=MESH_ID)

        mine = pltpu.make_async_copy(x_ref, rows(*me), local_sem)
        mine.start()
        first = [copy(0, me, sibling, src=x_ref)]
        first += [copy(1 + j, me, (*chip, mc), src=x_ref) for j, chip in enumerate(chips)]
        for cp in first:
            cp.start()
        passed = [copy(4 + j, (*chip, mc), sibling) for j, chip in enumerate(chips)]
        for j, chip in enumerate(chips):
            copy(1 + j, (*chip, mc), me).wait_recv()
            passed[j].start()
        copy(0, sibling, me).wait_recv()
        for j, chip in enumerate(chips):
            copy(4 + j, (*chip, 1 - mc), me).wait_recv()
        for cp in first + passed:
            cp.wait_send()
        mine.wait()
        if reduce_sum:
            acc = gat_ref[pl.ds(0, R), :]
            for d in range(1, N_DEV):
                acc = acc + gat_ref[pl.ds(d * R, R), :]
            out_ref[...] = acc

    sems = [pltpu.SemaphoreType.DMA((7,)), pltpu.SemaphoreType.DMA((7,)), pltpu.SemaphoreType.DMA]
    if reduce_sum:
        assert in_vmem
        out_shape = jax.ShapeDtypeStruct((R, Cc), x.dtype)
        scratch = [pltpu.VMEM((N_DEV * R, Cc), x.dtype)] + sems
    else:
        out_shape = jax.ShapeDtypeStruct((N_DEV * R, Cc), x.dtype)
        scratch = sems
    return pl.pallas_call(
        body, name=name, out_shape=out_shape, in_specs=[pl.BlockSpec(memory_space=space)],
        out_specs=pl.BlockSpec(memory_space=space), scratch_shapes=scratch,
        compiler_params=pltpu.CompilerParams(vmem_limit_bytes=VMEM_LIMIT))(x)


def _swap_with_sibling(g, *, name):
    _, Q, R, Cc = g.shape

    def body(g_ref, out_ref, send_sem, recv_sem):
        mx, my, mc = _place()
        cp = pltpu.make_async_remote_copy(src_ref=g_ref.at[1 - mc], dst_ref=out_ref, send_sem=send_sem,
                                          recv_sem=recv_sem, device_id=(mx, my, 1 - mc), device_id_type=MESH_ID)
        cp.start()
        cp.wait()

    return pl.pallas_call(
        body, name=name, out_shape=jax.ShapeDtypeStruct((Q, R, Cc), g.dtype),
        in_specs=[pl.BlockSpec(memory_space=pl.ANY)], out_specs=pl.BlockSpec(memory_space=pl.ANY),
        scratch_shapes=[pltpu.SemaphoreType.DMA, pltpu.SemaphoreType.DMA],
        compiler_params=pltpu.CompilerParams(vmem_limit_bytes=VMEM_LIMIT))(g)


def _swap_with_chips(t, *, name):
    _, R, Cc = t.shape

    def body(t_ref, out_ref, send_sems, recv_sems):
        mx, my, mc = _place()
        chips = [(1 - mx, my), (mx, 1 - my), (1 - mx, 1 - my)]
        copies = [pltpu.make_async_remote_copy(
            src_ref=t_ref.at[2 * px + py], dst_ref=out_ref.at[k], send_sem=send_sems.at[k], recv_sem=recv_sems.at[k],
            device_id=(px, py, mc), device_id_type=MESH_ID) for k, (px, py) in enumerate(chips)]
        for cp in copies:
            cp.start()
        for cp in copies:
            cp.wait()

    return pl.pallas_call(
        body, name=name, out_shape=jax.ShapeDtypeStruct((3, R, Cc), t.dtype),
        in_specs=[pl.BlockSpec(memory_space=pl.ANY)], out_specs=pl.BlockSpec(memory_space=pl.ANY),
        scratch_shapes=[pltpu.SemaphoreType.DMA((3,)), pltpu.SemaphoreType.DMA((3,))],
        compiler_params=pltpu.CompilerParams(vmem_limit_bytes=VMEM_LIMIT))(t)


_BIG = (("w_mem_kv", 1), ("w_out", 1), ("hg_w_in", 2), ("gm_w_in", 2), ("w_ffn_in", 2), ("w_ffn_out", 1))


def _pack_shards(shards, dtype):
    return jnp.concatenate([s.astype(dtype).reshape(-1, LANE) for s in shards], axis=0)


def _unpack_gathered(flat, shards):
    out, r0 = [], 0
    for (_, axis), s in zip(_BIG, shards):
        rows = s.size // LANE
        blk = flat[:, r0:r0 + rows].reshape((N_DEV,) + s.shape)
        blk = jnp.moveaxis(blk, 0, axis)
        shape = list(s.shape)
        shape[axis] *= N_DEV
        out.append(blk.reshape(shape))
        r0 += rows
    return out


def _pack_grads(grads, shards, dtype):
    parts = []
    for (_, axis), s, g in zip(_BIG, shards, grads):
        shape = list(s.shape)
        split = shape[:axis] + [N_DEV] + shape[axis:]
        blk = jnp.moveaxis(g.astype(dtype).reshape(split), axis, 0)
        parts.append(blk.reshape(N_DEV, -1, LANE))
    return jnp.concatenate(parts, axis=1)


def _unpack_local(flat, shards):
    out, r0 = [], 0
    for s in shards:
        rows = s.size // LANE
        out.append(flat[r0:r0 + rows].reshape(s.shape))
        r0 += rows
    return out


def _pad_rows(a, mult):
    r = (-a.shape[0]) % mult
    return a if r == 0 else jnp.concatenate([a, jnp.zeros((r,) + a.shape[1:], a.dtype)], axis=0)


def kernel(x, mem, mix_norm, mem_norm, w_mem_kv, w_out, hg_w_in, hg_lb, hg_onorm, gm_w_in, gm_ln_g, gm_ln_b, gm_ws, gm_bs, ffn_norm, w_ffn_in, w_ffn_out, final_norm, loss_target, m_mix_norm, m_mem_norm, m_w_mem_kv, m_w_out, m_hg_w_in, m_hg_lb, m_hg_onorm, m_gm_w_in, m_gm_ln_g, m_gm_ln_b, m_gm_ws, m_gm_bs, m_ffn_norm, m_w_ffn_in, m_w_ffn_out, m_final_norm, v_mix_norm, v_mem_norm, v_w_mem_kv, v_w_out, v_hg_w_in, v_hg_lb, v_hg_onorm, v_gm_w_in, v_gm_ln_g, v_gm_ln_b, v_gm_ws, v_gm_bs, v_ffn_norm, v_w_ffn_in, v_w_ffn_out, v_final_norm):
    mx, my, mc = _place()
    me = 4 * mx + 2 * my + mc
    S = x.shape[1]
    xs = x[0]
    mems = mem[0]
    tgt = loss_target[0]

    shards = [w_mem_kv, w_out, hg_w_in, gm_w_in, w_ffn_in, w_ffn_out]
    packed = _pack_shards(shards, BF16)
    rows_big = packed.shape[0]
    gathered = _all_gather(packed, name="gather_weights", in_vmem=False).reshape(N_DEV, rows_big, LANE)
    W_kv, W_out, W_hg, W_gm, W_fi, W_fo = _unpack_gathered(gathered, shards)
    W_hg, W_gm = W_hg[0], W_gm[0]

    ln_local = _pad_rows(jnp.concatenate([gm_ln_g, gm_ln_b], axis=0), 8)
    ln_local = jnp.concatenate([ln_local, jnp.zeros((8, LANE - ln_local.shape[1]), F32)], axis=1)
    ln_all = _all_gather(ln_local, name="gather_ln", in_vmem=True).reshape(N_DEV, 8, LANE)
    ln_g = ln_all[:, 0, :D_TOK // N_DEV].reshape(1, D_TOK)
    ln_b = ln_all[:, 1, :D_TOK // N_DEV].reshape(1, D_TOK)

    lb_soft = jax.nn.softmax(hg_lb, axis=0)
    lb0 = lb_soft[0:1]
    bsb = jnp.broadcast_to(gm_bs[0][:, :, None], (N_HEADS, GM_CHUNK, GM_CHUNK))
    ws = gm_ws[0]

    mem_n, kv = [], []
    for i in range(2):
        mn = _rms_fwd(mems, mem_norm[i:i + 1], name=f"mem_norm{i}")
        mem_n.append(mn)
        kv.append(_matmul(mn, W_kv[i], name=f"mem_kv{i}"))

    def ffn_fwd(xin, i):
        hf = _rms_fwd(xin, ffn_norm[i:i + 1], name=f"ffn_norm{i}")
        gu = _matmul(hf, W_fi[i], name=f"ffn_in{i}")
        act = _swiglu_fwd(gu, name=f"swiglu{i}")
        xout = _matmul(act, W_fo[i], res=xin, name=f"ffn_out{i}")
        return hf, gu, act, xout

    h0 = _rms_fwd(xs, mix_norm[0:1], name="mix_norm0")
    p0 = _matmul(h0, W_hg, name="hg_in")
    tok0, o0, states = _hgrn2_fwd(p0, lb0, hg_onorm, name="hgrn2_fwd")
    mo0 = _attn_fwd(p0, 4 * D_TOK // D_MEM, kv[0], name="attn_fwd0")
    heads0 = jnp.concatenate([tok0, mo0], axis=1)
    x1 = _matmul(heads0, W_out[0], res=xs, name="out_proj0")
    hf0, gu0, act0, x2 = ffn_fwd(x1, 0)

    h1 = _rms_fwd(x2, mix_norm[1:2], name="mix_norm1")
    p1 = _matmul(h1, W_gm, name="gm_in")
    tok1 = _gmlp_fwd(p1, ln_g, ln_b, ws, bsb, name="gmlp_fwd")
    mo1 = _attn_fwd(p1, 2 * D_TOK // D_MEM, kv[1], name="attn_fwd1")
    heads1 = jnp.concatenate([tok1, mo1], axis=1)
    x3 = _matmul(heads1, W_out[1], res=x2, name="out_proj1")
    hf1, gu1, act1, x4 = ffn_fwd(x3, 1)

    dx, g_final, loss_part = _final_loss(x4, final_norm.reshape(1, D_MODEL), tgt, name="final_loss")

    def ffn_bwd(dx, xin, hf, gu, act, i):
        dxb = dx.astype(BF16)
        dact = _matmul(dxb, W_fo[i], tb=True, name=f"ffn_out_dx{i}")
        g_wfo = _matmul(act, dxb, ta=True, name=f"ffn_out_dw{i}")
        dgu = _swiglu_bwd(gu, dact, name=f"swiglu_bwd{i}")
        g_wfi = _matmul(hf, dgu, ta=True, name=f"ffn_in_dw{i}")
        dhf = _matmul(dgu, W_fi[i], tb=True, name=f"ffn_in_dx{i}")
        dx, g_norm = _rms_bwd(xin, ffn_norm[i:i + 1], dhf, dx, name=f"ffn_norm_bwd{i}")
        return dx, g_wfi, g_wfo, g_norm

    def mem_bwd(dkv, i):
        g_wkv = _matmul(mem_n[i], dkv, ta=True, name=f"mem_kv_dw{i}")
        dmn = _matmul(dkv, W_kv[i], tb=True, name=f"mem_kv_dx{i}")
        _, g_norm = _rms_bwd(mems, mem_norm[i:i + 1], dmn, jnp.zeros_like(mems), name=f"mem_norm_bwd{i}")
        return g_wkv, g_norm

    dx, g_wfi1, g_wfo1, g_ffn1 = ffn_bwd(dx, x3, hf1, gu1, act1, 1)
    dxb = dx.astype(BF16)
    dheads = _matmul(dxb, W_out[1], tb=True, name="out_proj_dx1")
    g_wout1 = _matmul(heads1, dxb, ta=True, name="out_proj_dw1")
    dpu, dpv, g_ws, g_bs, g_lng, g_lnb = _gmlp_bwd(p1, ln_g, ln_b, ws, bsb, dheads, name="gmlp_bwd")
    dqm, dk, dv = _attn_bwd(p1, 2 * D_TOK // D_MEM, kv[1], dheads, name="attn_bwd1")
    g_wkv1, g_mem1 = mem_bwd(jnp.concatenate([dk, dv], axis=1), 1)
    dp = jnp.concatenate([dpu, dpv, dqm], axis=1)
    g_wgm = _matmul(h1, dp, ta=True, name="gm_in_dw")
    dh = _matmul(dp, W_gm, tb=True, name="gm_in_dx")
    dx, g_mix1 = _rms_bwd(x2, mix_norm[1:2], dh, dx, name="mix_norm_bwd1")

    dx, g_wfi0, g_wfo0, g_ffn0 = ffn_bwd(dx, x1, hf0, gu0, act0, 0)
    dxb = dx.astype(BF16)
    dheads = _matmul(dxb, W_out[0], tb=True, name="out_proj_dx0")
    g_wout0 = _matmul(heads0, dxb, ta=True, name="out_proj_dw0")
    dq, dfz, dvv, dgg, g_lb0, g_onorm = _hgrn2_bwd(p0, lb0, hg_onorm, o0, states, dheads, name="hgrn2_bwd")
    dqm, dk, dv = _attn_bwd(p0, 4 * D_TOK // D_MEM, kv[0], dheads, name="attn_bwd0")
    g_wkv0, g_mem0 = mem_bwd(jnp.concatenate([dk, dv], axis=1), 0)
    dp = jnp.concatenate([dq, dfz, dvv, dgg, dqm], axis=1)
    g_whg = _matmul(h0, dp, ta=True, name="hg_in_dw")
    dh = _matmul(dp, W_hg, tb=True, name="hg_in_dx")
    grad_x, g_mix0 = _rms_bwd(xs, mix_norm[0:1], dh, dx, name="mix_norm_bwd0")

    full_grads = [jnp.stack([g_wkv0, g_wkv1]), jnp.stack([g_wout0, g_wout1]), g_whg[None], g_wgm[None],
                  jnp.stack([g_wfi0, g_wfi1]), jnp.stack([g_wfo0, g_wfo1])]
    gp = _pack_grads(full_grads, shards, BF16)
    gp = jnp.moveaxis(gp.reshape(4, 2, rows_big, LANE), 1, 0)
    from_sibling = _swap_with_sibling(gp, name="reduce_sibling")
    mine_c = lax.dynamic_index_in_dim(gp, mc, 0, keepdims=False)
    chip_sum = _add_pairs(mine_c.reshape(-1, LANE), from_sibling.reshape(-1, LANE), name="reduce_add_sibling",
                          out_dtype=BF16).reshape(4, rows_big, LANE)
    from_chips = _swap_with_chips(chip_sum, name="reduce_chips")
    own = lax.dynamic_index_in_dim(chip_sum, 2 * mx + my, 0, keepdims=False)
    g_local = _add_four(own, from_chips, name="reduce_add_chips")
    g_shards = _unpack_local(g_local, shards)

    small = [loss_part, jnp.concatenate([g_mix0, g_mix1], axis=1), jnp.concatenate([g_mem0, g_mem1], axis=1),
             g_lb0, g_onorm, g_lng, g_lnb, g_ws.reshape(1, -1), g_bs.reshape(1, -1),
             jnp.concatenate([g_ffn0, g_ffn1], axis=1), g_final]
    sizes = [t.shape[1] for t in small]
    small_rows = _pad_rows(jnp.concatenate(small, axis=1).reshape(-1, LANE), 8)
    red = _all_gather(small_rows, name="reduce_small", in_vmem=True, reduce_sum=True).reshape(-1)
    pieces, off = [], 0
    for n in sizes:
        pieces.append(red[off:off + n])
        off += n
    loss = pieces[0][0]
    g_mix_norm = pieces[1].reshape(2, D_MODEL)
    g_mem_norm = pieces[2].reshape(2, D_MODEL)
    g_hg_lb = pieces[3][None, :] * lb0 * (jnp.eye(3, dtype=F32)[:, 0:1] - lb_soft)
    g_hg_onorm = pieces[4].reshape(1, D_TOK)
    width = D_TOK // N_DEV
    g_gm_ln_g = lax.dynamic_slice(pieces[5], (me * width,), (width,)).reshape(1, width)
    g_gm_ln_b = lax.dynamic_slice(pieces[6], (me * width,), (width,)).reshape(1, width)
    g_gm_ws = pieces[7].reshape(gm_ws.shape)
    g_gm_bs = pieces[8].reshape(gm_bs.shape)
    g_ffn_norm = pieces[9].reshape(2, D_MODEL)
    g_final_norm = pieces[10]

    grads = [g_mix_norm, g_mem_norm, g_shards[0], g_shards[1], g_shards[2], g_hg_lb, g_hg_onorm, g_shards[3],
             g_gm_ln_g, g_gm_ln_b, g_gm_ws, g_gm_bs, g_ffn_norm, g_shards[4], g_shards[5], g_final_norm]
    weights = [mix_norm, mem_norm, w_mem_kv, w_out, hg_w_in, hg_lb, hg_onorm, gm_w_in, gm_ln_g, gm_ln_b, gm_ws, gm_bs,
               ffn_norm, w_ffn_in, w_ffn_out, final_norm]
    ms = [m_mix_norm, m_mem_norm, m_w_mem_kv, m_w_out, m_hg_w_in, m_hg_lb, m_hg_onorm, m_gm_w_in, m_gm_ln_g,
          m_gm_ln_b, m_gm_ws, m_gm_bs, m_ffn_norm, m_w_ffn_in, m_w_ffn_out, m_final_norm]
    vs = [v_mix_norm, v_mem_norm, v_w_mem_kv, v_w_out, v_hg_w_in, v_hg_lb, v_hg_onorm, v_gm_w_in, v_gm_ln_g,
          v_gm_ln_b, v_gm_ws, v_gm_bs, v_ffn_norm, v_w_ffn_in, v_w_ffn_out, v_final_norm]
    deltas, new_m, new_v = [], [], []
    for n, (w, g, m, v) in enumerate(zip(weights, grads, ms, vs)):
        if w.ndim == 1:
            d, nm, nv = _adamw(w[None], g.reshape(1, -1), m[None], v[None], name=f"adamw{n}")
            d, nm, nv = d[0], nm[0], nv[0]
        else:
            d, nm, nv = _adamw(w, g.reshape(w.shape), m, v, name=f"adamw{n}")
        deltas.append(d)
        new_m.append(nm)
        new_v.append(nv)
    grads = [g.reshape(w.shape) for g, w in zip(grads, weights)]
    return (loss, grad_x[None], *grads, *deltas, *new_m, *new_v)
```

```python
import functools

import jax
import jax.numpy as jnp
from jax import lax
from jax.experimental import pallas as pl
from jax.experimental.pallas import tpu as pltpu

F32 = jnp.float32
BF16 = jnp.bfloat16
MXU_DTYPE = jnp.bfloat16
MESH_ID = pl.DeviceIdType.MESH

N_DEV = 8
EPS = 1e-6
D_MODEL = 1024
D_TOK = 768
D_MEM = 256
N_HEADS = 6
HEAD = 128
MEM_HEADS = 4
MEM_HDIM = 64
GM_CHUNK = 128
D_FF = 2816
HG_SUB = 16
HG_IN = 4 * D_TOK + D_MEM
GM_IN = 2 * D_TOK + D_MEM
LANE = 128

ADAM_LR = 0.001
ADAM_B1 = 0.9
ADAM_B2 = 0.999
ADAM_EPS = 1e-08
ADAM_WD = 0.01
ADAM_STEP = 10

VMEM_LIMIT = 48 * 2 ** 20


def _params(sem=None):
    return pltpu.CompilerParams(dimension_semantics=sem, vmem_limit_bytes=VMEM_LIMIT)


def _tile(n, cap, q=LANE):
    if n <= cap:
        return n
    best = None
    for t in range(q, cap + 1, q):
        if n % t == 0:
            best = t
    assert best is not None, (n, cap, q)
    return best


def _sigmoid(x):
    return 1.0 / (1.0 + jnp.exp(-x))


def _gelu(x):
    return 0.5 * x * (1.0 + lax.erf(x * 0.7071067811865476))


def _gelu_grad(x):
    return 0.5 * (1.0 + lax.erf(x * 0.7071067811865476)) + x * jnp.exp(-0.5 * x * x) * 0.3989422804014327


def _matmul(a, b, *, name, ta=False, tb=False, res=None, out_dtype=F32, a_halves=False, b_halves=False):
    if a_halves:
        assert not ta
        M, K = a.shape[1], 2 * a.shape[2]
    else:
        K, M = a.shape if ta else a.shape[::-1]
    if b_halves:
        assert not tb and b.shape[1] == K
        N = 2 * b.shape[2]
    else:
        N = b.shape[0] if tb else b.shape[1]
        assert (b.shape[1] if tb else b.shape[0]) == K
    tm = _tile(M, 512 if ta else 1024)
    tn = _tile(N // 2 if b_halves else N, 1792)
    tk = _tile(K // 2 if a_halves else K, 1664)
    nk = K // tk
    dims = (((0 if ta else 1,), (1 if tb else 0,)), ((), ()))

    def body(*refs):
        if res is None:
            a_ref, b_ref, o_ref, acc = refs
        else:
            a_ref, b_ref, r_ref, o_ref, acc = refs
        k = pl.program_id(2)

        @pl.when(k == 0)
        def _():
            acc[...] = jnp.zeros_like(acc)

        acc[...] += lax.dot_general(a_ref[...].astype(MXU_DTYPE), b_ref[...].astype(MXU_DTYPE), dims,
                                    preferred_element_type=F32)

        @pl.when(k == nk - 1)
        def _():
            r = acc[...]
            if res is not None:
                r = r + r_ref[...].astype(F32)
            o_ref[...] = r.astype(out_dtype)

    if a_halves:
        kh = nk // 2
        a_spec = pl.BlockSpec((None, tm, tk), lambda i, j, k: (k // kh, i, k % kh))
    elif ta:
        a_spec = pl.BlockSpec((tk, tm), lambda i, j, k: (k, i))
    else:
        a_spec = pl.BlockSpec((tm, tk), lambda i, j, k: (i, k))
    if b_halves:
        nh = N // 2 // tn
        b_spec = pl.BlockSpec((None, tk, tn), lambda i, j, k: (j // nh, k, j % nh))
    elif tb:
        b_spec = pl.BlockSpec((tn, tk), lambda i, j, k: (j, k))
    else:
        b_spec = pl.BlockSpec((tk, tn), lambda i, j, k: (k, j))
    o_spec = pl.BlockSpec((tm, tn), lambda i, j, k: (i, j))
    in_specs = [a_spec, b_spec] + ([o_spec] if res is not None else [])
    args = (a, b) + ((res,) if res is not None else ())
    return pl.pallas_call(
        body, name=name, grid=(M // tm, N // tn, nk), in_specs=in_specs, out_specs=o_spec,
        out_shape=jax.ShapeDtypeStruct((M, N), out_dtype), scratch_shapes=[pltpu.VMEM((tm, tn), F32)],
        compiler_params=_params(("parallel", "parallel", "arbitrary")))(*args)


def _ffn_in(hf, w, *, name):
    S, K = hf.shape
    tm = _tile(S, 512)
    tn = _tile(D_FF, 1408)
    nh = D_FF // tn

    def body(a_ref, bg_ref, bu_ref, gu_ref, act_ref):
        av = a_ref[...].astype(MXU_DTYPE)
        gate = jnp.dot(av, bg_ref[...].astype(MXU_DTYPE), preferred_element_type=F32)
        up = jnp.dot(av, bu_ref[...].astype(MXU_DTYPE), preferred_element_type=F32)
        gu_ref[0] = gate.astype(gu_ref.dtype)
        gu_ref[1] = up.astype(gu_ref.dtype)
        act_ref[...] = (gate * _sigmoid(gate) * up).astype(act_ref.dtype)

    return pl.pallas_call(
        body, name=name, grid=(S // tm, nh),
        in_specs=[pl.BlockSpec((tm, K), lambda i, j: (i, 0)), pl.BlockSpec((K, tn), lambda i, j: (0, j)),
                  pl.BlockSpec((K, tn), lambda i, j: (0, j + nh))],
        out_specs=[pl.BlockSpec((2, tm, tn), lambda i, j: (0, i, j)), pl.BlockSpec((tm, tn), lambda i, j: (i, j))],
        out_shape=[jax.ShapeDtypeStruct((2, S, D_FF), BF16), jax.ShapeDtypeStruct((S, D_FF), BF16)],
        compiler_params=_params(("parallel", "parallel")))(hf, w, w)


def _ffn_out_dx(dx, w, gu, *, name):
    S, K = dx.shape
    tm = _tile(S, 512)
    tn = _tile(D_FF, 1408)

    def body(a_ref, b_ref, gu_ref, o_ref):
        da = lax.dot_general(a_ref[...].astype(MXU_DTYPE), b_ref[...].astype(MXU_DTYPE), (((1,), (1,)), ((), ())),
                             preferred_element_type=F32)
        gate = gu_ref[0].astype(F32)
        up = gu_ref[1].astype(F32)
        sg = _sigmoid(gate)
        o_ref[0] = (da * up * sg * (1.0 + gate * (1.0 - sg))).astype(o_ref.dtype)
        o_ref[1] = (da * gate * sg).astype(o_ref.dtype)

    halves = pl.BlockSpec((2, tm, tn), lambda i, j: (0, i, j))
    return pl.pallas_call(
        body, name=name, grid=(S // tm, D_FF // tn),
        in_specs=[pl.BlockSpec((tm, K), lambda i, j: (i, 0)), pl.BlockSpec((tn, K), lambda i, j: (j, 0)), halves],
        out_specs=halves, out_shape=jax.ShapeDtypeStruct((2, S, D_FF), BF16),
        compiler_params=_params(("parallel", "parallel")))(dx, w, gu)


def _rms_fwd(x, g, *, name):
    R, Dm = x.shape
    tr = _tile(R, 512, 8)

    def body(x_ref, g_ref, o_ref):
        xv = x_ref[...]
        r = lax.rsqrt(jnp.mean(xv * xv, axis=-1, keepdims=True) + EPS)
        o_ref[...] = (xv * r * g_ref[...]).astype(o_ref.dtype)

    return pl.pallas_call(
        body, name=name, grid=(R // tr,),
        in_specs=[pl.BlockSpec((tr, Dm), lambda i: (i, 0)), pl.BlockSpec((1, Dm), lambda i: (0, 0))],
        out_specs=pl.BlockSpec((tr, Dm), lambda i: (i, 0)), out_shape=jax.ShapeDtypeStruct((R, Dm), BF16),
        compiler_params=_params(("parallel",)))(x, g)


def _rms_bwd(x, g, dh, dres, *, name):
    R, Dm = x.shape
    tr = _tile(R, 256, 8)

    def body(x_ref, g_ref, dh_ref, dres_ref, dx_ref, dg_ref):
        @pl.when(pl.program_id(0) == 0)
        def _():
            dg_ref[...] = jnp.zeros_like(dg_ref)

        xv = x_ref[...]
        r = lax.rsqrt(jnp.mean(xv * xv, axis=-1, keepdims=True) + EPS)
        xh = xv * r
        dhv = dh_ref[...].astype(F32)
        dg_ref[...] += jnp.sum(dhv * xh, axis=0, keepdims=True)
        u = dhv * g_ref[...]
        dx = r * (u - xh * jnp.mean(u * xh, axis=-1, keepdims=True))
        dx_ref[...] = dres_ref[...] + dx

    row = pl.BlockSpec((tr, Dm), lambda i: (i, 0))
    vec = pl.BlockSpec((1, Dm), lambda i: (0, 0))
    return pl.pallas_call(
        body, name=name, grid=(R // tr,), in_specs=[row, vec, row, row], out_specs=[row, vec],
        out_shape=[jax.ShapeDtypeStruct((R, Dm), F32), jax.ShapeDtypeStruct((1, Dm), F32)],
        compiler_params=_params(("arbitrary",)))(x, g, dh, dres)


def _final_loss(x, g, tgt, *, name):
    R, Dm = x.shape
    tr = _tile(R, 256, 8)

    def body(x_ref, g_ref, t_ref, dx_ref, dg_ref, loss_ref):
        @pl.when(pl.program_id(0) == 0)
        def _():
            dg_ref[...] = jnp.zeros_like(dg_ref)
            loss_ref[...] = jnp.zeros_like(loss_ref)

        xv = x_ref[...]
        r = lax.rsqrt(jnp.mean(xv * xv, axis=-1, keepdims=True) + EPS)
        xh = xv * r
        gv = g_ref[...]
        err = xh * gv - t_ref[...]
        part = 0.5 * jnp.sum(jnp.mean(err * err, axis=-1, keepdims=True), axis=0, keepdims=True)
        loss_ref[...] += jnp.broadcast_to(part, loss_ref.shape)
        dy = err * (1.0 / Dm)
        dg_ref[...] += jnp.sum(dy * xh, axis=0, keepdims=True)
        u = dy * gv
        dx_ref[...] = r * (u - xh * jnp.mean(u * xh, axis=-1, keepdims=True))

    row = pl.BlockSpec((tr, Dm), lambda i: (i, 0))
    vec = pl.BlockSpec((1, Dm), lambda i: (0, 0))
    one = pl.BlockSpec((1, LANE), lambda i: (0, 0))
    return pl.pallas_call(
        body, name=name, grid=(R // tr,), in_specs=[row, vec, row], out_specs=[row, vec, one],
        out_shape=[jax.ShapeDtypeStruct((R, Dm), F32), jax.ShapeDtypeStruct((1, Dm), F32),
                   jax.ShapeDtypeStruct((1, LANE), F32)],
        compiler_params=_params(("arbitrary",)))(x, g, tgt)


def _head_mask(h):
    lane = lax.broadcasted_iota(jnp.int32, (1, D_MEM), 1)
    return (lane >= h * MEM_HDIM) & (lane < (h + 1) * MEM_HDIM)


def _attn_probs(qv, k_mx, mask):
    s = lax.dot_general(jnp.where(mask, qv, 0.0).astype(MXU_DTYPE), k_mx, (((1,), (1,)), ((), ())),
                        preferred_element_type=F32) * (MEM_HDIM ** -0.5)
    e = jnp.exp(s - jnp.max(s, axis=-1, keepdims=True))
    return e / jnp.sum(e, axis=-1, keepdims=True)


def _attn_fwd(p, qcol, kv, heads, *, name):
    S = p.shape[0]
    M = kv.shape[0]
    ts = _tile(S, 512, 8)

    def body(q_ref, k_ref, v_ref, heads_in, o_ref):
        del heads_in
        qv = q_ref[...]
        kx = k_ref[...].astype(MXU_DTYPE)
        vv = v_ref[...]
        out = jnp.zeros((ts, D_MEM), F32)
        for h in range(MEM_HEADS):
            mask = _head_mask(h)
            pr = _attn_probs(qv, kx, mask)
            out = out + jnp.dot(pr.astype(MXU_DTYPE), jnp.where(mask, vv, 0.0).astype(MXU_DTYPE),
                                preferred_element_type=F32)
        o_ref[...] = out.astype(o_ref.dtype)

    return pl.pallas_call(
        body, name=name, grid=(S // ts,),
        in_specs=[pl.BlockSpec((ts, D_MEM), lambda i: (i, qcol)), pl.BlockSpec((M, D_MEM), lambda i: (0, 0)),
                  pl.BlockSpec((M, D_MEM), lambda i: (0, 1)), pl.BlockSpec(memory_space=pl.ANY)],
        out_specs=pl.BlockSpec((ts, D_MEM), lambda i: (i, D_TOK // D_MEM)),
        out_shape=jax.ShapeDtypeStruct(heads.shape, heads.dtype), input_output_aliases={3: 0},
        compiler_params=_params(("parallel",)))(p, kv, kv, heads)


def _attn_bwd(p, qcol, kv, dheads, dp, *, name):
    S = p.shape[0]
    M = kv.shape[0]
    ts = _tile(S, 512, 8)
    scale = MEM_HDIM ** -0.5

    def body(q_ref, k_ref, v_ref, do_ref, dp_in, dq_ref, dk_ref, dv_ref):
        del dp_in

        @pl.when(pl.program_id(0) == 0)
        def _():
            dk_ref[...] = jnp.zeros_like(dk_ref)
            dv_ref[...] = jnp.zeros_like(dv_ref)

        qv = q_ref[...]
        kv_ = k_ref[...]
        kx = kv_.astype(MXU_DTYPE)
        vv = v_ref[...]
        dox = do_ref[...].astype(MXU_DTYPE)
        qx = qv.astype(MXU_DTYPE)
        dq = jnp.zeros((ts, D_MEM), F32)
        for h in range(MEM_HEADS):
            mask = _head_mask(h)
            pr = _attn_probs(qv, kx, mask)
            vh = jnp.where(mask, vv, 0.0).astype(MXU_DTYPE)
            dpr = lax.dot_general(dox, vh, (((1,), (1,)), ((), ())), preferred_element_type=F32)
            ds = (pr * (dpr - jnp.sum(dpr * pr, axis=-1, keepdims=True)) * scale).astype(MXU_DTYPE)
            dq = dq + jnp.dot(ds, jnp.where(mask, kv_, 0.0).astype(MXU_DTYPE), preferred_element_type=F32)
            dk_h = lax.dot_general(ds, qx, (((0,), (0,)), ((), ())), preferred_element_type=F32)
            dv_h = lax.dot_general(pr.astype(MXU_DTYPE), dox, (((0,), (0,)), ((), ())), preferred_element_type=F32)
            dk_ref[...] += jnp.where(mask, dk_h, 0.0)
            dv_ref[...] += jnp.where(mask, dv_h, 0.0)
        dq_ref[...] = dq.astype(dq_ref.dtype)

    return pl.pallas_call(
        body, name=name, grid=(S // ts,),
        in_specs=[pl.BlockSpec((ts, D_MEM), lambda i: (i, qcol)), pl.BlockSpec((M, D_MEM), lambda i: (0, 0)),
                  pl.BlockSpec((M, D_MEM), lambda i: (0, 1)),
                  pl.BlockSpec((ts, D_MEM), lambda i: (i, D_TOK // D_MEM)), pl.BlockSpec(memory_space=pl.ANY)],
        out_specs=[pl.BlockSpec((ts, D_MEM), lambda i: (i, qcol)), pl.BlockSpec((M, D_MEM), lambda i: (0, 0)),
                   pl.BlockSpec((M, D_MEM), lambda i: (0, 0))],
        out_shape=[jax.ShapeDtypeStruct(dp.shape, dp.dtype), jax.ShapeDtypeStruct((M, D_MEM), F32),
                   jax.ShapeDtypeStruct((M, D_MEM), F32)],
        input_output_aliases={4: 0}, compiler_params=_params(("arbitrary",)))(p, kv, kv, dheads, dp)


def _gm_forward_parts(u_ref, v_ref, lng_ref, lnb_ref, w_ref, bsb_ref):
    zu = _gelu(u_ref[...])
    zv = _gelu(v_ref[...])
    mu = jnp.mean(zv, axis=-1, keepdims=True)
    cen = zv - mu
    rs = lax.rsqrt(jnp.mean(cen * cen, axis=-1, keepdims=True) + EPS)
    vh = cen * rs
    vn = vh * lng_ref[...] + lnb_ref[...]
    row = lax.broadcasted_iota(jnp.int32, (GM_CHUNK, GM_CHUNK), 0)
    col = lax.broadcasted_iota(jnp.int32, (GM_CHUNK, GM_CHUNK), 1)
    tril = row >= col
    wm = [jnp.where(tril, w_ref[g], 0.0).astype(MXU_DTYPE) for g in range(N_HEADS)]
    vnx = [vn[:, g * HEAD:(g + 1) * HEAD].astype(MXU_DTYPE) for g in range(N_HEADS)]
    sv = [jnp.dot(wm[g], vnx[g], preferred_element_type=F32) + bsb_ref[g] for g in range(N_HEADS)]
    return zu, vh, rs, wm, vnx, sv, tril


def _gmlp_fwd(p, lng, lnb, ws, bsb, *, name):
    S = p.shape[0]

    def body(u_ref, v_ref, lng_ref, lnb_ref, w_ref, bsb_ref, o_ref):
        zu, _, _, _, _, sv, _ = _gm_forward_parts(u_ref, v_ref, lng_ref, lnb_ref, w_ref, bsb_ref)
        for g in range(N_HEADS):
            o_ref[:, g * HEAD:(g + 1) * HEAD] = (zu[:, g * HEAD:(g + 1) * HEAD] * sv[g]).astype(o_ref.dtype)

    blk = lambda c: pl.BlockSpec((GM_CHUNK, D_TOK), lambda i: (i, c))
    vec = pl.BlockSpec((1, D_TOK), lambda i: (0, 0))
    cube = pl.BlockSpec((N_HEADS, GM_CHUNK, GM_CHUNK), lambda i: (0, 0, 0))
    return pl.pallas_call(
        body, name=name, grid=(S // GM_CHUNK,), in_specs=[blk(0), blk(1), vec, vec, cube, cube],
        out_specs=blk(0), out_shape=jax.ShapeDtypeStruct((S, D_MODEL), BF16),
        compiler_params=_params(("parallel",)))(p, p, lng, lnb, ws, bsb)


def _gmlp_bwd(p, lng, lnb, ws, bsb, dheads, *, name):
    S = p.shape[0]

    def body(u_ref, v_ref, lng_ref, lnb_ref, w_ref, bsb_ref, dt_ref, dp_ref, dw_ref, dbs_ref, dlg_ref, dlb_ref):
        @pl.when(pl.program_id(0) == 0)
        def _():
            dw_ref[...] = jnp.zeros_like(dw_ref)
            dbs_ref[...] = jnp.zeros_like(dbs_ref)
            dlg_ref[...] = jnp.zeros_like(dlg_ref)
            dlb_ref[...] = jnp.zeros_like(dlb_ref)

        zu, vh, rs, wm, vnx, sv, tril = _gm_forward_parts(u_ref, v_ref, lng_ref, lnb_ref, w_ref, bsb_ref)
        dt = dt_ref[...].astype(F32)
        dvn_parts = []
        for g in range(N_HEADS):
            sl = slice(g * HEAD, (g + 1) * HEAD)
            dsv = dt[:, sl] * zu[:, sl]
            dp_ref[:, sl] = (dt[:, sl] * sv[g] * _gelu_grad(u_ref[:, sl])).astype(dp_ref.dtype)
            dsx = dsv.astype(MXU_DTYPE)
            dw = lax.dot_general(dsx, vnx[g], (((1,), (1,)), ((), ())), preferred_element_type=F32)
            dw_ref[g] += jnp.where(tril, dw, 0.0)
            dbs_ref[g] += jnp.sum(dsv, axis=-1, keepdims=True)
            dvn_parts.append(lax.dot_general(wm[g], dsx, (((0,), (0,)), ((), ())), preferred_element_type=F32))
        dvn = jnp.concatenate(dvn_parts, axis=-1)
        dlg_ref[...] += jnp.sum(dvn * vh, axis=0, keepdims=True)
        dlb_ref[...] += jnp.sum(dvn, axis=0, keepdims=True)
        dvh = dvn * lng_ref[...]
        dzv = rs * (dvh - jnp.mean(dvh, axis=-1, keepdims=True) - vh * jnp.mean(dvh * vh, axis=-1, keepdims=True))
        dp_ref[:, D_TOK:] = (dzv * _gelu_grad(v_ref[...])).astype(dp_ref.dtype)

    blk = lambda c: pl.BlockSpec((GM_CHUNK, D_TOK), lambda i: (i, c))
    vec = pl.BlockSpec((1, D_TOK), lambda i: (0, 0))
    cube = pl.BlockSpec((N_HEADS, GM_CHUNK, GM_CHUNK), lambda i: (0, 0, 0))
    col = pl.BlockSpec((N_HEADS, GM_CHUNK, 1), lambda i: (0, 0, 0))
    return pl.pallas_call(
        body, name=name, grid=(S // GM_CHUNK,), in_specs=[blk(0), blk(1), vec, vec, cube, cube, blk(0)],
        out_specs=[pl.BlockSpec((GM_CHUNK, 2 * D_TOK), lambda i: (i, 0)), cube, col, vec, vec],
        out_shape=[jax.ShapeDtypeStruct((S, GM_IN), BF16), jax.ShapeDtypeStruct((N_HEADS, GM_CHUNK, GM_CHUNK), F32),
                   jax.ShapeDtypeStruct((N_HEADS, GM_CHUNK, 1), F32), jax.ShapeDtypeStruct((1, D_TOK), F32),
                   jax.ShapeDtypeStruct((1, D_TOK), F32)],
        compiler_params=_params(("arbitrary",)))(p, p, lng, lnb, ws, bsb, dheads)


def _tri(n, upper):
    r = lax.broadcasted_iota(jnp.int32, (n, n), 0)
    c = lax.broadcasted_iota(jnp.int32, (n, n), 1)
    return jnp.where((r <= c) if upper else (r >= c), 1.0, 0.0).astype(F32)


def _hg_gates(fz, lb):
    sg = _sigmoid(fz)
    f = lb + (1.0 - lb) * sg
    kk = (1.0 - lb) * (1.0 - sg)
    return sg, f, jnp.log(f), kk


def _hgrn2_fwd(p, lb, onorm, *, name):
    S = p.shape[0]
    C = HG_SUB
    tb = _tile(S, 256, C)
    nsub = tb // C

    def body(q_ref, fz_ref, v_ref, g_ref, lb_ref, on_ref, tok_ref, o_ref, st_ref, state, bsc, ksc, vsc):
        @pl.when(pl.program_id(0) == 0)
        def _():
            state[...] = jnp.zeros_like(state)

        lower = _tri(C, False)
        tt = lax.broadcasted_iota(jnp.int32, (C, HEAD), 0)

        def sub(c, carry):
            rows = pl.ds(pl.multiple_of(c * C, C), C)
            for h in range(N_HEADS):
                cols = slice(h * HEAD, (h + 1) * HEAD)
                qv = q_ref[rows, cols]
                vv = v_ref[rows, cols]
                gv = g_ref[rows, cols]
                _, _, lg, kk = _hg_gates(fz_ref[rows, cols], lb_ref[:, cols])
                b = jnp.dot(lower, lg, precision=lax.Precision.HIGHEST, preferred_element_type=F32)
                st0 = state[h]
                st_ref[c, h] = st0
                inter = lax.dot_general((qv * jnp.exp(b)).astype(MXU_DTYPE), st0.astype(MXU_DTYPE),
                                        (((1,), (1,)), ((), ())), preferred_element_type=F32)
                bsc[h] = b
                ksc[h] = kk
                vsc[h] = vv
                intra = jnp.zeros((C, HEAD), F32)
                for s in range(C):
                    dec = jnp.where(tt >= s, jnp.exp(jnp.minimum(b - bsc[h, pl.ds(s, 1), :], 0.0)), 0.0)
                    a_s = jnp.sum(qv * ksc[h, pl.ds(s, 1), :] * dec, axis=-1, keepdims=True)
                    intra = intra + a_s * vsc[h, pl.ds(s, 1), :]
                o = inter + intra
                b_last = bsc[h, pl.ds(C - 1, 1), :]
                ke = kk * jnp.exp(b_last - b)
                state[h] = st0 * jnp.exp(b_last) + lax.dot_general(
                    vv.astype(MXU_DTYPE), ke.astype(MXU_DTYPE), (((0,), (0,)), ((), ())),
                    preferred_element_type=F32)
                o_ref[rows, cols] = o
                n = o * lax.rsqrt(jnp.mean(o * o, axis=-1, keepdims=True) + EPS)
                tok_ref[rows, cols] = (n * (gv * _sigmoid(gv)) * on_ref[:, cols]).astype(tok_ref.dtype)
            return carry

        lax.fori_loop(0, nsub, sub, 0)

    blk = lambda c: pl.BlockSpec((tb, D_TOK), lambda i, c=c: (i, c))
    vec = pl.BlockSpec((1, D_TOK), lambda i: (0, 0))
    stb = pl.BlockSpec((nsub, N_HEADS, HEAD, HEAD), lambda i: (i, 0, 0, 0))
    return pl.pallas_call(
        body, name=name, grid=(S // tb,), in_specs=[blk(0), blk(1), blk(2), blk(3), vec, vec],
        out_specs=[blk(0), blk(0), stb],
        out_shape=[jax.ShapeDtypeStruct((S, D_MODEL), BF16), jax.ShapeDtypeStruct((S, D_TOK), F32),
                   jax.ShapeDtypeStruct((S // C, N_HEADS, HEAD, HEAD), F32)],
        scratch_shapes=[pltpu.VMEM((N_HEADS, HEAD, HEAD), F32)] + [pltpu.VMEM((N_HEADS, C, HEAD), F32)] * 3,
        compiler_params=_params(("arbitrary",)))(p, p, p, p, lb, onorm)


def _hgrn2_bwd(p, lb, onorm, o, states, dheads, *, name):
    S = p.shape[0]
    C = HG_SUB
    tb = _tile(S, 256, C)
    nsub = tb // C
    nblk = S // tb

    def body(q_ref, fz_ref, v_ref, g_ref, lb_ref, on_ref, o_ref, st_ref, dt_ref, dp_ref, dlb_ref, don_ref, dstate,
             bsc, ksc, vsc, qsc, dosc):
        @pl.when(pl.program_id(0) == 0)
        def _():
            dstate[...] = jnp.zeros_like(dstate)
            dlb_ref[...] = jnp.zeros_like(dlb_ref)
            don_ref[...] = jnp.zeros_like(don_ref)

        lower = _tri(C, False)
        upper = _tri(C, True)
        tt = lax.broadcasted_iota(jnp.int32, (C, HEAD), 0)

        def sub(j, carry):
            c = nsub - 1 - j
            rows = pl.ds(pl.multiple_of(c * C, C), C)
            for h in range(N_HEADS):
                cols = slice(h * HEAD, (h + 1) * HEAD)
                lbv = lb_ref[:, cols]
                onv = on_ref[:, cols]
                qv = q_ref[rows, cols]
                vv = v_ref[rows, cols]
                gv = g_ref[rows, cols]
                ov = o_ref[rows, cols]
                dt = dt_ref[rows, cols].astype(F32)
                sgg = _sigmoid(gv)
                sil = gv * sgg
                rinv = lax.rsqrt(jnp.mean(ov * ov, axis=-1, keepdims=True) + EPS)
                n = ov * rinv
                don_ref[:, cols] += jnp.sum(dt * n * sil, axis=0, keepdims=True)
                dn = dt * sil * onv
                dp_ref[rows, 3 * D_TOK + h * HEAD:3 * D_TOK + (h + 1) * HEAD] = (
                    dt * n * onv * sgg * (1.0 + gv * (1.0 - sgg))).astype(dp_ref.dtype)
                do = rinv * (dn - n * jnp.mean(dn * n, axis=-1, keepdims=True))
                sg, f, lg, kk = _hg_gates(fz_ref[rows, cols], lbv)
                b = jnp.dot(lower, lg, precision=lax.Precision.HIGHEST, preferred_element_type=F32)
                bsc[h] = b
                ksc[h] = kk
                vsc[h] = vv
                qsc[h] = qv
                dosc[h] = do
                b_last = bsc[h, pl.ds(C - 1, 1), :]
                eb = jnp.exp(b)
                qe = qv * eb
                ebb = jnp.exp(b_last - b)
                ke = kk * ebb
                e_last = jnp.exp(b_last)
                st0 = st_ref[c, h]
                dst1 = dstate[h]
                st0x = st0.astype(MXU_DTYPE)
                dst1x = dst1.astype(MXU_DTYPE)
                dox = do.astype(MXU_DTYPE)
                dqe = jnp.dot(dox, st0x, preferred_element_type=F32)
                dke = jnp.dot(vv.astype(MXU_DTYPE), dst1x, preferred_element_type=F32)
                dv = lax.dot_general(ke.astype(MXU_DTYPE), dst1x, (((1,), (1,)), ((), ())),
                                     preferred_element_type=F32)
                db_last = (e_last * jnp.sum(st0 * dst1, axis=0, keepdims=True)
                           + jnp.sum(dke * ke, axis=0, keepdims=True))
                dstate[h] = dst1 * e_last + lax.dot_general(dox, qe.astype(MXU_DTYPE), (((0,), (0,)), ((), ())),
                                                            preferred_element_type=F32)
                dq = dqe * eb
                db = dqe * qe - dke * ke
                dkk = dke * ebb
                for s in range(C):
                    dec = jnp.where(tt >= s, jnp.exp(jnp.minimum(b - bsc[h, pl.ds(s, 1), :], 0.0)), 0.0)
                    da_s = jnp.sum(do * vsc[h, pl.ds(s, 1), :], axis=-1, keepdims=True)
                    pq = da_s * ksc[h, pl.ds(s, 1), :] * dec
                    dq = dq + pq
                    db = db + pq * qv
                for t in range(C):
                    q_t = qsc[h, pl.ds(t, 1), :]
                    do_t = dosc[h, pl.ds(t, 1), :]
                    dec = jnp.where(tt <= t, jnp.exp(jnp.minimum(bsc[h, pl.ds(t, 1), :] - b, 0.0)), 0.0)
                    da_t = jnp.sum(vv * do_t, axis=-1, keepdims=True)
                    pk = da_t * q_t * dec
                    dkk = dkk + pk
                    db = db - pk * kk
                    a_t = jnp.sum(q_t * kk * dec, axis=-1, keepdims=True)
                    dv = dv + a_t * do_t
                db = db + jnp.where(tt == C - 1, db_last, 0.0)
                dlg = jnp.dot(upper, db, precision=lax.Precision.HIGHEST, preferred_element_type=F32)
                w = dlg / f - dkk
                dp_ref[rows, cols] = dq.astype(dp_ref.dtype)
                dp_ref[rows, D_TOK + h * HEAD:D_TOK + (h + 1) * HEAD] = (
                    w * (1.0 - lbv) * sg * (1.0 - sg)).astype(dp_ref.dtype)
                dp_ref[rows, 2 * D_TOK + h * HEAD:2 * D_TOK + (h + 1) * HEAD] = dv.astype(dp_ref.dtype)
                dlb_ref[:, cols] += jnp.sum(w * (1.0 - sg), axis=0, keepdims=True)
            return carry

        lax.fori_loop(0, nsub, sub, 0)

    blk = lambda c: pl.BlockSpec((tb, D_TOK), lambda i, c=c: (nblk - 1 - i, c))
    vec = pl.BlockSpec((1, D_TOK), lambda i: (0, 0))
    stb = pl.BlockSpec((nsub, N_HEADS, HEAD, HEAD), lambda i: (nblk - 1 - i, 0, 0, 0))
    small = jax.ShapeDtypeStruct((1, D_TOK), F32)
    return pl.pallas_call(
        body, name=name, grid=(nblk,), in_specs=[blk(0), blk(1), blk(2), blk(3), vec, vec, blk(0), stb, blk(0)],
        out_specs=[pl.BlockSpec((tb, 4 * D_TOK), lambda i: (nblk - 1 - i, 0)), vec, vec],
        out_shape=[jax.ShapeDtypeStruct((S, HG_IN), BF16), small, small],
        scratch_shapes=[pltpu.VMEM((N_HEADS, HEAD, HEAD), F32)] + [pltpu.VMEM((N_HEADS, C, HEAD), F32)] * 5,
        compiler_params=_params(("arbitrary",)))(p, p, p, p, lb, onorm, o, states, dheads)


def _adamw(w, g, m, v, *, name):
    shape = w.shape
    cols = shape[-1]
    w2, g2, m2, v2 = (t.reshape(-1, cols) for t in (w, g, m, v))
    R = w2.shape[0]
    tr = _tile(R, 512, 8)

    def body(w_ref, g_ref, m_ref, v_ref, d_ref, nm_ref, nv_ref):
        gv = g_ref[...]
        nm = ADAM_B1 * m_ref[...] + (1.0 - ADAM_B1) * gv
        nv = ADAM_B2 * v_ref[...] + (1.0 - ADAM_B2) * (gv * gv)
        m_hat = nm / (1.0 - ADAM_B1 ** ADAM_STEP)
        v_hat = nv / (1.0 - ADAM_B2 ** ADAM_STEP)
        d_ref[...] = -ADAM_LR * (m_hat / (jnp.sqrt(v_hat) + ADAM_EPS) + ADAM_WD * w_ref[...])
        nm_ref[...] = nm
        nv_ref[...] = nv

    spec = pl.BlockSpec((tr, cols), lambda i: (i, 0))
    out = jax.ShapeDtypeStruct((R, cols), F32)
    d, nm, nv = pl.pallas_call(body, name=name, grid=(R // tr,), in_specs=[spec] * 4, out_specs=[spec] * 3,
                               out_shape=[out] * 3, compiler_params=_params(("parallel",)))(w2, g2, m2, v2)
    return d.reshape(shape), nm.reshape(shape), nv.reshape(shape)


def _add_pairs(a, b, *, name, out_dtype):
    R = a.shape[0]
    tr = _tile(R, 2048, 16)

    def body(a_ref, b_ref, o_ref):
        o_ref[...] = (a_ref[...].astype(F32) + b_ref[...].astype(F32)).astype(out_dtype)

    spec = pl.BlockSpec((tr, LANE), lambda i: (i, 0))
    return pl.pallas_call(body, name=name, grid=(R // tr,), in_specs=[spec, spec], out_specs=spec,
                          out_shape=jax.ShapeDtypeStruct((R, LANE), out_dtype),
                          compiler_params=_params(("parallel",)))(a, b)


def _add_four(a, b3, *, name):
    R = a.shape[0]
    tr = _tile(R, 2048, 16)

    def body(a_ref, b_ref, o_ref):
        acc = a_ref[...].astype(F32)
        for k in range(3):
            acc = acc + b_ref[k].astype(F32)
        o_ref[...] = acc

    return pl.pallas_call(
        body, name=name, grid=(R // tr,),
        in_specs=[pl.BlockSpec((tr, LANE), lambda i: (i, 0)), pl.BlockSpec((3, tr, LANE), lambda i: (0, i, 0))],
        out_specs=pl.BlockSpec((tr, LANE), lambda i: (i, 0)), out_shape=jax.ShapeDtypeStruct((R, LANE), F32),
        compiler_params=_params(("parallel",)))(a, b3)


def _place():
    return lax.axis_index("x"), lax.axis_index("y"), lax.axis_index("c")


def _all_gather(x, *, name, in_vmem, reduce_sum=False):
    R, Cc = x.shape
    space = pltpu.VMEM if in_vmem else pl.ANY

    def body(x_ref, out_ref, *scratch):
        if reduce_sum:
            gat_ref, send_sems, recv_sems, local_sem = scratch
        else:
            gat_ref = out_ref
            send_sems, recv_sems, local_sem = scratch
        mx, my, mc = _place()
        me, sibling = (mx, my, mc), (mx, my, 1 - mc)
        chips = [(1 - mx, my), (mx, 1 - my), (1 - mx, 1 - my)]

        def rows(px, py, pc):
            return gat_ref.at[pl.ds((4 * px + 2 * py + pc) * R, R), :]

        def copy(k, block, to, src=None):
            return pltpu.make_async_remote_copy(
                src_ref=rows(*block) if src is None else src, dst_ref=rows(*block), send_sem=send_sems.at[k],
                recv_sem=recv_sems.at[k], device_id=to, device_id_type=MESH_ID)

        mine = pltpu.make_async_copy(x_ref, rows(*me), local_sem)
        mine.start()
        first = [copy(0, me, sibling, src=x_ref)]
        first += [copy(1 + j, me, (*chip, mc), src=x_ref) for j, chip in enumerate(chips)]
        for cp in first:
            cp.start()
        passed = [copy(4 + j, (*chip, mc), sibling) for j, chip in enumerate(chips)]
        for j, chip in enumerate(chips):
            copy(1 + j, (*chip, mc), me).wait_recv()
            passed[j].start()
        copy(0, sibling, me).wait_recv()
        for j, chip in enumerate(chips):
            copy(4 + j, (*chip, 1 - mc), me).wait_recv()
        for cp in first + passed:
            cp.wait_send()
        mine.wait()
        if reduce_sum:
            acc = gat_ref[pl.ds(0, R), :]
            for d in range(1, N_DEV):
                acc = acc + gat_ref[pl.ds(d * R, R), :]
            out_ref[...] = acc

    sems = [pltpu.SemaphoreType.DMA((7,)), pltpu.SemaphoreType.DMA((7,)), pltpu.SemaphoreType.DMA]
    if reduce_sum:
        assert in_vmem
        out_shape = jax.ShapeDtypeStruct((R, Cc), x.dtype)
        scratch = [pltpu.VMEM((N_DEV * R, Cc), x.dtype)] + sems
    else:
        out_shape = jax.ShapeDtypeStruct((N_DEV * R, Cc), x.dtype)
        scratch = sems
    return pl.pallas_call(
        body, name=name, out_shape=out_shape, in_specs=[pl.BlockSpec(memory_space=space)],
        out_specs=pl.BlockSpec(memory_space=space), scratch_shapes=scratch,
        compiler_params=pltpu.CompilerParams(vmem_limit_bytes=VMEM_LIMIT))(x)


def _swap_with_sibling(g, *, name):
    _, Q, R, Cc = g.shape

    def body(g_ref, out_ref, send_sem, recv_sem):
        mx, my, mc = _place()
        cp = pltpu.make_async_remote_copy(src_ref=g_ref.at[1 - mc], dst_ref=out_ref, send_sem=send_sem,
                                          recv_sem=recv_sem, device_id=(mx, my, 1 - mc), device_id_type=MESH_ID)
        cp.start()
        cp.wait()

    return pl.pallas_call(
        body, name=name, out_shape=jax.ShapeDtypeStruct((Q, R, Cc), g.dtype),
        in_specs=[pl.BlockSpec(memory_space=pl.ANY)], out_specs=pl.BlockSpec(memory_space=pl.ANY),
        scratch_shapes=[pltpu.SemaphoreType.DMA, pltpu.SemaphoreType.DMA],
        compiler_params=pltpu.CompilerParams(vmem_limit_bytes=VMEM_LIMIT))(g)


def _swap_with_chips(t, *, name):
    _, R, Cc = t.shape

    def body(t_ref, out_ref, send_sems, recv_sems):
        mx, my, mc = _place()
        chips = [(1 - mx, my), (mx, 1 - my), (1 - mx, 1 - my)]
        copies = [pltpu.make_async_remote_copy(
            src_ref=t_ref.at[2 * px + py], dst_ref=out_ref.at[k], send_sem=send_sems.at[k], recv_sem=recv_sems.at[k],
            device_id=(px, py, mc), device_id_type=MESH_ID) for k, (px, py) in enumerate(chips)]
        for cp in copies:
            cp.start()
        for cp in copies:
            cp.wait()

    return pl.pallas_call(
        body, name=name, out_shape=jax.ShapeDtypeStruct((3, R, Cc), t.dtype),
        in_specs=[pl.BlockSpec(memory_space=pl.ANY)], out_specs=pl.BlockSpec(memory_space=pl.ANY),
        scratch_shapes=[pltpu.SemaphoreType.DMA((3,)), pltpu.SemaphoreType.DMA((3,))],
        compiler_params=pltpu.CompilerParams(vmem_limit_bytes=VMEM_LIMIT))(t)


_BIG = (("w_mem_kv", 1), ("w_out", 1), ("hg_w_in", 2), ("gm_w_in", 2), ("w_ffn_in", 2), ("w_ffn_out", 1))


def _pack_shards(shards, dtype):
    return jnp.concatenate([s.astype(dtype).reshape(-1, LANE) for s in shards], axis=0)


def _unpack_gathered(flat, shards):
    out, r0 = [], 0
    for (_, axis), s in zip(_BIG, shards):
        rows = s.size // LANE
        blk = flat[:, r0:r0 + rows].reshape((N_DEV,) + s.shape)
        blk = jnp.moveaxis(blk, 0, axis)
        shape = list(s.shape)
        shape[axis] *= N_DEV
        out.append(blk.reshape(shape))
        r0 += rows
    return out


def _pack_grads(grads, shards, dtype):
    parts = []
    for (_, axis), s, g in zip(_BIG, shards, grads):
        shape = list(s.shape)
        split = shape[:axis] + [N_DEV] + shape[axis:]
        blk = jnp.moveaxis(g.astype(dtype).reshape(split), axis, 0)
        parts.append(blk.reshape(N_DEV, -1, LANE))
    return jnp.concatenate(parts, axis=1)


def _unpack_local(flat, shards):
    out, r0 = [], 0
    for s in shards:
        rows = s.size // LANE
        out.append(flat[r0:r0 + rows].reshape(s.shape))
        r0 += rows
    return out


def _pad_rows(a, mult):
    r = (-a.shape[0]) % mult
    return a if r == 0 else jnp.concatenate([a, jnp.zeros((r,) + a.shape[1:], a.dtype)], axis=0)


def kernel(x, mem, mix_norm, mem_norm, w_mem_kv, w_out, hg_w_in, hg_lb, hg_onorm, gm_w_in, gm_ln_g, gm_ln_b, gm_ws, gm_bs, ffn_norm, w_ffn_in, w_ffn_out, final_norm, loss_target, m_mix_norm, m_mem_norm, m_w_mem_kv, m_w_out, m_hg_w_in, m_hg_lb, m_hg_onorm, m_gm_w_in, m_gm_ln_g, m_gm_ln_b, m_gm_ws, m_gm_bs, m_ffn_norm, m_w_ffn_in, m_w_ffn_out, m_final_norm, v_mix_norm, v_mem_norm, v_w_mem_kv, v_w_out, v_hg_w_in, v_hg_lb, v_hg_onorm, v_gm_w_in, v_gm_ln_g, v_gm_ln_b, v_gm_ws, v_gm_bs, v_ffn_norm, v_w_ffn_in, v_w_ffn_out, v_final_norm):
    mx, my, mc = _place()
    me = 4 * mx + 2 * my + mc
    xs = x[0]
    mems = mem[0]
    tgt = loss_target[0]

    shards = [w_mem_kv, w_out, hg_w_in, gm_w_in, w_ffn_in, w_ffn_out]
    packed = _pack_shards(shards, BF16)
    rows_big = packed.shape[0]
    gathered = _all_gather(packed, name="gather_weights", in_vmem=False).reshape(N_DEV, rows_big, LANE)
    W_kv, W_out, W_hg, W_gm, W_fi, W_fo = _unpack_gathered(gathered, shards)
    W_hg, W_gm = W_hg[0], W_gm[0]

    ln_local = _pad_rows(jnp.concatenate([gm_ln_g, gm_ln_b], axis=0), 8)
    ln_local = jnp.concatenate([ln_local, jnp.zeros((8, LANE - ln_local.shape[1]), F32)], axis=1)
    ln_all = _all_gather(ln_local, name="gather_ln", in_vmem=True).reshape(N_DEV, 8, LANE)
    ln_g = ln_all[:, 0, :D_TOK // N_DEV].reshape(1, D_TOK)
    ln_b = ln_all[:, 1, :D_TOK // N_DEV].reshape(1, D_TOK)

    lb_soft = jax.nn.softmax(hg_lb, axis=0)
    lb0 = lb_soft[0:1]
    bsb = jnp.broadcast_to(gm_bs[0][:, :, None], (N_HEADS, GM_CHUNK, GM_CHUNK))
    ws = gm_ws[0]

    mem_n, kv = [], []
    for i in range(2):
        mn = _rms_fwd(mems, mem_norm[i:i + 1], name=f"mem_norm{i}")
        mem_n.append(mn)
        kv.append(_matmul(mn, W_kv[i], name=f"mem_kv{i}"))

    def ffn_fwd(xin, i):
        hf = _rms_fwd(xin, ffn_norm[i:i + 1], name=f"ffn_norm{i}")
        gu, act = _ffn_in(hf, W_fi[i], name=f"ffn_in{i}")
        xout = _matmul(act, W_fo[i], res=xin, name=f"ffn_out{i}")
        return hf, gu, act, xout

    h0 = _rms_fwd(xs, mix_norm[0:1], name="mix_norm0")
    p0 = _matmul(h0, W_hg, name="hg_in")
    heads0, o0, states = _hgrn2_fwd(p0, lb0, hg_onorm, name="hgrn2_fwd")
    heads0 = _attn_fwd(p0, 4 * D_TOK // D_MEM, kv[0], heads0, name="attn_fwd0")
    x1 = _matmul(heads0, W_out[0], res=xs, name="out_proj0")
    hf0, gu0, act0, x2 = ffn_fwd(x1, 0)

    h1 = _rms_fwd(x2, mix_norm[1:2], name="mix_norm1")
    p1 = _matmul(h1, W_gm, name="gm_in")
    heads1 = _gmlp_fwd(p1, ln_g, ln_b, ws, bsb, name="gmlp_fwd")
    heads1 = _attn_fwd(p1, 2 * D_TOK // D_MEM, kv[1], heads1, name="attn_fwd1")
    x3 = _matmul(heads1, W_out[1], res=x2, name="out_proj1")
    hf1, gu1, act1, x4 = ffn_fwd(x3, 1)

    dx, g_final, loss_part = _final_loss(x4, final_norm.reshape(1, D_MODEL), tgt, name="final_loss")

    def ffn_bwd(dx, xin, hf, gu, act, i):
        dgu = _ffn_out_dx(dx, W_fo[i], gu, name=f"ffn_out_dx{i}")
        g_wfo = _matmul(act, dx, ta=True, name=f"ffn_out_dw{i}")
        g_wfi = _matmul(hf, dgu, ta=True, b_halves=True, name=f"ffn_in_dw{i}")
        dhf = _matmul(dgu, W_fi[i], tb=True, a_halves=True, name=f"ffn_in_dx{i}")
        dx, g_norm = _rms_bwd(xin, ffn_norm[i:i + 1], dhf, dx, name=f"ffn_norm_bwd{i}")
        return dx, g_wfi, g_wfo, g_norm

    def mem_bwd(dkv, i):
        g_wkv = _matmul(mem_n[i], dkv, ta=True, name=f"mem_kv_dw{i}")
        dmn = _matmul(dkv, W_kv[i], tb=True, name=f"mem_kv_dx{i}")
        _, g_norm = _rms_bwd(mems, mem_norm[i:i + 1], dmn, jnp.zeros_like(mems), name=f"mem_norm_bwd{i}")
        return g_wkv, g_norm

    dx, g_wfi1, g_wfo1, g_ffn1 = ffn_bwd(dx, x3, hf1, gu1, act1, 1)
    dheads = _matmul(dx, W_out[1], tb=True, name="out_proj_dx1")
    g_wout1 = _matmul(heads1, dx, ta=True, name="out_proj_dw1")
    dp, g_ws, g_bs, g_lng, g_lnb = _gmlp_bwd(p1, ln_g, ln_b, ws, bsb, dheads, name="gmlp_bwd")
    dp, dk, dv = _attn_bwd(p1, 2 * D_TOK // D_MEM, kv[1], dheads, dp, name="attn_bwd1")
    g_wkv1, g_mem1 = mem_bwd(jnp.concatenate([dk, dv], axis=1), 1)
    g_wgm = _matmul(h1, dp, ta=True, name="gm_in_dw")
    dh = _matmul(dp, W_gm, tb=True, name="gm_in_dx")
    dx, g_mix1 = _rms_bwd(x2, mix_norm[1:2], dh, dx, name="mix_norm_bwd1")

    dx, g_wfi0, g_wfo0, g_ffn0 = ffn_bwd(dx, x1, hf0, gu0, act0, 0)
    dheads = _matmul(dx, W_out[0], tb=True, name="out_proj_dx0")
    g_wout0 = _matmul(heads0, dx, ta=True, name="out_proj_dw0")
    dp, g_lb0, g_onorm = _hgrn2_bwd(p0, lb0, hg_onorm, o0, states, dheads, name="hgrn2_bwd")
    dp, dk, dv = _attn_bwd(p0, 4 * D_TOK // D_MEM, kv[0], dheads, dp, name="attn_bwd0")
    g_wkv0, g_mem0 = mem_bwd(jnp.concatenate([dk, dv], axis=1), 0)
    g_whg = _matmul(h0, dp, ta=True, name="hg_in_dw")
    dh = _matmul(dp, W_hg, tb=True, name="hg_in_dx")
    grad_x, g_mix0 = _rms_bwd(xs, mix_norm[0:1], dh, dx, name="mix_norm_bwd0")

    full_grads = [jnp.stack([g_wkv0, g_wkv1]), jnp.stack([g_wout0, g_wout1]), g_whg[None], g_wgm[None],
                  jnp.stack([g_wfi0, g_wfi1]), jnp.stack([g_wfo0, g_wfo1])]
    gp = _pack_grads(full_grads, shards, BF16)
    gp = jnp.moveaxis(gp.reshape(4, 2, rows_big, LANE), 1, 0)
    from_sibling = _swap_with_sibling(gp, name="reduce_sibling")
    mine_c = lax.dynamic_index_in_dim(gp, mc, 0, keepdims=False)
    chip_sum = _add_pairs(mine_c.reshape(-1, LANE), from_sibling.reshape(-1, LANE), name="reduce_add_sibling",
                          out_dtype=BF16).reshape(4, rows_big, LANE)
    from_chips = _swap_with_chips(chip_sum, name="reduce_chips")
    own = lax.dynamic_index_in_dim(chip_sum, 2 * mx + my, 0, keepdims=False)
    g_local = _add_four(own, from_chips, name="reduce_add_chips")
    g_shards = _unpack_local(g_local, shards)

    small = [loss_part, jnp.concatenate([g_mix0, g_mix1], axis=1), jnp.concatenate([g_mem0, g_mem1], axis=1),
             g_lb0, g_onorm, g_lng, g_lnb, g_ws.reshape(1, -1), g_bs.reshape(1, -1),
             jnp.concatenate([g_ffn0, g_ffn1], axis=1), g_final]
    sizes = [t.shape[1] for t in small]
    small_rows = _pad_rows(jnp.concatenate(small, axis=1).reshape(-1, LANE), 8)
    red = _all_gather(small_rows, name="reduce_small", in_vmem=True, reduce_sum=True).reshape(-1)
    pieces, off = [], 0
    for n in sizes:
        pieces.append(red[off:off + n])
        off += n
    loss = pieces[0][0]
    g_mix_norm = pieces[1].reshape(2, D_MODEL)
    g_mem_norm = pieces[2].reshape(2, D_MODEL)
    g_hg_lb = pieces[3][None, :] * lb0 * (jnp.eye(3, dtype=F32)[:, 0:1] - lb_soft)
    g_hg_onorm = pieces[4].reshape(1, D_TOK)
    width = D_TOK // N_DEV
    g_gm_ln_g = lax.dynamic_slice(pieces[5], (me * width,), (width,)).reshape(1, width)
    g_gm_ln_b = lax.dynamic_slice(pieces[6], (me * width,), (width,)).reshape(1, width)
    g_gm_ws = pieces[7].reshape(gm_ws.shape)
    g_gm_bs = pieces[8].reshape(gm_bs.shape)
    g_ffn_norm = pieces[9].reshape(2, D_MODEL)
    g_final_norm = pieces[10]

    grads = [g_mix_norm, g_mem_norm, g_shards[0], g_shards[1], g_shards[2], g_hg_lb, g_hg_onorm, g_shards[3],
             g_gm_ln_g, g_gm_ln_b, g_gm_ws, g_gm_bs, g_ffn_norm, g_shards[4], g_shards[5], g_final_norm]
    weights = [mix_norm, mem_norm, w_mem_kv, w_out, hg_w_in, hg_lb, hg_onorm, gm_w_in, gm_ln_g, gm_ln_b, gm_ws, gm_bs,
               ffn_norm, w_ffn_in, w_ffn_out, final_norm]
    ms = [m_mix_norm, m_mem_norm, m_w_mem_kv, m_w_out, m_hg_w_in, m_hg_lb, m_hg_onorm, m_gm_w_in, m_gm_ln_g,
          m_gm_ln_b, m_gm_ws, m_gm_bs, m_ffn_norm, m_w_ffn_in, m_w_ffn_out, m_final_norm]
    vs = [v_mix_norm, v_mem_norm, v_w_mem_kv, v_w_out, v_hg_w_in, v_hg_lb, v_hg_onorm, v_gm_w_in, v_gm_ln_g,
          v_gm_ln_b, v_gm_ws, v_gm_bs, v_ffn_norm, v_w_ffn_in, v_w_ffn_out, v_final_norm]
    deltas, new_m, new_v = [], [], []
    for n, (w, g, m, v) in enumerate(zip(weights, grads, ms, vs)):
        if w.ndim == 1:
            d, nm, nv = _adamw(w[None], g.reshape(1, -1), m[None], v[None], name=f"adamw{n}")
            d, nm, nv = d[0], nm[0], nv[0]
        else:
            d, nm, nv = _adamw(w, g.reshape(w.shape), m, v, name=f"adamw{n}")
        deltas.append(d)
        new_m.append(nm)
        new_v.append(nv)
    grads = [g.reshape(w.shape) for g, w in zip(grads, weights)]
    return (loss, grad_x[None], *grads, *deltas, *new_m, *new_v)
```

```python
import functools

import jax
import jax.numpy as jnp
from jax import lax
from jax.experimental import pallas as pl
from jax.experimental.pallas import tpu as pltpu

F32 = jnp.float32
BF16 = jnp.bfloat16
MXU_DTYPE = jnp.bfloat16
MESH_ID = pl.DeviceIdType.MESH

N_DEV = 8
EPS = 1e-6
D_MODEL = 1024
D_TOK = 768
D_MEM = 256
N_HEADS = 6
HEAD = 128
MEM_HEADS = 4
MEM_HDIM = 64
GM_CHUNK = 128
D_FF = 2816
HG_SUB = 16
HG_IN = 4 * D_TOK + D_MEM
GM_IN = 2 * D_TOK + D_MEM
LANE = 128

ADAM_LR = 0.001
ADAM_B1 = 0.9
ADAM_B2 = 0.999
ADAM_EPS = 1e-08
ADAM_WD = 0.01
ADAM_STEP = 10

VMEM_LIMIT = 48 * 2 ** 20


def _params(sem=None):
    return pltpu.CompilerParams(dimension_semantics=sem, vmem_limit_bytes=VMEM_LIMIT)


def _tile(n, cap, q=LANE):
    if n <= cap:
        return n
    best = None
    for t in range(q, cap + 1, q):
        if n % t == 0:
            best = t
    assert best is not None, (n, cap, q)
    return best


def _sigmoid(x):
    return 1.0 / (1.0 + jnp.exp(-x))


def _gelu(x):
    return 0.5 * x * (1.0 + lax.erf(x * 0.7071067811865476))


def _gelu_grad(x):
    return 0.5 * (1.0 + lax.erf(x * 0.7071067811865476)) + x * jnp.exp(-0.5 * x * x) * 0.3989422804014327


def _matmul(a, b, *, name, ta=False, tb=False, res=None, out_dtype=F32, a_halves=False, b_halves=False, dep=None):
    if a_halves:
        assert not ta
        M, K = a.shape[1], 2 * a.shape[2]
    else:
        K, M = a.shape if ta else a.shape[::-1]
    if b_halves:
        assert not tb and b.shape[1] == K
        N = 2 * b.shape[2]
    else:
        N = b.shape[0] if tb else b.shape[1]
        assert (b.shape[1] if tb else b.shape[0]) == K
    tm = _tile(M, 512 if ta else 1024)
    tn = _tile(N // 2 if b_halves else N, 1792)
    tk = _tile(K // 2 if a_halves else K, 1664)
    nk = K // tk
    dims = (((0 if ta else 1,), (1 if tb else 0,)), ((), ()))

    def body(*refs):
        a_ref, b_ref = refs[:2]
        r_ref = refs[2] if res is not None else None
        o_ref, acc = refs[-2:]
        k = pl.program_id(2)

        @pl.when(k == 0)
        def _():
            acc[...] = jnp.zeros_like(acc)

        acc[...] += lax.dot_general(a_ref[...].astype(MXU_DTYPE), b_ref[...].astype(MXU_DTYPE), dims,
                                    preferred_element_type=F32)

        @pl.when(k == nk - 1)
        def _():
            r = acc[...]
            if res is not None:
                r = r + r_ref[...].astype(F32)
            o_ref[...] = r.astype(out_dtype)

    if a_halves:
        kh = nk // 2
        a_spec = pl.BlockSpec((None, tm, tk), lambda i, j, k: (k // kh, i, k % kh))
    elif ta:
        a_spec = pl.BlockSpec((tk, tm), lambda i, j, k: (k, i))
    else:
        a_spec = pl.BlockSpec((tm, tk), lambda i, j, k: (i, k))
    if b_halves:
        nh = N // 2 // tn
        b_spec = pl.BlockSpec((None, tk, tn), lambda i, j, k: (j // nh, k, j % nh))
    elif tb:
        b_spec = pl.BlockSpec((tn, tk), lambda i, j, k: (j, k))
    else:
        b_spec = pl.BlockSpec((tk, tn), lambda i, j, k: (k, j))
    o_spec = pl.BlockSpec((tm, tn), lambda i, j, k: (i, j))
    in_specs = [a_spec, b_spec] + ([o_spec] if res is not None else [])
    args = (a, b) + ((res,) if res is not None else ())
    if dep is not None:
        in_specs.append(pl.BlockSpec(memory_space=pl.ANY))
        args += (dep,)
    return pl.pallas_call(
        body, name=name, grid=(M // tm, N // tn, nk), in_specs=in_specs, out_specs=o_spec,
        out_shape=jax.ShapeDtypeStruct((M, N), out_dtype), scratch_shapes=[pltpu.VMEM((tm, tn), F32)],
        compiler_params=_params(("parallel", "parallel", "arbitrary")))(*args)


def _ffn_in(hf, w, *, name):
    S, K = hf.shape
    tm = _tile(S, 512)
    tn = _tile(D_FF, 1408)
    nh = D_FF // tn

    def body(a_ref, bg_ref, bu_ref, gu_ref, act_ref):
        av = a_ref[...].astype(MXU_DTYPE)
        gate = jnp.dot(av, bg_ref[...].astype(MXU_DTYPE), preferred_element_type=F32)
        up = jnp.dot(av, bu_ref[...].astype(MXU_DTYPE), preferred_element_type=F32)
        gu_ref[0] = gate.astype(gu_ref.dtype)
        gu_ref[1] = up.astype(gu_ref.dtype)
        act_ref[...] = (gate * _sigmoid(gate) * up).astype(act_ref.dtype)

    return pl.pallas_call(
        body, name=name, grid=(S // tm, nh),
        in_specs=[pl.BlockSpec((tm, K), lambda i, j: (i, 0)), pl.BlockSpec((K, tn), lambda i, j: (0, j)),
                  pl.BlockSpec((K, tn), lambda i, j: (0, j + nh))],
        out_specs=[pl.BlockSpec((2, tm, tn), lambda i, j: (0, i, j)), pl.BlockSpec((tm, tn), lambda i, j: (i, j))],
        out_shape=[jax.ShapeDtypeStruct((2, S, D_FF), BF16), jax.ShapeDtypeStruct((S, D_FF), BF16)],
        compiler_params=_params(("parallel", "parallel")))(hf, w, w)


def _ffn_out_dx(dx, w, gu, *, name):
    S, K = dx.shape
    tm = _tile(S, 512)
    tn = _tile(D_FF, 1408)

    def body(a_ref, b_ref, gu_ref, o_ref):
        da = lax.dot_general(a_ref[...].astype(MXU_DTYPE), b_ref[...].astype(MXU_DTYPE), (((1,), (1,)), ((), ())),
                             preferred_element_type=F32)
        gate = gu_ref[0].astype(F32)
        up = gu_ref[1].astype(F32)
        sg = _sigmoid(gate)
        o_ref[0] = (da * up * sg * (1.0 + gate * (1.0 - sg))).astype(o_ref.dtype)
        o_ref[1] = (da * gate * sg).astype(o_ref.dtype)

    halves = pl.BlockSpec((2, tm, tn), lambda i, j: (0, i, j))
    return pl.pallas_call(
        body, name=name, grid=(S // tm, D_FF // tn),
        in_specs=[pl.BlockSpec((tm, K), lambda i, j: (i, 0)), pl.BlockSpec((tn, K), lambda i, j: (j, 0)), halves],
        out_specs=halves, out_shape=jax.ShapeDtypeStruct((2, S, D_FF), BF16),
        compiler_params=_params(("parallel", "parallel")))(dx, w, gu)


def _rms_fwd(x, g, *, name, dep=None):
    R, Dm = x.shape
    tr = _tile(R, 512, 8)

    def body(x_ref, g_ref, *rest):
        o_ref = rest[-1]
        xv = x_ref[...]
        r = lax.rsqrt(jnp.mean(xv * xv, axis=-1, keepdims=True) + EPS)
        o_ref[...] = (xv * r * g_ref[...]).astype(o_ref.dtype)

    in_specs = [pl.BlockSpec((tr, Dm), lambda i: (i, 0)), pl.BlockSpec((1, Dm), lambda i: (0, 0))]
    args = (x, g)
    if dep is not None:
        in_specs.append(pl.BlockSpec(memory_space=pl.ANY))
        args += (dep,)
    return pl.pallas_call(
        body, name=name, grid=(R // tr,), in_specs=in_specs,
        out_specs=pl.BlockSpec((tr, Dm), lambda i: (i, 0)), out_shape=jax.ShapeDtypeStruct((R, Dm), BF16),
        compiler_params=_params(("parallel",)))(*args)


def _rms_bwd(x, g, dh, dres, *, name):
    R, Dm = x.shape
    tr = _tile(R, 256, 8)

    def body(x_ref, g_ref, dh_ref, dres_ref, dx_ref, dg_ref):
        @pl.when(pl.program_id(0) == 0)
        def _():
            dg_ref[...] = jnp.zeros_like(dg_ref)

        xv = x_ref[...]
        r = lax.rsqrt(jnp.mean(xv * xv, axis=-1, keepdims=True) + EPS)
        xh = xv * r
        dhv = dh_ref[...].astype(F32)
        dg_ref[...] += jnp.sum(dhv * xh, axis=0, keepdims=True)
        u = dhv * g_ref[...]
        dx = r * (u - xh * jnp.mean(u * xh, axis=-1, keepdims=True))
        dx_ref[...] = dres_ref[...] + dx

    row = pl.BlockSpec((tr, Dm), lambda i: (i, 0))
    vec = pl.BlockSpec((1, Dm), lambda i: (0, 0))
    return pl.pallas_call(
        body, name=name, grid=(R // tr,), in_specs=[row, vec, row, row], out_specs=[row, vec],
        out_shape=[jax.ShapeDtypeStruct((R, Dm), F32), jax.ShapeDtypeStruct((1, Dm), F32)],
        compiler_params=_params(("arbitrary",)))(x, g, dh, dres)


def _final_loss(x, g, tgt, *, name):
    R, Dm = x.shape
    tr = _tile(R, 256, 8)

    def body(x_ref, g_ref, t_ref, dx_ref, dg_ref, loss_ref):
        @pl.when(pl.program_id(0) == 0)
        def _():
            dg_ref[...] = jnp.zeros_like(dg_ref)
            loss_ref[...] = jnp.zeros_like(loss_ref)

        xv = x_ref[...]
        r = lax.rsqrt(jnp.mean(xv * xv, axis=-1, keepdims=True) + EPS)
        xh = xv * r
        gv = g_ref[...]
        err = xh * gv - t_ref[...]
        part = 0.5 * jnp.sum(jnp.mean(err * err, axis=-1, keepdims=True), axis=0, keepdims=True)
        loss_ref[...] += jnp.broadcast_to(part, loss_ref.shape)
        dy = err * (1.0 / Dm)
        dg_ref[...] += jnp.sum(dy * xh, axis=0, keepdims=True)
        u = dy * gv
        dx_ref[...] = r * (u - xh * jnp.mean(u * xh, axis=-1, keepdims=True))

    row = pl.BlockSpec((tr, Dm), lambda i: (i, 0))
    vec = pl.BlockSpec((1, Dm), lambda i: (0, 0))
    one = pl.BlockSpec((1, LANE), lambda i: (0, 0))
    return pl.pallas_call(
        body, name=name, grid=(R // tr,), in_specs=[row, vec, row], out_specs=[row, vec, one],
        out_shape=[jax.ShapeDtypeStruct((R, Dm), F32), jax.ShapeDtypeStruct((1, Dm), F32),
                   jax.ShapeDtypeStruct((1, LANE), F32)],
        compiler_params=_params(("arbitrary",)))(x, g, tgt)


def _head_mask(h):
    lane = lax.broadcasted_iota(jnp.int32, (1, D_MEM), 1)
    return (lane >= h * MEM_HDIM) & (lane < (h + 1) * MEM_HDIM)


def _attn_probs(qv, k_mx, mask):
    s = lax.dot_general(jnp.where(mask, qv, 0.0).astype(MXU_DTYPE), k_mx, (((1,), (1,)), ((), ())),
                        preferred_element_type=F32) * (MEM_HDIM ** -0.5)
    e = jnp.exp(s - jnp.max(s, axis=-1, keepdims=True))
    return e / jnp.sum(e, axis=-1, keepdims=True)


def _attn_fwd(p, qcol, kv, heads, *, name):
    S = p.shape[0]
    M = kv.shape[0]
    ts = _tile(S, 512, 8)

    def body(q_ref, k_ref, v_ref, heads_in, o_ref):
        del heads_in
        qv = q_ref[...]
        kx = k_ref[...].astype(MXU_DTYPE)
        vv = v_ref[...]
        out = jnp.zeros((ts, D_MEM), F32)
        for h in range(MEM_HEADS):
            mask = _head_mask(h)
            pr = _attn_probs(qv, kx, mask)
            out = out + jnp.dot(pr.astype(MXU_DTYPE), jnp.where(mask, vv, 0.0).astype(MXU_DTYPE),
                                preferred_element_type=F32)
        o_ref[...] = out.astype(o_ref.dtype)

    return pl.pallas_call(
        body, name=name, grid=(S // ts,),
        in_specs=[pl.BlockSpec((ts, D_MEM), lambda i: (i, qcol)), pl.BlockSpec((M, D_MEM), lambda i: (0, 0)),
                  pl.BlockSpec((M, D_MEM), lambda i: (0, 1)), pl.BlockSpec(memory_space=pl.ANY)],
        out_specs=pl.BlockSpec((ts, D_MEM), lambda i: (i, D_TOK // D_MEM)),
        out_shape=jax.ShapeDtypeStruct(heads.shape, heads.dtype), input_output_aliases={3: 0},
        compiler_params=_params(("parallel",)))(p, kv, kv, heads)


def _attn_bwd(p, qcol, kv, dheads, dp, *, name):
    S = p.shape[0]
    M = kv.shape[0]
    ts = _tile(S, 512, 8)
    scale = MEM_HDIM ** -0.5

    def body(q_ref, k_ref, v_ref, do_ref, dp_in, dq_ref, dk_ref, dv_ref):
        del dp_in

        @pl.when(pl.program_id(0) == 0)
        def _():
            dk_ref[...] = jnp.zeros_like(dk_ref)
            dv_ref[...] = jnp.zeros_like(dv_ref)

        qv = q_ref[...]
        kv_ = k_ref[...]
        kx = kv_.astype(MXU_DTYPE)
        vv = v_ref[...]
        dox = do_ref[...].astype(MXU_DTYPE)
        qx = qv.astype(MXU_DTYPE)
        dq = jnp.zeros((ts, D_MEM), F32)
        for h in range(MEM_HEADS):
            mask = _head_mask(h)
            pr = _attn_probs(qv, kx, mask)
            vh = jnp.where(mask, vv, 0.0).astype(MXU_DTYPE)
            dpr = lax.dot_general(dox, vh, (((1,), (1,)), ((), ())), preferred_element_type=F32)
            ds = (pr * (dpr - jnp.sum(dpr * pr, axis=-1, keepdims=True)) * scale).astype(MXU_DTYPE)
            dq = dq + jnp.dot(ds, jnp.where(mask, kv_, 0.0).astype(MXU_DTYPE), preferred_element_type=F32)
            dk_h = lax.dot_general(ds, qx, (((0,), (0,)), ((), ())), preferred_element_type=F32)
            dv_h = lax.dot_general(pr.astype(MXU_DTYPE), dox, (((0,), (0,)), ((), ())), preferred_element_type=F32)
            dk_ref[...] += jnp.where(mask, dk_h, 0.0)
            dv_ref[...] += jnp.where(mask, dv_h, 0.0)
        dq_ref[...] = dq.astype(dq_ref.dtype)

    return pl.pallas_call(
        body, name=name, grid=(S // ts,),
        in_specs=[pl.BlockSpec((ts, D_MEM), lambda i: (i, qcol)), pl.BlockSpec((M, D_MEM), lambda i: (0, 0)),
                  pl.BlockSpec((M, D_MEM), lambda i: (0, 1)),
                  pl.BlockSpec((ts, D_MEM), lambda i: (i, D_TOK // D_MEM)), pl.BlockSpec(memory_space=pl.ANY)],
        out_specs=[pl.BlockSpec((ts, D_MEM), lambda i: (i, qcol)), pl.BlockSpec((M, D_MEM), lambda i: (0, 0)),
                   pl.BlockSpec((M, D_MEM), lambda i: (0, 0))],
        out_shape=[jax.ShapeDtypeStruct(dp.shape, dp.dtype), jax.ShapeDtypeStruct((M, D_MEM), F32),
                   jax.ShapeDtypeStruct((M, D_MEM), F32)],
        input_output_aliases={4: 0}, compiler_params=_params(("arbitrary",)))(p, kv, kv, dheads, dp)


def _gm_forward_parts(u_ref, v_ref, lng_ref, lnb_ref, w_ref, bsb_ref):
    zu = _gelu(u_ref[...])
    zv = _gelu(v_ref[...])
    mu = jnp.mean(zv, axis=-1, keepdims=True)
    cen = zv - mu
    rs = lax.rsqrt(jnp.mean(cen * cen, axis=-1, keepdims=True) + EPS)
    vh = cen * rs
    vn = vh * lng_ref[...] + lnb_ref[...]
    row = lax.broadcasted_iota(jnp.int32, (GM_CHUNK, GM_CHUNK), 0)
    col = lax.broadcasted_iota(jnp.int32, (GM_CHUNK, GM_CHUNK), 1)
    tril = row >= col
    wm = [jnp.where(tril, w_ref[g], 0.0).astype(MXU_DTYPE) for g in range(N_HEADS)]
    vnx = [vn[:, g * HEAD:(g + 1) * HEAD].astype(MXU_DTYPE) for g in range(N_HEADS)]
    sv = [jnp.dot(wm[g], vnx[g], preferred_element_type=F32) + bsb_ref[g] for g in range(N_HEADS)]
    return zu, vh, rs, wm, vnx, sv, tril


def _gmlp_fwd(p, lng, lnb, ws, bsb, *, name):
    S = p.shape[0]

    def body(u_ref, v_ref, lng_ref, lnb_ref, w_ref, bsb_ref, o_ref):
        zu, _, _, _, _, sv, _ = _gm_forward_parts(u_ref, v_ref, lng_ref, lnb_ref, w_ref, bsb_ref)
        for g in range(N_HEADS):
            o_ref[:, g * HEAD:(g + 1) * HEAD] = (zu[:, g * HEAD:(g + 1) * HEAD] * sv[g]).astype(o_ref.dtype)

    blk = lambda c: pl.BlockSpec((GM_CHUNK, D_TOK), lambda i: (i, c))
    vec = pl.BlockSpec((1, D_TOK), lambda i: (0, 0))
    cube = pl.BlockSpec((N_HEADS, GM_CHUNK, GM_CHUNK), lambda i: (0, 0, 0))
    return pl.pallas_call(
        body, name=name, grid=(S // GM_CHUNK,), in_specs=[blk(0), blk(1), vec, vec, cube, cube],
        out_specs=blk(0), out_shape=jax.ShapeDtypeStruct((S, D_MODEL), BF16),
        compiler_params=_params(("parallel",)))(p, p, lng, lnb, ws, bsb)


def _gmlp_bwd(p, lng, lnb, ws, bsb, dheads, *, name):
    S = p.shape[0]

    def body(u_ref, v_ref, lng_ref, lnb_ref, w_ref, bsb_ref, dt_ref, dp_ref, dw_ref, dbs_ref, dlg_ref, dlb_ref):
        @pl.when(pl.program_id(0) == 0)
        def _():
            dw_ref[...] = jnp.zeros_like(dw_ref)
            dbs_ref[...] = jnp.zeros_like(dbs_ref)
            dlg_ref[...] = jnp.zeros_like(dlg_ref)
            dlb_ref[...] = jnp.zeros_like(dlb_ref)

        zu, vh, rs, wm, vnx, sv, tril = _gm_forward_parts(u_ref, v_ref, lng_ref, lnb_ref, w_ref, bsb_ref)
        dt = dt_ref[...].astype(F32)
        dvn_parts = []
        for g in range(N_HEADS):
            sl = slice(g * HEAD, (g + 1) * HEAD)
            dsv = dt[:, sl] * zu[:, sl]
            dp_ref[:, sl] = (dt[:, sl] * sv[g] * _gelu_grad(u_ref[:, sl])).astype(dp_ref.dtype)
            dsx = dsv.astype(MXU_DTYPE)
            dw = lax.dot_general(dsx, vnx[g], (((1,), (1,)), ((), ())), preferred_element_type=F32)
            dw_ref[g] += jnp.where(tril, dw, 0.0)
            dbs_ref[g] += jnp.sum(dsv, axis=-1, keepdims=True)
            dvn_parts.append(lax.dot_general(wm[g], dsx, (((0,), (0,)), ((), ())), preferred_element_type=F32))
        dvn = jnp.concatenate(dvn_parts, axis=-1)
        dlg_ref[...] += jnp.sum(dvn * vh, axis=0, keepdims=True)
        dlb_ref[...] += jnp.sum(dvn, axis=0, keepdims=True)
        dvh = dvn * lng_ref[...]
        dzv = rs * (dvh - jnp.mean(dvh, axis=-1, keepdims=True) - vh * jnp.mean(dvh * vh, axis=-1, keepdims=True))
        dp_ref[:, D_TOK:] = (dzv * _gelu_grad(v_ref[...])).astype(dp_ref.dtype)

    blk = lambda c: pl.BlockSpec((GM_CHUNK, D_TOK), lambda i: (i, c))
    vec = pl.BlockSpec((1, D_TOK), lambda i: (0, 0))
    cube = pl.BlockSpec((N_HEADS, GM_CHUNK, GM_CHUNK), lambda i: (0, 0, 0))
    col = pl.BlockSpec((N_HEADS, GM_CHUNK, 1), lambda i: (0, 0, 0))
    return pl.pallas_call(
        body, name=name, grid=(S // GM_CHUNK,), in_specs=[blk(0), blk(1), vec, vec, cube, cube, blk(0)],
        out_specs=[pl.BlockSpec((GM_CHUNK, 2 * D_TOK), lambda i: (i, 0)), cube, col, vec, vec],
        out_shape=[jax.ShapeDtypeStruct((S, GM_IN), BF16), jax.ShapeDtypeStruct((N_HEADS, GM_CHUNK, GM_CHUNK), F32),
                   jax.ShapeDtypeStruct((N_HEADS, GM_CHUNK, 1), F32), jax.ShapeDtypeStruct((1, D_TOK), F32),
                   jax.ShapeDtypeStruct((1, D_TOK), F32)],
        compiler_params=_params(("arbitrary",)))(p, p, lng, lnb, ws, bsb, dheads)


def _tri(n, upper):
    r = lax.broadcasted_iota(jnp.int32, (n, n), 0)
    c = lax.broadcasted_iota(jnp.int32, (n, n), 1)
    return jnp.where((r <= c) if upper else (r >= c), 1.0, 0.0).astype(F32)


def _hg_gates(fz, lb):
    sg = _sigmoid(fz)
    f = lb + (1.0 - lb) * sg
    kk = (1.0 - lb) * (1.0 - sg)
    return sg, f, jnp.log(f), kk


def _hgrn2_fwd(p, lb, onorm, *, name):
    S = p.shape[0]
    C = HG_SUB
    tb = _tile(S, 256, C)
    nsub = tb // C

    def body(q_ref, fz_ref, v_ref, g_ref, lb_ref, on_ref, tok_ref, o_ref, st_ref, state, bsc, ksc, vsc):
        @pl.when(pl.program_id(0) == 0)
        def _():
            state[...] = jnp.zeros_like(state)

        lower = _tri(C, False)
        tt = lax.broadcasted_iota(jnp.int32, (C, HEAD), 0)

        def sub(c, carry):
            rows = pl.ds(pl.multiple_of(c * C, C), C)
            for h in range(N_HEADS):
                cols = slice(h * HEAD, (h + 1) * HEAD)
                qv = q_ref[rows, cols]
                vv = v_ref[rows, cols]
                gv = g_ref[rows, cols]
                _, _, lg, kk = _hg_gates(fz_ref[rows, cols], lb_ref[:, cols])
                b = jnp.dot(lower, lg, precision=lax.Precision.HIGHEST, preferred_element_type=F32)
                st0 = state[h]
                st_ref[c, h] = st0
                inter = lax.dot_general((qv * jnp.exp(b)).astype(MXU_DTYPE), st0.astype(MXU_DTYPE),
                                        (((1,), (1,)), ((), ())), preferred_element_type=F32)
                bsc[h] = b
                ksc[h] = kk
                vsc[h] = vv
                intra = jnp.zeros((C, HEAD), F32)
                for s in range(C):
                    dec = jnp.where(tt >= s, jnp.exp(jnp.minimum(b - bsc[h, pl.ds(s, 1), :], 0.0)), 0.0)
                    a_s = jnp.sum(qv * ksc[h, pl.ds(s, 1), :] * dec, axis=-1, keepdims=True)
                    intra = intra + a_s * vsc[h, pl.ds(s, 1), :]
                o = inter + intra
                b_last = bsc[h, pl.ds(C - 1, 1), :]
                ke = kk * jnp.exp(b_last - b)
                state[h] = st0 * jnp.exp(b_last) + lax.dot_general(
                    vv.astype(MXU_DTYPE), ke.astype(MXU_DTYPE), (((0,), (0,)), ((), ())),
                    preferred_element_type=F32)
                o_ref[rows, cols] = o
                n = o * lax.rsqrt(jnp.mean(o * o, axis=-1, keepdims=True) + EPS)
                tok_ref[rows, cols] = (n * (gv * _sigmoid(gv)) * on_ref[:, cols]).astype(tok_ref.dtype)
            return carry

        lax.fori_loop(0, nsub, sub, 0)

    blk = lambda c: pl.BlockSpec((tb, D_TOK), lambda i, c=c: (i, c))
    vec = pl.BlockSpec((1, D_TOK), lambda i: (0, 0))
    stb = pl.BlockSpec((nsub, N_HEADS, HEAD, HEAD), lambda i: (i, 0, 0, 0))
    return pl.pallas_call(
        body, name=name, grid=(S // tb,), in_specs=[blk(0), blk(1), blk(2), blk(3), vec, vec],
        out_specs=[blk(0), blk(0), stb],
        out_shape=[jax.ShapeDtypeStruct((S, D_MODEL), BF16), jax.ShapeDtypeStruct((S, D_TOK), F32),
                   jax.ShapeDtypeStruct((S // C, N_HEADS, HEAD, HEAD), F32)],
        scratch_shapes=[pltpu.VMEM((N_HEADS, HEAD, HEAD), F32)] + [pltpu.VMEM((N_HEADS, C, HEAD), F32)] * 3,
        compiler_params=_params(("arbitrary",)))(p, p, p, p, lb, onorm)


def _hgrn2_bwd(p, lb, onorm, o, states, dheads, *, name):
    S = p.shape[0]
    C = HG_SUB
    tb = _tile(S, 256, C)
    nsub = tb // C
    nblk = S // tb

    def body(q_ref, fz_ref, v_ref, g_ref, lb_ref, on_ref, o_ref, st_ref, dt_ref, dp_ref, dlb_ref, don_ref, dstate,
             bsc, ksc, vsc, qsc, dosc):
        @pl.when(pl.program_id(0) == 0)
        def _():
            dstate[...] = jnp.zeros_like(dstate)
            dlb_ref[...] = jnp.zeros_like(dlb_ref)
            don_ref[...] = jnp.zeros_like(don_ref)

        lower = _tri(C, False)
        upper = _tri(C, True)
        tt = lax.broadcasted_iota(jnp.int32, (C, HEAD), 0)

        def sub(j, carry):
            c = nsub - 1 - j
            rows = pl.ds(pl.multiple_of(c * C, C), C)
            for h in range(N_HEADS):
                cols = slice(h * HEAD, (h + 1) * HEAD)
                lbv = lb_ref[:, cols]
                onv = on_ref[:, cols]
                qv = q_ref[rows, cols]
                vv = v_ref[rows, cols]
                gv = g_ref[rows, cols]
                ov = o_ref[rows, cols]
                dt = dt_ref[rows, cols].astype(F32)
                sgg = _sigmoid(gv)
                sil = gv * sgg
                rinv = lax.rsqrt(jnp.mean(ov * ov, axis=-1, keepdims=True) + EPS)
                n = ov * rinv
                don_ref[:, cols] += jnp.sum(dt * n * sil, axis=0, keepdims=True)
                dn = dt * sil * onv
                dp_ref[rows, 3 * D_TOK + h * HEAD:3 * D_TOK + (h + 1) * HEAD] = (
                    dt * n * onv * sgg * (1.0 + gv * (1.0 - sgg))).astype(dp_ref.dtype)
                do = rinv * (dn - n * jnp.mean(dn * n, axis=-1, keepdims=True))
                sg, f, lg, kk = _hg_gates(fz_ref[rows, cols], lbv)
                b = jnp.dot(lower, lg, precision=lax.Precision.HIGHEST, preferred_element_type=F32)
                bsc[h] = b
                ksc[h] = kk
                vsc[h] = vv
                qsc[h] = qv
                dosc[h] = do
                b_last = bsc[h, pl.ds(C - 1, 1), :]
                eb = jnp.exp(b)
                qe = qv * eb
                ebb = jnp.exp(b_last - b)
                ke = kk * ebb
                e_last = jnp.exp(b_last)
                st0 = st_ref[c, h]
                dst1 = dstate[h]
                st0x = st0.astype(MXU_DTYPE)
                dst1x = dst1.astype(MXU_DTYPE)
                dox = do.astype(MXU_DTYPE)
                dqe = jnp.dot(dox, st0x, preferred_element_type=F32)
                dke = jnp.dot(vv.astype(MXU_DTYPE), dst1x, preferred_element_type=F32)
                dv = lax.dot_general(ke.astype(MXU_DTYPE), dst1x, (((1,), (1,)), ((), ())),
                                     preferred_element_type=F32)
                db_last = (e_last * jnp.sum(st0 * dst1, axis=0, keepdims=True)
                           + jnp.sum(dke * ke, axis=0, keepdims=True))
                dstate[h] = dst1 * e_last + lax.dot_general(dox, qe.astype(MXU_DTYPE), (((0,), (0,)), ((), ())),
                                                            preferred_element_type=F32)
                dq = dqe * eb
                db = dqe * qe - dke * ke
                dkk = dke * ebb
                for s in range(C):
                    dec = jnp.where(tt >= s, jnp.exp(jnp.minimum(b - bsc[h, pl.ds(s, 1), :], 0.0)), 0.0)
                    da_s = jnp.sum(do * vsc[h, pl.ds(s, 1), :], axis=-1, keepdims=True)
                    pq = da_s * ksc[h, pl.ds(s, 1), :] * dec
                    dq = dq + pq
                    db = db + pq * qv
                for t in range(C):
                    q_t = qsc[h, pl.ds(t, 1), :]
                    do_t = dosc[h, pl.ds(t, 1), :]
                    dec = jnp.where(tt <= t, jnp.exp(jnp.minimum(bsc[h, pl.ds(t, 1), :] - b, 0.0)), 0.0)
                    da_t = jnp.sum(vv * do_t, axis=-1, keepdims=True)
                    pk = da_t * q_t * dec
                    dkk = dkk + pk
                    db = db - pk * kk
                    a_t = jnp.sum(q_t * kk * dec, axis=-1, keepdims=True)
                    dv = dv + a_t * do_t
                db = db + jnp.where(tt == C - 1, db_last, 0.0)
                dlg = jnp.dot(upper, db, precision=lax.Precision.HIGHEST, preferred_element_type=F32)
                w = dlg / f - dkk
                dp_ref[rows, cols] = dq.astype(dp_ref.dtype)
                dp_ref[rows, D_TOK + h * HEAD:D_TOK + (h + 1) * HEAD] = (
                    w * (1.0 - lbv) * sg * (1.0 - sg)).astype(dp_ref.dtype)
                dp_ref[rows, 2 * D_TOK + h * HEAD:2 * D_TOK + (h + 1) * HEAD] = dv.astype(dp_ref.dtype)
                dlb_ref[:, cols] += jnp.sum(w * (1.0 - sg), axis=0, keepdims=True)
            return carry

        lax.fori_loop(0, nsub, sub, 0)

    blk = lambda c: pl.BlockSpec((tb, D_TOK), lambda i, c=c: (nblk - 1 - i, c))
    vec = pl.BlockSpec((1, D_TOK), lambda i: (0, 0))
    stb = pl.BlockSpec((nsub, N_HEADS, HEAD, HEAD), lambda i: (nblk - 1 - i, 0, 0, 0))
    small = jax.ShapeDtypeStruct((1, D_TOK), F32)
    return pl.pallas_call(
        body, name=name, grid=(nblk,), in_specs=[blk(0), blk(1), blk(2), blk(3), vec, vec, blk(0), stb, blk(0)],
        out_specs=[pl.BlockSpec((tb, 4 * D_TOK), lambda i: (nblk - 1 - i, 0)), vec, vec],
        out_shape=[jax.ShapeDtypeStruct((S, HG_IN), BF16), small, small],
        scratch_shapes=[pltpu.VMEM((N_HEADS, HEAD, HEAD), F32)] + [pltpu.VMEM((N_HEADS, C, HEAD), F32)] * 5,
        compiler_params=_params(("arbitrary",)))(p, p, p, p, lb, onorm, o, states, dheads)


def _adamw(w, g, m, v, *, name):
    shape = w.shape
    cols = shape[-1]
    w2, g2, m2, v2 = (t.reshape(-1, cols) for t in (w, g, m, v))
    R = w2.shape[0]
    tr = _tile(R, 512, 8)

    def body(w_ref, g_ref, m_ref, v_ref, d_ref, nm_ref, nv_ref):
        gv = g_ref[...]
        nm = ADAM_B1 * m_ref[...] + (1.0 - ADAM_B1) * gv
        nv = ADAM_B2 * v_ref[...] + (1.0 - ADAM_B2) * (gv * gv)
        m_hat = nm / (1.0 - ADAM_B1 ** ADAM_STEP)
        v_hat = nv / (1.0 - ADAM_B2 ** ADAM_STEP)
        d_ref[...] = -ADAM_LR * (m_hat / (jnp.sqrt(v_hat) + ADAM_EPS) + ADAM_WD * w_ref[...])
        nm_ref[...] = nm
        nv_ref[...] = nv

    spec = pl.BlockSpec((tr, cols), lambda i: (i, 0))
    out = jax.ShapeDtypeStruct((R, cols), F32)
    d, nm, nv = pl.pallas_call(body, name=name, grid=(R // tr,), in_specs=[spec] * 4, out_specs=[spec] * 3,
                               out_shape=[out] * 3, compiler_params=_params(("parallel",)))(w2, g2, m2, v2)
    return d.reshape(shape), nm.reshape(shape), nv.reshape(shape)


def _add_received(own, got, *, name):
    R = own.shape[0]
    n = got.shape[0]
    tr = _tile(R, 2048, 16)

    def body(a_ref, b_ref, o_ref):
        acc = a_ref[...].astype(F32)
        for k in range(n):
            acc = acc + b_ref[k].astype(F32)
        o_ref[...] = acc

    return pl.pallas_call(
        body, name=name, grid=(R // tr,),
        in_specs=[pl.BlockSpec((tr, LANE), lambda i: (i, 0)), pl.BlockSpec((n, tr, LANE), lambda i: (0, i, 0))],
        out_specs=pl.BlockSpec((tr, LANE), lambda i: (i, 0)), out_shape=jax.ShapeDtypeStruct((R, LANE), F32),
        compiler_params=_params(("parallel",)))(own, got)


def _place():
    return lax.axis_index("x"), lax.axis_index("y"), lax.axis_index("c")


def _all_gather(x, *, name, in_vmem, reduce_sum=False, with_token=False):
    R, Cc = x.shape
    space = pltpu.VMEM if in_vmem else pl.ANY

    def body(x_ref, out_ref, *scratch):
        if with_token:
            scratch[0][...] = jnp.zeros_like(scratch[0])
            scratch = scratch[1:]
        if reduce_sum:
            gat_ref, send_sems, recv_sems, local_sem = scratch
        else:
            gat_ref = out_ref
            send_sems, recv_sems, local_sem = scratch
        mx, my, mc = _place()
        me, sibling = (mx, my, mc), (mx, my, 1 - mc)
        chips = [(1 - mx, my), (mx, 1 - my), (1 - mx, 1 - my)]

        def rows(px, py, pc):
            return gat_ref.at[pl.ds((4 * px + 2 * py + pc) * R, R), :]

        def copy(k, block, to, src=None):
            return pltpu.make_async_remote_copy(
                src_ref=rows(*block) if src is None else src, dst_ref=rows(*block), send_sem=send_sems.at[k],
                recv_sem=recv_sems.at[k], device_id=to, device_id_type=MESH_ID)

        mine = pltpu.make_async_copy(x_ref, rows(*me), local_sem)
        mine.start()
        first = [copy(0, me, sibling, src=x_ref)]
        first += [copy(1 + j, me, (*chip, mc), src=x_ref) for j, chip in enumerate(chips)]
        for cp in first:
            cp.start()
        passed = [copy(4 + j, (*chip, mc), sibling) for j, chip in enumerate(chips)]
        for j, chip in enumerate(chips):
            copy(1 + j, (*chip, mc), me).wait_recv()
            passed[j].start()
        copy(0, sibling, me).wait_recv()
        for j, chip in enumerate(chips):
            copy(4 + j, (*chip, 1 - mc), me).wait_recv()
        for cp in first + passed:
            cp.wait_send()
        mine.wait()
        if reduce_sum:
            acc = gat_ref[pl.ds(0, R), :]
            for d in range(1, N_DEV):
                acc = acc + gat_ref[pl.ds(d * R, R), :]
            out_ref[...] = acc

    sems = [pltpu.SemaphoreType.DMA((7,)), pltpu.SemaphoreType.DMA((7,)), pltpu.SemaphoreType.DMA]
    if reduce_sum:
        assert in_vmem
        out_shape = jax.ShapeDtypeStruct((R, Cc), x.dtype)
        scratch = [pltpu.VMEM((N_DEV * R, Cc), x.dtype)] + sems
    else:
        out_shape = jax.ShapeDtypeStruct((N_DEV * R, Cc), x.dtype)
        scratch = sems
    out_specs = pl.BlockSpec(memory_space=space)
    if with_token:
        out_shape = (out_shape, jax.ShapeDtypeStruct((8, LANE), F32))
        out_specs = (out_specs, pl.BlockSpec(memory_space=pltpu.VMEM))
    return pl.pallas_call(
        body, name=name, out_shape=out_shape, in_specs=[pl.BlockSpec(memory_space=space)], out_specs=out_specs,
        scratch_shapes=scratch, compiler_params=pltpu.CompilerParams(vmem_limit_bytes=VMEM_LIMIT))(x)


def _peer(k, mx, my, mc):
    bits = k + 1
    return (1 - mx if bits & 4 else mx, 1 - my if bits & 2 else my, 1 - mc if bits & 1 else mc)


HBM_SPEC = pl.BlockSpec(memory_space=pltpu.HBM)
SEM_SPEC = pl.BlockSpec(memory_space=pltpu.SEMAPHORE)
DATAFLOW = pltpu.SideEffectType.DATAFLOW_SIDE_EFFECTING


def _exchange_copies(x_ref, land_ref, send_sems, recv_sems, scatter):
    mx, my, mc = _place()
    me = 4 * mx + 2 * my + mc
    copies = []
    for k in range(N_DEV - 1):
        px, py, pc = _peer(k, mx, my, mc)
        copies.append(pltpu.make_async_remote_copy(
            src_ref=x_ref.at[4 * px + 2 * py + pc] if scatter else x_ref,
            dst_ref=land_ref.at[k] if scatter else land_ref.at[me],
            send_sem=send_sems.at[k], recv_sem=recv_sems.at[k], device_id=(px, py, pc), device_id_type=MESH_ID))
    return copies


def _exchange_start(x, *, name, scatter):
    rows = x.shape[-2]
    land_shape = (N_DEV - 1, rows, LANE) if scatter else (N_DEV, rows, LANE)

    def body(x_ref, land_ref, send_sems, recv_sems, x_thru, land_thru, token):
        del x_thru, land_thru
        for cp in _exchange_copies(x_ref, land_ref, send_sems, recv_sems, scatter):
            cp.start()
        token[...] = jnp.zeros_like(token)

    sems = pltpu.SemaphoreType.DMA((N_DEV - 1,))
    return pl.pallas_call(
        body, name=name,
        out_shape=(sems, sems, pltpu.HBM(x.shape, x.dtype), pltpu.HBM(land_shape, x.dtype),
                   jax.ShapeDtypeStruct((8, LANE), F32)),
        in_specs=(HBM_SPEC, HBM_SPEC),
        out_specs=(SEM_SPEC, SEM_SPEC, HBM_SPEC, HBM_SPEC, pl.BlockSpec(memory_space=pltpu.VMEM)),
        input_output_aliases={0: 2, 1: 3}, compiler_params=pltpu.CompilerParams(has_side_effects=DATAFLOW))(
            pltpu.with_memory_space_constraint(x, pltpu.HBM),
            pltpu.with_memory_space_constraint(lax.empty(land_shape, x.dtype), pltpu.HBM))


def _exchange_wait(started, after, *, name, scatter):
    send_sems, recv_sems, x_thru, land_thru, _ = started

    def body(x_ref, land_ref, send_sems, recv_sems, after_ref, x_dead, got_ref):
        del after_ref, x_dead, got_ref
        for cp in _exchange_copies(x_ref, land_ref, send_sems, recv_sems, scatter):
            cp.wait_send()
            cp.wait_recv()

    return pl.pallas_call(
        body, name=name,
        out_shape=(pltpu.HBM(x_thru.shape, x_thru.dtype), pltpu.HBM(land_thru.shape, land_thru.dtype)),
        in_specs=(HBM_SPEC, HBM_SPEC, SEM_SPEC, SEM_SPEC, pl.BlockSpec(memory_space=pl.ANY)),
        out_specs=(HBM_SPEC, HBM_SPEC), input_output_aliases={0: 0, 1: 1},
        compiler_params=pltpu.CompilerParams(has_side_effects=DATAFLOW))(
            x_thru, land_thru, send_sems, recv_sems, after)[1]


def _pack_shards(group):
    return jnp.concatenate([s.astype(BF16).reshape(-1, LANE) for s, _ in group], axis=0)


def _unpack_gathered(flat, group):
    out, r0 = [], 0
    for s, axis in group:
        rows = s.size // LANE
        blk = flat[:, r0:r0 + rows].reshape((N_DEV,) + s.shape)
        blk = jnp.moveaxis(blk, 0, axis)
        shape = list(s.shape)
        shape[axis] *= N_DEV
        out.append(blk.reshape(shape))
        r0 += rows
    return out


def _pack_grads(grads, group):
    parts = []
    for g, (s, axis) in zip(grads, group):
        shape = list(s.shape)
        split = shape[:axis] + [N_DEV] + shape[axis:]
        blk = jnp.moveaxis(g.astype(BF16).reshape(split), axis, 0)
        parts.append(blk.reshape(N_DEV, -1, LANE))
    return jnp.concatenate(parts, axis=1)


def _unpack_local(flat, group):
    out, r0 = [], 0
    for s, _ in group:
        rows = s.size // LANE
        out.append(flat[r0:r0 + rows].reshape(s.shape))
        r0 += rows
    return out


def _gather_start(group, token, *, name):
    packed = _pack_shards(group) + token[0, 0].astype(BF16)
    return packed, _exchange_start(packed, name=name, scatter=False)


def _gather_finish(packed, started, after, group, me, *, name):
    land = _exchange_wait(started, after, name=name, scatter=False)
    land = lax.dynamic_update_index_in_dim(land, packed, me, 0)
    return _unpack_gathered(land, group)


def _reduce_start(grads, group, *, name):
    packed = _pack_grads(grads, group)
    return packed, _exchange_start(packed, name=name, scatter=True)


def _reduce_finish(packed, started, after, group, me, *, name):
    got = _exchange_wait(started, after, name=name + "_wait", scatter=True)
    own = lax.dynamic_index_in_dim(packed, me, 0, keepdims=False)
    return _unpack_local(_add_received(own, got, name=name + "_add"), group)


def _pad_rows(a, mult):
    r = (-a.shape[0]) % mult
    return a if r == 0 else jnp.concatenate([a, jnp.zeros((r,) + a.shape[1:], a.dtype)], axis=0)


def kernel(x, mem, mix_norm, mem_norm, w_mem_kv, w_out, hg_w_in, hg_lb, hg_onorm, gm_w_in, gm_ln_g, gm_ln_b, gm_ws, gm_bs, ffn_norm, w_ffn_in, w_ffn_out, final_norm, loss_target, m_mix_norm, m_mem_norm, m_w_mem_kv, m_w_out, m_hg_w_in, m_hg_lb, m_hg_onorm, m_gm_w_in, m_gm_ln_g, m_gm_ln_b, m_gm_ws, m_gm_bs, m_ffn_norm, m_w_ffn_in, m_w_ffn_out, m_final_norm, v_mix_norm, v_mem_norm, v_w_mem_kv, v_w_out, v_hg_w_in, v_hg_lb, v_hg_onorm, v_gm_w_in, v_gm_ln_g, v_gm_ln_b, v_gm_ws, v_gm_bs, v_ffn_norm, v_w_ffn_in, v_w_ffn_out, v_final_norm):
    mx, my, mc = _place()
    me = 4 * mx + 2 * my + mc
    xs = x[0]
    mems = mem[0]
    tgt = loss_target[0]

    grp_first = [(hg_w_in, 2)]
    grp_l0 = [(w_mem_kv, 1), (w_out, 1), (w_ffn_in[0:1], 2), (w_ffn_out[0:1], 1)]
    grp_l1 = [(gm_w_in, 2), (w_ffn_in[1:2], 2), (w_ffn_out[1:2], 1)]
    first, token = _all_gather(_pack_shards(grp_first), name="gather_first", in_vmem=False, with_token=True)
    W_hg = _unpack_gathered(first.reshape(N_DEV, -1, LANE), grp_first)[0][0]
    packed_l0, gather_l0 = _gather_start(grp_l0, token, name="gather_l0_start")
    packed_l1, gather_l1 = _gather_start(grp_l1, gather_l0[4], name="gather_l1_start")

    ln_local = _pad_rows(jnp.concatenate([gm_ln_g, gm_ln_b], axis=0), 8)
    ln_local = jnp.concatenate([ln_local, jnp.zeros((8, LANE - ln_local.shape[1]), F32)], axis=1)
    ln_all = _all_gather(ln_local, name="gather_ln", in_vmem=True).reshape(N_DEV, 8, LANE)
    ln_g = ln_all[:, 0, :D_TOK // N_DEV].reshape(1, D_TOK)
    ln_b = ln_all[:, 1, :D_TOK // N_DEV].reshape(1, D_TOK)

    lb_soft = jax.nn.softmax(hg_lb, axis=0)
    lb0 = lb_soft[0:1]
    bsb = jnp.broadcast_to(gm_bs[0][:, :, None], (N_HEADS, GM_CHUNK, GM_CHUNK))
    ws = gm_ws[0]

    def ffn_fwd(xin, i):
        hf = _rms_fwd(xin, ffn_norm[i:i + 1], name=f"ffn_norm{i}")
        gu, act = _ffn_in(hf, W_fi[i], name=f"ffn_in{i}")
        xout = _matmul(act, W_fo[i], res=xin, name=f"ffn_out{i}")
        return hf, gu, act, xout

    h0 = _rms_fwd(xs, mix_norm[0:1], name="mix_norm0", dep=gather_l1[4])
    p0 = _matmul(h0, W_hg, name="hg_in")
    heads0, o0, states = _hgrn2_fwd(p0, lb0, hg_onorm, name="hgrn2_fwd")

    W_kv, W_out, W_fi0, W_fo0 = _gather_finish(packed_l0, gather_l0, o0, grp_l0, me, name="gather_l0_wait")
    W_fi, W_fo = [W_fi0[0]], [W_fo0[0]]
    mem_n, kv = [], []
    for i in range(2):
        mn = _rms_fwd(mems, mem_norm[i:i + 1], name=f"mem_norm{i}")
        mem_n.append(mn)
        kv.append(_matmul(mn, W_kv[i], name=f"mem_kv{i}"))

    heads0 = _attn_fwd(p0, 4 * D_TOK // D_MEM, kv[0], heads0, name="attn_fwd0")
    x1 = _matmul(heads0, W_out[0], res=xs, name="out_proj0")
    hf0, gu0, act0, x2 = ffn_fwd(x1, 0)

    W_gm, W_fi1, W_fo1 = _gather_finish(packed_l1, gather_l1, x2, grp_l1, me, name="gather_l1_wait")
    W_gm = W_gm[0]
    W_fi.append(W_fi1[0])
    W_fo.append(W_fo1[0])
    h1 = _rms_fwd(x2, mix_norm[1:2], name="mix_norm1")
    p1 = _matmul(h1, W_gm, name="gm_in")
    heads1 = _gmlp_fwd(p1, ln_g, ln_b, ws, bsb, name="gmlp_fwd")
    heads1 = _attn_fwd(p1, 2 * D_TOK // D_MEM, kv[1], heads1, name="attn_fwd1")
    x3 = _matmul(heads1, W_out[1], res=x2, name="out_proj1")
    hf1, gu1, act1, x4 = ffn_fwd(x3, 1)

    dx, g_final, loss_part = _final_loss(x4, final_norm.reshape(1, D_MODEL), tgt, name="final_loss")

    def ffn_bwd(dx, xin, hf, gu, act, i, dep=None):
        dgu = _ffn_out_dx(dx, W_fo[i], gu, name=f"ffn_out_dx{i}")
        g_wfo = _matmul(act, dx, ta=True, name=f"ffn_out_dw{i}", dep=dep)
        g_wfi = _matmul(hf, dgu, ta=True, b_halves=True, name=f"ffn_in_dw{i}")
        dhf = _matmul(dgu, W_fi[i], tb=True, a_halves=True, name=f"ffn_in_dx{i}")
        dx, g_norm = _rms_bwd(xin, ffn_norm[i:i + 1], dhf, dx, name=f"ffn_norm_bwd{i}")
        return dx, g_wfi, g_wfo, g_norm

    def mem_bwd(dkv, i):
        g_wkv = _matmul(mem_n[i], dkv, ta=True, name=f"mem_kv_dw{i}")
        dmn = _matmul(dkv, W_kv[i], tb=True, name=f"mem_kv_dx{i}")
        _, g_norm = _rms_bwd(mems, mem_norm[i:i + 1], dmn, jnp.zeros_like(mems), name=f"mem_norm_bwd{i}")
        return g_wkv, g_norm

    dx, g_wfi1, g_wfo1, g_ffn1 = ffn_bwd(dx, x3, hf1, gu1, act1, 1)
    dheads = _matmul(dx, W_out[1], tb=True, name="out_proj_dx1")
    g_wout1 = _matmul(heads1, dx, ta=True, name="out_proj_dw1")
    dp, g_ws, g_bs, g_lng, g_lnb = _gmlp_bwd(p1, ln_g, ln_b, ws, bsb, dheads, name="gmlp_bwd")
    dp, dk, dv = _attn_bwd(p1, 2 * D_TOK // D_MEM, kv[1], dheads, dp, name="attn_bwd1")
    g_wkv1, g_mem1 = mem_bwd(jnp.concatenate([dk, dv], axis=1), 1)
    g_wgm = _matmul(h1, dp, ta=True, name="gm_in_dw")
    dh = _matmul(dp, W_gm, tb=True, name="gm_in_dx")
    dx, g_mix1 = _rms_bwd(x2, mix_norm[1:2], dh, dx, name="mix_norm_bwd1")
    red_l1 = [(w_mem_kv[1:2], 1), (w_out[1:2], 1), (gm_w_in, 2), (w_ffn_in[1:2], 2), (w_ffn_out[1:2], 1)]
    packed_r1, reduce_l1 = _reduce_start([g_wkv1[None], g_wout1[None], g_wgm[None], g_wfi1[None], g_wfo1[None]],
                                         red_l1, name="reduce_l1_start")

    dx, g_wfi0, g_wfo0, g_ffn0 = ffn_bwd(dx, x1, hf0, gu0, act0, 0, dep=reduce_l1[4])
    red_ffn0 = [(w_ffn_in[0:1], 2), (w_ffn_out[0:1], 1)]
    packed_r2, reduce_ffn0 = _reduce_start([g_wfi0[None], g_wfo0[None]], red_ffn0, name="reduce_ffn0_start")
    dheads = _matmul(dx, W_out[0], tb=True, name="out_proj_dx0")
    g_wout0 = _matmul(heads0, dx, ta=True, name="out_proj_dw0", dep=reduce_ffn0[4])
    dp, g_lb0, g_onorm = _hgrn2_bwd(p0, lb0, hg_onorm, o0, states, dheads, name="hgrn2_bwd")
    dp, dk, dv = _attn_bwd(p0, 4 * D_TOK // D_MEM, kv[0], dheads, dp, name="attn_bwd0")
    g_wkv0, g_mem0 = mem_bwd(jnp.concatenate([dk, dv], axis=1), 0)
    g_whg = _matmul(h0, dp, ta=True, name="hg_in_dw")
    red_mix0 = [(w_mem_kv[0:1], 1), (w_out[0:1], 1), (hg_w_in, 2)]
    packed_r3, reduce_mix0 = _reduce_start([g_wkv0[None], g_wout0[None], g_whg[None]], red_mix0,
                                           name="reduce_mix0_start")
    dh = _matmul(dp, W_hg, tb=True, name="hg_in_dx", dep=reduce_mix0[4])
    grad_x, g_mix0 = _rms_bwd(xs, mix_norm[0:1], dh, dx, name="mix_norm_bwd0")

    g_kv1, g_out1, g_gm, g_fi1, g_fo1 = _reduce_finish(packed_r1, reduce_l1, grad_x, red_l1, me, name="reduce_l1")
    g_fi0, g_fo0 = _reduce_finish(packed_r2, reduce_ffn0, g_kv1, red_ffn0, me, name="reduce_ffn0")
    g_kv0, g_out0, g_hg = _reduce_finish(packed_r3, reduce_mix0, g_fi0, red_mix0, me, name="reduce_mix0")
    g_shards = [jnp.concatenate([g_kv0, g_kv1], axis=0), jnp.concatenate([g_out0, g_out1], axis=0), g_hg, g_gm,
                jnp.concatenate([g_fi0, g_fi1], axis=0), jnp.concatenate([g_fo0, g_fo1], axis=0)]

    small = [loss_part, jnp.concatenate([g_mix0, g_mix1], axis=1), jnp.concatenate([g_mem0, g_mem1], axis=1),
             g_lb0, g_onorm, g_lng, g_lnb, g_ws.reshape(1, -1), g_bs.reshape(1, -1),
             jnp.concatenate([g_ffn0, g_ffn1], axis=1), g_final]
    sizes = [t.shape[1] for t in small]
    small_rows = _pad_rows(jnp.concatenate(small, axis=1).reshape(-1, LANE), 8)
    red = _all_gather(small_rows, name="reduce_small", in_vmem=True, reduce_sum=True).reshape(-1)
    pieces, off = [], 0
    for n in sizes:
        pieces.append(red[off:off + n])
        off += n
    loss = pieces[0][0]
    g_mix_norm = pieces[1].reshape(2, D_MODEL)
    g_mem_norm = pieces[2].reshape(2, D_MODEL)
    g_hg_lb = pieces[3][None, :] * lb0 * (jnp.eye(3, dtype=F32)[:, 0:1] - lb_soft)
    g_hg_onorm = pieces[4].reshape(1, D_TOK)
    width = D_TOK // N_DEV
    g_gm_ln_g = lax.dynamic_slice(pieces[5], (me * width,), (width,)).reshape(1, width)
    g_gm_ln_b = lax.dynamic_slice(pieces[6], (me * width,), (width,)).reshape(1, width)
    g_gm_ws = pieces[7].reshape(gm_ws.shape)
    g_gm_bs = pieces[8].reshape(gm_bs.shape)
    g_ffn_norm = pieces[9].reshape(2, D_MODEL)
    g_final_norm = pieces[10]

    grads = [g_mix_norm, g_mem_norm, g_shards[0], g_shards[1], g_shards[2], g_hg_lb, g_hg_onorm, g_shards[3],
             g_gm_ln_g, g_gm_ln_b, g_gm_ws, g_gm_bs, g_ffn_norm, g_shards[4], g_shards[5], g_final_norm]
    weights = [mix_norm, mem_norm, w_mem_kv, w_out, hg_w_in, hg_lb, hg_onorm, gm_w_in, gm_ln_g, gm_ln_b, gm_ws, gm_bs,
               ffn_norm, w_ffn_in, w_ffn_out, final_norm]
    ms = [m_mix_norm, m_mem_norm, m_w_mem_kv, m_w_out, m_hg_w_in, m_hg_lb, m_hg_onorm, m_gm_w_in, m_gm_ln_g,
          m_gm_ln_b, m_gm_ws, m_gm_bs, m_ffn_norm, m_w_ffn_in, m_w_ffn_out, m_final_norm]
    vs = [v_mix_norm, v_mem_norm, v_w_mem_kv, v_w_out, v_hg_w_in, v_hg_lb, v_hg_onorm, v_gm_w_in, v_gm_ln_g,
          v_gm_ln_b, v_gm_ws, v_gm_bs, v_ffn_norm, v_w_ffn_in, v_w_ffn_out, v_final_norm]
    deltas, new_m, new_v = [], [], []
    for n, (w, g, m, v) in enumerate(zip(weights, grads, ms, vs)):
        if w.ndim == 1:
            d, nm, nv = _adamw(w[None], g.reshape(1, -1), m[None], v[None], name=f"adamw{n}")
            d, nm, nv = d[0], nm[0], nv[0]
        else:
            d, nm, nv = _adamw(w, g.reshape(w.shape), m, v, name=f"adamw{n}")
        deltas.append(d)
        new_m.append(nm)
        new_v.append(nv)
    grads = [g.reshape(w.shape) for g, w in zip(grads, weights)]
    return (loss, grad_x[None], *grads, *deltas, *new_m, *new_v)
```

```python
import functools

import jax
import jax.numpy as jnp
from jax import lax
from jax.experimental import pallas as pl
from jax.experimental.pallas import tpu as pltpu

F32 = jnp.float32
BF16 = jnp.bfloat16
MXU_DTYPE = jnp.bfloat16
MESH_ID = pl.DeviceIdType.MESH

N_DEV = 8
EPS = 1e-6
D_MODEL = 1024
D_TOK = 768
D_MEM = 256
N_HEADS = 6
HEAD = 128
MEM_HEADS = 4
MEM_HDIM = 64
GM_CHUNK = 128
D_FF = 2816
HG_SUB = 16
HG_IN = 4 * D_TOK + D_MEM
GM_IN = 2 * D_TOK + D_MEM
LANE = 128

ADAM_LR = 0.001
ADAM_B1 = 0.9
ADAM_B2 = 0.999
ADAM_EPS = 1e-08
ADAM_WD = 0.01
ADAM_STEP = 10

VMEM_LIMIT = 48 * 2 ** 20


def _params(sem=None):
    return pltpu.CompilerParams(dimension_semantics=sem, vmem_limit_bytes=VMEM_LIMIT)


def _tile(n, cap, q=LANE):
    if n <= cap:
        return n
    best = None
    for t in range(q, cap + 1, q):
        if n % t == 0:
            best = t
    assert best is not None, (n, cap, q)
    return best


def _sigmoid(x):
    return 1.0 / (1.0 + jnp.exp(-x))


def _gelu(x):
    return 0.5 * x * (1.0 + lax.erf(x * 0.7071067811865476))


def _gelu_grad(x):
    return 0.5 * (1.0 + lax.erf(x * 0.7071067811865476)) + x * jnp.exp(-0.5 * x * x) * 0.3989422804014327


def _matmul(a, b, *, name, ta=False, tb=False, res=None, out_dtype=F32, a_halves=False, b_halves=False, dep=None):
    if a_halves and ta:
        K, M = a.shape[1], 2 * a.shape[2]
    elif a_halves:
        M, K = a.shape[1], 2 * a.shape[2]
    else:
        K, M = a.shape if ta else a.shape[::-1]
    if b_halves:
        assert not tb and b.shape[1] == K
        N = 2 * b.shape[2]
    else:
        N = b.shape[0] if tb else b.shape[1]
        assert (b.shape[1] if tb else b.shape[0]) == K
    tm = _tile(M // 2 if (a_halves and ta) else M, 1664 if ta else 1024)
    tn = _tile(N // 2 if b_halves else N, 1792)
    tk = _tile(K // 2 if (a_halves and not ta) else K, 1024 if ta else 1664)
    nk = K // tk
    dims = (((0 if ta else 1,), (1 if tb else 0,)), ((), ()))

    def body(*refs):
        a_ref, b_ref = refs[:2]
        r_ref = refs[2] if res is not None else None
        o_ref, acc = refs[-2:]
        k = pl.program_id(2)

        @pl.when(k == 0)
        def _():
            acc[...] = jnp.zeros_like(acc)

        acc[...] += lax.dot_general(a_ref[...].astype(MXU_DTYPE), b_ref[...].astype(MXU_DTYPE), dims,
                                    preferred_element_type=F32)

        @pl.when(k == nk - 1)
        def _():
            r = acc[...]
            if res is not None:
                r = r + r_ref[...].astype(F32)
            o_ref[...] = r.astype(out_dtype)

    if a_halves and ta:
        mh = M // 2 // tm
        a_spec = pl.BlockSpec((None, tk, tm), lambda i, j, k: (i // mh, k, i % mh))
    elif a_halves:
        kh = nk // 2
        a_spec = pl.BlockSpec((None, tm, tk), lambda i, j, k: (k // kh, i, k % kh))
    elif ta:
        a_spec = pl.BlockSpec((tk, tm), lambda i, j, k: (k, i))
    else:
        a_spec = pl.BlockSpec((tm, tk), lambda i, j, k: (i, k))
    if b_halves:
        nh = N // 2 // tn
        b_spec = pl.BlockSpec((None, tk, tn), lambda i, j, k: (j // nh, k, j % nh))
    elif tb:
        b_spec = pl.BlockSpec((tn, tk), lambda i, j, k: (j, k))
    else:
        b_spec = pl.BlockSpec((tk, tn), lambda i, j, k: (k, j))
    o_spec = pl.BlockSpec((tm, tn), lambda i, j, k: (i, j))
    in_specs = [a_spec, b_spec] + ([o_spec] if res is not None else [])
    args = (a, b) + ((res,) if res is not None else ())
    if dep is not None:
        in_specs.append(pl.BlockSpec(memory_space=pl.ANY))
        args += (dep,)
    return pl.pallas_call(
        body, name=name, grid=(M // tm, N // tn, nk), in_specs=in_specs, out_specs=o_spec,
        out_shape=jax.ShapeDtypeStruct((M, N), out_dtype), scratch_shapes=[pltpu.VMEM((tm, tn), F32)],
        compiler_params=_params(("parallel", "parallel", "arbitrary")))(*args)


def _ffn_in(hf, wt, *, name):
    S, K = hf.shape
    tm = _tile(S, 512)
    tn = _tile(D_FF, 1408)
    nh = D_FF // tn
    nt = (((1,), (1,)), ((), ()))

    def body(a_ref, bg_ref, bu_ref, gu_ref, act_ref):
        av = a_ref[...].astype(MXU_DTYPE)
        gate = lax.dot_general(av, bg_ref[...].astype(MXU_DTYPE), nt, preferred_element_type=F32)
        up = lax.dot_general(av, bu_ref[...].astype(MXU_DTYPE), nt, preferred_element_type=F32)
        gu_ref[0] = gate.astype(gu_ref.dtype)
        gu_ref[1] = up.astype(gu_ref.dtype)
        act_ref[...] = (gate * _sigmoid(gate) * up).astype(act_ref.dtype)

    return pl.pallas_call(
        body, name=name, grid=(S // tm, nh),
        in_specs=[pl.BlockSpec((tm, K), lambda i, j: (i, 0)), pl.BlockSpec((tn, K), lambda i, j: (j, 0)),
                  pl.BlockSpec((tn, K), lambda i, j: (j + nh, 0))],
        out_specs=[pl.BlockSpec((2, tm, tn), lambda i, j: (0, i, j)), pl.BlockSpec((tm, tn), lambda i, j: (i, j))],
        out_shape=[jax.ShapeDtypeStruct((2, S, D_FF), BF16), jax.ShapeDtypeStruct((S, D_FF), BF16)],
        compiler_params=_params(("parallel", "parallel")))(hf, wt, wt)


def _ffn_out_dx(dx, w, gu, dep, *, name):
    S, K = dx.shape
    tm = _tile(S, 512)
    tn = _tile(D_FF, 1408)

    def body(a_ref, b_ref, gu_ref, dep_ref, o_ref):
        del dep_ref
        da = lax.dot_general(a_ref[...].astype(MXU_DTYPE), b_ref[...].astype(MXU_DTYPE), (((1,), (1,)), ((), ())),
                             preferred_element_type=F32)
        gate = gu_ref[0].astype(F32)
        up = gu_ref[1].astype(F32)
        sg = _sigmoid(gate)
        o_ref[0] = (da * up * sg * (1.0 + gate * (1.0 - sg))).astype(o_ref.dtype)
        o_ref[1] = (da * gate * sg).astype(o_ref.dtype)

    halves = pl.BlockSpec((2, tm, tn), lambda i, j: (0, i, j))
    return pl.pallas_call(
        body, name=name, grid=(S // tm, D_FF // tn),
        in_specs=[pl.BlockSpec((tm, K), lambda i, j: (i, 0)), pl.BlockSpec((tn, K), lambda i, j: (j, 0)), halves,
                  pl.BlockSpec(memory_space=pl.ANY)],
        out_specs=halves, out_shape=jax.ShapeDtypeStruct((2, S, D_FF), BF16),
        compiler_params=_params(("parallel", "parallel")))(dx, w, gu, dep)


def _rms_fwd(x, g, *, name, dep=None):
    R, Dm = x.shape
    tr = _tile(R, 512, 8)

    def body(x_ref, g_ref, *rest):
        o_ref = rest[-1]
        xv = x_ref[...]
        r = lax.rsqrt(jnp.mean(xv * xv, axis=-1, keepdims=True) + EPS)
        o_ref[...] = (xv * r * g_ref[...]).astype(o_ref.dtype)

    in_specs = [pl.BlockSpec((tr, Dm), lambda i: (i, 0)), pl.BlockSpec((1, Dm), lambda i: (0, 0))]
    args = (x, g)
    if dep is not None:
        in_specs.append(pl.BlockSpec(memory_space=pl.ANY))
        args += (dep,)
    return pl.pallas_call(
        body, name=name, grid=(R // tr,), in_specs=in_specs,
        out_specs=pl.BlockSpec((tr, Dm), lambda i: (i, 0)), out_shape=jax.ShapeDtypeStruct((R, Dm), BF16),
        compiler_params=_params(("parallel",)))(*args)


def _rms_bwd(x, g, dh, dres, *, name):
    R, Dm = x.shape
    tr = _tile(R, 256, 8)

    def body(x_ref, g_ref, dh_ref, dres_ref, dx_ref, dg_ref):
        @pl.when(pl.program_id(0) == 0)
        def _():
            dg_ref[...] = jnp.zeros_like(dg_ref)

        xv = x_ref[...]
        r = lax.rsqrt(jnp.mean(xv * xv, axis=-1, keepdims=True) + EPS)
        xh = xv * r
        dhv = dh_ref[...].astype(F32)
        dg_ref[...] += jnp.sum(dhv * xh, axis=0, keepdims=True)
        u = dhv * g_ref[...]
        dx = r * (u - xh * jnp.mean(u * xh, axis=-1, keepdims=True))
        dx_ref[...] = dres_ref[...] + dx

    row = pl.BlockSpec((tr, Dm), lambda i: (i, 0))
    vec = pl.BlockSpec((1, Dm), lambda i: (0, 0))
    return pl.pallas_call(
        body, name=name, grid=(R // tr,), in_specs=[row, vec, row, row], out_specs=[row, vec],
        out_shape=[jax.ShapeDtypeStruct((R, Dm), F32), jax.ShapeDtypeStruct((1, Dm), F32)],
        compiler_params=_params(("arbitrary",)))(x, g, dh, dres)


def _final_loss(x, g, tgt, *, name):
    R, Dm = x.shape
    tr = _tile(R, 256, 8)

    def body(x_ref, g_ref, t_ref, dx_ref, dg_ref, loss_ref):
        @pl.when(pl.program_id(0) == 0)
        def _():
            dg_ref[...] = jnp.zeros_like(dg_ref)
            loss_ref[...] = jnp.zeros_like(loss_ref)

        xv = x_ref[...]
        r = lax.rsqrt(jnp.mean(xv * xv, axis=-1, keepdims=True) + EPS)
        xh = xv * r
        gv = g_ref[...]
        err = xh * gv - t_ref[...]
        part = 0.5 * jnp.sum(jnp.mean(err * err, axis=-1, keepdims=True), axis=0, keepdims=True)
        loss_ref[...] += jnp.broadcast_to(part, loss_ref.shape)
        dy = err * (1.0 / Dm)
        dg_ref[...] += jnp.sum(dy * xh, axis=0, keepdims=True)
        u = dy * gv
        dx_ref[...] = r * (u - xh * jnp.mean(u * xh, axis=-1, keepdims=True))

    row = pl.BlockSpec((tr, Dm), lambda i: (i, 0))
    vec = pl.BlockSpec((1, Dm), lambda i: (0, 0))
    one = pl.BlockSpec((1, LANE), lambda i: (0, 0))
    return pl.pallas_call(
        body, name=name, grid=(R // tr,), in_specs=[row, vec, row], out_specs=[row, vec, one],
        out_shape=[jax.ShapeDtypeStruct((R, Dm), F32), jax.ShapeDtypeStruct((1, Dm), F32),
                   jax.ShapeDtypeStruct((1, LANE), F32)],
        compiler_params=_params(("arbitrary",)))(x, g, tgt)


def _head_mask(h):
    lane = lax.broadcasted_iota(jnp.int32, (1, D_MEM), 1)
    return (lane >= h * MEM_HDIM) & (lane < (h + 1) * MEM_HDIM)


def _attn_probs(qv, k_mx, mask):
    s = lax.dot_general(jnp.where(mask, qv, 0.0).astype(MXU_DTYPE), k_mx, (((1,), (1,)), ((), ())),
                        preferred_element_type=F32) * (MEM_HDIM ** -0.5)
    e = jnp.exp(s - jnp.max(s, axis=-1, keepdims=True))
    return e / jnp.sum(e, axis=-1, keepdims=True)


def _attn_fwd(p, qcol, kv, heads, *, name):
    S = p.shape[0]
    M = kv.shape[0]
    ts = _tile(S, 512, 8)

    def body(q_ref, k_ref, v_ref, heads_in, o_ref):
        del heads_in
        qv = q_ref[...]
        kx = k_ref[...].astype(MXU_DTYPE)
        vv = v_ref[...]
        out = jnp.zeros((ts, D_MEM), F32)
        for h in range(MEM_HEADS):
            mask = _head_mask(h)
            pr = _attn_probs(qv, kx, mask)
            out = out + jnp.dot(pr.astype(MXU_DTYPE), jnp.where(mask, vv, 0.0).astype(MXU_DTYPE),
                                preferred_element_type=F32)
        o_ref[...] = out.astype(o_ref.dtype)

    return pl.pallas_call(
        body, name=name, grid=(S // ts,),
        in_specs=[pl.BlockSpec((ts, D_MEM), lambda i: (i, qcol)), pl.BlockSpec((M, D_MEM), lambda i: (0, 0)),
                  pl.BlockSpec((M, D_MEM), lambda i: (0, 1)), pl.BlockSpec(memory_space=pl.ANY)],
        out_specs=pl.BlockSpec((ts, D_MEM), lambda i: (i, D_TOK // D_MEM)),
        out_shape=jax.ShapeDtypeStruct(heads.shape, heads.dtype), input_output_aliases={3: 0},
        compiler_params=_params(("parallel",)))(p, kv, kv, heads)


def _attn_bwd(p, qcol, kv, dheads, dp, *, name):
    S = p.shape[0]
    M = kv.shape[0]
    ts = _tile(S, 512, 8)
    scale = MEM_HDIM ** -0.5

    def body(q_ref, k_ref, v_ref, do_ref, dp_in, dq_ref, dk_ref, dv_ref):
        del dp_in

        @pl.when(pl.program_id(0) == 0)
        def _():
            dk_ref[...] = jnp.zeros_like(dk_ref)
            dv_ref[...] = jnp.zeros_like(dv_ref)

        qv = q_ref[...]
        kv_ = k_ref[...]
        kx = kv_.astype(MXU_DTYPE)
        vv = v_ref[...]
        dox = do_ref[...].astype(MXU_DTYPE)
        qx = qv.astype(MXU_DTYPE)
        dq = jnp.zeros((ts, D_MEM), F32)
        for h in range(MEM_HEADS):
            mask = _head_mask(h)
            pr = _attn_probs(qv, kx, mask)
            vh = jnp.where(mask, vv, 0.0).astype(MXU_DTYPE)
            dpr = lax.dot_general(dox, vh, (((1,), (1,)), ((), ())), preferred_element_type=F32)
            ds = (pr * (dpr - jnp.sum(dpr * pr, axis=-1, keepdims=True)) * scale).astype(MXU_DTYPE)
            dq = dq + jnp.dot(ds, jnp.where(mask, kv_, 0.0).astype(MXU_DTYPE), preferred_element_type=F32)
            dk_h = lax.dot_general(ds, qx, (((0,), (0,)), ((), ())), preferred_element_type=F32)
            dv_h = lax.dot_general(pr.astype(MXU_DTYPE), dox, (((0,), (0,)), ((), ())), preferred_element_type=F32)
            dk_ref[...] += jnp.where(mask, dk_h, 0.0)
            dv_ref[...] += jnp.where(mask, dv_h, 0.0)
        dq_ref[...] = dq.astype(dq_ref.dtype)

    return pl.pallas_call(
        body, name=name, grid=(S // ts,),
        in_specs=[pl.BlockSpec((ts, D_MEM), lambda i: (i, qcol)), pl.BlockSpec((M, D_MEM), lambda i: (0, 0)),
                  pl.BlockSpec((M, D_MEM), lambda i: (0, 1)),
                  pl.BlockSpec((ts, D_MEM), lambda i: (i, D_TOK // D_MEM)), pl.BlockSpec(memory_space=pl.ANY)],
        out_specs=[pl.BlockSpec((ts, D_MEM), lambda i: (i, qcol)), pl.BlockSpec((M, D_MEM), lambda i: (0, 0)),
                   pl.BlockSpec((M, D_MEM), lambda i: (0, 0))],
        out_shape=[jax.ShapeDtypeStruct(dp.shape, dp.dtype), jax.ShapeDtypeStruct((M, D_MEM), F32),
                   jax.ShapeDtypeStruct((M, D_MEM), F32)],
        input_output_aliases={4: 0}, compiler_params=_params(("arbitrary",)))(p, kv, kv, dheads, dp)


def _gm_forward_parts(u_ref, v_ref, lng_ref, lnb_ref, w_ref, bsb_ref):
    zu = _gelu(u_ref[...])
    zv = _gelu(v_ref[...])
    mu = jnp.mean(zv, axis=-1, keepdims=True)
    cen = zv - mu
    rs = lax.rsqrt(jnp.mean(cen * cen, axis=-1, keepdims=True) + EPS)
    vh = cen * rs
    vn = vh * lng_ref[...] + lnb_ref[...]
    row = lax.broadcasted_iota(jnp.int32, (GM_CHUNK, GM_CHUNK), 0)
    col = lax.broadcasted_iota(jnp.int32, (GM_CHUNK, GM_CHUNK), 1)
    tril = row >= col
    wm = [jnp.where(tril, w_ref[g], 0.0).astype(MXU_DTYPE) for g in range(N_HEADS)]
    vnx = [vn[:, g * HEAD:(g + 1) * HEAD].astype(MXU_DTYPE) for g in range(N_HEADS)]
    sv = [jnp.dot(wm[g], vnx[g], preferred_element_type=F32) + bsb_ref[g] for g in range(N_HEADS)]
    return zu, vh, rs, wm, vnx, sv, tril


def _gmlp_fwd(p, lng, lnb, ws, bsb, *, name):
    S = p.shape[0]

    def body(u_ref, v_ref, lng_ref, lnb_ref, w_ref, bsb_ref, o_ref):
        zu, _, _, _, _, sv, _ = _gm_forward_parts(u_ref, v_ref, lng_ref, lnb_ref, w_ref, bsb_ref)
        for g in range(N_HEADS):
            o_ref[:, g * HEAD:(g + 1) * HEAD] = (zu[:, g * HEAD:(g + 1) * HEAD] * sv[g]).astype(o_ref.dtype)

    blk = lambda c: pl.BlockSpec((GM_CHUNK, D_TOK), lambda i: (i, c))
    vec = pl.BlockSpec((1, D_TOK), lambda i: (0, 0))
    cube = pl.BlockSpec((N_HEADS, GM_CHUNK, GM_CHUNK), lambda i: (0, 0, 0))
    return pl.pallas_call(
        body, name=name, grid=(S // GM_CHUNK,), in_specs=[blk(0), blk(1), vec, vec, cube, cube],
        out_specs=blk(0), out_shape=jax.ShapeDtypeStruct((S, D_MODEL), BF16),
        compiler_params=_params(("parallel",)))(p, p, lng, lnb, ws, bsb)


def _gmlp_bwd(p, lng, lnb, ws, bsb, dheads, *, name):
    S = p.shape[0]

    def body(u_ref, v_ref, lng_ref, lnb_ref, w_ref, bsb_ref, dt_ref, dp_ref, dw_ref, dbs_ref, dlg_ref, dlb_ref):
        @pl.when(pl.program_id(0) == 0)
        def _():
            dw_ref[...] = jnp.zeros_like(dw_ref)
            dbs_ref[...] = jnp.zeros_like(dbs_ref)
            dlg_ref[...] = jnp.zeros_like(dlg_ref)
            dlb_ref[...] = jnp.zeros_like(dlb_ref)

        zu, vh, rs, wm, vnx, sv, tril = _gm_forward_parts(u_ref, v_ref, lng_ref, lnb_ref, w_ref, bsb_ref)
        dt = dt_ref[...].astype(F32)
        dvn_parts = []
        for g in range(N_HEADS):
            sl = slice(g * HEAD, (g + 1) * HEAD)
            dsv = dt[:, sl] * zu[:, sl]
            dp_ref[:, sl] = (dt[:, sl] * sv[g] * _gelu_grad(u_ref[:, sl])).astype(dp_ref.dtype)
            dsx = dsv.astype(MXU_DTYPE)
            dw = lax.dot_general(dsx, vnx[g], (((1,), (1,)), ((), ())), preferred_element_type=F32)
            dw_ref[g] += jnp.where(tril, dw, 0.0)
            dbs_ref[g] += jnp.sum(dsv, axis=-1, keepdims=True)
            dvn_parts.append(lax.dot_general(wm[g], dsx, (((0,), (0,)), ((), ())), preferred_element_type=F32))
        dvn = jnp.concatenate(dvn_parts, axis=-1)
        dlg_ref[...] += jnp.sum(dvn * vh, axis=0, keepdims=True)
        dlb_ref[...] += jnp.sum(dvn, axis=0, keepdims=True)
        dvh = dvn * lng_ref[...]
        dzv = rs * (dvh - jnp.mean(dvh, axis=-1, keepdims=True) - vh * jnp.mean(dvh * vh, axis=-1, keepdims=True))
        dp_ref[:, D_TOK:] = (dzv * _gelu_grad(v_ref[...])).astype(dp_ref.dtype)

    blk = lambda c: pl.BlockSpec((GM_CHUNK, D_TOK), lambda i: (i, c))
    vec = pl.BlockSpec((1, D_TOK), lambda i: (0, 0))
    cube = pl.BlockSpec((N_HEADS, GM_CHUNK, GM_CHUNK), lambda i: (0, 0, 0))
    col = pl.BlockSpec((N_HEADS, GM_CHUNK, 1), lambda i: (0, 0, 0))
    return pl.pallas_call(
        body, name=name, grid=(S // GM_CHUNK,), in_specs=[blk(0), blk(1), vec, vec, cube, cube, blk(0)],
        out_specs=[pl.BlockSpec((GM_CHUNK, 2 * D_TOK), lambda i: (i, 0)), cube, col, vec, vec],
        out_shape=[jax.ShapeDtypeStruct((S, GM_IN), BF16), jax.ShapeDtypeStruct((N_HEADS, GM_CHUNK, GM_CHUNK), F32),
                   jax.ShapeDtypeStruct((N_HEADS, GM_CHUNK, 1), F32), jax.ShapeDtypeStruct((1, D_TOK), F32),
                   jax.ShapeDtypeStruct((1, D_TOK), F32)],
        compiler_params=_params(("arbitrary",)))(p, p, lng, lnb, ws, bsb, dheads)


def _tri(n, upper):
    r = lax.broadcasted_iota(jnp.int32, (n, n), 0)
    c = lax.broadcasted_iota(jnp.int32, (n, n), 1)
    return jnp.where((r <= c) if upper else (r >= c), 1.0, 0.0).astype(F32)


def _hg_gates(fz, lb):
    sg = _sigmoid(fz)
    f = lb + (1.0 - lb) * sg
    kk = (1.0 - lb) * (1.0 - sg)
    return sg, f, jnp.log(f), kk


def _hgrn2_fwd(p, lb, onorm, *, name):
    S = p.shape[0]
    C = HG_SUB
    tb = _tile(S, 256, C)
    nsub = tb // C

    def body(q_ref, fz_ref, v_ref, g_ref, lb_ref, on_ref, tok_ref, o_ref, st_ref, state, bsc, ksc, vsc):
        @pl.when(pl.program_id(0) == 0)
        def _():
            state[...] = jnp.zeros_like(state)

        lower = _tri(C, False)
        tt = lax.broadcasted_iota(jnp.int32, (C, HEAD), 0)

        def sub(c, carry):
            rows = pl.ds(pl.multiple_of(c * C, C), C)
            for h in range(N_HEADS):
                cols = slice(h * HEAD, (h + 1) * HEAD)
                qv = q_ref[rows, cols]
                vv = v_ref[rows, cols]
                gv = g_ref[rows, cols]
                _, _, lg, kk = _hg_gates(fz_ref[rows, cols], lb_ref[:, cols])
                b = jnp.dot(lower, lg, precision=lax.Precision.HIGHEST, preferred_element_type=F32)
                st0 = state[h]
                st_ref[c, h] = st0
                inter = lax.dot_general((qv * jnp.exp(b)).astype(MXU_DTYPE), st0.astype(MXU_DTYPE),
                                        (((1,), (1,)), ((), ())), preferred_element_type=F32)
                bsc[h] = b
                ksc[h] = kk
                vsc[h] = vv
                intra = jnp.zeros((C, HEAD), F32)
                for s in range(C):
                    dec = jnp.where(tt >= s, jnp.exp(jnp.minimum(b - bsc[h, pl.ds(s, 1), :], 0.0)), 0.0)
                    a_s = jnp.sum(qv * ksc[h, pl.ds(s, 1), :] * dec, axis=-1, keepdims=True)
                    intra = intra + a_s * vsc[h, pl.ds(s, 1), :]
                o = inter + intra
                b_last = bsc[h, pl.ds(C - 1, 1), :]
                ke = kk * jnp.exp(b_last - b)
                state[h] = st0 * jnp.exp(b_last) + lax.dot_general(
                    vv.astype(MXU_DTYPE), ke.astype(MXU_DTYPE), (((0,), (0,)), ((), ())),
                    preferred_element_type=F32)
                o_ref[rows, cols] = o
                n = o * lax.rsqrt(jnp.mean(o * o, axis=-1, keepdims=True) + EPS)
                tok_ref[rows, cols] = (n * (gv * _sigmoid(gv)) * on_ref[:, cols]).astype(tok_ref.dtype)
            return carry

        lax.fori_loop(0, nsub, sub, 0)

    blk = lambda c: pl.BlockSpec((tb, D_TOK), lambda i, c=c: (i, c))
    vec = pl.BlockSpec((1, D_TOK), lambda i: (0, 0))
    stb = pl.BlockSpec((nsub, N_HEADS, HEAD, HEAD), lambda i: (i, 0, 0, 0))
    return pl.pallas_call(
        body, name=name, grid=(S // tb,), in_specs=[blk(0), blk(1), blk(2), blk(3), vec, vec],
        out_specs=[blk(0), blk(0), stb],
        out_shape=[jax.ShapeDtypeStruct((S, D_MODEL), BF16), jax.ShapeDtypeStruct((S, D_TOK), F32),
                   jax.ShapeDtypeStruct((S // C, N_HEADS, HEAD, HEAD), F32)],
        scratch_shapes=[pltpu.VMEM((N_HEADS, HEAD, HEAD), F32)] + [pltpu.VMEM((N_HEADS, C, HEAD), F32)] * 3,
        compiler_params=_params(("arbitrary",)))(p, p, p, p, lb, onorm)


def _hgrn2_bwd(p, lb, onorm, o, states, dheads, *, name):
    S = p.shape[0]
    C = HG_SUB
    tb = _tile(S, 256, C)
    nsub = tb // C
    nblk = S // tb

    def body(q_ref, fz_ref, v_ref, g_ref, lb_ref, on_ref, o_ref, st_ref, dt_ref, dp_ref, dlb_ref, don_ref, dstate,
             bsc, ksc, vsc, qsc, dosc):
        @pl.when(pl.program_id(0) == 0)
        def _():
            dstate[...] = jnp.zeros_like(dstate)
            dlb_ref[...] = jnp.zeros_like(dlb_ref)
            don_ref[...] = jnp.zeros_like(don_ref)

        lower = _tri(C, False)
        upper = _tri(C, True)
        tt = lax.broadcasted_iota(jnp.int32, (C, HEAD), 0)

        def sub(j, carry):
            c = nsub - 1 - j
            rows = pl.ds(pl.multiple_of(c * C, C), C)
            for h in range(N_HEADS):
                cols = slice(h * HEAD, (h + 1) * HEAD)
                lbv = lb_ref[:, cols]
                onv = on_ref[:, cols]
                qv = q_ref[rows, cols]
                vv = v_ref[rows, cols]
                gv = g_ref[rows, cols]
                ov = o_ref[rows, cols]
                dt = dt_ref[rows, cols].astype(F32)
                sgg = _sigmoid(gv)
                sil = gv * sgg
                rinv = lax.rsqrt(jnp.mean(ov * ov, axis=-1, keepdims=True) + EPS)
                n = ov * rinv
                don_ref[:, cols] += jnp.sum(dt * n * sil, axis=0, keepdims=True)
                dn = dt * sil * onv
                dp_ref[rows, 3 * D_TOK + h * HEAD:3 * D_TOK + (h + 1) * HEAD] = (
                    dt * n * onv * sgg * (1.0 + gv * (1.0 - sgg))).astype(dp_ref.dtype)
                do = rinv * (dn - n * jnp.mean(dn * n, axis=-1, keepdims=True))
                sg, f, lg, kk = _hg_gates(fz_ref[rows, cols], lbv)
                b = jnp.dot(lower, lg, precision=lax.Precision.HIGHEST, preferred_element_type=F32)
                bsc[h] = b
                ksc[h] = kk
                vsc[h] = vv
                qsc[h] = qv
                dosc[h] = do
                b_last = bsc[h, pl.ds(C - 1, 1), :]
                eb = jnp.exp(b)
                qe = qv * eb
                ebb = jnp.exp(b_last - b)
                ke = kk * ebb
                e_last = jnp.exp(b_last)
                st0 = st_ref[c, h]
                dst1 = dstate[h]
                st0x = st0.astype(MXU_DTYPE)
                dst1x = dst1.astype(MXU_DTYPE)
                dox = do.astype(MXU_DTYPE)
                dqe = jnp.dot(dox, st0x, preferred_element_type=F32)
                dke = jnp.dot(vv.astype(MXU_DTYPE), dst1x, preferred_element_type=F32)
                dv = lax.dot_general(ke.astype(MXU_DTYPE), dst1x, (((1,), (1,)), ((), ())),
                                     preferred_element_type=F32)
                db_last = (e_last * jnp.sum(st0 * dst1, axis=0, keepdims=True)
                           + jnp.sum(dke * ke, axis=0, keepdims=True))
                dstate[h] = dst1 * e_last + lax.dot_general(dox, qe.astype(MXU_DTYPE), (((0,), (0,)), ((), ())),
                                                            preferred_element_type=F32)
                dq = dqe * eb
                db = dqe * qe - dke * ke
                dkk = dke * ebb
                for s in range(C):
                    dec = jnp.where(tt >= s, jnp.exp(jnp.minimum(b - bsc[h, pl.ds(s, 1), :], 0.0)), 0.0)
                    da_s = jnp.sum(do * vsc[h, pl.ds(s, 1), :], axis=-1, keepdims=True)
                    pq = da_s * ksc[h, pl.ds(s, 1), :] * dec
                    dq = dq + pq
                    db = db + pq * qv
                for t in range(C):
                    q_t = qsc[h, pl.ds(t, 1), :]
                    do_t = dosc[h, pl.ds(t, 1), :]
                    dec = jnp.where(tt <= t, jnp.exp(jnp.minimum(bsc[h, pl.ds(t, 1), :] - b, 0.0)), 0.0)
                    da_t = jnp.sum(vv * do_t, axis=-1, keepdims=True)
                    pk = da_t * q_t * dec
                    dkk = dkk + pk
                    db = db - pk * kk
                    a_t = jnp.sum(q_t * kk * dec, axis=-1, keepdims=True)
                    dv = dv + a_t * do_t
                db = db + jnp.where(tt == C - 1, db_last, 0.0)
                dlg = jnp.dot(upper, db, precision=lax.Precision.HIGHEST, preferred_element_type=F32)
                w = dlg / f - dkk
                dp_ref[rows, cols] = dq.astype(dp_ref.dtype)
                dp_ref[rows, D_TOK + h * HEAD:D_TOK + (h + 1) * HEAD] = (
                    w * (1.0 - lbv) * sg * (1.0 - sg)).astype(dp_ref.dtype)
                dp_ref[rows, 2 * D_TOK + h * HEAD:2 * D_TOK + (h + 1) * HEAD] = dv.astype(dp_ref.dtype)
                dlb_ref[:, cols] += jnp.sum(w * (1.0 - sg), axis=0, keepdims=True)
            return carry

        lax.fori_loop(0, nsub, sub, 0)

    blk = lambda c: pl.BlockSpec((tb, D_TOK), lambda i, c=c: (nblk - 1 - i, c))
    vec = pl.BlockSpec((1, D_TOK), lambda i: (0, 0))
    stb = pl.BlockSpec((nsub, N_HEADS, HEAD, HEAD), lambda i: (nblk - 1 - i, 0, 0, 0))
    small = jax.ShapeDtypeStruct((1, D_TOK), F32)
    return pl.pallas_call(
        body, name=name, grid=(nblk,), in_specs=[blk(0), blk(1), blk(2), blk(3), vec, vec, blk(0), stb, blk(0)],
        out_specs=[pl.BlockSpec((tb, 4 * D_TOK), lambda i: (nblk - 1 - i, 0)), vec, vec],
        out_shape=[jax.ShapeDtypeStruct((S, HG_IN), BF16), small, small],
        scratch_shapes=[pltpu.VMEM((N_HEADS, HEAD, HEAD), F32)] + [pltpu.VMEM((N_HEADS, C, HEAD), F32)] * 5,
        compiler_params=_params(("arbitrary",)))(p, p, p, p, lb, onorm, o, states, dheads)


def _adamw(w, g, m, v, *, name):
    shape = w.shape
    cols = shape[-1]
    w2, g2, m2, v2 = (t.reshape(-1, cols) for t in (w, g, m, v))
    R = w2.shape[0]
    tr = _tile(R, 512, 8)

    def body(w_ref, g_ref, m_ref, v_ref, d_ref, nm_ref, nv_ref):
        gv = g_ref[...]
        nm = ADAM_B1 * m_ref[...] + (1.0 - ADAM_B1) * gv
        nv = ADAM_B2 * v_ref[...] + (1.0 - ADAM_B2) * (gv * gv)
        m_hat = nm / (1.0 - ADAM_B1 ** ADAM_STEP)
        v_hat = nv / (1.0 - ADAM_B2 ** ADAM_STEP)
        d_ref[...] = -ADAM_LR * (m_hat / (jnp.sqrt(v_hat) + ADAM_EPS) + ADAM_WD * w_ref[...])
        nm_ref[...] = nm
        nv_ref[...] = nv

    spec = pl.BlockSpec((tr, cols), lambda i: (i, 0))
    out = jax.ShapeDtypeStruct((R, cols), F32)
    d, nm, nv = pl.pallas_call(body, name=name, grid=(R // tr,), in_specs=[spec] * 4, out_specs=[spec] * 3,
                               out_shape=[out] * 3, compiler_params=_params(("parallel",)))(w2, g2, m2, v2)
    return d.reshape(shape), nm.reshape(shape), nv.reshape(shape)


def _add_received(own, got, *, name):
    R, Cc = own.shape
    n = got.shape[0]
    tr = _tile(R, 256, 16)

    def body(a_ref, b_ref, o_ref):
        acc = a_ref[...].astype(F32)
        for k in range(n):
            acc = acc + b_ref[k].astype(F32)
        o_ref[...] = acc

    return pl.pallas_call(
        body, name=name, grid=(R // tr,),
        in_specs=[pl.BlockSpec((tr, Cc), lambda i: (i, 0)), pl.BlockSpec((n, tr, Cc), lambda i: (0, i, 0))],
        out_specs=pl.BlockSpec((tr, Cc), lambda i: (i, 0)), out_shape=jax.ShapeDtypeStruct((R, Cc), F32),
        compiler_params=_params(("parallel",)))(own, got)


def _place():
    return lax.axis_index("x"), lax.axis_index("y"), lax.axis_index("c")


def _all_gather(x, *, name, in_vmem, reduce_sum=False, with_token=False):
    R, Cc = x.shape
    space = pltpu.VMEM if in_vmem else pl.ANY

    def body(x_ref, out_ref, *scratch):
        if with_token:
            scratch[0][...] = jnp.zeros_like(scratch[0])
            scratch = scratch[1:]
        if reduce_sum:
            gat_ref, send_sems, recv_sems, local_sem = scratch
        else:
            gat_ref = out_ref
            send_sems, recv_sems, local_sem = scratch
        mx, my, mc = _place()
        me, sibling = (mx, my, mc), (mx, my, 1 - mc)
        chips = [(1 - mx, my), (mx, 1 - my), (1 - mx, 1 - my)]

        def rows(px, py, pc):
            return gat_ref.at[pl.ds((4 * px + 2 * py + pc) * R, R), :]

        def copy(k, block, to, src=None):
            return pltpu.make_async_remote_copy(
                src_ref=rows(*block) if src is None else src, dst_ref=rows(*block), send_sem=send_sems.at[k],
                recv_sem=recv_sems.at[k], device_id=to, device_id_type=MESH_ID)

        mine = pltpu.make_async_copy(x_ref, rows(*me), local_sem)
        mine.start()
        first = [copy(0, me, sibling, src=x_ref)]
        first += [copy(1 + j, me, (*chip, mc), src=x_ref) for j, chip in enumerate(chips)]
        for cp in first:
            cp.start()
        passed = [copy(4 + j, (*chip, mc), sibling) for j, chip in enumerate(chips)]
        for j, chip in enumerate(chips):
            copy(1 + j, (*chip, mc), me).wait_recv()
            passed[j].start()
        copy(0, sibling, me).wait_recv()
        for j, chip in enumerate(chips):
            copy(4 + j, (*chip, 1 - mc), me).wait_recv()
        for cp in first + passed:
            cp.wait_send()
        mine.wait()
        if reduce_sum:
            acc = gat_ref[pl.ds(0, R), :]
            for d in range(1, N_DEV):
                acc = acc + gat_ref[pl.ds(d * R, R), :]
            out_ref[...] = acc

    sems = [pltpu.SemaphoreType.DMA((7,)), pltpu.SemaphoreType.DMA((7,)), pltpu.SemaphoreType.DMA]
    if reduce_sum:
        assert in_vmem
        out_shape = jax.ShapeDtypeStruct((R, Cc), x.dtype)
        scratch = [pltpu.VMEM((N_DEV * R, Cc), x.dtype)] + sems
    else:
        out_shape = jax.ShapeDtypeStruct((N_DEV * R, Cc), x.dtype)
        scratch = sems
    out_specs = pl.BlockSpec(memory_space=space)
    if with_token:
        out_shape = (out_shape, jax.ShapeDtypeStruct((8, LANE), F32))
        out_specs = (out_specs, pl.BlockSpec(memory_space=pltpu.VMEM))
    return pl.pallas_call(
        body, name=name, out_shape=out_shape, in_specs=[pl.BlockSpec(memory_space=space)], out_specs=out_specs,
        scratch_shapes=scratch, compiler_params=pltpu.CompilerParams(vmem_limit_bytes=VMEM_LIMIT))(x)


def _peer(k, mx, my, mc):
    bits = k + 1
    return (1 - mx if bits & 4 else mx, 1 - my if bits & 2 else my, 1 - mc if bits & 1 else mc)


HBM_SPEC = pl.BlockSpec(memory_space=pltpu.HBM)
SEM_SPEC = pl.BlockSpec(memory_space=pltpu.SEMAPHORE)
DATAFLOW = pltpu.SideEffectType.DATAFLOW_SIDE_EFFECTING


def _exchange_copies(x_refs, land_refs, send_sems, recv_sems, scatter):
    mx, my, mc = _place()
    me = 4 * mx + 2 * my + mc
    n = len(x_refs)
    copies = []
    for k in range(N_DEV - 1):
        px, py, pc = _peer(k, mx, my, mc)
        for m, (x_ref, land_ref) in enumerate(zip(x_refs, land_refs)):
            rows = land_ref.shape[1] if scatter else x_ref.shape[0]
            if scatter:
                src = x_ref.at[pl.ds(pl.multiple_of((4 * px + 2 * py + pc) * rows, 16), rows), :]
                dst = land_ref.at[k]
            else:
                src = x_ref
                dst = land_ref.at[pl.ds(pl.multiple_of(me * rows, 16), rows), :]
            copies.append(pltpu.make_async_remote_copy(
                src_ref=src, dst_ref=dst, send_sem=send_sems.at[k * n + m], recv_sem=recv_sems.at[k * n + m],
                device_id=(px, py, pc), device_id_type=MESH_ID))
    return copies


def _land_shape(x, scatter):
    return (N_DEV - 1, x.shape[0] // N_DEV, x.shape[1]) if scatter else (N_DEV * x.shape[0], x.shape[1])


def _exchange_start(xs, *, name, scatter):
    n = len(xs)
    lands = [lax.empty(_land_shape(x, scatter), x.dtype) for x in xs]

    def body(*refs):
        x_refs, land_refs = refs[:n], refs[n:2 * n]
        send_sems, recv_sems = refs[2 * n:2 * n + 2]
        token = refs[-1]
        for cp in _exchange_copies(x_refs, land_refs, send_sems, recv_sems, scatter):
            cp.start()
        token[...] = jnp.zeros_like(token)

    sems = pltpu.SemaphoreType.DMA(((N_DEV - 1) * n,))
    out = pl.pallas_call(
        body, name=name,
        out_shape=(sems, sems, *[pltpu.HBM(x.shape, x.dtype) for x in xs],
                   *[pltpu.HBM(l.shape, l.dtype) for l in lands], jax.ShapeDtypeStruct((8, LANE), F32)),
        in_specs=(HBM_SPEC,) * (2 * n),
        out_specs=(SEM_SPEC, SEM_SPEC) + (HBM_SPEC,) * (2 * n) + (pl.BlockSpec(memory_space=pltpu.VMEM),),
        input_output_aliases={i: 2 + i for i in range(2 * n)},
        compiler_params=pltpu.CompilerParams(has_side_effects=DATAFLOW))(
            *[pltpu.with_memory_space_constraint(t, pltpu.HBM) for t in list(xs) + lands])
    return out[0], out[1], list(out[2:2 + n]), list(out[2 + n:2 + 2 * n]), out[-1]


def _exchange_wait(started, after, *, name, scatter):
    send_sems, recv_sems, xs, lands, _ = started
    n = len(xs)

    def body(*refs):
        x_refs, land_refs = refs[:n], refs[n:2 * n]
        send_sems, recv_sems = refs[2 * n:2 * n + 2]
        for cp in _exchange_copies(x_refs, land_refs, send_sems, recv_sems, scatter):
            cp.wait_send()
            cp.wait_recv()

    out = pl.pallas_call(
        body, name=name, out_shape=tuple(pltpu.HBM(t.shape, t.dtype) for t in xs + lands),
        in_specs=(HBM_SPEC,) * (2 * n) + (SEM_SPEC, SEM_SPEC, pl.BlockSpec(memory_space=pl.ANY)),
        out_specs=(HBM_SPEC,) * (2 * n), input_output_aliases={i: i for i in range(2 * n)},
        compiler_params=pltpu.CompilerParams(has_side_effects=DATAFLOW))(*xs, *lands, send_sems, recv_sems, after)
    return list(out[:n]), list(out[n:])


def _gather_start(shards, token, *, name):
    shards = [shards[0] + token[0, 0].astype(shards[0].dtype)] + list(shards[1:])
    return _exchange_start(shards, name=name, scatter=False)


def _gather_finish(started, after, me, *, name):
    xs, lands = _exchange_wait(started, after, name=name, scatter=False)
    return [lax.dynamic_update_slice(land, x, (me * x.shape[0], 0)) for land, x in zip(lands, xs)]


def _reduce_start(grads, *, name):
    return _exchange_start(grads, name=name, scatter=True)


def _reduce_finish(started, after, me, *, name):
    sent, gots = _exchange_wait(started, after, name=name + "_wait", scatter=True)
    out = []
    for m, (g, got) in enumerate(zip(sent, gots)):
        rows = g.shape[0] // N_DEV
        own = lax.dynamic_slice(g, (me * rows, 0), (rows, g.shape[1]))
        out.append(_add_received(own, got, name=f"{name}_add{m}"))
    return out


def _pad_rows(a, mult):
    r = (-a.shape[0]) % mult
    return a if r == 0 else jnp.concatenate([a, jnp.zeros((r,) + a.shape[1:], a.dtype)], axis=0)


def kernel(x, mem, mix_norm, mem_norm, w_mem_kv, w_out, hg_w_in, hg_lb, hg_onorm, gm_w_in, gm_ln_g, gm_ln_b, gm_ws, gm_bs, ffn_norm, w_ffn_in, w_ffn_out, final_norm, loss_target, m_mix_norm, m_mem_norm, m_w_mem_kv, m_w_out, m_hg_w_in, m_hg_lb, m_hg_onorm, m_gm_w_in, m_gm_ln_g, m_gm_ln_b, m_gm_ws, m_gm_bs, m_ffn_norm, m_w_ffn_in, m_w_ffn_out, m_final_norm, v_mix_norm, v_mem_norm, v_w_mem_kv, v_w_out, v_hg_w_in, v_hg_lb, v_hg_onorm, v_gm_w_in, v_gm_ln_g, v_gm_ln_b, v_gm_ws, v_gm_bs, v_ffn_norm, v_w_ffn_in, v_w_ffn_out, v_final_norm):
    mx, my, mc = _place()
    me = 4 * mx + 2 * my + mc
    xs = x[0]
    mems = mem[0]
    tgt = loss_target[0]

    hg_t = hg_w_in[0].T.astype(BF16)
    gm_t = gm_w_in[0].T.astype(BF16)
    fi_t = [w_ffn_in[i].T.astype(BF16) for i in range(2)]
    kv_b = [w_mem_kv[i].astype(BF16) for i in range(2)]
    out_b = [w_out[i].astype(BF16) for i in range(2)]
    fo_b = [w_ffn_out[i].astype(BF16) for i in range(2)]
    W_hgT, token = _all_gather(hg_t, name="gather_first", in_vmem=False, with_token=True)
    gather_l0 = _gather_start(kv_b + out_b + [fi_t[0], fo_b[0]], token, name="gather_l0_start")
    gather_l1 = _gather_start([gm_t, fi_t[1], fo_b[1]], gather_l0[4], name="gather_l1_start")

    ln_local = _pad_rows(jnp.concatenate([gm_ln_g, gm_ln_b], axis=0), 8)
    ln_local = jnp.concatenate([ln_local, jnp.zeros((8, LANE - ln_local.shape[1]), F32)], axis=1)
    ln_all = _all_gather(ln_local, name="gather_ln", in_vmem=True).reshape(N_DEV, 8, LANE)
    ln_g = ln_all[:, 0, :D_TOK // N_DEV].reshape(1, D_TOK)
    ln_b = ln_all[:, 1, :D_TOK // N_DEV].reshape(1, D_TOK)

    lb_soft = jax.nn.softmax(hg_lb, axis=0)
    lb0 = lb_soft[0:1]
    bsb = jnp.broadcast_to(gm_bs[0][:, :, None], (N_HEADS, GM_CHUNK, GM_CHUNK))
    ws = gm_ws[0]

    def ffn_fwd(xin, i):
        hf = _rms_fwd(xin, ffn_norm[i:i + 1], name=f"ffn_norm{i}")
        gu, act = _ffn_in(hf, W_fiT[i], name=f"ffn_in{i}")
        xout = _matmul(act, W_fo[i], res=xin, name=f"ffn_out{i}")
        return hf, gu, act, xout

    h0 = _rms_fwd(xs, mix_norm[0:1], name="mix_norm0", dep=gather_l1[4])
    p0 = _matmul(h0, W_hgT, tb=True, name="hg_in")
    heads0, o0, states = _hgrn2_fwd(p0, lb0, hg_onorm, name="hgrn2_fwd")

    kv0, kv1, wo0, wo1, fi0, fo0 = _gather_finish(gather_l0, o0, me, name="gather_l0_wait")
    W_kv, W_out, W_fiT, W_fo = [kv0, kv1], [wo0, wo1], [fi0], [fo0]
    mem_n, kv = [], []
    for i in range(2):
        mn = _rms_fwd(mems, mem_norm[i:i + 1], name=f"mem_norm{i}")
        mem_n.append(mn)
        kv.append(_matmul(mn, W_kv[i], name=f"mem_kv{i}"))

    heads0 = _attn_fwd(p0, 4 * D_TOK // D_MEM, kv[0], heads0, name="attn_fwd0")
    x1 = _matmul(heads0, W_out[0], res=xs, name="out_proj0")
    hf0, gu0, act0, x2 = ffn_fwd(x1, 0)

    W_gmT, fi1, fo1 = _gather_finish(gather_l1, x2, me, name="gather_l1_wait")
    W_fiT.append(fi1)
    W_fo.append(fo1)
    h1 = _rms_fwd(x2, mix_norm[1:2], name="mix_norm1")
    p1 = _matmul(h1, W_gmT, tb=True, name="gm_in")
    heads1 = _gmlp_fwd(p1, ln_g, ln_b, ws, bsb, name="gmlp_fwd")
    heads1 = _attn_fwd(p1, 2 * D_TOK // D_MEM, kv[1], heads1, name="attn_fwd1")
    x3 = _matmul(heads1, W_out[1], res=x2, name="out_proj1")
    hf1, gu1, act1, x4 = ffn_fwd(x3, 1)

    dx, g_final, loss_part = _final_loss(x4, final_norm.reshape(1, D_MODEL), tgt, name="final_loss")

    def ffn_bwd(dx, xin, hf, gu, act, i, dep):
        dgu = _ffn_out_dx(dx, W_fo[i], gu, dep, name=f"ffn_out_dx{i}")
        g_wfo = _matmul(act, dx, ta=True, out_dtype=BF16, name=f"ffn_out_dw{i}")
        g_wfi_t = _matmul(dgu, hf, ta=True, a_halves=True, out_dtype=BF16, name=f"ffn_in_dw{i}")
        dhf = _matmul(dgu, W_fiT[i], a_halves=True, name=f"ffn_in_dx{i}")
        dx, g_norm = _rms_bwd(xin, ffn_norm[i:i + 1], dhf, dx, name=f"ffn_norm_bwd{i}")
        return dx, g_wfi_t, g_wfo, g_norm

    def mem_bwd(dkv, i):
        g_wkv = _matmul(mem_n[i], dkv, ta=True, out_dtype=BF16, name=f"mem_kv_dw{i}")
        dmn = _matmul(dkv, W_kv[i], tb=True, name=f"mem_kv_dx{i}")
        _, g_norm = _rms_bwd(mems, mem_norm[i:i + 1], dmn, jnp.zeros_like(mems), name=f"mem_norm_bwd{i}")
        return g_wkv, g_norm

    dx, g_wfi1_t, g_wfo1, g_ffn1 = ffn_bwd(dx, x3, hf1, gu1, act1, 1, loss_part)
    dheads = _matmul(dx, W_out[1], tb=True, name="out_proj_dx1")
    g_wout1 = _matmul(heads1, dx, ta=True, out_dtype=BF16, name="out_proj_dw1")
    dp, g_ws, g_bs, g_lng, g_lnb = _gmlp_bwd(p1, ln_g, ln_b, ws, bsb, dheads, name="gmlp_bwd")
    dp, dk, dv = _attn_bwd(p1, 2 * D_TOK // D_MEM, kv[1], dheads, dp, name="attn_bwd1")
    g_wkv1, g_mem1 = mem_bwd(jnp.concatenate([dk, dv], axis=1), 1)
    g_wgm_t = _matmul(dp, h1, ta=True, out_dtype=BF16, name="gm_in_dw")
    dh = _matmul(dp, W_gmT, name="gm_in_dx")
    dx, g_mix1 = _rms_bwd(x2, mix_norm[1:2], dh, dx, name="mix_norm_bwd1")
    reduce_l1 = _reduce_start([g_wkv1, g_wout1, g_wgm_t, g_wfi1_t, g_wfo1], name="reduce_l1_start")

    dx, g_wfi0_t, g_wfo0, g_ffn0 = ffn_bwd(dx, x1, hf0, gu0, act0, 0, reduce_l1[4])
    reduce_ffn0 = _reduce_start([g_wfi0_t, g_wfo0], name="reduce_ffn0_start")
    dheads = _matmul(dx, W_out[0], tb=True, name="out_proj_dx0", dep=reduce_ffn0[4])
    g_wout0 = _matmul(heads0, dx, ta=True, out_dtype=BF16, name="out_proj_dw0")
    dp, g_lb0, g_onorm = _hgrn2_bwd(p0, lb0, hg_onorm, o0, states, dheads, name="hgrn2_bwd")
    dp, dk, dv = _attn_bwd(p0, 4 * D_TOK // D_MEM, kv[0], dheads, dp, name="attn_bwd0")
    g_wkv0, g_mem0 = mem_bwd(jnp.concatenate([dk, dv], axis=1), 0)
    g_whg_t = _matmul(dp, h0, ta=True, out_dtype=BF16, name="hg_in_dw")
    reduce_mix0 = _reduce_start([g_wkv0, g_wout0, g_whg_t], name="reduce_mix0_start")
    dh = _matmul(dp, W_hgT, name="hg_in_dx", dep=reduce_mix0[4])
    grad_x, g_mix0 = _rms_bwd(xs, mix_norm[0:1], dh, dx, name="mix_norm_bwd0")

    g_kv1, g_out1, g_gm_t, g_fi1_t, g_fo1 = _reduce_finish(reduce_l1, grad_x, me, name="reduce_l1")
    g_fi0_t, g_fo0 = _reduce_finish(reduce_ffn0, g_kv1, me, name="reduce_ffn0")
    g_kv0, g_out0, g_hg_t = _reduce_finish(reduce_mix0, g_fi0_t, me, name="reduce_mix0")
    g_shards = [jnp.stack([g_kv0, g_kv1]), jnp.stack([g_out0, g_out1]), g_hg_t.T[None], g_gm_t.T[None],
                jnp.stack([g_fi0_t.T, g_fi1_t.T]), jnp.stack([g_fo0, g_fo1])]

    small = [loss_part, jnp.concatenate([g_mix0, g_mix1], axis=1), jnp.concatenate([g_mem0, g_mem1], axis=1),
             g_lb0, g_onorm, g_lng, g_lnb, g_ws.reshape(1, -1), g_bs.reshape(1, -1),
             jnp.concatenate([g_ffn0, g_ffn1], axis=1), g_final]
    sizes = [t.shape[1] for t in small]
    small_rows = _pad_rows(jnp.concatenate(small, axis=1).reshape(-1, LANE), 8)
    red = _all_gather(small_rows, name="reduce_small", in_vmem=True, reduce_sum=True).reshape(-1)
    pieces, off = [], 0
    for n in sizes:
        pieces.append(red[off:off + n])
        off += n
    loss = pieces[0][0]
    g_mix_norm = pieces[1].reshape(2, D_MODEL)
    g_mem_norm = pieces[2].reshape(2, D_MODEL)
    g_hg_lb = pieces[3][None, :] * lb0 * (jnp.eye(3, dtype=F32)[:, 0:1] - lb_soft)
    g_hg_onorm = pieces[4].reshape(1, D_TOK)
    width = D_TOK // N_DEV
    g_gm_ln_g = lax.dynamic_slice(pieces[5], (me * width,), (width,)).reshape(1, width)
    g_gm_ln_b = lax.dynamic_slice(pieces[6], (me * width,), (width,)).reshape(1, width)
    g_gm_ws = pieces[7].reshape(gm_ws.shape)
    g_gm_bs = pieces[8].reshape(gm_bs.shape)
    g_ffn_norm = pieces[9].reshape(2, D_MODEL)
    g_final_norm = pieces[10]

    grads = [g_mix_norm, g_mem_norm, g_shards[0], g_shards[1], g_shards[2], g_hg_lb, g_hg_onorm, g_shards[3],
             g_gm_ln_g, g_gm_ln_b, g_gm_ws, g_gm_bs, g_ffn_norm, g_shards[4], g_shards[5], g_final_norm]
    weights = [mix_norm, mem_norm, w_mem_kv, w_out, hg_w_in, hg_lb, hg_onorm, gm_w_in, gm_ln_g, gm_ln_b, gm_ws, gm_bs,
               ffn_norm, w_ffn_in, w_ffn_out, final_norm]
    ms = [m_mix_norm, m_mem_norm, m_w_mem_kv, m_w_out, m_hg_w_in, m_hg_lb, m_hg_onorm, m_gm_w_in, m_gm_ln_g,
          m_gm_ln_b, m_gm_ws, m_gm_bs, m_ffn_norm, m_w_ffn_in, m_w_ffn_out, m_final_norm]
    vs = [v_mix_norm, v_mem_norm, v_w_mem_kv, v_w_out, v_hg_w_in, v_hg_lb, v_hg_onorm, v_gm_w_in, v_gm_ln_g,
          v_gm_ln_b, v_gm_ws, v_gm_bs, v_ffn_norm, v_w_ffn_in, v_w_ffn_out, v_final_norm]
    deltas, new_m, new_v = [], [], []
    for n, (w, g, m, v) in enumerate(zip(weights, grads, ms, vs)):
        if w.ndim == 1:
            d, nm, nv = _adamw(w[None], g.reshape(1, -1), m[None], v[None], name=f"adamw{n}")
            d, nm, nv = d[0], nm[0], nv[0]
        else:
            d, nm, nv = _adamw(w, g.reshape(w.shape), m, v, name=f"adamw{n}")
        deltas.append(d)
        new_m.append(nm)
        new_v.append(nv)
    grads = [g.reshape(w.shape) for g, w in zip(grads, weights)]
    return (loss, grad_x[None], *grads, *deltas, *new_m, *new_v)
```

```python
import functools

import jax
import jax.numpy as jnp
from jax import lax
from jax.experimental import pallas as pl
from jax.experimental.pallas import tpu as pltpu

F32 = jnp.float32
BF16 = jnp.bfloat16
MXU_DTYPE = jnp.bfloat16
MESH_ID = pl.DeviceIdType.MESH

N_DEV = 8
EPS = 1e-6
D_MODEL = 1024
D_TOK = 768
D_MEM = 256
N_HEADS = 6
HEAD = 128
MEM_HEADS = 4
MEM_HDIM = 64
GM_CHUNK = 128
D_FF = 2816
HG_SUB = 16
HG_IN = 4 * D_TOK + D_MEM
GM_IN = 2 * D_TOK + D_MEM
LANE = 128

ADAM_LR = 0.001
ADAM_B1 = 0.9
ADAM_B2 = 0.999
ADAM_EPS = 1e-08
ADAM_WD = 0.01
ADAM_STEP = 10

VMEM_LIMIT = 48 * 2 ** 20


def _params(sem=None):
    return pltpu.CompilerParams(dimension_semantics=sem, vmem_limit_bytes=VMEM_LIMIT)


def _tile(n, cap, q=LANE):
    if n <= cap:
        return n
    best = None
    for t in range(q, cap + 1, q):
        if n % t == 0:
            best = t
    assert best is not None, (n, cap, q)
    return best


def _sigmoid(x):
    return 1.0 / (1.0 + jnp.exp(-x))


def _gelu(x):
    return 0.5 * x * (1.0 + lax.erf(x * 0.7071067811865476))


def _gelu_grad(x):
    return 0.5 * (1.0 + lax.erf(x * 0.7071067811865476)) + x * jnp.exp(-0.5 * x * x) * 0.3989422804014327


def _matmul(a, b, *, name, ta=False, tb=False, res=None, out_dtype=F32, a_halves=False, b_halves=False, dep=None):
    if a_halves and ta:
        K, M = a.shape[1], 2 * a.shape[2]
    elif a_halves:
        M, K = a.shape[1], 2 * a.shape[2]
    else:
        K, M = a.shape if ta else a.shape[::-1]
    if b_halves:
        assert not tb and b.shape[1] == K
        N = 2 * b.shape[2]
    else:
        N = b.shape[0] if tb else b.shape[1]
        assert (b.shape[1] if tb else b.shape[0]) == K
    tm = _tile(M // 2 if (a_halves and ta) else M, 1664 if ta else 1024)
    tn = _tile(N // 2 if b_halves else N, 1792)
    tk = _tile(K // 2 if (a_halves and not ta) else K, 1024 if ta else 1664)
    nk = K // tk
    dims = (((0 if ta else 1,), (1 if tb else 0,)), ((), ()))

    def body(*refs):
        a_ref, b_ref = refs[:2]
        r_ref = refs[2] if res is not None else None
        o_ref, acc = refs[-2:]
        k = pl.program_id(2)

        @pl.when(k == 0)
        def _():
            acc[...] = jnp.zeros_like(acc)

        acc[...] += lax.dot_general(a_ref[...].astype(MXU_DTYPE), b_ref[...].astype(MXU_DTYPE), dims,
                                    preferred_element_type=F32)

        @pl.when(k == nk - 1)
        def _():
            r = acc[...]
            if res is not None:
                r = r + r_ref[...].astype(F32)
            o_ref[...] = r.astype(out_dtype)

    if a_halves and ta:
        mh = M // 2 // tm
        a_spec = pl.BlockSpec((None, tk, tm), lambda i, j, k: (i // mh, k, i % mh))
    elif a_halves:
        kh = nk // 2
        a_spec = pl.BlockSpec((None, tm, tk), lambda i, j, k: (k // kh, i, k % kh))
    elif ta:
        a_spec = pl.BlockSpec((tk, tm), lambda i, j, k: (k, i))
    else:
        a_spec = pl.BlockSpec((tm, tk), lambda i, j, k: (i, k))
    if b_halves:
        nh = N // 2 // tn
        b_spec = pl.BlockSpec((None, tk, tn), lambda i, j, k: (j // nh, k, j % nh))
    elif tb:
        b_spec = pl.BlockSpec((tn, tk), lambda i, j, k: (j, k))
    else:
        b_spec = pl.BlockSpec((tk, tn), lambda i, j, k: (k, j))
    o_spec = pl.BlockSpec((tm, tn), lambda i, j, k: (i, j))
    in_specs = [a_spec, b_spec] + ([o_spec] if res is not None else [])
    args = (a, b) + ((res,) if res is not None else ())
    if dep is not None:
        in_specs.append(pl.BlockSpec(memory_space=pl.ANY))
        args += (dep,)
    return pl.pallas_call(
        body, name=name, grid=(M // tm, N // tn, nk), in_specs=in_specs, out_specs=o_spec,
        out_shape=jax.ShapeDtypeStruct((M, N), out_dtype), scratch_shapes=[pltpu.VMEM((tm, tn), F32)],
        compiler_params=_params(("parallel", "parallel", "arbitrary")))(*args)


def _ffn_in(hf, wt, *, name):
    S, K = hf.shape
    tm = _tile(S, 512)
    tn = _tile(D_FF, 1408)
    nh = D_FF // tn
    nt = (((1,), (1,)), ((), ()))

    def body(a_ref, bg_ref, bu_ref, gu_ref, act_ref):
        av = a_ref[...].astype(MXU_DTYPE)
        gate = lax.dot_general(av, bg_ref[...].astype(MXU_DTYPE), nt, preferred_element_type=F32)
        up = lax.dot_general(av, bu_ref[...].astype(MXU_DTYPE), nt, preferred_element_type=F32)
        gu_ref[0] = gate.astype(gu_ref.dtype)
        gu_ref[1] = up.astype(gu_ref.dtype)
        act_ref[...] = (gate * _sigmoid(gate) * up).astype(act_ref.dtype)

    return pl.pallas_call(
        body, name=name, grid=(S // tm, nh),
        in_specs=[pl.BlockSpec((tm, K), lambda i, j: (i, 0)), pl.BlockSpec((tn, K), lambda i, j: (j, 0)),
                  pl.BlockSpec((tn, K), lambda i, j: (j + nh, 0))],
        out_specs=[pl.BlockSpec((2, tm, tn), lambda i, j: (0, i, j)), pl.BlockSpec((tm, tn), lambda i, j: (i, j))],
        out_shape=[jax.ShapeDtypeStruct((2, S, D_FF), BF16), jax.ShapeDtypeStruct((S, D_FF), BF16)],
        compiler_params=_params(("parallel", "parallel")))(hf, wt, wt)


def _ffn_out_dx(dx, w, gu, dep, *, name):
    S, K = dx.shape
    tm = _tile(S, 512)
    tn = _tile(D_FF, 1408)

    def body(a_ref, b_ref, gu_ref, dep_ref, o_ref):
        del dep_ref
        da = lax.dot_general(a_ref[...].astype(MXU_DTYPE), b_ref[...].astype(MXU_DTYPE), (((1,), (1,)), ((), ())),
                             preferred_element_type=F32)
        gate = gu_ref[0].astype(F32)
        up = gu_ref[1].astype(F32)
        sg = _sigmoid(gate)
        o_ref[0] = (da * up * sg * (1.0 + gate * (1.0 - sg))).astype(o_ref.dtype)
        o_ref[1] = (da * gate * sg).astype(o_ref.dtype)

    halves = pl.BlockSpec((2, tm, tn), lambda i, j: (0, i, j))
    return pl.pallas_call(
        body, name=name, grid=(S // tm, D_FF // tn),
        in_specs=[pl.BlockSpec((tm, K), lambda i, j: (i, 0)), pl.BlockSpec((tn, K), lambda i, j: (j, 0)), halves,
                  pl.BlockSpec(memory_space=pl.ANY)],
        out_specs=halves, out_shape=jax.ShapeDtypeStruct((2, S, D_FF), BF16),
        compiler_params=_params(("parallel", "parallel")))(dx, w, gu, dep)


def _rms_fwd(x, g, *, name, dep=None):
    R, Dm = x.shape
    tr = _tile(R, 512, 8)

    def body(x_ref, g_ref, *rest):
        o_ref = rest[-1]
        xv = x_ref[...]
        r = lax.rsqrt(jnp.mean(xv * xv, axis=-1, keepdims=True) + EPS)
        o_ref[...] = (xv * r * g_ref[...]).astype(o_ref.dtype)

    in_specs = [pl.BlockSpec((tr, Dm), lambda i: (i, 0)), pl.BlockSpec((1, Dm), lambda i: (0, 0))]
    args = (x, g)
    if dep is not None:
        in_specs.append(pl.BlockSpec(memory_space=pl.ANY))
        args += (dep,)
    return pl.pallas_call(
        body, name=name, grid=(R // tr,), in_specs=in_specs,
        out_specs=pl.BlockSpec((tr, Dm), lambda i: (i, 0)), out_shape=jax.ShapeDtypeStruct((R, Dm), BF16),
        compiler_params=_params(("parallel",)))(*args)


def _rms_bwd(x, g, dh, dres, *, name):
    R, Dm = x.shape
    tr = _tile(R, 256, 8)

    def body(x_ref, g_ref, dh_ref, dres_ref, dx_ref, dg_ref):
        @pl.when(pl.program_id(0) == 0)
        def _():
            dg_ref[...] = jnp.zeros_like(dg_ref)

        xv = x_ref[...]
        r = lax.rsqrt(jnp.mean(xv * xv, axis=-1, keepdims=True) + EPS)
        xh = xv * r
        dhv = dh_ref[...].astype(F32)
        dg_ref[...] += jnp.sum(dhv * xh, axis=0, keepdims=True)
        u = dhv * g_ref[...]
        dx = r * (u - xh * jnp.mean(u * xh, axis=-1, keepdims=True))
        dx_ref[...] = dres_ref[...] + dx

    row = pl.BlockSpec((tr, Dm), lambda i: (i, 0))
    vec = pl.BlockSpec((1, Dm), lambda i: (0, 0))
    return pl.pallas_call(
        body, name=name, grid=(R // tr,), in_specs=[row, vec, row, row], out_specs=[row, vec],
        out_shape=[jax.ShapeDtypeStruct((R, Dm), F32), jax.ShapeDtypeStruct((1, Dm), F32)],
        compiler_params=_params(("arbitrary",)))(x, g, dh, dres)


def _final_loss(x, g, tgt, *, name):
    R, Dm = x.shape
    tr = _tile(R, 256, 8)

    def body(x_ref, g_ref, t_ref, dx_ref, dg_ref, loss_ref):
        @pl.when(pl.program_id(0) == 0)
        def _():
            dg_ref[...] = jnp.zeros_like(dg_ref)
            loss_ref[...] = jnp.zeros_like(loss_ref)

        xv = x_ref[...]
        r = lax.rsqrt(jnp.mean(xv * xv, axis=-1, keepdims=True) + EPS)
        xh = xv * r
        gv = g_ref[...]
        err = xh * gv - t_ref[...]
        part = 0.5 * jnp.sum(jnp.mean(err * err, axis=-1, keepdims=True), axis=0, keepdims=True)
        loss_ref[...] += jnp.broadcast_to(part, loss_ref.shape)
        dy = err * (1.0 / Dm)
        dg_ref[...] += jnp.sum(dy * xh, axis=0, keepdims=True)
        u = dy * gv
        dx_ref[...] = r * (u - xh * jnp.mean(u * xh, axis=-1, keepdims=True))

    row = pl.BlockSpec((tr, Dm), lambda i: (i, 0))
    vec = pl.BlockSpec((1, Dm), lambda i: (0, 0))
    one = pl.BlockSpec((1, LANE), lambda i: (0, 0))
    return pl.pallas_call(
        body, name=name, grid=(R // tr,), in_specs=[row, vec, row], out_specs=[row, vec, one],
        out_shape=[jax.ShapeDtypeStruct((R, Dm), F32), jax.ShapeDtypeStruct((1, Dm), F32),
                   jax.ShapeDtypeStruct((1, LANE), F32)],
        compiler_params=_params(("arbitrary",)))(x, g, tgt)


def _head_mask(h):
    lane = lax.broadcasted_iota(jnp.int32, (1, D_MEM), 1)
    return (lane >= h * MEM_HDIM) & (lane < (h + 1) * MEM_HDIM)


def _attn_probs(qv, k_mx, mask):
    s = lax.dot_general(jnp.where(mask, qv, 0.0).astype(MXU_DTYPE), k_mx, (((1,), (1,)), ((), ())),
                        preferred_element_type=F32) * (MEM_HDIM ** -0.5)
    e = jnp.exp(s - jnp.max(s, axis=-1, keepdims=True))
    return e / jnp.sum(e, axis=-1, keepdims=True)


def _attn_fwd(p, qcol, kv, heads, *, name):
    S = p.shape[0]
    M = kv.shape[0]
    ts = _tile(S, 512, 8)

    def body(q_ref, k_ref, v_ref, heads_in, o_ref):
        del heads_in
        qv = q_ref[...]
        kx = k_ref[...].astype(MXU_DTYPE)
        vv = v_ref[...]
        out = jnp.zeros((ts, D_MEM), F32)
        for h in range(MEM_HEADS):
            mask = _head_mask(h)
            pr = _attn_probs(qv, kx, mask)
            out = out + jnp.dot(pr.astype(MXU_DTYPE), jnp.where(mask, vv, 0.0).astype(MXU_DTYPE),
                                preferred_element_type=F32)
        o_ref[...] = out.astype(o_ref.dtype)

    return pl.pallas_call(
        body, name=name, grid=(S // ts,),
        in_specs=[pl.BlockSpec((ts, D_MEM), lambda i: (i, qcol)), pl.BlockSpec((M, D_MEM), lambda i: (0, 0)),
                  pl.BlockSpec((M, D_MEM), lambda i: (0, 1)), pl.BlockSpec(memory_space=pl.ANY)],
        out_specs=pl.BlockSpec((ts, D_MEM), lambda i: (i, D_TOK // D_MEM)),
        out_shape=jax.ShapeDtypeStruct(heads.shape, heads.dtype), input_output_aliases={3: 0},
        compiler_params=_params(("parallel",)))(p, kv, kv, heads)


def _attn_bwd(p, qcol, kv, dheads, dp, *, name):
    S = p.shape[0]
    M = kv.shape[0]
    ts = _tile(S, 512, 8)
    scale = MEM_HDIM ** -0.5

    def body(q_ref, k_ref, v_ref, do_ref, dp_in, dq_ref, dk_ref, dv_ref):
        del dp_in

        @pl.when(pl.program_id(0) == 0)
        def _():
            dk_ref[...] = jnp.zeros_like(dk_ref)
            dv_ref[...] = jnp.zeros_like(dv_ref)

        qv = q_ref[...]
        kv_ = k_ref[...]
        kx = kv_.astype(MXU_DTYPE)
        vv = v_ref[...]
        dox = do_ref[...].astype(MXU_DTYPE)
        qx = qv.astype(MXU_DTYPE)
        dq = jnp.zeros((ts, D_MEM), F32)
        for h in range(MEM_HEADS):
            mask = _head_mask(h)
            pr = _attn_probs(qv, kx, mask)
            vh = jnp.where(mask, vv, 0.0).astype(MXU_DTYPE)
            dpr = lax.dot_general(dox, vh, (((1,), (1,)), ((), ())), preferred_element_type=F32)
            ds = (pr * (dpr - jnp.sum(dpr * pr, axis=-1, keepdims=True)) * scale).astype(MXU_DTYPE)
            dq = dq + jnp.dot(ds, jnp.where(mask, kv_, 0.0).astype(MXU_DTYPE), preferred_element_type=F32)
            dk_h = lax.dot_general(ds, qx, (((0,), (0,)), ((), ())), preferred_element_type=F32)
            dv_h = lax.dot_general(pr.astype(MXU_DTYPE), dox, (((0,), (0,)), ((), ())), preferred_element_type=F32)
            dk_ref[...] += jnp.where(mask, dk_h, 0.0)
            dv_ref[...] += jnp.where(mask, dv_h, 0.0)
        dq_ref[...] = dq.astype(dq_ref.dtype)

    return pl.pallas_call(
        body, name=name, grid=(S // ts,),
        in_specs=[pl.BlockSpec((ts, D_MEM), lambda i: (i, qcol)), pl.BlockSpec((M, D_MEM), lambda i: (0, 0)),
                  pl.BlockSpec((M, D_MEM), lambda i: (0, 1)),
                  pl.BlockSpec((ts, D_MEM), lambda i: (i, D_TOK // D_MEM)), pl.BlockSpec(memory_space=pl.ANY)],
        out_specs=[pl.BlockSpec((ts, D_MEM), lambda i: (i, qcol)), pl.BlockSpec((M, D_MEM), lambda i: (0, 0)),
                   pl.BlockSpec((M, D_MEM), lambda i: (0, 0))],
        out_shape=[jax.ShapeDtypeStruct(dp.shape, dp.dtype), jax.ShapeDtypeStruct((M, D_MEM), F32),
                   jax.ShapeDtypeStruct((M, D_MEM), F32)],
        input_output_aliases={4: 0}, compiler_params=_params(("arbitrary",)))(p, kv, kv, dheads, dp)


def _gm_forward_parts(u_ref, v_ref, lng_ref, lnb_ref, w_ref, bsb_ref):
    zu = _gelu(u_ref[...])
    zv = _gelu(v_ref[...])
    mu = jnp.mean(zv, axis=-1, keepdims=True)
    cen = zv - mu
    rs = lax.rsqrt(jnp.mean(cen * cen, axis=-1, keepdims=True) + EPS)
    vh = cen * rs
    vn = vh * lng_ref[...] + lnb_ref[...]
    row = lax.broadcasted_iota(jnp.int32, (GM_CHUNK, GM_CHUNK), 0)
    col = lax.broadcasted_iota(jnp.int32, (GM_CHUNK, GM_CHUNK), 1)
    tril = row >= col
    wm = [jnp.where(tril, w_ref[g], 0.0).astype(MXU_DTYPE) for g in range(N_HEADS)]
    vnx = [vn[:, g * HEAD:(g + 1) * HEAD].astype(MXU_DTYPE) for g in range(N_HEADS)]
    sv = [jnp.dot(wm[g], vnx[g], preferred_element_type=F32) + bsb_ref[g] for g in range(N_HEADS)]
    return zu, vh, rs, wm, vnx, sv, tril


def _gmlp_fwd(p, lng, lnb, ws, bsb, *, name):
    S = p.shape[0]

    def body(u_ref, v_ref, lng_ref, lnb_ref, w_ref, bsb_ref, o_ref):
        zu, _, _, _, _, sv, _ = _gm_forward_parts(u_ref, v_ref, lng_ref, lnb_ref, w_ref, bsb_ref)
        for g in range(N_HEADS):
            o_ref[:, g * HEAD:(g + 1) * HEAD] = (zu[:, g * HEAD:(g + 1) * HEAD] * sv[g]).astype(o_ref.dtype)

    blk = lambda c: pl.BlockSpec((GM_CHUNK, D_TOK), lambda i: (i, c))
    vec = pl.BlockSpec((1, D_TOK), lambda i: (0, 0))
    cube = pl.BlockSpec((N_HEADS, GM_CHUNK, GM_CHUNK), lambda i: (0, 0, 0))
    return pl.pallas_call(
        body, name=name, grid=(S // GM_CHUNK,), in_specs=[blk(0), blk(1), vec, vec, cube, cube],
        out_specs=blk(0), out_shape=jax.ShapeDtypeStruct((S, D_MODEL), BF16),
        compiler_params=_params(("parallel",)))(p, p, lng, lnb, ws, bsb)


def _gmlp_bwd(p, lng, lnb, ws, bsb, dheads, *, name):
    S = p.shape[0]

    def body(u_ref, v_ref, lng_ref, lnb_ref, w_ref, bsb_ref, dt_ref, dp_ref, dw_ref, dbs_ref, dlg_ref, dlb_ref):
        @pl.when(pl.program_id(0) == 0)
        def _():
            dw_ref[...] = jnp.zeros_like(dw_ref)
            dbs_ref[...] = jnp.zeros_like(dbs_ref)
            dlg_ref[...] = jnp.zeros_like(dlg_ref)
            dlb_ref[...] = jnp.zeros_like(dlb_ref)

        zu, vh, rs, wm, vnx, sv, tril = _gm_forward_parts(u_ref, v_ref, lng_ref, lnb_ref, w_ref, bsb_ref)
        dt = dt_ref[...].astype(F32)
        dvn_parts = []
        for g in range(N_HEADS):
            sl = slice(g * HEAD, (g + 1) * HEAD)
            dsv = dt[:, sl] * zu[:, sl]
            dp_ref[:, sl] = (dt[:, sl] * sv[g] * _gelu_grad(u_ref[:, sl])).astype(dp_ref.dtype)
            dsx = dsv.astype(MXU_DTYPE)
            dw = lax.dot_general(dsx, vnx[g], (((1,), (1,)), ((), ())), preferred_element_type=F32)
            dw_ref[g] += jnp.where(tril, dw, 0.0)
            dbs_ref[g] += jnp.sum(dsv, axis=-1, keepdims=True)
            dvn_parts.append(lax.dot_general(wm[g], dsx, (((0,), (0,)), ((), ())), preferred_element_type=F32))
        dvn = jnp.concatenate(dvn_parts, axis=-1)
        dlg_ref[...] += jnp.sum(dvn * vh, axis=0, keepdims=True)
        dlb_ref[...] += jnp.sum(dvn, axis=0, keepdims=True)
        dvh = dvn * lng_ref[...]
        dzv = rs * (dvh - jnp.mean(dvh, axis=-1, keepdims=True) - vh * jnp.mean(dvh * vh, axis=-1, keepdims=True))
        dp_ref[:, D_TOK:] = (dzv * _gelu_grad(v_ref[...])).astype(dp_ref.dtype)

    blk = lambda c: pl.BlockSpec((GM_CHUNK, D_TOK), lambda i: (i, c))
    vec = pl.BlockSpec((1, D_TOK), lambda i: (0, 0))
    cube = pl.BlockSpec((N_HEADS, GM_CHUNK, GM_CHUNK), lambda i: (0, 0, 0))
    col = pl.BlockSpec((N_HEADS, GM_CHUNK, 1), lambda i: (0, 0, 0))
    return pl.pallas_call(
        body, name=name, grid=(S // GM_CHUNK,), in_specs=[blk(0), blk(1), vec, vec, cube, cube, blk(0)],
        out_specs=[pl.BlockSpec((GM_CHUNK, 2 * D_TOK), lambda i: (i, 0)), cube, col, vec, vec],
        out_shape=[jax.ShapeDtypeStruct((S, GM_IN), BF16), jax.ShapeDtypeStruct((N_HEADS, GM_CHUNK, GM_CHUNK), F32),
                   jax.ShapeDtypeStruct((N_HEADS, GM_CHUNK, 1), F32), jax.ShapeDtypeStruct((1, D_TOK), F32),
                   jax.ShapeDtypeStruct((1, D_TOK), F32)],
        compiler_params=_params(("arbitrary",)))(p, p, lng, lnb, ws, bsb, dheads)


def _tri(n, upper):
    r = lax.broadcasted_iota(jnp.int32, (n, n), 0)
    c = lax.broadcasted_iota(jnp.int32, (n, n), 1)
    return jnp.where((r <= c) if upper else (r >= c), 1.0, 0.0).astype(F32)


def _hg_gates(fz, lb):
    sg = _sigmoid(fz)
    f = lb + (1.0 - lb) * sg
    kk = (1.0 - lb) * (1.0 - sg)
    return sg, f, jnp.log(f), kk


def _hgrn2_fwd(p, lb, onorm, *, name):
    S = p.shape[0]
    C = HG_SUB
    tb = _tile(S, 256, C)
    nsub = tb // C

    def body(q_ref, fz_ref, v_ref, g_ref, lb_ref, on_ref, tok_ref, o_ref, st_ref, state, bsc, ksc, vsc):
        @pl.when(pl.program_id(0) == 0)
        def _():
            state[...] = jnp.zeros_like(state)

        lower = _tri(C, False)
        tt = lax.broadcasted_iota(jnp.int32, (C, HEAD), 0)

        def sub(c, carry):
            rows = pl.ds(pl.multiple_of(c * C, C), C)
            for h in range(N_HEADS):
                cols = slice(h * HEAD, (h + 1) * HEAD)
                qv = q_ref[rows, cols]
                vv = v_ref[rows, cols]
                gv = g_ref[rows, cols]
                _, _, lg, kk = _hg_gates(fz_ref[rows, cols], lb_ref[:, cols])
                b = jnp.dot(lower, lg, precision=lax.Precision.HIGHEST, preferred_element_type=F32)
                st0 = state[h]
                st_ref[c, h] = st0
                inter = lax.dot_general((qv * jnp.exp(b)).astype(MXU_DTYPE), st0.astype(MXU_DTYPE),
                                        (((1,), (1,)), ((), ())), preferred_element_type=F32)
                bsc[h] = b
                ksc[h] = kk
                vsc[h] = vv
                intra = jnp.zeros((C, HEAD), F32)
                for s in range(C):
                    dec = jnp.where(tt >= s, jnp.exp(jnp.minimum(b - bsc[h, pl.ds(s, 1), :], 0.0)), 0.0)
                    a_s = jnp.sum(qv * ksc[h, pl.ds(s, 1), :] * dec, axis=-1, keepdims=True)
                    intra = intra + a_s * vsc[h, pl.ds(s, 1), :]
                o = inter + intra
                b_last = bsc[h, pl.ds(C - 1, 1), :]
                ke = kk * jnp.exp(b_last - b)
                state[h] = st0 * jnp.exp(b_last) + lax.dot_general(
                    vv.astype(MXU_DTYPE), ke.astype(MXU_DTYPE), (((0,), (0,)), ((), ())),
                    preferred_element_type=F32)
                o_ref[rows, cols] = o
                n = o * lax.rsqrt(jnp.mean(o * o, axis=-1, keepdims=True) + EPS)
                tok_ref[rows, cols] = (n * (gv * _sigmoid(gv)) * on_ref[:, cols]).astype(tok_ref.dtype)
            return carry

        lax.fori_loop(0, nsub, sub, 0)

    blk = lambda c: pl.BlockSpec((tb, D_TOK), lambda i, c=c: (i, c))
    vec = pl.BlockSpec((1, D_TOK), lambda i: (0, 0))
    stb = pl.BlockSpec((nsub, N_HEADS, HEAD, HEAD), lambda i: (i, 0, 0, 0))
    return pl.pallas_call(
        body, name=name, grid=(S // tb,), in_specs=[blk(0), blk(1), blk(2), blk(3), vec, vec],
        out_specs=[blk(0), blk(0), stb],
        out_shape=[jax.ShapeDtypeStruct((S, D_MODEL), BF16), jax.ShapeDtypeStruct((S, D_TOK), F32),
                   jax.ShapeDtypeStruct((S // C, N_HEADS, HEAD, HEAD), F32)],
        scratch_shapes=[pltpu.VMEM((N_HEADS, HEAD, HEAD), F32)] + [pltpu.VMEM((N_HEADS, C, HEAD), F32)] * 3,
        compiler_params=_params(("arbitrary",)))(p, p, p, p, lb, onorm)


def _hgrn2_bwd(p, lb, onorm, o, states, dheads, *, name):
    S = p.shape[0]
    C = HG_SUB
    tb = _tile(S, 256, C)
    nsub = tb // C
    nblk = S // tb

    def body(q_ref, fz_ref, v_ref, g_ref, lb_ref, on_ref, o_ref, st_ref, dt_ref, dp_ref, dlb_ref, don_ref, dstate,
             bsc, ksc, vsc, qsc, dosc):
        @pl.when(pl.program_id(0) == 0)
        def _():
            dstate[...] = jnp.zeros_like(dstate)
            dlb_ref[...] = jnp.zeros_like(dlb_ref)
            don_ref[...] = jnp.zeros_like(don_ref)

        lower = _tri(C, False)
        upper = _tri(C, True)
        tt = lax.broadcasted_iota(jnp.int32, (C, HEAD), 0)

        def sub(j, carry):
            c = nsub - 1 - j
            rows = pl.ds(pl.multiple_of(c * C, C), C)
            for h in range(N_HEADS):
                cols = slice(h * HEAD, (h + 1) * HEAD)
                lbv = lb_ref[:, cols]
                onv = on_ref[:, cols]
                qv = q_ref[rows, cols]
                vv = v_ref[rows, cols]
                gv = g_ref[rows, cols]
                ov = o_ref[rows, cols]
                dt = dt_ref[rows, cols].astype(F32)
                sgg = _sigmoid(gv)
                sil = gv * sgg
                rinv = lax.rsqrt(jnp.mean(ov * ov, axis=-1, keepdims=True) + EPS)
                n = ov * rinv
                don_ref[:, cols] += jnp.sum(dt * n * sil, axis=0, keepdims=True)
                dn = dt * sil * onv
                dp_ref[rows, 3 * D_TOK + h * HEAD:3 * D_TOK + (h + 1) * HEAD] = (
                    dt * n * onv * sgg * (1.0 + gv * (1.0 - sgg))).astype(dp_ref.dtype)
                do = rinv * (dn - n * jnp.mean(dn * n, axis=-1, keepdims=True))
                sg, f, lg, kk = _hg_gates(fz_ref[rows, cols], lbv)
                b = jnp.dot(lower, lg, precision=lax.Precision.HIGHEST, preferred_element_type=F32)
                bsc[h] = b
                ksc[h] = kk
                vsc[h] = vv
                qsc[h] = qv
                dosc[h] = do
                b_last = bsc[h, pl.ds(C - 1, 1), :]
                eb = jnp.exp(b)
                qe = qv * eb
                ebb = jnp.exp(b_last - b)
                ke = kk * ebb
                e_last = jnp.exp(b_last)
                st0 = st_ref[c, h]
                dst1 = dstate[h]
                st0x = st0.astype(MXU_DTYPE)
                dst1x = dst1.astype(MXU_DTYPE)
                dox = do.astype(MXU_DTYPE)
                dqe = jnp.dot(dox, st0x, preferred_element_type=F32)
                dke = jnp.dot(vv.astype(MXU_DTYPE), dst1x, preferred_element_type=F32)
                dv = lax.dot_general(ke.astype(MXU_DTYPE), dst1x, (((1,), (1,)), ((), ())),
                                     preferred_element_type=F32)
                db_last = (e_last * jnp.sum(st0 * dst1, axis=0, keepdims=True)
                           + jnp.sum(dke * ke, axis=0, keepdims=True))
                dstate[h] = dst1 * e_last + lax.dot_general(dox, qe.astype(MXU_DTYPE), (((0,), (0,)), ((), ())),
                                                            preferred_element_type=F32)
                dq = dqe * eb
                db = dqe * qe - dke * ke
                dkk = dke * ebb
                for s in range(C):
                    dec = jnp.where(tt >= s, jnp.exp(jnp.minimum(b - bsc[h, pl.ds(s, 1), :], 0.0)), 0.0)
                    da_s = jnp.sum(do * vsc[h, pl.ds(s, 1), :], axis=-1, keepdims=True)
                    pq = da_s * ksc[h, pl.ds(s, 1), :] * dec
                    dq = dq + pq
                    db = db + pq * qv
                for t in range(C):
                    q_t = qsc[h, pl.ds(t, 1), :]
                    do_t = dosc[h, pl.ds(t, 1), :]
                    dec = jnp.where(tt <= t, jnp.exp(jnp.minimum(bsc[h, pl.ds(t, 1), :] - b, 0.0)), 0.0)
                    da_t = jnp.sum(vv * do_t, axis=-1, keepdims=True)
                    pk = da_t * q_t * dec
                    dkk = dkk + pk
                    db = db - pk * kk
                    a_t = jnp.sum(q_t * kk * dec, axis=-1, keepdims=True)
                    dv = dv + a_t * do_t
                db = db + jnp.where(tt == C - 1, db_last, 0.0)
                dlg = jnp.dot(upper, db, precision=lax.Precision.HIGHEST, preferred_element_type=F32)
                w = dlg / f - dkk
                dp_ref[rows, cols] = dq.astype(dp_ref.dtype)
                dp_ref[rows, D_TOK + h * HEAD:D_TOK + (h + 1) * HEAD] = (
                    w * (1.0 - lbv) * sg * (1.0 - sg)).astype(dp_ref.dtype)
                dp_ref[rows, 2 * D_TOK + h * HEAD:2 * D_TOK + (h + 1) * HEAD] = dv.astype(dp_ref.dtype)
                dlb_ref[:, cols] += jnp.sum(w * (1.0 - sg), axis=0, keepdims=True)
            return carry

        lax.fori_loop(0, nsub, sub, 0)

    blk = lambda c: pl.BlockSpec((tb, D_TOK), lambda i, c=c: (nblk - 1 - i, c))
    vec = pl.BlockSpec((1, D_TOK), lambda i: (0, 0))
    stb = pl.BlockSpec((nsub, N_HEADS, HEAD, HEAD), lambda i: (nblk - 1 - i, 0, 0, 0))
    small = jax.ShapeDtypeStruct((1, D_TOK), F32)
    return pl.pallas_call(
        body, name=name, grid=(nblk,), in_specs=[blk(0), blk(1), blk(2), blk(3), vec, vec, blk(0), stb, blk(0)],
        out_specs=[pl.BlockSpec((tb, 4 * D_TOK), lambda i: (nblk - 1 - i, 0)), vec, vec],
        out_shape=[jax.ShapeDtypeStruct((S, HG_IN), BF16), small, small],
        scratch_shapes=[pltpu.VMEM((N_HEADS, HEAD, HEAD), F32)] + [pltpu.VMEM((N_HEADS, C, HEAD), F32)] * 5,
        compiler_params=_params(("arbitrary",)))(p, p, p, p, lb, onorm, o, states, dheads)


def _adamw(w, g, m, v, *, name):
    shape = w.shape
    cols = shape[-1]
    w2, g2, m2, v2 = (t.reshape(-1, cols) for t in (w, g, m, v))
    R = w2.shape[0]
    tr = _tile(R, 512, 8)

    def body(w_ref, g_ref, m_ref, v_ref, d_ref, nm_ref, nv_ref):
        gv = g_ref[...]
        nm = ADAM_B1 * m_ref[...] + (1.0 - ADAM_B1) * gv
        nv = ADAM_B2 * v_ref[...] + (1.0 - ADAM_B2) * (gv * gv)
        m_hat = nm / (1.0 - ADAM_B1 ** ADAM_STEP)
        v_hat = nv / (1.0 - ADAM_B2 ** ADAM_STEP)
        d_ref[...] = -ADAM_LR * (m_hat / (jnp.sqrt(v_hat) + ADAM_EPS) + ADAM_WD * w_ref[...])
        nm_ref[...] = nm
        nv_ref[...] = nv

    spec = pl.BlockSpec((tr, cols), lambda i: (i, 0))
    out = jax.ShapeDtypeStruct((R, cols), F32)
    d, nm, nv = pl.pallas_call(body, name=name, grid=(R // tr,), in_specs=[spec] * 4, out_specs=[spec] * 3,
                               out_shape=[out] * 3, compiler_params=_params(("parallel",)))(w2, g2, m2, v2)
    return d.reshape(shape), nm.reshape(shape), nv.reshape(shape)


def _add_received(own, got, *, name):
    R, Cc = own.shape
    n = got.shape[0]
    tr = _tile(R, 256, 16)

    def body(a_ref, b_ref, o_ref):
        acc = a_ref[...].astype(F32)
        for k in range(n):
            acc = acc + b_ref[k].astype(F32)
        o_ref[...] = acc

    return pl.pallas_call(
        body, name=name, grid=(R // tr,),
        in_specs=[pl.BlockSpec((tr, Cc), lambda i: (i, 0)), pl.BlockSpec((n, tr, Cc), lambda i: (0, i, 0))],
        out_specs=pl.BlockSpec((tr, Cc), lambda i: (i, 0)), out_shape=jax.ShapeDtypeStruct((R, Cc), F32),
        compiler_params=_params(("parallel",)))(own, got)


def _place():
    return lax.axis_index("x"), lax.axis_index("y"), lax.axis_index("c")


def _all_gather(x, *, name, in_vmem, reduce_sum=False, with_token=False):
    R, Cc = x.shape
    space = pltpu.VMEM if in_vmem else pl.ANY

    def body(x_ref, out_ref, *scratch):
        if with_token:
            scratch[0][...] = jnp.zeros_like(scratch[0])
            scratch = scratch[1:]
        if reduce_sum:
            gat_ref, send_sems, recv_sems, local_sem = scratch
        else:
            gat_ref = out_ref
            send_sems, recv_sems, local_sem = scratch
        mx, my, mc = _place()
        me, sibling = (mx, my, mc), (mx, my, 1 - mc)
        chips = [(1 - mx, my), (mx, 1 - my), (1 - mx, 1 - my)]

        def rows(px, py, pc):
            return gat_ref.at[pl.ds((4 * px + 2 * py + pc) * R, R), :]

        def copy(k, block, to, src=None):
            return pltpu.make_async_remote_copy(
                src_ref=rows(*block) if src is None else src, dst_ref=rows(*block), send_sem=send_sems.at[k],
                recv_sem=recv_sems.at[k], device_id=to, device_id_type=MESH_ID)

        mine = pltpu.make_async_copy(x_ref, rows(*me), local_sem)
        mine.start()
        first = [copy(0, me, sibling, src=x_ref)]
        first += [copy(1 + j, me, (*chip, mc), src=x_ref) for j, chip in enumerate(chips)]
        for cp in first:
            cp.start()
        passed = [copy(4 + j, (*chip, mc), sibling) for j, chip in enumerate(chips)]
        for j, chip in enumerate(chips):
            copy(1 + j, (*chip, mc), me).wait_recv()
            passed[j].start()
        copy(0, sibling, me).wait_recv()
        for j, chip in enumerate(chips):
            copy(4 + j, (*chip, 1 - mc), me).wait_recv()
        for cp in first + passed:
            cp.wait_send()
        mine.wait()
        if reduce_sum:
            acc = gat_ref[pl.ds(0, R), :]
            for d in range(1, N_DEV):
                acc = acc + gat_ref[pl.ds(d * R, R), :]
            out_ref[...] = acc

    sems = [pltpu.SemaphoreType.DMA((7,)), pltpu.SemaphoreType.DMA((7,)), pltpu.SemaphoreType.DMA]
    if reduce_sum:
        assert in_vmem
        out_shape = jax.ShapeDtypeStruct((R, Cc), x.dtype)
        scratch = [pltpu.VMEM((N_DEV * R, Cc), x.dtype)] + sems
    else:
        out_shape = jax.ShapeDtypeStruct((N_DEV * R, Cc), x.dtype)
        scratch = sems
    out_specs = pl.BlockSpec(memory_space=space)
    if with_token:
        out_shape = (out_shape, jax.ShapeDtypeStruct((8, LANE), F32))
        out_specs = (out_specs, pl.BlockSpec(memory_space=pltpu.VMEM))
    return pl.pallas_call(
        body, name=name, out_shape=out_shape, in_specs=[pl.BlockSpec(memory_space=space)], out_specs=out_specs,
        scratch_shapes=scratch, compiler_params=pltpu.CompilerParams(vmem_limit_bytes=VMEM_LIMIT))(x)


def _peer(k, mx, my, mc):
    bits = k + 1
    return (1 - mx if bits & 4 else mx, 1 - my if bits & 2 else my, 1 - mc if bits & 1 else mc)


HBM_SPEC = pl.BlockSpec(memory_space=pltpu.HBM)
SEM_SPEC = pl.BlockSpec(memory_space=pltpu.SEMAPHORE)
DATAFLOW = pltpu.SideEffectType.DATAFLOW_SIDE_EFFECTING


def _exchange_copies(x_refs, land_refs, send_sems, recv_sems, scatter):
    mx, my, mc = _place()
    me = 4 * mx + 2 * my + mc
    n = len(x_refs)
    copies = []
    for k in range(N_DEV - 1):
        px, py, pc = _peer(k, mx, my, mc)
        for m, (x_ref, land_ref) in enumerate(zip(x_refs, land_refs)):
            rows = land_ref.shape[1] if scatter else x_ref.shape[0]
            if scatter:
                src = x_ref.at[pl.ds(pl.multiple_of((4 * px + 2 * py + pc) * rows, 16), rows), :]
                dst = land_ref.at[k]
            else:
                src = x_ref
                dst = land_ref.at[pl.ds(pl.multiple_of(me * rows, 16), rows), :]
            copies.append(pltpu.make_async_remote_copy(
                src_ref=src, dst_ref=dst, send_sem=send_sems.at[k * n + m], recv_sem=recv_sems.at[k * n + m],
                device_id=(px, py, pc), device_id_type=MESH_ID))
    return copies


def _land_shape(x, scatter):
    return (N_DEV - 1, x.shape[0] // N_DEV, x.shape[1]) if scatter else (N_DEV * x.shape[0], x.shape[1])


def _exchange_start(xs, *, name, scatter):
    n = len(xs)
    lands = [lax.empty(_land_shape(x, scatter), x.dtype) for x in xs]

    def body(*refs):
        x_refs, land_refs = refs[:n], refs[n:2 * n]
        send_sems, recv_sems = refs[2 * n:2 * n + 2]
        token = refs[-1]
        for cp in _exchange_copies(x_refs, land_refs, send_sems, recv_sems, scatter):
            cp.start()
        token[...] = jnp.zeros_like(token)

    sems = pltpu.SemaphoreType.DMA(((N_DEV - 1) * n,))
    out = pl.pallas_call(
        body, name=name,
        out_shape=(sems, sems, *[pltpu.HBM(x.shape, x.dtype) for x in xs],
                   *[pltpu.HBM(l.shape, l.dtype) for l in lands], jax.ShapeDtypeStruct((8, LANE), F32)),
        in_specs=(HBM_SPEC,) * (2 * n),
        out_specs=(SEM_SPEC, SEM_SPEC) + (HBM_SPEC,) * (2 * n) + (pl.BlockSpec(memory_space=pltpu.VMEM),),
        input_output_aliases={i: 2 + i for i in range(2 * n)},
        compiler_params=pltpu.CompilerParams(has_side_effects=DATAFLOW))(
            *[pltpu.with_memory_space_constraint(t, pltpu.HBM) for t in list(xs) + lands])
    return out[0], out[1], list(out[2:2 + n]), list(out[2 + n:2 + 2 * n]), out[-1]


def _exchange_wait(started, after, *, name, scatter):
    send_sems, recv_sems, xs, lands, _ = started
    n = len(xs)

    def body(*refs):
        x_refs, land_refs = refs[:n], refs[n:2 * n]
        send_sems, recv_sems = refs[2 * n:2 * n + 2]
        for cp in _exchange_copies(x_refs, land_refs, send_sems, recv_sems, scatter):
            cp.wait_send()
            cp.wait_recv()

    out = pl.pallas_call(
        body, name=name, out_shape=tuple(pltpu.HBM(t.shape, t.dtype) for t in xs + lands),
        in_specs=(HBM_SPEC,) * (2 * n) + (SEM_SPEC, SEM_SPEC, pl.BlockSpec(memory_space=pl.ANY)),
        out_specs=(HBM_SPEC,) * (2 * n), input_output_aliases={i: i for i in range(2 * n)},
        compiler_params=pltpu.CompilerParams(has_side_effects=DATAFLOW))(*xs, *lands, send_sems, recv_sems, after)
    return list(out[:n]), list(out[n:])


def _gather_start(shards, token, *, name):
    shards = [shards[0] + token[0, 0].astype(shards[0].dtype)] + list(shards[1:])
    return _exchange_start(shards, name=name, scatter=False)


def _gather_finish(started, after, me, *, name):
    xs, lands = _exchange_wait(started, after, name=name, scatter=False)
    return [lax.dynamic_update_slice(land, x, (me * x.shape[0], 0)) for land, x in zip(lands, xs)]


def _reduce_start(grads, *, name):
    return _exchange_start(grads, name=name, scatter=True)


def _reduce_finish(started, after, me, *, name):
    sent, gots = _exchange_wait(started, after, name=name + "_wait", scatter=True)
    out = []
    for m, (g, got) in enumerate(zip(sent, gots)):
        rows = g.shape[0] // N_DEV
        own = lax.dynamic_slice(g, (me * rows, 0), (rows, g.shape[1]))
        out.append(_add_received(own, got, name=f"{name}_add{m}"))
    return out


def _pad_rows(a, mult):
    r = (-a.shape[0]) % mult
    return a if r == 0 else jnp.concatenate([a, jnp.zeros((r,) + a.shape[1:], a.dtype)], axis=0)


def kernel(x, mem, mix_norm, mem_norm, w_mem_kv, w_out, hg_w_in, hg_lb, hg_onorm, gm_w_in, gm_ln_g, gm_ln_b, gm_ws, gm_bs, ffn_norm, w_ffn_in, w_ffn_out, final_norm, loss_target, m_mix_norm, m_mem_norm, m_w_mem_kv, m_w_out, m_hg_w_in, m_hg_lb, m_hg_onorm, m_gm_w_in, m_gm_ln_g, m_gm_ln_b, m_gm_ws, m_gm_bs, m_ffn_norm, m_w_ffn_in, m_w_ffn_out, m_final_norm, v_mix_norm, v_mem_norm, v_w_mem_kv, v_w_out, v_hg_w_in, v_hg_lb, v_hg_onorm, v_gm_w_in, v_gm_ln_g, v_gm_ln_b, v_gm_ws, v_gm_bs, v_ffn_norm, v_w_ffn_in, v_w_ffn_out, v_final_norm):
    mx, my, mc = _place()
    me = 4 * mx + 2 * my + mc
    xs = x[0]
    mems = mem[0]
    tgt = loss_target[0]

    hg_t = hg_w_in[0].T.astype(BF16)
    gm_t = gm_w_in[0].T.astype(BF16)
    fi_t = [w_ffn_in[i].T.astype(BF16) for i in range(2)]
    kv_b = [w_mem_kv[i].astype(BF16) for i in range(2)]
    out_b = [w_out[i].astype(BF16) for i in range(2)]
    fo_b = [w_ffn_out[i].astype(BF16) for i in range(2)]
    ln_local = _pad_rows(jnp.concatenate([gm_ln_g, gm_ln_b], axis=0), 8)
    ln_local = jnp.concatenate([ln_local, jnp.zeros((8, LANE - ln_local.shape[1]), F32)], axis=1)
    ln_all, token = _all_gather(ln_local, name="gather_ln", in_vmem=True, with_token=True)
    ln_all = ln_all.reshape(N_DEV, 8, LANE)
    ln_g = ln_all[:, 0, :D_TOK // N_DEV].reshape(1, D_TOK)
    ln_b = ln_all[:, 1, :D_TOK // N_DEV].reshape(1, D_TOK)
    W_hgT, token = _all_gather(hg_t + token[0, 0].astype(BF16), name="gather_first", in_vmem=False,
                               with_token=True)
    gather_l0 = _gather_start(kv_b + out_b + [fi_t[0], fo_b[0]], token, name="gather_l0_start")
    gather_l1 = _gather_start([gm_t, fi_t[1], fo_b[1]], gather_l0[4], name="gather_l1_start")

    lb_soft = jax.nn.softmax(hg_lb, axis=0)
    lb0 = lb_soft[0:1]
    bsb = jnp.broadcast_to(gm_bs[0][:, :, None], (N_HEADS, GM_CHUNK, GM_CHUNK))
    ws = gm_ws[0]

    def ffn_fwd(xin, i):
        hf = _rms_fwd(xin, ffn_norm[i:i + 1], name=f"ffn_norm{i}")
        gu, act = _ffn_in(hf, W_fiT[i], name=f"ffn_in{i}")
        xout = _matmul(act, W_fo[i], res=xin, name=f"ffn_out{i}")
        return hf, gu, act, xout

    h0 = _rms_fwd(xs, mix_norm[0:1], name="mix_norm0", dep=gather_l1[4])
    p0 = _matmul(h0, W_hgT, tb=True, name="hg_in")
    heads0, o0, states = _hgrn2_fwd(p0, lb0, hg_onorm, name="hgrn2_fwd")

    kv0, kv1, wo0, wo1, fi0, fo0 = _gather_finish(gather_l0, o0, me, name="gather_l0_wait")
    W_kv, W_out, W_fiT, W_fo = [kv0, kv1], [wo0, wo1], [fi0], [fo0]
    mem_n, kv = [], []
    for i in range(2):
        mn = _rms_fwd(mems, mem_norm[i:i + 1], name=f"mem_norm{i}")
        mem_n.append(mn)
        kv.append(_matmul(mn, W_kv[i], name=f"mem_kv{i}"))

    heads0 = _attn_fwd(p0, 4 * D_TOK // D_MEM, kv[0], heads0, name="attn_fwd0")
    x1 = _matmul(heads0, W_out[0], res=xs, name="out_proj0")
    hf0, gu0, act0, x2 = ffn_fwd(x1, 0)

    W_gmT, fi1, fo1 = _gather_finish(gather_l1, x2, me, name="gather_l1_wait")
    W_fiT.append(fi1)
    W_fo.append(fo1)
    h1 = _rms_fwd(x2, mix_norm[1:2], name="mix_norm1")
    p1 = _matmul(h1, W_gmT, tb=True, name="gm_in")
    heads1 = _gmlp_fwd(p1, ln_g, ln_b, ws, bsb, name="gmlp_fwd")
    heads1 = _attn_fwd(p1, 2 * D_TOK // D_MEM, kv[1], heads1, name="attn_fwd1")
    x3 = _matmul(heads1, W_out[1], res=x2, name="out_proj1")
    hf1, gu1, act1, x4 = ffn_fwd(x3, 1)

    dx, g_final, loss_part = _final_loss(x4, final_norm.reshape(1, D_MODEL), tgt, name="final_loss")

    def ffn_bwd(dx, xin, hf, gu, act, i, dep):
        dgu = _ffn_out_dx(dx, W_fo[i], gu, dep, name=f"ffn_out_dx{i}")
        g_wfo = _matmul(act, dx, ta=True, out_dtype=BF16, name=f"ffn_out_dw{i}")
        g_wfi_t = _matmul(dgu, hf, ta=True, a_halves=True, out_dtype=BF16, name=f"ffn_in_dw{i}")
        dhf = _matmul(dgu, W_fiT[i], a_halves=True, name=f"ffn_in_dx{i}")
        dx, g_norm = _rms_bwd(xin, ffn_norm[i:i + 1], dhf, dx, name=f"ffn_norm_bwd{i}")
        return dx, g_wfi_t, g_wfo, g_norm

    def mem_bwd(dkv, i):
        g_wkv = _matmul(mem_n[i], dkv, ta=True, out_dtype=BF16, name=f"mem_kv_dw{i}")
        dmn = _matmul(dkv, W_kv[i], tb=True, name=f"mem_kv_dx{i}")
        _, g_norm = _rms_bwd(mems, mem_norm[i:i + 1], dmn, jnp.zeros_like(mems), name=f"mem_norm_bwd{i}")
        return g_wkv, g_norm

    dx, g_wfi1_t, g_wfo1, g_ffn1 = ffn_bwd(dx, x3, hf1, gu1, act1, 1, loss_part)
    dheads = _matmul(dx, W_out[1], tb=True, name="out_proj_dx1")
    g_wout1 = _matmul(heads1, dx, ta=True, out_dtype=BF16, name="out_proj_dw1")
    dp, g_ws, g_bs, g_lng, g_lnb = _gmlp_bwd(p1, ln_g, ln_b, ws, bsb, dheads, name="gmlp_bwd")
    dp, dk, dv = _attn_bwd(p1, 2 * D_TOK // D_MEM, kv[1], dheads, dp, name="attn_bwd1")
    g_wkv1, g_mem1 = mem_bwd(jnp.concatenate([dk, dv], axis=1), 1)
    g_wgm_t = _matmul(dp, h1, ta=True, out_dtype=BF16, name="gm_in_dw")
    dh = _matmul(dp, W_gmT, name="gm_in_dx")
    dx, g_mix1 = _rms_bwd(x2, mix_norm[1:2], dh, dx, name="mix_norm_bwd1")
    reduce_l1 = _reduce_start([g_wkv1, g_wout1, g_wgm_t, g_wfi1_t, g_wfo1], name="reduce_l1_start")

    dx, g_wfi0_t, g_wfo0, g_ffn0 = ffn_bwd(dx, x1, hf0, gu0, act0, 0, reduce_l1[4])
    reduce_ffn0 = _reduce_start([g_wfi0_t, g_wfo0], name="reduce_ffn0_start")
    dheads = _matmul(dx, W_out[0], tb=True, name="out_proj_dx0", dep=reduce_ffn0[4])
    g_wout0 = _matmul(heads0, dx, ta=True, out_dtype=BF16, name="out_proj_dw0")
    dp, g_lb0, g_onorm = _hgrn2_bwd(p0, lb0, hg_onorm, o0, states, dheads, name="hgrn2_bwd")
    dp, dk, dv = _attn_bwd(p0, 4 * D_TOK // D_MEM, kv[0], dheads, dp, name="attn_bwd0")
    g_wkv0, g_mem0 = mem_bwd(jnp.concatenate([dk, dv], axis=1), 0)
    g_whg_t = _matmul(dp, h0, ta=True, out_dtype=BF16, name="hg_in_dw")
    reduce_mix0 = _reduce_start([g_wkv0, g_wout0, g_whg_t], name="reduce_mix0_start")
    dh = _matmul(dp, W_hgT, name="hg_in_dx", dep=reduce_mix0[4])
    grad_x, g_mix0 = _rms_bwd(xs, mix_norm[0:1], dh, dx, name="mix_norm_bwd0")

    g_kv1, g_out1, g_gm_t, g_fi1_t, g_fo1 = _reduce_finish(reduce_l1, grad_x, me, name="reduce_l1")
    g_fi0_t, g_fo0 = _reduce_finish(reduce_ffn0, g_kv1, me, name="reduce_ffn0")
    g_kv0, g_out0, g_hg_t = _reduce_finish(reduce_mix0, g_fi0_t, me, name="reduce_mix0")
    g_shards = [jnp.stack([g_kv0, g_kv1]), jnp.stack([g_out0, g_out1]), g_hg_t.T[None], g_gm_t.T[None],
                jnp.stack([g_fi0_t.T, g_fi1_t.T]), jnp.stack([g_fo0, g_fo1])]

    small = [loss_part, jnp.concatenate([g_mix0, g_mix1], axis=1), jnp.concatenate([g_mem0, g_mem1], axis=1),
             g_lb0, g_onorm, g_lng, g_lnb, g_ws.reshape(1, -1), g_bs.reshape(1, -1),
             jnp.concatenate([g_ffn0, g_ffn1], axis=1), g_final]
    sizes = [t.shape[1] for t in small]
    small_rows = _pad_rows(jnp.concatenate(small, axis=1).reshape(-1, LANE), 8)
    red = _all_gather(small_rows, name="reduce_small", in_vmem=True, reduce_sum=True).reshape(-1)
    pieces, off = [], 0
    for n in sizes:
        pieces.append(red[off:off + n])
        off += n
    loss = pieces[0][0]
    g_mix_norm = pieces[1].reshape(2, D_MODEL)
    g_mem_norm = pieces[2].reshape(2, D_MODEL)
    g_hg_lb = pieces[3][None, :] * lb0 * (jnp.eye(3, dtype=F32)[:, 0:1] - lb_soft)
    g_hg_onorm = pieces[4].reshape(1, D_TOK)
    width = D_TOK // N_DEV
    g_gm_ln_g = lax.dynamic_slice(pieces[5], (me * width,), (width,)).reshape(1, width)
    g_gm_ln_b = lax.dynamic_slice(pieces[6], (me * width,), (width,)).reshape(1, width)
    g_gm_ws = pieces[7].reshape(gm_ws.shape)
    g_gm_bs = pieces[8].reshape(gm_bs.shape)
    g_ffn_norm = pieces[9].reshape(2, D_MODEL)
    g_final_norm = pieces[10]

    grads = [g_mix_norm, g_mem_norm, g_shards[0], g_shards[1], g_shards[2], g_hg_lb, g_hg_onorm, g_shards[3],
             g_gm_ln_g, g_gm_ln_b, g_gm_ws, g_gm_bs, g_ffn_norm, g_shards[4], g_shards[5], g_final_norm]
    weights = [mix_norm, mem_norm, w_mem_kv, w_out, hg_w_in, hg_lb, hg_onorm, gm_w_in, gm_ln_g, gm_ln_b, gm_ws, gm_bs,
               ffn_norm, w_ffn_in, w_ffn_out, final_norm]
    ms = [m_mix_norm, m_mem_norm, m_w_mem_kv, m_w_out, m_hg_w_in, m_hg_lb, m_hg_onorm, m_gm_w_in, m_gm_ln_g,
          m_gm_ln_b, m_gm_ws, m_gm_bs, m_ffn_norm, m_w_ffn_in, m_w_ffn_out, m_final_norm]
    vs = [v_mix_norm, v_mem_norm, v_w_mem_kv, v_w_out, v_hg_w_in, v_hg_lb, v_hg_onorm, v_gm_w_in, v_gm_ln_g,
          v_gm_ln_b, v_gm_ws, v_gm_bs, v_ffn_norm, v_w_ffn_in, v_w_ffn_out, v_final_norm]
    deltas, new_m, new_v = [], [], []
    for n, (w, g, m, v) in enumerate(zip(weights, grads, ms, vs)):
        if w.ndim == 1:
            d, nm, nv = _adamw(w[None], g.reshape(1, -1), m[None], v[None], name=f"adamw{n}")
            d, nm, nv = d[0], nm[0], nv[0]
        else:
            d, nm, nv = _adamw(w, g.reshape(w.shape), m, v, name=f"adamw{n}")
        deltas.append(d)
        new_m.append(nm)
        new_v.append(nv)
    grads = [g.reshape(w.shape) for g, w in zip(grads, weights)]
    return (loss, grad_x[None], *grads, *deltas, *new_m, *new_v)
```

```python
import functools

import jax
import jax.numpy as jnp
from jax import lax
from jax.experimental import pallas as pl
from jax.experimental.pallas import tpu as pltpu

F32 = jnp.float32
BF16 = jnp.bfloat16
MXU_DTYPE = jnp.bfloat16
MESH_ID = pl.DeviceIdType.MESH

N_DEV = 8
EPS = 1e-6
D_MODEL = 1024
D_TOK = 768
D_MEM = 256
N_HEADS = 6
HEAD = 128
MEM_HEADS = 4
MEM_HDIM = 64
GM_CHUNK = 128
D_FF = 2816
HG_SUB = 16
HG_IN = 4 * D_TOK + D_MEM
GM_IN = 2 * D_TOK + D_MEM
LANE = 128

ADAM_LR = 0.001
ADAM_B1 = 0.9
ADAM_B2 = 0.999
ADAM_EPS = 1e-08
ADAM_WD = 0.01
ADAM_STEP = 10

VMEM_LIMIT = 48 * 2 ** 20


def _params(sem=None):
    return pltpu.CompilerParams(dimension_semantics=sem, vmem_limit_bytes=VMEM_LIMIT)


def _tile(n, cap, q=LANE):
    if n <= cap:
        return n
    best = None
    for t in range(q, cap + 1, q):
        if n % t == 0:
            best = t
    assert best is not None, (n, cap, q)
    return best


def _sigmoid(x):
    return 1.0 / (1.0 + jnp.exp(-x))


def _gelu(x):
    return 0.5 * x * (1.0 + lax.erf(x * 0.7071067811865476))


def _gelu_grad(x):
    return 0.5 * (1.0 + lax.erf(x * 0.7071067811865476)) + x * jnp.exp(-0.5 * x * x) * 0.3989422804014327


def _matmul(a, b, *, name, ta=False, tb=False, res=None, out_dtype=F32, a_halves=False, b_halves=False, dep=None):
    if a_halves and ta:
        K, M = a.shape[1], 2 * a.shape[2]
    elif a_halves:
        M, K = a.shape[1], 2 * a.shape[2]
    else:
        K, M = a.shape if ta else a.shape[::-1]
    if b_halves:
        assert not tb and b.shape[1] == K
        N = 2 * b.shape[2]
    else:
        N = b.shape[0] if tb else b.shape[1]
        assert (b.shape[1] if tb else b.shape[0]) == K
    tm = _tile(M // 2 if (a_halves and ta) else M, 1664 if ta else 1024)
    tn = _tile(N // 2 if b_halves else N, 1792)
    tk = _tile(K // 2 if (a_halves and not ta) else K, 1024 if ta else 1664)
    nk = K // tk
    dims = (((0 if ta else 1,), (1 if tb else 0,)), ((), ()))

    def body(*refs):
        a_ref, b_ref = refs[:2]
        r_ref = refs[2] if res is not None else None
        o_ref, acc = refs[-2:]
        k = pl.program_id(2)

        @pl.when(k == 0)
        def _():
            acc[...] = jnp.zeros_like(acc)

        acc[...] += lax.dot_general(a_ref[...].astype(MXU_DTYPE), b_ref[...].astype(MXU_DTYPE), dims,
                                    preferred_element_type=F32)

        @pl.when(k == nk - 1)
        def _():
            r = acc[...]
            if res is not None:
                r = r + r_ref[...].astype(F32)
            o_ref[...] = r.astype(out_dtype)

    if a_halves and ta:
        mh = M // 2 // tm
        a_spec = pl.BlockSpec((None, tk, tm), lambda i, j, k: (i // mh, k, i % mh))
    elif a_halves:
        kh = nk // 2
        a_spec = pl.BlockSpec((None, tm, tk), lambda i, j, k: (k // kh, i, k % kh))
    elif ta:
        a_spec = pl.BlockSpec((tk, tm), lambda i, j, k: (k, i))
    else:
        a_spec = pl.BlockSpec((tm, tk), lambda i, j, k: (i, k))
    if b_halves:
        nh = N // 2 // tn
        b_spec = pl.BlockSpec((None, tk, tn), lambda i, j, k: (j // nh, k, j % nh))
    elif tb:
        b_spec = pl.BlockSpec((tn, tk), lambda i, j, k: (j, k))
    else:
        b_spec = pl.BlockSpec((tk, tn), lambda i, j, k: (k, j))
    o_spec = pl.BlockSpec((tm, tn), lambda i, j, k: (i, j))
    in_specs = [a_spec, b_spec] + ([o_spec] if res is not None else [])
    args = (a, b) + ((res,) if res is not None else ())
    if dep is not None:
        in_specs.append(pl.BlockSpec(memory_space=pl.ANY))
        args += (dep,)
    return pl.pallas_call(
        body, name=name, grid=(M // tm, N // tn, nk), in_specs=in_specs, out_specs=o_spec,
        out_shape=jax.ShapeDtypeStruct((M, N), out_dtype), scratch_shapes=[pltpu.VMEM((tm, tn), F32)],
        compiler_params=_params(("parallel", "parallel", "arbitrary")))(*args)


def _ffn_in(hf, wt, *, name):
    S, K = hf.shape
    tm = _tile(S, 512)
    tn = _tile(D_FF, 1408)
    nh = D_FF // tn
    nt = (((1,), (1,)), ((), ()))

    def body(a_ref, bg_ref, bu_ref, gu_ref, act_ref):
        av = a_ref[...].astype(MXU_DTYPE)
        gate = lax.dot_general(av, bg_ref[...].astype(MXU_DTYPE), nt, preferred_element_type=F32)
        up = lax.dot_general(av, bu_ref[...].astype(MXU_DTYPE), nt, preferred_element_type=F32)
        gu_ref[0] = gate.astype(gu_ref.dtype)
        gu_ref[1] = up.astype(gu_ref.dtype)
        act_ref[...] = (gate * _sigmoid(gate) * up).astype(act_ref.dtype)

    return pl.pallas_call(
        body, name=name, grid=(S // tm, nh),
        in_specs=[pl.BlockSpec((tm, K), lambda i, j: (i, 0)), pl.BlockSpec((tn, K), lambda i, j: (j, 0)),
                  pl.BlockSpec((tn, K), lambda i, j: (j + nh, 0))],
        out_specs=[pl.BlockSpec((2, tm, tn), lambda i, j: (0, i, j)), pl.BlockSpec((tm, tn), lambda i, j: (i, j))],
        out_shape=[jax.ShapeDtypeStruct((2, S, D_FF), BF16), jax.ShapeDtypeStruct((S, D_FF), BF16)],
        compiler_params=_params(("parallel", "parallel")))(hf, wt, wt)


def _ffn_out_dx(dx, w, gu, dep, *, name):
    S, K = dx.shape
    tm = _tile(S, 512)
    tn = _tile(D_FF, 1408)

    def body(a_ref, b_ref, gu_ref, dep_ref, o_ref):
        del dep_ref
        da = lax.dot_general(a_ref[...].astype(MXU_DTYPE), b_ref[...].astype(MXU_DTYPE), (((1,), (1,)), ((), ())),
                             preferred_element_type=F32)
        gate = gu_ref[0].astype(F32)
        up = gu_ref[1].astype(F32)
        sg = _sigmoid(gate)
        o_ref[0] = (da * up * sg * (1.0 + gate * (1.0 - sg))).astype(o_ref.dtype)
        o_ref[1] = (da * gate * sg).astype(o_ref.dtype)

    halves = pl.BlockSpec((2, tm, tn), lambda i, j: (0, i, j))
    return pl.pallas_call(
        body, name=name, grid=(S // tm, D_FF // tn),
        in_specs=[pl.BlockSpec((tm, K), lambda i, j: (i, 0)), pl.BlockSpec((tn, K), lambda i, j: (j, 0)), halves,
                  pl.BlockSpec(memory_space=pl.ANY)],
        out_specs=halves, out_shape=jax.ShapeDtypeStruct((2, S, D_FF), BF16),
        compiler_params=_params(("parallel", "parallel")))(dx, w, gu, dep)


def _rms_fwd(x, g, *, name, dep=None):
    R, Dm = x.shape
    tr = _tile(R, 512, 8)

    def body(x_ref, g_ref, *rest):
        o_ref = rest[-1]
        xv = x_ref[...]
        r = lax.rsqrt(jnp.mean(xv * xv, axis=-1, keepdims=True) + EPS)
        o_ref[...] = (xv * r * g_ref[...]).astype(o_ref.dtype)

    in_specs = [pl.BlockSpec((tr, Dm), lambda i: (i, 0)), pl.BlockSpec((1, Dm), lambda i: (0, 0))]
    args = (x, g)
    if dep is not None:
        in_specs.append(pl.BlockSpec(memory_space=pl.ANY))
        args += (dep,)
    return pl.pallas_call(
        body, name=name, grid=(R // tr,), in_specs=in_specs,
        out_specs=pl.BlockSpec((tr, Dm), lambda i: (i, 0)), out_shape=jax.ShapeDtypeStruct((R, Dm), BF16),
        compiler_params=_params(("parallel",)))(*args)


def _rms_bwd(x, g, dh, dres, *, name):
    R, Dm = x.shape
    tr = _tile(R, 256, 8)

    def body(x_ref, g_ref, dh_ref, dres_ref, dx_ref, dg_ref):
        @pl.when(pl.program_id(0) == 0)
        def _():
            dg_ref[...] = jnp.zeros_like(dg_ref)

        xv = x_ref[...]
        r = lax.rsqrt(jnp.mean(xv * xv, axis=-1, keepdims=True) + EPS)
        xh = xv * r
        dhv = dh_ref[...].astype(F32)
        dg_ref[...] += jnp.sum(dhv * xh, axis=0, keepdims=True)
        u = dhv * g_ref[...]
        dx = r * (u - xh * jnp.mean(u * xh, axis=-1, keepdims=True))
        dx_ref[...] = dres_ref[...] + dx

    row = pl.BlockSpec((tr, Dm), lambda i: (i, 0))
    vec = pl.BlockSpec((1, Dm), lambda i: (0, 0))
    return pl.pallas_call(
        body, name=name, grid=(R // tr,), in_specs=[row, vec, row, row], out_specs=[row, vec],
        out_shape=[jax.ShapeDtypeStruct((R, Dm), F32), jax.ShapeDtypeStruct((1, Dm), F32)],
        compiler_params=_params(("arbitrary",)))(x, g, dh, dres)


def _final_loss(x, g, tgt, *, name):
    R, Dm = x.shape
    tr = _tile(R, 256, 8)

    def body(x_ref, g_ref, t_ref, dx_ref, dg_ref, loss_ref):
        @pl.when(pl.program_id(0) == 0)
        def _():
            dg_ref[...] = jnp.zeros_like(dg_ref)
            loss_ref[...] = jnp.zeros_like(loss_ref)

        xv = x_ref[...]
        r = lax.rsqrt(jnp.mean(xv * xv, axis=-1, keepdims=True) + EPS)
        xh = xv * r
        gv = g_ref[...]
        err = xh * gv - t_ref[...]
        part = 0.5 * jnp.sum(jnp.mean(err * err, axis=-1, keepdims=True), axis=0, keepdims=True)
        loss_ref[...] += jnp.broadcast_to(part, loss_ref.shape)
        dy = err * (1.0 / Dm)
        dg_ref[...] += jnp.sum(dy * xh, axis=0, keepdims=True)
        u = dy * gv
        dx_ref[...] = r * (u - xh * jnp.mean(u * xh, axis=-1, keepdims=True))

    row = pl.BlockSpec((tr, Dm), lambda i: (i, 0))
    vec = pl.BlockSpec((1, Dm), lambda i: (0, 0))
    one = pl.BlockSpec((1, LANE), lambda i: (0, 0))
    return pl.pallas_call(
        body, name=name, grid=(R // tr,), in_specs=[row, vec, row], out_specs=[row, vec, one],
        out_shape=[jax.ShapeDtypeStruct((R, Dm), F32), jax.ShapeDtypeStruct((1, Dm), F32),
                   jax.ShapeDtypeStruct((1, LANE), F32)],
        compiler_params=_params(("arbitrary",)))(x, g, tgt)


def _head_mask(h):
    lane = lax.broadcasted_iota(jnp.int32, (1, D_MEM), 1)
    return (lane >= h * MEM_HDIM) & (lane < (h + 1) * MEM_HDIM)


def _attn_probs(qv, k_mx, mask):
    s = lax.dot_general(jnp.where(mask, qv, 0.0).astype(MXU_DTYPE), k_mx, (((1,), (1,)), ((), ())),
                        preferred_element_type=F32) * (MEM_HDIM ** -0.5)
    e = jnp.exp(s - jnp.max(s, axis=-1, keepdims=True))
    return e / jnp.sum(e, axis=-1, keepdims=True)


def _attn_fwd(p, qcol, kv, heads, *, name):
    S = p.shape[0]
    M = kv.shape[0]
    ts = _tile(S, 512, 8)

    def body(q_ref, k_ref, v_ref, heads_in, o_ref):
        del heads_in
        qv = q_ref[...]
        kx = k_ref[...].astype(MXU_DTYPE)
        vv = v_ref[...]
        out = jnp.zeros((ts, D_MEM), F32)
        for h in range(MEM_HEADS):
            mask = _head_mask(h)
            pr = _attn_probs(qv, kx, mask)
            out = out + jnp.dot(pr.astype(MXU_DTYPE), jnp.where(mask, vv, 0.0).astype(MXU_DTYPE),
                                preferred_element_type=F32)
        o_ref[...] = out.astype(o_ref.dtype)

    return pl.pallas_call(
        body, name=name, grid=(S // ts,),
        in_specs=[pl.BlockSpec((ts, D_MEM), lambda i: (i, qcol)), pl.BlockSpec((M, D_MEM), lambda i: (0, 0)),
                  pl.BlockSpec((M, D_MEM), lambda i: (0, 1)), pl.BlockSpec(memory_space=pl.ANY)],
        out_specs=pl.BlockSpec((ts, D_MEM), lambda i: (i, D_TOK // D_MEM)),
        out_shape=jax.ShapeDtypeStruct(heads.shape, heads.dtype), input_output_aliases={3: 0},
        compiler_params=_params(("parallel",)))(p, kv, kv, heads)


def _attn_bwd(p, qcol, kv, dheads, dp, *, name):
    S = p.shape[0]
    M = kv.shape[0]
    ts = _tile(S, 512, 8)
    scale = MEM_HDIM ** -0.5

    def body(q_ref, k_ref, v_ref, do_ref, dp_in, dq_ref, dk_ref, dv_ref):
        del dp_in

        @pl.when(pl.program_id(0) == 0)
        def _():
            dk_ref[...] = jnp.zeros_like(dk_ref)
            dv_ref[...] = jnp.zeros_like(dv_ref)

        qv = q_ref[...]
        kv_ = k_ref[...]
        kx = kv_.astype(MXU_DTYPE)
        vv = v_ref[...]
        dox = do_ref[...].astype(MXU_DTYPE)
        qx = qv.astype(MXU_DTYPE)
        dq = jnp.zeros((ts, D_MEM), F32)
        for h in range(MEM_HEADS):
            mask = _head_mask(h)
            pr = _attn_probs(qv, kx, mask)
            vh = jnp.where(mask, vv, 0.0).astype(MXU_DTYPE)
            dpr = lax.dot_general(dox, vh, (((1,), (1,)), ((), ())), preferred_element_type=F32)
            ds = (pr * (dpr - jnp.sum(dpr * pr, axis=-1, keepdims=True)) * scale).astype(MXU_DTYPE)
            dq = dq + jnp.dot(ds, jnp.where(mask, kv_, 0.0).astype(MXU_DTYPE), preferred_element_type=F32)
            dk_h = lax.dot_general(ds, qx, (((0,), (0,)), ((), ())), preferred_element_type=F32)
            dv_h = lax.dot_general(pr.astype(MXU_DTYPE), dox, (((0,), (0,)), ((), ())), preferred_element_type=F32)
            dk_ref[...] += jnp.where(mask, dk_h, 0.0)
            dv_ref[...] += jnp.where(mask, dv_h, 0.0)
        dq_ref[...] = dq.astype(dq_ref.dtype)

    return pl.pallas_call(
        body, name=name, grid=(S // ts,),
        in_specs=[pl.BlockSpec((ts, D_MEM), lambda i: (i, qcol)), pl.BlockSpec((M, D_MEM), lambda i: (0, 0)),
                  pl.BlockSpec((M, D_MEM), lambda i: (0, 1)),
                  pl.BlockSpec((ts, D_MEM), lambda i: (i, D_TOK // D_MEM)), pl.BlockSpec(memory_space=pl.ANY)],
        out_specs=[pl.BlockSpec((ts, D_MEM), lambda i: (i, qcol)), pl.BlockSpec((M, D_MEM), lambda i: (0, 0)),
                   pl.BlockSpec((M, D_MEM), lambda i: (0, 0))],
        out_shape=[jax.ShapeDtypeStruct(dp.shape, dp.dtype), jax.ShapeDtypeStruct((M, D_MEM), F32),
                   jax.ShapeDtypeStruct((M, D_MEM), F32)],
        input_output_aliases={4: 0}, compiler_params=_params(("arbitrary",)))(p, kv, kv, dheads, dp)


def _gm_forward_parts(u_ref, v_ref, lng_ref, lnb_ref, w_ref, bsb_ref):
    zu = _gelu(u_ref[...])
    zv = _gelu(v_ref[...])
    mu = jnp.mean(zv, axis=-1, keepdims=True)
    cen = zv - mu
    rs = lax.rsqrt(jnp.mean(cen * cen, axis=-1, keepdims=True) + EPS)
    vh = cen * rs
    vn = vh * lng_ref[...] + lnb_ref[...]
    row = lax.broadcasted_iota(jnp.int32, (GM_CHUNK, GM_CHUNK), 0)
    col = lax.broadcasted_iota(jnp.int32, (GM_CHUNK, GM_CHUNK), 1)
    tril = row >= col
    wm = [jnp.where(tril, w_ref[g], 0.0).astype(MXU_DTYPE) for g in range(N_HEADS)]
    vnx = [vn[:, g * HEAD:(g + 1) * HEAD].astype(MXU_DTYPE) for g in range(N_HEADS)]
    sv = [jnp.dot(wm[g], vnx[g], preferred_element_type=F32) + bsb_ref[g] for g in range(N_HEADS)]
    return zu, vh, rs, wm, vnx, sv, tril


def _gmlp_fwd(p, lng, lnb, ws, bsb, *, name):
    S = p.shape[0]

    def body(u_ref, v_ref, lng_ref, lnb_ref, w_ref, bsb_ref, o_ref):
        zu, _, _, _, _, sv, _ = _gm_forward_parts(u_ref, v_ref, lng_ref, lnb_ref, w_ref, bsb_ref)
        for g in range(N_HEADS):
            o_ref[:, g * HEAD:(g + 1) * HEAD] = (zu[:, g * HEAD:(g + 1) * HEAD] * sv[g]).astype(o_ref.dtype)

    blk = lambda c: pl.BlockSpec((GM_CHUNK, D_TOK), lambda i: (i, c))
    vec = pl.BlockSpec((1, D_TOK), lambda i: (0, 0))
    cube = pl.BlockSpec((N_HEADS, GM_CHUNK, GM_CHUNK), lambda i: (0, 0, 0))
    return pl.pallas_call(
        body, name=name, grid=(S // GM_CHUNK,), in_specs=[blk(0), blk(1), vec, vec, cube, cube],
        out_specs=blk(0), out_shape=jax.ShapeDtypeStruct((S, D_MODEL), BF16),
        compiler_params=_params(("parallel",)))(p, p, lng, lnb, ws, bsb)


def _gmlp_bwd(p, lng, lnb, ws, bsb, dheads, *, name):
    S = p.shape[0]

    def body(u_ref, v_ref, lng_ref, lnb_ref, w_ref, bsb_ref, dt_ref, dp_ref, dw_ref, dbs_ref, dlg_ref, dlb_ref):
        @pl.when(pl.program_id(0) == 0)
        def _():
            dw_ref[...] = jnp.zeros_like(dw_ref)
            dbs_ref[...] = jnp.zeros_like(dbs_ref)
            dlg_ref[...] = jnp.zeros_like(dlg_ref)
            dlb_ref[...] = jnp.zeros_like(dlb_ref)

        zu, vh, rs, wm, vnx, sv, tril = _gm_forward_parts(u_ref, v_ref, lng_ref, lnb_ref, w_ref, bsb_ref)
        dt = dt_ref[...].astype(F32)
        dvn_parts = []
        for g in range(N_HEADS):
            sl = slice(g * HEAD, (g + 1) * HEAD)
            dsv = dt[:, sl] * zu[:, sl]
            dp_ref[:, sl] = (dt[:, sl] * sv[g] * _gelu_grad(u_ref[:, sl])).astype(dp_ref.dtype)
            dsx = dsv.astype(MXU_DTYPE)
            dw = lax.dot_general(dsx, vnx[g], (((1,), (1,)), ((), ())), preferred_element_type=F32)
            dw_ref[g] += jnp.where(tril, dw, 0.0)
            dbs_ref[g] += jnp.sum(dsv, axis=-1, keepdims=True)
            dvn_parts.append(lax.dot_general(wm[g], dsx, (((0,), (0,)), ((), ())), preferred_element_type=F32))
        dvn = jnp.concatenate(dvn_parts, axis=-1)
        dlg_ref[...] += jnp.sum(dvn * vh, axis=0, keepdims=True)
        dlb_ref[...] += jnp.sum(dvn, axis=0, keepdims=True)
        dvh = dvn * lng_ref[...]
        dzv = rs * (dvh - jnp.mean(dvh, axis=-1, keepdims=True) - vh * jnp.mean(dvh * vh, axis=-1, keepdims=True))
        dp_ref[:, D_TOK:] = (dzv * _gelu_grad(v_ref[...])).astype(dp_ref.dtype)

    blk = lambda c: pl.BlockSpec((GM_CHUNK, D_TOK), lambda i: (i, c))
    vec = pl.BlockSpec((1, D_TOK), lambda i: (0, 0))
    cube = pl.BlockSpec((N_HEADS, GM_CHUNK, GM_CHUNK), lambda i: (0, 0, 0))
    col = pl.BlockSpec((N_HEADS, GM_CHUNK, 1), lambda i: (0, 0, 0))
    return pl.pallas_call(
        body, name=name, grid=(S // GM_CHUNK,), in_specs=[blk(0), blk(1), vec, vec, cube, cube, blk(0)],
        out_specs=[pl.BlockSpec((GM_CHUNK, 2 * D_TOK), lambda i: (i, 0)), cube, col, vec, vec],
        out_shape=[jax.ShapeDtypeStruct((S, GM_IN), BF16), jax.ShapeDtypeStruct((N_HEADS, GM_CHUNK, GM_CHUNK), F32),
                   jax.ShapeDtypeStruct((N_HEADS, GM_CHUNK, 1), F32), jax.ShapeDtypeStruct((1, D_TOK), F32),
                   jax.ShapeDtypeStruct((1, D_TOK), F32)],
        compiler_params=_params(("arbitrary",)))(p, p, lng, lnb, ws, bsb, dheads)


def _chunk_tri(n, chunk, upper):
    r = lax.broadcasted_iota(jnp.int32, (n, n), 0)
    c = lax.broadcasted_iota(jnp.int32, (n, n), 1)
    same = (r // chunk) == (c // chunk)
    return jnp.where(same & ((r <= c) if upper else (r >= c)), 1.0, 0.0).astype(F32)


def _hg_gates(fz, lb):
    sg = _sigmoid(fz)
    f = lb + (1.0 - lb) * sg
    kk = (1.0 - lb) * (1.0 - sg)
    return sg, f, jnp.log(f), kk


def _hgrn2_fwd(p, lb, onorm, *, name):
    S = p.shape[0]
    C = HG_SUB
    tb = _tile(S, 256, C)
    nsub = tb // C

    def body(q_ref, fz_ref, v_ref, g_ref, lb_ref, on_ref, tok_ref, o_ref, st_ref, state, b_blk, k_blk, bsc, ksc, vsc):
        @pl.when(pl.program_id(0) == 0)
        def _():
            state[...] = jnp.zeros_like(state)

        _, _, lg, kk = _hg_gates(fz_ref[...], lb_ref[...])
        b_blk[...] = jnp.dot(_chunk_tri(tb, C, False), lg, precision=lax.Precision.HIGHEST,
                             preferred_element_type=F32)
        k_blk[...] = kk
        tt = lax.broadcasted_iota(jnp.int32, (C, HEAD), 0)

        def sub(c, carry):
            rows = pl.ds(pl.multiple_of(c * C, C), C)
            for h in range(N_HEADS):
                cols = slice(h * HEAD, (h + 1) * HEAD)
                qv = q_ref[rows, cols]
                vv = v_ref[rows, cols]
                b = b_blk[rows, cols]
                kk = k_blk[rows, cols]
                st0 = state[h]
                st_ref[c, h] = st0
                inter = lax.dot_general((qv * jnp.exp(b)).astype(MXU_DTYPE), st0.astype(MXU_DTYPE),
                                        (((1,), (1,)), ((), ())), preferred_element_type=F32)
                bsc[h] = b
                ksc[h] = kk
                vsc[h] = vv
                intra = jnp.zeros((C, HEAD), F32)
                for s in range(C):
                    dec = jnp.where(tt >= s, jnp.exp(b - bsc[h, pl.ds(s, 1), :]), 0.0)
                    a_s = jnp.sum(qv * ksc[h, pl.ds(s, 1), :] * dec, axis=-1, keepdims=True)
                    intra = intra + a_s * vsc[h, pl.ds(s, 1), :]
                o_ref[rows, cols] = inter + intra
                b_last = bsc[h, pl.ds(C - 1, 1), :]
                ke = kk * jnp.exp(b_last - b)
                state[h] = st0 * jnp.exp(b_last) + lax.dot_general(
                    vv.astype(MXU_DTYPE), ke.astype(MXU_DTYPE), (((0,), (0,)), ((), ())),
                    preferred_element_type=F32)
            return carry

        lax.fori_loop(0, nsub, sub, 0)

        for h in range(N_HEADS):
            cols = slice(h * HEAD, (h + 1) * HEAD)
            o = o_ref[:, cols]
            gv = g_ref[:, cols]
            n = o * lax.rsqrt(jnp.mean(o * o, axis=-1, keepdims=True) + EPS)
            tok_ref[:, cols] = (n * (gv * _sigmoid(gv)) * on_ref[:, cols]).astype(tok_ref.dtype)

    blk = lambda c: pl.BlockSpec((tb, D_TOK), lambda i, c=c: (i, c))
    vec = pl.BlockSpec((1, D_TOK), lambda i: (0, 0))
    stb = pl.BlockSpec((nsub, N_HEADS, HEAD, HEAD), lambda i: (i, 0, 0, 0))
    return pl.pallas_call(
        body, name=name, grid=(S // tb,), in_specs=[blk(0), blk(1), blk(2), blk(3), vec, vec],
        out_specs=[blk(0), blk(0), stb],
        out_shape=[jax.ShapeDtypeStruct((S, D_MODEL), BF16), jax.ShapeDtypeStruct((S, D_TOK), F32),
                   jax.ShapeDtypeStruct((S // C, N_HEADS, HEAD, HEAD), F32)],
        scratch_shapes=[pltpu.VMEM((N_HEADS, HEAD, HEAD), F32)] + [pltpu.VMEM((tb, D_TOK), F32)] * 2
        + [pltpu.VMEM((N_HEADS, C, HEAD), F32)] * 3,
        compiler_params=_params(("arbitrary",)))(p, p, p, p, lb, onorm)


def _hgrn2_bwd(p, lb, onorm, o, states, dheads, *, name):
    S = p.shape[0]
    C = HG_SUB
    tb = _tile(S, 256, C)
    nsub = tb // C
    nblk = S // tb

    def body(q_ref, fz_ref, v_ref, g_ref, lb_ref, on_ref, o_ref, st_ref, dt_ref, dp_ref, dlb_ref, don_ref, dstate,
             b_blk, k_blk, do_blk, db_blk, dk_blk, bsc, ksc, vsc, qsc, dosc):
        @pl.when(pl.program_id(0) == 0)
        def _():
            dstate[...] = jnp.zeros_like(dstate)
            dlb_ref[...] = jnp.zeros_like(dlb_ref)
            don_ref[...] = jnp.zeros_like(don_ref)

        for h in range(N_HEADS):
            cols = slice(h * HEAD, (h + 1) * HEAD)
            onv = on_ref[:, cols]
            gv = g_ref[:, cols]
            ov = o_ref[:, cols]
            dt = dt_ref[:, cols].astype(F32)
            sgg = _sigmoid(gv)
            sil = gv * sgg
            rinv = lax.rsqrt(jnp.mean(ov * ov, axis=-1, keepdims=True) + EPS)
            n = ov * rinv
            don_ref[:, cols] += jnp.sum(dt * n * sil, axis=0, keepdims=True)
            dn = dt * sil * onv
            dp_ref[:, 3 * D_TOK + h * HEAD:3 * D_TOK + (h + 1) * HEAD] = (
                dt * n * onv * sgg * (1.0 + gv * (1.0 - sgg))).astype(dp_ref.dtype)
            do_blk[:, cols] = rinv * (dn - n * jnp.mean(dn * n, axis=-1, keepdims=True))
        _, _, lg, kk = _hg_gates(fz_ref[...], lb_ref[...])
        b_blk[...] = jnp.dot(_chunk_tri(tb, C, False), lg, precision=lax.Precision.HIGHEST,
                             preferred_element_type=F32)
        k_blk[...] = kk
        tt = lax.broadcasted_iota(jnp.int32, (C, HEAD), 0)

        def sub(j, carry):
            c = nsub - 1 - j
            rows = pl.ds(pl.multiple_of(c * C, C), C)
            for h in range(N_HEADS):
                cols = slice(h * HEAD, (h + 1) * HEAD)
                qv = q_ref[rows, cols]
                vv = v_ref[rows, cols]
                do = do_blk[rows, cols]
                b = b_blk[rows, cols]
                kk = k_blk[rows, cols]
                bsc[h] = b
                ksc[h] = kk
                vsc[h] = vv
                qsc[h] = qv
                dosc[h] = do
                b_last = bsc[h, pl.ds(C - 1, 1), :]
                eb = jnp.exp(b)
                qe = qv * eb
                ebb = jnp.exp(b_last - b)
                ke = kk * ebb
                e_last = jnp.exp(b_last)
                st0 = st_ref[c, h]
                dst1 = dstate[h]
                st0x = st0.astype(MXU_DTYPE)
                dst1x = dst1.astype(MXU_DTYPE)
                dox = do.astype(MXU_DTYPE)
                dqe = jnp.dot(dox, st0x, preferred_element_type=F32)
                dke = jnp.dot(vv.astype(MXU_DTYPE), dst1x, preferred_element_type=F32)
                dv = lax.dot_general(ke.astype(MXU_DTYPE), dst1x, (((1,), (1,)), ((), ())),
                                     preferred_element_type=F32)
                db_last = (e_last * jnp.sum(st0 * dst1, axis=0, keepdims=True)
                           + jnp.sum(dke * ke, axis=0, keepdims=True))
                dstate[h] = dst1 * e_last + lax.dot_general(dox, qe.astype(MXU_DTYPE), (((0,), (0,)), ((), ())),
                                                            preferred_element_type=F32)
                dq = dqe * eb
                db = dqe * qe - dke * ke
                dkk = dke * ebb
                for s in range(C):
                    dec = jnp.where(tt >= s, jnp.exp(b - bsc[h, pl.ds(s, 1), :]), 0.0)
                    da_s = jnp.sum(do * vsc[h, pl.ds(s, 1), :], axis=-1, keepdims=True)
                    pq = da_s * ksc[h, pl.ds(s, 1), :] * dec
                    dq = dq + pq
                    db = db + pq * qv
                for t in range(C):
                    q_t = qsc[h, pl.ds(t, 1), :]
                    do_t = dosc[h, pl.ds(t, 1), :]
                    dec = jnp.where(tt <= t, jnp.exp(bsc[h, pl.ds(t, 1), :] - b), 0.0)
                    da_t = jnp.sum(vv * do_t, axis=-1, keepdims=True)
                    pk = da_t * q_t * dec
                    dkk = dkk + pk
                    db = db - pk * kk
                    a_t = jnp.sum(q_t * kk * dec, axis=-1, keepdims=True)
                    dv = dv + a_t * do_t
                db_blk[rows, cols] = db + jnp.where(tt == C - 1, db_last, 0.0)
                dk_blk[rows, cols] = dkk
                dp_ref[rows, cols] = dq.astype(dp_ref.dtype)
                dp_ref[rows, 2 * D_TOK + h * HEAD:2 * D_TOK + (h + 1) * HEAD] = dv.astype(dp_ref.dtype)
            return carry

        lax.fori_loop(0, nsub, sub, 0)

        dlg = jnp.dot(_chunk_tri(tb, C, True), db_blk[...], precision=lax.Precision.HIGHEST,
                      preferred_element_type=F32)
        lbv = lb_ref[...]
        sg, f, _, _ = _hg_gates(fz_ref[...], lbv)
        w = dlg / f - dk_blk[...]
        dp_ref[:, D_TOK:2 * D_TOK] = (w * (1.0 - lbv) * sg * (1.0 - sg)).astype(dp_ref.dtype)
        dlb_ref[...] += jnp.sum(w * (1.0 - sg), axis=0, keepdims=True)

    blk = lambda c: pl.BlockSpec((tb, D_TOK), lambda i, c=c: (nblk - 1 - i, c))
    vec = pl.BlockSpec((1, D_TOK), lambda i: (0, 0))
    stb = pl.BlockSpec((nsub, N_HEADS, HEAD, HEAD), lambda i: (nblk - 1 - i, 0, 0, 0))
    small = jax.ShapeDtypeStruct((1, D_TOK), F32)
    return pl.pallas_call(
        body, name=name, grid=(nblk,), in_specs=[blk(0), blk(1), blk(2), blk(3), vec, vec, blk(0), stb, blk(0)],
        out_specs=[pl.BlockSpec((tb, 4 * D_TOK), lambda i: (nblk - 1 - i, 0)), vec, vec],
        out_shape=[jax.ShapeDtypeStruct((S, HG_IN), BF16), small, small],
        scratch_shapes=[pltpu.VMEM((N_HEADS, HEAD, HEAD), F32)] + [pltpu.VMEM((tb, D_TOK), F32)] * 5
        + [pltpu.VMEM((N_HEADS, C, HEAD), F32)] * 5,
        compiler_params=_params(("arbitrary",)))(p, p, p, p, lb, onorm, o, states, dheads)


def _adamw(w, g, m, v, *, name):
    shape = w.shape
    cols = shape[-1]
    w2, g2, m2, v2 = (t.reshape(-1, cols) for t in (w, g, m, v))
    R = w2.shape[0]
    tr = _tile(R, 512, 8)

    def body(w_ref, g_ref, m_ref, v_ref, d_ref, nm_ref, nv_ref):
        gv = g_ref[...]
        nm = ADAM_B1 * m_ref[...] + (1.0 - ADAM_B1) * gv
        nv = ADAM_B2 * v_ref[...] + (1.0 - ADAM_B2) * (gv * gv)
        m_hat = nm / (1.0 - ADAM_B1 ** ADAM_STEP)
        v_hat = nv / (1.0 - ADAM_B2 ** ADAM_STEP)
        d_ref[...] = -ADAM_LR * (m_hat / (jnp.sqrt(v_hat) + ADAM_EPS) + ADAM_WD * w_ref[...])
        nm_ref[...] = nm
        nv_ref[...] = nv

    spec = pl.BlockSpec((tr, cols), lambda i: (i, 0))
    out = jax.ShapeDtypeStruct((R, cols), F32)
    d, nm, nv = pl.pallas_call(body, name=name, grid=(R // tr,), in_specs=[spec] * 4, out_specs=[spec] * 3,
                               out_shape=[out] * 3, compiler_params=_params(("parallel",)))(w2, g2, m2, v2)
    return d.reshape(shape), nm.reshape(shape), nv.reshape(shape)


def _add_received(own, got, *, name):
    R, Cc = own.shape
    n = got.shape[0]
    tr = _tile(R, 256, 16)

    def body(a_ref, b_ref, o_ref):
        acc = a_ref[...].astype(F32)
        for k in range(n):
            acc = acc + b_ref[k].astype(F32)
        o_ref[...] = acc

    return pl.pallas_call(
        body, name=name, grid=(R // tr,),
        in_specs=[pl.BlockSpec((tr, Cc), lambda i: (i, 0)), pl.BlockSpec((n, tr, Cc), lambda i: (0, i, 0))],
        out_specs=pl.BlockSpec((tr, Cc), lambda i: (i, 0)), out_shape=jax.ShapeDtypeStruct((R, Cc), F32),
        compiler_params=_params(("parallel",)))(own, got)


def _place():
    return lax.axis_index("x"), lax.axis_index("y"), lax.axis_index("c")


def _all_gather(x, *, name, in_vmem, reduce_sum=False, with_token=False):
    R, Cc = x.shape
    space = pltpu.VMEM if in_vmem else pl.ANY

    def body(x_ref, out_ref, *scratch):
        if with_token:
            scratch[0][...] = jnp.zeros_like(scratch[0])
            scratch = scratch[1:]
        if reduce_sum:
            gat_ref, send_sems, recv_sems, local_sem = scratch
        else:
            gat_ref = out_ref
            send_sems, recv_sems, local_sem = scratch
        mx, my, mc = _place()
        me, sibling = (mx, my, mc), (mx, my, 1 - mc)
        chips = [(1 - mx, my), (mx, 1 - my), (1 - mx, 1 - my)]

        def rows(px, py, pc):
            return gat_ref.at[pl.ds((4 * px + 2 * py + pc) * R, R), :]

        def copy(k, block, to, src=None):
            return pltpu.make_async_remote_copy(
                src_ref=rows(*block) if src is None else src, dst_ref=rows(*block), send_sem=send_sems.at[k],
                recv_sem=recv_sems.at[k], device_id=to, device_id_type=MESH_ID)

        mine = pltpu.make_async_copy(x_ref, rows(*me), local_sem)
        mine.start()
        first = [copy(0, me, sibling, src=x_ref)]
        first += [copy(1 + j, me, (*chip, mc), src=x_ref) for j, chip in enumerate(chips)]
        for cp in first:
            cp.start()
        passed = [copy(4 + j, (*chip, mc), sibling) for j, chip in enumerate(chips)]
        for j, chip in enumerate(chips):
            copy(1 + j, (*chip, mc), me).wait_recv()
            passed[j].start()
        copy(0, sibling, me).wait_recv()
        for j, chip in enumerate(chips):
            copy(4 + j, (*chip, 1 - mc), me).wait_recv()
        for cp in first + passed:
            cp.wait_send()
        mine.wait()
        if reduce_sum:
            acc = gat_ref[pl.ds(0, R), :]
            for d in range(1, N_DEV):
                acc = acc + gat_ref[pl.ds(d * R, R), :]
            out_ref[...] = acc

    sems = [pltpu.SemaphoreType.DMA((7,)), pltpu.SemaphoreType.DMA((7,)), pltpu.SemaphoreType.DMA]
    if reduce_sum:
        assert in_vmem
        out_shape = jax.ShapeDtypeStruct((R, Cc), x.dtype)
        scratch = [pltpu.VMEM((N_DEV * R, Cc), x.dtype)] + sems
    else:
        out_shape = jax.ShapeDtypeStruct((N_DEV * R, Cc), x.dtype)
        scratch = sems
    out_specs = pl.BlockSpec(memory_space=space)
    if with_token:
        out_shape = (out_shape, jax.ShapeDtypeStruct((8, LANE), F32))
        out_specs = (out_specs, pl.BlockSpec(memory_space=pltpu.VMEM))
    return pl.pallas_call(
        body, name=name, out_shape=out_shape, in_specs=[pl.BlockSpec(memory_space=space)], out_specs=out_specs,
        scratch_shapes=scratch, compiler_params=pltpu.CompilerParams(vmem_limit_bytes=VMEM_LIMIT))(x)


def _peer(k, mx, my, mc):
    bits = k + 1
    return (1 - mx if bits & 4 else mx, 1 - my if bits & 2 else my, 1 - mc if bits & 1 else mc)


HBM_SPEC = pl.BlockSpec(memory_space=pltpu.HBM)
SEM_SPEC = pl.BlockSpec(memory_space=pltpu.SEMAPHORE)
DATAFLOW = pltpu.SideEffectType.DATAFLOW_SIDE_EFFECTING


def _exchange_copies(x_refs, land_refs, send_sems, recv_sems, scatter):
    mx, my, mc = _place()
    me = 4 * mx + 2 * my + mc
    n = len(x_refs)
    copies = []
    for k in range(N_DEV - 1):
        px, py, pc = _peer(k, mx, my, mc)
        for m, (x_ref, land_ref) in enumerate(zip(x_refs, land_refs)):
            rows = land_ref.shape[1] if scatter else x_ref.shape[0]
            if scatter:
                src = x_ref.at[pl.ds(pl.multiple_of((4 * px + 2 * py + pc) * rows, 16), rows), :]
                dst = land_ref.at[k]
            else:
                src = x_ref
                dst = land_ref.at[pl.ds(pl.multiple_of(me * rows, 16), rows), :]
            copies.append(pltpu.make_async_remote_copy(
                src_ref=src, dst_ref=dst, send_sem=send_sems.at[k * n + m], recv_sem=recv_sems.at[k * n + m],
                device_id=(px, py, pc), device_id_type=MESH_ID))
    return copies


def _land_shape(x, scatter):
    return (N_DEV - 1, x.shape[0] // N_DEV, x.shape[1]) if scatter else (N_DEV * x.shape[0], x.shape[1])


def _exchange_start(xs, *, name, scatter):
    n = len(xs)
    lands = [lax.empty(_land_shape(x, scatter), x.dtype) for x in xs]

    def body(*refs):
        x_refs, land_refs = refs[:n], refs[n:2 * n]
        send_sems, recv_sems = refs[2 * n:2 * n + 2]
        token = refs[-1]
        for cp in _exchange_copies(x_refs, land_refs, send_sems, recv_sems, scatter):
            cp.start()
        token[...] = jnp.zeros_like(token)

    sems = pltpu.SemaphoreType.DMA(((N_DEV - 1) * n,))
    out = pl.pallas_call(
        body, name=name,
        out_shape=(sems, sems, *[pltpu.HBM(x.shape, x.dtype) for x in xs],
                   *[pltpu.HBM(l.shape, l.dtype) for l in lands], jax.ShapeDtypeStruct((8, LANE), F32)),
        in_specs=(HBM_SPEC,) * (2 * n),
        out_specs=(SEM_SPEC, SEM_SPEC) + (HBM_SPEC,) * (2 * n) + (pl.BlockSpec(memory_space=pltpu.VMEM),),
        input_output_aliases={i: 2 + i for i in range(2 * n)},
        compiler_params=pltpu.CompilerParams(has_side_effects=DATAFLOW))(
            *[pltpu.with_memory_space_constraint(t, pltpu.HBM) for t in list(xs) + lands])
    return out[0], out[1], list(out[2:2 + n]), list(out[2 + n:2 + 2 * n]), out[-1]


def _exchange_wait(started, after, *, name, scatter):
    send_sems, recv_sems, xs, lands, _ = started
    n = len(xs)

    def body(*refs):
        x_refs, land_refs = refs[:n], refs[n:2 * n]
        send_sems, recv_sems = refs[2 * n:2 * n + 2]
        for cp in _exchange_copies(x_refs, land_refs, send_sems, recv_sems, scatter):
            cp.wait_send()
            cp.wait_recv()

    out = pl.pallas_call(
        body, name=name, out_shape=tuple(pltpu.HBM(t.shape, t.dtype) for t in xs + lands),
        in_specs=(HBM_SPEC,) * (2 * n) + (SEM_SPEC, SEM_SPEC, pl.BlockSpec(memory_space=pl.ANY)),
        out_specs=(HBM_SPEC,) * (2 * n), input_output_aliases={i: i for i in range(2 * n)},
        compiler_params=pltpu.CompilerParams(has_side_effects=DATAFLOW))(*xs, *lands, send_sems, recv_sems, after)
    return list(out[:n]), list(out[n:])


def _gather_start(shards, token, *, name):
    shards = [shards[0] + token[0, 0].astype(shards[0].dtype)] + list(shards[1:])
    return _exchange_start(shards, name=name, scatter=False)


def _gather_finish(started, after, me, *, name):
    xs, lands = _exchange_wait(started, after, name=name, scatter=False)
    return [lax.dynamic_update_slice(land, x, (me * x.shape[0], 0)) for land, x in zip(lands, xs)]


def _reduce_start(grads, *, name):
    return _exchange_start(grads, name=name, scatter=True)


def _reduce_finish(started, after, me, *, name):
    sent, gots = _exchange_wait(started, after, name=name + "_wait", scatter=True)
    out = []
    for m, (g, got) in enumerate(zip(sent, gots)):
        rows = g.shape[0] // N_DEV
        own = lax.dynamic_slice(g, (me * rows, 0), (rows, g.shape[1]))
        out.append(_add_received(own, got, name=f"{name}_add{m}"))
    return out


def _pad_rows(a, mult):
    r = (-a.shape[0]) % mult
    return a if r == 0 else jnp.concatenate([a, jnp.zeros((r,) + a.shape[1:], a.dtype)], axis=0)


def kernel(x, mem, mix_norm, mem_norm, w_mem_kv, w_out, hg_w_in, hg_lb, hg_onorm, gm_w_in, gm_ln_g, gm_ln_b, gm_ws, gm_bs, ffn_norm, w_ffn_in, w_ffn_out, final_norm, loss_target, m_mix_norm, m_mem_norm, m_w_mem_kv, m_w_out, m_hg_w_in, m_hg_lb, m_hg_onorm, m_gm_w_in, m_gm_ln_g, m_gm_ln_b, m_gm_ws, m_gm_bs, m_ffn_norm, m_w_ffn_in, m_w_ffn_out, m_final_norm, v_mix_norm, v_mem_norm, v_w_mem_kv, v_w_out, v_hg_w_in, v_hg_lb, v_hg_onorm, v_gm_w_in, v_gm_ln_g, v_gm_ln_b, v_gm_ws, v_gm_bs, v_ffn_norm, v_w_ffn_in, v_w_ffn_out, v_final_norm):
    mx, my, mc = _place()
    me = 4 * mx + 2 * my + mc
    xs = x[0]
    mems = mem[0]
    tgt = loss_target[0]

    hg_t = hg_w_in[0].T.astype(BF16)
    gm_t = gm_w_in[0].T.astype(BF16)
    fi_t = [w_ffn_in[i].T.astype(BF16) for i in range(2)]
    kv_b = [w_mem_kv[i].astype(BF16) for i in range(2)]
    out_b = [w_out[i].astype(BF16) for i in range(2)]
    fo_b = [w_ffn_out[i].astype(BF16) for i in range(2)]
    ln_local = _pad_rows(jnp.concatenate([gm_ln_g, gm_ln_b], axis=0), 8)
    ln_local = jnp.concatenate([ln_local, jnp.zeros((8, LANE - ln_local.shape[1]), F32)], axis=1)
    ln_all, token = _all_gather(ln_local, name="gather_ln", in_vmem=True, with_token=True)
    ln_all = ln_all.reshape(N_DEV, 8, LANE)
    ln_g = ln_all[:, 0, :D_TOK // N_DEV].reshape(1, D_TOK)
    ln_b = ln_all[:, 1, :D_TOK // N_DEV].reshape(1, D_TOK)
    W_hgT, token = _all_gather(hg_t + token[0, 0].astype(BF16), name="gather_first", in_vmem=False,
                               with_token=True)
    gather_l0 = _gather_start(kv_b + out_b + [fi_t[0], fo_b[0]], token, name="gather_l0_start")
    gather_l1 = _gather_start([gm_t, fi_t[1], fo_b[1]], gather_l0[4], name="gather_l1_start")

    lb_soft = jax.nn.softmax(hg_lb, axis=0)
    lb0 = lb_soft[0:1]
    bsb = jnp.broadcast_to(gm_bs[0][:, :, None], (N_HEADS, GM_CHUNK, GM_CHUNK))
    ws = gm_ws[0]

    def ffn_fwd(xin, i):
        hf = _rms_fwd(xin, ffn_norm[i:i + 1], name=f"ffn_norm{i}")
        gu, act = _ffn_in(hf, W_fiT[i], name=f"ffn_in{i}")
        xout = _matmul(act, W_fo[i], res=xin, name=f"ffn_out{i}")
        return hf, gu, act, xout

    h0 = _rms_fwd(xs, mix_norm[0:1], name="mix_norm0", dep=gather_l1[4])
    p0 = _matmul(h0, W_hgT, tb=True, name="hg_in")
    heads0, o0, states = _hgrn2_fwd(p0, lb0, hg_onorm, name="hgrn2_fwd")

    kv0, kv1, wo0, wo1, fi0, fo0 = _gather_finish(gather_l0, o0, me, name="gather_l0_wait")
    W_kv, W_out, W_fiT, W_fo = [kv0, kv1], [wo0, wo1], [fi0], [fo0]
    mem_n, kv = [], []
    for i in range(2):
        mn = _rms_fwd(mems, mem_norm[i:i + 1], name=f"mem_norm{i}")
        mem_n.append(mn)
        kv.append(_matmul(mn, W_kv[i], name=f"mem_kv{i}"))

    heads0 = _attn_fwd(p0, 4 * D_TOK // D_MEM, kv[0], heads0, name="attn_fwd0")
    x1 = _matmul(heads0, W_out[0], res=xs, name="out_proj0")
    hf0, gu0, act0, x2 = ffn_fwd(x1, 0)

    W_gmT, fi1, fo1 = _gather_finish(gather_l1, x2, me, name="gather_l1_wait")
    W_fiT.append(fi1)
    W_fo.append(fo1)
    h1 = _rms_fwd(x2, mix_norm[1:2], name="mix_norm1")
    p1 = _matmul(h1, W_gmT, tb=True, name="gm_in")
    heads1 = _gmlp_fwd(p1, ln_g, ln_b, ws, bsb, name="gmlp_fwd")
    heads1 = _attn_fwd(p1, 2 * D_TOK // D_MEM, kv[1], heads1, name="attn_fwd1")
    x3 = _matmul(heads1, W_out[1], res=x2, name="out_proj1")
    hf1, gu1, act1, x4 = ffn_fwd(x3, 1)

    dx, g_final, loss_part = _final_loss(x4, final_norm.reshape(1, D_MODEL), tgt, name="final_loss")

    def ffn_bwd(dx, xin, hf, gu, act, i, dep):
        dgu = _ffn_out_dx(dx, W_fo[i], gu, dep, name=f"ffn_out_dx{i}")
        g_wfo = _matmul(act, dx, ta=True, out_dtype=BF16, name=f"ffn_out_dw{i}")
        g_wfi_t = _matmul(dgu, hf, ta=True, a_halves=True, out_dtype=BF16, name=f"ffn_in_dw{i}")
        dhf = _matmul(dgu, W_fiT[i], a_halves=True, name=f"ffn_in_dx{i}")
        dx, g_norm = _rms_bwd(xin, ffn_norm[i:i + 1], dhf, dx, name=f"ffn_norm_bwd{i}")
        return dx, g_wfi_t, g_wfo, g_norm

    def mem_bwd(dkv, i):
        g_wkv = _matmul(mem_n[i], dkv, ta=True, out_dtype=BF16, name=f"mem_kv_dw{i}")
        dmn = _matmul(dkv, W_kv[i], tb=True, name=f"mem_kv_dx{i}")
        _, g_norm = _rms_bwd(mems, mem_norm[i:i + 1], dmn, jnp.zeros_like(mems), name=f"mem_norm_bwd{i}")
        return g_wkv, g_norm

    dx, g_wfi1_t, g_wfo1, g_ffn1 = ffn_bwd(dx, x3, hf1, gu1, act1, 1, loss_part)
    dheads = _matmul(dx, W_out[1], tb=True, name="out_proj_dx1")
    g_wout1 = _matmul(heads1, dx, ta=True, out_dtype=BF16, name="out_proj_dw1")
    dp, g_ws, g_bs, g_lng, g_lnb = _gmlp_bwd(p1, ln_g, ln_b, ws, bsb, dheads, name="gmlp_bwd")
    dp, dk, dv = _attn_bwd(p1, 2 * D_TOK // D_MEM, kv[1], dheads, dp, name="attn_bwd1")
    g_wkv1, g_mem1 = mem_bwd(jnp.concatenate([dk, dv], axis=1), 1)
    g_wgm_t = _matmul(dp, h1, ta=True, out_dtype=BF16, name="gm_in_dw")
    dh = _matmul(dp, W_gmT, name="gm_in_dx")
    dx, g_mix1 = _rms_bwd(x2, mix_norm[1:2], dh, dx, name="mix_norm_bwd1")
    reduce_l1 = _reduce_start([g_wkv1, g_wout1, g_wgm_t, g_wfi1_t, g_wfo1], name="reduce_l1_start")

    dx, g_wfi0_t, g_wfo0, g_ffn0 = ffn_bwd(dx, x1, hf0, gu0, act0, 0, reduce_l1[4])
    reduce_ffn0 = _reduce_start([g_wfi0_t, g_wfo0], name="reduce_ffn0_start")
    dheads = _matmul(dx, W_out[0], tb=True, name="out_proj_dx0", dep=reduce_ffn0[4])
    g_wout0 = _matmul(heads0, dx, ta=True, out_dtype=BF16, name="out_proj_dw0")
    dp, g_lb0, g_onorm = _hgrn2_bwd(p0, lb0, hg_onorm, o0, states, dheads, name="hgrn2_bwd")
    dp, dk, dv = _attn_bwd(p0, 4 * D_TOK // D_MEM, kv[0], dheads, dp, name="attn_bwd0")
    g_wkv0, g_mem0 = mem_bwd(jnp.concatenate([dk, dv], axis=1), 0)
    g_whg_t = _matmul(dp, h0, ta=True, out_dtype=BF16, name="hg_in_dw")
    reduce_mix0 = _reduce_start([g_wkv0, g_wout0, g_whg_t], name="reduce_mix0_start")
    dh = _matmul(dp, W_hgT, name="hg_in_dx", dep=reduce_mix0[4])
    grad_x, g_mix0 = _rms_bwd(xs, mix_norm[0:1], dh, dx, name="mix_norm_bwd0")

    g_kv1, g_out1, g_gm_t, g_fi1_t, g_fo1 = _reduce_finish(reduce_l1, grad_x, me, name="reduce_l1")
    g_fi0_t, g_fo0 = _reduce_finish(reduce_ffn0, g_kv1, me, name="reduce_ffn0")
    g_kv0, g_out0, g_hg_t = _reduce_finish(reduce_mix0, g_fi0_t, me, name="reduce_mix0")
    g_shards = [jnp.stack([g_kv0, g_kv1]), jnp.stack([g_out0, g_out1]), g_hg_t.T[None], g_gm_t.T[None],
                jnp.stack([g_fi0_t.T, g_fi1_t.T]), jnp.stack([g_fo0, g_fo1])]

    small = [loss_part, jnp.concatenate([g_mix0, g_mix1], axis=1), jnp.concatenate([g_mem0, g_mem1], axis=1),
             g_lb0, g_onorm, g_lng, g_lnb, g_ws.reshape(1, -1), g_bs.reshape(1, -1),
             jnp.concatenate([g_ffn0, g_ffn1], axis=1), g_final]
    sizes = [t.shape[1] for t in small]
    small_rows = _pad_rows(jnp.concatenate(small, axis=1).reshape(-1, LANE), 8)
    red = _all_gather(small_rows, name="reduce_small", in_vmem=True, reduce_sum=True).reshape(-1)
    pieces, off = [], 0
    for n in sizes:
        pieces.append(red[off:off + n])
        off += n
    loss = pieces[0][0]
    g_mix_norm = pieces[1].reshape(2, D_MODEL)
    g_mem_norm = pieces[2].reshape(2, D_MODEL)
    g_hg_lb = pieces[3][None, :] * lb0 * (jnp.eye(3, dtype=F32)[:, 0:1] - lb_soft)
    g_hg_onorm = pieces[4].reshape(1, D_TOK)
    width = D_TOK // N_DEV
    g_gm_ln_g = lax.dynamic_slice(pieces[5], (me * width,), (width,)).reshape(1, width)
    g_gm_ln_b = lax.dynamic_slice(pieces[6], (me * width,), (width,)).reshape(1, width)
    g_gm_ws = pieces[7].reshape(gm_ws.shape)
    g_gm_bs = pieces[8].reshape(gm_bs.shape)
    g_ffn_norm = pieces[9].reshape(2, D_MODEL)
    g_final_norm = pieces[10]

    grads = [g_mix_norm, g_mem_norm, g_shards[0], g_shards[1], g_shards[2], g_hg_lb, g_hg_onorm, g_shards[3],
             g_gm_ln_g, g_gm_ln_b, g_gm_ws, g_gm_bs, g_ffn_norm, g_shards[4], g_shards[5], g_final_norm]
    weights = [mix_norm, mem_norm, w_mem_kv, w_out, hg_w_in, hg_lb, hg_onorm, gm_w_in, gm_ln_g, gm_ln_b, gm_ws, gm_bs,
               ffn_norm, w_ffn_in, w_ffn_out, final_norm]
    ms = [m_mix_norm, m_mem_norm, m_w_mem_kv, m_w_out, m_hg_w_in, m_hg_lb, m_hg_onorm, m_gm_w_in, m_gm_ln_g,
          m_gm_ln_b, m_gm_ws, m_gm_bs, m_ffn_norm, m_w_ffn_in, m_w_ffn_out, m_final_norm]
    vs = [v_mix_norm, v_mem_norm, v_w_mem_kv, v_w_out, v_hg_w_in, v_hg_lb, v_hg_onorm, v_gm_w_in, v_gm_ln_g,
          v_gm_ln_b, v_gm_ws, v_gm_bs, v_ffn_norm, v_w_ffn_in, v_w_ffn_out, v_final_norm]
    deltas, new_m, new_v = [], [], []
    for n, (w, g, m, v) in enumerate(zip(weights, grads, ms, vs)):
        if w.ndim == 1:
            d, nm, nv = _adamw(w[None], g.reshape(1, -1), m[None], v[None], name=f"adamw{n}")
            d, nm, nv = d[0], nm[0], nv[0]
        else:
            d, nm, nv = _adamw(w, g.reshape(w.shape), m, v, name=f"adamw{n}")
        deltas.append(d)
        new_m.append(nm)
        new_v.append(nv)
    grads = [g.reshape(w.shape) for g, w in zip(grads, weights)]
    return (loss, grad_x[None], *grads, *deltas, *new_m, *new_v)
```

```python
import functools

import jax
import jax.numpy as jnp
from jax import lax
from jax.experimental import pallas as pl
from jax.experimental.pallas import tpu as pltpu

F32 = jnp.float32
BF16 = jnp.bfloat16
MXU_DTYPE = jnp.bfloat16
MESH_ID = pl.DeviceIdType.MESH

N_DEV = 8
EPS = 1e-6
D_MODEL = 1024
D_TOK = 768
D_MEM = 256
N_HEADS = 6
HEAD = 128
MEM_HEADS = 4
MEM_HDIM = 64
GM_CHUNK = 128
D_FF = 2816
HG_SUB = 16
HG_IN = 4 * D_TOK + D_MEM
GM_IN = 2 * D_TOK + D_MEM
LANE = 128
MXU_COLS = 256

ADAM_LR = 0.001
ADAM_B1 = 0.9
ADAM_B2 = 0.999
ADAM_EPS = 1e-08
ADAM_WD = 0.01
ADAM_STEP = 10

VMEM_LIMIT = 48 * 2 ** 20


def _params(sem=None):
    return pltpu.CompilerParams(dimension_semantics=sem, vmem_limit_bytes=VMEM_LIMIT)


def _tile(n, cap, q=LANE):
    if n <= cap:
        return n
    best = None
    for t in range(q, cap + 1, q):
        if n % t == 0:
            best = t
    assert best is not None, (n, cap, q)
    return best


def _sigmoid(x):
    return 1.0 / (1.0 + jnp.exp(-x))


def _gelu(x):
    return 0.5 * x * (1.0 + lax.erf(x * 0.7071067811865476))


def _gelu_grad(x):
    return 0.5 * (1.0 + lax.erf(x * 0.7071067811865476)) + x * jnp.exp(-0.5 * x * x) * 0.3989422804014327


def _matmul(a, b, *, name, ta=False, tb=False, res=None, out_dtype=F32, a_halves=False, b_halves=False, dep=None):
    if a_halves and ta:
        K, M = a.shape[1], 2 * a.shape[2]
    elif a_halves:
        M, K = a.shape[1], 2 * a.shape[2]
    else:
        K, M = a.shape if ta else a.shape[::-1]
    if b_halves:
        assert not tb and b.shape[1] == K
        N = 2 * b.shape[2]
    else:
        N = b.shape[0] if tb else b.shape[1]
        assert (b.shape[1] if tb else b.shape[0]) == K
    tm = _tile(M // 2 if (a_halves and ta) else M, 1664 if ta else 1024)
    tn = _tile(N // 2 if b_halves else N, 1792)
    tk = _tile(K // 2 if (a_halves and not ta) else K, 1024 if ta else 1664)
    nk = K // tk
    dims = (((0 if ta else 1,), (1 if tb else 0,)), ((), ()))

    def body(*refs):
        a_ref, b_ref = refs[:2]
        r_ref = refs[2] if res is not None else None
        o_ref, acc = (refs[-1], None) if nk == 1 else refs[-2:]
        k = pl.program_id(2)

        def product():
            return lax.dot_general(a_ref[...].astype(MXU_DTYPE), b_ref[...].astype(MXU_DTYPE), dims,
                                   preferred_element_type=F32)

        def finish(r):
            if res is not None:
                r = r + r_ref[...].astype(F32)
            o_ref[...] = r.astype(out_dtype)

        if nk == 1:
            finish(product())
            return

        @pl.when(k == 0)
        def _():
            acc[...] = product()

        @pl.when((k > 0) & (k < nk - 1))
        def _():
            acc[...] += product()

        @pl.when(k == nk - 1)
        def _():
            finish(acc[...] + product())

    if a_halves and ta:
        mh = M // 2 // tm
        a_spec = pl.BlockSpec((None, tk, tm), lambda i, j, k: (i // mh, k, i % mh))
    elif a_halves:
        kh = nk // 2
        a_spec = pl.BlockSpec((None, tm, tk), lambda i, j, k: (k // kh, i, k % kh))
    elif ta:
        a_spec = pl.BlockSpec((tk, tm), lambda i, j, k: (k, i))
    else:
        a_spec = pl.BlockSpec((tm, tk), lambda i, j, k: (i, k))
    if b_halves:
        nh = N // 2 // tn
        b_spec = pl.BlockSpec((None, tk, tn), lambda i, j, k: (j // nh, k, j % nh))
    elif tb:
        b_spec = pl.BlockSpec((tn, tk), lambda i, j, k: (j, k))
    else:
        b_spec = pl.BlockSpec((tk, tn), lambda i, j, k: (k, j))
    o_spec = pl.BlockSpec((tm, tn), lambda i, j, k: (i, j))
    in_specs = [a_spec, b_spec] + ([o_spec] if res is not None else [])
    args = (a, b) + ((res,) if res is not None else ())
    if dep is not None:
        in_specs.append(pl.BlockSpec(memory_space=pl.ANY))
        args += (dep,)
    return pl.pallas_call(
        body, name=name, grid=(M // tm, N // tn, nk), in_specs=in_specs, out_specs=o_spec,
        out_shape=jax.ShapeDtypeStruct((M, N), out_dtype),
        scratch_shapes=[] if nk == 1 else [pltpu.VMEM((tm, tn), F32)],
        compiler_params=_params(("parallel", "parallel", "arbitrary")))(*args)


def _ffn_in(hf, wt, *, name):
    S, K = hf.shape
    tm = _tile(S, 512)
    tn = _tile(D_FF, 1408)
    nh = D_FF // tn
    nt = (((1,), (1,)), ((), ()))

    def body(a_ref, bg_ref, bu_ref, gu_ref, act_ref):
        av = a_ref[...].astype(MXU_DTYPE)
        for c0 in range(0, tn, MXU_COLS):
            cs = slice(c0, min(c0 + MXU_COLS, tn))
            gate = lax.dot_general(av, bg_ref[cs, :].astype(MXU_DTYPE), nt, preferred_element_type=F32)
            up = lax.dot_general(av, bu_ref[cs, :].astype(MXU_DTYPE), nt, preferred_element_type=F32)
            gu_ref[0, :, cs] = gate.astype(gu_ref.dtype)
            gu_ref[1, :, cs] = up.astype(gu_ref.dtype)
            act_ref[:, cs] = (gate * _sigmoid(gate) * up).astype(act_ref.dtype)

    return pl.pallas_call(
        body, name=name, grid=(S // tm, nh),
        in_specs=[pl.BlockSpec((tm, K), lambda i, j: (i, 0)), pl.BlockSpec((tn, K), lambda i, j: (j, 0)),
                  pl.BlockSpec((tn, K), lambda i, j: (j + nh, 0))],
        out_specs=[pl.BlockSpec((2, tm, tn), lambda i, j: (0, i, j)), pl.BlockSpec((tm, tn), lambda i, j: (i, j))],
        out_shape=[jax.ShapeDtypeStruct((2, S, D_FF), BF16), jax.ShapeDtypeStruct((S, D_FF), BF16)],
        compiler_params=_params(("parallel", "parallel")))(hf, wt, wt)


def _ffn_out_dx(dx, w, gu, dep, *, name):
    S, K = dx.shape
    tm = _tile(S, 512)
    tn = _tile(D_FF, 1408)

    def body(a_ref, b_ref, gu_ref, dep_ref, o_ref):
        del dep_ref
        av = a_ref[...].astype(MXU_DTYPE)
        for c0 in range(0, tn, MXU_COLS):
            cs = slice(c0, min(c0 + MXU_COLS, tn))
            da = lax.dot_general(av, b_ref[cs, :].astype(MXU_DTYPE), (((1,), (1,)), ((), ())),
                                 preferred_element_type=F32)
            gate = gu_ref[0, :, cs].astype(F32)
            up = gu_ref[1, :, cs].astype(F32)
            sg = _sigmoid(gate)
            o_ref[0, :, cs] = (da * up * sg * (1.0 + gate * (1.0 - sg))).astype(o_ref.dtype)
            o_ref[1, :, cs] = (da * gate * sg).astype(o_ref.dtype)

    halves = pl.BlockSpec((2, tm, tn), lambda i, j: (0, i, j))
    return pl.pallas_call(
        body, name=name, grid=(S // tm, D_FF // tn),
        in_specs=[pl.BlockSpec((tm, K), lambda i, j: (i, 0)), pl.BlockSpec((tn, K), lambda i, j: (j, 0)), halves,
                  pl.BlockSpec(memory_space=pl.ANY)],
        out_specs=halves, out_shape=jax.ShapeDtypeStruct((2, S, D_FF), BF16),
        compiler_params=_params(("parallel", "parallel")))(dx, w, gu, dep)


def _rms_fwd(x, g, *, name, dep=None):
    R, Dm = x.shape
    tr = _tile(R, 512, 8)

    def body(x_ref, g_ref, *rest):
        o_ref = rest[-1]
        xv = x_ref[...]
        r = lax.rsqrt(jnp.mean(xv * xv, axis=-1, keepdims=True) + EPS)
        o_ref[...] = (xv * r * g_ref[...]).astype(o_ref.dtype)

    in_specs = [pl.BlockSpec((tr, Dm), lambda i: (i, 0)), pl.BlockSpec((1, Dm), lambda i: (0, 0))]
    args = (x, g)
    if dep is not None:
        in_specs.append(pl.BlockSpec(memory_space=pl.ANY))
        args += (dep,)
    return pl.pallas_call(
        body, name=name, grid=(R // tr,), in_specs=in_specs,
        out_specs=pl.BlockSpec((tr, Dm), lambda i: (i, 0)), out_shape=jax.ShapeDtypeStruct((R, Dm), BF16),
        compiler_params=_params(("parallel",)))(*args)


def _rms_bwd(x, g, dh, dres, *, name):
    R, Dm = x.shape
    tr = _tile(R, 256, 8)

    def body(x_ref, g_ref, dh_ref, dres_ref, dx_ref, dg_ref):
        @pl.when(pl.program_id(0) == 0)
        def _():
            dg_ref[...] = jnp.zeros_like(dg_ref)

        xv = x_ref[...]
        r = lax.rsqrt(jnp.mean(xv * xv, axis=-1, keepdims=True) + EPS)
        xh = xv * r
        dhv = dh_ref[...].astype(F32)
        dg_ref[...] += jnp.sum(dhv * xh, axis=0, keepdims=True)
        u = dhv * g_ref[...]
        dx = r * (u - xh * jnp.mean(u * xh, axis=-1, keepdims=True))
        dx_ref[...] = dres_ref[...] + dx

    row = pl.BlockSpec((tr, Dm), lambda i: (i, 0))
    vec = pl.BlockSpec((1, Dm), lambda i: (0, 0))
    return pl.pallas_call(
        body, name=name, grid=(R // tr,), in_specs=[row, vec, row, row], out_specs=[row, vec],
        out_shape=[jax.ShapeDtypeStruct((R, Dm), F32), jax.ShapeDtypeStruct((1, Dm), F32)],
        compiler_params=_params(("arbitrary",)))(x, g, dh, dres)


def _final_loss(x, g, tgt, *, name):
    R, Dm = x.shape
    tr = _tile(R, 256, 8)

    def body(x_ref, g_ref, t_ref, dx_ref, dg_ref, loss_ref):
        @pl.when(pl.program_id(0) == 0)
        def _():
            dg_ref[...] = jnp.zeros_like(dg_ref)
            loss_ref[...] = jnp.zeros_like(loss_ref)

        xv = x_ref[...]
        r = lax.rsqrt(jnp.mean(xv * xv, axis=-1, keepdims=True) + EPS)
        xh = xv * r
        gv = g_ref[...]
        err = xh * gv - t_ref[...]
        part = 0.5 * jnp.sum(jnp.mean(err * err, axis=-1, keepdims=True), axis=0, keepdims=True)
        loss_ref[...] += jnp.broadcast_to(part, loss_ref.shape)
        dy = err * (1.0 / Dm)
        dg_ref[...] += jnp.sum(dy * xh, axis=0, keepdims=True)
        u = dy * gv
        dx_ref[...] = r * (u - xh * jnp.mean(u * xh, axis=-1, keepdims=True))

    row = pl.BlockSpec((tr, Dm), lambda i: (i, 0))
    vec = pl.BlockSpec((1, Dm), lambda i: (0, 0))
    one = pl.BlockSpec((1, LANE), lambda i: (0, 0))
    return pl.pallas_call(
        body, name=name, grid=(R // tr,), in_specs=[row, vec, row], out_specs=[row, vec, one],
        out_shape=[jax.ShapeDtypeStruct((R, Dm), F32), jax.ShapeDtypeStruct((1, Dm), F32),
                   jax.ShapeDtypeStruct((1, LANE), F32)],
        compiler_params=_params(("arbitrary",)))(x, g, tgt)


def _head_mask(h):
    lane = lax.broadcasted_iota(jnp.int32, (1, D_MEM), 1)
    return (lane >= h * MEM_HDIM) & (lane < (h + 1) * MEM_HDIM)


def _attn_probs(qv, k_mx, mask):
    s = lax.dot_general(jnp.where(mask, qv, 0.0).astype(MXU_DTYPE), k_mx, (((1,), (1,)), ((), ())),
                        preferred_element_type=F32) * (MEM_HDIM ** -0.5)
    e = jnp.exp(s - jnp.max(s, axis=-1, keepdims=True))
    return e / jnp.sum(e, axis=-1, keepdims=True)


def _attn_fwd(p, qcol, kv, heads, *, name):
    S = p.shape[0]
    M = kv.shape[0]
    ts = _tile(S, 512, 8)

    def body(q_ref, k_ref, v_ref, heads_in, o_ref):
        del heads_in
        qv = q_ref[...]
        kx = k_ref[...].astype(MXU_DTYPE)
        vv = v_ref[...]
        out = jnp.zeros((ts, D_MEM), F32)
        for h in range(MEM_HEADS):
            mask = _head_mask(h)
            pr = _attn_probs(qv, kx, mask)
            out = out + jnp.dot(pr.astype(MXU_DTYPE), jnp.where(mask, vv, 0.0).astype(MXU_DTYPE),
                                preferred_element_type=F32)
        o_ref[...] = out.astype(o_ref.dtype)

    return pl.pallas_call(
        body, name=name, grid=(S // ts,),
        in_specs=[pl.BlockSpec((ts, D_MEM), lambda i: (i, qcol)), pl.BlockSpec((M, D_MEM), lambda i: (0, 0)),
                  pl.BlockSpec((M, D_MEM), lambda i: (0, 1)), pl.BlockSpec(memory_space=pl.ANY)],
        out_specs=pl.BlockSpec((ts, D_MEM), lambda i: (i, D_TOK // D_MEM)),
        out_shape=jax.ShapeDtypeStruct(heads.shape, heads.dtype), input_output_aliases={3: 0},
        compiler_params=_params(("parallel",)))(p, kv, kv, heads)


def _attn_bwd(p, qcol, kv, dheads, dp, *, name):
    S = p.shape[0]
    M = kv.shape[0]
    ts = _tile(S, 512, 8)
    scale = MEM_HDIM ** -0.5

    def body(q_ref, k_ref, v_ref, do_ref, dp_in, dq_ref, dk_ref, dv_ref):
        del dp_in

        @pl.when(pl.program_id(0) == 0)
        def _():
            dk_ref[...] = jnp.zeros_like(dk_ref)
            dv_ref[...] = jnp.zeros_like(dv_ref)

        qv = q_ref[...]
        kv_ = k_ref[...]
        kx = kv_.astype(MXU_DTYPE)
        vv = v_ref[...]
        dox = do_ref[...].astype(MXU_DTYPE)
        qx = qv.astype(MXU_DTYPE)
        dq = jnp.zeros((ts, D_MEM), F32)
        for h in range(MEM_HEADS):
            mask = _head_mask(h)
            pr = _attn_probs(qv, kx, mask)
            vh = jnp.where(mask, vv, 0.0).astype(MXU_DTYPE)
            dpr = lax.dot_general(dox, vh, (((1,), (1,)), ((), ())), preferred_element_type=F32)
            ds = (pr * (dpr - jnp.sum(dpr * pr, axis=-1, keepdims=True)) * scale).astype(MXU_DTYPE)
            dq = dq + jnp.dot(ds, jnp.where(mask, kv_, 0.0).astype(MXU_DTYPE), preferred_element_type=F32)
            dk_h = lax.dot_general(ds, qx, (((0,), (0,)), ((), ())), preferred_element_type=F32)
            dv_h = lax.dot_general(pr.astype(MXU_DTYPE), dox, (((0,), (0,)), ((), ())), preferred_element_type=F32)
            dk_ref[...] += jnp.where(mask, dk_h, 0.0)
            dv_ref[...] += jnp.where(mask, dv_h, 0.0)
        dq_ref[...] = dq.astype(dq_ref.dtype)

    return pl.pallas_call(
        body, name=name, grid=(S // ts,),
        in_specs=[pl.BlockSpec((ts, D_MEM), lambda i: (i, qcol)), pl.BlockSpec((M, D_MEM), lambda i: (0, 0)),
                  pl.BlockSpec((M, D_MEM), lambda i: (0, 1)),
                  pl.BlockSpec((ts, D_MEM), lambda i: (i, D_TOK // D_MEM)), pl.BlockSpec(memory_space=pl.ANY)],
        out_specs=[pl.BlockSpec((ts, D_MEM), lambda i: (i, qcol)), pl.BlockSpec((M, D_MEM), lambda i: (0, 0)),
                   pl.BlockSpec((M, D_MEM), lambda i: (0, 0))],
        out_shape=[jax.ShapeDtypeStruct(dp.shape, dp.dtype), jax.ShapeDtypeStruct((M, D_MEM), F32),
                   jax.ShapeDtypeStruct((M, D_MEM), F32)],
        input_output_aliases={4: 0}, compiler_params=_params(("arbitrary",)))(p, kv, kv, dheads, dp)


def _gm_forward_parts(u_ref, v_ref, lng_ref, lnb_ref, w_ref, bsb_ref):
    zu = _gelu(u_ref[...])
    zv = _gelu(v_ref[...])
    mu = jnp.mean(zv, axis=-1, keepdims=True)
    cen = zv - mu
    rs = lax.rsqrt(jnp.mean(cen * cen, axis=-1, keepdims=True) + EPS)
    vh = cen * rs
    vn = vh * lng_ref[...] + lnb_ref[...]
    row = lax.broadcasted_iota(jnp.int32, (GM_CHUNK, GM_CHUNK), 0)
    col = lax.broadcasted_iota(jnp.int32, (GM_CHUNK, GM_CHUNK), 1)
    tril = row >= col
    wm = [jnp.where(tril, w_ref[g], 0.0).astype(MXU_DTYPE) for g in range(N_HEADS)]
    vnx = [vn[:, g * HEAD:(g + 1) * HEAD].astype(MXU_DTYPE) for g in range(N_HEADS)]
    sv = [jnp.dot(wm[g], vnx[g], preferred_element_type=F32) + bsb_ref[g] for g in range(N_HEADS)]
    return zu, vh, rs, wm, vnx, sv, tril


def _gmlp_fwd(p, lng, lnb, ws, bsb, *, name):
    S = p.shape[0]

    def body(u_ref, v_ref, lng_ref, lnb_ref, w_ref, bsb_ref, o_ref):
        zu, _, _, _, _, sv, _ = _gm_forward_parts(u_ref, v_ref, lng_ref, lnb_ref, w_ref, bsb_ref)
        for g in range(N_HEADS):
            o_ref[:, g * HEAD:(g + 1) * HEAD] = (zu[:, g * HEAD:(g + 1) * HEAD] * sv[g]).astype(o_ref.dtype)

    blk = lambda c: pl.BlockSpec((GM_CHUNK, D_TOK), lambda i: (i, c))
    vec = pl.BlockSpec((1, D_TOK), lambda i: (0, 0))
    cube = pl.BlockSpec((N_HEADS, GM_CHUNK, GM_CHUNK), lambda i: (0, 0, 0))
    return pl.pallas_call(
        body, name=name, grid=(S // GM_CHUNK,), in_specs=[blk(0), blk(1), vec, vec, cube, cube],
        out_specs=blk(0), out_shape=jax.ShapeDtypeStruct((S, D_MODEL), BF16),
        compiler_params=_params(("parallel",)))(p, p, lng, lnb, ws, bsb)


def _gmlp_bwd(p, lng, lnb, ws, bsb, dheads, *, name):
    S = p.shape[0]

    def body(u_ref, v_ref, lng_ref, lnb_ref, w_ref, bsb_ref, dt_ref, dp_ref, dw_ref, dbs_ref, dlg_ref, dlb_ref):
        @pl.when(pl.program_id(0) == 0)
        def _():
            dw_ref[...] = jnp.zeros_like(dw_ref)
            dbs_ref[...] = jnp.zeros_like(dbs_ref)
            dlg_ref[...] = jnp.zeros_like(dlg_ref)
            dlb_ref[...] = jnp.zeros_like(dlb_ref)

        zu, vh, rs, wm, vnx, sv, tril = _gm_forward_parts(u_ref, v_ref, lng_ref, lnb_ref, w_ref, bsb_ref)
        dt = dt_ref[...].astype(F32)
        dvn_parts = []
        for g in range(N_HEADS):
            sl = slice(g * HEAD, (g + 1) * HEAD)
            dsv = dt[:, sl] * zu[:, sl]
            dp_ref[:, sl] = (dt[:, sl] * sv[g] * _gelu_grad(u_ref[:, sl])).astype(dp_ref.dtype)
            dsx = dsv.astype(MXU_DTYPE)
            dw = lax.dot_general(dsx, vnx[g], (((1,), (1,)), ((), ())), preferred_element_type=F32)
            dw_ref[g] += jnp.where(tril, dw, 0.0)
            dbs_ref[g] += jnp.sum(dsv, axis=-1, keepdims=True)
            dvn_parts.append(lax.dot_general(wm[g], dsx, (((0,), (0,)), ((), ())), preferred_element_type=F32))
        dvn = jnp.concatenate(dvn_parts, axis=-1)
        dlg_ref[...] += jnp.sum(dvn * vh, axis=0, keepdims=True)
        dlb_ref[...] += jnp.sum(dvn, axis=0, keepdims=True)
        dvh = dvn * lng_ref[...]
        dzv = rs * (dvh - jnp.mean(dvh, axis=-1, keepdims=True) - vh * jnp.mean(dvh * vh, axis=-1, keepdims=True))
        dp_ref[:, D_TOK:] = (dzv * _gelu_grad(v_ref[...])).astype(dp_ref.dtype)

    blk = lambda c: pl.BlockSpec((GM_CHUNK, D_TOK), lambda i: (i, c))
    vec = pl.BlockSpec((1, D_TOK), lambda i: (0, 0))
    cube = pl.BlockSpec((N_HEADS, GM_CHUNK, GM_CHUNK), lambda i: (0, 0, 0))
    col = pl.BlockSpec((N_HEADS, GM_CHUNK, 1), lambda i: (0, 0, 0))
    return pl.pallas_call(
        body, name=name, grid=(S // GM_CHUNK,), in_specs=[blk(0), blk(1), vec, vec, cube, cube, blk(0)],
        out_specs=[pl.BlockSpec((GM_CHUNK, 2 * D_TOK), lambda i: (i, 0)), cube, col, vec, vec],
        out_shape=[jax.ShapeDtypeStruct((S, GM_IN), BF16), jax.ShapeDtypeStruct((N_HEADS, GM_CHUNK, GM_CHUNK), F32),
                   jax.ShapeDtypeStruct((N_HEADS, GM_CHUNK, 1), F32), jax.ShapeDtypeStruct((1, D_TOK), F32),
                   jax.ShapeDtypeStruct((1, D_TOK), F32)],
        compiler_params=_params(("arbitrary",)))(p, p, lng, lnb, ws, bsb, dheads)


def _chunk_tri(n, chunk, upper):
    r = lax.broadcasted_iota(jnp.int32, (n, n), 0)
    c = lax.broadcasted_iota(jnp.int32, (n, n), 1)
    same = (r // chunk) == (c // chunk)
    return jnp.where(same & ((r <= c) if upper else (r >= c)), 1.0, 0.0).astype(F32)


def _hg_gates(fz, lb):
    sg = _sigmoid(fz)
    f = lb + (1.0 - lb) * sg
    kk = (1.0 - lb) * (1.0 - sg)
    return sg, f, jnp.log(f), kk


def _hgrn2_fwd(p, lb, onorm, *, name):
    S = p.shape[0]
    C = HG_SUB
    tb = _tile(S, 256, C)
    nsub = tb // C

    def body(q_ref, fz_ref, v_ref, g_ref, lb_ref, on_ref, tok_ref, o_ref, st_ref, state, b_blk, k_blk, bsc, ksc, vsc):
        @pl.when(pl.program_id(0) == 0)
        def _():
            state[...] = jnp.zeros_like(state)

        _, _, lg, kk = _hg_gates(fz_ref[...], lb_ref[...])
        b_blk[...] = jnp.dot(_chunk_tri(tb, C, False), lg, precision=lax.Precision.HIGHEST,
                             preferred_element_type=F32)
        k_blk[...] = kk
        tt = lax.broadcasted_iota(jnp.int32, (C, HEAD), 0)

        def sub(c, carry):
            rows = pl.ds(pl.multiple_of(c * C, C), C)
            for h in range(N_HEADS):
                cols = slice(h * HEAD, (h + 1) * HEAD)
                qv = q_ref[rows, cols]
                vv = v_ref[rows, cols]
                b = b_blk[rows, cols]
                kk = k_blk[rows, cols]
                st0 = state[h]
                st_ref[c, h] = st0
                inter = lax.dot_general((qv * jnp.exp(b)).astype(MXU_DTYPE), st0.astype(MXU_DTYPE),
                                        (((1,), (1,)), ((), ())), preferred_element_type=F32)
                bsc[h] = b
                ksc[h] = kk
                vsc[h] = vv
                intra = jnp.zeros((C, HEAD), F32)
                for s in range(C):
                    dec = jnp.where(tt >= s, jnp.exp(b - bsc[h, pl.ds(s, 1), :]), 0.0)
                    a_s = jnp.sum(qv * ksc[h, pl.ds(s, 1), :] * dec, axis=-1, keepdims=True)
                    intra = intra + a_s * vsc[h, pl.ds(s, 1), :]
                o_ref[rows, cols] = inter + intra
                b_last = bsc[h, pl.ds(C - 1, 1), :]
                ke = kk * jnp.exp(b_last - b)
                state[h] = st0 * jnp.exp(b_last) + lax.dot_general(
                    vv.astype(MXU_DTYPE), ke.astype(MXU_DTYPE), (((0,), (0,)), ((), ())),
                    preferred_element_type=F32)
            return carry

        lax.fori_loop(0, nsub, sub, 0)

        for h in range(N_HEADS):
            cols = slice(h * HEAD, (h + 1) * HEAD)
            o = o_ref[:, cols]
            gv = g_ref[:, cols]
            n = o * lax.rsqrt(jnp.mean(o * o, axis=-1, keepdims=True) + EPS)
            tok_ref[:, cols] = (n * (gv * _sigmoid(gv)) * on_ref[:, cols]).astype(tok_ref.dtype)

    blk = lambda c: pl.BlockSpec((tb, D_TOK), lambda i, c=c: (i, c))
    vec = pl.BlockSpec((1, D_TOK), lambda i: (0, 0))
    stb = pl.BlockSpec((nsub, N_HEADS, HEAD, HEAD), lambda i: (i, 0, 0, 0))
    return pl.pallas_call(
        body, name=name, grid=(S // tb,), in_specs=[blk(0), blk(1), blk(2), blk(3), vec, vec],
        out_specs=[blk(0), blk(0), stb],
        out_shape=[jax.ShapeDtypeStruct((S, D_MODEL), BF16), jax.ShapeDtypeStruct((S, D_TOK), F32),
                   jax.ShapeDtypeStruct((S // C, N_HEADS, HEAD, HEAD), F32)],
        scratch_shapes=[pltpu.VMEM((N_HEADS, HEAD, HEAD), F32)] + [pltpu.VMEM((tb, D_TOK), F32)] * 2
        + [pltpu.VMEM((N_HEADS, C, HEAD), F32)] * 3,
        compiler_params=_params(("arbitrary",)))(p, p, p, p, lb, onorm)


def _hgrn2_bwd(p, lb, onorm, o, states, dheads, *, name):
    S = p.shape[0]
    C = HG_SUB
    tb = _tile(S, 256, C)
    nsub = tb // C
    nblk = S // tb

    def body(q_ref, fz_ref, v_ref, g_ref, lb_ref, on_ref, o_ref, st_ref, dt_ref, dp_ref, dlb_ref, don_ref, dstate,
             b_blk, k_blk, do_blk, db_blk, dk_blk, bsc, ksc, vsc, qsc, dosc):
        @pl.when(pl.program_id(0) == 0)
        def _():
            dstate[...] = jnp.zeros_like(dstate)
            dlb_ref[...] = jnp.zeros_like(dlb_ref)
            don_ref[...] = jnp.zeros_like(don_ref)

        for h in range(N_HEADS):
            cols = slice(h * HEAD, (h + 1) * HEAD)
            onv = on_ref[:, cols]
            gv = g_ref[:, cols]
            ov = o_ref[:, cols]
            dt = dt_ref[:, cols].astype(F32)
            sgg = _sigmoid(gv)
            sil = gv * sgg
            rinv = lax.rsqrt(jnp.mean(ov * ov, axis=-1, keepdims=True) + EPS)
            n = ov * rinv
            don_ref[:, cols] += jnp.sum(dt * n * sil, axis=0, keepdims=True)
            dn = dt * sil * onv
            dp_ref[:, 3 * D_TOK + h * HEAD:3 * D_TOK + (h + 1) * HEAD] = (
                dt * n * onv * sgg * (1.0 + gv * (1.0 - sgg))).astype(dp_ref.dtype)
            do_blk[:, cols] = rinv * (dn - n * jnp.mean(dn * n, axis=-1, keepdims=True))
        _, _, lg, kk = _hg_gates(fz_ref[...], lb_ref[...])
        b_blk[...] = jnp.dot(_chunk_tri(tb, C, False), lg, precision=lax.Precision.HIGHEST,
                             preferred_element_type=F32)
        k_blk[...] = kk
        tt = lax.broadcasted_iota(jnp.int32, (C, HEAD), 0)

        def sub(j, carry):
            c = nsub - 1 - j
            rows = pl.ds(pl.multiple_of(c * C, C), C)
            for h in range(N_HEADS):
                cols = slice(h * HEAD, (h + 1) * HEAD)
                qv = q_ref[rows, cols]
                vv = v_ref[rows, cols]
                do = do_blk[rows, cols]
                b = b_blk[rows, cols]
                kk = k_blk[rows, cols]
                bsc[h] = b
                ksc[h] = kk
                vsc[h] = vv
                qsc[h] = qv
                dosc[h] = do
                b_last = bsc[h, pl.ds(C - 1, 1), :]
                eb = jnp.exp(b)
                qe = qv * eb
                ebb = jnp.exp(b_last - b)
                ke = kk * ebb
                e_last = jnp.exp(b_last)
                st0 = st_ref[c, h]
                dst1 = dstate[h]
                st0x = st0.astype(MXU_DTYPE)
                dst1x = dst1.astype(MXU_DTYPE)
                dox = do.astype(MXU_DTYPE)
                dqe = jnp.dot(dox, st0x, preferred_element_type=F32)
                dke = jnp.dot(vv.astype(MXU_DTYPE), dst1x, preferred_element_type=F32)
                dv = lax.dot_general(ke.astype(MXU_DTYPE), dst1x, (((1,), (1,)), ((), ())),
                                     preferred_element_type=F32)
                db_last = (e_last * jnp.sum(st0 * dst1, axis=0, keepdims=True)
                           + jnp.sum(dke * ke, axis=0, keepdims=True))
                dstate[h] = dst1 * e_last + lax.dot_general(dox, qe.astype(MXU_DTYPE), (((0,), (0,)), ((), ())),
                                                            preferred_element_type=F32)
                dq = dqe * eb
                db = dqe * qe - dke * ke
                dkk = dke * ebb
                for s in range(C):
                    dec = jnp.where(tt >= s, jnp.exp(b - bsc[h, pl.ds(s, 1), :]), 0.0)
                    da_s = jnp.sum(do * vsc[h, pl.ds(s, 1), :], axis=-1, keepdims=True)
                    pq = da_s * ksc[h, pl.ds(s, 1), :] * dec
                    dq = dq + pq
                    db = db + pq * qv
                for t in range(C):
                    q_t = qsc[h, pl.ds(t, 1), :]
                    do_t = dosc[h, pl.ds(t, 1), :]
                    dec = jnp.where(tt <= t, jnp.exp(bsc[h, pl.ds(t, 1), :] - b), 0.0)
                    da_t = jnp.sum(vv * do_t, axis=-1, keepdims=True)
                    pk = da_t * q_t * dec
                    dkk = dkk + pk
                    db = db - pk * kk
                    a_t = jnp.sum(q_t * kk * dec, axis=-1, keepdims=True)
                    dv = dv + a_t * do_t
                db_blk[rows, cols] = db + jnp.where(tt == C - 1, db_last, 0.0)
                dk_blk[rows, cols] = dkk
                dp_ref[rows, cols] = dq.astype(dp_ref.dtype)
                dp_ref[rows, 2 * D_TOK + h * HEAD:2 * D_TOK + (h + 1) * HEAD] = dv.astype(dp_ref.dtype)
            return carry

        lax.fori_loop(0, nsub, sub, 0)

        dlg = jnp.dot(_chunk_tri(tb, C, True), db_blk[...], precision=lax.Precision.HIGHEST,
                      preferred_element_type=F32)
        lbv = lb_ref[...]
        sg, f, _, _ = _hg_gates(fz_ref[...], lbv)
        w = dlg / f - dk_blk[...]
        dp_ref[:, D_TOK:2 * D_TOK] = (w * (1.0 - lbv) * sg * (1.0 - sg)).astype(dp_ref.dtype)
        dlb_ref[...] += jnp.sum(w * (1.0 - sg), axis=0, keepdims=True)

    blk = lambda c: pl.BlockSpec((tb, D_TOK), lambda i, c=c: (nblk - 1 - i, c))
    vec = pl.BlockSpec((1, D_TOK), lambda i: (0, 0))
    stb = pl.BlockSpec((nsub, N_HEADS, HEAD, HEAD), lambda i: (nblk - 1 - i, 0, 0, 0))
    small = jax.ShapeDtypeStruct((1, D_TOK), F32)
    return pl.pallas_call(
        body, name=name, grid=(nblk,), in_specs=[blk(0), blk(1), blk(2), blk(3), vec, vec, blk(0), stb, blk(0)],
        out_specs=[pl.BlockSpec((tb, 4 * D_TOK), lambda i: (nblk - 1 - i, 0)), vec, vec],
        out_shape=[jax.ShapeDtypeStruct((S, HG_IN), BF16), small, small],
        scratch_shapes=[pltpu.VMEM((N_HEADS, HEAD, HEAD), F32)] + [pltpu.VMEM((tb, D_TOK), F32)] * 5
        + [pltpu.VMEM((N_HEADS, C, HEAD), F32)] * 5,
        compiler_params=_params(("arbitrary",)))(p, p, p, p, lb, onorm, o, states, dheads)


def _adamw(w, g, m, v, *, name):
    shape = w.shape
    cols = shape[-1]
    w2, g2, m2, v2 = (t.reshape(-1, cols) for t in (w, g, m, v))
    R = w2.shape[0]
    tr = _tile(R, 512, 8)

    def body(w_ref, g_ref, m_ref, v_ref, d_ref, nm_ref, nv_ref):
        gv = g_ref[...]
        nm = ADAM_B1 * m_ref[...] + (1.0 - ADAM_B1) * gv
        nv = ADAM_B2 * v_ref[...] + (1.0 - ADAM_B2) * (gv * gv)
        m_hat = nm / (1.0 - ADAM_B1 ** ADAM_STEP)
        v_hat = nv / (1.0 - ADAM_B2 ** ADAM_STEP)
        d_ref[...] = -ADAM_LR * (m_hat / (jnp.sqrt(v_hat) + ADAM_EPS) + ADAM_WD * w_ref[...])
        nm_ref[...] = nm
        nv_ref[...] = nv

    spec = pl.BlockSpec((tr, cols), lambda i: (i, 0))
    out = jax.ShapeDtypeStruct((R, cols), F32)
    d, nm, nv = pl.pallas_call(body, name=name, grid=(R // tr,), in_specs=[spec] * 4, out_specs=[spec] * 3,
                               out_shape=[out] * 3, compiler_params=_params(("parallel",)))(w2, g2, m2, v2)
    return d.reshape(shape), nm.reshape(shape), nv.reshape(shape)


def _add_received(own, got, *, name):
    R, Cc = own.shape
    n = got.shape[0]
    tr = _tile(R, 256, 16)

    def body(a_ref, b_ref, o_ref):
        acc = a_ref[...].astype(F32)
        for k in range(n):
            acc = acc + b_ref[k].astype(F32)
        o_ref[...] = acc

    return pl.pallas_call(
        body, name=name, grid=(R // tr,),
        in_specs=[pl.BlockSpec((tr, Cc), lambda i: (i, 0)), pl.BlockSpec((n, tr, Cc), lambda i: (0, i, 0))],
        out_specs=pl.BlockSpec((tr, Cc), lambda i: (i, 0)), out_shape=jax.ShapeDtypeStruct((R, Cc), F32),
        compiler_params=_params(("parallel",)))(own, got)


def _place():
    return lax.axis_index("x"), lax.axis_index("y"), lax.axis_index("c")


def _all_gather(x, *, name, in_vmem, reduce_sum=False, with_token=False):
    R, Cc = x.shape
    space = pltpu.VMEM if in_vmem else pl.ANY

    def body(x_ref, out_ref, *scratch):
        if with_token:
            scratch[0][...] = jnp.zeros_like(scratch[0])
            scratch = scratch[1:]
        if reduce_sum:
            gat_ref, send_sems, recv_sems, local_sem = scratch
        else:
            gat_ref = out_ref
            send_sems, recv_sems, local_sem = scratch
        mx, my, mc = _place()
        me, sibling = (mx, my, mc), (mx, my, 1 - mc)
        chips = [(1 - mx, my), (mx, 1 - my), (1 - mx, 1 - my)]

        def rows(px, py, pc):
            return gat_ref.at[pl.ds((4 * px + 2 * py + pc) * R, R), :]

        def copy(k, block, to, src=None):
            return pltpu.make_async_remote_copy(
                src_ref=rows(*block) if src is None else src, dst_ref=rows(*block), send_sem=send_sems.at[k],
                recv_sem=recv_sems.at[k], device_id=to, device_id_type=MESH_ID)

        mine = pltpu.make_async_copy(x_ref, rows(*me), local_sem)
        mine.start()
        first = [copy(0, me, sibling, src=x_ref)]
        first += [copy(1 + j, me, (*chip, mc), src=x_ref) for j, chip in enumerate(chips)]
        for cp in first:
            cp.start()
        passed = [copy(4 + j, (*chip, mc), sibling) for j, chip in enumerate(chips)]
        for j, chip in enumerate(chips):
            copy(1 + j, (*chip, mc), me).wait_recv()
            passed[j].start()
        copy(0, sibling, me).wait_recv()
        for j, chip in enumerate(chips):
            copy(4 + j, (*chip, 1 - mc), me).wait_recv()
        for cp in first + passed:
            cp.wait_send()
        mine.wait()
        if reduce_sum:
            acc = gat_ref[pl.ds(0, R), :]
            for d in range(1, N_DEV):
                acc = acc + gat_ref[pl.ds(d * R, R), :]
            out_ref[...] = acc

    sems = [pltpu.SemaphoreType.DMA((7,)), pltpu.SemaphoreType.DMA((7,)), pltpu.SemaphoreType.DMA]
    if reduce_sum:
        assert in_vmem
        out_shape = jax.ShapeDtypeStruct((R, Cc), x.dtype)
        scratch = [pltpu.VMEM((N_DEV * R, Cc), x.dtype)] + sems
    else:
        out_shape = jax.ShapeDtypeStruct((N_DEV * R, Cc), x.dtype)
        scratch = sems
    out_specs = pl.BlockSpec(memory_space=space)
    if with_token:
        out_shape = (out_shape, jax.ShapeDtypeStruct((8, LANE), F32))
        out_specs = (out_specs, pl.BlockSpec(memory_space=pltpu.VMEM))
    return pl.pallas_call(
        body, name=name, out_shape=out_shape, in_specs=[pl.BlockSpec(memory_space=space)], out_specs=out_specs,
        scratch_shapes=scratch, compiler_params=pltpu.CompilerParams(vmem_limit_bytes=VMEM_LIMIT))(x)


def _peer(k, mx, my, mc):
    bits = k + 1
    return (1 - mx if bits & 4 else mx, 1 - my if bits & 2 else my, 1 - mc if bits & 1 else mc)


HBM_SPEC = pl.BlockSpec(memory_space=pltpu.HBM)
SEM_SPEC = pl.BlockSpec(memory_space=pltpu.SEMAPHORE)
DATAFLOW = pltpu.SideEffectType.DATAFLOW_SIDE_EFFECTING


def _exchange_copies(x_refs, land_refs, send_sems, recv_sems, scatter):
    mx, my, mc = _place()
    me = 4 * mx + 2 * my + mc
    n = len(x_refs)
    copies = []
    for k in range(N_DEV - 1):
        px, py, pc = _peer(k, mx, my, mc)
        for m, (x_ref, land_ref) in enumerate(zip(x_refs, land_refs)):
            rows = land_ref.shape[1] if scatter else x_ref.shape[0]
            if scatter:
                src = x_ref.at[pl.ds(pl.multiple_of((4 * px + 2 * py + pc) * rows, 16), rows), :]
                dst = land_ref.at[k]
            else:
                src = x_ref
                dst = land_ref.at[pl.ds(pl.multiple_of(me * rows, 16), rows), :]
            copies.append(pltpu.make_async_remote_copy(
                src_ref=src, dst_ref=dst, send_sem=send_sems.at[k * n + m], recv_sem=recv_sems.at[k * n + m],
                device_id=(px, py, pc), device_id_type=MESH_ID))
    return copies


def _land_shape(x, scatter):
    return (N_DEV - 1, x.shape[0] // N_DEV, x.shape[1]) if scatter else (N_DEV * x.shape[0], x.shape[1])


def _exchange_start(xs, *, name, scatter):
    n = len(xs)
    lands = [lax.empty(_land_shape(x, scatter), x.dtype) for x in xs]

    def body(*refs):
        x_refs, land_refs = refs[:n], refs[n:2 * n]
        send_sems, recv_sems = refs[2 * n:2 * n + 2]
        token = refs[-1]
        for cp in _exchange_copies(x_refs, land_refs, send_sems, recv_sems, scatter):
            cp.start()
        token[...] = jnp.zeros_like(token)

    sems = pltpu.SemaphoreType.DMA(((N_DEV - 1) * n,))
    out = pl.pallas_call(
        body, name=name,
        out_shape=(sems, sems, *[pltpu.HBM(x.shape, x.dtype) for x in xs],
                   *[pltpu.HBM(l.shape, l.dtype) for l in lands], jax.ShapeDtypeStruct((8, LANE), F32)),
        in_specs=(HBM_SPEC,) * (2 * n),
        out_specs=(SEM_SPEC, SEM_SPEC) + (HBM_SPEC,) * (2 * n) + (pl.BlockSpec(memory_space=pltpu.VMEM),),
        input_output_aliases={i: 2 + i for i in range(2 * n)},
        compiler_params=pltpu.CompilerParams(has_side_effects=DATAFLOW))(
            *[pltpu.with_memory_space_constraint(t, pltpu.HBM) for t in list(xs) + lands])
    return out[0], out[1], list(out[2:2 + n]), list(out[2 + n:2 + 2 * n]), out[-1]


def _exchange_wait(started, after, *, name, scatter):
    send_sems, recv_sems, xs, lands, _ = started
    n = len(xs)

    def body(*refs):
        x_refs, land_refs = refs[:n], refs[n:2 * n]
        send_sems, recv_sems = refs[2 * n:2 * n + 2]
        for cp in _exchange_copies(x_refs, land_refs, send_sems, recv_sems, scatter):
            cp.wait_send()
            cp.wait_recv()

    out = pl.pallas_call(
        body, name=name, out_shape=tuple(pltpu.HBM(t.shape, t.dtype) for t in xs + lands),
        in_specs=(HBM_SPEC,) * (2 * n) + (SEM_SPEC, SEM_SPEC, pl.BlockSpec(memory_space=pl.ANY)),
        out_specs=(HBM_SPEC,) * (2 * n), input_output_aliases={i: i for i in range(2 * n)},
        compiler_params=pltpu.CompilerParams(has_side_effects=DATAFLOW))(*xs, *lands, send_sems, recv_sems, after)
    return list(out[:n]), list(out[n:])


def _gather_start(shards, token, *, name):
    shards = [shards[0] + token[0, 0].astype(shards[0].dtype)] + list(shards[1:])
    return _exchange_start(shards, name=name, scatter=False)


def _gather_finish(started, after, me, *, name):
    xs, lands = _exchange_wait(started, after, name=name, scatter=False)
    return [lax.dynamic_update_slice(land, x, (me * x.shape[0], 0)) for land, x in zip(lands, xs)]


def _reduce_start(grads, *, name):
    return _exchange_start(grads, name=name, scatter=True)


def _reduce_finish(started, after, me, *, name):
    sent, gots = _exchange_wait(started, after, name=name + "_wait", scatter=True)
    out = []
    for m, (g, got) in enumerate(zip(sent, gots)):
        rows = g.shape[0] // N_DEV
        own = lax.dynamic_slice(g, (me * rows, 0), (rows, g.shape[1]))
        out.append(_add_received(own, got, name=f"{name}_add{m}"))
    return out


def _pad_rows(a, mult):
    r = (-a.shape[0]) % mult
    return a if r == 0 else jnp.concatenate([a, jnp.zeros((r,) + a.shape[1:], a.dtype)], axis=0)


def kernel(x, mem, mix_norm, mem_norm, w_mem_kv, w_out, hg_w_in, hg_lb, hg_onorm, gm_w_in, gm_ln_g, gm_ln_b, gm_ws, gm_bs, ffn_norm, w_ffn_in, w_ffn_out, final_norm, loss_target, m_mix_norm, m_mem_norm, m_w_mem_kv, m_w_out, m_hg_w_in, m_hg_lb, m_hg_onorm, m_gm_w_in, m_gm_ln_g, m_gm_ln_b, m_gm_ws, m_gm_bs, m_ffn_norm, m_w_ffn_in, m_w_ffn_out, m_final_norm, v_mix_norm, v_mem_norm, v_w_mem_kv, v_w_out, v_hg_w_in, v_hg_lb, v_hg_onorm, v_gm_w_in, v_gm_ln_g, v_gm_ln_b, v_gm_ws, v_gm_bs, v_ffn_norm, v_w_ffn_in, v_w_ffn_out, v_final_norm):
    mx, my, mc = _place()
    me = 4 * mx + 2 * my + mc
    xs = x[0]
    mems = mem[0]
    tgt = loss_target[0]

    hg_t = hg_w_in[0].T.astype(BF16)
    gm_t = gm_w_in[0].T.astype(BF16)
    fi_t = [w_ffn_in[i].T.astype(BF16) for i in range(2)]
    kv_b = [w_mem_kv[i].astype(BF16) for i in range(2)]
    out_b = [w_out[i].astype(BF16) for i in range(2)]
    fo_b = [w_ffn_out[i].astype(BF16) for i in range(2)]
    ln_local = _pad_rows(jnp.concatenate([gm_ln_g, gm_ln_b], axis=0), 8)
    ln_local = jnp.concatenate([ln_local, jnp.zeros((8, LANE - ln_local.shape[1]), F32)], axis=1)
    ln_all, token = _all_gather(ln_local, name="gather_ln", in_vmem=True, with_token=True)
    ln_all = ln_all.reshape(N_DEV, 8, LANE)
    ln_g = ln_all[:, 0, :D_TOK // N_DEV].reshape(1, D_TOK)
    ln_b = ln_all[:, 1, :D_TOK // N_DEV].reshape(1, D_TOK)
    W_hgT, token = _all_gather(hg_t + token[0, 0].astype(BF16), name="gather_first", in_vmem=False,
                               with_token=True)
    gather_mix = _gather_start(kv_b + out_b, token, name="gather_mix_start")
    gather_fi = [_gather_start([fi_t[0]], gather_mix[4], name="gather_fi0_start")]
    gather_fo = [_gather_start([fo_b[0]], gather_fi[0][4], name="gather_fo0_start")]
    gather_gm = _gather_start([gm_t], gather_fo[0][4], name="gather_gm_start")
    gather_fi.append(_gather_start([fi_t[1]], gather_gm[4], name="gather_fi1_start"))
    gather_fo.append(_gather_start([fo_b[1]], gather_fi[1][4], name="gather_fo1_start"))

    lb_soft = jax.nn.softmax(hg_lb, axis=0)
    lb0 = lb_soft[0:1]
    bsb = jnp.broadcast_to(gm_bs[0][:, :, None], (N_HEADS, GM_CHUNK, GM_CHUNK))
    ws = gm_ws[0]

    W_fiT, W_fo = [], []

    def ffn_fwd(xin, i):
        hf = _rms_fwd(xin, ffn_norm[i:i + 1], name=f"ffn_norm{i}")
        W_fiT.extend(_gather_finish(gather_fi[i], hf, me, name=f"gather_fi{i}_wait"))
        gu, act = _ffn_in(hf, W_fiT[i], name=f"ffn_in{i}")
        W_fo.extend(_gather_finish(gather_fo[i], act, me, name=f"gather_fo{i}_wait"))
        xout = _matmul(act, W_fo[i], res=xin, name=f"ffn_out{i}")
        return hf, gu, act, xout

    h0 = _rms_fwd(xs, mix_norm[0:1], name="mix_norm0", dep=gather_fo[1][4])
    p0 = _matmul(h0, W_hgT, tb=True, name="hg_in")
    heads0, o0, states = _hgrn2_fwd(p0, lb0, hg_onorm, name="hgrn2_fwd")

    kv0, kv1, wo0, wo1 = _gather_finish(gather_mix, o0, me, name="gather_mix_wait")
    W_kv, W_out = [kv0, kv1], [wo0, wo1]
    mem_n, kv = [], []
    for i in range(2):
        mn = _rms_fwd(mems, mem_norm[i:i + 1], name=f"mem_norm{i}")
        mem_n.append(mn)
        kv.append(_matmul(mn, W_kv[i], name=f"mem_kv{i}"))

    heads0 = _attn_fwd(p0, 4 * D_TOK // D_MEM, kv[0], heads0, name="attn_fwd0")
    x1 = _matmul(heads0, W_out[0], res=xs, name="out_proj0")
    hf0, gu0, act0, x2 = ffn_fwd(x1, 0)

    h1 = _rms_fwd(x2, mix_norm[1:2], name="mix_norm1")
    W_gmT, = _gather_finish(gather_gm, h1, me, name="gather_gm_wait")
    p1 = _matmul(h1, W_gmT, tb=True, name="gm_in")
    heads1 = _gmlp_fwd(p1, ln_g, ln_b, ws, bsb, name="gmlp_fwd")
    heads1 = _attn_fwd(p1, 2 * D_TOK // D_MEM, kv[1], heads1, name="attn_fwd1")
    x3 = _matmul(heads1, W_out[1], res=x2, name="out_proj1")
    hf1, gu1, act1, x4 = ffn_fwd(x3, 1)

    dx, g_final, loss_part = _final_loss(x4, final_norm.reshape(1, D_MODEL), tgt, name="final_loss")

    def ffn_bwd(dx, xin, hf, gu, act, i, dep):
        dgu = _ffn_out_dx(dx, W_fo[i], gu, dep, name=f"ffn_out_dx{i}")
        g_wfo = _matmul(act, dx, ta=True, out_dtype=BF16, name=f"ffn_out_dw{i}")
        g_wfi_t = _matmul(dgu, hf, ta=True, a_halves=True, out_dtype=BF16, name=f"ffn_in_dw{i}")
        dhf = _matmul(dgu, W_fiT[i], a_halves=True, name=f"ffn_in_dx{i}")
        dx, g_norm = _rms_bwd(xin, ffn_norm[i:i + 1], dhf, dx, name=f"ffn_norm_bwd{i}")
        return dx, g_wfi_t, g_wfo, g_norm

    def mem_bwd(dkv, i):
        g_wkv = _matmul(mem_n[i], dkv, ta=True, out_dtype=BF16, name=f"mem_kv_dw{i}")
        dmn = _matmul(dkv, W_kv[i], tb=True, name=f"mem_kv_dx{i}")
        _, g_norm = _rms_bwd(mems, mem_norm[i:i + 1], dmn, jnp.zeros_like(mems), name=f"mem_norm_bwd{i}")
        return g_wkv, g_norm

    dx, g_wfi1_t, g_wfo1, g_ffn1 = ffn_bwd(dx, x3, hf1, gu1, act1, 1, loss_part)
    dheads = _matmul(dx, W_out[1], tb=True, name="out_proj_dx1")
    g_wout1 = _matmul(heads1, dx, ta=True, out_dtype=BF16, name="out_proj_dw1")
    dp, g_ws, g_bs, g_lng, g_lnb = _gmlp_bwd(p1, ln_g, ln_b, ws, bsb, dheads, name="gmlp_bwd")
    dp, dk, dv = _attn_bwd(p1, 2 * D_TOK // D_MEM, kv[1], dheads, dp, name="attn_bwd1")
    g_wkv1, g_mem1 = mem_bwd(jnp.concatenate([dk, dv], axis=1), 1)
    g_wgm_t = _matmul(dp, h1, ta=True, out_dtype=BF16, name="gm_in_dw")
    dh = _matmul(dp, W_gmT, name="gm_in_dx")
    dx, g_mix1 = _rms_bwd(x2, mix_norm[1:2], dh, dx, name="mix_norm_bwd1")
    reduce_l1 = _reduce_start([g_wkv1, g_wout1, g_wgm_t, g_wfi1_t, g_wfo1], name="reduce_l1_start")

    dx, g_wfi0_t, g_wfo0, g_ffn0 = ffn_bwd(dx, x1, hf0, gu0, act0, 0, reduce_l1[4])
    reduce_ffn0 = _reduce_start([g_wfi0_t, g_wfo0], name="reduce_ffn0_start")
    dheads = _matmul(dx, W_out[0], tb=True, name="out_proj_dx0", dep=reduce_ffn0[4])
    g_wout0 = _matmul(heads0, dx, ta=True, out_dtype=BF16, name="out_proj_dw0")
    dp, g_lb0, g_onorm = _hgrn2_bwd(p0, lb0, hg_onorm, o0, states, dheads, name="hgrn2_bwd")
    dp, dk, dv = _attn_bwd(p0, 4 * D_TOK // D_MEM, kv[0], dheads, dp, name="attn_bwd0")
    g_wkv0, g_mem0 = mem_bwd(jnp.concatenate([dk, dv], axis=1), 0)
    g_whg_t = _matmul(dp, h0, ta=True, out_dtype=BF16, name="hg_in_dw")
    reduce_mix0 = _reduce_start([g_wkv0, g_wout0, g_whg_t], name="reduce_mix0_start")
    dh = _matmul(dp, W_hgT, name="hg_in_dx", dep=reduce_mix0[4])
    grad_x, g_mix0 = _rms_bwd(xs, mix_norm[0:1], dh, dx, name="mix_norm_bwd0")

    g_kv1, g_out1, g_gm_t, g_fi1_t, g_fo1 = _reduce_finish(reduce_l1, grad_x, me, name="reduce_l1")
    g_fi0_t, g_fo0 = _reduce_finish(reduce_ffn0, g_kv1, me, name="reduce_ffn0")
    g_kv0, g_out0, g_hg_t = _reduce_finish(reduce_mix0, g_fi0_t, me, name="reduce_mix0")
    g_shards = [jnp.stack([g_kv0, g_kv1]), jnp.stack([g_out0, g_out1]), g_hg_t.T[None], g_gm_t.T[None],
                jnp.stack([g_fi0_t.T, g_fi1_t.T]), jnp.stack([g_fo0, g_fo1])]

    small = [loss_part, jnp.concatenate([g_mix0, g_mix1], axis=1), jnp.concatenate([g_mem0, g_mem1], axis=1),
             g_lb0, g_onorm, g_lng, g_lnb, g_ws.reshape(1, -1), g_bs.reshape(1, -1),
             jnp.concatenate([g_ffn0, g_ffn1], axis=1), g_final]
    sizes = [t.shape[1] for t in small]
    small_rows = _pad_rows(jnp.concatenate(small, axis=1).reshape(-1, LANE), 8)
    red = _all_gather(small_rows, name="reduce_small", in_vmem=True, reduce_sum=True).reshape(-1)
    pieces, off = [], 0
    for n in sizes:
        pieces.append(red[off:off + n])
        off += n
    loss = pieces[0][0]
    g_mix_norm = pieces[1].reshape(2, D_MODEL)
    g_mem_norm = pieces[2].reshape(2, D_MODEL)
    g_hg_lb = pieces[3][None, :] * lb0 * (jnp.eye(3, dtype=F32)[:, 0:1] - lb_soft)
    g_hg_onorm = pieces[4].reshape(1, D_TOK)
    width = D_TOK // N_DEV
    g_gm_ln_g = lax.dynamic_slice(pieces[5], (me * width,), (width,)).reshape(1, width)
    g_gm_ln_b = lax.dynamic_slice(pieces[6], (me * width,), (width,)).reshape(1, width)
    g_gm_ws = pieces[7].reshape(gm_ws.shape)
    g_gm_bs = pieces[8].reshape(gm_bs.shape)
    g_ffn_norm = pieces[9].reshape(2, D_MODEL)
    g_final_norm = pieces[10]

    grads = [g_mix_norm, g_mem_norm, g_shards[0], g_shards[1], g_shards[2], g_hg_lb, g_hg_onorm, g_shards[3],
             g_gm_ln_g, g_gm_ln_b, g_gm_ws, g_gm_bs, g_ffn_norm, g_shards[4], g_shards[5], g_final_norm]
    weights = [mix_norm, mem_norm, w_mem_kv, w_out, hg_w_in, hg_lb, hg_onorm, gm_w_in, gm_ln_g, gm_ln_b, gm_ws, gm_bs,
               ffn_norm, w_ffn_in, w_ffn_out, final_norm]
    ms = [m_mix_norm, m_mem_norm, m_w_mem_kv, m_w_out, m_hg_w_in, m_hg_lb, m_hg_onorm, m_gm_w_in, m_gm_ln_g,
          m_gm_ln_b, m_gm_ws, m_gm_bs, m_ffn_norm, m_w_ffn_in, m_w_ffn_out, m_final_norm]
    vs = [v_mix_norm, v_mem_norm, v_w_mem_kv, v_w_out, v_hg_w_in, v_hg_lb, v_hg_onorm, v_gm_w_in, v_gm_ln_g,
          v_gm_ln_b, v_gm_ws, v_gm_bs, v_ffn_norm, v_w_ffn_in, v_w_ffn_out, v_final_norm]
    deltas, new_m, new_v = [], [], []
    for n, (w, g, m, v) in enumerate(zip(weights, grads, ms, vs)):
        if w.ndim == 1:
            d, nm, nv = _adamw(w[None], g.reshape(1, -1), m[None], v[None], name=f"adamw{n}")
            d, nm, nv = d[0], nm[0], nv[0]
        else:
            d, nm, nv = _adamw(w, g.reshape(w.shape), m, v, name=f"adamw{n}")
        deltas.append(d)
        new_m.append(nm)
        new_v.append(nv)
    grads = [g.reshape(w.shape) for g, w in zip(grads, weights)]
    return (loss, grad_x[None], *grads, *deltas, *new_m, *new_v)
```

```python
import functools

import jax
import jax.numpy as jnp
from jax import lax
from jax.experimental import pallas as pl
from jax.experimental.pallas import tpu as pltpu

F32 = jnp.float32
BF16 = jnp.bfloat16
MXU_DTYPE = jnp.bfloat16
MESH_ID = pl.DeviceIdType.MESH

N_DEV = 8
EPS = 1e-6
D_MODEL = 1024
D_TOK = 768
D_MEM = 256
N_HEADS = 6
HEAD = 128
MEM_HEADS = 4
MEM_HDIM = 64
GM_CHUNK = 128
D_FF = 2816
HG_SUB = 16
HG_IN = 4 * D_TOK + D_MEM
GM_IN = 2 * D_TOK + D_MEM
LANE = 128
MXU_COLS = 256

ADAM_LR = 0.001
ADAM_B1 = 0.9
ADAM_B2 = 0.999
ADAM_EPS = 1e-08
ADAM_WD = 0.01
ADAM_STEP = 10

VMEM_LIMIT = 48 * 2 ** 20


def _params(sem=None):
    return pltpu.CompilerParams(dimension_semantics=sem, vmem_limit_bytes=VMEM_LIMIT)


def _tile(n, cap, q=LANE):
    if n <= cap:
        return n
    best = None
    for t in range(q, cap + 1, q):
        if n % t == 0:
            best = t
    assert best is not None, (n, cap, q)
    return best


def _sigmoid(x):
    return 1.0 / (1.0 + jnp.exp(-x))


def _gelu(x):
    return 0.5 * x * (1.0 + lax.erf(x * 0.7071067811865476))


def _gelu_grad(x):
    return 0.5 * (1.0 + lax.erf(x * 0.7071067811865476)) + x * jnp.exp(-0.5 * x * x) * 0.3989422804014327


def _matmul(a, b, *, name, ta=False, tb=False, res=None, out_dtype=F32, a_halves=False, b_halves=False, dep=None,
            norm_gain=None):
    if a_halves and ta:
        K, M = a.shape[1], 2 * a.shape[2]
    elif a_halves:
        M, K = a.shape[1], 2 * a.shape[2]
    else:
        K, M = a.shape if ta else a.shape[::-1]
    if b_halves:
        assert not tb and b.shape[1] == K
        N = 2 * b.shape[2]
    else:
        N = b.shape[0] if tb else b.shape[1]
        assert (b.shape[1] if tb else b.shape[0]) == K
    tm = _tile(M // 2 if (a_halves and ta) else M, 1664 if ta else 1024)
    tn = _tile(N // 2 if b_halves else N, 1792)
    tk = _tile(K // 2 if (a_halves and not ta) else K, 1024 if ta else 1664)
    nk = K // tk
    dims = (((0 if ta else 1,), (1 if tb else 0,)), ((), ()))

    n_in = 2 + (res is not None) + (norm_gain is not None) + (dep is not None)
    if norm_gain is not None:
        assert tn == N, "the fused norm needs whole rows"

    def body(*refs):
        a_ref, b_ref = refs[:2]
        r_ref = refs[2] if res is not None else None
        g_ref = refs[2 + (res is not None)] if norm_gain is not None else None
        o_ref = refs[n_in]
        h_ref = refs[n_in + 1] if norm_gain is not None else None
        acc = None if nk == 1 else refs[-1]
        k = pl.program_id(2)

        def product():
            return lax.dot_general(a_ref[...].astype(MXU_DTYPE), b_ref[...].astype(MXU_DTYPE), dims,
                                   preferred_element_type=F32)

        def finish(r):
            if res is not None:
                r = r + r_ref[...].astype(F32)
            o_ref[...] = r.astype(out_dtype)
            if norm_gain is not None:
                scale = lax.rsqrt(jnp.mean(r * r, axis=-1, keepdims=True) + EPS)
                h_ref[...] = (r * scale * g_ref[...]).astype(h_ref.dtype)

        if nk == 1:
            finish(product())
            return

        @pl.when(k == 0)
        def _():
            acc[...] = product()

        @pl.when((k > 0) & (k < nk - 1))
        def _():
            acc[...] += product()

        @pl.when(k == nk - 1)
        def _():
            finish(acc[...] + product())

    if a_halves and ta:
        mh = M // 2 // tm
        a_spec = pl.BlockSpec((None, tk, tm), lambda i, j, k: (i // mh, k, i % mh))
    elif a_halves:
        kh = nk // 2
        a_spec = pl.BlockSpec((None, tm, tk), lambda i, j, k: (k // kh, i, k % kh))
    elif ta:
        a_spec = pl.BlockSpec((tk, tm), lambda i, j, k: (k, i))
    else:
        a_spec = pl.BlockSpec((tm, tk), lambda i, j, k: (i, k))
    if b_halves:
        nh = N // 2 // tn
        b_spec = pl.BlockSpec((None, tk, tn), lambda i, j, k: (j // nh, k, j % nh))
    elif tb:
        b_spec = pl.BlockSpec((tn, tk), lambda i, j, k: (j, k))
    else:
        b_spec = pl.BlockSpec((tk, tn), lambda i, j, k: (k, j))
    o_spec = pl.BlockSpec((tm, tn), lambda i, j, k: (i, j))
    in_specs = [a_spec, b_spec] + ([o_spec] if res is not None else [])
    args = (a, b) + ((res,) if res is not None else ())
    out_specs, out_shape = o_spec, jax.ShapeDtypeStruct((M, N), out_dtype)
    if norm_gain is not None:
        in_specs.append(pl.BlockSpec((1, N), lambda i, j, k: (0, 0)))
        args += (norm_gain,)
        out_specs, out_shape = [o_spec, o_spec], [out_shape, jax.ShapeDtypeStruct((M, N), BF16)]
    if dep is not None:
        in_specs.append(pl.BlockSpec(memory_space=pl.ANY))
        args += (dep,)
    return pl.pallas_call(
        body, name=name, grid=(M // tm, N // tn, nk), in_specs=in_specs, out_specs=out_specs, out_shape=out_shape,
        scratch_shapes=[] if nk == 1 else [pltpu.VMEM((tm, tn), F32)],
        compiler_params=_params(("parallel", "parallel", "arbitrary")))(*args)


def _ffn_in(hf, wt, *, name):
    S, K = hf.shape
    tm = _tile(S, 512)
    tn = _tile(D_FF, 1408)
    nh = D_FF // tn
    nt = (((1,), (1,)), ((), ()))

    def body(a_ref, bg_ref, bu_ref, gu_ref, act_ref):
        av = a_ref[...].astype(MXU_DTYPE)
        for c0 in range(0, tn, MXU_COLS):
            cs = slice(c0, min(c0 + MXU_COLS, tn))
            gate = lax.dot_general(av, bg_ref[cs, :].astype(MXU_DTYPE), nt, preferred_element_type=F32)
            up = lax.dot_general(av, bu_ref[cs, :].astype(MXU_DTYPE), nt, preferred_element_type=F32)
            gu_ref[0, :, cs] = gate.astype(gu_ref.dtype)
            gu_ref[1, :, cs] = up.astype(gu_ref.dtype)
            act_ref[:, cs] = (gate * _sigmoid(gate) * up).astype(act_ref.dtype)

    return pl.pallas_call(
        body, name=name, grid=(nh, S // tm),
        in_specs=[pl.BlockSpec((tm, K), lambda j, i: (i, 0)), pl.BlockSpec((tn, K), lambda j, i: (j, 0)),
                  pl.BlockSpec((tn, K), lambda j, i: (j + nh, 0))],
        out_specs=[pl.BlockSpec((2, tm, tn), lambda j, i: (0, i, j)), pl.BlockSpec((tm, tn), lambda j, i: (i, j))],
        out_shape=[jax.ShapeDtypeStruct((2, S, D_FF), BF16), jax.ShapeDtypeStruct((S, D_FF), BF16)],
        compiler_params=_params(("parallel", "parallel")))(hf, wt, wt)


def _ffn_out_dx(dx, w, gu, dep, *, name):
    S, K = dx.shape
    tm = _tile(S, 512)
    tn = _tile(D_FF, 1408)

    def body(a_ref, b_ref, gu_ref, dep_ref, o_ref):
        del dep_ref
        av = a_ref[...].astype(MXU_DTYPE)
        for c0 in range(0, tn, MXU_COLS):
            cs = slice(c0, min(c0 + MXU_COLS, tn))
            da = lax.dot_general(av, b_ref[cs, :].astype(MXU_DTYPE), (((1,), (1,)), ((), ())),
                                 preferred_element_type=F32)
            gate = gu_ref[0, :, cs].astype(F32)
            up = gu_ref[1, :, cs].astype(F32)
            sg = _sigmoid(gate)
            o_ref[0, :, cs] = (da * up * sg * (1.0 + gate * (1.0 - sg))).astype(o_ref.dtype)
            o_ref[1, :, cs] = (da * gate * sg).astype(o_ref.dtype)

    halves = pl.BlockSpec((2, tm, tn), lambda j, i: (0, i, j))
    return pl.pallas_call(
        body, name=name, grid=(D_FF // tn, S // tm),
        in_specs=[pl.BlockSpec((tm, K), lambda j, i: (i, 0)), pl.BlockSpec((tn, K), lambda j, i: (j, 0)), halves,
                  pl.BlockSpec(memory_space=pl.ANY)],
        out_specs=halves, out_shape=jax.ShapeDtypeStruct((2, S, D_FF), BF16),
        compiler_params=_params(("parallel", "parallel")))(dx, w, gu, dep)


def _rms_fwd(x, g, *, name, dep=None):
    R, Dm = x.shape
    tr = _tile(R, 512, 8)

    def body(x_ref, g_ref, *rest):
        o_ref = rest[-1]
        xv = x_ref[...]
        r = lax.rsqrt(jnp.mean(xv * xv, axis=-1, keepdims=True) + EPS)
        o_ref[...] = (xv * r * g_ref[...]).astype(o_ref.dtype)

    in_specs = [pl.BlockSpec((tr, Dm), lambda i: (i, 0)), pl.BlockSpec((1, Dm), lambda i: (0, 0))]
    args = (x, g)
    if dep is not None:
        in_specs.append(pl.BlockSpec(memory_space=pl.ANY))
        args += (dep,)
    return pl.pallas_call(
        body, name=name, grid=(R // tr,), in_specs=in_specs,
        out_specs=pl.BlockSpec((tr, Dm), lambda i: (i, 0)), out_shape=jax.ShapeDtypeStruct((R, Dm), BF16),
        compiler_params=_params(("parallel",)))(*args)


def _rms_bwd(x, g, dh, dres, *, name):
    R, Dm = x.shape
    tr = _tile(R, 256, 8)

    def body(x_ref, g_ref, dh_ref, dres_ref, dx_ref, dg_ref):
        @pl.when(pl.program_id(0) == 0)
        def _():
            dg_ref[...] = jnp.zeros_like(dg_ref)

        xv = x_ref[...]
        r = lax.rsqrt(jnp.mean(xv * xv, axis=-1, keepdims=True) + EPS)
        xh = xv * r
        dhv = dh_ref[...].astype(F32)
        dg_ref[...] += jnp.sum(dhv * xh, axis=0, keepdims=True)
        u = dhv * g_ref[...]
        dx = r * (u - xh * jnp.mean(u * xh, axis=-1, keepdims=True))
        dx_ref[...] = dres_ref[...] + dx

    row = pl.BlockSpec((tr, Dm), lambda i: (i, 0))
    vec = pl.BlockSpec((1, Dm), lambda i: (0, 0))
    return pl.pallas_call(
        body, name=name, grid=(R // tr,), in_specs=[row, vec, row, row], out_specs=[row, vec],
        out_shape=[jax.ShapeDtypeStruct((R, Dm), F32), jax.ShapeDtypeStruct((1, Dm), F32)],
        compiler_params=_params(("arbitrary",)))(x, g, dh, dres)


def _final_loss(x, g, tgt, *, name):
    R, Dm = x.shape
    tr = _tile(R, 256, 8)

    def body(x_ref, g_ref, t_ref, dx_ref, dg_ref, loss_ref):
        @pl.when(pl.program_id(0) == 0)
        def _():
            dg_ref[...] = jnp.zeros_like(dg_ref)
            loss_ref[...] = jnp.zeros_like(loss_ref)

        xv = x_ref[...]
        r = lax.rsqrt(jnp.mean(xv * xv, axis=-1, keepdims=True) + EPS)
        xh = xv * r
        gv = g_ref[...]
        err = xh * gv - t_ref[...]
        part = 0.5 * jnp.sum(jnp.mean(err * err, axis=-1, keepdims=True), axis=0, keepdims=True)
        loss_ref[...] += jnp.broadcast_to(part, loss_ref.shape)
        dy = err * (1.0 / Dm)
        dg_ref[...] += jnp.sum(dy * xh, axis=0, keepdims=True)
        u = dy * gv
        dx_ref[...] = r * (u - xh * jnp.mean(u * xh, axis=-1, keepdims=True))

    row = pl.BlockSpec((tr, Dm), lambda i: (i, 0))
    vec = pl.BlockSpec((1, Dm), lambda i: (0, 0))
    one = pl.BlockSpec((1, LANE), lambda i: (0, 0))
    return pl.pallas_call(
        body, name=name, grid=(R // tr,), in_specs=[row, vec, row], out_specs=[row, vec, one],
        out_shape=[jax.ShapeDtypeStruct((R, Dm), F32), jax.ShapeDtypeStruct((1, Dm), F32),
                   jax.ShapeDtypeStruct((1, LANE), F32)],
        compiler_params=_params(("arbitrary",)))(x, g, tgt)


def _head_mask(h):
    lane = lax.broadcasted_iota(jnp.int32, (1, D_MEM), 1)
    return (lane >= h * MEM_HDIM) & (lane < (h + 1) * MEM_HDIM)


def _attn_probs(qv, k_mx, mask):
    s = lax.dot_general(jnp.where(mask, qv, 0.0).astype(MXU_DTYPE), k_mx, (((1,), (1,)), ((), ())),
                        preferred_element_type=F32) * (MEM_HDIM ** -0.5)
    e = jnp.exp(s - jnp.max(s, axis=-1, keepdims=True))
    return e / jnp.sum(e, axis=-1, keepdims=True)


def _attn_fwd(p, qcol, kv, heads, *, name):
    S = p.shape[0]
    M = kv.shape[0]
    ts = _tile(S, 512, 8)

    def body(q_ref, k_ref, v_ref, heads_in, o_ref):
        del heads_in
        qv = q_ref[...]
        kx = k_ref[...].astype(MXU_DTYPE)
        vv = v_ref[...]
        out = jnp.zeros((ts, D_MEM), F32)
        for h in range(MEM_HEADS):
            mask = _head_mask(h)
            pr = _attn_probs(qv, kx, mask)
            out = out + jnp.dot(pr.astype(MXU_DTYPE), jnp.where(mask, vv, 0.0).astype(MXU_DTYPE),
                                preferred_element_type=F32)
        o_ref[...] = out.astype(o_ref.dtype)

    return pl.pallas_call(
        body, name=name, grid=(S // ts,),
        in_specs=[pl.BlockSpec((ts, D_MEM), lambda i: (i, qcol)), pl.BlockSpec((M, D_MEM), lambda i: (0, 0)),
                  pl.BlockSpec((M, D_MEM), lambda i: (0, 1)), pl.BlockSpec(memory_space=pl.ANY)],
        out_specs=pl.BlockSpec((ts, D_MEM), lambda i: (i, D_TOK // D_MEM)),
        out_shape=jax.ShapeDtypeStruct(heads.shape, heads.dtype), input_output_aliases={3: 0},
        compiler_params=_params(("parallel",)))(p, kv, kv, heads)


def _attn_bwd(p, qcol, kv, dheads, dp, *, name):
    S = p.shape[0]
    M = kv.shape[0]
    ts = _tile(S, 512, 8)
    scale = MEM_HDIM ** -0.5

    def body(q_ref, k_ref, v_ref, do_ref, dp_in, dq_ref, dk_ref, dv_ref):
        del dp_in

        @pl.when(pl.program_id(0) == 0)
        def _():
            dk_ref[...] = jnp.zeros_like(dk_ref)
            dv_ref[...] = jnp.zeros_like(dv_ref)

        qv = q_ref[...]
        kv_ = k_ref[...]
        kx = kv_.astype(MXU_DTYPE)
        vv = v_ref[...]
        dox = do_ref[...].astype(MXU_DTYPE)
        qx = qv.astype(MXU_DTYPE)
        dq = jnp.zeros((ts, D_MEM), F32)
        for h in range(MEM_HEADS):
            mask = _head_mask(h)
            pr = _attn_probs(qv, kx, mask)
            vh = jnp.where(mask, vv, 0.0).astype(MXU_DTYPE)
            dpr = lax.dot_general(dox, vh, (((1,), (1,)), ((), ())), preferred_element_type=F32)
            ds = (pr * (dpr - jnp.sum(dpr * pr, axis=-1, keepdims=True)) * scale).astype(MXU_DTYPE)
            dq = dq + jnp.dot(ds, jnp.where(mask, kv_, 0.0).astype(MXU_DTYPE), preferred_element_type=F32)
            dk_h = lax.dot_general(ds, qx, (((0,), (0,)), ((), ())), preferred_element_type=F32)
            dv_h = lax.dot_general(pr.astype(MXU_DTYPE), dox, (((0,), (0,)), ((), ())), preferred_element_type=F32)
            dk_ref[...] += jnp.where(mask, dk_h, 0.0)
            dv_ref[...] += jnp.where(mask, dv_h, 0.0)
        dq_ref[...] = dq.astype(dq_ref.dtype)

    return pl.pallas_call(
        body, name=name, grid=(S // ts,),
        in_specs=[pl.BlockSpec((ts, D_MEM), lambda i: (i, qcol)), pl.BlockSpec((M, D_MEM), lambda i: (0, 0)),
                  pl.BlockSpec((M, D_MEM), lambda i: (0, 1)),
                  pl.BlockSpec((ts, D_MEM), lambda i: (i, D_TOK // D_MEM)), pl.BlockSpec(memory_space=pl.ANY)],
        out_specs=[pl.BlockSpec((ts, D_MEM), lambda i: (i, qcol)), pl.BlockSpec((M, D_MEM), lambda i: (0, 0)),
                   pl.BlockSpec((M, D_MEM), lambda i: (0, 0))],
        out_shape=[jax.ShapeDtypeStruct(dp.shape, dp.dtype), jax.ShapeDtypeStruct((M, D_MEM), F32),
                   jax.ShapeDtypeStruct((M, D_MEM), F32)],
        input_output_aliases={4: 0}, compiler_params=_params(("arbitrary",)))(p, kv, kv, dheads, dp)


def _gm_forward_parts(u_ref, v_ref, lng_ref, lnb_ref, w_ref, bsb_ref):
    zu = _gelu(u_ref[...])
    zv = _gelu(v_ref[...])
    mu = jnp.mean(zv, axis=-1, keepdims=True)
    cen = zv - mu
    rs = lax.rsqrt(jnp.mean(cen * cen, axis=-1, keepdims=True) + EPS)
    vh = cen * rs
    vn = vh * lng_ref[...] + lnb_ref[...]
    row = lax.broadcasted_iota(jnp.int32, (GM_CHUNK, GM_CHUNK), 0)
    col = lax.broadcasted_iota(jnp.int32, (GM_CHUNK, GM_CHUNK), 1)
    tril = row >= col
    wm = [jnp.where(tril, w_ref[g], 0.0).astype(MXU_DTYPE) for g in range(N_HEADS)]
    vnx = [vn[:, g * HEAD:(g + 1) * HEAD].astype(MXU_DTYPE) for g in range(N_HEADS)]
    sv = [jnp.dot(wm[g], vnx[g], preferred_element_type=F32) + bsb_ref[g] for g in range(N_HEADS)]
    return zu, vh, rs, wm, vnx, sv, tril


def _gmlp_fwd(p, lng, lnb, ws, bsb, *, name):
    S = p.shape[0]

    def body(u_ref, v_ref, lng_ref, lnb_ref, w_ref, bsb_ref, o_ref):
        zu, _, _, _, _, sv, _ = _gm_forward_parts(u_ref, v_ref, lng_ref, lnb_ref, w_ref, bsb_ref)
        for g in range(N_HEADS):
            o_ref[:, g * HEAD:(g + 1) * HEAD] = (zu[:, g * HEAD:(g + 1) * HEAD] * sv[g]).astype(o_ref.dtype)

    blk = lambda c: pl.BlockSpec((GM_CHUNK, D_TOK), lambda i: (i, c))
    vec = pl.BlockSpec((1, D_TOK), lambda i: (0, 0))
    cube = pl.BlockSpec((N_HEADS, GM_CHUNK, GM_CHUNK), lambda i: (0, 0, 0))
    return pl.pallas_call(
        body, name=name, grid=(S // GM_CHUNK,), in_specs=[blk(0), blk(1), vec, vec, cube, cube],
        out_specs=blk(0), out_shape=jax.ShapeDtypeStruct((S, D_MODEL), BF16),
        compiler_params=_params(("parallel",)))(p, p, lng, lnb, ws, bsb)


def _gmlp_bwd(p, lng, lnb, ws, bsb, dheads, *, name):
    S = p.shape[0]

    def body(u_ref, v_ref, lng_ref, lnb_ref, w_ref, bsb_ref, dt_ref, dp_ref, dw_ref, dbs_ref, dlg_ref, dlb_ref):
        @pl.when(pl.program_id(0) == 0)
        def _():
            dw_ref[...] = jnp.zeros_like(dw_ref)
            dbs_ref[...] = jnp.zeros_like(dbs_ref)
            dlg_ref[...] = jnp.zeros_like(dlg_ref)
            dlb_ref[...] = jnp.zeros_like(dlb_ref)

        zu, vh, rs, wm, vnx, sv, tril = _gm_forward_parts(u_ref, v_ref, lng_ref, lnb_ref, w_ref, bsb_ref)
        dt = dt_ref[...].astype(F32)
        dvn_parts = []
        for g in range(N_HEADS):
            sl = slice(g * HEAD, (g + 1) * HEAD)
            dsv = dt[:, sl] * zu[:, sl]
            dp_ref[:, sl] = (dt[:, sl] * sv[g] * _gelu_grad(u_ref[:, sl])).astype(dp_ref.dtype)
            dsx = dsv.astype(MXU_DTYPE)
            dw = lax.dot_general(dsx, vnx[g], (((1,), (1,)), ((), ())), preferred_element_type=F32)
            dw_ref[g] += jnp.where(tril, dw, 0.0)
            dbs_ref[g] += jnp.sum(dsv, axis=-1, keepdims=True)
            dvn_parts.append(lax.dot_general(wm[g], dsx, (((0,), (0,)), ((), ())), preferred_element_type=F32))
        dvn = jnp.concatenate(dvn_parts, axis=-1)
        dlg_ref[...] += jnp.sum(dvn * vh, axis=0, keepdims=True)
        dlb_ref[...] += jnp.sum(dvn, axis=0, keepdims=True)
        dvh = dvn * lng_ref[...]
        dzv = rs * (dvh - jnp.mean(dvh, axis=-1, keepdims=True) - vh * jnp.mean(dvh * vh, axis=-1, keepdims=True))
        dp_ref[:, D_TOK:] = (dzv * _gelu_grad(v_ref[...])).astype(dp_ref.dtype)

    blk = lambda c: pl.BlockSpec((GM_CHUNK, D_TOK), lambda i: (i, c))
    vec = pl.BlockSpec((1, D_TOK), lambda i: (0, 0))
    cube = pl.BlockSpec((N_HEADS, GM_CHUNK, GM_CHUNK), lambda i: (0, 0, 0))
    col = pl.BlockSpec((N_HEADS, GM_CHUNK, 1), lambda i: (0, 0, 0))
    return pl.pallas_call(
        body, name=name, grid=(S // GM_CHUNK,), in_specs=[blk(0), blk(1), vec, vec, cube, cube, blk(0)],
        out_specs=[pl.BlockSpec((GM_CHUNK, 2 * D_TOK), lambda i: (i, 0)), cube, col, vec, vec],
        out_shape=[jax.ShapeDtypeStruct((S, GM_IN), BF16), jax.ShapeDtypeStruct((N_HEADS, GM_CHUNK, GM_CHUNK), F32),
                   jax.ShapeDtypeStruct((N_HEADS, GM_CHUNK, 1), F32), jax.ShapeDtypeStruct((1, D_TOK), F32),
                   jax.ShapeDtypeStruct((1, D_TOK), F32)],
        compiler_params=_params(("arbitrary",)))(p, p, lng, lnb, ws, bsb, dheads)


def _chunk_tri(n, chunk, upper):
    r = lax.broadcasted_iota(jnp.int32, (n, n), 0)
    c = lax.broadcasted_iota(jnp.int32, (n, n), 1)
    same = (r // chunk) == (c // chunk)
    return jnp.where(same & ((r <= c) if upper else (r >= c)), 1.0, 0.0).astype(F32)


def _hg_gates(fz, lb):
    sg = _sigmoid(fz)
    f = lb + (1.0 - lb) * sg
    kk = (1.0 - lb) * (1.0 - sg)
    return sg, f, jnp.log(f), kk


def _hgrn2_fwd(p, lb, onorm, *, name):
    S = p.shape[0]
    C = HG_SUB
    tb = _tile(S, 256, C)
    nsub = tb // C

    def body(q_ref, fz_ref, v_ref, g_ref, lb_ref, on_ref, tok_ref, o_ref, st_ref, state, b_blk, k_blk, bsc, ksc, vsc):
        @pl.when(pl.program_id(0) == 0)
        def _():
            state[...] = jnp.zeros_like(state)

        _, _, lg, kk = _hg_gates(fz_ref[...], lb_ref[...])
        b_blk[...] = jnp.dot(_chunk_tri(tb, C, False), lg, precision=lax.Precision.HIGHEST,
                             preferred_element_type=F32)
        k_blk[...] = kk
        tt = lax.broadcasted_iota(jnp.int32, (C, HEAD), 0)

        def sub(c, carry):
            rows = pl.ds(pl.multiple_of(c * C, C), C)
            for h in range(N_HEADS):
                cols = slice(h * HEAD, (h + 1) * HEAD)
                qv = q_ref[rows, cols]
                vv = v_ref[rows, cols]
                b = b_blk[rows, cols]
                kk = k_blk[rows, cols]
                st0 = state[h]
                st_ref[c, h] = st0
                inter = lax.dot_general((qv * jnp.exp(b)).astype(MXU_DTYPE), st0.astype(MXU_DTYPE),
                                        (((1,), (1,)), ((), ())), preferred_element_type=F32)
                bsc[h] = b
                ksc[h] = kk
                vsc[h] = vv
                intra = jnp.zeros((C, HEAD), F32)
                for s in range(C):
                    dec = jnp.where(tt >= s, jnp.exp(b - bsc[h, pl.ds(s, 1), :]), 0.0)
                    a_s = jnp.sum(qv * ksc[h, pl.ds(s, 1), :] * dec, axis=-1, keepdims=True)
                    intra = intra + a_s * vsc[h, pl.ds(s, 1), :]
                o_ref[rows, cols] = inter + intra
                b_last = bsc[h, pl.ds(C - 1, 1), :]
                ke = kk * jnp.exp(b_last - b)
                state[h] = st0 * jnp.exp(b_last) + lax.dot_general(
                    vv.astype(MXU_DTYPE), ke.astype(MXU_DTYPE), (((0,), (0,)), ((), ())),
                    preferred_element_type=F32)
            return carry

        lax.fori_loop(0, nsub, sub, 0)

        for h in range(N_HEADS):
            cols = slice(h * HEAD, (h + 1) * HEAD)
            o = o_ref[:, cols]
            gv = g_ref[:, cols]
            n = o * lax.rsqrt(jnp.mean(o * o, axis=-1, keepdims=True) + EPS)
            tok_ref[:, cols] = (n * (gv * _sigmoid(gv)) * on_ref[:, cols]).astype(tok_ref.dtype)

    blk = lambda c: pl.BlockSpec((tb, D_TOK), lambda i, c=c: (i, c))
    vec = pl.BlockSpec((1, D_TOK), lambda i: (0, 0))
    stb = pl.BlockSpec((nsub, N_HEADS, HEAD, HEAD), lambda i: (i, 0, 0, 0))
    return pl.pallas_call(
        body, name=name, grid=(S // tb,), in_specs=[blk(0), blk(1), blk(2), blk(3), vec, vec],
        out_specs=[blk(0), blk(0), stb],
        out_shape=[jax.ShapeDtypeStruct((S, D_MODEL), BF16), jax.ShapeDtypeStruct((S, D_TOK), F32),
                   jax.ShapeDtypeStruct((S // C, N_HEADS, HEAD, HEAD), F32)],
        scratch_shapes=[pltpu.VMEM((N_HEADS, HEAD, HEAD), F32)] + [pltpu.VMEM((tb, D_TOK), F32)] * 2
        + [pltpu.VMEM((N_HEADS, C, HEAD), F32)] * 3,
        compiler_params=_params(("arbitrary",)))(p, p, p, p, lb, onorm)


def _hgrn2_bwd(p, lb, onorm, o, states, dheads, *, name):
    S = p.shape[0]
    C = HG_SUB
    tb = _tile(S, 256, C)
    nsub = tb // C
    nblk = S // tb

    def body(q_ref, fz_ref, v_ref, g_ref, lb_ref, on_ref, o_ref, st_ref, dt_ref, dp_ref, dlb_ref, don_ref, dstate,
             b_blk, k_blk, do_blk, db_blk, dk_blk, bsc, ksc, vsc, qsc, dosc):
        @pl.when(pl.program_id(0) == 0)
        def _():
            dstate[...] = jnp.zeros_like(dstate)
            dlb_ref[...] = jnp.zeros_like(dlb_ref)
            don_ref[...] = jnp.zeros_like(don_ref)

        for h in range(N_HEADS):
            cols = slice(h * HEAD, (h + 1) * HEAD)
            onv = on_ref[:, cols]
            gv = g_ref[:, cols]
            ov = o_ref[:, cols]
            dt = dt_ref[:, cols].astype(F32)
            sgg = _sigmoid(gv)
            sil = gv * sgg
            rinv = lax.rsqrt(jnp.mean(ov * ov, axis=-1, keepdims=True) + EPS)
            n = ov * rinv
            don_ref[:, cols] += jnp.sum(dt * n * sil, axis=0, keepdims=True)
            dn = dt * sil * onv
            dp_ref[:, 3 * D_TOK + h * HEAD:3 * D_TOK + (h + 1) * HEAD] = (
                dt * n * onv * sgg * (1.0 + gv * (1.0 - sgg))).astype(dp_ref.dtype)
            do_blk[:, cols] = rinv * (dn - n * jnp.mean(dn * n, axis=-1, keepdims=True))
        _, _, lg, kk = _hg_gates(fz_ref[...], lb_ref[...])
        b_blk[...] = jnp.dot(_chunk_tri(tb, C, False), lg, precision=lax.Precision.HIGHEST,
                             preferred_element_type=F32)
        k_blk[...] = kk
        tt = lax.broadcasted_iota(jnp.int32, (C, HEAD), 0)

        def sub(j, carry):
            c = nsub - 1 - j
            rows = pl.ds(pl.multiple_of(c * C, C), C)
            for h in range(N_HEADS):
                cols = slice(h * HEAD, (h + 1) * HEAD)
                qv = q_ref[rows, cols]
                vv = v_ref[rows, cols]
                do = do_blk[rows, cols]
                b = b_blk[rows, cols]
                kk = k_blk[rows, cols]
                bsc[h] = b
                ksc[h] = kk
                vsc[h] = vv
                qsc[h] = qv
                dosc[h] = do
                b_last = bsc[h, pl.ds(C - 1, 1), :]
                eb = jnp.exp(b)
                qe = qv * eb
                ebb = jnp.exp(b_last - b)
                ke = kk * ebb
                e_last = jnp.exp(b_last)
                st0 = st_ref[c, h]
                dst1 = dstate[h]
                st0x = st0.astype(MXU_DTYPE)
                dst1x = dst1.astype(MXU_DTYPE)
                dox = do.astype(MXU_DTYPE)
                dqe = jnp.dot(dox, st0x, preferred_element_type=F32)
                dke = jnp.dot(vv.astype(MXU_DTYPE), dst1x, preferred_element_type=F32)
                dv = lax.dot_general(ke.astype(MXU_DTYPE), dst1x, (((1,), (1,)), ((), ())),
                                     preferred_element_type=F32)
                db_last = (e_last * jnp.sum(st0 * dst1, axis=0, keepdims=True)
                           + jnp.sum(dke * ke, axis=0, keepdims=True))
                dstate[h] = dst1 * e_last + lax.dot_general(dox, qe.astype(MXU_DTYPE), (((0,), (0,)), ((), ())),
                                                            preferred_element_type=F32)
                dq = dqe * eb
                db = dqe * qe - dke * ke
                dkk = dke * ebb
                for s in range(C):
                    dec = jnp.where(tt >= s, jnp.exp(b - bsc[h, pl.ds(s, 1), :]), 0.0)
                    da_s = jnp.sum(do * vsc[h, pl.ds(s, 1), :], axis=-1, keepdims=True)
                    pq = da_s * ksc[h, pl.ds(s, 1), :] * dec
                    dq = dq + pq
                    db = db + pq * qv
                for t in range(C):
                    q_t = qsc[h, pl.ds(t, 1), :]
                    do_t = dosc[h, pl.ds(t, 1), :]
                    dec = jnp.where(tt <= t, jnp.exp(bsc[h, pl.ds(t, 1), :] - b), 0.0)
                    da_t = jnp.sum(vv * do_t, axis=-1, keepdims=True)
                    pk = da_t * q_t * dec
                    dkk = dkk + pk
                    db = db - pk * kk
                    a_t = jnp.sum(q_t * kk * dec, axis=-1, keepdims=True)
                    dv = dv + a_t * do_t
                db_blk[rows, cols] = db + jnp.where(tt == C - 1, db_last, 0.0)
                dk_blk[rows, cols] = dkk
                dp_ref[rows, cols] = dq.astype(dp_ref.dtype)
                dp_ref[rows, 2 * D_TOK + h * HEAD:2 * D_TOK + (h + 1) * HEAD] = dv.astype(dp_ref.dtype)
            return carry

        lax.fori_loop(0, nsub, sub, 0)

        dlg = jnp.dot(_chunk_tri(tb, C, True), db_blk[...], precision=lax.Precision.HIGHEST,
                      preferred_element_type=F32)
        lbv = lb_ref[...]
        sg, f, _, _ = _hg_gates(fz_ref[...], lbv)
        w = dlg / f - dk_blk[...]
        dp_ref[:, D_TOK:2 * D_TOK] = (w * (1.0 - lbv) * sg * (1.0 - sg)).astype(dp_ref.dtype)
        dlb_ref[...] += jnp.sum(w * (1.0 - sg), axis=0, keepdims=True)

    blk = lambda c: pl.BlockSpec((tb, D_TOK), lambda i, c=c: (nblk - 1 - i, c))
    vec = pl.BlockSpec((1, D_TOK), lambda i: (0, 0))
    stb = pl.BlockSpec((nsub, N_HEADS, HEAD, HEAD), lambda i: (nblk - 1 - i, 0, 0, 0))
    small = jax.ShapeDtypeStruct((1, D_TOK), F32)
    return pl.pallas_call(
        body, name=name, grid=(nblk,), in_specs=[blk(0), blk(1), blk(2), blk(3), vec, vec, blk(0), stb, blk(0)],
        out_specs=[pl.BlockSpec((tb, 4 * D_TOK), lambda i: (nblk - 1 - i, 0)), vec, vec],
        out_shape=[jax.ShapeDtypeStruct((S, HG_IN), BF16), small, small],
        scratch_shapes=[pltpu.VMEM((N_HEADS, HEAD, HEAD), F32)] + [pltpu.VMEM((tb, D_TOK), F32)] * 5
        + [pltpu.VMEM((N_HEADS, C, HEAD), F32)] * 5,
        compiler_params=_params(("arbitrary",)))(p, p, p, p, lb, onorm, o, states, dheads)


def _adamw(w, g, m, v, *, name):
    shape = w.shape
    cols = shape[-1]
    w2, g2, m2, v2 = (t.reshape(-1, cols) for t in (w, g, m, v))
    R = w2.shape[0]
    tr = _tile(R, 512, 8)

    def body(w_ref, g_ref, m_ref, v_ref, d_ref, nm_ref, nv_ref):
        gv = g_ref[...]
        nm = ADAM_B1 * m_ref[...] + (1.0 - ADAM_B1) * gv
        nv = ADAM_B2 * v_ref[...] + (1.0 - ADAM_B2) * (gv * gv)
        m_hat = nm / (1.0 - ADAM_B1 ** ADAM_STEP)
        v_hat = nv / (1.0 - ADAM_B2 ** ADAM_STEP)
        d_ref[...] = -ADAM_LR * (m_hat / (jnp.sqrt(v_hat) + ADAM_EPS) + ADAM_WD * w_ref[...])
        nm_ref[...] = nm
        nv_ref[...] = nv

    spec = pl.BlockSpec((tr, cols), lambda i: (i, 0))
    out = jax.ShapeDtypeStruct((R, cols), F32)
    d, nm, nv = pl.pallas_call(body, name=name, grid=(R // tr,), in_specs=[spec] * 4, out_specs=[spec] * 3,
                               out_shape=[out] * 3, compiler_params=_params(("parallel",)))(w2, g2, m2, v2)
    return d.reshape(shape), nm.reshape(shape), nv.reshape(shape)


def _add_received(own, got, *, name):
    R, Cc = own.shape
    n = got.shape[0]
    tr = _tile(R, 256, 16)

    def body(a_ref, b_ref, o_ref):
        acc = a_ref[...].astype(F32)
        for k in range(n):
            acc = acc + b_ref[k].astype(F32)
        o_ref[...] = acc

    return pl.pallas_call(
        body, name=name, grid=(R // tr,),
        in_specs=[pl.BlockSpec((tr, Cc), lambda i: (i, 0)), pl.BlockSpec((n, tr, Cc), lambda i: (0, i, 0))],
        out_specs=pl.BlockSpec((tr, Cc), lambda i: (i, 0)), out_shape=jax.ShapeDtypeStruct((R, Cc), F32),
        compiler_params=_params(("parallel",)))(own, got)


def _place():
    return lax.axis_index("x"), lax.axis_index("y"), lax.axis_index("c")


def _all_gather(x, *, name, in_vmem, reduce_sum=False, with_token=False):
    R, Cc = x.shape
    space = pltpu.VMEM if in_vmem else pl.ANY

    def body(x_ref, out_ref, *scratch):
        if with_token:
            scratch[0][...] = jnp.zeros_like(scratch[0])
            scratch = scratch[1:]
        if reduce_sum:
            gat_ref, send_sems, recv_sems, local_sem = scratch
        else:
            gat_ref = out_ref
            send_sems, recv_sems, local_sem = scratch
        mx, my, mc = _place()
        me, sibling = (mx, my, mc), (mx, my, 1 - mc)
        chips = [(1 - mx, my), (mx, 1 - my), (1 - mx, 1 - my)]

        def rows(px, py, pc):
            return gat_ref.at[pl.ds((4 * px + 2 * py + pc) * R, R), :]

        def copy(k, block, to, src=None):
            return pltpu.make_async_remote_copy(
                src_ref=rows(*block) if src is None else src, dst_ref=rows(*block), send_sem=send_sems.at[k],
                recv_sem=recv_sems.at[k], device_id=to, device_id_type=MESH_ID)

        mine = pltpu.make_async_copy(x_ref, rows(*me), local_sem)
        mine.start()
        first = [copy(0, me, sibling, src=x_ref)]
        first += [copy(1 + j, me, (*chip, mc), src=x_ref) for j, chip in enumerate(chips)]
        for cp in first:
            cp.start()
        passed = [copy(4 + j, (*chip, mc), sibling) for j, chip in enumerate(chips)]
        for j, chip in enumerate(chips):
            copy(1 + j, (*chip, mc), me).wait_recv()
            passed[j].start()
        copy(0, sibling, me).wait_recv()
        for j, chip in enumerate(chips):
            copy(4 + j, (*chip, 1 - mc), me).wait_recv()
        for cp in first + passed:
            cp.wait_send()
        mine.wait()
        if reduce_sum:
            acc = gat_ref[pl.ds(0, R), :]
            for d in range(1, N_DEV):
                acc = acc + gat_ref[pl.ds(d * R, R), :]
            out_ref[...] = acc

    sems = [pltpu.SemaphoreType.DMA((7,)), pltpu.SemaphoreType.DMA((7,)), pltpu.SemaphoreType.DMA]
    if reduce_sum:
        assert in_vmem
        out_shape = jax.ShapeDtypeStruct((R, Cc), x.dtype)
        scratch = [pltpu.VMEM((N_DEV * R, Cc), x.dtype)] + sems
    else:
        out_shape = jax.ShapeDtypeStruct((N_DEV * R, Cc), x.dtype)
        scratch = sems
    out_specs = pl.BlockSpec(memory_space=space)
    if with_token:
        out_shape = (out_shape, jax.ShapeDtypeStruct((8, LANE), F32))
        out_specs = (out_specs, pl.BlockSpec(memory_space=pltpu.VMEM))
    return pl.pallas_call(
        body, name=name, out_shape=out_shape, in_specs=[pl.BlockSpec(memory_space=space)], out_specs=out_specs,
        scratch_shapes=scratch, compiler_params=pltpu.CompilerParams(vmem_limit_bytes=VMEM_LIMIT))(x)


def _peer(k, mx, my, mc):
    bits = k + 1
    return (1 - mx if bits & 4 else mx, 1 - my if bits & 2 else my, 1 - mc if bits & 1 else mc)


HBM_SPEC = pl.BlockSpec(memory_space=pltpu.HBM)
SEM_SPEC = pl.BlockSpec(memory_space=pltpu.SEMAPHORE)
DATAFLOW = pltpu.SideEffectType.DATAFLOW_SIDE_EFFECTING


def _exchange_copies(x_refs, land_refs, send_sems, recv_sems, scatter):
    mx, my, mc = _place()
    me = 4 * mx + 2 * my + mc
    n = len(x_refs)
    copies = []
    for k in range(N_DEV - 1):
        px, py, pc = _peer(k, mx, my, mc)
        for m, (x_ref, land_ref) in enumerate(zip(x_refs, land_refs)):
            rows = land_ref.shape[1] if scatter else x_ref.shape[0]
            if scatter:
                src = x_ref.at[pl.ds(pl.multiple_of((4 * px + 2 * py + pc) * rows, 16), rows), :]
                dst = land_ref.at[k]
            else:
                src = x_ref
                dst = land_ref.at[pl.ds(pl.multiple_of(me * rows, 16), rows), :]
            copies.append(pltpu.make_async_remote_copy(
                src_ref=src, dst_ref=dst, send_sem=send_sems.at[k * n + m], recv_sem=recv_sems.at[k * n + m],
                device_id=(px, py, pc), device_id_type=MESH_ID))
    return copies


def _land_shape(x, scatter):
    return (N_DEV - 1, x.shape[0] // N_DEV, x.shape[1]) if scatter else (N_DEV * x.shape[0], x.shape[1])


def _exchange_start(xs, *, name, scatter):
    n = len(xs)
    lands = [lax.empty(_land_shape(x, scatter), x.dtype) for x in xs]

    def body(*refs):
        x_refs, land_refs = refs[:n], refs[n:2 * n]
        send_sems, recv_sems = refs[2 * n:2 * n + 2]
        token = refs[-1]
        for cp in _exchange_copies(x_refs, land_refs, send_sems, recv_sems, scatter):
            cp.start()
        token[...] = jnp.zeros_like(token)

    sems = pltpu.SemaphoreType.DMA(((N_DEV - 1) * n,))
    out = pl.pallas_call(
        body, name=name,
        out_shape=(sems, sems, *[pltpu.HBM(x.shape, x.dtype) for x in xs],
                   *[pltpu.HBM(l.shape, l.dtype) for l in lands], jax.ShapeDtypeStruct((8, LANE), F32)),
        in_specs=(HBM_SPEC,) * (2 * n),
        out_specs=(SEM_SPEC, SEM_SPEC) + (HBM_SPEC,) * (2 * n) + (pl.BlockSpec(memory_space=pltpu.VMEM),),
        input_output_aliases={i: 2 + i for i in range(2 * n)},
        compiler_params=pltpu.CompilerParams(has_side_effects=DATAFLOW))(
            *[pltpu.with_memory_space_constraint(t, pltpu.HBM) for t in list(xs) + lands])
    return out[0], out[1], list(out[2:2 + n]), list(out[2 + n:2 + 2 * n]), out[-1]


def _exchange_wait(started, after, *, name, scatter):
    send_sems, recv_sems, xs, lands, _ = started
    n = len(xs)

    def body(*refs):
        x_refs, land_refs = refs[:n], refs[n:2 * n]
        send_sems, recv_sems = refs[2 * n:2 * n + 2]
        for cp in _exchange_copies(x_refs, land_refs, send_sems, recv_sems, scatter):
            cp.wait_send()
            cp.wait_recv()
        if not scatter:
            mx, my, mc = _place()
            me = 4 * mx + 2 * my + mc
            local_sem = refs[-1]
            own = [pltpu.make_async_copy(
                x_ref, land_ref.at[pl.ds(pl.multiple_of(me * x_ref.shape[0], 16), x_ref.shape[0]), :], local_sem.at[m])
                for m, (x_ref, land_ref) in enumerate(zip(x_refs, land_refs))]
            for cp in own:
                cp.start()
            for cp in own:
                cp.wait()

    out = pl.pallas_call(
        body, name=name, out_shape=tuple(pltpu.HBM(t.shape, t.dtype) for t in xs + lands),
        in_specs=(HBM_SPEC,) * (2 * n) + (SEM_SPEC, SEM_SPEC, pl.BlockSpec(memory_space=pl.ANY)),
        out_specs=(HBM_SPEC,) * (2 * n), input_output_aliases={i: i for i in range(2 * n)},
        scratch_shapes=[] if scatter else [pltpu.SemaphoreType.DMA((n,))],
        compiler_params=pltpu.CompilerParams(has_side_effects=DATAFLOW))(*xs, *lands, send_sems, recv_sems, after)
    return list(out[:n]), list(out[n:])


def _gather_start(shards, token, *, name):
    shards = [shards[0] + token[0, 0].astype(shards[0].dtype)] + list(shards[1:])
    return _exchange_start(shards, name=name, scatter=False)


def _gather_finish(started, after, me, *, name):
    del me
    return _exchange_wait(started, after, name=name, scatter=False)[1]


def _reduce_start(grads, *, name):
    return _exchange_start(grads, name=name, scatter=True)


def _reduce_finish(started, after, me, *, name):
    sent, gots = _exchange_wait(started, after, name=name + "_wait", scatter=True)
    out = []
    for m, (g, got) in enumerate(zip(sent, gots)):
        rows = g.shape[0] // N_DEV
        own = lax.dynamic_slice(g, (me * rows, 0), (rows, g.shape[1]))
        out.append(_add_received(own, got, name=f"{name}_add{m}"))
    return out


def _pad_rows(a, mult):
    r = (-a.shape[0]) % mult
    return a if r == 0 else jnp.concatenate([a, jnp.zeros((r,) + a.shape[1:], a.dtype)], axis=0)


def kernel(x, mem, mix_norm, mem_norm, w_mem_kv, w_out, hg_w_in, hg_lb, hg_onorm, gm_w_in, gm_ln_g, gm_ln_b, gm_ws, gm_bs, ffn_norm, w_ffn_in, w_ffn_out, final_norm, loss_target, m_mix_norm, m_mem_norm, m_w_mem_kv, m_w_out, m_hg_w_in, m_hg_lb, m_hg_onorm, m_gm_w_in, m_gm_ln_g, m_gm_ln_b, m_gm_ws, m_gm_bs, m_ffn_norm, m_w_ffn_in, m_w_ffn_out, m_final_norm, v_mix_norm, v_mem_norm, v_w_mem_kv, v_w_out, v_hg_w_in, v_hg_lb, v_hg_onorm, v_gm_w_in, v_gm_ln_g, v_gm_ln_b, v_gm_ws, v_gm_bs, v_ffn_norm, v_w_ffn_in, v_w_ffn_out, v_final_norm):
    mx, my, mc = _place()
    me = 4 * mx + 2 * my + mc
    xs = x[0]
    mems = mem[0]
    tgt = loss_target[0]

    hg_t = hg_w_in[0].T.astype(BF16)
    gm_t = gm_w_in[0].T.astype(BF16)
    fi_t = [w_ffn_in[i].T.astype(BF16) for i in range(2)]
    kv_b = [w_mem_kv[i].astype(BF16) for i in range(2)]
    out_b = [w_out[i].astype(BF16) for i in range(2)]
    fo_b = [w_ffn_out[i].astype(BF16) for i in range(2)]
    ln_local = _pad_rows(jnp.concatenate([gm_ln_g, gm_ln_b], axis=0), 8)
    ln_local = jnp.concatenate([ln_local, jnp.zeros((8, LANE - ln_local.shape[1]), F32)], axis=1)
    ln_all, token = _all_gather(ln_local, name="gather_ln", in_vmem=True, with_token=True)
    ln_all = ln_all.reshape(N_DEV, 8, LANE)
    ln_g = ln_all[:, 0, :D_TOK // N_DEV].reshape(1, D_TOK)
    ln_b = ln_all[:, 1, :D_TOK // N_DEV].reshape(1, D_TOK)
    W_hgT, token = _all_gather(hg_t + token[0, 0].astype(BF16), name="gather_first", in_vmem=False,
                               with_token=True)
    gather_mix = _gather_start(kv_b + out_b, token, name="gather_mix_start")
    gather_fi = [_gather_start([fi_t[0]], gather_mix[4], name="gather_fi0_start")]
    gather_fo = [_gather_start([fo_b[0]], gather_fi[0][4], name="gather_fo0_start")]
    gather_gm = _gather_start([gm_t], gather_fo[0][4], name="gather_gm_start")
    gather_fi.append(_gather_start([fi_t[1]], gather_gm[4], name="gather_fi1_start"))
    gather_fo.append(_gather_start([fo_b[1]], gather_fi[1][4], name="gather_fo1_start"))

    lb_soft = jax.nn.softmax(hg_lb, axis=0)
    lb0 = lb_soft[0:1]
    bsb = jnp.broadcast_to(gm_bs[0][:, :, None], (N_HEADS, GM_CHUNK, GM_CHUNK))
    ws = gm_ws[0]

    W_fiT, W_fo = [], []

    def ffn_fwd(xin, hf, i, next_gain):
        W_fiT.extend(_gather_finish(gather_fi[i], hf, me, name=f"gather_fi{i}_wait"))
        gu, act = _ffn_in(hf, W_fiT[i], name=f"ffn_in{i}")
        W_fo.extend(_gather_finish(gather_fo[i], act, me, name=f"gather_fo{i}_wait"))
        return gu, act, _matmul(act, W_fo[i], res=xin, norm_gain=next_gain, name=f"ffn_out{i}")

    h0 = _rms_fwd(xs, mix_norm[0:1], name="mix_norm0", dep=gather_fo[1][4])
    p0 = _matmul(h0, W_hgT, tb=True, name="hg_in")
    heads0, o0, states = _hgrn2_fwd(p0, lb0, hg_onorm, name="hgrn2_fwd")

    kv0, kv1, wo0, wo1 = _gather_finish(gather_mix, o0, me, name="gather_mix_wait")
    W_kv, W_out = [kv0, kv1], [wo0, wo1]
    mem_n, kv = [], []
    for i in range(2):
        mn = _rms_fwd(mems, mem_norm[i:i + 1], name=f"mem_norm{i}")
        mem_n.append(mn)
        kv.append(_matmul(mn, W_kv[i], name=f"mem_kv{i}"))

    heads0 = _attn_fwd(p0, 4 * D_TOK // D_MEM, kv[0], heads0, name="attn_fwd0")
    x1, hf0 = _matmul(heads0, W_out[0], res=xs, norm_gain=ffn_norm[0:1], name="out_proj0")
    gu0, act0, (x2, h1) = ffn_fwd(x1, hf0, 0, mix_norm[1:2])

    W_gmT, = _gather_finish(gather_gm, h1, me, name="gather_gm_wait")
    p1 = _matmul(h1, W_gmT, tb=True, name="gm_in")
    heads1 = _gmlp_fwd(p1, ln_g, ln_b, ws, bsb, name="gmlp_fwd")
    heads1 = _attn_fwd(p1, 2 * D_TOK // D_MEM, kv[1], heads1, name="attn_fwd1")
    x3, hf1 = _matmul(heads1, W_out[1], res=x2, norm_gain=ffn_norm[1:2], name="out_proj1")
    gu1, act1, x4 = ffn_fwd(x3, hf1, 1, None)

    dx, g_final, loss_part = _final_loss(x4, final_norm.reshape(1, D_MODEL), tgt, name="final_loss")

    def ffn_bwd(dx, xin, hf, gu, act, i, dep):
        dgu = _ffn_out_dx(dx, W_fo[i], gu, dep, name=f"ffn_out_dx{i}")
        g_wfo = _matmul(act, dx, ta=True, out_dtype=BF16, name=f"ffn_out_dw{i}")
        g_wfi_t = _matmul(dgu, hf, ta=True, a_halves=True, out_dtype=BF16, name=f"ffn_in_dw{i}")
        dhf = _matmul(dgu, W_fiT[i], a_halves=True, name=f"ffn_in_dx{i}")
        dx, g_norm = _rms_bwd(xin, ffn_norm[i:i + 1], dhf, dx, name=f"ffn_norm_bwd{i}")
        return dx, g_wfi_t, g_wfo, g_norm

    def mem_bwd(dkv, i):
        g_wkv = _matmul(mem_n[i], dkv, ta=True, out_dtype=BF16, name=f"mem_kv_dw{i}")
        dmn = _matmul(dkv, W_kv[i], tb=True, name=f"mem_kv_dx{i}")
        _, g_norm = _rms_bwd(mems, mem_norm[i:i + 1], dmn, jnp.zeros_like(mems), name=f"mem_norm_bwd{i}")
        return g_wkv, g_norm

    dx, g_wfi1_t, g_wfo1, g_ffn1 = ffn_bwd(dx, x3, hf1, gu1, act1, 1, loss_part)
    dheads = _matmul(dx, W_out[1], tb=True, name="out_proj_dx1")
    g_wout1 = _matmul(heads1, dx, ta=True, out_dtype=BF16, name="out_proj_dw1")
    dp, g_ws, g_bs, g_lng, g_lnb = _gmlp_bwd(p1, ln_g, ln_b, ws, bsb, dheads, name="gmlp_bwd")
    dp, dk, dv = _attn_bwd(p1, 2 * D_TOK // D_MEM, kv[1], dheads, dp, name="attn_bwd1")
    g_wkv1, g_mem1 = mem_bwd(jnp.concatenate([dk, dv], axis=1), 1)
    g_wgm_t = _matmul(dp, h1, ta=True, out_dtype=BF16, name="gm_in_dw")
    dh = _matmul(dp, W_gmT, name="gm_in_dx")
    dx, g_mix1 = _rms_bwd(x2, mix_norm[1:2], dh, dx, name="mix_norm_bwd1")
    reduce_l1 = _reduce_start([g_wkv1, g_wout1, g_wgm_t, g_wfi1_t, g_wfo1], name="reduce_l1_start")

    dx, g_wfi0_t, g_wfo0, g_ffn0 = ffn_bwd(dx, x1, hf0, gu0, act0, 0, reduce_l1[4])
    reduce_ffn0 = _reduce_start([g_wfi0_t, g_wfo0], name="reduce_ffn0_start")
    dheads = _matmul(dx, W_out[0], tb=True, name="out_proj_dx0", dep=reduce_ffn0[4])
    g_wout0 = _matmul(heads0, dx, ta=True, out_dtype=BF16, name="out_proj_dw0")
    dp, g_lb0, g_onorm = _hgrn2_bwd(p0, lb0, hg_onorm, o0, states, dheads, name="hgrn2_bwd")
    dp, dk, dv = _attn_bwd(p0, 4 * D_TOK // D_MEM, kv[0], dheads, dp, name="attn_bwd0")
    g_wkv0, g_mem0 = mem_bwd(jnp.concatenate([dk, dv], axis=1), 0)
    g_whg_t = _matmul(dp, h0, ta=True, out_dtype=BF16, name="hg_in_dw")
    reduce_mix0 = _reduce_start([g_wkv0, g_wout0, g_whg_t], name="reduce_mix0_start")
    dh = _matmul(dp, W_hgT, name="hg_in_dx", dep=reduce_mix0[4])
    grad_x, g_mix0 = _rms_bwd(xs, mix_norm[0:1], dh, dx, name="mix_norm_bwd0")

    g_kv1, g_out1, g_gm_t, g_fi1_t, g_fo1 = _reduce_finish(reduce_l1, grad_x, me, name="reduce_l1")
    g_fi0_t, g_fo0 = _reduce_finish(reduce_ffn0, g_kv1, me, name="reduce_ffn0")
    g_kv0, g_out0, g_hg_t = _reduce_finish(reduce_mix0, g_fi0_t, me, name="reduce_mix0")
    g_shards = [jnp.stack([g_kv0, g_kv1]), jnp.stack([g_out0, g_out1]), g_hg_t.T[None], g_gm_t.T[None],
                jnp.stack([g_fi0_t.T, g_fi1_t.T]), jnp.stack([g_fo0, g_fo1])]

    small = [loss_part, jnp.concatenate([g_mix0, g_mix1], axis=1), jnp.concatenate([g_mem0, g_mem1], axis=1),
             g_lb0, g_onorm, g_lng, g_lnb, g_ws.reshape(1, -1), g_bs.reshape(1, -1),
             jnp.concatenate([g_ffn0, g_ffn1], axis=1), g_final]
    sizes = [t.shape[1] for t in small]
    small_rows = _pad_rows(jnp.concatenate(small, axis=1).reshape(-1, LANE), 8)
    red = _all_gather(small_rows, name="reduce_small", in_vmem=True, reduce_sum=True).reshape(-1)
    pieces, off = [], 0
    for n in sizes:
        pieces.append(red[off:off + n])
        off += n
    loss = pieces[0][0]
    g_mix_norm = pieces[1].reshape(2, D_MODEL)
    g_mem_norm = pieces[2].reshape(2, D_MODEL)
    g_hg_lb = pieces[3][None, :] * lb0 * (jnp.eye(3, dtype=F32)[:, 0:1] - lb_soft)
    g_hg_onorm = pieces[4].reshape(1, D_TOK)
    width = D_TOK // N_DEV
    g_gm_ln_g = lax.dynamic_slice(pieces[5], (me * width,), (width,)).reshape(1, width)
    g_gm_ln_b = lax.dynamic_slice(pieces[6], (me * width,), (width,)).reshape(1, width)
    g_gm_ws = pieces[7].reshape(gm_ws.shape)
    g_gm_bs = pieces[8].reshape(gm_bs.shape)
    g_ffn_norm = pieces[9].reshape(2, D_MODEL)
    g_final_norm = pieces[10]

    grads = [g_mix_norm, g_mem_norm, g_shards[0], g_shards[1], g_shards[2], g_hg_lb, g_hg_onorm, g_shards[3],
             g_gm_ln_g, g_gm_ln_b, g_gm_ws, g_gm_bs, g_ffn_norm, g_shards[4], g_shards[5], g_final_norm]
    weights = [mix_norm, mem_norm, w_mem_kv, w_out, hg_w_in, hg_lb, hg_onorm, gm_w_in, gm_ln_g, gm_ln_b, gm_ws, gm_bs,
               ffn_norm, w_ffn_in, w_ffn_out, final_norm]
    ms = [m_mix_norm, m_mem_norm, m_w_mem_kv, m_w_out, m_hg_w_in, m_hg_lb, m_hg_onorm, m_gm_w_in, m_gm_ln_g,
          m_gm_ln_b, m_gm_ws, m_gm_bs, m_ffn_norm, m_w_ffn_in, m_w_ffn_out, m_final_norm]
    vs = [v_mix_norm, v_mem_norm, v_w_mem_kv, v_w_out, v_hg_w_in, v_hg_lb, v_hg_onorm, v_gm_w_in, v_gm_ln_g,
          v_gm_ln_b, v_gm_ws, v_gm_bs, v_ffn_norm, v_w_ffn_in, v_w_ffn_out, v_final_norm]
    deltas, new_m, new_v = [], [], []
    for n, (w, g, m, v) in enumerate(zip(weights, grads, ms, vs)):
        if w.ndim == 1:
            d, nm, nv = _adamw(w[None], g.reshape(1, -1), m[None], v[None], name=f"adamw{n}")
            d, nm, nv = d[0], nm[0], nv[0]
        else:
            d, nm, nv = _adamw(w, g.reshape(w.shape), m, v, name=f"adamw{n}")
        deltas.append(d)
        new_m.append(nm)
        new_v.append(nv)
    grads = [g.reshape(w.shape) for g, w in zip(grads, weights)]
    return (loss, grad_x[None], *grads, *deltas, *new_m, *new_v)
```

```python
import functools

import jax
import jax.numpy as jnp
from jax import lax
from jax.experimental import pallas as pl
from jax.experimental.pallas import tpu as pltpu

F32 = jnp.float32
BF16 = jnp.bfloat16
MXU_DTYPE = jnp.bfloat16
MESH_ID = pl.DeviceIdType.MESH

N_DEV = 8
EPS = 1e-6
D_MODEL = 1024
D_TOK = 768
D_MEM = 256
N_HEADS = 6
HEAD = 128
MEM_HEADS = 4
MEM_HDIM = 64
GM_CHUNK = 128
D_FF = 2816
HG_SUB = 16
HG_IN = 4 * D_TOK + D_MEM
GM_IN = 2 * D_TOK + D_MEM
LANE = 128
MXU_COLS = 256

ADAM_LR = 0.001
ADAM_B1 = 0.9
ADAM_B2 = 0.999
ADAM_EPS = 1e-08
ADAM_WD = 0.01
ADAM_STEP = 10

VMEM_LIMIT = 48 * 2 ** 20


def _params(sem=None):
    return pltpu.CompilerParams(dimension_semantics=sem, vmem_limit_bytes=VMEM_LIMIT)


def _tile(n, cap, q=LANE):
    if n <= cap:
        return n
    best = None
    for t in range(q, cap + 1, q):
        if n % t == 0:
            best = t
    assert best is not None, (n, cap, q)
    return best


def _sigmoid(x):
    return 1.0 / (1.0 + jnp.exp(-x))


def _gelu(x):
    return 0.5 * x * (1.0 + lax.erf(x * 0.7071067811865476))


def _gelu_grad(x):
    return 0.5 * (1.0 + lax.erf(x * 0.7071067811865476)) + x * jnp.exp(-0.5 * x * x) * 0.3989422804014327


def _matmul(a, b, *, name, ta=False, tb=False, res=None, out_dtype=F32, a_halves=False, b_halves=False, dep=None,
            norm_gain=None, norm_bwd=None):
    if a_halves and ta:
        K, M = a.shape[1], 2 * a.shape[2]
    elif a_halves:
        M, K = a.shape[1], 2 * a.shape[2]
    else:
        K, M = a.shape if ta else a.shape[::-1]
    if b_halves:
        assert not tb and b.shape[1] == K
        N = 2 * b.shape[2]
    else:
        N = b.shape[0] if tb else b.shape[1]
        assert (b.shape[1] if tb else b.shape[0]) == K
    tm = _tile(M // 2 if (a_halves and ta) else M, 1664 if ta else (512 if norm_bwd is not None else 1024))
    tn = _tile(N // 2 if b_halves else N, 1792)
    tk = _tile(K // 2 if (a_halves and not ta) else K, 1024 if ta else 1664)
    nk = K // tk
    dims = (((0 if ta else 1,), (1 if tb else 0,)), ((), ()))

    n_in = 2 + (res is not None) + (norm_gain is not None) + 2 * (norm_bwd is not None) + (dep is not None)
    if norm_gain is not None or norm_bwd is not None:
        assert tn == N, "the fused norm needs whole rows"
        assert norm_gain is None or norm_bwd is None
        assert norm_bwd is None or res is not None

    def body(*refs):
        a_ref, b_ref = refs[:2]
        r_ref = refs[2] if res is not None else None
        g_ref = refs[2 + (res is not None)] if (norm_gain is not None or norm_bwd is not None) else None
        x_ref = refs[3 + (res is not None)] if norm_bwd is not None else None
        o_ref = refs[n_in]
        h_ref = refs[n_in + 1] if (norm_gain is not None or norm_bwd is not None) else None
        acc = None if nk == 1 else refs[-1]
        k = pl.program_id(2)

        def product():
            return lax.dot_general(a_ref[...].astype(MXU_DTYPE), b_ref[...].astype(MXU_DTYPE), dims,
                                   preferred_element_type=F32)

        def finish(r):
            if norm_bwd is not None:
                @pl.when(pl.program_id(0) == 0)
                def _():
                    h_ref[...] = jnp.zeros_like(h_ref)

                xv = x_ref[...]
                scale = lax.rsqrt(jnp.mean(xv * xv, axis=-1, keepdims=True) + EPS)
                xh = xv * scale
                h_ref[...] += jnp.sum(r * xh, axis=0, keepdims=True)
                u = r * g_ref[...]
                o_ref[...] = r_ref[...] + scale * (u - xh * jnp.mean(u * xh, axis=-1, keepdims=True))
                return
            if res is not None:
                r = r + r_ref[...].astype(F32)
            o_ref[...] = r.astype(out_dtype)
            if norm_gain is not None:
                scale = lax.rsqrt(jnp.mean(r * r, axis=-1, keepdims=True) + EPS)
                h_ref[...] = (r * scale * g_ref[...]).astype(h_ref.dtype)

        if nk == 1:
            finish(product())
            return

        @pl.when(k == 0)
        def _():
            acc[...] = product()

        @pl.when((k > 0) & (k < nk - 1))
        def _():
            acc[...] += product()

        @pl.when(k == nk - 1)
        def _():
            finish(acc[...] + product())

    if a_halves and ta:
        mh = M // 2 // tm
        a_spec = pl.BlockSpec((None, tk, tm), lambda i, j, k: (i // mh, k, i % mh))
    elif a_halves:
        kh = nk // 2
        a_spec = pl.BlockSpec((None, tm, tk), lambda i, j, k: (k // kh, i, k % kh))
    elif ta:
        a_spec = pl.BlockSpec((tk, tm), lambda i, j, k: (k, i))
    else:
        a_spec = pl.BlockSpec((tm, tk), lambda i, j, k: (i, k))
    if b_halves:
        nh = N // 2 // tn
        b_spec = pl.BlockSpec((None, tk, tn), lambda i, j, k: (j // nh, k, j % nh))
    elif tb:
        b_spec = pl.BlockSpec((tn, tk), lambda i, j, k: (j, k))
    else:
        b_spec = pl.BlockSpec((tk, tn), lambda i, j, k: (k, j))
    o_spec = pl.BlockSpec((tm, tn), lambda i, j, k: (i, j))
    in_specs = [a_spec, b_spec] + ([o_spec] if res is not None else [])
    args = (a, b) + ((res,) if res is not None else ())
    out_specs, out_shape = o_spec, jax.ShapeDtypeStruct((M, N), out_dtype)
    vec = pl.BlockSpec((1, N), lambda i, j, k: (0, 0))
    sem = ("parallel", "parallel", "arbitrary")
    if norm_gain is not None:
        in_specs.append(vec)
        args += (norm_gain,)
        out_specs, out_shape = [o_spec, o_spec], [out_shape, jax.ShapeDtypeStruct((M, N), BF16)]
    if norm_bwd is not None:
        x_in, gain = norm_bwd
        in_specs += [vec, o_spec]
        args += (gain, x_in)
        out_specs, out_shape = [o_spec, vec], [out_shape, jax.ShapeDtypeStruct((1, N), F32)]
        sem = ("arbitrary", "arbitrary", "arbitrary")
    if dep is not None:
        in_specs.append(pl.BlockSpec(memory_space=pl.ANY))
        args += (dep,)
    return pl.pallas_call(
        body, name=name, grid=(M // tm, N // tn, nk), in_specs=in_specs, out_specs=out_specs, out_shape=out_shape,
        scratch_shapes=[] if nk == 1 else [pltpu.VMEM((tm, tn), F32)], compiler_params=_params(sem))(*args)


def _ffn_in(hf, wt, *, name):
    S, K = hf.shape
    tm = _tile(S, 512)
    tn = _tile(D_FF, 1408)
    nh = D_FF // tn
    nt = (((1,), (1,)), ((), ()))

    def body(a_ref, bg_ref, bu_ref, gu_ref, act_ref):
        av = a_ref[...].astype(MXU_DTYPE)
        for c0 in range(0, tn, MXU_COLS):
            cs = slice(c0, min(c0 + MXU_COLS, tn))
            gate = lax.dot_general(av, bg_ref[cs, :].astype(MXU_DTYPE), nt, preferred_element_type=F32)
            up = lax.dot_general(av, bu_ref[cs, :].astype(MXU_DTYPE), nt, preferred_element_type=F32)
            gu_ref[0, :, cs] = gate.astype(gu_ref.dtype)
            gu_ref[1, :, cs] = up.astype(gu_ref.dtype)
            act_ref[:, cs] = (gate * _sigmoid(gate) * up).astype(act_ref.dtype)

    return pl.pallas_call(
        body, name=name, grid=(nh, S // tm),
        in_specs=[pl.BlockSpec((tm, K), lambda j, i: (i, 0)), pl.BlockSpec((tn, K), lambda j, i: (j, 0)),
                  pl.BlockSpec((tn, K), lambda j, i: (j + nh, 0))],
        out_specs=[pl.BlockSpec((2, tm, tn), lambda j, i: (0, i, j)), pl.BlockSpec((tm, tn), lambda j, i: (i, j))],
        out_shape=[jax.ShapeDtypeStruct((2, S, D_FF), BF16), jax.ShapeDtypeStruct((S, D_FF), BF16)],
        compiler_params=_params(("parallel", "parallel")))(hf, wt, wt)


def _ffn_out_dx(dx, w, gu, dep, *, name):
    S, K = dx.shape
    tm = _tile(S, 1024)
    tn = _tile(D_FF, 1408)

    def body(a_ref, b_ref, gu_ref, dep_ref, o_ref):
        del dep_ref
        av = a_ref[...].astype(MXU_DTYPE)
        for c0 in range(0, tn, MXU_COLS):
            cs = slice(c0, min(c0 + MXU_COLS, tn))
            da = lax.dot_general(av, b_ref[cs, :].astype(MXU_DTYPE), (((1,), (1,)), ((), ())),
                                 preferred_element_type=F32)
            gate = gu_ref[0, :, cs].astype(F32)
            up = gu_ref[1, :, cs].astype(F32)
            sg = _sigmoid(gate)
            o_ref[0, :, cs] = (da * up * sg * (1.0 + gate * (1.0 - sg))).astype(o_ref.dtype)
            o_ref[1, :, cs] = (da * gate * sg).astype(o_ref.dtype)

    halves = pl.BlockSpec((2, tm, tn), lambda i, j: (0, i, j))
    return pl.pallas_call(
        body, name=name, grid=(S // tm, D_FF // tn),
        in_specs=[pl.BlockSpec((tm, K), lambda i, j: (i, 0)), pl.BlockSpec((tn, K), lambda i, j: (j, 0)), halves,
                  pl.BlockSpec(memory_space=pl.ANY)],
        out_specs=halves, out_shape=jax.ShapeDtypeStruct((2, S, D_FF), BF16),
        compiler_params=_params(("parallel", "parallel")))(dx, w, gu, dep)


def _rms_fwd(x, g, *, name, dep=None):
    R, Dm = x.shape
    tr = _tile(R, 512, 8)

    def body(x_ref, g_ref, *rest):
        o_ref = rest[-1]
        xv = x_ref[...]
        r = lax.rsqrt(jnp.mean(xv * xv, axis=-1, keepdims=True) + EPS)
        o_ref[...] = (xv * r * g_ref[...]).astype(o_ref.dtype)

    in_specs = [pl.BlockSpec((tr, Dm), lambda i: (i, 0)), pl.BlockSpec((1, Dm), lambda i: (0, 0))]
    args = (x, g)
    if dep is not None:
        in_specs.append(pl.BlockSpec(memory_space=pl.ANY))
        args += (dep,)
    return pl.pallas_call(
        body, name=name, grid=(R // tr,), in_specs=in_specs,
        out_specs=pl.BlockSpec((tr, Dm), lambda i: (i, 0)), out_shape=jax.ShapeDtypeStruct((R, Dm), BF16),
        compiler_params=_params(("parallel",)))(*args)


def _rms_bwd(x, g, dh, dres, *, name):
    R, Dm = x.shape
    tr = _tile(R, 256, 8)

    def body(x_ref, g_ref, dh_ref, dres_ref, dx_ref, dg_ref):
        @pl.when(pl.program_id(0) == 0)
        def _():
            dg_ref[...] = jnp.zeros_like(dg_ref)

        xv = x_ref[...]
        r = lax.rsqrt(jnp.mean(xv * xv, axis=-1, keepdims=True) + EPS)
        xh = xv * r
        dhv = dh_ref[...].astype(F32)
        dg_ref[...] += jnp.sum(dhv * xh, axis=0, keepdims=True)
        u = dhv * g_ref[...]
        dx = r * (u - xh * jnp.mean(u * xh, axis=-1, keepdims=True))
        dx_ref[...] = dres_ref[...] + dx

    row = pl.BlockSpec((tr, Dm), lambda i: (i, 0))
    vec = pl.BlockSpec((1, Dm), lambda i: (0, 0))
    return pl.pallas_call(
        body, name=name, grid=(R // tr,), in_specs=[row, vec, row, row], out_specs=[row, vec],
        out_shape=[jax.ShapeDtypeStruct((R, Dm), F32), jax.ShapeDtypeStruct((1, Dm), F32)],
        compiler_params=_params(("arbitrary",)))(x, g, dh, dres)


def _final_loss(x, g, tgt, *, name):
    R, Dm = x.shape
    tr = _tile(R, 256, 8)

    def body(x_ref, g_ref, t_ref, dx_ref, dg_ref, loss_ref):
        @pl.when(pl.program_id(0) == 0)
        def _():
            dg_ref[...] = jnp.zeros_like(dg_ref)
            loss_ref[...] = jnp.zeros_like(loss_ref)

        xv = x_ref[...]
        r = lax.rsqrt(jnp.mean(xv * xv, axis=-1, keepdims=True) + EPS)
        xh = xv * r
        gv = g_ref[...]
        err = xh * gv - t_ref[...]
        part = 0.5 * jnp.sum(jnp.mean(err * err, axis=-1, keepdims=True), axis=0, keepdims=True)
        loss_ref[...] += jnp.broadcast_to(part, loss_ref.shape)
        dy = err * (1.0 / Dm)
        dg_ref[...] += jnp.sum(dy * xh, axis=0, keepdims=True)
        u = dy * gv
        dx_ref[...] = r * (u - xh * jnp.mean(u * xh, axis=-1, keepdims=True))

    row = pl.BlockSpec((tr, Dm), lambda i: (i, 0))
    vec = pl.BlockSpec((1, Dm), lambda i: (0, 0))
    one = pl.BlockSpec((1, LANE), lambda i: (0, 0))
    return pl.pallas_call(
        body, name=name, grid=(R // tr,), in_specs=[row, vec, row], out_specs=[row, vec, one],
        out_shape=[jax.ShapeDtypeStruct((R, Dm), F32), jax.ShapeDtypeStruct((1, Dm), F32),
                   jax.ShapeDtypeStruct((1, LANE), F32)],
        compiler_params=_params(("arbitrary",)))(x, g, tgt)


def _head_mask(h):
    lane = lax.broadcasted_iota(jnp.int32, (1, D_MEM), 1)
    return (lane >= h * MEM_HDIM) & (lane < (h + 1) * MEM_HDIM)


def _attn_probs(qv, k_mx, mask):
    s = lax.dot_general(jnp.where(mask, qv, 0.0).astype(MXU_DTYPE), k_mx, (((1,), (1,)), ((), ())),
                        preferred_element_type=F32) * (MEM_HDIM ** -0.5)
    e = jnp.exp(s - jnp.max(s, axis=-1, keepdims=True))
    return e / jnp.sum(e, axis=-1, keepdims=True)


def _attn_fwd(p, qcol, kv, heads, *, name):
    S = p.shape[0]
    M = kv.shape[0]
    ts = _tile(S, 512, 8)

    def body(q_ref, k_ref, v_ref, heads_in, o_ref):
        del heads_in
        qv = q_ref[...]
        kx = k_ref[...].astype(MXU_DTYPE)
        vv = v_ref[...]
        out = jnp.zeros((ts, D_MEM), F32)
        for h in range(MEM_HEADS):
            mask = _head_mask(h)
            pr = _attn_probs(qv, kx, mask)
            out = out + jnp.dot(pr.astype(MXU_DTYPE), jnp.where(mask, vv, 0.0).astype(MXU_DTYPE),
                                preferred_element_type=F32)
        o_ref[...] = out.astype(o_ref.dtype)

    return pl.pallas_call(
        body, name=name, grid=(S // ts,),
        in_specs=[pl.BlockSpec((ts, D_MEM), lambda i: (i, qcol)), pl.BlockSpec((M, D_MEM), lambda i: (0, 0)),
                  pl.BlockSpec((M, D_MEM), lambda i: (0, 1)), pl.BlockSpec(memory_space=pl.ANY)],
        out_specs=pl.BlockSpec((ts, D_MEM), lambda i: (i, D_TOK // D_MEM)),
        out_shape=jax.ShapeDtypeStruct(heads.shape, heads.dtype), input_output_aliases={3: 0},
        compiler_params=_params(("parallel",)))(p, kv, kv, heads)


def _attn_bwd(p, qcol, kv, dheads, dp, *, name):
    S = p.shape[0]
    M = kv.shape[0]
    ts = _tile(S, 512, 8)
    scale = MEM_HDIM ** -0.5

    def body(q_ref, k_ref, v_ref, do_ref, dp_in, dq_ref, dk_ref, dv_ref):
        del dp_in

        @pl.when(pl.program_id(0) == 0)
        def _():
            dk_ref[...] = jnp.zeros_like(dk_ref)
            dv_ref[...] = jnp.zeros_like(dv_ref)

        qv = q_ref[...]
        kv_ = k_ref[...]
        kx = kv_.astype(MXU_DTYPE)
        vv = v_ref[...]
        dox = do_ref[...].astype(MXU_DTYPE)
        qx = qv.astype(MXU_DTYPE)
        dq = jnp.zeros((ts, D_MEM), F32)
        for h in range(MEM_HEADS):
            mask = _head_mask(h)
            pr = _attn_probs(qv, kx, mask)
            vh = jnp.where(mask, vv, 0.0).astype(MXU_DTYPE)
            dpr = lax.dot_general(dox, vh, (((1,), (1,)), ((), ())), preferred_element_type=F32)
            ds = (pr * (dpr - jnp.sum(dpr * pr, axis=-1, keepdims=True)) * scale).astype(MXU_DTYPE)
            dq = dq + jnp.dot(ds, jnp.where(mask, kv_, 0.0).astype(MXU_DTYPE), preferred_element_type=F32)
            dk_h = lax.dot_general(ds, qx, (((0,), (0,)), ((), ())), preferred_element_type=F32)
            dv_h = lax.dot_general(pr.astype(MXU_DTYPE), dox, (((0,), (0,)), ((), ())), preferred_element_type=F32)
            dk_ref[...] += jnp.where(mask, dk_h, 0.0)
            dv_ref[...] += jnp.where(mask, dv_h, 0.0)
        dq_ref[...] = dq.astype(dq_ref.dtype)

    return pl.pallas_call(
        body, name=name, grid=(S // ts,),
        in_specs=[pl.BlockSpec((ts, D_MEM), lambda i: (i, qcol)), pl.BlockSpec((M, D_MEM), lambda i: (0, 0)),
                  pl.BlockSpec((M, D_MEM), lambda i: (0, 1)),
                  pl.BlockSpec((ts, D_MEM), lambda i: (i, D_TOK // D_MEM)), pl.BlockSpec(memory_space=pl.ANY)],
        out_specs=[pl.BlockSpec((ts, D_MEM), lambda i: (i, qcol)), pl.BlockSpec((M, D_MEM), lambda i: (0, 0)),
                   pl.BlockSpec((M, D_MEM), lambda i: (0, 0))],
        out_shape=[jax.ShapeDtypeStruct(dp.shape, dp.dtype), jax.ShapeDtypeStruct((M, D_MEM), F32),
                   jax.ShapeDtypeStruct((M, D_MEM), F32)],
        input_output_aliases={4: 0}, compiler_params=_params(("arbitrary",)))(p, kv, kv, dheads, dp)


def _gm_forward_parts(u_ref, v_ref, lng_ref, lnb_ref, w_ref, bsb_ref):
    zu = _gelu(u_ref[...])
    zv = _gelu(v_ref[...])
    mu = jnp.mean(zv, axis=-1, keepdims=True)
    cen = zv - mu
    rs = lax.rsqrt(jnp.mean(cen * cen, axis=-1, keepdims=True) + EPS)
    vh = cen * rs
    vn = vh * lng_ref[...] + lnb_ref[...]
    row = lax.broadcasted_iota(jnp.int32, (GM_CHUNK, GM_CHUNK), 0)
    col = lax.broadcasted_iota(jnp.int32, (GM_CHUNK, GM_CHUNK), 1)
    tril = row >= col
    wm = [jnp.where(tril, w_ref[g], 0.0).astype(MXU_DTYPE) for g in range(N_HEADS)]
    vnx = [vn[:, g * HEAD:(g + 1) * HEAD].astype(MXU_DTYPE) for g in range(N_HEADS)]
    sv = [jnp.dot(wm[g], vnx[g], preferred_element_type=F32) + bsb_ref[g] for g in range(N_HEADS)]
    return zu, vh, rs, wm, vnx, sv, tril


def _gmlp_fwd(p, lng, lnb, ws, bsb, *, name):
    S = p.shape[0]

    def body(u_ref, v_ref, lng_ref, lnb_ref, w_ref, bsb_ref, o_ref):
        zu, _, _, _, _, sv, _ = _gm_forward_parts(u_ref, v_ref, lng_ref, lnb_ref, w_ref, bsb_ref)
        for g in range(N_HEADS):
            o_ref[:, g * HEAD:(g + 1) * HEAD] = (zu[:, g * HEAD:(g + 1) * HEAD] * sv[g]).astype(o_ref.dtype)

    blk = lambda c: pl.BlockSpec((GM_CHUNK, D_TOK), lambda i: (i, c))
    vec = pl.BlockSpec((1, D_TOK), lambda i: (0, 0))
    cube = pl.BlockSpec((N_HEADS, GM_CHUNK, GM_CHUNK), lambda i: (0, 0, 0))
    return pl.pallas_call(
        body, name=name, grid=(S // GM_CHUNK,), in_specs=[blk(0), blk(1), vec, vec, cube, cube],
        out_specs=blk(0), out_shape=jax.ShapeDtypeStruct((S, D_MODEL), BF16),
        compiler_params=_params(("parallel",)))(p, p, lng, lnb, ws, bsb)


def _gmlp_bwd(p, lng, lnb, ws, bsb, dheads, *, name):
    S = p.shape[0]

    def body(u_ref, v_ref, lng_ref, lnb_ref, w_ref, bsb_ref, dt_ref, dp_ref, dw_ref, dbs_ref, dlg_ref, dlb_ref):
        @pl.when(pl.program_id(0) == 0)
        def _():
            dw_ref[...] = jnp.zeros_like(dw_ref)
            dbs_ref[...] = jnp.zeros_like(dbs_ref)
            dlg_ref[...] = jnp.zeros_like(dlg_ref)
            dlb_ref[...] = jnp.zeros_like(dlb_ref)

        zu, vh, rs, wm, vnx, sv, tril = _gm_forward_parts(u_ref, v_ref, lng_ref, lnb_ref, w_ref, bsb_ref)
        dt = dt_ref[...].astype(F32)
        dvn_parts = []
        for g in range(N_HEADS):
            sl = slice(g * HEAD, (g + 1) * HEAD)
            dsv = dt[:, sl] * zu[:, sl]
            dp_ref[:, sl] = (dt[:, sl] * sv[g] * _gelu_grad(u_ref[:, sl])).astype(dp_ref.dtype)
            dsx = dsv.astype(MXU_DTYPE)
            dw = lax.dot_general(dsx, vnx[g], (((1,), (1,)), ((), ())), preferred_element_type=F32)
            dw_ref[g] += jnp.where(tril, dw, 0.0)
            dbs_ref[g] += jnp.sum(dsv, axis=-1, keepdims=True)
            dvn_parts.append(lax.dot_general(wm[g], dsx, (((0,), (0,)), ((), ())), preferred_element_type=F32))
        dvn = jnp.concatenate(dvn_parts, axis=-1)
        dlg_ref[...] += jnp.sum(dvn * vh, axis=0, keepdims=True)
        dlb_ref[...] += jnp.sum(dvn, axis=0, keepdims=True)
        dvh = dvn * lng_ref[...]
        dzv = rs * (dvh - jnp.mean(dvh, axis=-1, keepdims=True) - vh * jnp.mean(dvh * vh, axis=-1, keepdims=True))
        dp_ref[:, D_TOK:] = (dzv * _gelu_grad(v_ref[...])).astype(dp_ref.dtype)

    blk = lambda c: pl.BlockSpec((GM_CHUNK, D_TOK), lambda i: (i, c))
    vec = pl.BlockSpec((1, D_TOK), lambda i: (0, 0))
    cube = pl.BlockSpec((N_HEADS, GM_CHUNK, GM_CHUNK), lambda i: (0, 0, 0))
    col = pl.BlockSpec((N_HEADS, GM_CHUNK, 1), lambda i: (0, 0, 0))
    return pl.pallas_call(
        body, name=name, grid=(S // GM_CHUNK,), in_specs=[blk(0), blk(1), vec, vec, cube, cube, blk(0)],
        out_specs=[pl.BlockSpec((GM_CHUNK, 2 * D_TOK), lambda i: (i, 0)), cube, col, vec, vec],
        out_shape=[jax.ShapeDtypeStruct((S, GM_IN), BF16), jax.ShapeDtypeStruct((N_HEADS, GM_CHUNK, GM_CHUNK), F32),
                   jax.ShapeDtypeStruct((N_HEADS, GM_CHUNK, 1), F32), jax.ShapeDtypeStruct((1, D_TOK), F32),
                   jax.ShapeDtypeStruct((1, D_TOK), F32)],
        compiler_params=_params(("arbitrary",)))(p, p, lng, lnb, ws, bsb, dheads)


def _chunk_tri(n, chunk, upper):
    r = lax.broadcasted_iota(jnp.int32, (n, n), 0)
    c = lax.broadcasted_iota(jnp.int32, (n, n), 1)
    same = (r // chunk) == (c // chunk)
    return jnp.where(same & ((r <= c) if upper else (r >= c)), 1.0, 0.0).astype(F32)


def _hg_gates(fz, lb):
    sg = _sigmoid(fz)
    f = lb + (1.0 - lb) * sg
    kk = (1.0 - lb) * (1.0 - sg)
    return sg, f, jnp.log(f), kk


def _hgrn2_fwd(p, lb, onorm, *, name):
    S = p.shape[0]
    C = HG_SUB
    tb = _tile(S, 256, C)
    nsub = tb // C

    def body(q_ref, fz_ref, v_ref, g_ref, lb_ref, on_ref, tok_ref, o_ref, st_ref, state, b_blk, k_blk, bsc, ksc, vsc):
        @pl.when(pl.program_id(0) == 0)
        def _():
            state[...] = jnp.zeros_like(state)

        _, _, lg, kk = _hg_gates(fz_ref[...], lb_ref[...])
        b_blk[...] = jnp.dot(_chunk_tri(tb, C, False), lg, precision=lax.Precision.HIGHEST,
                             preferred_element_type=F32)
        k_blk[...] = kk
        tt = lax.broadcasted_iota(jnp.int32, (C, HEAD), 0)

        def sub(c, carry):
            rows = pl.ds(pl.multiple_of(c * C, C), C)
            for h in range(N_HEADS):
                cols = slice(h * HEAD, (h + 1) * HEAD)
                qv = q_ref[rows, cols]
                vv = v_ref[rows, cols]
                b = b_blk[rows, cols]
                kk = k_blk[rows, cols]
                st0 = state[h]
                st_ref[c, h] = st0
                inter = lax.dot_general((qv * jnp.exp(b)).astype(MXU_DTYPE), st0.astype(MXU_DTYPE),
                                        (((1,), (1,)), ((), ())), preferred_element_type=F32)
                bsc[h] = b
                ksc[h] = kk
                vsc[h] = vv
                intra = jnp.zeros((C, HEAD), F32)
                for s in range(C):
                    dec = jnp.where(tt >= s, jnp.exp(b - bsc[h, pl.ds(s, 1), :]), 0.0)
                    a_s = jnp.sum(qv * ksc[h, pl.ds(s, 1), :] * dec, axis=-1, keepdims=True)
                    intra = intra + a_s * vsc[h, pl.ds(s, 1), :]
                o_ref[rows, cols] = inter + intra
                b_last = bsc[h, pl.ds(C - 1, 1), :]
                ke = kk * jnp.exp(b_last - b)
                state[h] = st0 * jnp.exp(b_last) + lax.dot_general(
                    vv.astype(MXU_DTYPE), ke.astype(MXU_DTYPE), (((0,), (0,)), ((), ())),
                    preferred_element_type=F32)
            return carry

        lax.fori_loop(0, nsub, sub, 0)

        for h in range(N_HEADS):
            cols = slice(h * HEAD, (h + 1) * HEAD)
            o = o_ref[:, cols]
            gv = g_ref[:, cols]
            n = o * lax.rsqrt(jnp.mean(o * o, axis=-1, keepdims=True) + EPS)
            tok_ref[:, cols] = (n * (gv * _sigmoid(gv)) * on_ref[:, cols]).astype(tok_ref.dtype)

    blk = lambda c: pl.BlockSpec((tb, D_TOK), lambda i, c=c: (i, c))
    vec = pl.BlockSpec((1, D_TOK), lambda i: (0, 0))
    stb = pl.BlockSpec((nsub, N_HEADS, HEAD, HEAD), lambda i: (i, 0, 0, 0))
    return pl.pallas_call(
        body, name=name, grid=(S // tb,), in_specs=[blk(0), blk(1), blk(2), blk(3), vec, vec],
        out_specs=[blk(0), blk(0), stb],
        out_shape=[jax.ShapeDtypeStruct((S, D_MODEL), BF16), jax.ShapeDtypeStruct((S, D_TOK), F32),
                   jax.ShapeDtypeStruct((S // C, N_HEADS, HEAD, HEAD), F32)],
        scratch_shapes=[pltpu.VMEM((N_HEADS, HEAD, HEAD), F32)] + [pltpu.VMEM((tb, D_TOK), F32)] * 2
        + [pltpu.VMEM((N_HEADS, C, HEAD), F32)] * 3,
        compiler_params=_params(("arbitrary",)))(p, p, p, p, lb, onorm)


def _hgrn2_bwd(p, lb, onorm, o, states, dheads, *, name):
    S = p.shape[0]
    C = HG_SUB
    tb = _tile(S, 256, C)
    nsub = tb // C
    nblk = S // tb

    def body(q_ref, fz_ref, v_ref, g_ref, lb_ref, on_ref, o_ref, st_ref, dt_ref, dp_ref, dlb_ref, don_ref, dstate,
             b_blk, k_blk, do_blk, db_blk, dk_blk, bsc, ksc, vsc, qsc, dosc):
        @pl.when(pl.program_id(0) == 0)
        def _():
            dstate[...] = jnp.zeros_like(dstate)
            dlb_ref[...] = jnp.zeros_like(dlb_ref)
            don_ref[...] = jnp.zeros_like(don_ref)

        for h in range(N_HEADS):
            cols = slice(h * HEAD, (h + 1) * HEAD)
            onv = on_ref[:, cols]
            gv = g_ref[:, cols]
            ov = o_ref[:, cols]
            dt = dt_ref[:, cols].astype(F32)
            sgg = _sigmoid(gv)
            sil = gv * sgg
            rinv = lax.rsqrt(jnp.mean(ov * ov, axis=-1, keepdims=True) + EPS)
            n = ov * rinv
            don_ref[:, cols] += jnp.sum(dt * n * sil, axis=0, keepdims=True)
            dn = dt * sil * onv
            dp_ref[:, 3 * D_TOK + h * HEAD:3 * D_TOK + (h + 1) * HEAD] = (
                dt * n * onv * sgg * (1.0 + gv * (1.0 - sgg))).astype(dp_ref.dtype)
            do_blk[:, cols] = rinv * (dn - n * jnp.mean(dn * n, axis=-1, keepdims=True))
        _, _, lg, kk = _hg_gates(fz_ref[...], lb_ref[...])
        b_blk[...] = jnp.dot(_chunk_tri(tb, C, False), lg, precision=lax.Precision.HIGHEST,
                             preferred_element_type=F32)
        k_blk[...] = kk
        tt = lax.broadcasted_iota(jnp.int32, (C, HEAD), 0)

        def sub(j, carry):
            c = nsub - 1 - j
            rows = pl.ds(pl.multiple_of(c * C, C), C)
            for h in range(N_HEADS):
                cols = slice(h * HEAD, (h + 1) * HEAD)
                qv = q_ref[rows, cols]
                vv = v_ref[rows, cols]
                do = do_blk[rows, cols]
                b = b_blk[rows, cols]
                kk = k_blk[rows, cols]
                bsc[h] = b
                ksc[h] = kk
                vsc[h] = vv
                qsc[h] = qv
                dosc[h] = do
                b_last = bsc[h, pl.ds(C - 1, 1), :]
                eb = jnp.exp(b)
                qe = qv * eb
                ebb = jnp.exp(b_last - b)
                ke = kk * ebb
                e_last = jnp.exp(b_last)
                st0 = st_ref[c, h]
                dst1 = dstate[h]
                st0x = st0.astype(MXU_DTYPE)
                dst1x = dst1.astype(MXU_DTYPE)
                dox = do.astype(MXU_DTYPE)
                dqe = jnp.dot(dox, st0x, preferred_element_type=F32)
                dke = jnp.dot(vv.astype(MXU_DTYPE), dst1x, preferred_element_type=F32)
                dv = lax.dot_general(ke.astype(MXU_DTYPE), dst1x, (((1,), (1,)), ((), ())),
                                     preferred_element_type=F32)
                db_last = (e_last * jnp.sum(st0 * dst1, axis=0, keepdims=True)
                           + jnp.sum(dke * ke, axis=0, keepdims=True))
                dstate[h] = dst1 * e_last + lax.dot_general(dox, qe.astype(MXU_DTYPE), (((0,), (0,)), ((), ())),
                                                            preferred_element_type=F32)
                dq = dqe * eb
                db = dqe * qe - dke * ke
                dkk = dke * ebb
                for s in range(C):
                    dec = jnp.where(tt >= s, jnp.exp(b - bsc[h, pl.ds(s, 1), :]), 0.0)
                    da_s = jnp.sum(do * vsc[h, pl.ds(s, 1), :], axis=-1, keepdims=True)
                    pq = da_s * ksc[h, pl.ds(s, 1), :] * dec
                    dq = dq + pq
                    db = db + pq * qv
                for t in range(C):
                    q_t = qsc[h, pl.ds(t, 1), :]
                    do_t = dosc[h, pl.ds(t, 1), :]
                    dec = jnp.where(tt <= t, jnp.exp(bsc[h, pl.ds(t, 1), :] - b), 0.0)
                    da_t = jnp.sum(vv * do_t, axis=-1, keepdims=True)
                    pk = da_t * q_t * dec
                    dkk = dkk + pk
                    db = db - pk * kk
                    a_t = jnp.sum(q_t * kk * dec, axis=-1, keepdims=True)
                    dv = dv + a_t * do_t
                db_blk[rows, cols] = db + jnp.where(tt == C - 1, db_last, 0.0)
                dk_blk[rows, cols] = dkk
                dp_ref[rows, cols] = dq.astype(dp_ref.dtype)
                dp_ref[rows, 2 * D_TOK + h * HEAD:2 * D_TOK + (h + 1) * HEAD] = dv.astype(dp_ref.dtype)
            return carry

        lax.fori_loop(0, nsub, sub, 0)

        dlg = jnp.dot(_chunk_tri(tb, C, True), db_blk[...], precision=lax.Precision.HIGHEST,
                      preferred_element_type=F32)
        lbv = lb_ref[...]
        sg, f, _, _ = _hg_gates(fz_ref[...], lbv)
        w = dlg / f - dk_blk[...]
        dp_ref[:, D_TOK:2 * D_TOK] = (w * (1.0 - lbv) * sg * (1.0 - sg)).astype(dp_ref.dtype)
        dlb_ref[...] += jnp.sum(w * (1.0 - sg), axis=0, keepdims=True)

    blk = lambda c: pl.BlockSpec((tb, D_TOK), lambda i, c=c: (nblk - 1 - i, c))
    vec = pl.BlockSpec((1, D_TOK), lambda i: (0, 0))
    stb = pl.BlockSpec((nsub, N_HEADS, HEAD, HEAD), lambda i: (nblk - 1 - i, 0, 0, 0))
    small = jax.ShapeDtypeStruct((1, D_TOK), F32)
    return pl.pallas_call(
        body, name=name, grid=(nblk,), in_specs=[blk(0), blk(1), blk(2), blk(3), vec, vec, blk(0), stb, blk(0)],
        out_specs=[pl.BlockSpec((tb, 4 * D_TOK), lambda i: (nblk - 1 - i, 0)), vec, vec],
        out_shape=[jax.ShapeDtypeStruct((S, HG_IN), BF16), small, small],
        scratch_shapes=[pltpu.VMEM((N_HEADS, HEAD, HEAD), F32)] + [pltpu.VMEM((tb, D_TOK), F32)] * 5
        + [pltpu.VMEM((N_HEADS, C, HEAD), F32)] * 5,
        compiler_params=_params(("arbitrary",)))(p, p, p, p, lb, onorm, o, states, dheads)


def _adamw(w, g, m, v, *, name):
    shape = w.shape
    cols = shape[-1]
    w2, g2, m2, v2 = (t.reshape(-1, cols) for t in (w, g, m, v))
    R = w2.shape[0]
    tr = _tile(R, 512, 8)

    def body(w_ref, g_ref, m_ref, v_ref, d_ref, nm_ref, nv_ref):
        gv = g_ref[...]
        nm = ADAM_B1 * m_ref[...] + (1.0 - ADAM_B1) * gv
        nv = ADAM_B2 * v_ref[...] + (1.0 - ADAM_B2) * (gv * gv)
        m_hat = nm / (1.0 - ADAM_B1 ** ADAM_STEP)
        v_hat = nv / (1.0 - ADAM_B2 ** ADAM_STEP)
        d_ref[...] = -ADAM_LR * (m_hat / (jnp.sqrt(v_hat) + ADAM_EPS) + ADAM_WD * w_ref[...])
        nm_ref[...] = nm
        nv_ref[...] = nv

    spec = pl.BlockSpec((tr, cols), lambda i: (i, 0))
    out = jax.ShapeDtypeStruct((R, cols), F32)
    d, nm, nv = pl.pallas_call(body, name=name, grid=(R // tr,), in_specs=[spec] * 4, out_specs=[spec] * 3,
                               out_shape=[out] * 3, compiler_params=_params(("parallel",)))(w2, g2, m2, v2)
    return d.reshape(shape), nm.reshape(shape), nv.reshape(shape)


def _add_received(own, got, *, name):
    R, Cc = own.shape
    n = got.shape[0]
    tr = _tile(R, 256, 16)

    def body(a_ref, b_ref, o_ref):
        acc = a_ref[...].astype(F32)
        for k in range(n):
            acc = acc + b_ref[k].astype(F32)
        o_ref[...] = acc

    return pl.pallas_call(
        body, name=name, grid=(R // tr,),
        in_specs=[pl.BlockSpec((tr, Cc), lambda i: (i, 0)), pl.BlockSpec((n, tr, Cc), lambda i: (0, i, 0))],
        out_specs=pl.BlockSpec((tr, Cc), lambda i: (i, 0)), out_shape=jax.ShapeDtypeStruct((R, Cc), F32),
        compiler_params=_params(("parallel",)))(own, got)


def _place():
    return lax.axis_index("x"), lax.axis_index("y"), lax.axis_index("c")


def _all_gather(x, *, name, in_vmem, reduce_sum=False, with_token=False):
    R, Cc = x.shape
    space = pltpu.VMEM if in_vmem else pl.ANY

    def body(x_ref, out_ref, *scratch):
        if with_token:
            scratch[0][...] = jnp.zeros_like(scratch[0])
            scratch = scratch[1:]
        if reduce_sum:
            gat_ref, send_sems, recv_sems, local_sem = scratch
        else:
            gat_ref = out_ref
            send_sems, recv_sems, local_sem = scratch
        mx, my, mc = _place()
        me, sibling = (mx, my, mc), (mx, my, 1 - mc)
        chips = [(1 - mx, my), (mx, 1 - my), (1 - mx, 1 - my)]

        def rows(px, py, pc):
            return gat_ref.at[pl.ds((4 * px + 2 * py + pc) * R, R), :]

        def copy(k, block, to, src=None):
            return pltpu.make_async_remote_copy(
                src_ref=rows(*block) if src is None else src, dst_ref=rows(*block), send_sem=send_sems.at[k],
                recv_sem=recv_sems.at[k], device_id=to, device_id_type=MESH_ID)

        mine = pltpu.make_async_copy(x_ref, rows(*me), local_sem)
        mine.start()
        first = [copy(0, me, sibling, src=x_ref)]
        first += [copy(1 + j, me, (*chip, mc), src=x_ref) for j, chip in enumerate(chips)]
        for cp in first:
            cp.start()
        passed = [copy(4 + j, (*chip, mc), sibling) for j, chip in enumerate(chips)]
        for j, chip in enumerate(chips):
            copy(1 + j, (*chip, mc), me).wait_recv()
            passed[j].start()
        copy(0, sibling, me).wait_recv()
        for j, chip in enumerate(chips):
            copy(4 + j, (*chip, 1 - mc), me).wait_recv()
        for cp in first + passed:
            cp.wait_send()
        mine.wait()
        if reduce_sum:
            acc = gat_ref[pl.ds(0, R), :]
            for d in range(1, N_DEV):
                acc = acc + gat_ref[pl.ds(d * R, R), :]
            out_ref[...] = acc

    sems = [pltpu.SemaphoreType.DMA((7,)), pltpu.SemaphoreType.DMA((7,)), pltpu.SemaphoreType.DMA]
    if reduce_sum:
        assert in_vmem
        out_shape = jax.ShapeDtypeStruct((R, Cc), x.dtype)
        scratch = [pltpu.VMEM((N_DEV * R, Cc), x.dtype)] + sems
    else:
        out_shape = jax.ShapeDtypeStruct((N_DEV * R, Cc), x.dtype)
        scratch = sems
    out_specs = pl.BlockSpec(memory_space=space)
    if with_token:
        out_shape = (out_shape, jax.ShapeDtypeStruct((8, LANE), F32))
        out_specs = (out_specs, pl.BlockSpec(memory_space=pltpu.VMEM))
    return pl.pallas_call(
        body, name=name, out_shape=out_shape, in_specs=[pl.BlockSpec(memory_space=space)], out_specs=out_specs,
        scratch_shapes=scratch, compiler_params=pltpu.CompilerParams(vmem_limit_bytes=VMEM_LIMIT))(x)


def _peer(k, mx, my, mc):
    bits = k + 1
    return (1 - mx if bits & 4 else mx, 1 - my if bits & 2 else my, 1 - mc if bits & 1 else mc)


HBM_SPEC = pl.BlockSpec(memory_space=pltpu.HBM)
SEM_SPEC = pl.BlockSpec(memory_space=pltpu.SEMAPHORE)
DATAFLOW = pltpu.SideEffectType.DATAFLOW_SIDE_EFFECTING


def _exchange_copies(x_refs, land_refs, send_sems, recv_sems, scatter):
    mx, my, mc = _place()
    me = 4 * mx + 2 * my + mc
    n = len(x_refs)
    copies = []
    for k in range(N_DEV - 1):
        px, py, pc = _peer(k, mx, my, mc)
        for m, (x_ref, land_ref) in enumerate(zip(x_refs, land_refs)):
            rows = land_ref.shape[1] if scatter else x_ref.shape[0]
            if scatter:
                src = x_ref.at[pl.ds(pl.multiple_of((4 * px + 2 * py + pc) * rows, 16), rows), :]
                dst = land_ref.at[k]
            else:
                src = x_ref
                dst = land_ref.at[pl.ds(pl.multiple_of(me * rows, 16), rows), :]
            copies.append(pltpu.make_async_remote_copy(
                src_ref=src, dst_ref=dst, send_sem=send_sems.at[k * n + m], recv_sem=recv_sems.at[k * n + m],
                device_id=(px, py, pc), device_id_type=MESH_ID))
    return copies


def _land_shape(x, scatter):
    return (N_DEV - 1, x.shape[0] // N_DEV, x.shape[1]) if scatter else (N_DEV * x.shape[0], x.shape[1])


def _exchange_start(xs, *, name, scatter):
    n = len(xs)
    lands = [lax.empty(_land_shape(x, scatter), x.dtype) for x in xs]

    def body(*refs):
        x_refs, land_refs = refs[:n], refs[n:2 * n]
        send_sems, recv_sems = refs[2 * n:2 * n + 2]
        token = refs[-1]
        for cp in _exchange_copies(x_refs, land_refs, send_sems, recv_sems, scatter):
            cp.start()
        token[...] = jnp.zeros_like(token)

    sems = pltpu.SemaphoreType.DMA(((N_DEV - 1) * n,))
    out = pl.pallas_call(
        body, name=name,
        out_shape=(sems, sems, *[pltpu.HBM(x.shape, x.dtype) for x in xs],
                   *[pltpu.HBM(l.shape, l.dtype) for l in lands], jax.ShapeDtypeStruct((8, LANE), F32)),
        in_specs=(HBM_SPEC,) * (2 * n),
        out_specs=(SEM_SPEC, SEM_SPEC) + (HBM_SPEC,) * (2 * n) + (pl.BlockSpec(memory_space=pltpu.VMEM),),
        input_output_aliases={i: 2 + i for i in range(2 * n)},
        compiler_params=pltpu.CompilerParams(has_side_effects=DATAFLOW))(
            *[pltpu.with_memory_space_constraint(t, pltpu.HBM) for t in list(xs) + lands])
    return out[0], out[1], list(out[2:2 + n]), list(out[2 + n:2 + 2 * n]), out[-1]


def _exchange_wait(started, after, *, name, scatter):
    send_sems, recv_sems, xs, lands, _ = started
    n = len(xs)

    def body(*refs):
        x_refs, land_refs = refs[:n], refs[n:2 * n]
        send_sems, recv_sems = refs[2 * n:2 * n + 2]
        for cp in _exchange_copies(x_refs, land_refs, send_sems, recv_sems, scatter):
            cp.wait_send()
            cp.wait_recv()

    out = pl.pallas_call(
        body, name=name, out_shape=tuple(pltpu.HBM(t.shape, t.dtype) for t in xs + lands),
        in_specs=(HBM_SPEC,) * (2 * n) + (SEM_SPEC, SEM_SPEC, pl.BlockSpec(memory_space=pl.ANY)),
        out_specs=(HBM_SPEC,) * (2 * n), input_output_aliases={i: i for i in range(2 * n)},
        compiler_params=pltpu.CompilerParams(has_side_effects=DATAFLOW))(*xs, *lands, send_sems, recv_sems, after)
    return list(out[:n]), list(out[n:])


def _gather_start(shards, token, *, name):
    shards = [shards[0] + token[0, 0].astype(shards[0].dtype)] + list(shards[1:])
    return _exchange_start(shards, name=name, scatter=False)


def _gather_finish(started, after, me, *, name):
    xs, lands = _exchange_wait(started, after, name=name, scatter=False)
    return [lax.dynamic_update_slice(land, x, (me * x.shape[0], 0)) for land, x in zip(lands, xs)]


def _reduce_start(grads, *, name):
    return _exchange_start(grads, name=name, scatter=True)


def _reduce_finish(started, after, me, *, name):
    sent, gots = _exchange_wait(started, after, name=name + "_wait", scatter=True)
    out = []
    for m, (g, got) in enumerate(zip(sent, gots)):
        rows = g.shape[0] // N_DEV
        own = lax.dynamic_slice(g, (me * rows, 0), (rows, g.shape[1]))
        out.append(_add_received(own, got, name=f"{name}_add{m}"))
    return out


def _pad_rows(a, mult):
    r = (-a.shape[0]) % mult
    return a if r == 0 else jnp.concatenate([a, jnp.zeros((r,) + a.shape[1:], a.dtype)], axis=0)


def kernel(x, mem, mix_norm, mem_norm, w_mem_kv, w_out, hg_w_in, hg_lb, hg_onorm, gm_w_in, gm_ln_g, gm_ln_b, gm_ws, gm_bs, ffn_norm, w_ffn_in, w_ffn_out, final_norm, loss_target, m_mix_norm, m_mem_norm, m_w_mem_kv, m_w_out, m_hg_w_in, m_hg_lb, m_hg_onorm, m_gm_w_in, m_gm_ln_g, m_gm_ln_b, m_gm_ws, m_gm_bs, m_ffn_norm, m_w_ffn_in, m_w_ffn_out, m_final_norm, v_mix_norm, v_mem_norm, v_w_mem_kv, v_w_out, v_hg_w_in, v_hg_lb, v_hg_onorm, v_gm_w_in, v_gm_ln_g, v_gm_ln_b, v_gm_ws, v_gm_bs, v_ffn_norm, v_w_ffn_in, v_w_ffn_out, v_final_norm):
    mx, my, mc = _place()
    me = 4 * mx + 2 * my + mc
    xs = x[0]
    mems = mem[0]
    tgt = loss_target[0]

    hg_t = hg_w_in[0].T.astype(BF16)
    gm_t = gm_w_in[0].T.astype(BF16)
    fi_t = [w_ffn_in[i].T.astype(BF16) for i in range(2)]
    kv_b = [w_mem_kv[i].astype(BF16) for i in range(2)]
    out_b = [w_out[i].astype(BF16) for i in range(2)]
    fo_b = [w_ffn_out[i].astype(BF16) for i in range(2)]
    ln_local = _pad_rows(jnp.concatenate([gm_ln_g, gm_ln_b], axis=0), 8)
    ln_local = jnp.concatenate([ln_local, jnp.zeros((8, LANE - ln_local.shape[1]), F32)], axis=1)
    ln_all, token = _all_gather(ln_local, name="gather_ln", in_vmem=True, with_token=True)
    ln_all = ln_all.reshape(N_DEV, 8, LANE)
    ln_g = ln_all[:, 0, :D_TOK // N_DEV].reshape(1, D_TOK)
    ln_b = ln_all[:, 1, :D_TOK // N_DEV].reshape(1, D_TOK)
    W_hgT, token = _all_gather(hg_t + token[0, 0].astype(BF16), name="gather_first", in_vmem=False,
                               with_token=True)
    gather_mix = _gather_start(kv_b + out_b, token, name="gather_mix_start")
    gather_fi = [_gather_start([fi_t[0]], gather_mix[4], name="gather_fi0_start")]
    gather_fo = [_gather_start([fo_b[0]], gather_fi[0][4], name="gather_fo0_start")]
    gather_gm = _gather_start([gm_t], gather_fo[0][4], name="gather_gm_start")
    gather_fi.append(_gather_start([fi_t[1]], gather_gm[4], name="gather_fi1_start"))
    gather_fo.append(_gather_start([fo_b[1]], gather_fi[1][4], name="gather_fo1_start"))

    lb_soft = jax.nn.softmax(hg_lb, axis=0)
    lb0 = lb_soft[0:1]
    bsb = jnp.broadcast_to(gm_bs[0][:, :, None], (N_HEADS, GM_CHUNK, GM_CHUNK))
    ws = gm_ws[0]

    W_fiT, W_fo = [], []

    def ffn_fwd(xin, hf, i, next_gain):
        W_fiT.extend(_gather_finish(gather_fi[i], hf, me, name=f"gather_fi{i}_wait"))
        gu, act = _ffn_in(hf, W_fiT[i], name=f"ffn_in{i}")
        W_fo.extend(_gather_finish(gather_fo[i], act, me, name=f"gather_fo{i}_wait"))
        return gu, act, _matmul(act, W_fo[i], res=xin, norm_gain=next_gain, name=f"ffn_out{i}")

    h0 = _rms_fwd(xs, mix_norm[0:1], name="mix_norm0", dep=gather_fo[1][4])
    p0 = _matmul(h0, W_hgT, tb=True, name="hg_in")
    heads0, o0, states = _hgrn2_fwd(p0, lb0, hg_onorm, name="hgrn2_fwd")

    kv0, kv1, wo0, wo1 = _gather_finish(gather_mix, o0, me, name="gather_mix_wait")
    W_kv, W_out = [kv0, kv1], [wo0, wo1]
    mem_n, kv = [], []
    for i in range(2):
        mn = _rms_fwd(mems, mem_norm[i:i + 1], name=f"mem_norm{i}")
        mem_n.append(mn)
        kv.append(_matmul(mn, W_kv[i], name=f"mem_kv{i}"))

    heads0 = _attn_fwd(p0, 4 * D_TOK // D_MEM, kv[0], heads0, name="attn_fwd0")
    x1, hf0 = _matmul(heads0, W_out[0], res=xs, norm_gain=ffn_norm[0:1], name="out_proj0")
    gu0, act0, (x2, h1) = ffn_fwd(x1, hf0, 0, mix_norm[1:2])

    W_gmT, = _gather_finish(gather_gm, h1, me, name="gather_gm_wait")
    p1 = _matmul(h1, W_gmT, tb=True, name="gm_in")
    heads1 = _gmlp_fwd(p1, ln_g, ln_b, ws, bsb, name="gmlp_fwd")
    heads1 = _attn_fwd(p1, 2 * D_TOK // D_MEM, kv[1], heads1, name="attn_fwd1")
    x3, hf1 = _matmul(heads1, W_out[1], res=x2, norm_gain=ffn_norm[1:2], name="out_proj1")
    gu1, act1, x4 = ffn_fwd(x3, hf1, 1, None)

    dx, g_final, loss_part = _final_loss(x4, final_norm.reshape(1, D_MODEL), tgt, name="final_loss")

    def ffn_bwd(dx, xin, hf, gu, act, i, dep):
        dgu = _ffn_out_dx(dx, W_fo[i], gu, dep, name=f"ffn_out_dx{i}")
        g_wfo = _matmul(act, dx, ta=True, out_dtype=BF16, name=f"ffn_out_dw{i}")
        g_wfi_t = _matmul(dgu, hf, ta=True, a_halves=True, out_dtype=BF16, name=f"ffn_in_dw{i}")
        dx, g_norm = _matmul(dgu, W_fiT[i], a_halves=True, res=dx, norm_bwd=(xin, ffn_norm[i:i + 1]),
                             name=f"ffn_in_dx{i}")
        return dx, g_wfi_t, g_wfo, g_norm

    def mem_bwd(dkv, i):
        g_wkv = _matmul(mem_n[i], dkv, ta=True, out_dtype=BF16, name=f"mem_kv_dw{i}")
        dmn = _matmul(dkv, W_kv[i], tb=True, name=f"mem_kv_dx{i}")
        _, g_norm = _rms_bwd(mems, mem_norm[i:i + 1], dmn, jnp.zeros_like(mems), name=f"mem_norm_bwd{i}")
        return g_wkv, g_norm

    dx, g_wfi1_t, g_wfo1, g_ffn1 = ffn_bwd(dx, x3, hf1, gu1, act1, 1, loss_part)
    dheads = _matmul(dx, W_out[1], tb=True, name="out_proj_dx1")
    g_wout1 = _matmul(heads1, dx, ta=True, out_dtype=BF16, name="out_proj_dw1")
    dp, g_ws, g_bs, g_lng, g_lnb = _gmlp_bwd(p1, ln_g, ln_b, ws, bsb, dheads, name="gmlp_bwd")
    dp, dk, dv = _attn_bwd(p1, 2 * D_TOK // D_MEM, kv[1], dheads, dp, name="attn_bwd1")
    g_wkv1, g_mem1 = mem_bwd(jnp.concatenate([dk, dv], axis=1), 1)
    g_wgm_t = _matmul(dp, h1, ta=True, out_dtype=BF16, name="gm_in_dw")
    dx, g_mix1 = _matmul(dp, W_gmT, res=dx, norm_bwd=(x2, mix_norm[1:2]), name="gm_in_dx")
    reduce_l1 = _reduce_start([g_wkv1, g_wout1, g_wgm_t, g_wfi1_t, g_wfo1], name="reduce_l1_start")

    dx, g_wfi0_t, g_wfo0, g_ffn0 = ffn_bwd(dx, x1, hf0, gu0, act0, 0, reduce_l1[4])
    reduce_ffn0 = _reduce_start([g_wfi0_t, g_wfo0], name="reduce_ffn0_start")
    dheads = _matmul(dx, W_out[0], tb=True, name="out_proj_dx0", dep=reduce_ffn0[4])
    g_wout0 = _matmul(heads0, dx, ta=True, out_dtype=BF16, name="out_proj_dw0")
    dp, g_lb0, g_onorm = _hgrn2_bwd(p0, lb0, hg_onorm, o0, states, dheads, name="hgrn2_bwd")
    dp, dk, dv = _attn_bwd(p0, 4 * D_TOK // D_MEM, kv[0], dheads, dp, name="attn_bwd0")
    g_wkv0, g_mem0 = mem_bwd(jnp.concatenate([dk, dv], axis=1), 0)
    g_whg_t = _matmul(dp, h0, ta=True, out_dtype=BF16, name="hg_in_dw")
    reduce_mix0 = _reduce_start([g_wkv0, g_wout0, g_whg_t], name="reduce_mix0_start")
    grad_x, g_mix0 = _matmul(dp, W_hgT, res=dx, norm_bwd=(xs, mix_norm[0:1]), name="hg_in_dx", dep=reduce_mix0[4])

    g_kv1, g_out1, g_gm_t, g_fi1_t, g_fo1 = _reduce_finish(reduce_l1, grad_x, me, name="reduce_l1")
    g_fi0_t, g_fo0 = _reduce_finish(reduce_ffn0, g_kv1, me, name="reduce_ffn0")
    g_kv0, g_out0, g_hg_t = _reduce_finish(reduce_mix0, g_fi0_t, me, name="reduce_mix0")
    g_shards = [jnp.stack([g_kv0, g_kv1]), jnp.stack([g_out0, g_out1]), g_hg_t.T[None], g_gm_t.T[None],
                jnp.stack([g_fi0_t.T, g_fi1_t.T]), jnp.stack([g_fo0, g_fo1])]

    small = [loss_part, jnp.concatenate([g_mix0, g_mix1], axis=1), jnp.concatenate([g_mem0, g_mem1], axis=1),
             g_lb0, g_onorm, g_lng, g_lnb, g_ws.reshape(1, -1), g_bs.reshape(1, -1),
             jnp.concatenate([g_ffn0, g_ffn1], axis=1), g_final]
    sizes = [t.shape[1] for t in small]
    small_rows = _pad_rows(jnp.concatenate(small, axis=1).reshape(-1, LANE), 8)
    red = _all_gather(small_rows, name="reduce_small", in_vmem=True, reduce_sum=True).reshape(-1)
    pieces, off = [], 0
    for n in sizes:
        pieces.append(red[off:off + n])
        off += n
    loss = pieces[0][0]
    g_mix_norm = pieces[1].reshape(2, D_MODEL)
    g_mem_norm = pieces[2].reshape(2, D_MODEL)
    g_hg_lb = pieces[3][None, :] * lb0 * (jnp.eye(3, dtype=F32)[:, 0:1] - lb_soft)
    g_hg_onorm = pieces[4].reshape(1, D_TOK)
    width = D_TOK // N_DEV
    g_gm_ln_g = lax.dynamic_slice(pieces[5], (me * width,), (width,)).reshape(1, width)
    g_gm_ln_b = lax.dynamic_slice(pieces[6], (me * width,), (width,)).reshape(1, width)
    g_gm_ws = pieces[7].reshape(gm_ws.shape)
    g_gm_bs = pieces[8].reshape(gm_bs.shape)
    g_ffn_norm = pieces[9].reshape(2, D_MODEL)
    g_final_norm = pieces[10]

    grads = [g_mix_norm, g_mem_norm, g_shards[0], g_shards[1], g_shards[2], g_hg_lb, g_hg_onorm, g_shards[3],
             g_gm_ln_g, g_gm_ln_b, g_gm_ws, g_gm_bs, g_ffn_norm, g_shards[4], g_shards[5], g_final_norm]
    weights = [mix_norm, mem_norm, w_mem_kv, w_out, hg_w_in, hg_lb, hg_onorm, gm_w_in, gm_ln_g, gm_ln_b, gm_ws, gm_bs,
               ffn_norm, w_ffn_in, w_ffn_out, final_norm]
    ms = [m_mix_norm, m_mem_norm, m_w_mem_kv, m_w_out, m_hg_w_in, m_hg_lb, m_hg_onorm, m_gm_w_in, m_gm_ln_g,
          m_gm_ln_b, m_gm_ws, m_gm_bs, m_ffn_norm, m_w_ffn_in, m_w_ffn_out, m_final_norm]
    vs = [v_mix_norm, v_mem_norm, v_w_mem_kv, v_w_out, v_hg_w_in, v_hg_lb, v_hg_onorm, v_gm_w_in, v_gm_ln_g,
          v_gm_ln_b, v_gm_ws, v_gm_bs, v_ffn_norm, v_w_ffn_in, v_w_ffn_out, v_final_norm]
    deltas, new_m, new_v = [], [], []
    for n, (w, g, m, v) in enumerate(zip(weights, grads, ms, vs)):
        if w.ndim == 1:
            d, nm, nv = _adamw(w[None], g.reshape(1, -1), m[None], v[None], name=f"adamw{n}")
            d, nm, nv = d[0], nm[0], nv[0]
        else:
            d, nm, nv = _adamw(w, g.reshape(w.shape), m, v, name=f"adamw{n}")
        deltas.append(d)
        new_m.append(nm)
        new_v.append(nv)
    grads = [g.reshape(w.shape) for g, w in zip(grads, weights)]
    return (loss, grad_x[None], *grads, *deltas, *new_m, *new_v)
```

```python
import functools

import jax
import jax.numpy as jnp
from jax import lax
from jax.experimental import pallas as pl
from jax.experimental.pallas import tpu as pltpu

F32 = jnp.float32
BF16 = jnp.bfloat16
MXU_DTYPE = jnp.bfloat16
MESH_ID = pl.DeviceIdType.MESH

N_DEV = 8
EPS = 1e-6
D_MODEL = 1024
D_TOK = 768
D_MEM = 256
N_HEADS = 6
HEAD = 128
MEM_HEADS = 4
MEM_HDIM = 64
GM_CHUNK = 128
D_FF = 2816
HG_SUB = 16
HG_IN = 4 * D_TOK + D_MEM
GM_IN = 2 * D_TOK + D_MEM
LANE = 128
MXU_COLS = 256

ADAM_LR = 0.001
ADAM_B1 = 0.9
ADAM_B2 = 0.999
ADAM_EPS = 1e-08
ADAM_WD = 0.01
ADAM_STEP = 10

VMEM_LIMIT = 48 * 2 ** 20


def _params(sem=None):
    return pltpu.CompilerParams(dimension_semantics=sem, vmem_limit_bytes=VMEM_LIMIT)


def _tile(n, cap, q=LANE):
    if n <= cap:
        return n
    best = None
    for t in range(q, cap + 1, q):
        if n % t == 0:
            best = t
    assert best is not None, (n, cap, q)
    return best


def _sigmoid(x):
    return 1.0 / (1.0 + jnp.exp(-x))


def _gelu(x):
    return 0.5 * x * (1.0 + lax.erf(x * 0.7071067811865476))


def _gelu_grad(x):
    return 0.5 * (1.0 + lax.erf(x * 0.7071067811865476)) + x * jnp.exp(-0.5 * x * x) * 0.3989422804014327


def _matmul(a, b, *, name, ta=False, tb=False, res=None, out_dtype=F32, a_halves=False, b_halves=False, dep=None,
            norm_gain=None, norm_bwd=None):
    if a_halves and ta:
        K, M = a.shape[1], 2 * a.shape[2]
    elif a_halves:
        M, K = a.shape[1], 2 * a.shape[2]
    else:
        K, M = a.shape if ta else a.shape[::-1]
    if b_halves:
        assert not tb and b.shape[1] == K
        N = 2 * b.shape[2]
    else:
        N = b.shape[0] if tb else b.shape[1]
        assert (b.shape[1] if tb else b.shape[0]) == K
    tm = _tile(M // 2 if (a_halves and ta) else M, 1664 if ta else (512 if norm_bwd is not None else 1024))
    tn = _tile(N // 2 if b_halves else N, 1792)
    tk = _tile(K // 2 if (a_halves and not ta) else K, 1024 if ta else 1664)
    nk = K // tk
    dims = (((0 if ta else 1,), (1 if tb else 0,)), ((), ()))

    n_in = 2 + (res is not None) + (norm_gain is not None) + 2 * (norm_bwd is not None) + (dep is not None)
    if norm_gain is not None or norm_bwd is not None:
        assert tn == N, "the fused norm needs whole rows"
        assert norm_gain is None or norm_bwd is None
        assert norm_bwd is None or res is not None

    def body(*refs):
        a_ref, b_ref = refs[:2]
        r_ref = refs[2] if res is not None else None
        g_ref = refs[2 + (res is not None)] if (norm_gain is not None or norm_bwd is not None) else None
        x_ref = refs[3 + (res is not None)] if norm_bwd is not None else None
        o_ref = refs[n_in]
        h_ref = refs[n_in + 1] if (norm_gain is not None or norm_bwd is not None) else None
        acc = None if nk == 1 else refs[-1]
        k = pl.program_id(2)

        def product():
            return lax.dot_general(a_ref[...].astype(MXU_DTYPE), b_ref[...].astype(MXU_DTYPE), dims,
                                   preferred_element_type=F32)

        def finish(r):
            if norm_bwd is not None:
                @pl.when(pl.program_id(0) == 0)
                def _():
                    h_ref[...] = jnp.zeros_like(h_ref)

                xv = x_ref[...]
                scale = lax.rsqrt(jnp.mean(xv * xv, axis=-1, keepdims=True) + EPS)
                xh = xv * scale
                h_ref[...] += jnp.sum(r * xh, axis=0, keepdims=True)
                u = r * g_ref[...]
                o_ref[...] = r_ref[...] + scale * (u - xh * jnp.mean(u * xh, axis=-1, keepdims=True))
                return
            if res is not None:
                r = r + r_ref[...].astype(F32)
            o_ref[...] = r.astype(out_dtype)
            if norm_gain is not None:
                scale = lax.rsqrt(jnp.mean(r * r, axis=-1, keepdims=True) + EPS)
                h_ref[...] = (r * scale * g_ref[...]).astype(h_ref.dtype)

        if nk == 1:
            finish(product())
            return

        @pl.when(k == 0)
        def _():
            acc[...] = product()

        @pl.when((k > 0) & (k < nk - 1))
        def _():
            acc[...] += product()

        @pl.when(k == nk - 1)
        def _():
            finish(acc[...] + product())

    if a_halves and ta:
        mh = M // 2 // tm
        a_spec = pl.BlockSpec((None, tk, tm), lambda i, j, k: (i // mh, k, i % mh))
    elif a_halves:
        kh = nk // 2
        a_spec = pl.BlockSpec((None, tm, tk), lambda i, j, k: (k // kh, i, k % kh))
    elif ta:
        a_spec = pl.BlockSpec((tk, tm), lambda i, j, k: (k, i))
    else:
        a_spec = pl.BlockSpec((tm, tk), lambda i, j, k: (i, k))
    if b_halves:
        nh = N // 2 // tn
        b_spec = pl.BlockSpec((None, tk, tn), lambda i, j, k: (j // nh, k, j % nh))
    elif tb:
        b_spec = pl.BlockSpec((tn, tk), lambda i, j, k: (j, k))
    else:
        b_spec = pl.BlockSpec((tk, tn), lambda i, j, k: (k, j))
    o_spec = pl.BlockSpec((tm, tn), lambda i, j, k: (i, j))
    in_specs = [a_spec, b_spec] + ([o_spec] if res is not None else [])
    args = (a, b) + ((res,) if res is not None else ())
    out_specs, out_shape = o_spec, jax.ShapeDtypeStruct((M, N), out_dtype)
    vec = pl.BlockSpec((1, N), lambda i, j, k: (0, 0))
    sem = ("parallel", "parallel", "arbitrary")
    if norm_gain is not None:
        in_specs.append(vec)
        args += (norm_gain,)
        out_specs, out_shape = [o_spec, o_spec], [out_shape, jax.ShapeDtypeStruct((M, N), BF16)]
    if norm_bwd is not None:
        x_in, gain = norm_bwd
        in_specs += [vec, o_spec]
        args += (gain, x_in)
        out_specs, out_shape = [o_spec, vec], [out_shape, jax.ShapeDtypeStruct((1, N), F32)]
        sem = ("arbitrary", "arbitrary", "arbitrary")
    if dep is not None:
        in_specs.append(pl.BlockSpec(memory_space=pl.ANY))
        args += (dep,)
    return pl.pallas_call(
        body, name=name, grid=(M // tm, N // tn, nk), in_specs=in_specs, out_specs=out_specs, out_shape=out_shape,
        scratch_shapes=[] if nk == 1 else [pltpu.VMEM((tm, tn), F32)], compiler_params=_params(sem))(*args)


def _ffn_in(hf, wt, *, name):
    S, K = hf.shape
    tm = _tile(S, 512)
    tn = _tile(D_FF, 1408)
    nh = D_FF // tn
    nt = (((1,), (1,)), ((), ()))

    def body(a_ref, bg_ref, bu_ref, gu_ref, act_ref):
        av = a_ref[...].astype(MXU_DTYPE)
        for c0 in range(0, tn, MXU_COLS):
            cs = slice(c0, min(c0 + MXU_COLS, tn))
            gate = lax.dot_general(av, bg_ref[cs, :].astype(MXU_DTYPE), nt, preferred_element_type=F32)
            up = lax.dot_general(av, bu_ref[cs, :].astype(MXU_DTYPE), nt, preferred_element_type=F32)
            gu_ref[0, :, cs] = gate.astype(gu_ref.dtype)
            gu_ref[1, :, cs] = up.astype(gu_ref.dtype)
            act_ref[:, cs] = (gate * _sigmoid(gate) * up).astype(act_ref.dtype)

    return pl.pallas_call(
        body, name=name, grid=(nh, S // tm),
        in_specs=[pl.BlockSpec((tm, K), lambda j, i: (i, 0)), pl.BlockSpec((tn, K), lambda j, i: (j, 0)),
                  pl.BlockSpec((tn, K), lambda j, i: (j + nh, 0))],
        out_specs=[pl.BlockSpec((2, tm, tn), lambda j, i: (0, i, j)), pl.BlockSpec((tm, tn), lambda j, i: (i, j))],
        out_shape=[jax.ShapeDtypeStruct((2, S, D_FF), BF16), jax.ShapeDtypeStruct((S, D_FF), BF16)],
        compiler_params=_params(("parallel", "parallel")))(hf, wt, wt)


def _ffn_out_dx(dx, w, gu, dep, *, name):
    S, K = dx.shape
    tm = _tile(S, 1024)
    tn = _tile(D_FF, 1408)

    def body(a_ref, b_ref, gu_ref, dep_ref, o_ref):
        del dep_ref
        av = a_ref[...].astype(MXU_DTYPE)
        for c0 in range(0, tn, MXU_COLS):
            cs = slice(c0, min(c0 + MXU_COLS, tn))
            da = lax.dot_general(av, b_ref[cs, :].astype(MXU_DTYPE), (((1,), (1,)), ((), ())),
                                 preferred_element_type=F32)
            gate = gu_ref[0, :, cs].astype(F32)
            up = gu_ref[1, :, cs].astype(F32)
            sg = _sigmoid(gate)
            o_ref[0, :, cs] = (da * up * sg * (1.0 + gate * (1.0 - sg))).astype(o_ref.dtype)
            o_ref[1, :, cs] = (da * gate * sg).astype(o_ref.dtype)

    halves = pl.BlockSpec((2, tm, tn), lambda i, j: (0, i, j))
    return pl.pallas_call(
        body, name=name, grid=(S // tm, D_FF // tn),
        in_specs=[pl.BlockSpec((tm, K), lambda i, j: (i, 0)), pl.BlockSpec((tn, K), lambda i, j: (j, 0)), halves,
                  pl.BlockSpec(memory_space=pl.ANY)],
        out_specs=halves, out_shape=jax.ShapeDtypeStruct((2, S, D_FF), BF16),
        compiler_params=_params(("parallel", "parallel")))(dx, w, gu, dep)


def _rms_fwd(x, g, *, name, dep=None):
    R, Dm = x.shape
    tr = _tile(R, 512, 8)

    def body(x_ref, g_ref, *rest):
        o_ref = rest[-1]
        xv = x_ref[...]
        r = lax.rsqrt(jnp.mean(xv * xv, axis=-1, keepdims=True) + EPS)
        o_ref[...] = (xv * r * g_ref[...]).astype(o_ref.dtype)

    in_specs = [pl.BlockSpec((tr, Dm), lambda i: (i, 0)), pl.BlockSpec((1, Dm), lambda i: (0, 0))]
    args = (x, g)
    if dep is not None:
        in_specs.append(pl.BlockSpec(memory_space=pl.ANY))
        args += (dep,)
    return pl.pallas_call(
        body, name=name, grid=(R // tr,), in_specs=in_specs,
        out_specs=pl.BlockSpec((tr, Dm), lambda i: (i, 0)), out_shape=jax.ShapeDtypeStruct((R, Dm), BF16),
        compiler_params=_params(("parallel",)))(*args)


def _rms_bwd(x, g, dh, dres, *, name):
    R, Dm = x.shape
    tr = _tile(R, 256, 8)

    def body(x_ref, g_ref, dh_ref, dres_ref, dx_ref, dg_ref):
        @pl.when(pl.program_id(0) == 0)
        def _():
            dg_ref[...] = jnp.zeros_like(dg_ref)

        xv = x_ref[...]
        r = lax.rsqrt(jnp.mean(xv * xv, axis=-1, keepdims=True) + EPS)
        xh = xv * r
        dhv = dh_ref[...].astype(F32)
        dg_ref[...] += jnp.sum(dhv * xh, axis=0, keepdims=True)
        u = dhv * g_ref[...]
        dx = r * (u - xh * jnp.mean(u * xh, axis=-1, keepdims=True))
        dx_ref[...] = dres_ref[...] + dx

    row = pl.BlockSpec((tr, Dm), lambda i: (i, 0))
    vec = pl.BlockSpec((1, Dm), lambda i: (0, 0))
    return pl.pallas_call(
        body, name=name, grid=(R // tr,), in_specs=[row, vec, row, row], out_specs=[row, vec],
        out_shape=[jax.ShapeDtypeStruct((R, Dm), F32), jax.ShapeDtypeStruct((1, Dm), F32)],
        compiler_params=_params(("arbitrary",)))(x, g, dh, dres)


def _final_loss(x, g, tgt, *, name):
    R, Dm = x.shape
    tr = _tile(R, 256, 8)

    def body(x_ref, g_ref, t_ref, dx_ref, dg_ref, loss_ref):
        @pl.when(pl.program_id(0) == 0)
        def _():
            dg_ref[...] = jnp.zeros_like(dg_ref)
            loss_ref[...] = jnp.zeros_like(loss_ref)

        xv = x_ref[...]
        r = lax.rsqrt(jnp.mean(xv * xv, axis=-1, keepdims=True) + EPS)
        xh = xv * r
        gv = g_ref[...]
        err = xh * gv - t_ref[...]
        part = 0.5 * jnp.sum(jnp.mean(err * err, axis=-1, keepdims=True), axis=0, keepdims=True)
        loss_ref[...] += jnp.broadcast_to(part, loss_ref.shape)
        dy = err * (1.0 / Dm)
        dg_ref[...] += jnp.sum(dy * xh, axis=0, keepdims=True)
        u = dy * gv
        dx_ref[...] = r * (u - xh * jnp.mean(u * xh, axis=-1, keepdims=True))

    row = pl.BlockSpec((tr, Dm), lambda i: (i, 0))
    vec = pl.BlockSpec((1, Dm), lambda i: (0, 0))
    one = pl.BlockSpec((1, LANE), lambda i: (0, 0))
    return pl.pallas_call(
        body, name=name, grid=(R // tr,), in_specs=[row, vec, row], out_specs=[row, vec, one],
        out_shape=[jax.ShapeDtypeStruct((R, Dm), F32), jax.ShapeDtypeStruct((1, Dm), F32),
                   jax.ShapeDtypeStruct((1, LANE), F32)],
        compiler_params=_params(("arbitrary",)))(x, g, tgt)


def _head_mask(h):
    lane = lax.broadcasted_iota(jnp.int32, (1, D_MEM), 1)
    return (lane >= h * MEM_HDIM) & (lane < (h + 1) * MEM_HDIM)


def _attn_probs(qv, k_mx, mask):
    s = lax.dot_general(jnp.where(mask, qv, 0.0).astype(MXU_DTYPE), k_mx, (((1,), (1,)), ((), ())),
                        preferred_element_type=F32) * (MEM_HDIM ** -0.5)
    e = jnp.exp(s - jnp.max(s, axis=-1, keepdims=True))
    return e / jnp.sum(e, axis=-1, keepdims=True)


def _attn_fwd(p, qcol, kv, heads, *, name):
    S = p.shape[0]
    M = kv.shape[0]
    ts = _tile(S, 512, 8)

    def body(q_ref, k_ref, v_ref, heads_in, o_ref):
        del heads_in
        qv = q_ref[...]
        kx = k_ref[...].astype(MXU_DTYPE)
        vv = v_ref[...]
        out = jnp.zeros((ts, D_MEM), F32)
        for h in range(MEM_HEADS):
            mask = _head_mask(h)
            pr = _attn_probs(qv, kx, mask)
            out = out + jnp.dot(pr.astype(MXU_DTYPE), jnp.where(mask, vv, 0.0).astype(MXU_DTYPE),
                                preferred_element_type=F32)
        o_ref[...] = out.astype(o_ref.dtype)

    return pl.pallas_call(
        body, name=name, grid=(S // ts,),
        in_specs=[pl.BlockSpec((ts, D_MEM), lambda i: (i, qcol)), pl.BlockSpec((M, D_MEM), lambda i: (0, 0)),
                  pl.BlockSpec((M, D_MEM), lambda i: (0, 1)), pl.BlockSpec(memory_space=pl.ANY)],
        out_specs=pl.BlockSpec((ts, D_MEM), lambda i: (i, D_TOK // D_MEM)),
        out_shape=jax.ShapeDtypeStruct(heads.shape, heads.dtype), input_output_aliases={3: 0},
        compiler_params=_params(("parallel",)))(p, kv, kv, heads)


def _attn_bwd(p, qcol, kv, dheads, dp, *, name):
    S = p.shape[0]
    M = kv.shape[0]
    ts = _tile(S, 512, 8)
    scale = MEM_HDIM ** -0.5

    def body(q_ref, k_ref, v_ref, do_ref, dp_in, dq_ref, dk_ref, dv_ref):
        del dp_in

        @pl.when(pl.program_id(0) == 0)
        def _():
            dk_ref[...] = jnp.zeros_like(dk_ref)
            dv_ref[...] = jnp.zeros_like(dv_ref)

        qv = q_ref[...]
        kv_ = k_ref[...]
        kx = kv_.astype(MXU_DTYPE)
        vv = v_ref[...]
        dox = do_ref[...].astype(MXU_DTYPE)
        qx = qv.astype(MXU_DTYPE)
        dq = jnp.zeros((ts, D_MEM), F32)
        for h in range(MEM_HEADS):
            mask = _head_mask(h)
            pr = _attn_probs(qv, kx, mask)
            vh = jnp.where(mask, vv, 0.0).astype(MXU_DTYPE)
            dpr = lax.dot_general(dox, vh, (((1,), (1,)), ((), ())), preferred_element_type=F32)
            ds = (pr * (dpr - jnp.sum(dpr * pr, axis=-1, keepdims=True)) * scale).astype(MXU_DTYPE)
            dq = dq + jnp.dot(ds, jnp.where(mask, kv_, 0.0).astype(MXU_DTYPE), preferred_element_type=F32)
            dk_h = lax.dot_general(ds, qx, (((0,), (0,)), ((), ())), preferred_element_type=F32)
            dv_h = lax.dot_general(pr.astype(MXU_DTYPE), dox, (((0,), (0,)), ((), ())), preferred_element_type=F32)
            dk_ref[...] += jnp.where(mask, dk_h, 0.0)
            dv_ref[...] += jnp.where(mask, dv_h, 0.0)
        dq_ref[...] = dq.astype(dq_ref.dtype)

    return pl.pallas_call(
        body, name=name, grid=(S // ts,),
        in_specs=[pl.BlockSpec((ts, D_MEM), lambda i: (i, qcol)), pl.BlockSpec((M, D_MEM), lambda i: (0, 0)),
                  pl.BlockSpec((M, D_MEM), lambda i: (0, 1)),
                  pl.BlockSpec((ts, D_MEM), lambda i: (i, D_TOK // D_MEM)), pl.BlockSpec(memory_space=pl.ANY)],
        out_specs=[pl.BlockSpec((ts, D_MEM), lambda i: (i, qcol)), pl.BlockSpec((M, D_MEM), lambda i: (0, 0)),
                   pl.BlockSpec((M, D_MEM), lambda i: (0, 0))],
        out_shape=[jax.ShapeDtypeStruct(dp.shape, dp.dtype), jax.ShapeDtypeStruct((M, D_MEM), F32),
                   jax.ShapeDtypeStruct((M, D_MEM), F32)],
        input_output_aliases={4: 0}, compiler_params=_params(("arbitrary",)))(p, kv, kv, dheads, dp)


def _gm_forward_parts(u_ref, v_ref, lng_ref, lnb_ref, w_ref, bsb_ref):
    zu = _gelu(u_ref[...])
    zv = _gelu(v_ref[...])
    mu = jnp.mean(zv, axis=-1, keepdims=True)
    cen = zv - mu
    rs = lax.rsqrt(jnp.mean(cen * cen, axis=-1, keepdims=True) + EPS)
    vh = cen * rs
    vn = vh * lng_ref[...] + lnb_ref[...]
    row = lax.broadcasted_iota(jnp.int32, (GM_CHUNK, GM_CHUNK), 0)
    col = lax.broadcasted_iota(jnp.int32, (GM_CHUNK, GM_CHUNK), 1)
    tril = row >= col
    wm = [jnp.where(tril, w_ref[g], 0.0).astype(MXU_DTYPE) for g in range(N_HEADS)]
    vnx = [vn[:, g * HEAD:(g + 1) * HEAD].astype(MXU_DTYPE) for g in range(N_HEADS)]
    sv = [jnp.dot(wm[g], vnx[g], preferred_element_type=F32) + bsb_ref[g] for g in range(N_HEADS)]
    return zu, vh, rs, wm, vnx, sv, tril


def _gmlp_fwd(p, lng, lnb, ws, bsb, *, name):
    S = p.shape[0]

    def body(u_ref, v_ref, lng_ref, lnb_ref, w_ref, bsb_ref, o_ref):
        zu, _, _, _, _, sv, _ = _gm_forward_parts(u_ref, v_ref, lng_ref, lnb_ref, w_ref, bsb_ref)
        for g in range(N_HEADS):
            o_ref[:, g * HEAD:(g + 1) * HEAD] = (zu[:, g * HEAD:(g + 1) * HEAD] * sv[g]).astype(o_ref.dtype)

    blk = lambda c: pl.BlockSpec((GM_CHUNK, D_TOK), lambda i: (i, c))
    vec = pl.BlockSpec((1, D_TOK), lambda i: (0, 0))
    cube = pl.BlockSpec((N_HEADS, GM_CHUNK, GM_CHUNK), lambda i: (0, 0, 0))
    return pl.pallas_call(
        body, name=name, grid=(S // GM_CHUNK,), in_specs=[blk(0), blk(1), vec, vec, cube, cube],
        out_specs=blk(0), out_shape=jax.ShapeDtypeStruct((S, D_MODEL), BF16),
        compiler_params=_params(("parallel",)))(p, p, lng, lnb, ws, bsb)


def _gmlp_bwd(p, lng, lnb, ws, bsb, dheads, *, name):
    S = p.shape[0]

    def body(u_ref, v_ref, lng_ref, lnb_ref, w_ref, bsb_ref, dt_ref, dp_ref, dw_ref, dbs_ref, dlg_ref, dlb_ref):
        @pl.when(pl.program_id(0) == 0)
        def _():
            dw_ref[...] = jnp.zeros_like(dw_ref)
            dbs_ref[...] = jnp.zeros_like(dbs_ref)
            dlg_ref[...] = jnp.zeros_like(dlg_ref)
            dlb_ref[...] = jnp.zeros_like(dlb_ref)

        zu, vh, rs, wm, vnx, sv, tril = _gm_forward_parts(u_ref, v_ref, lng_ref, lnb_ref, w_ref, bsb_ref)
        dt = dt_ref[...].astype(F32)
        dvn_parts = []
        for g in range(N_HEADS):
            sl = slice(g * HEAD, (g + 1) * HEAD)
            dsv = dt[:, sl] * zu[:, sl]
            dp_ref[:, sl] = (dt[:, sl] * sv[g] * _gelu_grad(u_ref[:, sl])).astype(dp_ref.dtype)
            dsx = dsv.astype(MXU_DTYPE)
            dw = lax.dot_general(dsx, vnx[g], (((1,), (1,)), ((), ())), preferred_element_type=F32)
            dw_ref[g] += jnp.where(tril, dw, 0.0)
            dbs_ref[g] += jnp.sum(dsv, axis=-1, keepdims=True)
            dvn_parts.append(lax.dot_general(wm[g], dsx, (((0,), (0,)), ((), ())), preferred_element_type=F32))
        dvn = jnp.concatenate(dvn_parts, axis=-1)
        dlg_ref[...] += jnp.sum(dvn * vh, axis=0, keepdims=True)
        dlb_ref[...] += jnp.sum(dvn, axis=0, keepdims=True)
        dvh = dvn * lng_ref[...]
        dzv = rs * (dvh - jnp.mean(dvh, axis=-1, keepdims=True) - vh * jnp.mean(dvh * vh, axis=-1, keepdims=True))
        dp_ref[:, D_TOK:] = (dzv * _gelu_grad(v_ref[...])).astype(dp_ref.dtype)

    blk = lambda c: pl.BlockSpec((GM_CHUNK, D_TOK), lambda i: (i, c))
    vec = pl.BlockSpec((1, D_TOK), lambda i: (0, 0))
    cube = pl.BlockSpec((N_HEADS, GM_CHUNK, GM_CHUNK), lambda i: (0, 0, 0))
    col = pl.BlockSpec((N_HEADS, GM_CHUNK, 1), lambda i: (0, 0, 0))
    return pl.pallas_call(
        body, name=name, grid=(S // GM_CHUNK,), in_specs=[blk(0), blk(1), vec, vec, cube, cube, blk(0)],
        out_specs=[pl.BlockSpec((GM_CHUNK, 2 * D_TOK), lambda i: (i, 0)), cube, col, vec, vec],
        out_shape=[jax.ShapeDtypeStruct((S, GM_IN), BF16), jax.ShapeDtypeStruct((N_HEADS, GM_CHUNK, GM_CHUNK), F32),
                   jax.ShapeDtypeStruct((N_HEADS, GM_CHUNK, 1), F32), jax.ShapeDtypeStruct((1, D_TOK), F32),
                   jax.ShapeDtypeStruct((1, D_TOK), F32)],
        compiler_params=_params(("arbitrary",)))(p, p, lng, lnb, ws, bsb, dheads)


def _chunk_tri(n, chunk, upper):
    r = lax.broadcasted_iota(jnp.int32, (n, n), 0)
    c = lax.broadcasted_iota(jnp.int32, (n, n), 1)
    same = (r // chunk) == (c // chunk)
    return jnp.where(same & ((r <= c) if upper else (r >= c)), 1.0, 0.0).astype(F32)


def _hg_gates(fz, lb):
    sg = _sigmoid(fz)
    f = lb + (1.0 - lb) * sg
    kk = (1.0 - lb) * (1.0 - sg)
    return sg, f, jnp.log(f), kk


def _hgrn2_fwd(p, lb, onorm, *, name):
    S = p.shape[0]
    C = HG_SUB
    tb = _tile(S, 256, C)
    nsub = tb // C

    def body(q_ref, fz_ref, v_ref, g_ref, lb_ref, on_ref, tok_ref, o_ref, st_ref, state, b_blk, k_blk, bsc, ksc, vsc):
        @pl.when(pl.program_id(0) == 0)
        def _():
            state[...] = jnp.zeros_like(state)

        _, _, lg, kk = _hg_gates(fz_ref[...], lb_ref[...])
        b_blk[...] = jnp.dot(_chunk_tri(tb, C, False), lg, precision=lax.Precision.HIGHEST,
                             preferred_element_type=F32)
        k_blk[...] = kk
        tt = lax.broadcasted_iota(jnp.int32, (C, HEAD), 0)

        def sub(c, carry):
            rows = pl.ds(pl.multiple_of(c * C, C), C)
            for h in range(N_HEADS):
                cols = slice(h * HEAD, (h + 1) * HEAD)
                qv = q_ref[rows, cols]
                vv = v_ref[rows, cols]
                b = b_blk[rows, cols]
                kk = k_blk[rows, cols]
                st0 = state[h]
                st0x = st0.astype(MXU_DTYPE)
                st_ref[c, h] = st0x.astype(st_ref.dtype)
                inter = lax.dot_general((qv * jnp.exp(b)).astype(MXU_DTYPE), st0x,
                                        (((1,), (1,)), ((), ())), preferred_element_type=F32)
                bsc[h] = b
                ksc[h] = kk
                vsc[h] = vv
                intra = jnp.zeros((C, HEAD), F32)
                for s in range(C):
                    dec = jnp.where(tt >= s, jnp.exp(b - bsc[h, pl.ds(s, 1), :]), 0.0)
                    a_s = jnp.sum(qv * ksc[h, pl.ds(s, 1), :] * dec, axis=-1, keepdims=True)
                    intra = intra + a_s * vsc[h, pl.ds(s, 1), :]
                o_ref[rows, cols] = inter + intra
                b_last = bsc[h, pl.ds(C - 1, 1), :]
                ke = kk * jnp.exp(b_last - b)
                state[h] = st0 * jnp.exp(b_last) + lax.dot_general(
                    vv.astype(MXU_DTYPE), ke.astype(MXU_DTYPE), (((0,), (0,)), ((), ())),
                    preferred_element_type=F32)
            return carry

        lax.fori_loop(0, nsub, sub, 0, unroll=2)

        for h in range(N_HEADS):
            cols = slice(h * HEAD, (h + 1) * HEAD)
            o = o_ref[:, cols]
            gv = g_ref[:, cols]
            n = o * lax.rsqrt(jnp.mean(o * o, axis=-1, keepdims=True) + EPS)
            tok_ref[:, cols] = (n * (gv * _sigmoid(gv)) * on_ref[:, cols]).astype(tok_ref.dtype)

    blk = lambda c: pl.BlockSpec((tb, D_TOK), lambda i, c=c: (i, c))
    vec = pl.BlockSpec((1, D_TOK), lambda i: (0, 0))
    stb = pl.BlockSpec((nsub, N_HEADS, HEAD, HEAD), lambda i: (i, 0, 0, 0))
    return pl.pallas_call(
        body, name=name, grid=(S // tb,), in_specs=[blk(0), blk(1), blk(2), blk(3), vec, vec],
        out_specs=[blk(0), blk(0), stb],
        out_shape=[jax.ShapeDtypeStruct((S, D_MODEL), BF16), jax.ShapeDtypeStruct((S, D_TOK), F32),
                   jax.ShapeDtypeStruct((S // C, N_HEADS, HEAD, HEAD), BF16)],
        scratch_shapes=[pltpu.VMEM((N_HEADS, HEAD, HEAD), F32)] + [pltpu.VMEM((tb, D_TOK), F32)] * 2
        + [pltpu.VMEM((N_HEADS, C, HEAD), F32)] * 3,
        compiler_params=_params(("arbitrary",)))(p, p, p, p, lb, onorm)


def _hgrn2_bwd(p, lb, onorm, o, states, dheads, *, name):
    S = p.shape[0]
    C = HG_SUB
    tb = _tile(S, 256, C)
    nsub = tb // C
    nblk = S // tb

    def body(q_ref, fz_ref, v_ref, g_ref, lb_ref, on_ref, o_ref, st_ref, dt_ref, dp_ref, dlb_ref, don_ref, dstate,
             b_blk, k_blk, do_blk, db_blk, dk_blk, dq_blk, dv_blk, bsc, ksc, vsc, qsc, dosc):
        @pl.when(pl.program_id(0) == 0)
        def _():
            dstate[...] = jnp.zeros_like(dstate)
            dlb_ref[...] = jnp.zeros_like(dlb_ref)
            don_ref[...] = jnp.zeros_like(don_ref)

        for h in range(N_HEADS):
            cols = slice(h * HEAD, (h + 1) * HEAD)
            onv = on_ref[:, cols]
            gv = g_ref[:, cols]
            ov = o_ref[:, cols]
            dt = dt_ref[:, cols].astype(F32)
            sgg = _sigmoid(gv)
            sil = gv * sgg
            rinv = lax.rsqrt(jnp.mean(ov * ov, axis=-1, keepdims=True) + EPS)
            n = ov * rinv
            don_ref[:, cols] += jnp.sum(dt * n * sil, axis=0, keepdims=True)
            dn = dt * sil * onv
            dp_ref[:, 3 * D_TOK + h * HEAD:3 * D_TOK + (h + 1) * HEAD] = (
                dt * n * onv * sgg * (1.0 + gv * (1.0 - sgg))).astype(dp_ref.dtype)
            do_blk[:, cols] = rinv * (dn - n * jnp.mean(dn * n, axis=-1, keepdims=True))
        _, _, lg, kk = _hg_gates(fz_ref[...], lb_ref[...])
        b_blk[...] = jnp.dot(_chunk_tri(tb, C, False), lg, precision=lax.Precision.HIGHEST,
                             preferred_element_type=F32)
        k_blk[...] = kk
        tt = lax.broadcasted_iota(jnp.int32, (C, HEAD), 0)

        def sub(j, carry):
            c = nsub - 1 - j
            rows = pl.ds(pl.multiple_of(c * C, C), C)
            for h in range(N_HEADS):
                cols = slice(h * HEAD, (h + 1) * HEAD)
                qv = q_ref[rows, cols]
                vv = v_ref[rows, cols]
                do = do_blk[rows, cols]
                b = b_blk[rows, cols]
                kk = k_blk[rows, cols]
                bsc[h] = b
                ksc[h] = kk
                vsc[h] = vv
                qsc[h] = qv
                dosc[h] = do
                b_last = bsc[h, pl.ds(C - 1, 1), :]
                eb = jnp.exp(b)
                qe = qv * eb
                ebb = jnp.exp(b_last - b)
                ke = kk * ebb
                e_last = jnp.exp(b_last)
                st0x = st_ref[c, h].astype(MXU_DTYPE)
                st0 = st0x.astype(F32)
                dst1 = dstate[h]
                dst1x = dst1.astype(MXU_DTYPE)
                dox = do.astype(MXU_DTYPE)
                dqe = jnp.dot(dox, st0x, preferred_element_type=F32)
                dke = jnp.dot(vv.astype(MXU_DTYPE), dst1x, preferred_element_type=F32)
                dv = lax.dot_general(ke.astype(MXU_DTYPE), dst1x, (((1,), (1,)), ((), ())),
                                     preferred_element_type=F32)
                db_last = (e_last * jnp.sum(st0 * dst1, axis=0, keepdims=True)
                           + jnp.sum(dke * ke, axis=0, keepdims=True))
                dstate[h] = dst1 * e_last + lax.dot_general(dox, qe.astype(MXU_DTYPE), (((0,), (0,)), ((), ())),
                                                            preferred_element_type=F32)
                dq = dqe * eb
                db = dqe * qe - dke * ke
                dkk = dke * ebb
                for s in range(C):
                    dec = jnp.where(tt >= s, jnp.exp(b - bsc[h, pl.ds(s, 1), :]), 0.0)
                    da_s = jnp.sum(do * vsc[h, pl.ds(s, 1), :], axis=-1, keepdims=True)
                    pq = da_s * ksc[h, pl.ds(s, 1), :] * dec
                    dq = dq + pq
                    db = db + pq * qv
                for t in range(C):
                    q_t = qsc[h, pl.ds(t, 1), :]
                    do_t = dosc[h, pl.ds(t, 1), :]
                    dec = jnp.where(tt <= t, jnp.exp(bsc[h, pl.ds(t, 1), :] - b), 0.0)
                    da_t = jnp.sum(vv * do_t, axis=-1, keepdims=True)
                    pk = da_t * q_t * dec
                    dkk = dkk + pk
                    db = db - pk * kk
                    a_t = jnp.sum(q_t * kk * dec, axis=-1, keepdims=True)
                    dv = dv + a_t * do_t
                db_blk[rows, cols] = db + jnp.where(tt == C - 1, db_last, 0.0)
                dk_blk[rows, cols] = dkk
                dq_blk[rows, cols] = dq
                dv_blk[rows, cols] = dv
            return carry

        lax.fori_loop(0, nsub, sub, 0)

        dlg = jnp.dot(_chunk_tri(tb, C, True), db_blk[...], precision=lax.Precision.HIGHEST,
                      preferred_element_type=F32)
        lbv = lb_ref[...]
        sg, f, _, _ = _hg_gates(fz_ref[...], lbv)
        w = dlg / f - dk_blk[...]
        dp_ref[:, 0:D_TOK] = dq_blk[...].astype(dp_ref.dtype)
        dp_ref[:, 2 * D_TOK:3 * D_TOK] = dv_blk[...].astype(dp_ref.dtype)
        dp_ref[:, D_TOK:2 * D_TOK] = (w * (1.0 - lbv) * sg * (1.0 - sg)).astype(dp_ref.dtype)
        dlb_ref[...] += jnp.sum(w * (1.0 - sg), axis=0, keepdims=True)

    blk = lambda c: pl.BlockSpec((tb, D_TOK), lambda i, c=c: (nblk - 1 - i, c))
    vec = pl.BlockSpec((1, D_TOK), lambda i: (0, 0))
    stb = pl.BlockSpec((nsub, N_HEADS, HEAD, HEAD), lambda i: (nblk - 1 - i, 0, 0, 0))
    small = jax.ShapeDtypeStruct((1, D_TOK), F32)
    return pl.pallas_call(
        body, name=name, grid=(nblk,), in_specs=[blk(0), blk(1), blk(2), blk(3), vec, vec, blk(0), stb, blk(0)],
        out_specs=[pl.BlockSpec((tb, 4 * D_TOK), lambda i: (nblk - 1 - i, 0)), vec, vec],
        out_shape=[jax.ShapeDtypeStruct((S, HG_IN), BF16), small, small],
        scratch_shapes=[pltpu.VMEM((N_HEADS, HEAD, HEAD), F32)] + [pltpu.VMEM((tb, D_TOK), F32)] * 7
        + [pltpu.VMEM((N_HEADS, C, HEAD), F32)] * 5,
        compiler_params=_params(("arbitrary",)))(p, p, p, p, lb, onorm, o, states, dheads)


def _adamw(w, g, m, v, *, name):
    shape = w.shape
    cols = shape[-1]
    w2, g2, m2, v2 = (t.reshape(-1, cols) for t in (w, g, m, v))
    R = w2.shape[0]
    tr = _tile(R, 512, 8)

    def body(w_ref, g_ref, m_ref, v_ref, d_ref, nm_ref, nv_ref):
        gv = g_ref[...]
        nm = ADAM_B1 * m_ref[...] + (1.0 - ADAM_B1) * gv
        nv = ADAM_B2 * v_ref[...] + (1.0 - ADAM_B2) * (gv * gv)
        m_hat = nm / (1.0 - ADAM_B1 ** ADAM_STEP)
        v_hat = nv / (1.0 - ADAM_B2 ** ADAM_STEP)
        d_ref[...] = -ADAM_LR * (m_hat / (jnp.sqrt(v_hat) + ADAM_EPS) + ADAM_WD * w_ref[...])
        nm_ref[...] = nm
        nv_ref[...] = nv

    spec = pl.BlockSpec((tr, cols), lambda i: (i, 0))
    out = jax.ShapeDtypeStruct((R, cols), F32)
    d, nm, nv = pl.pallas_call(body, name=name, grid=(R // tr,), in_specs=[spec] * 4, out_specs=[spec] * 3,
                               out_shape=[out] * 3, compiler_params=_params(("parallel",)))(w2, g2, m2, v2)
    return d.reshape(shape), nm.reshape(shape), nv.reshape(shape)


def _add_received(own, got, *, name):
    R, Cc = own.shape
    n = got.shape[0]
    tr = _tile(R, 256, 16)

    def body(a_ref, b_ref, o_ref):
        acc = a_ref[...].astype(F32)
        for k in range(n):
            acc = acc + b_ref[k].astype(F32)
        o_ref[...] = acc

    return pl.pallas_call(
        body, name=name, grid=(R // tr,),
        in_specs=[pl.BlockSpec((tr, Cc), lambda i: (i, 0)), pl.BlockSpec((n, tr, Cc), lambda i: (0, i, 0))],
        out_specs=pl.BlockSpec((tr, Cc), lambda i: (i, 0)), out_shape=jax.ShapeDtypeStruct((R, Cc), F32),
        compiler_params=_params(("parallel",)))(own, got)


def _place():
    return lax.axis_index("x"), lax.axis_index("y"), lax.axis_index("c")


def _all_gather(x, *, name, in_vmem, reduce_sum=False, with_token=False):
    R, Cc = x.shape
    space = pltpu.VMEM if in_vmem else pl.ANY

    def body(x_ref, out_ref, *scratch):
        if with_token:
            scratch[0][...] = jnp.zeros_like(scratch[0])
            scratch = scratch[1:]
        if reduce_sum:
            gat_ref, send_sems, recv_sems, local_sem = scratch
        else:
            gat_ref = out_ref
            send_sems, recv_sems, local_sem = scratch
        mx, my, mc = _place()
        me, sibling = (mx, my, mc), (mx, my, 1 - mc)
        chips = [(1 - mx, my), (mx, 1 - my), (1 - mx, 1 - my)]

        def rows(px, py, pc):
            return gat_ref.at[pl.ds((4 * px + 2 * py + pc) * R, R), :]

        def copy(k, block, to, src=None):
            return pltpu.make_async_remote_copy(
                src_ref=rows(*block) if src is None else src, dst_ref=rows(*block), send_sem=send_sems.at[k],
                recv_sem=recv_sems.at[k], device_id=to, device_id_type=MESH_ID)

        mine = pltpu.make_async_copy(x_ref, rows(*me), local_sem)
        mine.start()
        first = [copy(0, me, sibling, src=x_ref)]
        first += [copy(1 + j, me, (*chip, mc), src=x_ref) for j, chip in enumerate(chips)]
        for cp in first:
            cp.start()
        passed = [copy(4 + j, (*chip, mc), sibling) for j, chip in enumerate(chips)]
        for j, chip in enumerate(chips):
            copy(1 + j, (*chip, mc), me).wait_recv()
            passed[j].start()
        copy(0, sibling, me).wait_recv()
        for j, chip in enumerate(chips):
            copy(4 + j, (*chip, 1 - mc), me).wait_recv()
        for cp in first + passed:
            cp.wait_send()
        mine.wait()
        if reduce_sum:
            acc = gat_ref[pl.ds(0, R), :]
            for d in range(1, N_DEV):
                acc = acc + gat_ref[pl.ds(d * R, R), :]
            out_ref[...] = acc

    sems = [pltpu.SemaphoreType.DMA((7,)), pltpu.SemaphoreType.DMA((7,)), pltpu.SemaphoreType.DMA]
    if reduce_sum:
        assert in_vmem
        out_shape = jax.ShapeDtypeStruct((R, Cc), x.dtype)
        scratch = [pltpu.VMEM((N_DEV * R, Cc), x.dtype)] + sems
    else:
        out_shape = jax.ShapeDtypeStruct((N_DEV * R, Cc), x.dtype)
        scratch = sems
    out_specs = pl.BlockSpec(memory_space=space)
    if with_token:
        out_shape = (out_shape, jax.ShapeDtypeStruct((8, LANE), F32))
        out_specs = (out_specs, pl.BlockSpec(memory_space=pltpu.VMEM))
    return pl.pallas_call(
        body, name=name, out_shape=out_shape, in_specs=[pl.BlockSpec(memory_space=space)], out_specs=out_specs,
        scratch_shapes=scratch, compiler_params=pltpu.CompilerParams(vmem_limit_bytes=VMEM_LIMIT))(x)


def _peer(k, mx, my, mc):
    bits = k + 1
    return (1 - mx if bits & 4 else mx, 1 - my if bits & 2 else my, 1 - mc if bits & 1 else mc)


HBM_SPEC = pl.BlockSpec(memory_space=pltpu.HBM)
SEM_SPEC = pl.BlockSpec(memory_space=pltpu.SEMAPHORE)
DATAFLOW = pltpu.SideEffectType.DATAFLOW_SIDE_EFFECTING


def _exchange_copies(x_refs, land_refs, send_sems, recv_sems, scatter):
    mx, my, mc = _place()
    me = 4 * mx + 2 * my + mc
    n = len(x_refs)
    copies = []
    for k in range(N_DEV - 1):
        px, py, pc = _peer(k, mx, my, mc)
        for m, (x_ref, land_ref) in enumerate(zip(x_refs, land_refs)):
            rows = land_ref.shape[1] if scatter else x_ref.shape[0]
            if scatter:
                src = x_ref.at[pl.ds(pl.multiple_of((4 * px + 2 * py + pc) * rows, 16), rows), :]
                dst = land_ref.at[k]
            else:
                src = x_ref
                dst = land_ref.at[pl.ds(pl.multiple_of(me * rows, 16), rows), :]
            copies.append(pltpu.make_async_remote_copy(
                src_ref=src, dst_ref=dst, send_sem=send_sems.at[k * n + m], recv_sem=recv_sems.at[k * n + m],
                device_id=(px, py, pc), device_id_type=MESH_ID))
    return copies


def _land_shape(x, scatter):
    return (N_DEV - 1, x.shape[0] // N_DEV, x.shape[1]) if scatter else (N_DEV * x.shape[0], x.shape[1])


def _exchange_start(xs, *, name, scatter):
    n = len(xs)
    lands = [lax.empty(_land_shape(x, scatter), x.dtype) for x in xs]

    def body(*refs):
        x_refs, land_refs = refs[:n], refs[n:2 * n]
        send_sems, recv_sems = refs[2 * n:2 * n + 2]
        token = refs[-1]
        for cp in _exchange_copies(x_refs, land_refs, send_sems, recv_sems, scatter):
            cp.start()
        token[...] = jnp.zeros_like(token)

    sems = pltpu.SemaphoreType.DMA(((N_DEV - 1) * n,))
    out = pl.pallas_call(
        body, name=name,
        out_shape=(sems, sems, *[pltpu.HBM(x.shape, x.dtype) for x in xs],
                   *[pltpu.HBM(l.shape, l.dtype) for l in lands], jax.ShapeDtypeStruct((8, LANE), F32)),
        in_specs=(HBM_SPEC,) * (2 * n),
        out_specs=(SEM_SPEC, SEM_SPEC) + (HBM_SPEC,) * (2 * n) + (pl.BlockSpec(memory_space=pltpu.VMEM),),
        input_output_aliases={i: 2 + i for i in range(2 * n)},
        compiler_params=pltpu.CompilerParams(has_side_effects=DATAFLOW))(
            *[pltpu.with_memory_space_constraint(t, pltpu.HBM) for t in list(xs) + lands])
    return out[0], out[1], list(out[2:2 + n]), list(out[2 + n:2 + 2 * n]), out[-1]


def _exchange_wait(started, after, *, name, scatter):
    send_sems, recv_sems, xs, lands, _ = started
    n = len(xs)

    def body(*refs):
        x_refs, land_refs = refs[:n], refs[n:2 * n]
        send_sems, recv_sems = refs[2 * n:2 * n + 2]
        for cp in _exchange_copies(x_refs, land_refs, send_sems, recv_sems, scatter):
            cp.wait_send()
            cp.wait_recv()

    out = pl.pallas_call(
        body, name=name, out_shape=tuple(pltpu.HBM(t.shape, t.dtype) for t in xs + lands),
        in_specs=(HBM_SPEC,) * (2 * n) + (SEM_SPEC, SEM_SPEC, pl.BlockSpec(memory_space=pl.ANY)),
        out_specs=(HBM_SPEC,) * (2 * n), input_output_aliases={i: i for i in range(2 * n)},
        compiler_params=pltpu.CompilerParams(has_side_effects=DATAFLOW))(*xs, *lands, send_sems, recv_sems, after)
    return list(out[:n]), list(out[n:])


def _gather_start(shards, token, *, name):
    shards = [shards[0] + token[0, 0].astype(shards[0].dtype)] + list(shards[1:])
    return _exchange_start(shards, name=name, scatter=False)


def _gather_finish(started, after, me, *, name):
    xs, lands = _exchange_wait(started, after, name=name, scatter=False)
    return [lax.dynamic_update_slice(land, x, (me * x.shape[0], 0)) for land, x in zip(lands, xs)]


def _reduce_start(grads, *, name):
    return _exchange_start(grads, name=name, scatter=True)


def _reduce_finish(started, after, me, *, name):
    sent, gots = _exchange_wait(started, after, name=name + "_wait", scatter=True)
    out = []
    for m, (g, got) in enumerate(zip(sent, gots)):
        rows = g.shape[0] // N_DEV
        own = lax.dynamic_slice(g, (me * rows, 0), (rows, g.shape[1]))
        out.append(_add_received(own, got, name=f"{name}_add{m}"))
    return out


def _pad_rows(a, mult):
    r = (-a.shape[0]) % mult
    return a if r == 0 else jnp.concatenate([a, jnp.zeros((r,) + a.shape[1:], a.dtype)], axis=0)


def kernel(x, mem, mix_norm, mem_norm, w_mem_kv, w_out, hg_w_in, hg_lb, hg_onorm, gm_w_in, gm_ln_g, gm_ln_b, gm_ws, gm_bs, ffn_norm, w_ffn_in, w_ffn_out, final_norm, loss_target, m_mix_norm, m_mem_norm, m_w_mem_kv, m_w_out, m_hg_w_in, m_hg_lb, m_hg_onorm, m_gm_w_in, m_gm_ln_g, m_gm_ln_b, m_gm_ws, m_gm_bs, m_ffn_norm, m_w_ffn_in, m_w_ffn_out, m_final_norm, v_mix_norm, v_mem_norm, v_w_mem_kv, v_w_out, v_hg_w_in, v_hg_lb, v_hg_onorm, v_gm_w_in, v_gm_ln_g, v_gm_ln_b, v_gm_ws, v_gm_bs, v_ffn_norm, v_w_ffn_in, v_w_ffn_out, v_final_norm):
    mx, my, mc = _place()
    me = 4 * mx + 2 * my + mc
    xs = x[0]
    mems = mem[0]
    tgt = loss_target[0]

    hg_t = hg_w_in[0].T.astype(BF16)
    gm_t = gm_w_in[0].T.astype(BF16)
    fi_t = [w_ffn_in[i].T.astype(BF16) for i in range(2)]
    kv_b = [w_mem_kv[i].astype(BF16) for i in range(2)]
    out_b = [w_out[i].astype(BF16) for i in range(2)]
    fo_b = [w_ffn_out[i].astype(BF16) for i in range(2)]
    ln_local = _pad_rows(jnp.concatenate([gm_ln_g, gm_ln_b], axis=0), 8)
    ln_local = jnp.concatenate([ln_local, jnp.zeros((8, LANE - ln_local.shape[1]), F32)], axis=1)
    ln_all, token = _all_gather(ln_local, name="gather_ln", in_vmem=True, with_token=True)
    ln_all = ln_all.reshape(N_DEV, 8, LANE)
    ln_g = ln_all[:, 0, :D_TOK // N_DEV].reshape(1, D_TOK)
    ln_b = ln_all[:, 1, :D_TOK // N_DEV].reshape(1, D_TOK)
    W_hgT, token = _all_gather(hg_t + token[0, 0].astype(BF16), name="gather_first", in_vmem=False,
                               with_token=True)
    gather_mix = _gather_start(kv_b + out_b, token, name="gather_mix_start")
    gather_fi = [_gather_start([fi_t[0]], gather_mix[4], name="gather_fi0_start")]
    gather_fo = [_gather_start([fo_b[0]], gather_fi[0][4], name="gather_fo0_start")]
    gather_gm = _gather_start([gm_t], gather_fo[0][4], name="gather_gm_start")
    gather_fi.append(_gather_start([fi_t[1]], gather_gm[4], name="gather_fi1_start"))
    gather_fo.append(_gather_start([fo_b[1]], gather_fi[1][4], name="gather_fo1_start"))

    lb_soft = jax.nn.softmax(hg_lb, axis=0)
    lb0 = lb_soft[0:1]
    bsb = jnp.broadcast_to(gm_bs[0][:, :, None], (N_HEADS, GM_CHUNK, GM_CHUNK))
    ws = gm_ws[0]

    W_fiT, W_fo = [], []

    def ffn_fwd(xin, hf, i, next_gain):
        W_fiT.extend(_gather_finish(gather_fi[i], hf, me, name=f"gather_fi{i}_wait"))
        gu, act = _ffn_in(hf, W_fiT[i], name=f"ffn_in{i}")
        W_fo.extend(_gather_finish(gather_fo[i], act, me, name=f"gather_fo{i}_wait"))
        return gu, act, _matmul(act, W_fo[i], res=xin, norm_gain=next_gain, name=f"ffn_out{i}")

    h0 = _rms_fwd(xs, mix_norm[0:1], name="mix_norm0", dep=gather_fo[1][4])
    p0 = _matmul(h0, W_hgT, tb=True, name="hg_in")
    heads0, o0, states = _hgrn2_fwd(p0, lb0, hg_onorm, name="hgrn2_fwd")

    kv0, kv1, wo0, wo1 = _gather_finish(gather_mix, o0, me, name="gather_mix_wait")
    W_kv, W_out = [kv0, kv1], [wo0, wo1]
    mem_n, kv = [], []
    for i in range(2):
        mn = _rms_fwd(mems, mem_norm[i:i + 1], name=f"mem_norm{i}")
        mem_n.append(mn)
        kv.append(_matmul(mn, W_kv[i], name=f"mem_kv{i}"))

    heads0 = _attn_fwd(p0, 4 * D_TOK // D_MEM, kv[0], heads0, name="attn_fwd0")
    x1, hf0 = _matmul(heads0, W_out[0], res=xs, norm_gain=ffn_norm[0:1], name="out_proj0")
    gu0, act0, (x2, h1) = ffn_fwd(x1, hf0, 0, mix_norm[1:2])

    W_gmT, = _gather_finish(gather_gm, h1, me, name="gather_gm_wait")
    p1 = _matmul(h1, W_gmT, tb=True, name="gm_in")
    heads1 = _gmlp_fwd(p1, ln_g, ln_b, ws, bsb, name="gmlp_fwd")
    heads1 = _attn_fwd(p1, 2 * D_TOK // D_MEM, kv[1], heads1, name="attn_fwd1")
    x3, hf1 = _matmul(heads1, W_out[1], res=x2, norm_gain=ffn_norm[1:2], name="out_proj1")
    gu1, act1, x4 = ffn_fwd(x3, hf1, 1, None)

    dx, g_final, loss_part = _final_loss(x4, final_norm.reshape(1, D_MODEL), tgt, name="final_loss")

    def ffn_bwd(dx, xin, hf, gu, act, i, dep):
        dgu = _ffn_out_dx(dx, W_fo[i], gu, dep, name=f"ffn_out_dx{i}")
        g_wfo = _matmul(act, dx, ta=True, out_dtype=BF16, name=f"ffn_out_dw{i}")
        g_wfi_t = _matmul(dgu, hf, ta=True, a_halves=True, out_dtype=BF16, name=f"ffn_in_dw{i}")
        dx, g_norm = _matmul(dgu, W_fiT[i], a_halves=True, res=dx, norm_bwd=(xin, ffn_norm[i:i + 1]),
                             name=f"ffn_in_dx{i}")
        return dx, g_wfi_t, g_wfo, g_norm

    def mem_bwd(dkv, i):
        g_wkv = _matmul(mem_n[i], dkv, ta=True, out_dtype=BF16, name=f"mem_kv_dw{i}")
        dmn = _matmul(dkv, W_kv[i], tb=True, name=f"mem_kv_dx{i}")
        _, g_norm = _rms_bwd(mems, mem_norm[i:i + 1], dmn, jnp.zeros_like(mems), name=f"mem_norm_bwd{i}")
        return g_wkv, g_norm

    dx, g_wfi1_t, g_wfo1, g_ffn1 = ffn_bwd(dx, x3, hf1, gu1, act1, 1, loss_part)
    dheads = _matmul(dx, W_out[1], tb=True, name="out_proj_dx1")
    g_wout1 = _matmul(heads1, dx, ta=True, out_dtype=BF16, name="out_proj_dw1")
    dp, g_ws, g_bs, g_lng, g_lnb = _gmlp_bwd(p1, ln_g, ln_b, ws, bsb, dheads, name="gmlp_bwd")
    dp, dk, dv = _attn_bwd(p1, 2 * D_TOK // D_MEM, kv[1], dheads, dp, name="attn_bwd1")
    g_wkv1, g_mem1 = mem_bwd(jnp.concatenate([dk, dv], axis=1), 1)
    g_wgm_t = _matmul(dp, h1, ta=True, out_dtype=BF16, name="gm_in_dw")
    dx, g_mix1 = _matmul(dp, W_gmT, res=dx, norm_bwd=(x2, mix_norm[1:2]), name="gm_in_dx")
    reduce_l1 = _reduce_start([g_wkv1, g_wout1, g_wgm_t, g_wfi1_t, g_wfo1], name="reduce_l1_start")

    dx, g_wfi0_t, g_wfo0, g_ffn0 = ffn_bwd(dx, x1, hf0, gu0, act0, 0, reduce_l1[4])
    reduce_ffn0 = _reduce_start([g_wfi0_t, g_wfo0], name="reduce_ffn0_start")
    dheads = _matmul(dx, W_out[0], tb=True, name="out_proj_dx0", dep=reduce_ffn0[4])
    g_wout0 = _matmul(heads0, dx, ta=True, out_dtype=BF16, name="out_proj_dw0")
    dp, g_lb0, g_onorm = _hgrn2_bwd(p0, lb0, hg_onorm, o0, states, dheads, name="hgrn2_bwd")
    dp, dk, dv = _attn_bwd(p0, 4 * D_TOK // D_MEM, kv[0], dheads, dp, name="attn_bwd0")
    g_wkv0, g_mem0 = mem_bwd(jnp.concatenate([dk, dv], axis=1), 0)
    g_whg_t = _matmul(dp, h0, ta=True, out_dtype=BF16, name="hg_in_dw")
    reduce_mix0 = _reduce_start([g_wkv0, g_wout0, g_whg_t], name="reduce_mix0_start")
    grad_x, g_mix0 = _matmul(dp, W_hgT, res=dx, norm_bwd=(xs, mix_norm[0:1]), name="hg_in_dx", dep=reduce_mix0[4])

    g_kv1, g_out1, g_gm_t, g_fi1_t, g_fo1 = _reduce_finish(reduce_l1, grad_x, me, name="reduce_l1")
    g_fi0_t, g_fo0 = _reduce_finish(reduce_ffn0, g_kv1, me, name="reduce_ffn0")
    g_kv0, g_out0, g_hg_t = _reduce_finish(reduce_mix0, g_fi0_t, me, name="reduce_mix0")
    g_shards = [jnp.stack([g_kv0, g_kv1]), jnp.stack([g_out0, g_out1]), g_hg_t[None], g_gm_t[None],
                jnp.stack([g_fi0_t, g_fi1_t]), jnp.stack([g_fo0, g_fo1])]
    transposed = (4, 7, 13)

    small = [loss_part, jnp.concatenate([g_mix0, g_mix1], axis=1), jnp.concatenate([g_mem0, g_mem1], axis=1),
             g_lb0, g_onorm, g_lng, g_lnb, g_ws.reshape(1, -1), g_bs.reshape(1, -1),
             jnp.concatenate([g_ffn0, g_ffn1], axis=1), g_final]
    sizes = [t.shape[1] for t in small]
    small_rows = _pad_rows(jnp.concatenate(small, axis=1).reshape(-1, LANE), 8)
    red = _all_gather(small_rows, name="reduce_small", in_vmem=True, reduce_sum=True).reshape(-1)
    pieces, off = [], 0
    for n in sizes:
        pieces.append(red[off:off + n])
        off += n
    loss = pieces[0][0]
    g_mix_norm = pieces[1].reshape(2, D_MODEL)
    g_mem_norm = pieces[2].reshape(2, D_MODEL)
    g_hg_lb = pieces[3][None, :] * lb0 * (jnp.eye(3, dtype=F32)[:, 0:1] - lb_soft)
    g_hg_onorm = pieces[4].reshape(1, D_TOK)
    width = D_TOK // N_DEV
    g_gm_ln_g = lax.dynamic_slice(pieces[5], (me * width,), (width,)).reshape(1, width)
    g_gm_ln_b = lax.dynamic_slice(pieces[6], (me * width,), (width,)).reshape(1, width)
    g_gm_ws = pieces[7].reshape(gm_ws.shape)
    g_gm_bs = pieces[8].reshape(gm_bs.shape)
    g_ffn_norm = pieces[9].reshape(2, D_MODEL)
    g_final_norm = pieces[10]

    grads = [g_mix_norm, g_mem_norm, g_shards[0], g_shards[1], g_shards[2], g_hg_lb, g_hg_onorm, g_shards[3],
             g_gm_ln_g, g_gm_ln_b, g_gm_ws, g_gm_bs, g_ffn_norm, g_shards[4], g_shards[5], g_final_norm]
    weights = [mix_norm, mem_norm, w_mem_kv, w_out, hg_w_in, hg_lb, hg_onorm, gm_w_in, gm_ln_g, gm_ln_b, gm_ws, gm_bs,
               ffn_norm, w_ffn_in, w_ffn_out, final_norm]
    ms = [m_mix_norm, m_mem_norm, m_w_mem_kv, m_w_out, m_hg_w_in, m_hg_lb, m_hg_onorm, m_gm_w_in, m_gm_ln_g,
          m_gm_ln_b, m_gm_ws, m_gm_bs, m_ffn_norm, m_w_ffn_in, m_w_ffn_out, m_final_norm]
    vs = [v_mix_norm, v_mem_norm, v_w_mem_kv, v_w_out, v_hg_w_in, v_hg_lb, v_hg_onorm, v_gm_w_in, v_gm_ln_g,
          v_gm_ln_b, v_gm_ws, v_gm_bs, v_ffn_norm, v_w_ffn_in, v_w_ffn_out, v_final_norm]
    deltas, new_m, new_v = [], [], []
    for n, (w, g, m, v) in enumerate(zip(weights, grads, ms, vs)):
        if w.ndim == 1:
            d, nm, nv = _adamw(w[None], g.reshape(1, -1), m[None], v[None], name=f"adamw{n}")
            d, nm, nv = d[0], nm[0], nv[0]
        elif n in transposed:
            flip = lambda t: jnp.swapaxes(t, 1, 2)
            d, nm, nv = (flip(t) for t in _adamw(flip(w), g, flip(m), flip(v), name=f"adamw{n}"))
            grads[n] = flip(g)
        else:
            d, nm, nv = _adamw(w, g.reshape(w.shape), m, v, name=f"adamw{n}")
        deltas.append(d)
        new_m.append(nm)
        new_v.append(nv)
    grads = [g.reshape(w.shape) for g, w in zip(grads, weights)]
    return (loss, grad_x[None], *grads, *deltas, *new_m, *new_v)
```

```python
import functools

import jax
import jax.numpy as jnp
from jax import lax
from jax.experimental import pallas as pl
from jax.experimental.pallas import tpu as pltpu

F32 = jnp.float32
BF16 = jnp.bfloat16
MXU_DTYPE = jnp.bfloat16
MESH_ID = pl.DeviceIdType.MESH

N_DEV = 8
EPS = 1e-6
D_MODEL = 1024
D_TOK = 768
D_MEM = 256
N_HEADS = 6
HEAD = 128
MEM_HEADS = 4
MEM_HDIM = 64
GM_CHUNK = 128
D_FF = 2816
HG_SUB = 16
HG_IN = 4 * D_TOK + D_MEM
GM_IN = 2 * D_TOK + D_MEM
LANE = 128
MXU_COLS = 256

ADAM_LR = 0.001
ADAM_B1 = 0.9
ADAM_B2 = 0.999
ADAM_EPS = 1e-08
ADAM_WD = 0.01
ADAM_STEP = 10

VMEM_LIMIT = 48 * 2 ** 20
VMEM_LIMIT_WIDE = 58 * 2 ** 20


def _params(sem=None, limit=VMEM_LIMIT):
    return pltpu.CompilerParams(dimension_semantics=sem, vmem_limit_bytes=limit)


def _tile(n, cap, q=LANE):
    if n <= cap:
        return n
    best = None
    for t in range(q, cap + 1, q):
        if n % t == 0:
            best = t
    assert best is not None, (n, cap, q)
    return best


def _sigmoid(x):
    return 1.0 / (1.0 + jnp.exp(-x))


def _gelu(x):
    return 0.5 * x * (1.0 + lax.erf(x * 0.7071067811865476))


def _gelu_grad(x):
    return 0.5 * (1.0 + lax.erf(x * 0.7071067811865476)) + x * jnp.exp(-0.5 * x * x) * 0.3989422804014327


def _matmul(a, b, *, name, ta=False, tb=False, res=None, out_dtype=F32, a_halves=False, b_halves=False, dep=None,
            norm_gain=None, norm_bwd=None):
    if a_halves and ta:
        K, M = a.shape[1], 2 * a.shape[2]
    elif a_halves:
        M, K = a.shape[1], 2 * a.shape[2]
    else:
        K, M = a.shape if ta else a.shape[::-1]
    if b_halves:
        assert not tb and b.shape[1] == K
        N = 2 * b.shape[2]
    else:
        N = b.shape[0] if tb else b.shape[1]
        assert (b.shape[1] if tb else b.shape[0]) == K
    tm = _tile(M // 2 if (a_halves and ta) else M, 1664 if ta else 1024)
    tn = _tile(N // 2 if b_halves else N, 1792)
    tk = _tile(K // 2 if (a_halves and not ta) else K, 1024 if ta else 1664)
    nk = K // tk
    dims = (((0 if ta else 1,), (1 if tb else 0,)), ((), ()))

    n_in = 2 + (res is not None) + (norm_gain is not None) + 2 * (norm_bwd is not None) + (dep is not None)
    if norm_gain is not None or norm_bwd is not None:
        assert tn == N, "the fused norm needs whole rows"
        assert norm_gain is None or norm_bwd is None
        assert norm_bwd is None or (res is not None and nk > 1 and tm % LANE == 0)

    def body(*refs):
        a_ref, b_ref = refs[:2]
        r_ref = refs[2] if res is not None else None
        g_ref = refs[2 + (res is not None)] if (norm_gain is not None or norm_bwd is not None) else None
        x_ref = refs[3 + (res is not None)] if norm_bwd is not None else None
        o_ref = refs[n_in]
        h_ref = refs[n_in + 1] if (norm_gain is not None or norm_bwd is not None) else None
        acc = None if nk == 1 else refs[-1]
        k = pl.program_id(2)

        def product():
            return lax.dot_general(a_ref[...].astype(MXU_DTYPE), b_ref[...].astype(MXU_DTYPE), dims,
                                   preferred_element_type=F32)

        def finish(r):
            if norm_bwd is not None:
                @pl.when(pl.program_id(0) == 0)
                def _():
                    h_ref[...] = jnp.zeros_like(h_ref)

                acc[...] = r
                gv = g_ref[...]

                def strip(s, dg):
                    rows = pl.ds(pl.multiple_of(s * LANE, LANE), LANE)
                    rv = acc[rows, :]
                    xv = x_ref[rows, :]
                    scale = lax.rsqrt(jnp.mean(xv * xv, axis=-1, keepdims=True) + EPS)
                    xh = xv * scale
                    u = rv * gv
                    o_ref[rows, :] = r_ref[rows, :] + scale * (u - xh * jnp.mean(u * xh, axis=-1, keepdims=True))
                    return dg + jnp.sum(rv * xh, axis=0, keepdims=True)

                h_ref[...] += lax.fori_loop(0, tm // LANE, strip, jnp.zeros((1, N), F32))
                return
            if res is not None:
                r = r + r_ref[...].astype(F32)
            o_ref[...] = r.astype(out_dtype)
            if norm_gain is not None:
                scale = lax.rsqrt(jnp.mean(r * r, axis=-1, keepdims=True) + EPS)
                h_ref[...] = (r * scale * g_ref[...]).astype(h_ref.dtype)

        if nk == 1:
            finish(product())
            return

        @pl.when(k == 0)
        def _():
            acc[...] = product()

        @pl.when((k > 0) & (k < nk - 1))
        def _():
            acc[...] += product()

        @pl.when(k == nk - 1)
        def _():
            finish(acc[...] + product())

    if a_halves and ta:
        mh = M // 2 // tm
        a_spec = pl.BlockSpec((None, tk, tm), lambda i, j, k: (i // mh, k, i % mh))
    elif a_halves:
        kh = nk // 2
        a_spec = pl.BlockSpec((None, tm, tk), lambda i, j, k: (k // kh, i, k % kh))
    elif ta:
        a_spec = pl.BlockSpec((tk, tm), lambda i, j, k: (k, i))
    else:
        a_spec = pl.BlockSpec((tm, tk), lambda i, j, k: (i, k))
    if b_halves:
        nh = N // 2 // tn
        b_spec = pl.BlockSpec((None, tk, tn), lambda i, j, k: (j // nh, k, j % nh))
    elif tb:
        b_spec = pl.BlockSpec((tn, tk), lambda i, j, k: (j, k))
    else:
        b_spec = pl.BlockSpec((tk, tn), lambda i, j, k: (k, j))
    o_spec = pl.BlockSpec((tm, tn), lambda i, j, k: (i, j))
    in_specs = [a_spec, b_spec] + ([o_spec] if res is not None else [])
    args = (a, b) + ((res,) if res is not None else ())
    out_specs, out_shape = o_spec, jax.ShapeDtypeStruct((M, N), out_dtype)
    vec = pl.BlockSpec((1, N), lambda i, j, k: (0, 0))
    sem = ("parallel", "parallel", "arbitrary")
    if norm_gain is not None:
        in_specs.append(vec)
        args += (norm_gain,)
        out_specs, out_shape = [o_spec, o_spec], [out_shape, jax.ShapeDtypeStruct((M, N), BF16)]
    if norm_bwd is not None:
        x_in, gain = norm_bwd
        in_specs += [vec, o_spec]
        args += (gain, x_in)
        out_specs, out_shape = [o_spec, vec], [out_shape, jax.ShapeDtypeStruct((1, N), F32)]
        sem = ("arbitrary", "arbitrary", "arbitrary")
    if dep is not None:
        in_specs.append(pl.BlockSpec(memory_space=pl.ANY))
        args += (dep,)
    return pl.pallas_call(
        body, name=name, grid=(M // tm, N // tn, nk), in_specs=in_specs, out_specs=out_specs, out_shape=out_shape,
        scratch_shapes=[] if nk == 1 else [pltpu.VMEM((tm, tn), F32)],
        compiler_params=_params(sem, VMEM_LIMIT_WIDE if norm_bwd is not None else VMEM_LIMIT))(*args)


def _ffn_in(hf, wt, *, name):
    S, K = hf.shape
    tm = _tile(S, 512)
    tn = _tile(D_FF, 1408)
    nh = D_FF // tn
    nt = (((1,), (1,)), ((), ()))

    def body(a_ref, bg_ref, bu_ref, gu_ref, act_ref):
        av = a_ref[...].astype(MXU_DTYPE)
        for c0 in range(0, tn, MXU_COLS):
            cs = slice(c0, min(c0 + MXU_COLS, tn))
            gate = lax.dot_general(av, bg_ref[cs, :].astype(MXU_DTYPE), nt, preferred_element_type=F32)
            up = lax.dot_general(av, bu_ref[cs, :].astype(MXU_DTYPE), nt, preferred_element_type=F32)
            gu_ref[0, :, cs] = gate.astype(gu_ref.dtype)
            gu_ref[1, :, cs] = up.astype(gu_ref.dtype)
            act_ref[:, cs] = (gate * _sigmoid(gate) * up).astype(act_ref.dtype)

    return pl.pallas_call(
        body, name=name, grid=(nh, S // tm),
        in_specs=[pl.BlockSpec((tm, K), lambda j, i: (i, 0)), pl.BlockSpec((tn, K), lambda j, i: (j, 0)),
                  pl.BlockSpec((tn, K), lambda j, i: (j + nh, 0))],
        out_specs=[pl.BlockSpec((2, tm, tn), lambda j, i: (0, i, j)), pl.BlockSpec((tm, tn), lambda j, i: (i, j))],
        out_shape=[jax.ShapeDtypeStruct((2, S, D_FF), BF16), jax.ShapeDtypeStruct((S, D_FF), BF16)],
        compiler_params=_params(("parallel", "parallel")))(hf, wt, wt)


def _ffn_out_dx(dx, w, gu, dep, *, name):
    S, K = dx.shape
    tm = _tile(S, 1024)
    tn = _tile(D_FF, 1408)

    def body(a_ref, b_ref, gu_ref, dep_ref, o_ref):
        del dep_ref
        av = a_ref[...].astype(MXU_DTYPE)
        for c0 in range(0, tn, MXU_COLS):
            cs = slice(c0, min(c0 + MXU_COLS, tn))
            da = lax.dot_general(av, b_ref[cs, :].astype(MXU_DTYPE), (((1,), (1,)), ((), ())),
                                 preferred_element_type=F32)
            gate = gu_ref[0, :, cs].astype(F32)
            up = gu_ref[1, :, cs].astype(F32)
            sg = _sigmoid(gate)
            o_ref[0, :, cs] = (da * up * sg * (1.0 + gate * (1.0 - sg))).astype(o_ref.dtype)
            o_ref[1, :, cs] = (da * gate * sg).astype(o_ref.dtype)

    halves = pl.BlockSpec((2, tm, tn), lambda i, j: (0, i, j))
    return pl.pallas_call(
        body, name=name, grid=(S // tm, D_FF // tn),
        in_specs=[pl.BlockSpec((tm, K), lambda i, j: (i, 0)), pl.BlockSpec((tn, K), lambda i, j: (j, 0)), halves,
                  pl.BlockSpec(memory_space=pl.ANY)],
        out_specs=halves, out_shape=jax.ShapeDtypeStruct((2, S, D_FF), BF16),
        compiler_params=_params(("parallel", "parallel")))(dx, w, gu, dep)


def _rms_fwd(x, g, *, name, dep=None):
    R, Dm = x.shape
    tr = _tile(R, 512, 8)

    def body(x_ref, g_ref, *rest):
        o_ref = rest[-1]
        xv = x_ref[...]
        r = lax.rsqrt(jnp.mean(xv * xv, axis=-1, keepdims=True) + EPS)
        o_ref[...] = (xv * r * g_ref[...]).astype(o_ref.dtype)

    in_specs = [pl.BlockSpec((tr, Dm), lambda i: (i, 0)), pl.BlockSpec((1, Dm), lambda i: (0, 0))]
    args = (x, g)
    if dep is not None:
        in_specs.append(pl.BlockSpec(memory_space=pl.ANY))
        args += (dep,)
    return pl.pallas_call(
        body, name=name, grid=(R // tr,), in_specs=in_specs,
        out_specs=pl.BlockSpec((tr, Dm), lambda i: (i, 0)), out_shape=jax.ShapeDtypeStruct((R, Dm), BF16),
        compiler_params=_params(("parallel",)))(*args)


def _rms_bwd(x, g, dh, dres, *, name):
    R, Dm = x.shape
    tr = _tile(R, 256, 8)

    def body(x_ref, g_ref, dh_ref, dres_ref, dx_ref, dg_ref):
        @pl.when(pl.program_id(0) == 0)
        def _():
            dg_ref[...] = jnp.zeros_like(dg_ref)

        xv = x_ref[...]
        r = lax.rsqrt(jnp.mean(xv * xv, axis=-1, keepdims=True) + EPS)
        xh = xv * r
        dhv = dh_ref[...].astype(F32)
        dg_ref[...] += jnp.sum(dhv * xh, axis=0, keepdims=True)
        u = dhv * g_ref[...]
        dx = r * (u - xh * jnp.mean(u * xh, axis=-1, keepdims=True))
        dx_ref[...] = dres_ref[...] + dx

    row = pl.BlockSpec((tr, Dm), lambda i: (i, 0))
    vec = pl.BlockSpec((1, Dm), lambda i: (0, 0))
    return pl.pallas_call(
        body, name=name, grid=(R // tr,), in_specs=[row, vec, row, row], out_specs=[row, vec],
        out_shape=[jax.ShapeDtypeStruct((R, Dm), F32), jax.ShapeDtypeStruct((1, Dm), F32)],
        compiler_params=_params(("arbitrary",)))(x, g, dh, dres)


def _final_loss(x, g, tgt, *, name):
    R, Dm = x.shape
    tr = _tile(R, 256, 8)

    def body(x_ref, g_ref, t_ref, dx_ref, dg_ref, loss_ref):
        @pl.when(pl.program_id(0) == 0)
        def _():
            dg_ref[...] = jnp.zeros_like(dg_ref)
            loss_ref[...] = jnp.zeros_like(loss_ref)

        xv = x_ref[...]
        r = lax.rsqrt(jnp.mean(xv * xv, axis=-1, keepdims=True) + EPS)
        xh = xv * r
        gv = g_ref[...]
        err = xh * gv - t_ref[...]
        part = 0.5 * jnp.sum(jnp.mean(err * err, axis=-1, keepdims=True), axis=0, keepdims=True)
        loss_ref[...] += jnp.broadcast_to(part, loss_ref.shape)
        dy = err * (1.0 / Dm)
        dg_ref[...] += jnp.sum(dy * xh, axis=0, keepdims=True)
        u = dy * gv
        dx_ref[...] = r * (u - xh * jnp.mean(u * xh, axis=-1, keepdims=True))

    row = pl.BlockSpec((tr, Dm), lambda i: (i, 0))
    vec = pl.BlockSpec((1, Dm), lambda i: (0, 0))
    one = pl.BlockSpec((1, LANE), lambda i: (0, 0))
    return pl.pallas_call(
        body, name=name, grid=(R // tr,), in_specs=[row, vec, row], out_specs=[row, vec, one],
        out_shape=[jax.ShapeDtypeStruct((R, Dm), F32), jax.ShapeDtypeStruct((1, Dm), F32),
                   jax.ShapeDtypeStruct((1, LANE), F32)],
        compiler_params=_params(("arbitrary",)))(x, g, tgt)


def _head_mask(h):
    lane = lax.broadcasted_iota(jnp.int32, (1, D_MEM), 1)
    return (lane >= h * MEM_HDIM) & (lane < (h + 1) * MEM_HDIM)


def _attn_probs(qv, k_mx, mask):
    s = lax.dot_general(jnp.where(mask, qv, 0.0).astype(MXU_DTYPE), k_mx, (((1,), (1,)), ((), ())),
                        preferred_element_type=F32) * (MEM_HDIM ** -0.5)
    e = jnp.exp(s - jnp.max(s, axis=-1, keepdims=True))
    return e / jnp.sum(e, axis=-1, keepdims=True)


def _attn_fwd(p, qcol, kv, heads, *, name):
    S = p.shape[0]
    M = kv.shape[0]
    ts = _tile(S, 512, 8)

    def body(q_ref, k_ref, v_ref, heads_in, o_ref):
        del heads_in
        qv = q_ref[...]
        kx = k_ref[...].astype(MXU_DTYPE)
        vv = v_ref[...]
        out = jnp.zeros((ts, D_MEM), F32)
        for h in range(MEM_HEADS):
            mask = _head_mask(h)
            pr = _attn_probs(qv, kx, mask)
            out = out + jnp.dot(pr.astype(MXU_DTYPE), jnp.where(mask, vv, 0.0).astype(MXU_DTYPE),
                                preferred_element_type=F32)
        o_ref[...] = out.astype(o_ref.dtype)

    return pl.pallas_call(
        body, name=name, grid=(S // ts,),
        in_specs=[pl.BlockSpec((ts, D_MEM), lambda i: (i, qcol)), pl.BlockSpec((M, D_MEM), lambda i: (0, 0)),
                  pl.BlockSpec((M, D_MEM), lambda i: (0, 1)), pl.BlockSpec(memory_space=pl.ANY)],
        out_specs=pl.BlockSpec((ts, D_MEM), lambda i: (i, D_TOK // D_MEM)),
        out_shape=jax.ShapeDtypeStruct(heads.shape, heads.dtype), input_output_aliases={3: 0},
        compiler_params=_params(("parallel",)))(p, kv, kv, heads)


def _attn_bwd(p, qcol, kv, dheads, dp, *, name):
    S = p.shape[0]
    M = kv.shape[0]
    ts = _tile(S, 512, 8)
    scale = MEM_HDIM ** -0.5

    def body(q_ref, k_ref, v_ref, do_ref, dp_in, dq_ref, dk_ref, dv_ref):
        del dp_in

        @pl.when(pl.program_id(0) == 0)
        def _():
            dk_ref[...] = jnp.zeros_like(dk_ref)
            dv_ref[...] = jnp.zeros_like(dv_ref)

        qv = q_ref[...]
        kv_ = k_ref[...]
        kx = kv_.astype(MXU_DTYPE)
        vv = v_ref[...]
        dox = do_ref[...].astype(MXU_DTYPE)
        qx = qv.astype(MXU_DTYPE)
        dq = jnp.zeros((ts, D_MEM), F32)
        for h in range(MEM_HEADS):
            mask = _head_mask(h)
            pr = _attn_probs(qv, kx, mask)
            vh = jnp.where(mask, vv, 0.0).astype(MXU_DTYPE)
            dpr = lax.dot_general(dox, vh, (((1,), (1,)), ((), ())), preferred_element_type=F32)
            ds = (pr * (dpr - jnp.sum(dpr * pr, axis=-1, keepdims=True)) * scale).astype(MXU_DTYPE)
            dq = dq + jnp.dot(ds, jnp.where(mask, kv_, 0.0).astype(MXU_DTYPE), preferred_element_type=F32)
            dk_h = lax.dot_general(ds, qx, (((0,), (0,)), ((), ())), preferred_element_type=F32)
            dv_h = lax.dot_general(pr.astype(MXU_DTYPE), dox, (((0,), (0,)), ((), ())), preferred_element_type=F32)
            dk_ref[...] += jnp.where(mask, dk_h, 0.0)
            dv_ref[...] += jnp.where(mask, dv_h, 0.0)
        dq_ref[...] = dq.astype(dq_ref.dtype)

    return pl.pallas_call(
        body, name=name, grid=(S // ts,),
        in_specs=[pl.BlockSpec((ts, D_MEM), lambda i: (i, qcol)), pl.BlockSpec((M, D_MEM), lambda i: (0, 0)),
                  pl.BlockSpec((M, D_MEM), lambda i: (0, 1)),
                  pl.BlockSpec((ts, D_MEM), lambda i: (i, D_TOK // D_MEM)), pl.BlockSpec(memory_space=pl.ANY)],
        out_specs=[pl.BlockSpec((ts, D_MEM), lambda i: (i, qcol)), pl.BlockSpec((M, D_MEM), lambda i: (0, 0)),
                   pl.BlockSpec((M, D_MEM), lambda i: (0, 0))],
        out_shape=[jax.ShapeDtypeStruct(dp.shape, dp.dtype), jax.ShapeDtypeStruct((M, D_MEM), F32),
                   jax.ShapeDtypeStruct((M, D_MEM), F32)],
        input_output_aliases={4: 0}, compiler_params=_params(("arbitrary",)))(p, kv, kv, dheads, dp)


def _gm_forward_parts(u_ref, v_ref, lng_ref, lnb_ref, w_ref, bsb_ref):
    zu = _gelu(u_ref[...])
    zv = _gelu(v_ref[...])
    mu = jnp.mean(zv, axis=-1, keepdims=True)
    cen = zv - mu
    rs = lax.rsqrt(jnp.mean(cen * cen, axis=-1, keepdims=True) + EPS)
    vh = cen * rs
    vn = vh * lng_ref[...] + lnb_ref[...]
    row = lax.broadcasted_iota(jnp.int32, (GM_CHUNK, GM_CHUNK), 0)
    col = lax.broadcasted_iota(jnp.int32, (GM_CHUNK, GM_CHUNK), 1)
    tril = row >= col
    wm = [jnp.where(tril, w_ref[g], 0.0).astype(MXU_DTYPE) for g in range(N_HEADS)]
    vnx = [vn[:, g * HEAD:(g + 1) * HEAD].astype(MXU_DTYPE) for g in range(N_HEADS)]
    sv = [jnp.dot(wm[g], vnx[g], preferred_element_type=F32) + bsb_ref[g] for g in range(N_HEADS)]
    return zu, vh, rs, wm, vnx, sv, tril


def _gmlp_fwd(p, lng, lnb, ws, bsb, *, name):
    S = p.shape[0]

    def body(u_ref, v_ref, lng_ref, lnb_ref, w_ref, bsb_ref, o_ref):
        zu, _, _, _, _, sv, _ = _gm_forward_parts(u_ref, v_ref, lng_ref, lnb_ref, w_ref, bsb_ref)
        for g in range(N_HEADS):
            o_ref[:, g * HEAD:(g + 1) * HEAD] = (zu[:, g * HEAD:(g + 1) * HEAD] * sv[g]).astype(o_ref.dtype)

    blk = lambda c: pl.BlockSpec((GM_CHUNK, D_TOK), lambda i: (i, c))
    vec = pl.BlockSpec((1, D_TOK), lambda i: (0, 0))
    cube = pl.BlockSpec((N_HEADS, GM_CHUNK, GM_CHUNK), lambda i: (0, 0, 0))
    return pl.pallas_call(
        body, name=name, grid=(S // GM_CHUNK,), in_specs=[blk(0), blk(1), vec, vec, cube, cube],
        out_specs=blk(0), out_shape=jax.ShapeDtypeStruct((S, D_MODEL), BF16),
        compiler_params=_params(("parallel",)))(p, p, lng, lnb, ws, bsb)


def _gmlp_bwd(p, lng, lnb, ws, bsb, dheads, *, name):
    S = p.shape[0]

    def body(u_ref, v_ref, lng_ref, lnb_ref, w_ref, bsb_ref, dt_ref, dp_ref, dw_ref, dbs_ref, dlg_ref, dlb_ref):
        @pl.when(pl.program_id(0) == 0)
        def _():
            dw_ref[...] = jnp.zeros_like(dw_ref)
            dbs_ref[...] = jnp.zeros_like(dbs_ref)
            dlg_ref[...] = jnp.zeros_like(dlg_ref)
            dlb_ref[...] = jnp.zeros_like(dlb_ref)

        zu, vh, rs, wm, vnx, sv, tril = _gm_forward_parts(u_ref, v_ref, lng_ref, lnb_ref, w_ref, bsb_ref)
        dt = dt_ref[...].astype(F32)
        dvn_parts = []
        for g in range(N_HEADS):
            sl = slice(g * HEAD, (g + 1) * HEAD)
            dsv = dt[:, sl] * zu[:, sl]
            dp_ref[:, sl] = (dt[:, sl] * sv[g] * _gelu_grad(u_ref[:, sl])).astype(dp_ref.dtype)
            dsx = dsv.astype(MXU_DTYPE)
            dw = lax.dot_general(dsx, vnx[g], (((1,), (1,)), ((), ())), preferred_element_type=F32)
            dw_ref[g] += jnp.where(tril, dw, 0.0)
            dbs_ref[g] += jnp.sum(dsv, axis=-1, keepdims=True)
            dvn_parts.append(lax.dot_general(wm[g], dsx, (((0,), (0,)), ((), ())), preferred_element_type=F32))
        dvn = jnp.concatenate(dvn_parts, axis=-1)
        dlg_ref[...] += jnp.sum(dvn * vh, axis=0, keepdims=True)
        dlb_ref[...] += jnp.sum(dvn, axis=0, keepdims=True)
        dvh = dvn * lng_ref[...]
        dzv = rs * (dvh - jnp.mean(dvh, axis=-1, keepdims=True) - vh * jnp.mean(dvh * vh, axis=-1, keepdims=True))
        dp_ref[:, D_TOK:] = (dzv * _gelu_grad(v_ref[...])).astype(dp_ref.dtype)

    blk = lambda c: pl.BlockSpec((GM_CHUNK, D_TOK), lambda i: (i, c))
    vec = pl.BlockSpec((1, D_TOK), lambda i: (0, 0))
    cube = pl.BlockSpec((N_HEADS, GM_CHUNK, GM_CHUNK), lambda i: (0, 0, 0))
    col = pl.BlockSpec((N_HEADS, GM_CHUNK, 1), lambda i: (0, 0, 0))
    return pl.pallas_call(
        body, name=name, grid=(S // GM_CHUNK,), in_specs=[blk(0), blk(1), vec, vec, cube, cube, blk(0)],
        out_specs=[pl.BlockSpec((GM_CHUNK, 2 * D_TOK), lambda i: (i, 0)), cube, col, vec, vec],
        out_shape=[jax.ShapeDtypeStruct((S, GM_IN), BF16), jax.ShapeDtypeStruct((N_HEADS, GM_CHUNK, GM_CHUNK), F32),
                   jax.ShapeDtypeStruct((N_HEADS, GM_CHUNK, 1), F32), jax.ShapeDtypeStruct((1, D_TOK), F32),
                   jax.ShapeDtypeStruct((1, D_TOK), F32)],
        compiler_params=_params(("arbitrary",)))(p, p, lng, lnb, ws, bsb, dheads)


def _chunk_tri(n, chunk, upper):
    r = lax.broadcasted_iota(jnp.int32, (n, n), 0)
    c = lax.broadcasted_iota(jnp.int32, (n, n), 1)
    same = (r // chunk) == (c // chunk)
    return jnp.where(same & ((r <= c) if upper else (r >= c)), 1.0, 0.0).astype(F32)


def _hg_gates(fz, lb):
    sg = _sigmoid(fz)
    f = lb + (1.0 - lb) * sg
    kk = (1.0 - lb) * (1.0 - sg)
    return sg, f, jnp.log(f), kk


def _hgrn2_fwd(p, lb, onorm, *, name):
    S = p.shape[0]
    C = HG_SUB
    tb = _tile(S, 256, C)
    nsub = tb // C

    def body(q_ref, fz_ref, v_ref, g_ref, lb_ref, on_ref, tok_ref, o_ref, st_ref, state, b_blk, k_blk, bsc, ksc, vsc):
        @pl.when(pl.program_id(0) == 0)
        def _():
            state[...] = jnp.zeros_like(state)

        _, _, lg, kk = _hg_gates(fz_ref[...], lb_ref[...])
        b_blk[...] = jnp.dot(_chunk_tri(tb, C, False), lg, precision=lax.Precision.HIGHEST,
                             preferred_element_type=F32)
        k_blk[...] = kk
        tt = lax.broadcasted_iota(jnp.int32, (C, HEAD), 0)

        def sub(c, carry):
            rows = pl.ds(pl.multiple_of(c * C, C), C)
            for h in range(N_HEADS):
                cols = slice(h * HEAD, (h + 1) * HEAD)
                qv = q_ref[rows, cols]
                vv = v_ref[rows, cols]
                b = b_blk[rows, cols]
                kk = k_blk[rows, cols]
                st0 = state[h]
                st0x = st0.astype(MXU_DTYPE)
                st_ref[c, h] = st0x.astype(st_ref.dtype)
                inter = lax.dot_general((qv * jnp.exp(b)).astype(MXU_DTYPE), st0x,
                                        (((1,), (1,)), ((), ())), preferred_element_type=F32)
                bsc[h] = b
                ksc[h] = kk
                vsc[h] = vv
                intra = jnp.zeros((C, HEAD), F32)
                for s in range(C):
                    dec = jnp.where(tt >= s, jnp.exp(b - bsc[h, pl.ds(s, 1), :]), 0.0)
                    a_s = jnp.sum(qv * ksc[h, pl.ds(s, 1), :] * dec, axis=-1, keepdims=True)
                    intra = intra + a_s * vsc[h, pl.ds(s, 1), :]
                o_ref[rows, cols] = inter + intra
                b_last = bsc[h, pl.ds(C - 1, 1), :]
                ke = kk * jnp.exp(b_last - b)
                state[h] = st0 * jnp.exp(b_last) + lax.dot_general(
                    vv.astype(MXU_DTYPE), ke.astype(MXU_DTYPE), (((0,), (0,)), ((), ())),
                    preferred_element_type=F32)
            return carry

        lax.fori_loop(0, nsub, sub, 0, unroll=2)

        for h in range(N_HEADS):
            cols = slice(h * HEAD, (h + 1) * HEAD)
            o = o_ref[:, cols]
            gv = g_ref[:, cols]
            n = o * lax.rsqrt(jnp.mean(o * o, axis=-1, keepdims=True) + EPS)
            tok_ref[:, cols] = (n * (gv * _sigmoid(gv)) * on_ref[:, cols]).astype(tok_ref.dtype)

    blk = lambda c: pl.BlockSpec((tb, D_TOK), lambda i, c=c: (i, c))
    vec = pl.BlockSpec((1, D_TOK), lambda i: (0, 0))
    stb = pl.BlockSpec((nsub, N_HEADS, HEAD, HEAD), lambda i: (i, 0, 0, 0))
    return pl.pallas_call(
        body, name=name, grid=(S // tb,), in_specs=[blk(0), blk(1), blk(2), blk(3), vec, vec],
        out_specs=[blk(0), blk(0), stb],
        out_shape=[jax.ShapeDtypeStruct((S, D_MODEL), BF16), jax.ShapeDtypeStruct((S, D_TOK), F32),
                   jax.ShapeDtypeStruct((S // C, N_HEADS, HEAD, HEAD), BF16)],
        scratch_shapes=[pltpu.VMEM((N_HEADS, HEAD, HEAD), F32)] + [pltpu.VMEM((tb, D_TOK), F32)] * 2
        + [pltpu.VMEM((N_HEADS, C, HEAD), F32)] * 3,
        compiler_params=_params(("arbitrary",)))(p, p, p, p, lb, onorm)


def _hgrn2_bwd(p, lb, onorm, o, states, dheads, *, name):
    S = p.shape[0]
    C = HG_SUB
    tb = _tile(S, 256, C)
    nsub = tb // C
    nblk = S // tb

    def body(q_ref, fz_ref, v_ref, g_ref, lb_ref, on_ref, o_ref, st_ref, dt_ref, dp_ref, dlb_ref, don_ref, dstate,
             b_blk, k_blk, do_blk, db_blk, dk_blk, dq_blk, dv_blk, bsc, ksc, vsc, qsc, dosc):
        @pl.when(pl.program_id(0) == 0)
        def _():
            dstate[...] = jnp.zeros_like(dstate)
            dlb_ref[...] = jnp.zeros_like(dlb_ref)
            don_ref[...] = jnp.zeros_like(don_ref)

        for h in range(N_HEADS):
            cols = slice(h * HEAD, (h + 1) * HEAD)
            onv = on_ref[:, cols]
            gv = g_ref[:, cols]
            ov = o_ref[:, cols]
            dt = dt_ref[:, cols].astype(F32)
            sgg = _sigmoid(gv)
            sil = gv * sgg
            rinv = lax.rsqrt(jnp.mean(ov * ov, axis=-1, keepdims=True) + EPS)
            n = ov * rinv
            don_ref[:, cols] += jnp.sum(dt * n * sil, axis=0, keepdims=True)
            dn = dt * sil * onv
            dp_ref[:, 3 * D_TOK + h * HEAD:3 * D_TOK + (h + 1) * HEAD] = (
                dt * n * onv * sgg * (1.0 + gv * (1.0 - sgg))).astype(dp_ref.dtype)
            do_blk[:, cols] = rinv * (dn - n * jnp.mean(dn * n, axis=-1, keepdims=True))
        _, _, lg, kk = _hg_gates(fz_ref[...], lb_ref[...])
        b_blk[...] = jnp.dot(_chunk_tri(tb, C, False), lg, precision=lax.Precision.HIGHEST,
                             preferred_element_type=F32)
        k_blk[...] = kk
        tt = lax.broadcasted_iota(jnp.int32, (C, HEAD), 0)

        def sub(j, carry):
            c = nsub - 1 - j
            rows = pl.ds(pl.multiple_of(c * C, C), C)
            for h in range(N_HEADS):
                cols = slice(h * HEAD, (h + 1) * HEAD)
                qv = q_ref[rows, cols]
                vv = v_ref[rows, cols]
                do = do_blk[rows, cols]
                b = b_blk[rows, cols]
                kk = k_blk[rows, cols]
                bsc[h] = b
                ksc[h] = kk
                vsc[h] = vv
                qsc[h] = qv
                dosc[h] = do
                b_last = bsc[h, pl.ds(C - 1, 1), :]
                eb = jnp.exp(b)
                qe = qv * eb
                ebb = jnp.exp(b_last - b)
                ke = kk * ebb
                e_last = jnp.exp(b_last)
                st0x = st_ref[c, h].astype(MXU_DTYPE)
                st0 = st0x.astype(F32)
                dst1 = dstate[h]
                dst1x = dst1.astype(MXU_DTYPE)
                dox = do.astype(MXU_DTYPE)
                dqe = jnp.dot(dox, st0x, preferred_element_type=F32)
                dke = jnp.dot(vv.astype(MXU_DTYPE), dst1x, preferred_element_type=F32)
                dv = lax.dot_general(ke.astype(MXU_DTYPE), dst1x, (((1,), (1,)), ((), ())),
                                     preferred_element_type=F32)
                db_last = (e_last * jnp.sum(st0 * dst1, axis=0, keepdims=True)
                           + jnp.sum(dke * ke, axis=0, keepdims=True))
                dstate[h] = dst1 * e_last + lax.dot_general(dox, qe.astype(MXU_DTYPE), (((0,), (0,)), ((), ())),
                                                            preferred_element_type=F32)
                dq = dqe * eb
                db = dqe * qe - dke * ke
                dkk = dke * ebb
                for s in range(C):
                    dec = jnp.where(tt >= s, jnp.exp(b - bsc[h, pl.ds(s, 1), :]), 0.0)
                    da_s = jnp.sum(do * vsc[h, pl.ds(s, 1), :], axis=-1, keepdims=True)
                    pq = da_s * ksc[h, pl.ds(s, 1), :] * dec
                    dq = dq + pq
                    db = db + pq * qv
                for t in range(C):
                    q_t = qsc[h, pl.ds(t, 1), :]
                    do_t = dosc[h, pl.ds(t, 1), :]
                    dec = jnp.where(tt <= t, jnp.exp(bsc[h, pl.ds(t, 1), :] - b), 0.0)
                    da_t = jnp.sum(vv * do_t, axis=-1, keepdims=True)
                    pk = da_t * q_t * dec
                    dkk = dkk + pk
                    db = db - pk * kk
                    a_t = jnp.sum(q_t * kk * dec, axis=-1, keepdims=True)
                    dv = dv + a_t * do_t
                db_blk[rows, cols] = db + jnp.where(tt == C - 1, db_last, 0.0)
                dk_blk[rows, cols] = dkk
                dq_blk[rows, cols] = dq
                dv_blk[rows, cols] = dv
            return carry

        lax.fori_loop(0, nsub, sub, 0)

        dlg = jnp.dot(_chunk_tri(tb, C, True), db_blk[...], precision=lax.Precision.HIGHEST,
                      preferred_element_type=F32)
        lbv = lb_ref[...]
        sg, f, _, _ = _hg_gates(fz_ref[...], lbv)
        w = dlg / f - dk_blk[...]
        dp_ref[:, 0:D_TOK] = dq_blk[...].astype(dp_ref.dtype)
        dp_ref[:, 2 * D_TOK:3 * D_TOK] = dv_blk[...].astype(dp_ref.dtype)
        dp_ref[:, D_TOK:2 * D_TOK] = (w * (1.0 - lbv) * sg * (1.0 - sg)).astype(dp_ref.dtype)
        dlb_ref[...] += jnp.sum(w * (1.0 - sg), axis=0, keepdims=True)

    blk = lambda c: pl.BlockSpec((tb, D_TOK), lambda i, c=c: (nblk - 1 - i, c))
    vec = pl.BlockSpec((1, D_TOK), lambda i: (0, 0))
    stb = pl.BlockSpec((nsub, N_HEADS, HEAD, HEAD), lambda i: (nblk - 1 - i, 0, 0, 0))
    small = jax.ShapeDtypeStruct((1, D_TOK), F32)
    return pl.pallas_call(
        body, name=name, grid=(nblk,), in_specs=[blk(0), blk(1), blk(2), blk(3), vec, vec, blk(0), stb, blk(0)],
        out_specs=[pl.BlockSpec((tb, 4 * D_TOK), lambda i: (nblk - 1 - i, 0)), vec, vec],
        out_shape=[jax.ShapeDtypeStruct((S, HG_IN), BF16), small, small],
        scratch_shapes=[pltpu.VMEM((N_HEADS, HEAD, HEAD), F32)] + [pltpu.VMEM((tb, D_TOK), F32)] * 7
        + [pltpu.VMEM((N_HEADS, C, HEAD), F32)] * 5,
        compiler_params=_params(("arbitrary",)))(p, p, p, p, lb, onorm, o, states, dheads)


def _adamw(w, g, m, v, *, name):
    shape = w.shape
    cols = shape[-1]
    w2, g2, m2, v2 = (t.reshape(-1, cols) for t in (w, g, m, v))
    R = w2.shape[0]
    tr = _tile(R, 512, 8)

    def body(w_ref, g_ref, m_ref, v_ref, d_ref, nm_ref, nv_ref):
        gv = g_ref[...]
        nm = ADAM_B1 * m_ref[...] + (1.0 - ADAM_B1) * gv
        nv = ADAM_B2 * v_ref[...] + (1.0 - ADAM_B2) * (gv * gv)
        m_hat = nm / (1.0 - ADAM_B1 ** ADAM_STEP)
        v_hat = nv / (1.0 - ADAM_B2 ** ADAM_STEP)
        d_ref[...] = -ADAM_LR * (m_hat / (jnp.sqrt(v_hat) + ADAM_EPS) + ADAM_WD * w_ref[...])
        nm_ref[...] = nm
        nv_ref[...] = nv

    spec = pl.BlockSpec((tr, cols), lambda i: (i, 0))
    out = jax.ShapeDtypeStruct((R, cols), F32)
    d, nm, nv = pl.pallas_call(body, name=name, grid=(R // tr,), in_specs=[spec] * 4, out_specs=[spec] * 3,
                               out_shape=[out] * 3, compiler_params=_params(("parallel",)))(w2, g2, m2, v2)
    return d.reshape(shape), nm.reshape(shape), nv.reshape(shape)


def _add_received(own, got, *, name):
    R, Cc = own.shape
    n = got.shape[0]
    tr = _tile(R, 256, 16)

    def body(a_ref, b_ref, o_ref):
        acc = a_ref[...].astype(F32)
        for k in range(n):
            acc = acc + b_ref[k].astype(F32)
        o_ref[...] = acc

    return pl.pallas_call(
        body, name=name, grid=(R // tr,),
        in_specs=[pl.BlockSpec((tr, Cc), lambda i: (i, 0)), pl.BlockSpec((n, tr, Cc), lambda i: (0, i, 0))],
        out_specs=pl.BlockSpec((tr, Cc), lambda i: (i, 0)), out_shape=jax.ShapeDtypeStruct((R, Cc), F32),
        compiler_params=_params(("parallel",)))(own, got)


def _place():
    return lax.axis_index("x"), lax.axis_index("y"), lax.axis_index("c")


def _all_gather(x, *, name, in_vmem, reduce_sum=False, with_token=False):
    R, Cc = x.shape
    space = pltpu.VMEM if in_vmem else pl.ANY

    def body(x_ref, out_ref, *scratch):
        if with_token:
            scratch[0][...] = jnp.zeros_like(scratch[0])
            scratch = scratch[1:]
        if reduce_sum:
            gat_ref, send_sems, recv_sems, local_sem = scratch
        else:
            gat_ref = out_ref
            send_sems, recv_sems, local_sem = scratch
        mx, my, mc = _place()
        me, sibling = (mx, my, mc), (mx, my, 1 - mc)
        chips = [(1 - mx, my), (mx, 1 - my), (1 - mx, 1 - my)]

        def rows(px, py, pc):
            return gat_ref.at[pl.ds((4 * px + 2 * py + pc) * R, R), :]

        def copy(k, block, to, src=None):
            return pltpu.make_async_remote_copy(
                src_ref=rows(*block) if src is None else src, dst_ref=rows(*block), send_sem=send_sems.at[k],
                recv_sem=recv_sems.at[k], device_id=to, device_id_type=MESH_ID)

        mine = pltpu.make_async_copy(x_ref, rows(*me), local_sem)
        mine.start()
        first = [copy(0, me, sibling, src=x_ref)]
        first += [copy(1 + j, me, (*chip, mc), src=x_ref) for j, chip in enumerate(chips)]
        for cp in first:
            cp.start()
        passed = [copy(4 + j, (*chip, mc), sibling) for j, chip in enumerate(chips)]
        for j, chip in enumerate(chips):
            copy(1 + j, (*chip, mc), me).wait_recv()
            passed[j].start()
        copy(0, sibling, me).wait_recv()
        for j, chip in enumerate(chips):
            copy(4 + j, (*chip, 1 - mc), me).wait_recv()
        for cp in first + passed:
            cp.wait_send()
        mine.wait()
        if reduce_sum:
            acc = gat_ref[pl.ds(0, R), :]
            for d in range(1, N_DEV):
                acc = acc + gat_ref[pl.ds(d * R, R), :]
            out_ref[...] = acc

    sems = [pltpu.SemaphoreType.DMA((7,)), pltpu.SemaphoreType.DMA((7,)), pltpu.SemaphoreType.DMA]
    if reduce_sum:
        assert in_vmem
        out_shape = jax.ShapeDtypeStruct((R, Cc), x.dtype)
        scratch = [pltpu.VMEM((N_DEV * R, Cc), x.dtype)] + sems
    else:
        out_shape = jax.ShapeDtypeStruct((N_DEV * R, Cc), x.dtype)
        scratch = sems
    out_specs = pl.BlockSpec(memory_space=space)
    if with_token:
        out_shape = (out_shape, jax.ShapeDtypeStruct((8, LANE), F32))
        out_specs = (out_specs, pl.BlockSpec(memory_space=pltpu.VMEM))
    return pl.pallas_call(
        body, name=name, out_shape=out_shape, in_specs=[pl.BlockSpec(memory_space=space)], out_specs=out_specs,
        scratch_shapes=scratch, compiler_params=pltpu.CompilerParams(vmem_limit_bytes=VMEM_LIMIT))(x)


def _peer(k, mx, my, mc):
    bits = k + 1
    return (1 - mx if bits & 4 else mx, 1 - my if bits & 2 else my, 1 - mc if bits & 1 else mc)


HBM_SPEC = pl.BlockSpec(memory_space=pltpu.HBM)
SEM_SPEC = pl.BlockSpec(memory_space=pltpu.SEMAPHORE)
DATAFLOW = pltpu.SideEffectType.DATAFLOW_SIDE_EFFECTING


def _exchange_copies(x_refs, land_refs, send_sems, recv_sems, scatter):
    mx, my, mc = _place()
    me = 4 * mx + 2 * my + mc
    n = len(x_refs)
    copies = []
    for k in range(N_DEV - 1):
        px, py, pc = _peer(k, mx, my, mc)
        for m, (x_ref, land_ref) in enumerate(zip(x_refs, land_refs)):
            rows = land_ref.shape[1] if scatter else x_ref.shape[0]
            if scatter:
                src = x_ref.at[pl.ds(pl.multiple_of((4 * px + 2 * py + pc) * rows, 16), rows), :]
                dst = land_ref.at[k]
            else:
                src = x_ref
                dst = land_ref.at[pl.ds(pl.multiple_of(me * rows, 16), rows), :]
            copies.append(pltpu.make_async_remote_copy(
                src_ref=src, dst_ref=dst, send_sem=send_sems.at[k * n + m], recv_sem=recv_sems.at[k * n + m],
                device_id=(px, py, pc), device_id_type=MESH_ID))
    return copies


def _land_shape(x, scatter):
    return (N_DEV - 1, x.shape[0] // N_DEV, x.shape[1]) if scatter else (N_DEV * x.shape[0], x.shape[1])


def _exchange_start(xs, *, name, scatter):
    n = len(xs)
    lands = [lax.empty(_land_shape(x, scatter), x.dtype) for x in xs]

    def body(*refs):
        x_refs, land_refs = refs[:n], refs[n:2 * n]
        send_sems, recv_sems = refs[2 * n:2 * n + 2]
        token = refs[-1]
        for cp in _exchange_copies(x_refs, land_refs, send_sems, recv_sems, scatter):
            cp.start()
        token[...] = jnp.zeros_like(token)

    sems = pltpu.SemaphoreType.DMA(((N_DEV - 1) * n,))
    out = pl.pallas_call(
        body, name=name,
        out_shape=(sems, sems, *[pltpu.HBM(x.shape, x.dtype) for x in xs],
                   *[pltpu.HBM(l.shape, l.dtype) for l in lands], jax.ShapeDtypeStruct((8, LANE), F32)),
        in_specs=(HBM_SPEC,) * (2 * n),
        out_specs=(SEM_SPEC, SEM_SPEC) + (HBM_SPEC,) * (2 * n) + (pl.BlockSpec(memory_space=pltpu.VMEM),),
        input_output_aliases={i: 2 + i for i in range(2 * n)},
        compiler_params=pltpu.CompilerParams(has_side_effects=DATAFLOW))(
            *[pltpu.with_memory_space_constraint(t, pltpu.HBM) for t in list(xs) + lands])
    return out[0], out[1], list(out[2:2 + n]), list(out[2 + n:2 + 2 * n]), out[-1]


def _exchange_wait(started, after, *, name, scatter):
    send_sems, recv_sems, xs, lands, _ = started
    n = len(xs)

    def body(*refs):
        x_refs, land_refs = refs[:n], refs[n:2 * n]
        send_sems, recv_sems = refs[2 * n:2 * n + 2]
        for cp in _exchange_copies(x_refs, land_refs, send_sems, recv_sems, scatter):
            cp.wait_send()
            cp.wait_recv()

    out = pl.pallas_call(
        body, name=name, out_shape=tuple(pltpu.HBM(t.shape, t.dtype) for t in xs + lands),
        in_specs=(HBM_SPEC,) * (2 * n) + (SEM_SPEC, SEM_SPEC, pl.BlockSpec(memory_space=pl.ANY)),
        out_specs=(HBM_SPEC,) * (2 * n), input_output_aliases={i: i for i in range(2 * n)},
        compiler_params=pltpu.CompilerParams(has_side_effects=DATAFLOW))(*xs, *lands, send_sems, recv_sems, after)
    return list(out[:n]), list(out[n:])


def _gather_start(shards, token, *, name):
    shards = [shards[0] + token[0, 0].astype(shards[0].dtype)] + list(shards[1:])
    return _exchange_start(shards, name=name, scatter=False)


def _gather_finish(started, after, me, *, name):
    xs, lands = _exchange_wait(started, after, name=name, scatter=False)
    return [lax.dynamic_update_slice(land, x, (me * x.shape[0], 0)) for land, x in zip(lands, xs)]


def _reduce_start(grads, *, name):
    return _exchange_start(grads, name=name, scatter=True)


def _reduce_finish(started, after, me, *, name):
    sent, gots = _exchange_wait(started, after, name=name + "_wait", scatter=True)
    out = []
    for m, (g, got) in enumerate(zip(sent, gots)):
        rows = g.shape[0] // N_DEV
        own = lax.dynamic_slice(g, (me * rows, 0), (rows, g.shape[1]))
        out.append(_add_received(own, got, name=f"{name}_add{m}"))
    return out


def _pad_rows(a, mult):
    r = (-a.shape[0]) % mult
    return a if r == 0 else jnp.concatenate([a, jnp.zeros((r,) + a.shape[1:], a.dtype)], axis=0)


def kernel(x, mem, mix_norm, mem_norm, w_mem_kv, w_out, hg_w_in, hg_lb, hg_onorm, gm_w_in, gm_ln_g, gm_ln_b, gm_ws, gm_bs, ffn_norm, w_ffn_in, w_ffn_out, final_norm, loss_target, m_mix_norm, m_mem_norm, m_w_mem_kv, m_w_out, m_hg_w_in, m_hg_lb, m_hg_onorm, m_gm_w_in, m_gm_ln_g, m_gm_ln_b, m_gm_ws, m_gm_bs, m_ffn_norm, m_w_ffn_in, m_w_ffn_out, m_final_norm, v_mix_norm, v_mem_norm, v_w_mem_kv, v_w_out, v_hg_w_in, v_hg_lb, v_hg_onorm, v_gm_w_in, v_gm_ln_g, v_gm_ln_b, v_gm_ws, v_gm_bs, v_ffn_norm, v_w_ffn_in, v_w_ffn_out, v_final_norm):
    mx, my, mc = _place()
    me = 4 * mx + 2 * my + mc
    xs = x[0]
    mems = mem[0]
    tgt = loss_target[0]

    hg_t = hg_w_in[0].T.astype(BF16)
    gm_t = gm_w_in[0].T.astype(BF16)
    fi_t = [w_ffn_in[i].T.astype(BF16) for i in range(2)]
    kv_b = [w_mem_kv[i].astype(BF16) for i in range(2)]
    out_b = [w_out[i].astype(BF16) for i in range(2)]
    fo_b = [w_ffn_out[i].astype(BF16) for i in range(2)]
    ln_local = _pad_rows(jnp.concatenate([gm_ln_g, gm_ln_b], axis=0), 8)
    ln_local = jnp.concatenate([ln_local, jnp.zeros((8, LANE - ln_local.shape[1]), F32)], axis=1)
    ln_all, token = _all_gather(ln_local, name="gather_ln", in_vmem=True, with_token=True)
    ln_all = ln_all.reshape(N_DEV, 8, LANE)
    ln_g = ln_all[:, 0, :D_TOK // N_DEV].reshape(1, D_TOK)
    ln_b = ln_all[:, 1, :D_TOK // N_DEV].reshape(1, D_TOK)
    W_hgT, token = _all_gather(hg_t + token[0, 0].astype(BF16), name="gather_first", in_vmem=False,
                               with_token=True)
    gather_mix = _gather_start(kv_b + out_b, token, name="gather_mix_start")
    gather_fi = [_gather_start([fi_t[0]], gather_mix[4], name="gather_fi0_start")]
    gather_fo = [_gather_start([fo_b[0]], gather_fi[0][4], name="gather_fo0_start")]
    gather_gm = _gather_start([gm_t], gather_fo[0][4], name="gather_gm_start")
    gather_fi.append(_gather_start([fi_t[1]], gather_gm[4], name="gather_fi1_start"))
    gather_fo.append(_gather_start([fo_b[1]], gather_fi[1][4], name="gather_fo1_start"))

    lb_soft = jax.nn.softmax(hg_lb, axis=0)
    lb0 = lb_soft[0:1]
    bsb = jnp.broadcast_to(gm_bs[0][:, :, None], (N_HEADS, GM_CHUNK, GM_CHUNK))
    ws = gm_ws[0]

    W_fiT, W_fo = [], []

    def ffn_fwd(xin, hf, i, next_gain):
        W_fiT.extend(_gather_finish(gather_fi[i], hf, me, name=f"gather_fi{i}_wait"))
        gu, act = _ffn_in(hf, W_fiT[i], name=f"ffn_in{i}")
        W_fo.extend(_gather_finish(gather_fo[i], act, me, name=f"gather_fo{i}_wait"))
        return gu, act, _matmul(act, W_fo[i], res=xin, norm_gain=next_gain, name=f"ffn_out{i}")

    h0 = _rms_fwd(xs, mix_norm[0:1], name="mix_norm0", dep=gather_fo[1][4])
    p0 = _matmul(h0, W_hgT, tb=True, name="hg_in")
    heads0, o0, states = _hgrn2_fwd(p0, lb0, hg_onorm, name="hgrn2_fwd")

    kv0, kv1, wo0, wo1 = _gather_finish(gather_mix, o0, me, name="gather_mix_wait")
    W_kv, W_out = [kv0, kv1], [wo0, wo1]
    mem_n, kv = [], []
    for i in range(2):
        mn = _rms_fwd(mems, mem_norm[i:i + 1], name=f"mem_norm{i}")
        mem_n.append(mn)
        kv.append(_matmul(mn, W_kv[i], name=f"mem_kv{i}"))

    heads0 = _attn_fwd(p0, 4 * D_TOK // D_MEM, kv[0], heads0, name="attn_fwd0")
    x1, hf0 = _matmul(heads0, W_out[0], res=xs, norm_gain=ffn_norm[0:1], name="out_proj0")
    gu0, act0, (x2, h1) = ffn_fwd(x1, hf0, 0, mix_norm[1:2])

    W_gmT, = _gather_finish(gather_gm, h1, me, name="gather_gm_wait")
    p1 = _matmul(h1, W_gmT, tb=True, name="gm_in")
    heads1 = _gmlp_fwd(p1, ln_g, ln_b, ws, bsb, name="gmlp_fwd")
    heads1 = _attn_fwd(p1, 2 * D_TOK // D_MEM, kv[1], heads1, name="attn_fwd1")
    x3, hf1 = _matmul(heads1, W_out[1], res=x2, norm_gain=ffn_norm[1:2], name="out_proj1")
    gu1, act1, x4 = ffn_fwd(x3, hf1, 1, None)

    dx, g_final, loss_part = _final_loss(x4, final_norm.reshape(1, D_MODEL), tgt, name="final_loss")

    def ffn_bwd(dx, xin, hf, gu, act, i, dep):
        dgu = _ffn_out_dx(dx, W_fo[i], gu, dep, name=f"ffn_out_dx{i}")
        g_wfo = _matmul(act, dx, ta=True, out_dtype=BF16, name=f"ffn_out_dw{i}")
        g_wfi_t = _matmul(dgu, hf, ta=True, a_halves=True, out_dtype=BF16, name=f"ffn_in_dw{i}")
        dx, g_norm = _matmul(dgu, W_fiT[i], a_halves=True, res=dx, norm_bwd=(xin, ffn_norm[i:i + 1]),
                             name=f"ffn_in_dx{i}")
        return dx, g_wfi_t, g_wfo, g_norm

    def mem_bwd(dkv, i):
        g_wkv = _matmul(mem_n[i], dkv, ta=True, out_dtype=BF16, name=f"mem_kv_dw{i}")
        dmn = _matmul(dkv, W_kv[i], tb=True, name=f"mem_kv_dx{i}")
        _, g_norm = _rms_bwd(mems, mem_norm[i:i + 1], dmn, jnp.zeros_like(mems), name=f"mem_norm_bwd{i}")
        return g_wkv, g_norm

    dx, g_wfi1_t, g_wfo1, g_ffn1 = ffn_bwd(dx, x3, hf1, gu1, act1, 1, loss_part)
    dheads = _matmul(dx, W_out[1], tb=True, name="out_proj_dx1")
    g_wout1 = _matmul(heads1, dx, ta=True, out_dtype=BF16, name="out_proj_dw1")
    dp, g_ws, g_bs, g_lng, g_lnb = _gmlp_bwd(p1, ln_g, ln_b, ws, bsb, dheads, name="gmlp_bwd")
    dp, dk, dv = _attn_bwd(p1, 2 * D_TOK // D_MEM, kv[1], dheads, dp, name="attn_bwd1")
    g_wkv1, g_mem1 = mem_bwd(jnp.concatenate([dk, dv], axis=1), 1)
    g_wgm_t = _matmul(dp, h1, ta=True, out_dtype=BF16, name="gm_in_dw")
    dx, g_mix1 = _matmul(dp, W_gmT, res=dx, norm_bwd=(x2, mix_norm[1:2]), name="gm_in_dx")
    reduce_l1 = _reduce_start([g_wkv1, g_wout1, g_wgm_t, g_wfi1_t, g_wfo1], name="reduce_l1_start")

    dx, g_wfi0_t, g_wfo0, g_ffn0 = ffn_bwd(dx, x1, hf0, gu0, act0, 0, reduce_l1[4])
    reduce_ffn0 = _reduce_start([g_wfi0_t, g_wfo0], name="reduce_ffn0_start")
    dheads = _matmul(dx, W_out[0], tb=True, name="out_proj_dx0", dep=reduce_ffn0[4])
    g_wout0 = _matmul(heads0, dx, ta=True, out_dtype=BF16, name="out_proj_dw0")
    dp, g_lb0, g_onorm = _hgrn2_bwd(p0, lb0, hg_onorm, o0, states, dheads, name="hgrn2_bwd")
    dp, dk, dv = _attn_bwd(p0, 4 * D_TOK // D_MEM, kv[0], dheads, dp, name="attn_bwd0")
    g_wkv0, g_mem0 = mem_bwd(jnp.concatenate([dk, dv], axis=1), 0)
    g_whg_t = _matmul(dp, h0, ta=True, out_dtype=BF16, name="hg_in_dw")
    reduce_mix0 = _reduce_start([g_wkv0, g_wout0, g_whg_t], name="reduce_mix0_start")
    grad_x, g_mix0 = _matmul(dp, W_hgT, res=dx, norm_bwd=(xs, mix_norm[0:1]), name="hg_in_dx", dep=reduce_mix0[4])

    g_kv1, g_out1, g_gm_t, g_fi1_t, g_fo1 = _reduce_finish(reduce_l1, grad_x, me, name="reduce_l1")
    g_fi0_t, g_fo0 = _reduce_finish(reduce_ffn0, g_kv1, me, name="reduce_ffn0")
    g_kv0, g_out0, g_hg_t = _reduce_finish(reduce_mix0, g_fi0_t, me, name="reduce_mix0")
    g_shards = [jnp.stack([g_kv0, g_kv1]), jnp.stack([g_out0, g_out1]), g_hg_t[None], g_gm_t[None],
                jnp.stack([g_fi0_t, g_fi1_t]), jnp.stack([g_fo0, g_fo1])]
    transposed = (4, 7, 13)

    small = [loss_part, jnp.concatenate([g_mix0, g_mix1], axis=1), jnp.concatenate([g_mem0, g_mem1], axis=1),
             g_lb0, g_onorm, g_lng, g_lnb, g_ws.reshape(1, -1), g_bs.reshape(1, -1),
             jnp.concatenate([g_ffn0, g_ffn1], axis=1), g_final]
    sizes = [t.shape[1] for t in small]
    small_rows = _pad_rows(jnp.concatenate(small, axis=1).reshape(-1, LANE), 8)
    red = _all_gather(small_rows, name="reduce_small", in_vmem=True, reduce_sum=True).reshape(-1)
    pieces, off = [], 0
    for n in sizes:
        pieces.append(red[off:off + n])
        off += n
    loss = pieces[0][0]
    g_mix_norm = pieces[1].reshape(2, D_MODEL)
    g_mem_norm = pieces[2].reshape(2, D_MODEL)
    g_hg_lb = pieces[3][None, :] * lb0 * (jnp.eye(3, dtype=F32)[:, 0:1] - lb_soft)
    g_hg_onorm = pieces[4].reshape(1, D_TOK)
    width = D_TOK // N_DEV
    g_gm_ln_g = lax.dynamic_slice(pieces[5], (me * width,), (width,)).reshape(1, width)
    g_gm_ln_b = lax.dynamic_slice(pieces[6], (me * width,), (width,)).reshape(1, width)
    g_gm_ws = pieces[7].reshape(gm_ws.shape)
    g_gm_bs = pieces[8].reshape(gm_bs.shape)
    g_ffn_norm = pieces[9].reshape(2, D_MODEL)
    g_final_norm = pieces[10]

    grads = [g_mix_norm, g_mem_norm, g_shards[0], g_shards[1], g_shards[2], g_hg_lb, g_hg_onorm, g_shards[3],
             g_gm_ln_g, g_gm_ln_b, g_gm_ws, g_gm_bs, g_ffn_norm, g_shards[4], g_shards[5], g_final_norm]
    weights = [mix_norm, mem_norm, w_mem_kv, w_out, hg_w_in, hg_lb, hg_onorm, gm_w_in, gm_ln_g, gm_ln_b, gm_ws, gm_bs,
               ffn_norm, w_ffn_in, w_ffn_out, final_norm]
    ms = [m_mix_norm, m_mem_norm, m_w_mem_kv, m_w_out, m_hg_w_in, m_hg_lb, m_hg_onorm, m_gm_w_in, m_gm_ln_g,
          m_gm_ln_b, m_gm_ws, m_gm_bs, m_ffn_norm, m_w_ffn_in, m_w_ffn_out, m_final_norm]
    vs = [v_mix_norm, v_mem_norm, v_w_mem_kv, v_w_out, v_hg_w_in, v_hg_lb, v_hg_onorm, v_gm_w_in, v_gm_ln_g,
          v_gm_ln_b, v_gm_ws, v_gm_bs, v_ffn_norm, v_w_ffn_in, v_w_ffn_out, v_final_norm]
    deltas, new_m, new_v = [], [], []
    for n, (w, g, m, v) in enumerate(zip(weights, grads, ms, vs)):
        if w.ndim == 1:
            d, nm, nv = _adamw(w[None], g.reshape(1, -1), m[None], v[None], name=f"adamw{n}")
            d, nm, nv = d[0], nm[0], nv[0]
        elif n in transposed:
            flip = lambda t: jnp.swapaxes(t, 1, 2)
            d, nm, nv = (flip(t) for t in _adamw(flip(w), g, flip(m), flip(v), name=f"adamw{n}"))
            grads[n] = flip(g)
        else:
            d, nm, nv = _adamw(w, g.reshape(w.shape), m, v, name=f"adamw{n}")
        deltas.append(d)
        new_m.append(nm)
        new_v.append(nv)
    grads = [g.reshape(w.shape) for g, w in zip(grads, weights)]
    return (loss, grad_x[None], *grads, *deltas, *new_m, *new_v)
```

```python
import functools

import jax
import jax.numpy as jnp
from jax import lax
from jax.experimental import pallas as pl
from jax.experimental.pallas import tpu as pltpu

F32 = jnp.float32
BF16 = jnp.bfloat16
MXU_DTYPE = jnp.bfloat16
MESH_ID = pl.DeviceIdType.MESH

N_DEV = 8
EPS = 1e-6
D_MODEL = 1024
D_TOK = 768
D_MEM = 256
N_HEADS = 6
HEAD = 128
MEM_HEADS = 4
MEM_HDIM = 64
GM_CHUNK = 128
D_FF = 2816
HG_SUB = 16
HG_IN = 4 * D_TOK + D_MEM
GM_IN = 2 * D_TOK + D_MEM
LANE = 128
MXU_COLS = 256

ADAM_LR = 0.001
ADAM_B1 = 0.9
ADAM_B2 = 0.999
ADAM_EPS = 1e-08
ADAM_WD = 0.01
ADAM_STEP = 10

VMEM_LIMIT = 48 * 2 ** 20
VMEM_LIMIT_WIDE = 58 * 2 ** 20


def _params(sem=None, limit=VMEM_LIMIT):
    return pltpu.CompilerParams(dimension_semantics=sem, vmem_limit_bytes=limit)


def _tile(n, cap, q=LANE):
    if n <= cap:
        return n
    best = None
    for t in range(q, cap + 1, q):
        if n % t == 0:
            best = t
    assert best is not None, (n, cap, q)
    return best


def _sigmoid(x):
    return 1.0 / (1.0 + jnp.exp(-x))


def _gelu(x, with_grad=False):
    cdf = 0.5 * (1.0 + lax.erf(x * 0.7071067811865476))
    if not with_grad:
        return x * cdf
    return x * cdf, cdf + x * jnp.exp(-0.5 * x * x) * 0.3989422804014327


def _matmul(a, b, *, name, ta=False, tb=False, res=None, out_dtype=F32, a_halves=False, b_halves=False, dep=None,
            norm_gain=None, norm_bwd=None):
    if a_halves and ta:
        K, M = a.shape[1], 2 * a.shape[2]
    elif a_halves:
        M, K = a.shape[1], 2 * a.shape[2]
    else:
        K, M = a.shape if ta else a.shape[::-1]
    if b_halves:
        assert not tb and b.shape[1] == K
        N = 2 * b.shape[2]
    else:
        N = b.shape[0] if tb else b.shape[1]
        assert (b.shape[1] if tb else b.shape[0]) == K
    tm = _tile(M // 2 if (a_halves and ta) else M, 1664 if ta else 1024)
    tn = _tile(N // 2 if b_halves else N, 1792)
    tk = _tile(K // 2 if (a_halves and not ta) else K, 1024 if ta else 1664)
    nk = K // tk
    dims = (((0 if ta else 1,), (1 if tb else 0,)), ((), ()))

    n_in = 2 + (res is not None) + (norm_gain is not None) + 2 * (norm_bwd is not None) + (dep is not None)
    if norm_gain is not None or norm_bwd is not None:
        assert tn == N, "the fused norm needs whole rows"
        assert norm_gain is None or norm_bwd is None
        assert norm_bwd is None or (res is not None and nk > 1 and tm % LANE == 0)

    def body(*refs):
        a_ref, b_ref = refs[:2]
        r_ref = refs[2] if res is not None else None
        g_ref = refs[2 + (res is not None)] if (norm_gain is not None or norm_bwd is not None) else None
        x_ref = refs[3 + (res is not None)] if norm_bwd is not None else None
        o_ref = refs[n_in]
        h_ref = refs[n_in + 1] if (norm_gain is not None or norm_bwd is not None) else None
        acc = None if nk == 1 else refs[-1]
        k = pl.program_id(2)

        def product():
            return lax.dot_general(a_ref[...].astype(MXU_DTYPE), b_ref[...].astype(MXU_DTYPE), dims,
                                   preferred_element_type=F32)

        def finish(r):
            if norm_bwd is not None:
                @pl.when(pl.program_id(0) == 0)
                def _():
                    h_ref[...] = jnp.zeros_like(h_ref)

                acc[...] = r
                gv = g_ref[...]

                def strip(s, dg):
                    rows = pl.ds(pl.multiple_of(s * LANE, LANE), LANE)
                    rv = acc[rows, :]
                    xv = x_ref[rows, :]
                    scale = lax.rsqrt(jnp.mean(xv * xv, axis=-1, keepdims=True) + EPS)
                    xh = xv * scale
                    u = rv * gv
                    o_ref[rows, :] = r_ref[rows, :] + scale * (u - xh * jnp.mean(u * xh, axis=-1, keepdims=True))
                    return dg + jnp.sum(rv * xh, axis=0, keepdims=True)

                h_ref[...] += lax.fori_loop(0, tm // LANE, strip, jnp.zeros((1, N), F32))
                return
            if res is not None:
                r = r + r_ref[...].astype(F32)
            o_ref[...] = r.astype(out_dtype)
            if norm_gain is not None:
                scale = lax.rsqrt(jnp.mean(r * r, axis=-1, keepdims=True) + EPS)
                h_ref[...] = (r * scale * g_ref[...]).astype(h_ref.dtype)

        if nk == 1:
            finish(product())
            return

        @pl.when(k == 0)
        def _():
            acc[...] = product()

        @pl.when((k > 0) & (k < nk - 1))
        def _():
            acc[...] += product()

        @pl.when(k == nk - 1)
        def _():
            finish(acc[...] + product())

    if a_halves and ta:
        mh = M // 2 // tm
        a_spec = pl.BlockSpec((None, tk, tm), lambda i, j, k: (i // mh, k, i % mh))
    elif a_halves:
        kh = nk // 2
        a_spec = pl.BlockSpec((None, tm, tk), lambda i, j, k: (k // kh, i, k % kh))
    elif ta:
        a_spec = pl.BlockSpec((tk, tm), lambda i, j, k: (k, i))
    else:
        a_spec = pl.BlockSpec((tm, tk), lambda i, j, k: (i, k))
    if b_halves:
        nh = N // 2 // tn
        b_spec = pl.BlockSpec((None, tk, tn), lambda i, j, k: (j // nh, k, j % nh))
    elif tb:
        b_spec = pl.BlockSpec((tn, tk), lambda i, j, k: (j, k))
    else:
        b_spec = pl.BlockSpec((tk, tn), lambda i, j, k: (k, j))
    o_spec = pl.BlockSpec((tm, tn), lambda i, j, k: (i, j))
    in_specs = [a_spec, b_spec] + ([o_spec] if res is not None else [])
    args = (a, b) + ((res,) if res is not None else ())
    out_specs, out_shape = o_spec, jax.ShapeDtypeStruct((M, N), out_dtype)
    vec = pl.BlockSpec((1, N), lambda i, j, k: (0, 0))
    sem = ("parallel", "parallel", "arbitrary")
    if norm_gain is not None:
        in_specs.append(vec)
        args += (norm_gain,)
        out_specs, out_shape = [o_spec, o_spec], [out_shape, jax.ShapeDtypeStruct((M, N), BF16)]
    if norm_bwd is not None:
        x_in, gain = norm_bwd
        in_specs += [vec, o_spec]
        args += (gain, x_in)
        out_specs, out_shape = [o_spec, vec], [out_shape, jax.ShapeDtypeStruct((1, N), F32)]
        sem = ("arbitrary", "arbitrary", "arbitrary")
    if dep is not None:
        in_specs.append(pl.BlockSpec(memory_space=pl.ANY))
        args += (dep,)
    return pl.pallas_call(
        body, name=name, grid=(M // tm, N // tn, nk), in_specs=in_specs, out_specs=out_specs, out_shape=out_shape,
        scratch_shapes=[] if nk == 1 else [pltpu.VMEM((tm, tn), F32)],
        compiler_params=_params(sem, VMEM_LIMIT_WIDE if norm_bwd is not None else VMEM_LIMIT))(*args)


def _ffn_in(hf, wt, *, name):
    S, K = hf.shape
    tm = _tile(S, 512)
    tn = _tile(D_FF, 1408)
    nh = D_FF // tn
    nt = (((1,), (1,)), ((), ()))

    def body(a_ref, bg_ref, bu_ref, gu_ref, act_ref):
        av = a_ref[...].astype(MXU_DTYPE)
        for c0 in range(0, tn, MXU_COLS):
            cs = slice(c0, min(c0 + MXU_COLS, tn))
            gate = lax.dot_general(av, bg_ref[cs, :].astype(MXU_DTYPE), nt, preferred_element_type=F32)
            up = lax.dot_general(av, bu_ref[cs, :].astype(MXU_DTYPE), nt, preferred_element_type=F32)
            gu_ref[0, :, cs] = gate.astype(gu_ref.dtype)
            gu_ref[1, :, cs] = up.astype(gu_ref.dtype)
            act_ref[:, cs] = (gate * _sigmoid(gate) * up).astype(act_ref.dtype)

    return pl.pallas_call(
        body, name=name, grid=(nh, S // tm),
        in_specs=[pl.BlockSpec((tm, K), lambda j, i: (i, 0)), pl.BlockSpec((tn, K), lambda j, i: (j, 0)),
                  pl.BlockSpec((tn, K), lambda j, i: (j + nh, 0))],
        out_specs=[pl.BlockSpec((2, tm, tn), lambda j, i: (0, i, j)), pl.BlockSpec((tm, tn), lambda j, i: (i, j))],
        out_shape=[jax.ShapeDtypeStruct((2, S, D_FF), BF16), jax.ShapeDtypeStruct((S, D_FF), BF16)],
        compiler_params=_params(("parallel", "parallel")))(hf, wt, wt)


def _ffn_out_dx(dx, w, gu, dep, *, name):
    S, K = dx.shape
    tm = _tile(S, 1024)
    tn = _tile(D_FF, 1408)

    def body(a_ref, b_ref, gu_ref, dep_ref, o_ref):
        del dep_ref
        av = a_ref[...].astype(MXU_DTYPE)
        for c0 in range(0, tn, MXU_COLS):
            cs = slice(c0, min(c0 + MXU_COLS, tn))
            da = lax.dot_general(av, b_ref[cs, :].astype(MXU_DTYPE), (((1,), (1,)), ((), ())),
                                 preferred_element_type=F32)
            gate = gu_ref[0, :, cs].astype(F32)
            up = gu_ref[1, :, cs].astype(F32)
            sg = _sigmoid(gate)
            o_ref[0, :, cs] = (da * up * sg * (1.0 + gate * (1.0 - sg))).astype(o_ref.dtype)
            o_ref[1, :, cs] = (da * gate * sg).astype(o_ref.dtype)

    halves = pl.BlockSpec((2, tm, tn), lambda i, j: (0, i, j))
    return pl.pallas_call(
        body, name=name, grid=(S // tm, D_FF // tn),
        in_specs=[pl.BlockSpec((tm, K), lambda i, j: (i, 0)), pl.BlockSpec((tn, K), lambda i, j: (j, 0)), halves,
                  pl.BlockSpec(memory_space=pl.ANY)],
        out_specs=halves, out_shape=jax.ShapeDtypeStruct((2, S, D_FF), BF16),
        compiler_params=_params(("parallel", "parallel")))(dx, w, gu, dep)


def _rms_fwd(x, g, *, name, dep=None):
    R, Dm = x.shape
    tr = _tile(R, 512, 8)

    def body(x_ref, g_ref, *rest):
        o_ref = rest[-1]
        xv = x_ref[...]
        r = lax.rsqrt(jnp.mean(xv * xv, axis=-1, keepdims=True) + EPS)
        o_ref[...] = (xv * r * g_ref[...]).astype(o_ref.dtype)

    in_specs = [pl.BlockSpec((tr, Dm), lambda i: (i, 0)), pl.BlockSpec((1, Dm), lambda i: (0, 0))]
    args = (x, g)
    if dep is not None:
        in_specs.append(pl.BlockSpec(memory_space=pl.ANY))
        args += (dep,)
    return pl.pallas_call(
        body, name=name, grid=(R // tr,), in_specs=in_specs,
        out_specs=pl.BlockSpec((tr, Dm), lambda i: (i, 0)), out_shape=jax.ShapeDtypeStruct((R, Dm), BF16),
        compiler_params=_params(("parallel",)))(*args)


def _rms_bwd(x, g, dh, dres, *, name):
    R, Dm = x.shape
    tr = _tile(R, 256, 8)

    def body(x_ref, g_ref, dh_ref, dres_ref, dx_ref, dg_ref):
        @pl.when(pl.program_id(0) == 0)
        def _():
            dg_ref[...] = jnp.zeros_like(dg_ref)

        xv = x_ref[...]
        r = lax.rsqrt(jnp.mean(xv * xv, axis=-1, keepdims=True) + EPS)
        xh = xv * r
        dhv = dh_ref[...].astype(F32)
        dg_ref[...] += jnp.sum(dhv * xh, axis=0, keepdims=True)
        u = dhv * g_ref[...]
        dx = r * (u - xh * jnp.mean(u * xh, axis=-1, keepdims=True))
        dx_ref[...] = dres_ref[...] + dx

    row = pl.BlockSpec((tr, Dm), lambda i: (i, 0))
    vec = pl.BlockSpec((1, Dm), lambda i: (0, 0))
    return pl.pallas_call(
        body, name=name, grid=(R // tr,), in_specs=[row, vec, row, row], out_specs=[row, vec],
        out_shape=[jax.ShapeDtypeStruct((R, Dm), F32), jax.ShapeDtypeStruct((1, Dm), F32)],
        compiler_params=_params(("arbitrary",)))(x, g, dh, dres)


def _final_loss(x, g, tgt, *, name):
    R, Dm = x.shape
    tr = _tile(R, 256, 8)

    def body(x_ref, g_ref, t_ref, dx_ref, dg_ref, loss_ref):
        @pl.when(pl.program_id(0) == 0)
        def _():
            dg_ref[...] = jnp.zeros_like(dg_ref)
            loss_ref[...] = jnp.zeros_like(loss_ref)

        xv = x_ref[...]
        r = lax.rsqrt(jnp.mean(xv * xv, axis=-1, keepdims=True) + EPS)
        xh = xv * r
        gv = g_ref[...]
        err = xh * gv - t_ref[...]
        part = 0.5 * jnp.sum(jnp.mean(err * err, axis=-1, keepdims=True), axis=0, keepdims=True)
        loss_ref[...] += jnp.broadcast_to(part, loss_ref.shape)
        dy = err * (1.0 / Dm)
        dg_ref[...] += jnp.sum(dy * xh, axis=0, keepdims=True)
        u = dy * gv
        dx_ref[...] = r * (u - xh * jnp.mean(u * xh, axis=-1, keepdims=True))

    row = pl.BlockSpec((tr, Dm), lambda i: (i, 0))
    vec = pl.BlockSpec((1, Dm), lambda i: (0, 0))
    one = pl.BlockSpec((1, LANE), lambda i: (0, 0))
    return pl.pallas_call(
        body, name=name, grid=(R // tr,), in_specs=[row, vec, row], out_specs=[row, vec, one],
        out_shape=[jax.ShapeDtypeStruct((R, Dm), F32), jax.ShapeDtypeStruct((1, Dm), F32),
                   jax.ShapeDtypeStruct((1, LANE), F32)],
        compiler_params=_params(("arbitrary",)))(x, g, tgt)


def _head_mask(h):
    lane = lax.broadcasted_iota(jnp.int32, (1, D_MEM), 1)
    return (lane >= h * MEM_HDIM) & (lane < (h + 1) * MEM_HDIM)


def _attn_probs(qv, k_mx, mask):
    s = lax.dot_general(jnp.where(mask, qv, 0.0).astype(MXU_DTYPE), k_mx, (((1,), (1,)), ((), ())),
                        preferred_element_type=F32) * (MEM_HDIM ** -0.5)
    e = jnp.exp(s - jnp.max(s, axis=-1, keepdims=True))
    return e / jnp.sum(e, axis=-1, keepdims=True)


def _attn_fwd(p, qcol, kv, heads, *, name):
    S = p.shape[0]
    M = kv.shape[0]
    ts = _tile(S, 512, 8)

    def body(q_ref, k_ref, v_ref, heads_in, o_ref):
        del heads_in
        qv = q_ref[...]
        kx = k_ref[...].astype(MXU_DTYPE)
        vv = v_ref[...]
        out = jnp.zeros((ts, D_MEM), F32)
        for h in range(MEM_HEADS):
            mask = _head_mask(h)
            pr = _attn_probs(qv, kx, mask)
            out = out + jnp.dot(pr.astype(MXU_DTYPE), jnp.where(mask, vv, 0.0).astype(MXU_DTYPE),
                                preferred_element_type=F32)
        o_ref[...] = out.astype(o_ref.dtype)

    return pl.pallas_call(
        body, name=name, grid=(S // ts,),
        in_specs=[pl.BlockSpec((ts, D_MEM), lambda i: (i, qcol)), pl.BlockSpec((M, D_MEM), lambda i: (0, 0)),
                  pl.BlockSpec((M, D_MEM), lambda i: (0, 1)), pl.BlockSpec(memory_space=pl.ANY)],
        out_specs=pl.BlockSpec((ts, D_MEM), lambda i: (i, D_TOK // D_MEM)),
        out_shape=jax.ShapeDtypeStruct(heads.shape, heads.dtype), input_output_aliases={3: 0},
        compiler_params=_params(("parallel",)))(p, kv, kv, heads)


def _attn_bwd(p, qcol, kv, dheads, dp, *, name):
    S = p.shape[0]
    M = kv.shape[0]
    ts = _tile(S, 512, 8)
    scale = MEM_HDIM ** -0.5

    def body(q_ref, k_ref, v_ref, do_ref, dp_in, dq_ref, dk_ref, dv_ref):
        del dp_in

        @pl.when(pl.program_id(0) == 0)
        def _():
            dk_ref[...] = jnp.zeros_like(dk_ref)
            dv_ref[...] = jnp.zeros_like(dv_ref)

        qv = q_ref[...]
        kv_ = k_ref[...]
        kx = kv_.astype(MXU_DTYPE)
        vv = v_ref[...]
        dox = do_ref[...].astype(MXU_DTYPE)
        qx = qv.astype(MXU_DTYPE)
        dq = jnp.zeros((ts, D_MEM), F32)
        for h in range(MEM_HEADS):
            mask = _head_mask(h)
            pr = _attn_probs(qv, kx, mask)
            vh = jnp.where(mask, vv, 0.0).astype(MXU_DTYPE)
            dpr = lax.dot_general(dox, vh, (((1,), (1,)), ((), ())), preferred_element_type=F32)
            ds = (pr * (dpr - jnp.sum(dpr * pr, axis=-1, keepdims=True)) * scale).astype(MXU_DTYPE)
            dq = dq + jnp.dot(ds, jnp.where(mask, kv_, 0.0).astype(MXU_DTYPE), preferred_element_type=F32)
            dk_h = lax.dot_general(ds, qx, (((0,), (0,)), ((), ())), preferred_element_type=F32)
            dv_h = lax.dot_general(pr.astype(MXU_DTYPE), dox, (((0,), (0,)), ((), ())), preferred_element_type=F32)
            dk_ref[...] += jnp.where(mask, dk_h, 0.0)
            dv_ref[...] += jnp.where(mask, dv_h, 0.0)
        dq_ref[...] = dq.astype(dq_ref.dtype)

    return pl.pallas_call(
        body, name=name, grid=(S // ts,),
        in_specs=[pl.BlockSpec((ts, D_MEM), lambda i: (i, qcol)), pl.BlockSpec((M, D_MEM), lambda i: (0, 0)),
                  pl.BlockSpec((M, D_MEM), lambda i: (0, 1)),
                  pl.BlockSpec((ts, D_MEM), lambda i: (i, D_TOK // D_MEM)), pl.BlockSpec(memory_space=pl.ANY)],
        out_specs=[pl.BlockSpec((ts, D_MEM), lambda i: (i, qcol)), pl.BlockSpec((M, D_MEM), lambda i: (0, 0)),
                   pl.BlockSpec((M, D_MEM), lambda i: (0, 0))],
        out_shape=[jax.ShapeDtypeStruct(dp.shape, dp.dtype), jax.ShapeDtypeStruct((M, D_MEM), F32),
                   jax.ShapeDtypeStruct((M, D_MEM), F32)],
        input_output_aliases={4: 0}, compiler_params=_params(("arbitrary",)))(p, kv, kv, dheads, dp)


def _gm_forward_parts(u_ref, v_ref, lng_ref, lnb_ref, w_ref, bsb_ref, with_grad=False):
    if with_grad:
        (zu, du_gelu), (zv, dv_gelu) = _gelu(u_ref[...], True), _gelu(v_ref[...], True)
    else:
        zu, zv, du_gelu, dv_gelu = _gelu(u_ref[...]), _gelu(v_ref[...]), None, None
    mu = jnp.mean(zv, axis=-1, keepdims=True)
    cen = zv - mu
    rs = lax.rsqrt(jnp.mean(cen * cen, axis=-1, keepdims=True) + EPS)
    vh = cen * rs
    vn = vh * lng_ref[...] + lnb_ref[...]
    row = lax.broadcasted_iota(jnp.int32, (GM_CHUNK, GM_CHUNK), 0)
    col = lax.broadcasted_iota(jnp.int32, (GM_CHUNK, GM_CHUNK), 1)
    tril = row >= col
    wm = [jnp.where(tril, w_ref[g], 0.0).astype(MXU_DTYPE) for g in range(N_HEADS)]
    vnx = [vn[:, g * HEAD:(g + 1) * HEAD].astype(MXU_DTYPE) for g in range(N_HEADS)]
    sv = [jnp.dot(wm[g], vnx[g], preferred_element_type=F32) + bsb_ref[g] for g in range(N_HEADS)]
    return zu, vh, rs, wm, vnx, sv, tril, du_gelu, dv_gelu


def _gmlp_fwd(p, lng, lnb, ws, bsb, *, name):
    S = p.shape[0]

    def body(u_ref, v_ref, lng_ref, lnb_ref, w_ref, bsb_ref, o_ref):
        zu, _, _, _, _, sv, _, _, _ = _gm_forward_parts(u_ref, v_ref, lng_ref, lnb_ref, w_ref, bsb_ref)
        for g in range(N_HEADS):
            o_ref[:, g * HEAD:(g + 1) * HEAD] = (zu[:, g * HEAD:(g + 1) * HEAD] * sv[g]).astype(o_ref.dtype)

    blk = lambda c: pl.BlockSpec((GM_CHUNK, D_TOK), lambda i: (i, c))
    vec = pl.BlockSpec((1, D_TOK), lambda i: (0, 0))
    cube = pl.BlockSpec((N_HEADS, GM_CHUNK, GM_CHUNK), lambda i: (0, 0, 0))
    return pl.pallas_call(
        body, name=name, grid=(S // GM_CHUNK,), in_specs=[blk(0), blk(1), vec, vec, cube, cube],
        out_specs=blk(0), out_shape=jax.ShapeDtypeStruct((S, D_MODEL), BF16),
        compiler_params=_params(("parallel",)))(p, p, lng, lnb, ws, bsb)


def _gmlp_bwd(p, lng, lnb, ws, bsb, dheads, *, name):
    S = p.shape[0]

    def body(u_ref, v_ref, lng_ref, lnb_ref, w_ref, bsb_ref, dt_ref, dp_ref, dw_ref, dbs_ref, dlg_ref, dlb_ref):
        @pl.when(pl.program_id(0) == 0)
        def _():
            dw_ref[...] = jnp.zeros_like(dw_ref)
            dbs_ref[...] = jnp.zeros_like(dbs_ref)
            dlg_ref[...] = jnp.zeros_like(dlg_ref)
            dlb_ref[...] = jnp.zeros_like(dlb_ref)

        zu, vh, rs, wm, vnx, sv, tril, du_gelu, dv_gelu = _gm_forward_parts(
            u_ref, v_ref, lng_ref, lnb_ref, w_ref, bsb_ref, with_grad=True)
        dt = dt_ref[...].astype(F32)
        dvn_parts = []
        for g in range(N_HEADS):
            sl = slice(g * HEAD, (g + 1) * HEAD)
            dsv = dt[:, sl] * zu[:, sl]
            dp_ref[:, sl] = (dt[:, sl] * sv[g] * du_gelu[:, sl]).astype(dp_ref.dtype)
            dsx = dsv.astype(MXU_DTYPE)
            dw = lax.dot_general(dsx, vnx[g], (((1,), (1,)), ((), ())), preferred_element_type=F32)
            dw_ref[g] += jnp.where(tril, dw, 0.0)
            dbs_ref[g] += jnp.sum(dsv, axis=-1, keepdims=True)
            dvn_parts.append(lax.dot_general(wm[g], dsx, (((0,), (0,)), ((), ())), preferred_element_type=F32))
        dvn = jnp.concatenate(dvn_parts, axis=-1)
        dlg_ref[...] += jnp.sum(dvn * vh, axis=0, keepdims=True)
        dlb_ref[...] += jnp.sum(dvn, axis=0, keepdims=True)
        dvh = dvn * lng_ref[...]
        dzv = rs * (dvh - jnp.mean(dvh, axis=-1, keepdims=True) - vh * jnp.mean(dvh * vh, axis=-1, keepdims=True))
        dp_ref[:, D_TOK:] = (dzv * dv_gelu).astype(dp_ref.dtype)

    blk = lambda c: pl.BlockSpec((GM_CHUNK, D_TOK), lambda i: (i, c))
    vec = pl.BlockSpec((1, D_TOK), lambda i: (0, 0))
    cube = pl.BlockSpec((N_HEADS, GM_CHUNK, GM_CHUNK), lambda i: (0, 0, 0))
    col = pl.BlockSpec((N_HEADS, GM_CHUNK, 1), lambda i: (0, 0, 0))
    return pl.pallas_call(
        body, name=name, grid=(S // GM_CHUNK,), in_specs=[blk(0), blk(1), vec, vec, cube, cube, blk(0)],
        out_specs=[pl.BlockSpec((GM_CHUNK, 2 * D_TOK), lambda i: (i, 0)), cube, col, vec, vec],
        out_shape=[jax.ShapeDtypeStruct((S, GM_IN), BF16), jax.ShapeDtypeStruct((N_HEADS, GM_CHUNK, GM_CHUNK), F32),
                   jax.ShapeDtypeStruct((N_HEADS, GM_CHUNK, 1), F32), jax.ShapeDtypeStruct((1, D_TOK), F32),
                   jax.ShapeDtypeStruct((1, D_TOK), F32)],
        compiler_params=_params(("arbitrary",)))(p, p, lng, lnb, ws, bsb, dheads)


def _chunk_tri(n, chunk, upper):
    r = lax.broadcasted_iota(jnp.int32, (n, n), 0)
    c = lax.broadcasted_iota(jnp.int32, (n, n), 1)
    same = (r // chunk) == (c // chunk)
    return jnp.where(same & ((r <= c) if upper else (r >= c)), 1.0, 0.0).astype(F32)


def _hg_gates(fz, lb):
    sg = _sigmoid(fz)
    f = lb + (1.0 - lb) * sg
    kk = (1.0 - lb) * (1.0 - sg)
    return sg, f, jnp.log(f), kk


def _hgrn2_fwd(p, lb, onorm, *, name):
    S = p.shape[0]
    C = HG_SUB
    tb = _tile(S, 256, C)
    nsub = tb // C

    def body(q_ref, fz_ref, v_ref, g_ref, lb_ref, on_ref, tok_ref, o_ref, st_ref, state, b_blk, k_blk, bsc, ksc, vsc):
        @pl.when(pl.program_id(0) == 0)
        def _():
            state[...] = jnp.zeros_like(state)

        _, _, lg, kk = _hg_gates(fz_ref[...], lb_ref[...])
        b_blk[...] = jnp.dot(_chunk_tri(tb, C, False), lg, precision=lax.Precision.HIGHEST,
                             preferred_element_type=F32)
        k_blk[...] = kk
        tt = lax.broadcasted_iota(jnp.int32, (C, HEAD), 0)

        def sub(c, carry):
            rows = pl.ds(pl.multiple_of(c * C, C), C)
            for h in range(N_HEADS):
                cols = slice(h * HEAD, (h + 1) * HEAD)
                qv = q_ref[rows, cols]
                vv = v_ref[rows, cols]
                b = b_blk[rows, cols]
                kk = k_blk[rows, cols]
                st0 = state[h]
                st0x = st0.astype(MXU_DTYPE)
                st_ref[c, h] = st0x.astype(st_ref.dtype)
                inter = lax.dot_general((qv * jnp.exp(b)).astype(MXU_DTYPE), st0x,
                                        (((1,), (1,)), ((), ())), preferred_element_type=F32)
                bsc[h] = b
                ksc[h] = kk
                vsc[h] = vv
                intra = jnp.zeros((C, HEAD), F32)
                for s in range(C):
                    dec = jnp.where(tt >= s, jnp.exp(b - bsc[h, pl.ds(s, 1), :]), 0.0)
                    a_s = jnp.sum(qv * ksc[h, pl.ds(s, 1), :] * dec, axis=-1, keepdims=True)
                    intra = intra + a_s * vsc[h, pl.ds(s, 1), :]
                o_ref[rows, cols] = inter + intra
                b_last = bsc[h, pl.ds(C - 1, 1), :]
                ke = kk * jnp.exp(b_last - b)
                state[h] = st0 * jnp.exp(b_last) + lax.dot_general(
                    vv.astype(MXU_DTYPE), ke.astype(MXU_DTYPE), (((0,), (0,)), ((), ())),
                    preferred_element_type=F32)
            return carry

        lax.fori_loop(0, nsub, sub, 0, unroll=2)

        for h in range(N_HEADS):
            cols = slice(h * HEAD, (h + 1) * HEAD)
            o = o_ref[:, cols]
            gv = g_ref[:, cols]
            n = o * lax.rsqrt(jnp.mean(o * o, axis=-1, keepdims=True) + EPS)
            tok_ref[:, cols] = (n * (gv * _sigmoid(gv)) * on_ref[:, cols]).astype(tok_ref.dtype)

    blk = lambda c: pl.BlockSpec((tb, D_TOK), lambda i, c=c: (i, c))
    vec = pl.BlockSpec((1, D_TOK), lambda i: (0, 0))
    stb = pl.BlockSpec((nsub, N_HEADS, HEAD, HEAD), lambda i: (i, 0, 0, 0))
    return pl.pallas_call(
        body, name=name, grid=(S // tb,), in_specs=[blk(0), blk(1), blk(2), blk(3), vec, vec],
        out_specs=[blk(0), blk(0), stb],
        out_shape=[jax.ShapeDtypeStruct((S, D_MODEL), BF16), jax.ShapeDtypeStruct((S, D_TOK), F32),
                   jax.ShapeDtypeStruct((S // C, N_HEADS, HEAD, HEAD), BF16)],
        scratch_shapes=[pltpu.VMEM((N_HEADS, HEAD, HEAD), F32)] + [pltpu.VMEM((tb, D_TOK), F32)] * 2
        + [pltpu.VMEM((N_HEADS, C, HEAD), F32)] * 3,
        compiler_params=_params(("arbitrary",)))(p, p, p, p, lb, onorm)


def _hgrn2_bwd(p, lb, onorm, o, states, dheads, *, name):
    S = p.shape[0]
    C = HG_SUB
    tb = _tile(S, 256, C)
    nsub = tb // C
    nblk = S // tb

    def body(q_ref, fz_ref, v_ref, g_ref, lb_ref, on_ref, o_ref, st_ref, dt_ref, dp_ref, dlb_ref, don_ref, dstate,
             b_blk, k_blk, do_blk, db_blk, dk_blk, dq_blk, dv_blk, bsc, ksc, vsc, qsc, dosc):
        @pl.when(pl.program_id(0) == 0)
        def _():
            dstate[...] = jnp.zeros_like(dstate)
            dlb_ref[...] = jnp.zeros_like(dlb_ref)
            don_ref[...] = jnp.zeros_like(don_ref)

        for h in range(N_HEADS):
            cols = slice(h * HEAD, (h + 1) * HEAD)
            onv = on_ref[:, cols]
            gv = g_ref[:, cols]
            ov = o_ref[:, cols]
            dt = dt_ref[:, cols].astype(F32)
            sgg = _sigmoid(gv)
            sil = gv * sgg
            rinv = lax.rsqrt(jnp.mean(ov * ov, axis=-1, keepdims=True) + EPS)
            n = ov * rinv
            don_ref[:, cols] += jnp.sum(dt * n * sil, axis=0, keepdims=True)
            dn = dt * sil * onv
            dp_ref[:, 3 * D_TOK + h * HEAD:3 * D_TOK + (h + 1) * HEAD] = (
                dt * n * onv * sgg * (1.0 + gv * (1.0 - sgg))).astype(dp_ref.dtype)
            do_blk[:, cols] = rinv * (dn - n * jnp.mean(dn * n, axis=-1, keepdims=True))
        _, _, lg, kk = _hg_gates(fz_ref[...], lb_ref[...])
        b_blk[...] = jnp.dot(_chunk_tri(tb, C, False), lg, precision=lax.Precision.HIGHEST,
                             preferred_element_type=F32)
        k_blk[...] = kk
        tt = lax.broadcasted_iota(jnp.int32, (C, HEAD), 0)

        def sub(j, carry):
            c = nsub - 1 - j
            rows = pl.ds(pl.multiple_of(c * C, C), C)
            for h in range(N_HEADS):
                cols = slice(h * HEAD, (h + 1) * HEAD)
                qv = q_ref[rows, cols]
                vv = v_ref[rows, cols]
                do = do_blk[rows, cols]
                b = b_blk[rows, cols]
                kk = k_blk[rows, cols]
                bsc[h] = b
                ksc[h] = kk
                vsc[h] = vv
                qsc[h] = qv
                dosc[h] = do
                b_last = bsc[h, pl.ds(C - 1, 1), :]
                eb = jnp.exp(b)
                qe = qv * eb
                ebb = jnp.exp(b_last - b)
                ke = kk * ebb
                e_last = jnp.exp(b_last)
                st0x = st_ref[c, h].astype(MXU_DTYPE)
                st0 = st0x.astype(F32)
                dst1 = dstate[h]
                dst1x = dst1.astype(MXU_DTYPE)
                dox = do.astype(MXU_DTYPE)
                dqe = jnp.dot(dox, st0x, preferred_element_type=F32)
                dke = jnp.dot(vv.astype(MXU_DTYPE), dst1x, preferred_element_type=F32)
                dv = lax.dot_general(ke.astype(MXU_DTYPE), dst1x, (((1,), (1,)), ((), ())),
                                     preferred_element_type=F32)
                db_last = (e_last * jnp.sum(st0 * dst1, axis=0, keepdims=True)
                           + jnp.sum(dke * ke, axis=0, keepdims=True))
                dstate[h] = dst1 * e_last + lax.dot_general(dox, qe.astype(MXU_DTYPE), (((0,), (0,)), ((), ())),
                                                            preferred_element_type=F32)
                dq = dqe * eb
                db = dqe * qe - dke * ke
                dkk = dke * ebb
                for s in range(C):
                    dec = jnp.where(tt >= s, jnp.exp(b - bsc[h, pl.ds(s, 1), :]), 0.0)
                    da_s = jnp.sum(do * vsc[h, pl.ds(s, 1), :], axis=-1, keepdims=True)
                    pq = da_s * ksc[h, pl.ds(s, 1), :] * dec
                    dq = dq + pq
                    db = db + pq * qv
                for t in range(C):
                    q_t = qsc[h, pl.ds(t, 1), :]
                    do_t = dosc[h, pl.ds(t, 1), :]
                    dec = jnp.where(tt <= t, jnp.exp(bsc[h, pl.ds(t, 1), :] - b), 0.0)
                    da_t = jnp.sum(vv * do_t, axis=-1, keepdims=True)
                    pk = da_t * q_t * dec
                    dkk = dkk + pk
                    db = db - pk * kk
                    a_t = jnp.sum(q_t * kk * dec, axis=-1, keepdims=True)
                    dv = dv + a_t * do_t
                db_blk[rows, cols] = db + jnp.where(tt == C - 1, db_last, 0.0)
                dk_blk[rows, cols] = dkk
                dq_blk[rows, cols] = dq
                dv_blk[rows, cols] = dv
            return carry

        lax.fori_loop(0, nsub, sub, 0)

        dlg = jnp.dot(_chunk_tri(tb, C, True), db_blk[...], precision=lax.Precision.HIGHEST,
                      preferred_element_type=F32)
        lbv = lb_ref[...]
        sg, f, _, _ = _hg_gates(fz_ref[...], lbv)
        w = dlg / f - dk_blk[...]
        dp_ref[:, 0:D_TOK] = dq_blk[...].astype(dp_ref.dtype)
        dp_ref[:, 2 * D_TOK:3 * D_TOK] = dv_blk[...].astype(dp_ref.dtype)
        dp_ref[:, D_TOK:2 * D_TOK] = (w * (1.0 - lbv) * sg * (1.0 - sg)).astype(dp_ref.dtype)
        dlb_ref[...] += jnp.sum(w * (1.0 - sg), axis=0, keepdims=True)

    blk = lambda c: pl.BlockSpec((tb, D_TOK), lambda i, c=c: (nblk - 1 - i, c))
    vec = pl.BlockSpec((1, D_TOK), lambda i: (0, 0))
    stb = pl.BlockSpec((nsub, N_HEADS, HEAD, HEAD), lambda i: (nblk - 1 - i, 0, 0, 0))
    small = jax.ShapeDtypeStruct((1, D_TOK), F32)
    return pl.pallas_call(
        body, name=name, grid=(nblk,), in_specs=[blk(0), blk(1), blk(2), blk(3), vec, vec, blk(0), stb, blk(0)],
        out_specs=[pl.BlockSpec((tb, 4 * D_TOK), lambda i: (nblk - 1 - i, 0)), vec, vec],
        out_shape=[jax.ShapeDtypeStruct((S, HG_IN), BF16), small, small],
        scratch_shapes=[pltpu.VMEM((N_HEADS, HEAD, HEAD), F32)] + [pltpu.VMEM((tb, D_TOK), F32)] * 7
        + [pltpu.VMEM((N_HEADS, C, HEAD), F32)] * 5,
        compiler_params=_params(("arbitrary",)))(p, p, p, p, lb, onorm, o, states, dheads)


def _adamw(w, g, m, v, *, name):
    shape = w.shape
    cols = shape[-1]
    w2, g2, m2, v2 = (t.reshape(-1, cols) for t in (w, g, m, v))
    R = w2.shape[0]
    tr = _tile(R, 512, 8)

    def body(w_ref, g_ref, m_ref, v_ref, d_ref, nm_ref, nv_ref):
        gv = g_ref[...]
        nm = ADAM_B1 * m_ref[...] + (1.0 - ADAM_B1) * gv
        nv = ADAM_B2 * v_ref[...] + (1.0 - ADAM_B2) * (gv * gv)
        m_hat = nm / (1.0 - ADAM_B1 ** ADAM_STEP)
        v_hat = nv / (1.0 - ADAM_B2 ** ADAM_STEP)
        d_ref[...] = -ADAM_LR * (m_hat / (jnp.sqrt(v_hat) + ADAM_EPS) + ADAM_WD * w_ref[...])
        nm_ref[...] = nm
        nv_ref[...] = nv

    spec = pl.BlockSpec((tr, cols), lambda i: (i, 0))
    out = jax.ShapeDtypeStruct((R, cols), F32)
    d, nm, nv = pl.pallas_call(body, name=name, grid=(R // tr,), in_specs=[spec] * 4, out_specs=[spec] * 3,
                               out_shape=[out] * 3, compiler_params=_params(("parallel",)))(w2, g2, m2, v2)
    return d.reshape(shape), nm.reshape(shape), nv.reshape(shape)


def _add_received(own, got, *, name):
    R, Cc = own.shape
    n = got.shape[0]
    tr = _tile(R, 256, 16)

    def body(a_ref, b_ref, o_ref):
        acc = a_ref[...].astype(F32)
        for k in range(n):
            acc = acc + b_ref[k].astype(F32)
        o_ref[...] = acc

    return pl.pallas_call(
        body, name=name, grid=(R // tr,),
        in_specs=[pl.BlockSpec((tr, Cc), lambda i: (i, 0)), pl.BlockSpec((n, tr, Cc), lambda i: (0, i, 0))],
        out_specs=pl.BlockSpec((tr, Cc), lambda i: (i, 0)), out_shape=jax.ShapeDtypeStruct((R, Cc), F32),
        compiler_params=_params(("parallel",)))(own, got)


def _place():
    return lax.axis_index("x"), lax.axis_index("y"), lax.axis_index("c")


def _all_gather(x, *, name, in_vmem, reduce_sum=False, with_token=False):
    R, Cc = x.shape
    space = pltpu.VMEM if in_vmem else pl.ANY

    def body(x_ref, out_ref, *scratch):
        if with_token:
            scratch[0][...] = jnp.zeros_like(scratch[0])
            scratch = scratch[1:]
        if reduce_sum:
            gat_ref, send_sems, recv_sems, local_sem = scratch
        else:
            gat_ref = out_ref
            send_sems, recv_sems, local_sem = scratch
        mx, my, mc = _place()
        me, sibling = (mx, my, mc), (mx, my, 1 - mc)
        chips = [(1 - mx, my), (mx, 1 - my), (1 - mx, 1 - my)]

        def rows(px, py, pc):
            return gat_ref.at[pl.ds((4 * px + 2 * py + pc) * R, R), :]

        def copy(k, block, to, src=None):
            return pltpu.make_async_remote_copy(
                src_ref=rows(*block) if src is None else src, dst_ref=rows(*block), send_sem=send_sems.at[k],
                recv_sem=recv_sems.at[k], device_id=to, device_id_type=MESH_ID)

        mine = pltpu.make_async_copy(x_ref, rows(*me), local_sem)
        mine.start()
        first = [copy(0, me, sibling, src=x_ref)]
        first += [copy(1 + j, me, (*chip, mc), src=x_ref) for j, chip in enumerate(chips)]
        for cp in first:
            cp.start()
        passed = [copy(4 + j, (*chip, mc), sibling) for j, chip in enumerate(chips)]
        for j, chip in enumerate(chips):
            copy(1 + j, (*chip, mc), me).wait_recv()
            passed[j].start()
        copy(0, sibling, me).wait_recv()
        for j, chip in enumerate(chips):
            copy(4 + j, (*chip, 1 - mc), me).wait_recv()
        for cp in first + passed:
            cp.wait_send()
        mine.wait()
        if reduce_sum:
            acc = gat_ref[pl.ds(0, R), :]
            for d in range(1, N_DEV):
                acc = acc + gat_ref[pl.ds(d * R, R), :]
            out_ref[...] = acc

    sems = [pltpu.SemaphoreType.DMA((7,)), pltpu.SemaphoreType.DMA((7,)), pltpu.SemaphoreType.DMA]
    if reduce_sum:
        assert in_vmem
        out_shape = jax.ShapeDtypeStruct((R, Cc), x.dtype)
        scratch = [pltpu.VMEM((N_DEV * R, Cc), x.dtype)] + sems
    else:
        out_shape = jax.ShapeDtypeStruct((N_DEV * R, Cc), x.dtype)
        scratch = sems
    out_specs = pl.BlockSpec(memory_space=space)
    if with_token:
        out_shape = (out_shape, jax.ShapeDtypeStruct((8, LANE), F32))
        out_specs = (out_specs, pl.BlockSpec(memory_space=pltpu.VMEM))
    return pl.pallas_call(
        body, name=name, out_shape=out_shape, in_specs=[pl.BlockSpec(memory_space=space)], out_specs=out_specs,
        scratch_shapes=scratch, compiler_params=pltpu.CompilerParams(vmem_limit_bytes=VMEM_LIMIT))(x)


def _peer(k, mx, my, mc):
    bits = k + 1
    return (1 - mx if bits & 4 else mx, 1 - my if bits & 2 else my, 1 - mc if bits & 1 else mc)


HBM_SPEC = pl.BlockSpec(memory_space=pltpu.HBM)
SEM_SPEC = pl.BlockSpec(memory_space=pltpu.SEMAPHORE)
DATAFLOW = pltpu.SideEffectType.DATAFLOW_SIDE_EFFECTING


def _exchange_copies(x_refs, land_refs, send_sems, recv_sems, scatter):
    mx, my, mc = _place()
    me = 4 * mx + 2 * my + mc
    n = len(x_refs)
    copies = []
    for k in range(N_DEV - 1):
        px, py, pc = _peer(k, mx, my, mc)
        for m, (x_ref, land_ref) in enumerate(zip(x_refs, land_refs)):
            rows = land_ref.shape[1] if scatter else x_ref.shape[0]
            if scatter:
                src = x_ref.at[pl.ds(pl.multiple_of((4 * px + 2 * py + pc) * rows, 16), rows), :]
                dst = land_ref.at[k]
            else:
                src = x_ref
                dst = land_ref.at[pl.ds(pl.multiple_of(me * rows, 16), rows), :]
            copies.append(pltpu.make_async_remote_copy(
                src_ref=src, dst_ref=dst, send_sem=send_sems.at[k * n + m], recv_sem=recv_sems.at[k * n + m],
                device_id=(px, py, pc), device_id_type=MESH_ID))
    return copies


def _land_shape(x, scatter):
    return (N_DEV - 1, x.shape[0] // N_DEV, x.shape[1]) if scatter else (N_DEV * x.shape[0], x.shape[1])


def _own_copies(x_refs, land_refs, local_sems):
    mx, my, mc = _place()
    me = 4 * mx + 2 * my + mc
    return [pltpu.make_async_copy(
        x_ref, land_ref.at[pl.ds(pl.multiple_of(me * x_ref.shape[0], 16), x_ref.shape[0]), :], local_sems.at[m])
        for m, (x_ref, land_ref) in enumerate(zip(x_refs, land_refs))]


def _exchange_start(groups, *, name, scatter):
    sizes = [len(g) for g in groups]
    xs = [x for g in groups for x in g]
    n = len(xs)
    lands = [lax.empty(_land_shape(x, scatter), x.dtype) for x in xs]
    per = 2 if scatter else 3

    def body(*refs):
        sems = refs[2 * n:2 * n + per * len(groups)]
        token = refs[-1]
        off = 0
        for gi, m in enumerate(sizes):
            x_refs, land_refs = refs[off:off + m], refs[n + off:n + off + m]
            for cp in _exchange_copies(x_refs, land_refs, sems[per * gi], sems[per * gi + 1], scatter):
                cp.start()
            if not scatter:
                for cp in _own_copies(x_refs, land_refs, sems[per * gi + 2]):
                    cp.start()
            off += m
        token[...] = jnp.zeros_like(token)

    sem_shapes = []
    for m in sizes:
        sem_shapes += [pltpu.SemaphoreType.DMA(((N_DEV - 1) * m,))] * 2
        if not scatter:
            sem_shapes.append(pltpu.SemaphoreType.DMA((m,)))
    ns = len(sem_shapes)
    out = pl.pallas_call(
        body, name=name,
        out_shape=(*sem_shapes, *[pltpu.HBM(x.shape, x.dtype) for x in xs],
                   *[pltpu.HBM(l.shape, l.dtype) for l in lands], jax.ShapeDtypeStruct((8, LANE), F32)),
        in_specs=(HBM_SPEC,) * (2 * n),
        out_specs=(SEM_SPEC,) * ns + (HBM_SPEC,) * (2 * n) + (pl.BlockSpec(memory_space=pltpu.VMEM),),
        input_output_aliases={i: ns + i for i in range(2 * n)},
        compiler_params=pltpu.CompilerParams(has_side_effects=DATAFLOW))(
            *[pltpu.with_memory_space_constraint(t, pltpu.HBM) for t in xs + lands])
    started, off = [], 0
    for gi, m in enumerate(sizes):
        sems = out[per * gi:per * gi + per]
        started.append((sems[0], sems[1], list(out[ns + off:ns + off + m]),
                        list(out[ns + n + off:ns + n + off + m]), out[-1], None if scatter else sems[2]))
        off += m
    return started


def _exchange_wait(started, after, *, name, scatter):
    send_sems, recv_sems, xs, lands, _, local_sems = started
    n = len(xs)

    def body(*refs):
        x_refs, land_refs = refs[:n], refs[n:2 * n]
        for cp in _exchange_copies(x_refs, land_refs, refs[2 * n], refs[2 * n + 1], scatter):
            cp.wait_send()
            cp.wait_recv()
        if not scatter:
            for cp in _own_copies(x_refs, land_refs, refs[2 * n + 2]):
                cp.wait()

    sems = (send_sems, recv_sems) if scatter else (send_sems, recv_sems, local_sems)
    out = pl.pallas_call(
        body, name=name, out_shape=tuple(pltpu.HBM(t.shape, t.dtype) for t in xs + lands),
        in_specs=(HBM_SPEC,) * (2 * n) + (SEM_SPEC,) * len(sems) + (pl.BlockSpec(memory_space=pl.ANY),),
        out_specs=(HBM_SPEC,) * (2 * n), input_output_aliases={i: i for i in range(2 * n)},
        compiler_params=pltpu.CompilerParams(has_side_effects=DATAFLOW))(*xs, *lands, *sems, after)
    return list(out[:n]), list(out[n:])


def _gather_start(groups, token, *, name):
    first = groups[0]
    groups = [[first[0] + token[0, 0].astype(first[0].dtype)] + list(first[1:])] + [list(g) for g in groups[1:]]
    return _exchange_start(groups, name=name, scatter=False)


def _gather_finish(started, after, *, name):
    return _exchange_wait(started, after, name=name, scatter=False)[1]


def _reduce_start(grads, *, name):
    return _exchange_start([grads], name=name, scatter=True)[0]


def _reduce_finish(started, after, me, *, name):
    sent, gots = _exchange_wait(started, after, name=name + "_wait", scatter=True)
    out = []
    for m, (g, got) in enumerate(zip(sent, gots)):
        rows = g.shape[0] // N_DEV
        own = lax.dynamic_slice(g, (me * rows, 0), (rows, g.shape[1]))
        out.append(_add_received(own, got, name=f"{name}_add{m}"))
    return out


def _pad_rows(a, mult):
    r = (-a.shape[0]) % mult
    return a if r == 0 else jnp.concatenate([a, jnp.zeros((r,) + a.shape[1:], a.dtype)], axis=0)


def kernel(x, mem, mix_norm, mem_norm, w_mem_kv, w_out, hg_w_in, hg_lb, hg_onorm, gm_w_in, gm_ln_g, gm_ln_b, gm_ws, gm_bs, ffn_norm, w_ffn_in, w_ffn_out, final_norm, loss_target, m_mix_norm, m_mem_norm, m_w_mem_kv, m_w_out, m_hg_w_in, m_hg_lb, m_hg_onorm, m_gm_w_in, m_gm_ln_g, m_gm_ln_b, m_gm_ws, m_gm_bs, m_ffn_norm, m_w_ffn_in, m_w_ffn_out, m_final_norm, v_mix_norm, v_mem_norm, v_w_mem_kv, v_w_out, v_hg_w_in, v_hg_lb, v_hg_onorm, v_gm_w_in, v_gm_ln_g, v_gm_ln_b, v_gm_ws, v_gm_bs, v_ffn_norm, v_w_ffn_in, v_w_ffn_out, v_final_norm):
    mx, my, mc = _place()
    me = 4 * mx + 2 * my + mc
    xs = x[0]
    mems = mem[0]
    tgt = loss_target[0]

    hg_t = hg_w_in[0].T.astype(BF16)
    gm_t = gm_w_in[0].T.astype(BF16)
    fi_t = [w_ffn_in[i].T.astype(BF16) for i in range(2)]
    kv_b = [w_mem_kv[i].astype(BF16) for i in range(2)]
    out_b = [w_out[i].astype(BF16) for i in range(2)]
    fo_b = [w_ffn_out[i].astype(BF16) for i in range(2)]
    ln_local = _pad_rows(jnp.concatenate([gm_ln_g, gm_ln_b], axis=0), 8)
    ln_local = jnp.concatenate([ln_local, jnp.zeros((8, LANE - ln_local.shape[1]), F32)], axis=1)
    ln_all, token = _all_gather(ln_local, name="gather_ln", in_vmem=True, with_token=True)
    ln_all = ln_all.reshape(N_DEV, 8, LANE)
    ln_g = ln_all[:, 0, :D_TOK // N_DEV].reshape(1, D_TOK)
    ln_b = ln_all[:, 1, :D_TOK // N_DEV].reshape(1, D_TOK)
    W_hgT, token = _all_gather(hg_t + token[0, 0].astype(BF16), name="gather_first", in_vmem=False,
                               with_token=True)
    gather_mix, fi0, fo0, gather_gm, fi1, fo1 = _gather_start(
        [kv_b + out_b, [fi_t[0]], [fo_b[0]], [gm_t], [fi_t[1]], [fo_b[1]]], token, name="gather_rest_start")
    gather_fi, gather_fo = [fi0, fi1], [fo0, fo1]

    lb_soft = jax.nn.softmax(hg_lb, axis=0)
    lb0 = lb_soft[0:1]
    bsb = jnp.broadcast_to(gm_bs[0][:, :, None], (N_HEADS, GM_CHUNK, GM_CHUNK))
    ws = gm_ws[0]

    W_fiT, W_fo = [], []

    def ffn_fwd(xin, hf, i, next_gain):
        W_fiT.extend(_gather_finish(gather_fi[i], hf, name=f"gather_fi{i}_wait"))
        gu, act = _ffn_in(hf, W_fiT[i], name=f"ffn_in{i}")
        W_fo.extend(_gather_finish(gather_fo[i], act, name=f"gather_fo{i}_wait"))
        return gu, act, _matmul(act, W_fo[i], res=xin, norm_gain=next_gain, name=f"ffn_out{i}")

    h0 = _rms_fwd(xs, mix_norm[0:1], name="mix_norm0", dep=gather_mix[4])
    p0 = _matmul(h0, W_hgT, tb=True, name="hg_in")
    heads0, o0, states = _hgrn2_fwd(p0, lb0, hg_onorm, name="hgrn2_fwd")

    kv0, kv1, wo0, wo1 = _gather_finish(gather_mix, o0, name="gather_mix_wait")
    W_kv, W_out = [kv0, kv1], [wo0, wo1]
    mem_n, kv = [], []
    for i in range(2):
        mn = _rms_fwd(mems, mem_norm[i:i + 1], name=f"mem_norm{i}")
        mem_n.append(mn)
        kv.append(_matmul(mn, W_kv[i], name=f"mem_kv{i}"))

    heads0 = _attn_fwd(p0, 4 * D_TOK // D_MEM, kv[0], heads0, name="attn_fwd0")
    x1, hf0 = _matmul(heads0, W_out[0], res=xs, norm_gain=ffn_norm[0:1], name="out_proj0")
    gu0, act0, (x2, h1) = ffn_fwd(x1, hf0, 0, mix_norm[1:2])

    W_gmT, = _gather_finish(gather_gm, h1, name="gather_gm_wait")
    p1 = _matmul(h1, W_gmT, tb=True, name="gm_in")
    heads1 = _gmlp_fwd(p1, ln_g, ln_b, ws, bsb, name="gmlp_fwd")
    heads1 = _attn_fwd(p1, 2 * D_TOK // D_MEM, kv[1], heads1, name="attn_fwd1")
    x3, hf1 = _matmul(heads1, W_out[1], res=x2, norm_gain=ffn_norm[1:2], name="out_proj1")
    gu1, act1, x4 = ffn_fwd(x3, hf1, 1, None)

    dx, g_final, loss_part = _final_loss(x4, final_norm.reshape(1, D_MODEL), tgt, name="final_loss")

    def ffn_bwd(dx, xin, hf, gu, act, i, dep):
        dgu = _ffn_out_dx(dx, W_fo[i], gu, dep, name=f"ffn_out_dx{i}")
        g_wfo = _matmul(act, dx, ta=True, out_dtype=BF16, name=f"ffn_out_dw{i}")
        g_wfi_t = _matmul(dgu, hf, ta=True, a_halves=True, out_dtype=BF16, name=f"ffn_in_dw{i}")
        dx, g_norm = _matmul(dgu, W_fiT[i], a_halves=True, res=dx, norm_bwd=(xin, ffn_norm[i:i + 1]),
                             name=f"ffn_in_dx{i}")
        return dx, g_wfi_t, g_wfo, g_norm

    def mem_bwd(dkv, i):
        g_wkv = _matmul(mem_n[i], dkv, ta=True, out_dtype=BF16, name=f"mem_kv_dw{i}")
        dmn = _matmul(dkv, W_kv[i], tb=True, name=f"mem_kv_dx{i}")
        _, g_norm = _rms_bwd(mems, mem_norm[i:i + 1], dmn, jnp.zeros_like(mems), name=f"mem_norm_bwd{i}")
        return g_wkv, g_norm

    dx, g_wfi1_t, g_wfo1, g_ffn1 = ffn_bwd(dx, x3, hf1, gu1, act1, 1, loss_part)
    dheads = _matmul(dx, W_out[1], tb=True, name="out_proj_dx1")
    g_wout1 = _matmul(heads1, dx, ta=True, out_dtype=BF16, name="out_proj_dw1")
    dp, g_ws, g_bs, g_lng, g_lnb = _gmlp_bwd(p1, ln_g, ln_b, ws, bsb, dheads, name="gmlp_bwd")
    dp, dk, dv = _attn_bwd(p1, 2 * D_TOK // D_MEM, kv[1], dheads, dp, name="attn_bwd1")
    g_wkv1, g_mem1 = mem_bwd(jnp.concatenate([dk, dv], axis=1), 1)
    g_wgm_t = _matmul(dp, h1, ta=True, out_dtype=BF16, name="gm_in_dw")
    dx, g_mix1 = _matmul(dp, W_gmT, res=dx, norm_bwd=(x2, mix_norm[1:2]), name="gm_in_dx")
    reduce_l1 = _reduce_start([g_wkv1, g_wout1, g_wgm_t, g_wfi1_t, g_wfo1], name="reduce_l1_start")

    dx, g_wfi0_t, g_wfo0, g_ffn0 = ffn_bwd(dx, x1, hf0, gu0, act0, 0, reduce_l1[4])
    reduce_ffn0 = _reduce_start([g_wfi0_t, g_wfo0], name="reduce_ffn0_start")
    dheads = _matmul(dx, W_out[0], tb=True, name="out_proj_dx0", dep=reduce_ffn0[4])
    g_wout0 = _matmul(heads0, dx, ta=True, out_dtype=BF16, name="out_proj_dw0")
    dp, g_lb0, g_onorm = _hgrn2_bwd(p0, lb0, hg_onorm, o0, states, dheads, name="hgrn2_bwd")
    dp, dk, dv = _attn_bwd(p0, 4 * D_TOK // D_MEM, kv[0], dheads, dp, name="attn_bwd0")
    g_wkv0, g_mem0 = mem_bwd(jnp.concatenate([dk, dv], axis=1), 0)
    g_whg_t = _matmul(dp, h0, ta=True, out_dtype=BF16, name="hg_in_dw")
    reduce_mix0 = _reduce_start([g_wkv0, g_wout0, g_whg_t], name="reduce_mix0_start")
    grad_x, g_mix0 = _matmul(dp, W_hgT, res=dx, norm_bwd=(xs, mix_norm[0:1]), name="hg_in_dx", dep=reduce_mix0[4])

    g_kv1, g_out1, g_gm_t, g_fi1_t, g_fo1 = _reduce_finish(reduce_l1, grad_x, me, name="reduce_l1")
    g_fi0_t, g_fo0 = _reduce_finish(reduce_ffn0, g_kv1, me, name="reduce_ffn0")
    g_kv0, g_out0, g_hg_t = _reduce_finish(reduce_mix0, g_fi0_t, me, name="reduce_mix0")
    g_shards = [jnp.stack([g_kv0, g_kv1]), jnp.stack([g_out0, g_out1]), g_hg_t[None], g_gm_t[None],
                jnp.stack([g_fi0_t, g_fi1_t]), jnp.stack([g_fo0, g_fo1])]
    transposed = (4, 7, 13)

    small = [loss_part, jnp.concatenate([g_mix0, g_mix1], axis=1), jnp.concatenate([g_mem0, g_mem1], axis=1),
             g_lb0, g_onorm, g_lng, g_lnb, g_ws.reshape(1, -1), g_bs.reshape(1, -1),
             jnp.concatenate([g_ffn0, g_ffn1], axis=1), g_final]
    sizes = [t.shape[1] for t in small]
    small_rows = _pad_rows(jnp.concatenate(small, axis=1).reshape(-1, LANE), 8)
    red = _all_gather(small_rows, name="reduce_small", in_vmem=True, reduce_sum=True).reshape(-1)
    pieces, off = [], 0
    for n in sizes:
        pieces.append(red[off:off + n])
        off += n
    loss = pieces[0][0]
    g_mix_norm = pieces[1].reshape(2, D_MODEL)
    g_mem_norm = pieces[2].reshape(2, D_MODEL)
    g_hg_lb = pieces[3][None, :] * lb0 * (jnp.eye(3, dtype=F32)[:, 0:1] - lb_soft)
    g_hg_onorm = pieces[4].reshape(1, D_TOK)
    width = D_TOK // N_DEV
    g_gm_ln_g = lax.dynamic_slice(pieces[5], (me * width,), (width,)).reshape(1, width)
    g_gm_ln_b = lax.dynamic_slice(pieces[6], (me * width,), (width,)).reshape(1, width)
    g_gm_ws = pieces[7].reshape(gm_ws.shape)
    g_gm_bs = pieces[8].reshape(gm_bs.shape)
    g_ffn_norm = pieces[9].reshape(2, D_MODEL)
    g_final_norm = pieces[10]

    grads = [g_mix_norm, g_mem_norm, g_shards[0], g_shards[1], g_shards[2], g_hg_lb, g_hg_onorm, g_shards[3],
             g_gm_ln_g, g_gm_ln_b, g_gm_ws, g_gm_bs, g_ffn_norm, g_shards[4], g_shards[5], g_final_norm]
    weights = [mix_norm, mem_norm, w_mem_kv, w_out, hg_w_in, hg_lb, hg_onorm, gm_w_in, gm_ln_g, gm_ln_b, gm_ws, gm_bs,
               ffn_norm, w_ffn_in, w_ffn_out, final_norm]
    ms = [m_mix_norm, m_mem_norm, m_w_mem_kv, m_w_out, m_hg_w_in, m_hg_lb, m_hg_onorm, m_gm_w_in, m_gm_ln_g,
          m_gm_ln_b, m_gm_ws, m_gm_bs, m_ffn_norm, m_w_ffn_in, m_w_ffn_out, m_final_norm]
    vs = [v_mix_norm, v_mem_norm, v_w_mem_kv, v_w_out, v_hg_w_in, v_hg_lb, v_hg_onorm, v_gm_w_in, v_gm_ln_g,
          v_gm_ln_b, v_gm_ws, v_gm_bs, v_ffn_norm, v_w_ffn_in, v_w_ffn_out, v_final_norm]
    deltas, new_m, new_v = [], [], []
    for n, (w, g, m, v) in enumerate(zip(weights, grads, ms, vs)):
        if w.ndim == 1:
            d, nm, nv = _adamw(w[None], g.reshape(1, -1), m[None], v[None], name=f"adamw{n}")
            d, nm, nv = d[0], nm[0], nv[0]
        elif n in transposed:
            flip = lambda t: jnp.swapaxes(t, 1, 2)
            d, nm, nv = (flip(t) for t in _adamw(flip(w), g, flip(m), flip(v), name=f"adamw{n}"))
            grads[n] = flip(g)
        else:
            d, nm, nv = _adamw(w, g.reshape(w.shape), m, v, name=f"adamw{n}")
        deltas.append(d)
        new_m.append(nm)
        new_v.append(nv)
    grads = [g.reshape(w.shape) for g, w in zip(grads, weights)]
    return (loss, grad_x[None], *grads, *deltas, *new_m, *new_v)
```

```python
import functools

import jax
import jax.numpy as jnp
from jax import lax
from jax.experimental import pallas as pl
from jax.experimental.pallas import tpu as pltpu

F32 = jnp.float32
BF16 = jnp.bfloat16
MXU_DTYPE = jnp.bfloat16
MESH_ID = pl.DeviceIdType.MESH

N_DEV = 8
EPS = 1e-6
D_MODEL = 1024
D_TOK = 768
D_MEM = 256
N_HEADS = 6
HEAD = 128
MEM_HEADS = 4
MEM_HDIM = 64
GM_CHUNK = 128
D_FF = 2816
HG_SUB = 16
HG_IN = 4 * D_TOK + D_MEM
GM_IN = 2 * D_TOK + D_MEM
LANE = 128
MXU_COLS = 256

ADAM_LR = 0.001
ADAM_B1 = 0.9
ADAM_B2 = 0.999
ADAM_EPS = 1e-08
ADAM_WD = 0.01
ADAM_STEP = 10

VMEM_LIMIT = 48 * 2 ** 20
VMEM_LIMIT_WIDE = 58 * 2 ** 20


def _params(sem=None, limit=VMEM_LIMIT):
    return pltpu.CompilerParams(dimension_semantics=sem, vmem_limit_bytes=limit)


def _tile(n, cap, q=LANE):
    if n <= cap:
        return n
    best = None
    for t in range(q, cap + 1, q):
        if n % t == 0:
            best = t
    assert best is not None, (n, cap, q)
    return best


def _sigmoid(x):
    return 1.0 / (1.0 + jnp.exp(-x))


def _gelu(x, with_grad=False):
    cdf = 0.5 * (1.0 + lax.erf(x * 0.7071067811865476))
    if not with_grad:
        return x * cdf
    return x * cdf, cdf + x * jnp.exp(-0.5 * x * x) * 0.3989422804014327


def _matmul(a, b, *, name, ta=False, tb=False, res=None, out_dtype=F32, a_halves=False, b_halves=False, dep=None,
            norm_gain=None, norm_bwd=None, loss_head=None):
    if a_halves and ta:
        K, M = a.shape[1], 2 * a.shape[2]
    elif a_halves:
        M, K = a.shape[1], 2 * a.shape[2]
    else:
        K, M = a.shape if ta else a.shape[::-1]
    if b_halves:
        assert not tb and b.shape[1] == K
        N = 2 * b.shape[2]
    else:
        N = b.shape[0] if tb else b.shape[1]
        assert (b.shape[1] if tb else b.shape[0]) == K
    tm = _tile(M // 2 if (a_halves and ta) else M, 1664 if ta else 1024)
    tn = _tile(N // 2 if b_halves else N, 1792)
    tk = _tile(K // 2 if (a_halves and not ta) else K, 1024 if ta else 1664)
    nk = K // tk
    dims = (((0 if ta else 1,), (1 if tb else 0,)), ((), ()))

    strips = norm_bwd is not None or loss_head is not None
    fused = norm_gain is not None or strips
    n_in = 2 + (res is not None) + (norm_gain is not None) + 2 * strips + (dep is not None)
    if fused:
        assert tn == N, "the fused norm needs whole rows"
        assert (norm_gain is not None) + (norm_bwd is not None) + (loss_head is not None) == 1
        assert not strips or (res is not None and nk > 1 and tm % LANE == 0)

    def body(*refs):
        a_ref, b_ref = refs[:2]
        r_ref = refs[2] if res is not None else None
        g_ref = refs[2 + (res is not None)] if fused else None
        x_ref = refs[3 + (res is not None)] if strips else None
        o_ref = refs[n_in]
        h_ref = refs[n_in + 1] if fused else None
        l_ref = refs[n_in + 2] if loss_head is not None else None
        acc = None if nk == 1 else refs[-1]
        k = pl.program_id(2)

        def product():
            return lax.dot_general(a_ref[...].astype(MXU_DTYPE), b_ref[...].astype(MXU_DTYPE), dims,
                                   preferred_element_type=F32)

        def finish(r):
            if loss_head is not None:
                @pl.when(pl.program_id(0) == 0)
                def _():
                    h_ref[...] = jnp.zeros_like(h_ref)
                    l_ref[...] = jnp.zeros_like(l_ref)

                acc[...] = r + r_ref[...]
                gv = g_ref[...]

                def strip(s, carry):
                    dg, loss = carry
                    rows = pl.ds(pl.multiple_of(s * LANE, LANE), LANE)
                    xv = acc[rows, :]
                    scale = lax.rsqrt(jnp.mean(xv * xv, axis=-1, keepdims=True) + EPS)
                    xh = xv * scale
                    err = xh * gv - x_ref[rows, :]
                    loss = loss + 0.5 * jnp.sum(jnp.mean(err * err, axis=-1, keepdims=True), axis=0, keepdims=True)
                    dy = err * (1.0 / N)
                    u = dy * gv
                    o_ref[rows, :] = scale * (u - xh * jnp.mean(u * xh, axis=-1, keepdims=True))
                    return dg + jnp.sum(dy * xh, axis=0, keepdims=True), loss

                dg, loss = lax.fori_loop(0, tm // LANE, strip, (jnp.zeros((1, N), F32), jnp.zeros((1, 1), F32)))
                h_ref[...] += dg
                l_ref[...] += jnp.broadcast_to(loss, l_ref.shape)
                return
            if norm_bwd is not None:
                @pl.when(pl.program_id(0) == 0)
                def _():
                    h_ref[...] = jnp.zeros_like(h_ref)

                acc[...] = r
                gv = g_ref[...]

                def strip(s, dg):
                    rows = pl.ds(pl.multiple_of(s * LANE, LANE), LANE)
                    rv = acc[rows, :]
                    xv = x_ref[rows, :]
                    scale = lax.rsqrt(jnp.mean(xv * xv, axis=-1, keepdims=True) + EPS)
                    xh = xv * scale
                    u = rv * gv
                    o_ref[rows, :] = r_ref[rows, :] + scale * (u - xh * jnp.mean(u * xh, axis=-1, keepdims=True))
                    return dg + jnp.sum(rv * xh, axis=0, keepdims=True)

                h_ref[...] += lax.fori_loop(0, tm // LANE, strip, jnp.zeros((1, N), F32))
                return
            if res is not None:
                r = r + r_ref[...].astype(F32)
            o_ref[...] = r.astype(out_dtype)
            if norm_gain is not None:
                scale = lax.rsqrt(jnp.mean(r * r, axis=-1, keepdims=True) + EPS)
                h_ref[...] = (r * scale * g_ref[...]).astype(h_ref.dtype)

        if nk == 1:
            finish(product())
            return

        @pl.when(k == 0)
        def _():
            acc[...] = product()

        @pl.when((k > 0) & (k < nk - 1))
        def _():
            acc[...] += product()

        @pl.when(k == nk - 1)
        def _():
            finish(acc[...] + product())

    if a_halves and ta:
        mh = M // 2 // tm
        a_spec = pl.BlockSpec((None, tk, tm), lambda i, j, k: (i // mh, k, i % mh))
    elif a_halves:
        kh = nk // 2
        a_spec = pl.BlockSpec((None, tm, tk), lambda i, j, k: (k // kh, i, k % kh))
    elif ta:
        a_spec = pl.BlockSpec((tk, tm), lambda i, j, k: (k, i))
    else:
        a_spec = pl.BlockSpec((tm, tk), lambda i, j, k: (i, k))
    if b_halves:
        nh = N // 2 // tn
        b_spec = pl.BlockSpec((None, tk, tn), lambda i, j, k: (j // nh, k, j % nh))
    elif tb:
        b_spec = pl.BlockSpec((tn, tk), lambda i, j, k: (j, k))
    else:
        b_spec = pl.BlockSpec((tk, tn), lambda i, j, k: (k, j))
    o_spec = pl.BlockSpec((tm, tn), lambda i, j, k: (i, j))
    in_specs = [a_spec, b_spec] + ([o_spec] if res is not None else [])
    args = (a, b) + ((res,) if res is not None else ())
    out_specs, out_shape = o_spec, jax.ShapeDtypeStruct((M, N), out_dtype)
    vec = pl.BlockSpec((1, N), lambda i, j, k: (0, 0))
    sem = ("parallel", "parallel", "arbitrary")
    if norm_gain is not None:
        in_specs.append(vec)
        args += (norm_gain,)
        out_specs, out_shape = [o_spec, o_spec], [out_shape, jax.ShapeDtypeStruct((M, N), BF16)]
    if norm_bwd is not None:
        x_in, gain = norm_bwd
        in_specs += [vec, o_spec]
        args += (gain, x_in)
        out_specs, out_shape = [o_spec, vec], [out_shape, jax.ShapeDtypeStruct((1, N), F32)]
        sem = ("arbitrary", "arbitrary", "arbitrary")
    if loss_head is not None:
        gain, target = loss_head
        in_specs += [vec, o_spec]
        args += (gain, target)
        one = pl.BlockSpec((1, LANE), lambda i, j, k: (0, 0))
        out_specs = [o_spec, vec, one]
        out_shape = [out_shape, jax.ShapeDtypeStruct((1, N), F32), jax.ShapeDtypeStruct((1, LANE), F32)]
        sem = ("arbitrary", "arbitrary", "arbitrary")
    if dep is not None:
        in_specs.append(pl.BlockSpec(memory_space=pl.ANY))
        args += (dep,)
    return pl.pallas_call(
        body, name=name, grid=(M // tm, N // tn, nk), in_specs=in_specs, out_specs=out_specs, out_shape=out_shape,
        scratch_shapes=[] if nk == 1 else [pltpu.VMEM((tm, tn), F32)],
        compiler_params=_params(sem, VMEM_LIMIT_WIDE if strips else VMEM_LIMIT))(*args)


def _ffn_in(hf, wt, *, name):
    S, K = hf.shape
    tm = _tile(S, 512)
    tn = _tile(D_FF, 1408)
    nh = D_FF // tn
    nt = (((1,), (1,)), ((), ()))

    def body(a_ref, bg_ref, bu_ref, gu_ref, act_ref):
        av = a_ref[...].astype(MXU_DTYPE)
        for c0 in range(0, tn, MXU_COLS):
            cs = slice(c0, min(c0 + MXU_COLS, tn))
            gate = lax.dot_general(av, bg_ref[cs, :].astype(MXU_DTYPE), nt, preferred_element_type=F32)
            up = lax.dot_general(av, bu_ref[cs, :].astype(MXU_DTYPE), nt, preferred_element_type=F32)
            gu_ref[0, :, cs] = gate.astype(gu_ref.dtype)
            gu_ref[1, :, cs] = up.astype(gu_ref.dtype)
            act_ref[:, cs] = (gate * _sigmoid(gate) * up).astype(act_ref.dtype)

    return pl.pallas_call(
        body, name=name, grid=(nh, S // tm),
        in_specs=[pl.BlockSpec((tm, K), lambda j, i: (i, 0)), pl.BlockSpec((tn, K), lambda j, i: (j, 0)),
                  pl.BlockSpec((tn, K), lambda j, i: (j + nh, 0))],
        out_specs=[pl.BlockSpec((2, tm, tn), lambda j, i: (0, i, j)), pl.BlockSpec((tm, tn), lambda j, i: (i, j))],
        out_shape=[jax.ShapeDtypeStruct((2, S, D_FF), BF16), jax.ShapeDtypeStruct((S, D_FF), BF16)],
        compiler_params=_params(("parallel", "parallel")))(hf, wt, wt)


def _ffn_out_dx(dx, w, gu, dep, *, name):
    S, K = dx.shape
    tm = _tile(S, 1024)
    tn = _tile(D_FF, 1408)

    def body(a_ref, b_ref, gu_ref, dep_ref, o_ref):
        del dep_ref
        av = a_ref[...].astype(MXU_DTYPE)
        for c0 in range(0, tn, MXU_COLS):
            cs = slice(c0, min(c0 + MXU_COLS, tn))
            da = lax.dot_general(av, b_ref[cs, :].astype(MXU_DTYPE), (((1,), (1,)), ((), ())),
                                 preferred_element_type=F32)
            gate = gu_ref[0, :, cs].astype(F32)
            up = gu_ref[1, :, cs].astype(F32)
            sg = _sigmoid(gate)
            o_ref[0, :, cs] = (da * up * sg * (1.0 + gate * (1.0 - sg))).astype(o_ref.dtype)
            o_ref[1, :, cs] = (da * gate * sg).astype(o_ref.dtype)

    halves = pl.BlockSpec((2, tm, tn), lambda i, j: (0, i, j))
    return pl.pallas_call(
        body, name=name, grid=(S // tm, D_FF // tn),
        in_specs=[pl.BlockSpec((tm, K), lambda i, j: (i, 0)), pl.BlockSpec((tn, K), lambda i, j: (j, 0)), halves,
                  pl.BlockSpec(memory_space=pl.ANY)],
        out_specs=halves, out_shape=jax.ShapeDtypeStruct((2, S, D_FF), BF16),
        compiler_params=_params(("parallel", "parallel")))(dx, w, gu, dep)


def _rms_fwd(x, g, *, name, dep=None):
    R, Dm = x.shape
    tr = _tile(R, 512, 8)

    def body(x_ref, g_ref, *rest):
        o_ref = rest[-1]
        xv = x_ref[...]
        r = lax.rsqrt(jnp.mean(xv * xv, axis=-1, keepdims=True) + EPS)
        o_ref[...] = (xv * r * g_ref[...]).astype(o_ref.dtype)

    in_specs = [pl.BlockSpec((tr, Dm), lambda i: (i, 0)), pl.BlockSpec((1, Dm), lambda i: (0, 0))]
    args = (x, g)
    if dep is not None:
        in_specs.append(pl.BlockSpec(memory_space=pl.ANY))
        args += (dep,)
    return pl.pallas_call(
        body, name=name, grid=(R // tr,), in_specs=in_specs,
        out_specs=pl.BlockSpec((tr, Dm), lambda i: (i, 0)), out_shape=jax.ShapeDtypeStruct((R, Dm), BF16),
        compiler_params=_params(("parallel",)))(*args)


def _rms_bwd(x, g, dh, dres, *, name):
    R, Dm = x.shape
    tr = _tile(R, 256, 8)

    def body(x_ref, g_ref, dh_ref, dres_ref, dx_ref, dg_ref):
        @pl.when(pl.program_id(0) == 0)
        def _():
            dg_ref[...] = jnp.zeros_like(dg_ref)

        xv = x_ref[...]
        r = lax.rsqrt(jnp.mean(xv * xv, axis=-1, keepdims=True) + EPS)
        xh = xv * r
        dhv = dh_ref[...].astype(F32)
        dg_ref[...] += jnp.sum(dhv * xh, axis=0, keepdims=True)
        u = dhv * g_ref[...]
        dx = r * (u - xh * jnp.mean(u * xh, axis=-1, keepdims=True))
        dx_ref[...] = dres_ref[...] + dx

    row = pl.BlockSpec((tr, Dm), lambda i: (i, 0))
    vec = pl.BlockSpec((1, Dm), lambda i: (0, 0))
    return pl.pallas_call(
        body, name=name, grid=(R // tr,), in_specs=[row, vec, row, row], out_specs=[row, vec],
        out_shape=[jax.ShapeDtypeStruct((R, Dm), F32), jax.ShapeDtypeStruct((1, Dm), F32)],
        compiler_params=_params(("arbitrary",)))(x, g, dh, dres)


def _head_mask(h):
    lane = lax.broadcasted_iota(jnp.int32, (1, D_MEM), 1)
    return (lane >= h * MEM_HDIM) & (lane < (h + 1) * MEM_HDIM)


def _attn_probs(qv, k_mx, mask):
    s = lax.dot_general(jnp.where(mask, qv, 0.0).astype(MXU_DTYPE), k_mx, (((1,), (1,)), ((), ())),
                        preferred_element_type=F32) * (MEM_HDIM ** -0.5)
    e = jnp.exp(s - jnp.max(s, axis=-1, keepdims=True))
    return e / jnp.sum(e, axis=-1, keepdims=True)


def _attn_fwd(p, qcol, kv, heads, *, name):
    S = p.shape[0]
    M = kv.shape[0]
    ts = _tile(S, 512, 8)

    def body(q_ref, k_ref, v_ref, heads_in, o_ref):
        del heads_in
        qv = q_ref[...]
        kx = k_ref[...].astype(MXU_DTYPE)
        vv = v_ref[...]
        out = jnp.zeros((ts, D_MEM), F32)
        for h in range(MEM_HEADS):
            mask = _head_mask(h)
            pr = _attn_probs(qv, kx, mask)
            out = out + jnp.dot(pr.astype(MXU_DTYPE), jnp.where(mask, vv, 0.0).astype(MXU_DTYPE),
                                preferred_element_type=F32)
        o_ref[...] = out.astype(o_ref.dtype)

    return pl.pallas_call(
        body, name=name, grid=(S // ts,),
        in_specs=[pl.BlockSpec((ts, D_MEM), lambda i: (i, qcol)), pl.BlockSpec((M, D_MEM), lambda i: (0, 0)),
                  pl.BlockSpec((M, D_MEM), lambda i: (0, 1)), pl.BlockSpec(memory_space=pl.ANY)],
        out_specs=pl.BlockSpec((ts, D_MEM), lambda i: (i, D_TOK // D_MEM)),
        out_shape=jax.ShapeDtypeStruct(heads.shape, heads.dtype), input_output_aliases={3: 0},
        compiler_params=_params(("parallel",)))(p, kv, kv, heads)


def _attn_bwd(p, qcol, kv, dheads, dp, *, name):
    S = p.shape[0]
    M = kv.shape[0]
    ts = _tile(S, 512, 8)
    scale = MEM_HDIM ** -0.5

    def body(q_ref, k_ref, v_ref, do_ref, dp_in, dq_ref, dk_ref, dv_ref):
        del dp_in

        @pl.when(pl.program_id(0) == 0)
        def _():
            dk_ref[...] = jnp.zeros_like(dk_ref)
            dv_ref[...] = jnp.zeros_like(dv_ref)

        qv = q_ref[...]
        kv_ = k_ref[...]
        kx = kv_.astype(MXU_DTYPE)
        vv = v_ref[...]
        dox = do_ref[...].astype(MXU_DTYPE)
        qx = qv.astype(MXU_DTYPE)
        dq = jnp.zeros((ts, D_MEM), F32)
        for h in range(MEM_HEADS):
            mask = _head_mask(h)
            pr = _attn_probs(qv, kx, mask)
            vh = jnp.where(mask, vv, 0.0).astype(MXU_DTYPE)
            dpr = lax.dot_general(dox, vh, (((1,), (1,)), ((), ())), preferred_element_type=F32)
            ds = (pr * (dpr - jnp.sum(dpr * pr, axis=-1, keepdims=True)) * scale).astype(MXU_DTYPE)
            dq = dq + jnp.dot(ds, jnp.where(mask, kv_, 0.0).astype(MXU_DTYPE), preferred_element_type=F32)
            dk_h = lax.dot_general(ds, qx, (((0,), (0,)), ((), ())), preferred_element_type=F32)
            dv_h = lax.dot_general(pr.astype(MXU_DTYPE), dox, (((0,), (0,)), ((), ())), preferred_element_type=F32)
            dk_ref[...] += jnp.where(mask, dk_h, 0.0)
            dv_ref[...] += jnp.where(mask, dv_h, 0.0)
        dq_ref[...] = dq.astype(dq_ref.dtype)

    return pl.pallas_call(
        body, name=name, grid=(S // ts,),
        in_specs=[pl.BlockSpec((ts, D_MEM), lambda i: (i, qcol)), pl.BlockSpec((M, D_MEM), lambda i: (0, 0)),
                  pl.BlockSpec((M, D_MEM), lambda i: (0, 1)),
                  pl.BlockSpec((ts, D_MEM), lambda i: (i, D_TOK // D_MEM)), pl.BlockSpec(memory_space=pl.ANY)],
        out_specs=[pl.BlockSpec((ts, D_MEM), lambda i: (i, qcol)), pl.BlockSpec((M, D_MEM), lambda i: (0, 0)),
                   pl.BlockSpec((M, D_MEM), lambda i: (0, 0))],
        out_shape=[jax.ShapeDtypeStruct(dp.shape, dp.dtype), jax.ShapeDtypeStruct((M, D_MEM), F32),
                   jax.ShapeDtypeStruct((M, D_MEM), F32)],
        input_output_aliases={4: 0}, compiler_params=_params(("arbitrary",)))(p, kv, kv, dheads, dp)


def _gm_forward_parts(u_ref, v_ref, lng_ref, lnb_ref, w_ref, bsb_ref, with_grad=False):
    if with_grad:
        (zu, du_gelu), (zv, dv_gelu) = _gelu(u_ref[...], True), _gelu(v_ref[...], True)
    else:
        zu, zv, du_gelu, dv_gelu = _gelu(u_ref[...]), _gelu(v_ref[...]), None, None
    mu = jnp.mean(zv, axis=-1, keepdims=True)
    cen = zv - mu
    rs = lax.rsqrt(jnp.mean(cen * cen, axis=-1, keepdims=True) + EPS)
    vh = cen * rs
    vn = vh * lng_ref[...] + lnb_ref[...]
    row = lax.broadcasted_iota(jnp.int32, (GM_CHUNK, GM_CHUNK), 0)
    col = lax.broadcasted_iota(jnp.int32, (GM_CHUNK, GM_CHUNK), 1)
    tril = row >= col
    wm = [jnp.where(tril, w_ref[g], 0.0).astype(MXU_DTYPE) for g in range(N_HEADS)]
    vnx = [vn[:, g * HEAD:(g + 1) * HEAD].astype(MXU_DTYPE) for g in range(N_HEADS)]
    sv = [jnp.dot(wm[g], vnx[g], preferred_element_type=F32) + bsb_ref[g] for g in range(N_HEADS)]
    return zu, vh, rs, wm, vnx, sv, tril, du_gelu, dv_gelu


def _gmlp_fwd(p, lng, lnb, ws, bsb, *, name):
    S = p.shape[0]

    def body(u_ref, v_ref, lng_ref, lnb_ref, w_ref, bsb_ref, o_ref):
        zu, _, _, _, _, sv, _, _, _ = _gm_forward_parts(u_ref, v_ref, lng_ref, lnb_ref, w_ref, bsb_ref)
        for g in range(N_HEADS):
            o_ref[:, g * HEAD:(g + 1) * HEAD] = (zu[:, g * HEAD:(g + 1) * HEAD] * sv[g]).astype(o_ref.dtype)

    blk = lambda c: pl.BlockSpec((GM_CHUNK, D_TOK), lambda i: (i, c))
    vec = pl.BlockSpec((1, D_TOK), lambda i: (0, 0))
    cube = pl.BlockSpec((N_HEADS, GM_CHUNK, GM_CHUNK), lambda i: (0, 0, 0))
    return pl.pallas_call(
        body, name=name, grid=(S // GM_CHUNK,), in_specs=[blk(0), blk(1), vec, vec, cube, cube],
        out_specs=blk(0), out_shape=jax.ShapeDtypeStruct((S, D_MODEL), BF16),
        compiler_params=_params(("parallel",)))(p, p, lng, lnb, ws, bsb)


def _gmlp_bwd(p, lng, lnb, ws, bsb, dheads, *, name):
    S = p.shape[0]

    def body(u_ref, v_ref, lng_ref, lnb_ref, w_ref, bsb_ref, dt_ref, dp_ref, dw_ref, dbs_ref, dlg_ref, dlb_ref):
        @pl.when(pl.program_id(0) == 0)
        def _():
            dw_ref[...] = jnp.zeros_like(dw_ref)
            dbs_ref[...] = jnp.zeros_like(dbs_ref)
            dlg_ref[...] = jnp.zeros_like(dlg_ref)
            dlb_ref[...] = jnp.zeros_like(dlb_ref)

        zu, vh, rs, wm, vnx, sv, tril, du_gelu, dv_gelu = _gm_forward_parts(
            u_ref, v_ref, lng_ref, lnb_ref, w_ref, bsb_ref, with_grad=True)
        dt = dt_ref[...].astype(F32)
        dvn_parts = []
        for g in range(N_HEADS):
            sl = slice(g * HEAD, (g + 1) * HEAD)
            dsv = dt[:, sl] * zu[:, sl]
            dp_ref[:, sl] = (dt[:, sl] * sv[g] * du_gelu[:, sl]).astype(dp_ref.dtype)
            dsx = dsv.astype(MXU_DTYPE)
            dw = lax.dot_general(dsx, vnx[g], (((1,), (1,)), ((), ())), preferred_element_type=F32)
            dw_ref[g] += jnp.where(tril, dw, 0.0)
            dbs_ref[g] += jnp.sum(dsv, axis=-1, keepdims=True)
            dvn_parts.append(lax.dot_general(wm[g], dsx, (((0,), (0,)), ((), ())), preferred_element_type=F32))
        dvn = jnp.concatenate(dvn_parts, axis=-1)
        dlg_ref[...] += jnp.sum(dvn * vh, axis=0, keepdims=True)
        dlb_ref[...] += jnp.sum(dvn, axis=0, keepdims=True)
        dvh = dvn * lng_ref[...]
        dzv = rs * (dvh - jnp.mean(dvh, axis=-1, keepdims=True) - vh * jnp.mean(dvh * vh, axis=-1, keepdims=True))
        dp_ref[:, D_TOK:] = (dzv * dv_gelu).astype(dp_ref.dtype)

    blk = lambda c: pl.BlockSpec((GM_CHUNK, D_TOK), lambda i: (i, c))
    vec = pl.BlockSpec((1, D_TOK), lambda i: (0, 0))
    cube = pl.BlockSpec((N_HEADS, GM_CHUNK, GM_CHUNK), lambda i: (0, 0, 0))
    col = pl.BlockSpec((N_HEADS, GM_CHUNK, 1), lambda i: (0, 0, 0))
    return pl.pallas_call(
        body, name=name, grid=(S // GM_CHUNK,), in_specs=[blk(0), blk(1), vec, vec, cube, cube, blk(0)],
        out_specs=[pl.BlockSpec((GM_CHUNK, 2 * D_TOK), lambda i: (i, 0)), cube, col, vec, vec],
        out_shape=[jax.ShapeDtypeStruct((S, GM_IN), BF16), jax.ShapeDtypeStruct((N_HEADS, GM_CHUNK, GM_CHUNK), F32),
                   jax.ShapeDtypeStruct((N_HEADS, GM_CHUNK, 1), F32), jax.ShapeDtypeStruct((1, D_TOK), F32),
                   jax.ShapeDtypeStruct((1, D_TOK), F32)],
        compiler_params=_params(("arbitrary",)))(p, p, lng, lnb, ws, bsb, dheads)


def _chunk_tri(n, chunk, upper):
    r = lax.broadcasted_iota(jnp.int32, (n, n), 0)
    c = lax.broadcasted_iota(jnp.int32, (n, n), 1)
    same = (r // chunk) == (c // chunk)
    return jnp.where(same & ((r <= c) if upper else (r >= c)), 1.0, 0.0).astype(F32)


def _running_sum(tri, x):
    hi = x.astype(BF16)
    rest = x - hi.astype(F32)
    mid = rest.astype(BF16)
    lo = (rest - mid.astype(F32)).astype(BF16)
    tri = tri.astype(BF16)
    return (jnp.dot(tri, hi, preferred_element_type=F32) + jnp.dot(tri, mid, preferred_element_type=F32)
            + jnp.dot(tri, lo, preferred_element_type=F32))


MASKED = -1e30


def _pair_masks(mask_ref, n, upper):
    row = lax.broadcasted_iota(jnp.int32, (n, HEAD), 0)
    for i in range(n):
        mask_ref[i] = jnp.where((row <= i) if upper else (row >= i), 0.0, MASKED).astype(F32)


def _hg_gates(fz, lb):
    sg = _sigmoid(fz)
    f = lb + (1.0 - lb) * sg
    kk = (1.0 - lb) * (1.0 - sg)
    return sg, f, jnp.log(f), kk


def _hgrn2_fwd(p, lb, onorm, *, name):
    S = p.shape[0]
    C = HG_SUB
    tb = _tile(S, 256, C)
    nsub = tb // C

    def body(q_ref, fz_ref, v_ref, g_ref, lb_ref, on_ref, tok_ref, o_ref, st_ref, state, b_blk, k_blk, bsc, ksc, vsc):
        @pl.when(pl.program_id(0) == 0)
        def _():
            state[...] = jnp.zeros_like(state)

        _, _, lg, kk = _hg_gates(fz_ref[...], lb_ref[...])
        b_blk[...] = _running_sum(_chunk_tri(tb, C, False), lg)
        k_blk[...] = kk
        tt = lax.broadcasted_iota(jnp.int32, (C, HEAD), 0)

        def sub(c, carry):
            rows = pl.ds(pl.multiple_of(c * C, C), C)
            for h in range(N_HEADS):
                cols = slice(h * HEAD, (h + 1) * HEAD)
                qv = q_ref[rows, cols]
                vv = v_ref[rows, cols]
                b = b_blk[rows, cols]
                kk = k_blk[rows, cols]
                st0 = state[h]
                st0x = st0.astype(MXU_DTYPE)
                st_ref[c, h] = st0x.astype(st_ref.dtype)
                inter = lax.dot_general((qv * jnp.exp(b)).astype(MXU_DTYPE), st0x,
                                        (((1,), (1,)), ((), ())), preferred_element_type=F32)
                bsc[h] = b
                ksc[h] = kk
                vsc[h] = vv
                intra = jnp.zeros((C, HEAD), F32)
                for s in range(C):
                    dec = jnp.where(tt >= s, jnp.exp(b - bsc[h, pl.ds(s, 1), :]), 0.0)
                    a_s = jnp.sum(qv * ksc[h, pl.ds(s, 1), :] * dec, axis=-1, keepdims=True)
                    intra = intra + a_s * vsc[h, pl.ds(s, 1), :]
                o_ref[rows, cols] = inter + intra
                b_last = bsc[h, pl.ds(C - 1, 1), :]
                ke = kk * jnp.exp(b_last - b)
                state[h] = st0 * jnp.exp(b_last) + lax.dot_general(
                    vv.astype(MXU_DTYPE), ke.astype(MXU_DTYPE), (((0,), (0,)), ((), ())),
                    preferred_element_type=F32)
            return carry

        lax.fori_loop(0, nsub, sub, 0, unroll=2)

        for h in range(N_HEADS):
            cols = slice(h * HEAD, (h + 1) * HEAD)
            o = o_ref[:, cols]
            gv = g_ref[:, cols]
            n = o * lax.rsqrt(jnp.mean(o * o, axis=-1, keepdims=True) + EPS)
            tok_ref[:, cols] = (n * (gv * _sigmoid(gv)) * on_ref[:, cols]).astype(tok_ref.dtype)

    blk = lambda c: pl.BlockSpec((tb, D_TOK), lambda i, c=c: (i, c))
    vec = pl.BlockSpec((1, D_TOK), lambda i: (0, 0))
    stb = pl.BlockSpec((nsub, N_HEADS, HEAD, HEAD), lambda i: (i, 0, 0, 0))
    return pl.pallas_call(
        body, name=name, grid=(S // tb,), in_specs=[blk(0), blk(1), blk(2), blk(3), vec, vec],
        out_specs=[blk(0), blk(0), stb],
        out_shape=[jax.ShapeDtypeStruct((S, D_MODEL), BF16), jax.ShapeDtypeStruct((S, D_TOK), F32),
                   jax.ShapeDtypeStruct((S // C, N_HEADS, HEAD, HEAD), BF16)],
        scratch_shapes=[pltpu.VMEM((N_HEADS, HEAD, HEAD), F32)] + [pltpu.VMEM((tb, D_TOK), F32)] * 2
        + [pltpu.VMEM((N_HEADS, C, HEAD), F32)] * 3,
        compiler_params=_params(("arbitrary",)))(p, p, p, p, lb, onorm)


def _hgrn2_bwd(p, lb, onorm, o, states, dheads, *, name):
    S = p.shape[0]
    C = HG_SUB
    tb = _tile(S, 256, C)
    nsub = tb // C
    nblk = S // tb

    def body(q_ref, fz_ref, v_ref, g_ref, lb_ref, on_ref, o_ref, st_ref, dt_ref, dp_ref, dlb_ref, don_ref, dstate,
             b_blk, k_blk, do_blk, db_blk, dk_blk, dq_blk, dv_blk, bsc, ksc, vsc, qsc, dosc, causal, anti):
        @pl.when(pl.program_id(0) == 0)
        def _():
            dstate[...] = jnp.zeros_like(dstate)
            dlb_ref[...] = jnp.zeros_like(dlb_ref)
            don_ref[...] = jnp.zeros_like(don_ref)

        for h in range(N_HEADS):
            cols = slice(h * HEAD, (h + 1) * HEAD)
            onv = on_ref[:, cols]
            gv = g_ref[:, cols]
            ov = o_ref[:, cols]
            dt = dt_ref[:, cols].astype(F32)
            sgg = _sigmoid(gv)
            sil = gv * sgg
            rinv = lax.rsqrt(jnp.mean(ov * ov, axis=-1, keepdims=True) + EPS)
            n = ov * rinv
            don_ref[:, cols] += jnp.sum(dt * n * sil, axis=0, keepdims=True)
            dn = dt * sil * onv
            dp_ref[:, 3 * D_TOK + h * HEAD:3 * D_TOK + (h + 1) * HEAD] = (
                dt * n * onv * sgg * (1.0 + gv * (1.0 - sgg))).astype(dp_ref.dtype)
            do_blk[:, cols] = rinv * (dn - n * jnp.mean(dn * n, axis=-1, keepdims=True))
        _, _, lg, kk = _hg_gates(fz_ref[...], lb_ref[...])
        b_blk[...] = _running_sum(_chunk_tri(tb, C, False), lg)
        k_blk[...] = kk
        _pair_masks(causal, C, False)
        _pair_masks(anti, C, True)
        tt = lax.broadcasted_iota(jnp.int32, (C, HEAD), 0)

        def sub(j, carry):
            c = nsub - 1 - j
            rows = pl.ds(pl.multiple_of(c * C, C), C)
            for h in range(N_HEADS):
                cols = slice(h * HEAD, (h + 1) * HEAD)
                qv = q_ref[rows, cols]
                vv = v_ref[rows, cols]
                do = do_blk[rows, cols]
                b = b_blk[rows, cols]
                kk = k_blk[rows, cols]
                bsc[h] = b
                ksc[h] = kk
                vsc[h] = vv
                qsc[h] = qv
                dosc[h] = do
                b_last = bsc[h, pl.ds(C - 1, 1), :]
                eb = jnp.exp(b)
                qe = qv * eb
                ebb = jnp.exp(b_last - b)
                ke = kk * ebb
                e_last = jnp.exp(b_last)
                st0x = st_ref[c, h].astype(MXU_DTYPE)
                st0 = st0x.astype(F32)
                dst1 = dstate[h]
                dst1x = dst1.astype(MXU_DTYPE)
                dox = do.astype(MXU_DTYPE)
                dqe = jnp.dot(dox, st0x, preferred_element_type=F32)
                dke = jnp.dot(vv.astype(MXU_DTYPE), dst1x, preferred_element_type=F32)
                dv = lax.dot_general(ke.astype(MXU_DTYPE), dst1x, (((1,), (1,)), ((), ())),
                                     preferred_element_type=F32)
                db_last = (e_last * jnp.sum(st0 * dst1, axis=0, keepdims=True)
                           + jnp.sum(dke * ke, axis=0, keepdims=True))
                dstate[h] = dst1 * e_last + lax.dot_general(dox, qe.astype(MXU_DTYPE), (((0,), (0,)), ((), ())),
                                                            preferred_element_type=F32)
                dq_pairs = jnp.zeros((C, HEAD), F32)
                for s in range(C):
                    dec = jnp.exp(b - bsc[h, pl.ds(s, 1), :] + causal[s])
                    da_s = jnp.sum(do * vsc[h, pl.ds(s, 1), :], axis=-1, keepdims=True)
                    dq_pairs = dq_pairs + da_s * (ksc[h, pl.ds(s, 1), :] * dec)
                dk_pairs = jnp.zeros((C, HEAD), F32)
                for t in range(C):
                    do_t = dosc[h, pl.ds(t, 1), :]
                    qd = qsc[h, pl.ds(t, 1), :] * jnp.exp(bsc[h, pl.ds(t, 1), :] - b + anti[t])
                    da_t = jnp.sum(vv * do_t, axis=-1, keepdims=True)
                    dk_pairs = dk_pairs + da_t * qd
                    a_t = jnp.sum(qd * kk, axis=-1, keepdims=True)
                    dv = dv + a_t * do_t
                db = dqe * qe - dke * ke + qv * dq_pairs - kk * dk_pairs
                db_blk[rows, cols] = db + jnp.where(tt == C - 1, db_last, 0.0)
                dk_blk[rows, cols] = dke * ebb + dk_pairs
                dq_blk[rows, cols] = dqe * eb + dq_pairs
                dv_blk[rows, cols] = dv
            return carry

        lax.fori_loop(0, nsub, sub, 0)

        dlg = _running_sum(_chunk_tri(tb, C, True), db_blk[...])
        lbv = lb_ref[...]
        sg, f, _, _ = _hg_gates(fz_ref[...], lbv)
        w = dlg / f - dk_blk[...]
        dp_ref[:, 0:D_TOK] = dq_blk[...].astype(dp_ref.dtype)
        dp_ref[:, 2 * D_TOK:3 * D_TOK] = dv_blk[...].astype(dp_ref.dtype)
        dp_ref[:, D_TOK:2 * D_TOK] = (w * (1.0 - lbv) * sg * (1.0 - sg)).astype(dp_ref.dtype)
        dlb_ref[...] += jnp.sum(w * (1.0 - sg), axis=0, keepdims=True)

    blk = lambda c: pl.BlockSpec((tb, D_TOK), lambda i, c=c: (nblk - 1 - i, c))
    vec = pl.BlockSpec((1, D_TOK), lambda i: (0, 0))
    stb = pl.BlockSpec((nsub, N_HEADS, HEAD, HEAD), lambda i: (nblk - 1 - i, 0, 0, 0))
    small = jax.ShapeDtypeStruct((1, D_TOK), F32)
    return pl.pallas_call(
        body, name=name, grid=(nblk,), in_specs=[blk(0), blk(1), blk(2), blk(3), vec, vec, blk(0), stb, blk(0)],
        out_specs=[pl.BlockSpec((tb, 4 * D_TOK), lambda i: (nblk - 1 - i, 0)), vec, vec],
        out_shape=[jax.ShapeDtypeStruct((S, HG_IN), BF16), small, small],
        scratch_shapes=[pltpu.VMEM((N_HEADS, HEAD, HEAD), F32)] + [pltpu.VMEM((tb, D_TOK), F32)] * 7
        + [pltpu.VMEM((N_HEADS, C, HEAD), F32)] * 5 + [pltpu.VMEM((C, C, HEAD), F32)] * 2,
        compiler_params=_params(("arbitrary",)))(p, p, p, p, lb, onorm, o, states, dheads)


def _adamw(w, g, m, v, *, name):
    shape = w.shape
    cols = shape[-1]
    w2, g2, m2, v2 = (t.reshape(-1, cols) for t in (w, g, m, v))
    R = w2.shape[0]
    tr = _tile(R, 512, 8)

    def body(w_ref, g_ref, m_ref, v_ref, d_ref, nm_ref, nv_ref):
        gv = g_ref[...]
        nm = ADAM_B1 * m_ref[...] + (1.0 - ADAM_B1) * gv
        nv = ADAM_B2 * v_ref[...] + (1.0 - ADAM_B2) * (gv * gv)
        m_hat = nm / (1.0 - ADAM_B1 ** ADAM_STEP)
        v_hat = nv / (1.0 - ADAM_B2 ** ADAM_STEP)
        d_ref[...] = -ADAM_LR * (m_hat / (jnp.sqrt(v_hat) + ADAM_EPS) + ADAM_WD * w_ref[...])
        nm_ref[...] = nm
        nv_ref[...] = nv

    spec = pl.BlockSpec((tr, cols), lambda i: (i, 0))
    out = jax.ShapeDtypeStruct((R, cols), F32)
    d, nm, nv = pl.pallas_call(body, name=name, grid=(R // tr,), in_specs=[spec] * 4, out_specs=[spec] * 3,
                               out_shape=[out] * 3, compiler_params=_params(("parallel",)))(w2, g2, m2, v2)
    return d.reshape(shape), nm.reshape(shape), nv.reshape(shape)


def _add_received(own, got, *, name):
    R, Cc = own.shape
    n = got.shape[0]
    tr = _tile(R, 256, 16)

    def body(a_ref, b_ref, o_ref):
        acc = a_ref[...].astype(F32)
        for k in range(n):
            acc = acc + b_ref[k].astype(F32)
        o_ref[...] = acc

    return pl.pallas_call(
        body, name=name, grid=(R // tr,),
        in_specs=[pl.BlockSpec((tr, Cc), lambda i: (i, 0)), pl.BlockSpec((n, tr, Cc), lambda i: (0, i, 0))],
        out_specs=pl.BlockSpec((tr, Cc), lambda i: (i, 0)), out_shape=jax.ShapeDtypeStruct((R, Cc), F32),
        compiler_params=_params(("parallel",)))(own, got)


def _place():
    return lax.axis_index("x"), lax.axis_index("y"), lax.axis_index("c")


def _all_gather(x, *, name, in_vmem, reduce_sum=False, with_token=False):
    R, Cc = x.shape
    space = pltpu.VMEM if in_vmem else pl.ANY

    def body(x_ref, out_ref, *scratch):
        if with_token:
            scratch[0][...] = jnp.zeros_like(scratch[0])
            scratch = scratch[1:]
        if reduce_sum:
            gat_ref, send_sems, recv_sems, local_sem = scratch
        else:
            gat_ref = out_ref
            send_sems, recv_sems, local_sem = scratch
        mx, my, mc = _place()
        me, sibling = (mx, my, mc), (mx, my, 1 - mc)
        chips = [(1 - mx, my), (mx, 1 - my), (1 - mx, 1 - my)]

        def rows(px, py, pc):
            return gat_ref.at[pl.ds((4 * px + 2 * py + pc) * R, R), :]

        def copy(k, block, to, src=None):
            return pltpu.make_async_remote_copy(
                src_ref=rows(*block) if src is None else src, dst_ref=rows(*block), send_sem=send_sems.at[k],
                recv_sem=recv_sems.at[k], device_id=to, device_id_type=MESH_ID)

        mine = pltpu.make_async_copy(x_ref, rows(*me), local_sem)
        mine.start()
        first = [copy(0, me, sibling, src=x_ref)]
        first += [copy(1 + j, me, (*chip, mc), src=x_ref) for j, chip in enumerate(chips)]
        for cp in first:
            cp.start()
        passed = [copy(4 + j, (*chip, mc), sibling) for j, chip in enumerate(chips)]
        for j, chip in enumerate(chips):
            copy(1 + j, (*chip, mc), me).wait_recv()
            passed[j].start()
        copy(0, sibling, me).wait_recv()
        for j, chip in enumerate(chips):
            copy(4 + j, (*chip, 1 - mc), me).wait_recv()
        for cp in first + passed:
            cp.wait_send()
        mine.wait()
        if reduce_sum:
            acc = gat_ref[pl.ds(0, R), :]
            for d in range(1, N_DEV):
                acc = acc + gat_ref[pl.ds(d * R, R), :]
            out_ref[...] = acc

    sems = [pltpu.SemaphoreType.DMA((7,)), pltpu.SemaphoreType.DMA((7,)), pltpu.SemaphoreType.DMA]
    if reduce_sum:
        assert in_vmem
        out_shape = jax.ShapeDtypeStruct((R, Cc), x.dtype)
        scratch = [pltpu.VMEM((N_DEV * R, Cc), x.dtype)] + sems
    else:
        out_shape = jax.ShapeDtypeStruct((N_DEV * R, Cc), x.dtype)
        scratch = sems
    out_specs = pl.BlockSpec(memory_space=space)
    if with_token:
        out_shape = (out_shape, jax.ShapeDtypeStruct((8, LANE), F32))
        out_specs = (out_specs, pl.BlockSpec(memory_space=pltpu.VMEM))
    return pl.pallas_call(
        body, name=name, out_shape=out_shape, in_specs=[pl.BlockSpec(memory_space=space)], out_specs=out_specs,
        scratch_shapes=scratch, compiler_params=pltpu.CompilerParams(vmem_limit_bytes=VMEM_LIMIT))(x)


def _peer(k, mx, my, mc):
    bits = k + 1
    return (1 - mx if bits & 4 else mx, 1 - my if bits & 2 else my, 1 - mc if bits & 1 else mc)


HBM_SPEC = pl.BlockSpec(memory_space=pltpu.HBM)
SEM_SPEC = pl.BlockSpec(memory_space=pltpu.SEMAPHORE)
DATAFLOW = pltpu.SideEffectType.DATAFLOW_SIDE_EFFECTING


def _exchange_copies(x_refs, land_refs, send_sems, recv_sems, scatter):
    mx, my, mc = _place()
    me = 4 * mx + 2 * my + mc
    n = len(x_refs)
    copies = []
    for k in range(N_DEV - 1):
        px, py, pc = _peer(k, mx, my, mc)
        for m, (x_ref, land_ref) in enumerate(zip(x_refs, land_refs)):
            rows = land_ref.shape[1] if scatter else x_ref.shape[0]
            if scatter:
                src = x_ref.at[pl.ds(pl.multiple_of((4 * px + 2 * py + pc) * rows, 16), rows), :]
                dst = land_ref.at[k]
            else:
                src = x_ref
                dst = land_ref.at[pl.ds(pl.multiple_of(me * rows, 16), rows), :]
            copies.append(pltpu.make_async_remote_copy(
                src_ref=src, dst_ref=dst, send_sem=send_sems.at[k * n + m], recv_sem=recv_sems.at[k * n + m],
                device_id=(px, py, pc), device_id_type=MESH_ID))
    return copies


def _land_shape(x, scatter):
    return (N_DEV - 1, x.shape[0] // N_DEV, x.shape[1]) if scatter else (N_DEV * x.shape[0], x.shape[1])


def _own_copies(x_refs, land_refs, local_sems):
    mx, my, mc = _place()
    me = 4 * mx + 2 * my + mc
    return [pltpu.make_async_copy(
        x_ref, land_ref.at[pl.ds(pl.multiple_of(me * x_ref.shape[0], 16), x_ref.shape[0]), :], local_sems.at[m])
        for m, (x_ref, land_ref) in enumerate(zip(x_refs, land_refs))]


def _exchange_start(groups, *, name, scatter):
    sizes = [len(g) for g in groups]
    xs = [x for g in groups for x in g]
    n = len(xs)
    lands = [lax.empty(_land_shape(x, scatter), x.dtype) for x in xs]
    per = 2 if scatter else 3

    def body(*refs):
        sems = refs[2 * n:2 * n + per * len(groups)]
        token = refs[-1]
        off = 0
        for gi, m in enumerate(sizes):
            x_refs, land_refs = refs[off:off + m], refs[n + off:n + off + m]
            for cp in _exchange_copies(x_refs, land_refs, sems[per * gi], sems[per * gi + 1], scatter):
                cp.start()
            if not scatter:
                for cp in _own_copies(x_refs, land_refs, sems[per * gi + 2]):
                    cp.start()
            off += m
        token[...] = jnp.zeros_like(token)

    sem_shapes = []
    for m in sizes:
        sem_shapes += [pltpu.SemaphoreType.DMA(((N_DEV - 1) * m,))] * 2
        if not scatter:
            sem_shapes.append(pltpu.SemaphoreType.DMA((m,)))
    ns = len(sem_shapes)
    out = pl.pallas_call(
        body, name=name,
        out_shape=(*sem_shapes, *[pltpu.HBM(x.shape, x.dtype) for x in xs],
                   *[pltpu.HBM(l.shape, l.dtype) for l in lands], jax.ShapeDtypeStruct((8, LANE), F32)),
        in_specs=(HBM_SPEC,) * (2 * n),
        out_specs=(SEM_SPEC,) * ns + (HBM_SPEC,) * (2 * n) + (pl.BlockSpec(memory_space=pltpu.VMEM),),
        input_output_aliases={i: ns + i for i in range(2 * n)},
        compiler_params=pltpu.CompilerParams(has_side_effects=DATAFLOW))(
            *[pltpu.with_memory_space_constraint(t, pltpu.HBM) for t in xs + lands])
    started, off = [], 0
    for gi, m in enumerate(sizes):
        sems = out[per * gi:per * gi + per]
        started.append((sems[0], sems[1], list(out[ns + off:ns + off + m]),
                        list(out[ns + n + off:ns + n + off + m]), out[-1], None if scatter else sems[2]))
        off += m
    return started


def _exchange_wait(started, after, *, name, scatter):
    send_sems, recv_sems, xs, lands, _, local_sems = started
    n = len(xs)

    def body(*refs):
        x_refs, land_refs = refs[:n], refs[n:2 * n]
        for cp in _exchange_copies(x_refs, land_refs, refs[2 * n], refs[2 * n + 1], scatter):
            cp.wait_send()
            cp.wait_recv()
        if not scatter:
            for cp in _own_copies(x_refs, land_refs, refs[2 * n + 2]):
                cp.wait()

    sems = (send_sems, recv_sems) if scatter else (send_sems, recv_sems, local_sems)
    out = pl.pallas_call(
        body, name=name, out_shape=tuple(pltpu.HBM(t.shape, t.dtype) for t in xs + lands),
        in_specs=(HBM_SPEC,) * (2 * n) + (SEM_SPEC,) * len(sems) + (pl.BlockSpec(memory_space=pl.ANY),),
        out_specs=(HBM_SPEC,) * (2 * n), input_output_aliases={i: i for i in range(2 * n)},
        compiler_params=pltpu.CompilerParams(has_side_effects=DATAFLOW))(*xs, *lands, *sems, after)
    return list(out[:n]), list(out[n:])


def _gather_start(groups, token, *, name):
    first = groups[0]
    groups = [[first[0] + token[0, 0].astype(first[0].dtype)] + list(first[1:])] + [list(g) for g in groups[1:]]
    return _exchange_start(groups, name=name, scatter=False)


def _gather_finish(started, after, *, name):
    return _exchange_wait(started, after, name=name, scatter=False)[1]


def _reduce_start(grads, *, name):
    return _exchange_start([grads], name=name, scatter=True)[0]


def _reduce_finish(started, after, me, *, name):
    sent, gots = _exchange_wait(started, after, name=name + "_wait", scatter=True)
    out = []
    for m, (g, got) in enumerate(zip(sent, gots)):
        rows = g.shape[0] // N_DEV
        own = lax.dynamic_slice(g, (me * rows, 0), (rows, g.shape[1]))
        out.append(_add_received(own, got, name=f"{name}_add{m}"))
    return out


def _pad_rows(a, mult):
    r = (-a.shape[0]) % mult
    return a if r == 0 else jnp.concatenate([a, jnp.zeros((r,) + a.shape[1:], a.dtype)], axis=0)


def kernel(x, mem, mix_norm, mem_norm, w_mem_kv, w_out, hg_w_in, hg_lb, hg_onorm, gm_w_in, gm_ln_g, gm_ln_b, gm_ws, gm_bs, ffn_norm, w_ffn_in, w_ffn_out, final_norm, loss_target, m_mix_norm, m_mem_norm, m_w_mem_kv, m_w_out, m_hg_w_in, m_hg_lb, m_hg_onorm, m_gm_w_in, m_gm_ln_g, m_gm_ln_b, m_gm_ws, m_gm_bs, m_ffn_norm, m_w_ffn_in, m_w_ffn_out, m_final_norm, v_mix_norm, v_mem_norm, v_w_mem_kv, v_w_out, v_hg_w_in, v_hg_lb, v_hg_onorm, v_gm_w_in, v_gm_ln_g, v_gm_ln_b, v_gm_ws, v_gm_bs, v_ffn_norm, v_w_ffn_in, v_w_ffn_out, v_final_norm):
    mx, my, mc = _place()
    me = 4 * mx + 2 * my + mc
    xs = x[0]
    mems = mem[0]
    tgt = loss_target[0]

    hg_t = hg_w_in[0].T.astype(BF16)
    gm_t = gm_w_in[0].T.astype(BF16)
    fi_t = [w_ffn_in[i].T.astype(BF16) for i in range(2)]
    kv_b = [w_mem_kv[i].astype(BF16) for i in range(2)]
    out_b = [w_out[i].astype(BF16) for i in range(2)]
    fo_b = [w_ffn_out[i].astype(BF16) for i in range(2)]
    ln_local = _pad_rows(jnp.concatenate([gm_ln_g, gm_ln_b], axis=0), 8)
    ln_local = jnp.concatenate([ln_local, jnp.zeros((8, LANE - ln_local.shape[1]), F32)], axis=1)
    ln_all, token = _all_gather(ln_local, name="gather_ln", in_vmem=True, with_token=True)
    ln_all = ln_all.reshape(N_DEV, 8, LANE)
    ln_g = ln_all[:, 0, :D_TOK // N_DEV].reshape(1, D_TOK)
    ln_b = ln_all[:, 1, :D_TOK // N_DEV].reshape(1, D_TOK)
    W_hgT, token = _all_gather(hg_t + token[0, 0].astype(BF16), name="gather_first", in_vmem=False,
                               with_token=True)
    gather_mix, fi0, fo0, gather_gm, fi1, fo1 = _gather_start(
        [kv_b + out_b, [fi_t[0]], [fo_b[0]], [gm_t], [fi_t[1]], [fo_b[1]]], token, name="gather_rest_start")
    gather_fi, gather_fo = [fi0, fi1], [fo0, fo1]

    lb_soft = jax.nn.softmax(hg_lb, axis=0)
    lb0 = lb_soft[0:1]
    bsb = jnp.broadcast_to(gm_bs[0][:, :, None], (N_HEADS, GM_CHUNK, GM_CHUNK))
    ws = gm_ws[0]

    W_fiT, W_fo = [], []

    def ffn_fwd(xin, hf, i, **tail):
        W_fiT.extend(_gather_finish(gather_fi[i], hf, name=f"gather_fi{i}_wait"))
        gu, act = _ffn_in(hf, W_fiT[i], name=f"ffn_in{i}")
        W_fo.extend(_gather_finish(gather_fo[i], act, name=f"gather_fo{i}_wait"))
        return gu, act, _matmul(act, W_fo[i], res=xin, name=f"ffn_out{i}", **tail)

    h0 = _rms_fwd(xs, mix_norm[0:1], name="mix_norm0", dep=gather_mix[4])
    p0 = _matmul(h0, W_hgT, tb=True, name="hg_in")
    heads0, o0, states = _hgrn2_fwd(p0, lb0, hg_onorm, name="hgrn2_fwd")

    kv0, kv1, wo0, wo1 = _gather_finish(gather_mix, o0, name="gather_mix_wait")
    W_kv, W_out = [kv0, kv1], [wo0, wo1]
    mem_n, kv = [], []
    for i in range(2):
        mn = _rms_fwd(mems, mem_norm[i:i + 1], name=f"mem_norm{i}")
        mem_n.append(mn)
        kv.append(_matmul(mn, W_kv[i], name=f"mem_kv{i}"))

    heads0 = _attn_fwd(p0, 4 * D_TOK // D_MEM, kv[0], heads0, name="attn_fwd0")
    x1, hf0 = _matmul(heads0, W_out[0], res=xs, norm_gain=ffn_norm[0:1], name="out_proj0")
    gu0, act0, (x2, h1) = ffn_fwd(x1, hf0, 0, norm_gain=mix_norm[1:2])

    W_gmT, = _gather_finish(gather_gm, h1, name="gather_gm_wait")
    p1 = _matmul(h1, W_gmT, tb=True, name="gm_in")
    heads1 = _gmlp_fwd(p1, ln_g, ln_b, ws, bsb, name="gmlp_fwd")
    heads1 = _attn_fwd(p1, 2 * D_TOK // D_MEM, kv[1], heads1, name="attn_fwd1")
    x3, hf1 = _matmul(heads1, W_out[1], res=x2, norm_gain=ffn_norm[1:2], name="out_proj1")
    gu1, act1, (dx, g_final, loss_part) = ffn_fwd(x3, hf1, 1, loss_head=(final_norm.reshape(1, D_MODEL), tgt))

    def ffn_bwd(dx, xin, hf, gu, act, i, dep):
        dgu = _ffn_out_dx(dx, W_fo[i], gu, dep, name=f"ffn_out_dx{i}")
        g_wfo = _matmul(act, dx, ta=True, out_dtype=BF16, name=f"ffn_out_dw{i}")
        g_wfi_t = _matmul(dgu, hf, ta=True, a_halves=True, out_dtype=BF16, name=f"ffn_in_dw{i}")
        dx, g_norm = _matmul(dgu, W_fiT[i], a_halves=True, res=dx, norm_bwd=(xin, ffn_norm[i:i + 1]),
                             name=f"ffn_in_dx{i}")
        return dx, g_wfi_t, g_wfo, g_norm

    def mem_bwd(dkv, i):
        g_wkv = _matmul(mem_n[i], dkv, ta=True, out_dtype=BF16, name=f"mem_kv_dw{i}")
        dmn = _matmul(dkv, W_kv[i], tb=True, name=f"mem_kv_dx{i}")
        _, g_norm = _rms_bwd(mems, mem_norm[i:i + 1], dmn, jnp.zeros_like(mems), name=f"mem_norm_bwd{i}")
        return g_wkv, g_norm

    dx, g_wfi1_t, g_wfo1, g_ffn1 = ffn_bwd(dx, x3, hf1, gu1, act1, 1, loss_part)
    dheads = _matmul(dx, W_out[1], tb=True, name="out_proj_dx1")
    g_wout1 = _matmul(heads1, dx, ta=True, out_dtype=BF16, name="out_proj_dw1")
    dp, g_ws, g_bs, g_lng, g_lnb = _gmlp_bwd(p1, ln_g, ln_b, ws, bsb, dheads, name="gmlp_bwd")
    dp, dk, dv = _attn_bwd(p1, 2 * D_TOK // D_MEM, kv[1], dheads, dp, name="attn_bwd1")
    g_wkv1, g_mem1 = mem_bwd(jnp.concatenate([dk, dv], axis=1), 1)
    g_wgm_t = _matmul(dp, h1, ta=True, out_dtype=BF16, name="gm_in_dw")
    dx, g_mix1 = _matmul(dp, W_gmT, res=dx, norm_bwd=(x2, mix_norm[1:2]), name="gm_in_dx")
    reduce_l1 = _reduce_start([g_wkv1, g_wout1, g_wgm_t, g_wfi1_t, g_wfo1], name="reduce_l1_start")

    dx, g_wfi0_t, g_wfo0, g_ffn0 = ffn_bwd(dx, x1, hf0, gu0, act0, 0, reduce_l1[4])
    reduce_ffn0 = _reduce_start([g_wfi0_t, g_wfo0], name="reduce_ffn0_start")
    dheads = _matmul(dx, W_out[0], tb=True, name="out_proj_dx0", dep=reduce_ffn0[4])
    g_wout0 = _matmul(heads0, dx, ta=True, out_dtype=BF16, name="out_proj_dw0")
    dp, g_lb0, g_onorm = _hgrn2_bwd(p0, lb0, hg_onorm, o0, states, dheads, name="hgrn2_bwd")
    dp, dk, dv = _attn_bwd(p0, 4 * D_TOK // D_MEM, kv[0], dheads, dp, name="attn_bwd0")
    g_wkv0, g_mem0 = mem_bwd(jnp.concatenate([dk, dv], axis=1), 0)
    g_whg_t = _matmul(dp, h0, ta=True, out_dtype=BF16, name="hg_in_dw")
    reduce_mix0 = _reduce_start([g_wkv0, g_wout0, g_whg_t], name="reduce_mix0_start")
    grad_x, g_mix0 = _matmul(dp, W_hgT, res=dx, norm_bwd=(xs, mix_norm[0:1]), name="hg_in_dx", dep=reduce_mix0[4])

    g_kv1, g_out1, g_gm_t, g_fi1_t, g_fo1 = _reduce_finish(reduce_l1, grad_x, me, name="reduce_l1")
    g_fi0_t, g_fo0 = _reduce_finish(reduce_ffn0, g_kv1, me, name="reduce_ffn0")
    g_kv0, g_out0, g_hg_t = _reduce_finish(reduce_mix0, g_fi0_t, me, name="reduce_mix0")
    g_shards = [jnp.stack([g_kv0, g_kv1]), jnp.stack([g_out0, g_out1]), g_hg_t[None], g_gm_t[None],
                jnp.stack([g_fi0_t, g_fi1_t]), jnp.stack([g_fo0, g_fo1])]
    transposed = (4, 7, 13)

    small = [loss_part, jnp.concatenate([g_mix0, g_mix1], axis=1), jnp.concatenate([g_mem0, g_mem1], axis=1),
             g_lb0, g_onorm, g_lng, g_lnb, g_ws.reshape(1, -1), g_bs.reshape(1, -1),
             jnp.concatenate([g_ffn0, g_ffn1], axis=1), g_final]
    sizes = [t.shape[1] for t in small]
    small_rows = _pad_rows(jnp.concatenate(small, axis=1).reshape(-1, LANE), 8)
    red = _all_gather(small_rows, name="reduce_small", in_vmem=True, reduce_sum=True).reshape(-1)
    pieces, off = [], 0
    for n in sizes:
        pieces.append(red[off:off + n])
        off += n
    loss = pieces[0][0]
    g_mix_norm = pieces[1].reshape(2, D_MODEL)
    g_mem_norm = pieces[2].reshape(2, D_MODEL)
    g_hg_lb = pieces[3][None, :] * lb0 * (jnp.eye(3, dtype=F32)[:, 0:1] - lb_soft)
    g_hg_onorm = pieces[4].reshape(1, D_TOK)
    width = D_TOK // N_DEV
    g_gm_ln_g = lax.dynamic_slice(pieces[5], (me * width,), (width,)).reshape(1, width)
    g_gm_ln_b = lax.dynamic_slice(pieces[6], (me * width,), (width,)).reshape(1, width)
    g_gm_ws = pieces[7].reshape(gm_ws.shape)
    g_gm_bs = pieces[8].reshape(gm_bs.shape)
    g_ffn_norm = pieces[9].reshape(2, D_MODEL)
    g_final_norm = pieces[10]

    grads = [g_mix_norm, g_mem_norm, g_shards[0], g_shards[1], g_shards[2], g_hg_lb, g_hg_onorm, g_shards[3],
             g_gm_ln_g, g_gm_ln_b, g_gm_ws, g_gm_bs, g_ffn_norm, g_shards[4], g_shards[5], g_final_norm]
    weights = [mix_norm, mem_norm, w_mem_kv, w_out, hg_w_in, hg_lb, hg_onorm, gm_w_in, gm_ln_g, gm_ln_b, gm_ws, gm_bs,
               ffn_norm, w_ffn_in, w_ffn_out, final_norm]
    ms = [m_mix_norm, m_mem_norm, m_w_mem_kv, m_w_out, m_hg_w_in, m_hg_lb, m_hg_onorm, m_gm_w_in, m_gm_ln_g,
          m_gm_ln_b, m_gm_ws, m_gm_bs, m_ffn_norm, m_w_ffn_in, m_w_ffn_out, m_final_norm]
    vs = [v_mix_norm, v_mem_norm, v_w_mem_kv, v_w_out, v_hg_w_in, v_hg_lb, v_hg_onorm, v_gm_w_in, v_gm_ln_g,
          v_gm_ln_b, v_gm_ws, v_gm_bs, v_ffn_norm, v_w_ffn_in, v_w_ffn_out, v_final_norm]
    deltas, new_m, new_v = [], [], []
    for n, (w, g, m, v) in enumerate(zip(weights, grads, ms, vs)):
        if w.ndim == 1:
            d, nm, nv = _adamw(w[None], g.reshape(1, -1), m[None], v[None], name=f"adamw{n}")
            d, nm, nv = d[0], nm[0], nv[0]
        elif n in transposed:
            flip = lambda t: jnp.swapaxes(t, 1, 2)
            d, nm, nv = (flip(t) for t in _adamw(flip(w), g, flip(m), flip(v), name=f"adamw{n}"))
            grads[n] = flip(g)
        else:
            d, nm, nv = _adamw(w, g.reshape(w.shape), m, v, name=f"adamw{n}")
        deltas.append(d)
        new_m.append(nm)
        new_v.append(nv)
    grads = [g.reshape(w.shape) for g, w in zip(grads, weights)]
    return (loss, grad_x[None], *grads, *deltas, *new_m, *new_v)
```

```python
import jax
import jax.numpy as jnp
from jax import lax
from jax.experimental import pallas as pl
from jax.experimental.pallas import tpu as pltpu

F32 = jnp.float32
BF16 = jnp.bfloat16
MXU_DTYPE = jnp.bfloat16
MESH_ID = pl.DeviceIdType.MESH

N_DEV = 8
EPS = 1e-6
D_MODEL = 1024
D_TOK = 768
D_MEM = 256
N_HEADS = 6
HEAD = 128
MEM_HEADS = 4
MEM_HDIM = 64
GM_CHUNK = 128
D_FF = 2816
HG_SUB = 16
HG_GROUP = 6
HG_IN = 4 * D_TOK + D_MEM
GM_IN = 2 * D_TOK + D_MEM
LANE = 128
MXU_COLS = 256

ADAM_LR = 0.001
ADAM_B1 = 0.9
ADAM_B2 = 0.999
ADAM_EPS = 1e-08
ADAM_WD = 0.01
ADAM_STEP = 10

VMEM_LIMIT = 48 * 2 ** 20
VMEM_LIMIT_WIDE = 58 * 2 ** 20


def _params(sem=None, limit=VMEM_LIMIT):
    return pltpu.CompilerParams(dimension_semantics=sem, vmem_limit_bytes=limit)


def _tile(n, cap, q=LANE):
    if n <= cap:
        return n
    best = None
    for t in range(q, cap + 1, q):
        if n % t == 0:
            best = t
    assert best is not None, (n, cap, q)
    return best


def _sigmoid(x):
    return 1.0 / (1.0 + jnp.exp(-x))


def _gelu(x, with_grad=False):
    cdf = 0.5 * (1.0 + lax.erf(x * 0.7071067811865476))
    if not with_grad:
        return x * cdf
    return x * cdf, cdf + x * jnp.exp(-0.5 * x * x) * 0.3989422804014327


def _matmul(a, b, *, name, ta=False, tb=False, res=None, out_dtype=F32, a_halves=False, b_halves=False, dep=None,
            norm_gain=None, norm_bwd=None, loss_head=None):
    if a_halves and ta:
        K, M = a.shape[1], 2 * a.shape[2]
    elif a_halves:
        M, K = a.shape[1], 2 * a.shape[2]
    else:
        K, M = a.shape if ta else a.shape[::-1]
    if b_halves:
        assert not tb and b.shape[1] == K
        N = 2 * b.shape[2]
    else:
        N = b.shape[0] if tb else b.shape[1]
        assert (b.shape[1] if tb else b.shape[0]) == K
    tm = _tile(M // 2 if (a_halves and ta) else M, 1664 if ta else 1024)
    tn = _tile(N // 2 if b_halves else N, 1792)
    tk = _tile(K // 2 if (a_halves and not ta) else K, 1024 if ta else 1664)
    nk = K // tk
    dims = (((0 if ta else 1,), (1 if tb else 0,)), ((), ()))

    strips = norm_bwd is not None or loss_head is not None
    fused = norm_gain is not None or strips
    n_in = 2 + (res is not None) + (norm_gain is not None) + 2 * strips + (dep is not None)
    if fused:
        assert tn == N, "the fused norm needs whole rows"
        assert (norm_gain is not None) + (norm_bwd is not None) + (loss_head is not None) == 1
        assert not strips or (res is not None and nk > 1 and tm % LANE == 0)

    def body(*refs):
        a_ref, b_ref = refs[:2]
        r_ref = refs[2] if res is not None else None
        g_ref = refs[2 + (res is not None)] if fused else None
        x_ref = refs[3 + (res is not None)] if strips else None
        o_ref = refs[n_in]
        h_ref = refs[n_in + 1] if fused else None
        l_ref = refs[n_in + 2] if loss_head is not None else None
        acc = None if nk == 1 else refs[-1]
        k = pl.program_id(2)

        def product():
            return lax.dot_general(a_ref[...].astype(MXU_DTYPE), b_ref[...].astype(MXU_DTYPE), dims,
                                   preferred_element_type=F32)

        def finish(r):
            if loss_head is not None:
                @pl.when(pl.program_id(0) == 0)
                def _():
                    h_ref[...] = jnp.zeros_like(h_ref)
                    l_ref[...] = jnp.zeros_like(l_ref)

                acc[...] = r + r_ref[...]
                gv = g_ref[...]

                def strip(s, carry):
                    dg, loss = carry
                    rows = pl.ds(pl.multiple_of(s * LANE, LANE), LANE)
                    xv = acc[rows, :]
                    scale = lax.rsqrt(jnp.mean(xv * xv, axis=-1, keepdims=True) + EPS)
                    xh = xv * scale
                    err = xh * gv - x_ref[rows, :]
                    loss = loss + 0.5 * jnp.sum(jnp.mean(err * err, axis=-1, keepdims=True), axis=0, keepdims=True)
                    dy = err * (1.0 / N)
                    u = dy * gv
                    o_ref[rows, :] = scale * (u - xh * jnp.mean(u * xh, axis=-1, keepdims=True))
                    return dg + jnp.sum(dy * xh, axis=0, keepdims=True), loss

                dg, loss = lax.fori_loop(0, tm // LANE, strip, (jnp.zeros((1, N), F32), jnp.zeros((1, 1), F32)))
                h_ref[...] += dg
                l_ref[...] += jnp.broadcast_to(loss, l_ref.shape)
                return
            if norm_bwd is not None:
                @pl.when(pl.program_id(0) == 0)
                def _():
                    h_ref[...] = jnp.zeros_like(h_ref)

                acc[...] = r
                gv = g_ref[...]

                def strip(s, dg):
                    rows = pl.ds(pl.multiple_of(s * LANE, LANE), LANE)
                    rv = acc[rows, :]
                    xv = x_ref[rows, :]
                    scale = lax.rsqrt(jnp.mean(xv * xv, axis=-1, keepdims=True) + EPS)
                    xh = xv * scale
                    u = rv * gv
                    o_ref[rows, :] = r_ref[rows, :] + scale * (u - xh * jnp.mean(u * xh, axis=-1, keepdims=True))
                    return dg + jnp.sum(rv * xh, axis=0, keepdims=True)

                h_ref[...] += lax.fori_loop(0, tm // LANE, strip, jnp.zeros((1, N), F32))
                return
            if res is not None:
                r = r + r_ref[...].astype(F32)
            o_ref[...] = r.astype(out_dtype)
            if norm_gain is not None:
                scale = lax.rsqrt(jnp.mean(r * r, axis=-1, keepdims=True) + EPS)
                h_ref[...] = (r * scale * g_ref[...]).astype(h_ref.dtype)

        if nk == 1:
            finish(product())
            return

        @pl.when(k == 0)
        def _():
            acc[...] = product()

        @pl.when((k > 0) & (k < nk - 1))
        def _():
            acc[...] += product()

        @pl.when(k == nk - 1)
        def _():
            finish(acc[...] + product())

    if a_halves and ta:
        mh = M // 2 // tm
        a_spec = pl.BlockSpec((None, tk, tm), lambda i, j, k: (i // mh, k, i % mh))
    elif a_halves:
        kh = nk // 2
        a_spec = pl.BlockSpec((None, tm, tk), lambda i, j, k: (k // kh, i, k % kh))
    elif ta:
        a_spec = pl.BlockSpec((tk, tm), lambda i, j, k: (k, i))
    else:
        a_spec = pl.BlockSpec((tm, tk), lambda i, j, k: (i, k))
    if b_halves:
        nh = N // 2 // tn
        b_spec = pl.BlockSpec((None, tk, tn), lambda i, j, k: (j // nh, k, j % nh))
    elif tb:
        b_spec = pl.BlockSpec((tn, tk), lambda i, j, k: (j, k))
    else:
        b_spec = pl.BlockSpec((tk, tn), lambda i, j, k: (k, j))
    o_spec = pl.BlockSpec((tm, tn), lambda i, j, k: (i, j))
    in_specs = [a_spec, b_spec] + ([o_spec] if res is not None else [])
    args = (a, b) + ((res,) if res is not None else ())
    out_specs, out_shape = o_spec, jax.ShapeDtypeStruct((M, N), out_dtype)
    vec = pl.BlockSpec((1, N), lambda i, j, k: (0, 0))
    sem = ("parallel", "parallel", "arbitrary")
    if norm_gain is not None:
        in_specs.append(vec)
        args += (norm_gain,)
        out_specs, out_shape = [o_spec, o_spec], [out_shape, jax.ShapeDtypeStruct((M, N), BF16)]
    if norm_bwd is not None:
        x_in, gain = norm_bwd
        in_specs += [vec, o_spec]
        args += (gain, x_in)
        out_specs, out_shape = [o_spec, vec], [out_shape, jax.ShapeDtypeStruct((1, N), F32)]
        sem = ("arbitrary", "arbitrary", "arbitrary")
    if loss_head is not None:
        gain, target = loss_head
        in_specs += [vec, o_spec]
        args += (gain, target)
        one = pl.BlockSpec((1, LANE), lambda i, j, k: (0, 0))
        out_specs = [o_spec, vec, one]
        out_shape = [out_shape, jax.ShapeDtypeStruct((1, N), F32), jax.ShapeDtypeStruct((1, LANE), F32)]
        sem = ("arbitrary", "arbitrary", "arbitrary")
    if dep is not None:
        in_specs.append(pl.BlockSpec(memory_space=pl.ANY))
        args += (dep,)
    return pl.pallas_call(
        body, name=name, grid=(M // tm, N // tn, nk), in_specs=in_specs, out_specs=out_specs, out_shape=out_shape,
        scratch_shapes=[] if nk == 1 else [pltpu.VMEM((tm, tn), F32)],
        compiler_params=_params(sem, VMEM_LIMIT_WIDE if strips else VMEM_LIMIT))(*args)


def _ffn_in(hf, wt, *, name):
    S, K = hf.shape
    tm = _tile(S, 512)
    tn = _tile(D_FF, 1408)
    nh = D_FF // tn
    nt = (((1,), (1,)), ((), ()))

    def body(a_ref, bg_ref, bu_ref, gu_ref, act_ref):
        av = a_ref[...].astype(MXU_DTYPE)
        for c0 in range(0, tn, MXU_COLS):
            cs = slice(c0, min(c0 + MXU_COLS, tn))
            gate = lax.dot_general(av, bg_ref[cs, :].astype(MXU_DTYPE), nt, preferred_element_type=F32)
            up = lax.dot_general(av, bu_ref[cs, :].astype(MXU_DTYPE), nt, preferred_element_type=F32)
            gu_ref[0, :, cs] = gate.astype(gu_ref.dtype)
            gu_ref[1, :, cs] = up.astype(gu_ref.dtype)
            act_ref[:, cs] = (gate * _sigmoid(gate) * up).astype(act_ref.dtype)

    return pl.pallas_call(
        body, name=name, grid=(nh, S // tm),
        in_specs=[pl.BlockSpec((tm, K), lambda j, i: (i, 0)), pl.BlockSpec((tn, K), lambda j, i: (j, 0)),
                  pl.BlockSpec((tn, K), lambda j, i: (j + nh, 0))],
        out_specs=[pl.BlockSpec((2, tm, tn), lambda j, i: (0, i, j)), pl.BlockSpec((tm, tn), lambda j, i: (i, j))],
        out_shape=[jax.ShapeDtypeStruct((2, S, D_FF), BF16), jax.ShapeDtypeStruct((S, D_FF), BF16)],
        compiler_params=_params(("parallel", "parallel")))(hf, wt, wt)


def _ffn_out_dx(dx, w, gu, dep, *, name):
    S, K = dx.shape
    tm = _tile(S, 1024)
    tn = _tile(D_FF, 1408)

    def body(a_ref, b_ref, gu_ref, dep_ref, o_ref):
        del dep_ref
        av = a_ref[...].astype(MXU_DTYPE)
        for c0 in range(0, tn, MXU_COLS):
            cs = slice(c0, min(c0 + MXU_COLS, tn))
            da = lax.dot_general(av, b_ref[cs, :].astype(MXU_DTYPE), (((1,), (1,)), ((), ())),
                                 preferred_element_type=F32)
            gate = gu_ref[0, :, cs].astype(F32)
            up = gu_ref[1, :, cs].astype(F32)
            sg = _sigmoid(gate)
            o_ref[0, :, cs] = (da * up * sg * (1.0 + gate * (1.0 - sg))).astype(o_ref.dtype)
            o_ref[1, :, cs] = (da * gate * sg).astype(o_ref.dtype)

    halves = pl.BlockSpec((2, tm, tn), lambda i, j: (0, i, j))
    return pl.pallas_call(
        body, name=name, grid=(S // tm, D_FF // tn),
        in_specs=[pl.BlockSpec((tm, K), lambda i, j: (i, 0)), pl.BlockSpec((tn, K), lambda i, j: (j, 0)), halves,
                  pl.BlockSpec(memory_space=pl.ANY)],
        out_specs=halves, out_shape=jax.ShapeDtypeStruct((2, S, D_FF), BF16),
        compiler_params=_params(("parallel", "parallel")))(dx, w, gu, dep)


def _rms_fwd(x, g, *, name, dep=None):
    R, Dm = x.shape
    tr = _tile(R, 512, 8)

    def body(x_ref, g_ref, *rest):
        o_ref = rest[-1]
        xv = x_ref[...]
        r = lax.rsqrt(jnp.mean(xv * xv, axis=-1, keepdims=True) + EPS)
        o_ref[...] = (xv * r * g_ref[...]).astype(o_ref.dtype)

    in_specs = [pl.BlockSpec((tr, Dm), lambda i: (i, 0)), pl.BlockSpec((1, Dm), lambda i: (0, 0))]
    args = (x, g)
    if dep is not None:
        in_specs.append(pl.BlockSpec(memory_space=pl.ANY))
        args += (dep,)
    return pl.pallas_call(
        body, name=name, grid=(R // tr,), in_specs=in_specs,
        out_specs=pl.BlockSpec((tr, Dm), lambda i: (i, 0)), out_shape=jax.ShapeDtypeStruct((R, Dm), BF16),
        compiler_params=_params(("parallel",)))(*args)


def _rms_bwd(x, g, dh, dres, *, name):
    R, Dm = x.shape
    tr = _tile(R, 256, 8)

    def body(x_ref, g_ref, dh_ref, dres_ref, dx_ref, dg_ref):
        @pl.when(pl.program_id(0) == 0)
        def _():
            dg_ref[...] = jnp.zeros_like(dg_ref)

        xv = x_ref[...]
        r = lax.rsqrt(jnp.mean(xv * xv, axis=-1, keepdims=True) + EPS)
        xh = xv * r
        dhv = dh_ref[...].astype(F32)
        dg_ref[...] += jnp.sum(dhv * xh, axis=0, keepdims=True)
        u = dhv * g_ref[...]
        dx = r * (u - xh * jnp.mean(u * xh, axis=-1, keepdims=True))
        dx_ref[...] = dres_ref[...] + dx

    row = pl.BlockSpec((tr, Dm), lambda i: (i, 0))
    vec = pl.BlockSpec((1, Dm), lambda i: (0, 0))
    return pl.pallas_call(
        body, name=name, grid=(R // tr,), in_specs=[row, vec, row, row], out_specs=[row, vec],
        out_shape=[jax.ShapeDtypeStruct((R, Dm), F32), jax.ShapeDtypeStruct((1, Dm), F32)],
        compiler_params=_params(("arbitrary",)))(x, g, dh, dres)


def _head_mask(h):
    lane = lax.broadcasted_iota(jnp.int32, (1, D_MEM), 1)
    return (lane >= h * MEM_HDIM) & (lane < (h + 1) * MEM_HDIM)


def _attn_probs(qv, k_mx, mask):
    s = lax.dot_general(jnp.where(mask, qv, 0.0).astype(MXU_DTYPE), k_mx, (((1,), (1,)), ((), ())),
                        preferred_element_type=F32) * (MEM_HDIM ** -0.5)
    e = jnp.exp(s - jnp.max(s, axis=-1, keepdims=True))
    return e / jnp.sum(e, axis=-1, keepdims=True)


def _attn_fwd(p, qcol, kv, heads, *, name):
    S = p.shape[0]
    M = kv.shape[0]
    ts = _tile(S, 512, 8)

    def body(q_ref, k_ref, v_ref, heads_in, o_ref):
        del heads_in
        qv = q_ref[...]
        kx = k_ref[...].astype(MXU_DTYPE)
        vv = v_ref[...]
        out = jnp.zeros((ts, D_MEM), F32)
        for h in range(MEM_HEADS):
            mask = _head_mask(h)
            pr = _attn_probs(qv, kx, mask)
            out = out + jnp.dot(pr.astype(MXU_DTYPE), jnp.where(mask, vv, 0.0).astype(MXU_DTYPE),
                                preferred_element_type=F32)
        o_ref[...] = out.astype(o_ref.dtype)

    return pl.pallas_call(
        body, name=name, grid=(S // ts,),
        in_specs=[pl.BlockSpec((ts, D_MEM), lambda i: (i, qcol)), pl.BlockSpec((M, D_MEM), lambda i: (0, 0)),
                  pl.BlockSpec((M, D_MEM), lambda i: (0, 1)), pl.BlockSpec(memory_space=pl.ANY)],
        out_specs=pl.BlockSpec((ts, D_MEM), lambda i: (i, D_TOK // D_MEM)),
        out_shape=jax.ShapeDtypeStruct(heads.shape, heads.dtype), input_output_aliases={3: 0},
        compiler_params=_params(("parallel",)))(p, kv, kv, heads)


def _attn_bwd(p, qcol, kv, dheads, dp, *, name):
    S = p.shape[0]
    M = kv.shape[0]
    ts = _tile(S, 512, 8)
    scale = MEM_HDIM ** -0.5

    def body(q_ref, k_ref, v_ref, do_ref, dp_in, dq_ref, dk_ref, dv_ref):
        del dp_in

        @pl.when(pl.program_id(0) == 0)
        def _():
            dk_ref[...] = jnp.zeros_like(dk_ref)
            dv_ref[...] = jnp.zeros_like(dv_ref)

        qv = q_ref[...]
        kv_ = k_ref[...]
        kx = kv_.astype(MXU_DTYPE)
        vv = v_ref[...]
        dox = do_ref[...].astype(MXU_DTYPE)
        qx = qv.astype(MXU_DTYPE)
        dq = jnp.zeros((ts, D_MEM), F32)
        for h in range(MEM_HEADS):
            mask = _head_mask(h)
            pr = _attn_probs(qv, kx, mask)
            vh = jnp.where(mask, vv, 0.0).astype(MXU_DTYPE)
            dpr = lax.dot_general(dox, vh, (((1,), (1,)), ((), ())), preferred_element_type=F32)
            ds = (pr * (dpr - jnp.sum(dpr * pr, axis=-1, keepdims=True)) * scale).astype(MXU_DTYPE)
            dq = dq + jnp.dot(ds, jnp.where(mask, kv_, 0.0).astype(MXU_DTYPE), preferred_element_type=F32)
            dk_h = lax.dot_general(ds, qx, (((0,), (0,)), ((), ())), preferred_element_type=F32)
            dv_h = lax.dot_general(pr.astype(MXU_DTYPE), dox, (((0,), (0,)), ((), ())), preferred_element_type=F32)
            dk_ref[...] += jnp.where(mask, dk_h, 0.0)
            dv_ref[...] += jnp.where(mask, dv_h, 0.0)
        dq_ref[...] = dq.astype(dq_ref.dtype)

    return pl.pallas_call(
        body, name=name, grid=(S // ts,),
        in_specs=[pl.BlockSpec((ts, D_MEM), lambda i: (i, qcol)), pl.BlockSpec((M, D_MEM), lambda i: (0, 0)),
                  pl.BlockSpec((M, D_MEM), lambda i: (0, 1)),
                  pl.BlockSpec((ts, D_MEM), lambda i: (i, D_TOK // D_MEM)), pl.BlockSpec(memory_space=pl.ANY)],
        out_specs=[pl.BlockSpec((ts, D_MEM), lambda i: (i, qcol)), pl.BlockSpec((M, D_MEM), lambda i: (0, 0)),
                   pl.BlockSpec((M, D_MEM), lambda i: (0, 0))],
        out_shape=[jax.ShapeDtypeStruct(dp.shape, dp.dtype), jax.ShapeDtypeStruct((M, D_MEM), F32),
                   jax.ShapeDtypeStruct((M, D_MEM), F32)],
        input_output_aliases={4: 0}, compiler_params=_params(("arbitrary",)))(p, kv, kv, dheads, dp)


def _gm_forward_parts(u_ref, v_ref, lng_ref, lnb_ref, w_ref, bsb_ref, with_grad=False):
    if with_grad:
        (zu, du_gelu), (zv, dv_gelu) = _gelu(u_ref[...], True), _gelu(v_ref[...], True)
    else:
        zu, zv, du_gelu, dv_gelu = _gelu(u_ref[...]), _gelu(v_ref[...]), None, None
    mu = jnp.mean(zv, axis=-1, keepdims=True)
    cen = zv - mu
    rs = lax.rsqrt(jnp.mean(cen * cen, axis=-1, keepdims=True) + EPS)
    vh = cen * rs
    vn = vh * lng_ref[...] + lnb_ref[...]
    row = lax.broadcasted_iota(jnp.int32, (GM_CHUNK, GM_CHUNK), 0)
    col = lax.broadcasted_iota(jnp.int32, (GM_CHUNK, GM_CHUNK), 1)
    tril = row >= col
    wm = [jnp.where(tril, w_ref[g], 0.0).astype(MXU_DTYPE) for g in range(N_HEADS)]
    vnx = [vn[:, g * HEAD:(g + 1) * HEAD].astype(MXU_DTYPE) for g in range(N_HEADS)]
    sv = [jnp.dot(wm[g], vnx[g], preferred_element_type=F32) + bsb_ref[g] for g in range(N_HEADS)]
    return zu, vh, rs, wm, vnx, sv, tril, du_gelu, dv_gelu


def _gmlp_fwd(p, lng, lnb, ws, bsb, *, name):
    S = p.shape[0]

    def body(u_ref, v_ref, lng_ref, lnb_ref, w_ref, bsb_ref, o_ref):
        zu, _, _, _, _, sv, _, _, _ = _gm_forward_parts(u_ref, v_ref, lng_ref, lnb_ref, w_ref, bsb_ref)
        for g in range(N_HEADS):
            o_ref[:, g * HEAD:(g + 1) * HEAD] = (zu[:, g * HEAD:(g + 1) * HEAD] * sv[g]).astype(o_ref.dtype)

    blk = lambda c: pl.BlockSpec((GM_CHUNK, D_TOK), lambda i: (i, c))
    vec = pl.BlockSpec((1, D_TOK), lambda i: (0, 0))
    cube = pl.BlockSpec((N_HEADS, GM_CHUNK, GM_CHUNK), lambda i: (0, 0, 0))
    return pl.pallas_call(
        body, name=name, grid=(S // GM_CHUNK,), in_specs=[blk(0), blk(1), vec, vec, cube, cube],
        out_specs=blk(0), out_shape=jax.ShapeDtypeStruct((S, D_MODEL), BF16),
        compiler_params=_params(("parallel",)))(p, p, lng, lnb, ws, bsb)


def _gmlp_bwd(p, lng, lnb, ws, bsb, dheads, *, name):
    S = p.shape[0]

    def body(u_ref, v_ref, lng_ref, lnb_ref, w_ref, bsb_ref, dt_ref, dp_ref, dw_ref, dbs_ref, dlg_ref, dlb_ref):
        @pl.when(pl.program_id(0) == 0)
        def _():
            dw_ref[...] = jnp.zeros_like(dw_ref)
            dbs_ref[...] = jnp.zeros_like(dbs_ref)
            dlg_ref[...] = jnp.zeros_like(dlg_ref)
            dlb_ref[...] = jnp.zeros_like(dlb_ref)

        zu, vh, rs, wm, vnx, sv, tril, du_gelu, dv_gelu = _gm_forward_parts(
            u_ref, v_ref, lng_ref, lnb_ref, w_ref, bsb_ref, with_grad=True)
        dt = dt_ref[...].astype(F32)
        dvn_parts = []
        for g in range(N_HEADS):
            sl = slice(g * HEAD, (g + 1) * HEAD)
            dsv = dt[:, sl] * zu[:, sl]
            dp_ref[:, sl] = (dt[:, sl] * sv[g] * du_gelu[:, sl]).astype(dp_ref.dtype)
            dsx = dsv.astype(MXU_DTYPE)
            dw = lax.dot_general(dsx, vnx[g], (((1,), (1,)), ((), ())), preferred_element_type=F32)
            dw_ref[g] += jnp.where(tril, dw, 0.0)
            dbs_ref[g] += jnp.sum(dsv, axis=-1, keepdims=True)
            dvn_parts.append(lax.dot_general(wm[g], dsx, (((0,), (0,)), ((), ())), preferred_element_type=F32))
        dvn = jnp.concatenate(dvn_parts, axis=-1)
        dlg_ref[...] += jnp.sum(dvn * vh, axis=0, keepdims=True)
        dlb_ref[...] += jnp.sum(dvn, axis=0, keepdims=True)
        dvh = dvn * lng_ref[...]
        dzv = rs * (dvh - jnp.mean(dvh, axis=-1, keepdims=True) - vh * jnp.mean(dvh * vh, axis=-1, keepdims=True))
        dp_ref[:, D_TOK:] = (dzv * dv_gelu).astype(dp_ref.dtype)

    blk = lambda c: pl.BlockSpec((GM_CHUNK, D_TOK), lambda i: (i, c))
    vec = pl.BlockSpec((1, D_TOK), lambda i: (0, 0))
    cube = pl.BlockSpec((N_HEADS, GM_CHUNK, GM_CHUNK), lambda i: (0, 0, 0))
    col = pl.BlockSpec((N_HEADS, GM_CHUNK, 1), lambda i: (0, 0, 0))
    return pl.pallas_call(
        body, name=name, grid=(S // GM_CHUNK,), in_specs=[blk(0), blk(1), vec, vec, cube, cube, blk(0)],
        out_specs=[pl.BlockSpec((GM_CHUNK, 2 * D_TOK), lambda i: (i, 0)), cube, col, vec, vec],
        out_shape=[jax.ShapeDtypeStruct((S, GM_IN), BF16), jax.ShapeDtypeStruct((N_HEADS, GM_CHUNK, GM_CHUNK), F32),
                   jax.ShapeDtypeStruct((N_HEADS, GM_CHUNK, 1), F32), jax.ShapeDtypeStruct((1, D_TOK), F32),
                   jax.ShapeDtypeStruct((1, D_TOK), F32)],
        compiler_params=_params(("arbitrary",)))(p, p, lng, lnb, ws, bsb, dheads)


def _chunk_tri(n, chunk, upper):
    r = lax.broadcasted_iota(jnp.int32, (n, n), 0)
    c = lax.broadcasted_iota(jnp.int32, (n, n), 1)
    same = (r // chunk) == (c // chunk)
    return jnp.where(same & ((r <= c) if upper else (r >= c)), 1.0, 0.0).astype(F32)


def _running_sum(tri, x):
    hi = x.astype(BF16)
    rest = x - hi.astype(F32)
    mid = rest.astype(BF16)
    lo = (rest - mid.astype(F32)).astype(BF16)
    tri = tri.astype(BF16)
    return (jnp.dot(tri, hi, preferred_element_type=F32) + jnp.dot(tri, mid, preferred_element_type=F32)
            + jnp.dot(tri, lo, preferred_element_type=F32))


MASKED = -1e30


def _pair_masks(mask_ref, n, upper):
    row = lax.broadcasted_iota(jnp.int32, (n, HEAD), 0)
    for i in range(n):
        mask_ref[i] = jnp.where((row <= i) if upper else (row >= i), 0.0, MASKED).astype(F32)


def _hg_gates(fz, lb):
    sg = _sigmoid(fz)
    f = lb + (1.0 - lb) * sg
    kk = (1.0 - lb) * (1.0 - sg)
    return sg, f, jnp.log(f), kk


def _hgrn2_fwd(p, lb, onorm, *, name):
    S = p.shape[0]
    C = HG_SUB
    tb = _tile(S, 256, C)
    nsub = tb // C

    def body(q_ref, fz_ref, v_ref, g_ref, lb_ref, on_ref, tok_ref, o_ref, st_ref, state, b_blk, k_blk, bsc, ksc, vsc):
        @pl.when(pl.program_id(0) == 0)
        def _():
            state[...] = jnp.zeros_like(state)

        _, _, lg, kk = _hg_gates(fz_ref[...], lb_ref[...])
        b_blk[...] = _running_sum(_chunk_tri(tb, C, False), lg)
        k_blk[...] = kk
        tt = lax.broadcasted_iota(jnp.int32, (C, HEAD), 0)

        def sub(c, carry):
            rows = pl.ds(pl.multiple_of(c * C, C), C)
            for h in range(N_HEADS):
                cols = slice(h * HEAD, (h + 1) * HEAD)
                qv = q_ref[rows, cols]
                vv = v_ref[rows, cols]
                b = b_blk[rows, cols]
                kk = k_blk[rows, cols]
                st0 = state[h]
                st0x = st0.astype(MXU_DTYPE)
                st_ref[c, h] = st0x.astype(st_ref.dtype)
                inter = lax.dot_general((qv * jnp.exp(b)).astype(MXU_DTYPE), st0x,
                                        (((1,), (1,)), ((), ())), preferred_element_type=F32)
                bsc[h] = b
                ksc[h] = kk
                vsc[h] = vv
                intra = jnp.zeros((C, HEAD), F32)
                for s in range(C):
                    dec = jnp.where(tt >= s, jnp.exp(b - bsc[h, pl.ds(s, 1), :]), 0.0)
                    a_s = jnp.sum(qv * ksc[h, pl.ds(s, 1), :] * dec, axis=-1, keepdims=True)
                    intra = intra + a_s * vsc[h, pl.ds(s, 1), :]
                o_ref[rows, cols] = inter + intra
                b_last = bsc[h, pl.ds(C - 1, 1), :]
                ke = kk * jnp.exp(b_last - b)
                state[h] = st0 * jnp.exp(b_last) + lax.dot_general(
                    vv.astype(MXU_DTYPE), ke.astype(MXU_DTYPE), (((0,), (0,)), ((), ())),
                    preferred_element_type=F32)
            return carry

        lax.fori_loop(0, nsub, sub, 0, unroll=2)

        for h in range(N_HEADS):
            cols = slice(h * HEAD, (h + 1) * HEAD)
            o = o_ref[:, cols]
            gv = g_ref[:, cols]
            n = o * lax.rsqrt(jnp.mean(o * o, axis=-1, keepdims=True) + EPS)
            tok_ref[:, cols] = (n * (gv * _sigmoid(gv)) * on_ref[:, cols]).astype(tok_ref.dtype)

    blk = lambda c: pl.BlockSpec((tb, D_TOK), lambda i, c=c: (i, c))
    vec = pl.BlockSpec((1, D_TOK), lambda i: (0, 0))
    stb = pl.BlockSpec((nsub, N_HEADS, HEAD, HEAD), lambda i: (i, 0, 0, 0))
    return pl.pallas_call(
        body, name=name, grid=(S // tb,), in_specs=[blk(0), blk(1), blk(2), blk(3), vec, vec],
        out_specs=[blk(0), blk(0), stb],
        out_shape=[jax.ShapeDtypeStruct((S, D_MODEL), BF16), jax.ShapeDtypeStruct((S, D_TOK), F32),
                   jax.ShapeDtypeStruct((S // C, N_HEADS, HEAD, HEAD), BF16)],
        scratch_shapes=[pltpu.VMEM((N_HEADS, HEAD, HEAD), F32)] + [pltpu.VMEM((tb, D_TOK), F32)] * 2
        + [pltpu.VMEM((N_HEADS, C, HEAD), F32)] * 3,
        compiler_params=_params(("arbitrary",)))(p, p, p, p, lb, onorm)


def _hgrn2_bwd(p, lb, onorm, o, states, dheads, *, name):
    S = p.shape[0]
    C = HG_SUB
    tb = _tile(S, 256, C)
    nsub = tb // C
    nblk = S // tb

    def body(q_ref, fz_ref, v_ref, g_ref, lb_ref, on_ref, o_ref, st_ref, dt_ref, dp_ref, dlb_ref, don_ref, dstate,
             b_blk, k_blk, do_blk, db_blk, dk_blk, dq_blk, dv_blk, bsc, ksc, vsc, qsc, dosc, causal, anti):
        @pl.when(pl.program_id(0) == 0)
        def _():
            dstate[...] = jnp.zeros_like(dstate)
            dlb_ref[...] = jnp.zeros_like(dlb_ref)
            don_ref[...] = jnp.zeros_like(don_ref)

        for h in range(N_HEADS):
            cols = slice(h * HEAD, (h + 1) * HEAD)
            onv = on_ref[:, cols]
            gv = g_ref[:, cols]
            ov = o_ref[:, cols]
            dt = dt_ref[:, cols].astype(F32)
            sgg = _sigmoid(gv)
            sil = gv * sgg
            rinv = lax.rsqrt(jnp.mean(ov * ov, axis=-1, keepdims=True) + EPS)
            n = ov * rinv
            don_ref[:, cols] += jnp.sum(dt * n * sil, axis=0, keepdims=True)
            dn = dt * sil * onv
            dp_ref[:, 3 * D_TOK + h * HEAD:3 * D_TOK + (h + 1) * HEAD] = (
                dt * n * onv * sgg * (1.0 + gv * (1.0 - sgg))).astype(dp_ref.dtype)
            do_blk[:, cols] = rinv * (dn - n * jnp.mean(dn * n, axis=-1, keepdims=True))
        _, _, lg, kk = _hg_gates(fz_ref[...], lb_ref[...])
        b_blk[...] = _running_sum(_chunk_tri(tb, C, False), lg)
        k_blk[...] = kk
        _pair_masks(causal, C, False)
        _pair_masks(anti, C, True)
        tt = lax.broadcasted_iota(jnp.int32, (C, HEAD), 0)

        def sub(j, heads):
            c = nsub - 1 - j
            rows = pl.ds(pl.multiple_of(c * C, C), C)
            for h in heads:
                cols = slice(h * HEAD, (h + 1) * HEAD)
                qv = q_ref[rows, cols]
                vv = v_ref[rows, cols]
                do = do_blk[rows, cols]
                b = b_blk[rows, cols]
                kk = k_blk[rows, cols]
                bsc[h] = b
                ksc[h] = kk
                vsc[h] = vv
                qsc[h] = qv
                dosc[h] = do
                b_last = bsc[h, pl.ds(C - 1, 1), :]
                eb = jnp.exp(b)
                qe = qv * eb
                ebb = jnp.exp(b_last - b)
                ke = kk * ebb
                e_last = jnp.exp(b_last)
                st0x = st_ref[c, h].astype(MXU_DTYPE)
                st0 = st0x.astype(F32)
                dst1 = dstate[h]
                dst1x = dst1.astype(MXU_DTYPE)
                dox = do.astype(MXU_DTYPE)
                dqe = jnp.dot(dox, st0x, preferred_element_type=F32)
                dke = jnp.dot(vv.astype(MXU_DTYPE), dst1x, preferred_element_type=F32)
                dv = lax.dot_general(ke.astype(MXU_DTYPE), dst1x, (((1,), (1,)), ((), ())),
                                     preferred_element_type=F32)
                db_last = (e_last * jnp.sum(st0 * dst1, axis=0, keepdims=True)
                           + jnp.sum(dke * ke, axis=0, keepdims=True))
                dstate[h] = dst1 * e_last + lax.dot_general(dox, qe.astype(MXU_DTYPE), (((0,), (0,)), ((), ())),
                                                            preferred_element_type=F32)
                dq_pairs = jnp.zeros((C, HEAD), F32)
                for s in range(C):
                    dec = jnp.exp(b - bsc[h, pl.ds(s, 1), :] + causal[s])
                    da_s = jnp.sum(do * vsc[h, pl.ds(s, 1), :], axis=-1, keepdims=True)
                    dq_pairs = dq_pairs + da_s * (ksc[h, pl.ds(s, 1), :] * dec)
                dk_pairs = jnp.zeros((C, HEAD), F32)
                for t in range(C):
                    do_t = dosc[h, pl.ds(t, 1), :]
                    qd = qsc[h, pl.ds(t, 1), :] * jnp.exp(bsc[h, pl.ds(t, 1), :] - b + anti[t])
                    da_t = jnp.sum(vv * do_t, axis=-1, keepdims=True)
                    dk_pairs = dk_pairs + da_t * qd
                    a_t = jnp.sum(qd * kk, axis=-1, keepdims=True)
                    dv = dv + a_t * do_t
                db = dqe * qe - dke * ke + qv * dq_pairs - kk * dk_pairs
                db_blk[rows, cols] = db + jnp.where(tt == C - 1, db_last, 0.0)
                dk_blk[rows, cols] = dke * ebb + dk_pairs
                dq_blk[rows, cols] = dqe * eb + dq_pairs
                dv_blk[rows, cols] = dv

        for first in range(0, N_HEADS, HG_GROUP):
            heads = tuple(range(first, first + HG_GROUP))
            pl.loop(0, nsub)(lambda j, heads=heads: sub(j, heads))

        dlg = _running_sum(_chunk_tri(tb, C, True), db_blk[...])
        lbv = lb_ref[...]
        sg, f, _, _ = _hg_gates(fz_ref[...], lbv)
        w = dlg / f - dk_blk[...]
        dp_ref[:, 0:D_TOK] = dq_blk[...].astype(dp_ref.dtype)
        dp_ref[:, 2 * D_TOK:3 * D_TOK] = dv_blk[...].astype(dp_ref.dtype)
        dp_ref[:, D_TOK:2 * D_TOK] = (w * (1.0 - lbv) * sg * (1.0 - sg)).astype(dp_ref.dtype)
        dlb_ref[...] += jnp.sum(w * (1.0 - sg), axis=0, keepdims=True)

    blk = lambda c: pl.BlockSpec((tb, D_TOK), lambda i, c=c: (nblk - 1 - i, c))
    vec = pl.BlockSpec((1, D_TOK), lambda i: (0, 0))
    stb = pl.BlockSpec((nsub, N_HEADS, HEAD, HEAD), lambda i: (nblk - 1 - i, 0, 0, 0))
    small = jax.ShapeDtypeStruct((1, D_TOK), F32)
    return pl.pallas_call(
        body, name=name, grid=(nblk,), in_specs=[blk(0), blk(1), blk(2), blk(3), vec, vec, blk(0), stb, blk(0)],
        out_specs=[pl.BlockSpec((tb, 4 * D_TOK), lambda i: (nblk - 1 - i, 0)), vec, vec],
        out_shape=[jax.ShapeDtypeStruct((S, HG_IN), BF16), small, small],
        scratch_shapes=[pltpu.VMEM((N_HEADS, HEAD, HEAD), F32)] + [pltpu.VMEM((tb, D_TOK), F32)] * 7
        + [pltpu.VMEM((N_HEADS, C, HEAD), F32)] * 5 + [pltpu.VMEM((C, C, HEAD), F32)] * 2,
        compiler_params=_params(("arbitrary",)))(p, p, p, p, lb, onorm, o, states, dheads)


def _adamw(w, g, m, v, *, name):
    shape = w.shape
    cols = shape[-1]
    w2, g2, m2, v2 = (t.reshape(-1, cols) for t in (w, g, m, v))
    R = w2.shape[0]
    tr = _tile(R, 512, 8)

    def body(w_ref, g_ref, m_ref, v_ref, d_ref, nm_ref, nv_ref):
        gv = g_ref[...]
        nm = ADAM_B1 * m_ref[...] + (1.0 - ADAM_B1) * gv
        nv = ADAM_B2 * v_ref[...] + (1.0 - ADAM_B2) * (gv * gv)
        m_hat = nm / (1.0 - ADAM_B1 ** ADAM_STEP)
        v_hat = nv / (1.0 - ADAM_B2 ** ADAM_STEP)
        d_ref[...] = -ADAM_LR * (m_hat / (jnp.sqrt(v_hat) + ADAM_EPS) + ADAM_WD * w_ref[...])
        nm_ref[...] = nm
        nv_ref[...] = nv

    spec = pl.BlockSpec((tr, cols), lambda i: (i, 0))
    out = jax.ShapeDtypeStruct((R, cols), F32)
    d, nm, nv = pl.pallas_call(body, name=name, grid=(R // tr,), in_specs=[spec] * 4, out_specs=[spec] * 3,
                               out_shape=[out] * 3, compiler_params=_params(("parallel",)))(w2, g2, m2, v2)
    return d.reshape(shape), nm.reshape(shape), nv.reshape(shape)


def _add_received(sent, got, me, *, name):
    n, R, Cc = got.shape
    tr = _tile(R, 256, 16)
    per = R // tr

    def body(me_ref, a_ref, b_ref, o_ref):
        del me_ref
        acc = a_ref[...].astype(F32)
        for k in range(n):
            acc = acc + b_ref[k].astype(F32)
        o_ref[...] = acc

    grid_spec = pltpu.PrefetchScalarGridSpec(
        num_scalar_prefetch=1, grid=(per,),
        in_specs=[pl.BlockSpec((tr, Cc), lambda i, me_ref: (me_ref[0] * per + i, 0)),
                  pl.BlockSpec((n, tr, Cc), lambda i, me_ref: (0, i, 0))],
        out_specs=pl.BlockSpec((tr, Cc), lambda i, me_ref: (i, 0)))
    return pl.pallas_call(body, name=name, grid_spec=grid_spec, out_shape=jax.ShapeDtypeStruct((R, Cc), F32),
                          compiler_params=_params(("parallel",)))(jnp.reshape(me, (1,)).astype(jnp.int32), sent, got)


def _sum_blocks(x, *, name):
    n, R, Cc = x.shape
    tr = _tile(R, 208, 8)

    def body(x_ref, o_ref):
        acc = x_ref[0]
        for k in range(1, n):
            acc = acc + x_ref[k]
        o_ref[...] = acc

    return pl.pallas_call(
        body, name=name, grid=(R // tr,), in_specs=[pl.BlockSpec((n, tr, Cc), lambda i: (0, i, 0))],
        out_specs=pl.BlockSpec((tr, Cc), lambda i: (i, 0)), out_shape=jax.ShapeDtypeStruct((R, Cc), F32),
        compiler_params=_params(("parallel",)))(x)


def _place():
    return lax.axis_index("x"), lax.axis_index("y"), lax.axis_index("c")


def _all_gather(x, *, name, in_vmem, reduce_sum=False, with_token=False):
    R, Cc = x.shape
    space = pltpu.VMEM if in_vmem else pl.ANY

    def body(x_ref, out_ref, *scratch):
        if with_token:
            scratch[0][...] = jnp.zeros_like(scratch[0])
            scratch = scratch[1:]
        if reduce_sum:
            gat_ref, send_sems, recv_sems, local_sem = scratch
        else:
            gat_ref = out_ref
            send_sems, recv_sems, local_sem = scratch
        mx, my, mc = _place()
        me, sibling = (mx, my, mc), (mx, my, 1 - mc)
        chips = [(1 - mx, my), (mx, 1 - my), (1 - mx, 1 - my)]

        def rows(px, py, pc):
            return gat_ref.at[pl.ds((4 * px + 2 * py + pc) * R, R), :]

        def copy(k, block, to, src=None):
            return pltpu.make_async_remote_copy(
                src_ref=rows(*block) if src is None else src, dst_ref=rows(*block), send_sem=send_sems.at[k],
                recv_sem=recv_sems.at[k], device_id=to, device_id_type=MESH_ID)

        mine = pltpu.make_async_copy(x_ref, rows(*me), local_sem)
        mine.start()
        first = [copy(0, me, sibling, src=x_ref)]
        first += [copy(1 + j, me, (*chip, mc), src=x_ref) for j, chip in enumerate(chips)]
        for cp in first:
            cp.start()
        passed = [copy(4 + j, (*chip, mc), sibling) for j, chip in enumerate(chips)]
        for j, chip in enumerate(chips):
            copy(1 + j, (*chip, mc), me).wait_recv()
            passed[j].start()
        copy(0, sibling, me).wait_recv()
        for j, chip in enumerate(chips):
            copy(4 + j, (*chip, 1 - mc), me).wait_recv()
        for cp in first + passed:
            cp.wait_send()
        mine.wait()
        if reduce_sum:
            acc = gat_ref[pl.ds(0, R), :]
            for d in range(1, N_DEV):
                acc = acc + gat_ref[pl.ds(d * R, R), :]
            out_ref[...] = acc

    sems = [pltpu.SemaphoreType.DMA((7,)), pltpu.SemaphoreType.DMA((7,)), pltpu.SemaphoreType.DMA]
    if reduce_sum:
        assert in_vmem
        out_shape = jax.ShapeDtypeStruct((R, Cc), x.dtype)
        scratch = [pltpu.VMEM((N_DEV * R, Cc), x.dtype)] + sems
    else:
        out_shape = jax.ShapeDtypeStruct((N_DEV * R, Cc), x.dtype)
        scratch = sems
    out_specs = pl.BlockSpec(memory_space=space)
    if with_token:
        out_shape = (out_shape, jax.ShapeDtypeStruct((8, LANE), F32))
        out_specs = (out_specs, pl.BlockSpec(memory_space=pltpu.VMEM))
    return pl.pallas_call(
        body, name=name, out_shape=out_shape, in_specs=[pl.BlockSpec(memory_space=space)], out_specs=out_specs,
        scratch_shapes=scratch, compiler_params=pltpu.CompilerParams(vmem_limit_bytes=VMEM_LIMIT))(x)


def _peer(k, mx, my, mc):
    bits = k + 1
    return (1 - mx if bits & 4 else mx, 1 - my if bits & 2 else my, 1 - mc if bits & 1 else mc)


HBM_SPEC = pl.BlockSpec(memory_space=pltpu.HBM)
SEM_SPEC = pl.BlockSpec(memory_space=pltpu.SEMAPHORE)
DATAFLOW = pltpu.SideEffectType.DATAFLOW_SIDE_EFFECTING


def _exchange_copies(x_refs, land_refs, send_sems, recv_sems, scatter):
    mx, my, mc = _place()
    me = 4 * mx + 2 * my + mc
    n = len(x_refs)
    copies = []
    for k in range(N_DEV - 1):
        px, py, pc = _peer(k, mx, my, mc)
        for m, (x_ref, land_ref) in enumerate(zip(x_refs, land_refs)):
            rows = land_ref.shape[1] if scatter else x_ref.shape[0]
            if scatter:
                src = x_ref.at[pl.ds(pl.multiple_of((4 * px + 2 * py + pc) * rows, 16), rows), :]
                dst = land_ref.at[k]
            else:
                src = x_ref
                dst = land_ref.at[pl.ds(pl.multiple_of(me * rows, 16), rows), :]
            copies.append(pltpu.make_async_remote_copy(
                src_ref=src, dst_ref=dst, send_sem=send_sems.at[k * n + m], recv_sem=recv_sems.at[k * n + m],
                device_id=(px, py, pc), device_id_type=MESH_ID))
    return copies


def _land_shape(x, scatter):
    return (N_DEV - 1, x.shape[0] // N_DEV, x.shape[1]) if scatter else (N_DEV * x.shape[0], x.shape[1])


def _own_copies(x_refs, land_refs, local_sems):
    mx, my, mc = _place()
    me = 4 * mx + 2 * my + mc
    return [pltpu.make_async_copy(
        x_ref, land_ref.at[pl.ds(pl.multiple_of(me * x_ref.shape[0], 16), x_ref.shape[0]), :], local_sems.at[m])
        for m, (x_ref, land_ref) in enumerate(zip(x_refs, land_refs))]


def _exchange_start(groups, *, name, scatter):
    sizes = [len(g) for g in groups]
    xs = [x for g in groups for x in g]
    n = len(xs)
    lands = [lax.empty(_land_shape(x, scatter), x.dtype) for x in xs]
    per = 2 if scatter else 3

    def body(*refs):
        sems = refs[2 * n:2 * n + per * len(groups)]
        token = refs[-1]
        off = 0
        for gi, m in enumerate(sizes):
            x_refs, land_refs = refs[off:off + m], refs[n + off:n + off + m]
            for cp in _exchange_copies(x_refs, land_refs, sems[per * gi], sems[per * gi + 1], scatter):
                cp.start()
            if not scatter:
                for cp in _own_copies(x_refs, land_refs, sems[per * gi + 2]):
                    cp.start()
            off += m
        token[...] = jnp.zeros_like(token)

    sem_shapes = []
    for m in sizes:
        sem_shapes += [pltpu.SemaphoreType.DMA(((N_DEV - 1) * m,))] * 2
        if not scatter:
            sem_shapes.append(pltpu.SemaphoreType.DMA((m,)))
    ns = len(sem_shapes)
    out = pl.pallas_call(
        body, name=name,
        out_shape=(*sem_shapes, *[pltpu.HBM(x.shape, x.dtype) for x in xs],
                   *[pltpu.HBM(l.shape, l.dtype) for l in lands], jax.ShapeDtypeStruct((8, LANE), F32)),
        in_specs=(HBM_SPEC,) * (2 * n),
        out_specs=(SEM_SPEC,) * ns + (HBM_SPEC,) * (2 * n) + (pl.BlockSpec(memory_space=pltpu.VMEM),),
        input_output_aliases={i: ns + i for i in range(2 * n)},
        compiler_params=pltpu.CompilerParams(has_side_effects=DATAFLOW))(
            *[pltpu.with_memory_space_constraint(t, pltpu.HBM) for t in xs + lands])
    started, off = [], 0
    for gi, m in enumerate(sizes):
        sems = out[per * gi:per * gi + per]
        started.append((sems[0], sems[1], list(out[ns + off:ns + off + m]),
                        list(out[ns + n + off:ns + n + off + m]), out[-1], None if scatter else sems[2]))
        off += m
    return started


def _exchange_wait(started, after, *, name, scatter):
    send_sems, recv_sems, xs, lands, _, local_sems = started
    n = len(xs)

    def body(*refs):
        x_refs, land_refs = refs[:n], refs[n:2 * n]
        for cp in _exchange_copies(x_refs, land_refs, refs[2 * n], refs[2 * n + 1], scatter):
            cp.wait_send()
            cp.wait_recv()
        if not scatter:
            for cp in _own_copies(x_refs, land_refs, refs[2 * n + 2]):
                cp.wait()

    sems = (send_sems, recv_sems) if scatter else (send_sems, recv_sems, local_sems)
    out = pl.pallas_call(
        body, name=name, out_shape=tuple(pltpu.HBM(t.shape, t.dtype) for t in xs + lands),
        in_specs=(HBM_SPEC,) * (2 * n) + (SEM_SPEC,) * len(sems) + (pl.BlockSpec(memory_space=pl.ANY),),
        out_specs=(HBM_SPEC,) * (2 * n), input_output_aliases={i: i for i in range(2 * n)},
        compiler_params=pltpu.CompilerParams(has_side_effects=DATAFLOW))(*xs, *lands, *sems, after)
    return list(out[:n]), list(out[n:])


def _gather_start(groups, token, *, name):
    first = groups[0]
    groups = [[first[0] + token[0, 0].astype(first[0].dtype)] + list(first[1:])] + [list(g) for g in groups[1:]]
    return _exchange_start(groups, name=name, scatter=False)


def _gather_finish(started, after, *, name):
    return _exchange_wait(started, after, name=name, scatter=False)[1]


def _reduce_start(grads, *, name):
    return _exchange_start([grads], name=name, scatter=True)[0]


def _reduce_finish(started, after, me, *, name):
    sent, gots = _exchange_wait(started, after, name=name + "_wait", scatter=True)
    return [_add_received(g, got, me, name=f"{name}_add{m}") for m, (g, got) in enumerate(zip(sent, gots))]


def _pad_rows(a, mult):
    r = (-a.shape[0]) % mult
    return a if r == 0 else jnp.concatenate([a, jnp.zeros((r,) + a.shape[1:], a.dtype)], axis=0)


def kernel(x, mem, mix_norm, mem_norm, w_mem_kv, w_out, hg_w_in, hg_lb, hg_onorm, gm_w_in, gm_ln_g, gm_ln_b, gm_ws, gm_bs, ffn_norm, w_ffn_in, w_ffn_out, final_norm, loss_target, m_mix_norm, m_mem_norm, m_w_mem_kv, m_w_out, m_hg_w_in, m_hg_lb, m_hg_onorm, m_gm_w_in, m_gm_ln_g, m_gm_ln_b, m_gm_ws, m_gm_bs, m_ffn_norm, m_w_ffn_in, m_w_ffn_out, m_final_norm, v_mix_norm, v_mem_norm, v_w_mem_kv, v_w_out, v_hg_w_in, v_hg_lb, v_hg_onorm, v_gm_w_in, v_gm_ln_g, v_gm_ln_b, v_gm_ws, v_gm_bs, v_ffn_norm, v_w_ffn_in, v_w_ffn_out, v_final_norm):
    mx, my, mc = _place()
    me = 4 * mx + 2 * my + mc
    xs = x[0]
    mems = mem[0]
    tgt = loss_target[0]

    hg_t = hg_w_in[0].T.astype(BF16)
    gm_t = gm_w_in[0].T.astype(BF16)
    fi_t = [w_ffn_in[i].T.astype(BF16) for i in range(2)]
    kv_b = [w_mem_kv[i].astype(BF16) for i in range(2)]
    out_b = [w_out[i].astype(BF16) for i in range(2)]
    fo_b = [w_ffn_out[i].astype(BF16) for i in range(2)]
    ln_local = _pad_rows(jnp.concatenate([gm_ln_g, gm_ln_b], axis=0), 8)
    ln_local = jnp.concatenate([ln_local, jnp.zeros((8, LANE - ln_local.shape[1]), F32)], axis=1)
    ln_all, token = _all_gather(ln_local, name="gather_ln", in_vmem=True, with_token=True)
    ln_all = ln_all.reshape(N_DEV, 8, LANE)
    ln_g = ln_all[:, 0, :D_TOK // N_DEV].reshape(1, D_TOK)
    ln_b = ln_all[:, 1, :D_TOK // N_DEV].reshape(1, D_TOK)
    W_hgT, token = _all_gather(hg_t + token[0, 0].astype(BF16), name="gather_first", in_vmem=False,
                               with_token=True)
    gather_mix, fi0, fo0, gather_gm, fi1, fo1 = _gather_start(
        [kv_b + out_b, [fi_t[0]], [fo_b[0]], [gm_t], [fi_t[1]], [fo_b[1]]], token, name="gather_rest_start")
    gather_fi, gather_fo = [fi0, fi1], [fo0, fo1]

    lb_soft = jax.nn.softmax(hg_lb, axis=0)
    lb0 = lb_soft[0:1]
    bsb = jnp.broadcast_to(gm_bs[0][:, :, None], (N_HEADS, GM_CHUNK, GM_CHUNK))
    ws = gm_ws[0]

    W_fiT, W_fo = [], []

    def ffn_fwd(xin, hf, i, **tail):
        W_fiT.extend(_gather_finish(gather_fi[i], hf, name=f"gather_fi{i}_wait"))
        gu, act = _ffn_in(hf, W_fiT[i], name=f"ffn_in{i}")
        W_fo.extend(_gather_finish(gather_fo[i], act, name=f"gather_fo{i}_wait"))
        return gu, act, _matmul(act, W_fo[i], res=xin, name=f"ffn_out{i}", **tail)

    h0 = _rms_fwd(xs, mix_norm[0:1], name="mix_norm0", dep=gather_mix[4])
    p0 = _matmul(h0, W_hgT, tb=True, name="hg_in")
    heads0, o0, states = _hgrn2_fwd(p0, lb0, hg_onorm, name="hgrn2_fwd")

    kv0, kv1, wo0, wo1 = _gather_finish(gather_mix, o0, name="gather_mix_wait")
    W_kv, W_out = [kv0, kv1], [wo0, wo1]
    mem_n, kv = [], []
    for i in range(2):
        mn = _rms_fwd(mems, mem_norm[i:i + 1], name=f"mem_norm{i}")
        mem_n.append(mn)
        kv.append(_matmul(mn, W_kv[i], name=f"mem_kv{i}"))

    heads0 = _attn_fwd(p0, 4 * D_TOK // D_MEM, kv[0], heads0, name="attn_fwd0")
    x1, hf0 = _matmul(heads0, W_out[0], res=xs, norm_gain=ffn_norm[0:1], name="out_proj0")
    gu0, act0, (x2, h1) = ffn_fwd(x1, hf0, 0, norm_gain=mix_norm[1:2])

    W_gmT, = _gather_finish(gather_gm, h1, name="gather_gm_wait")
    p1 = _matmul(h1, W_gmT, tb=True, name="gm_in")
    heads1 = _gmlp_fwd(p1, ln_g, ln_b, ws, bsb, name="gmlp_fwd")
    heads1 = _attn_fwd(p1, 2 * D_TOK // D_MEM, kv[1], heads1, name="attn_fwd1")
    x3, hf1 = _matmul(heads1, W_out[1], res=x2, norm_gain=ffn_norm[1:2], name="out_proj1")
    gu1, act1, (dx, g_final, loss_part) = ffn_fwd(x3, hf1, 1, loss_head=(final_norm.reshape(1, D_MODEL), tgt))

    def ffn_bwd(dx, xin, hf, gu, act, i, dep):
        dgu = _ffn_out_dx(dx, W_fo[i], gu, dep, name=f"ffn_out_dx{i}")
        g_wfo = _matmul(act, dx, ta=True, out_dtype=BF16, name=f"ffn_out_dw{i}")
        g_wfi_t = _matmul(dgu, hf, ta=True, a_halves=True, out_dtype=BF16, name=f"ffn_in_dw{i}")
        dx, g_norm = _matmul(dgu, W_fiT[i], a_halves=True, res=dx, norm_bwd=(xin, ffn_norm[i:i + 1]),
                             name=f"ffn_in_dx{i}")
        return dx, g_wfi_t, g_wfo, g_norm

    def mem_bwd(dkv, i):
        g_wkv = _matmul(mem_n[i], dkv, ta=True, out_dtype=BF16, name=f"mem_kv_dw{i}")
        dmn = _matmul(dkv, W_kv[i], tb=True, name=f"mem_kv_dx{i}")
        _, g_norm = _rms_bwd(mems, mem_norm[i:i + 1], dmn, jnp.zeros_like(mems), name=f"mem_norm_bwd{i}")
        return g_wkv, g_norm

    dx, g_wfi1_t, g_wfo1, g_ffn1 = ffn_bwd(dx, x3, hf1, gu1, act1, 1, loss_part)
    dheads = _matmul(dx, W_out[1], tb=True, name="out_proj_dx1")
    g_wout1 = _matmul(heads1, dx, ta=True, out_dtype=BF16, name="out_proj_dw1")
    dp, g_ws, g_bs, g_lng, g_lnb = _gmlp_bwd(p1, ln_g, ln_b, ws, bsb, dheads, name="gmlp_bwd")
    dp, dk, dv = _attn_bwd(p1, 2 * D_TOK // D_MEM, kv[1], dheads, dp, name="attn_bwd1")
    g_wkv1, g_mem1 = mem_bwd(jnp.concatenate([dk, dv], axis=1), 1)
    g_wgm_t = _matmul(dp, h1, ta=True, out_dtype=BF16, name="gm_in_dw")
    dx, g_mix1 = _matmul(dp, W_gmT, res=dx, norm_bwd=(x2, mix_norm[1:2]), name="gm_in_dx")
    reduce_l1 = _reduce_start([g_wkv1, g_wout1, g_wgm_t, g_wfi1_t, g_wfo1], name="reduce_l1_start")
    early = [loss_part, g_final, g_ffn1, g_mix1, g_mem1, g_ws.reshape(1, -1), g_bs.reshape(1, -1), g_lng, g_lnb]
    early_rows = _pad_rows(jnp.concatenate(early, axis=1).reshape(-1, LANE), 16) + reduce_l1[4][0, 0]
    small_early = _exchange_start([[early_rows]], name="reduce_small_start", scatter=False)[0]

    dx, g_wfi0_t, g_wfo0, g_ffn0 = ffn_bwd(dx, x1, hf0, gu0, act0, 0, small_early[4])
    reduce_ffn0 = _reduce_start([g_wfi0_t, g_wfo0], name="reduce_ffn0_start")
    dheads = _matmul(dx, W_out[0], tb=True, name="out_proj_dx0", dep=reduce_ffn0[4])
    g_wout0 = _matmul(heads0, dx, ta=True, out_dtype=BF16, name="out_proj_dw0")
    dp, g_lb0, g_onorm = _hgrn2_bwd(p0, lb0, hg_onorm, o0, states, dheads, name="hgrn2_bwd")
    dp, dk, dv = _attn_bwd(p0, 4 * D_TOK // D_MEM, kv[0], dheads, dp, name="attn_bwd0")
    g_wkv0, g_mem0 = mem_bwd(jnp.concatenate([dk, dv], axis=1), 0)
    g_whg_t = _matmul(dp, h0, ta=True, out_dtype=BF16, name="hg_in_dw")
    reduce_mix0 = _reduce_start([g_wkv0, g_wout0, g_whg_t], name="reduce_mix0_start")
    grad_x, g_mix0 = _matmul(dp, W_hgT, res=dx, norm_bwd=(xs, mix_norm[0:1]), name="hg_in_dx", dep=reduce_mix0[4])

    g_kv1, g_out1, g_gm_t, g_fi1_t, g_fo1 = _reduce_finish(reduce_l1, grad_x, me, name="reduce_l1")
    g_fi0_t, g_fo0 = _reduce_finish(reduce_ffn0, g_kv1, me, name="reduce_ffn0")
    g_kv0, g_out0, g_hg_t = _reduce_finish(reduce_mix0, g_fi0_t, me, name="reduce_mix0")
    g_shards = [jnp.stack([g_kv0, g_kv1]), jnp.stack([g_out0, g_out1]), g_hg_t[None], g_gm_t[None],
                jnp.stack([g_fi0_t, g_fi1_t]), jnp.stack([g_fo0, g_fo1])]
    transposed = (4, 7, 13)

    late = [g_ffn0, g_lb0, g_onorm, g_mem0, g_mix0]
    late_rows = _pad_rows(jnp.concatenate(late, axis=1).reshape(-1, LANE), 8)
    red_late = _all_gather(late_rows, name="reduce_small_late", in_vmem=True, reduce_sum=True)
    gathered = _exchange_wait(small_early, red_late, name="reduce_small_wait", scatter=False)[1][0]
    red_early = _sum_blocks(gathered.reshape(N_DEV, -1, LANE), name="reduce_small_sum")

    def split(flat, parts):
        out, off = [], 0
        for t in parts:
            out.append(flat[off:off + t.shape[1]])
            off += t.shape[1]
        return out

    r_loss, r_final, r_ffn1, r_mix1, r_mem1, r_ws, r_bs, r_lng, r_lnb = split(red_early.reshape(-1), early)
    r_ffn0, r_lb0, r_onorm, r_mem0, r_mix0 = split(red_late.reshape(-1), late)
    loss = r_loss[0]
    g_mix_norm = jnp.stack([r_mix0, r_mix1])
    g_mem_norm = jnp.stack([r_mem0, r_mem1])
    g_hg_lb = r_lb0[None, :] * lb0 * (jnp.eye(3, dtype=F32)[:, 0:1] - lb_soft)
    g_hg_onorm = r_onorm.reshape(1, D_TOK)
    width = D_TOK // N_DEV
    g_gm_ln_g = lax.dynamic_slice(r_lng, (me * width,), (width,)).reshape(1, width)
    g_gm_ln_b = lax.dynamic_slice(r_lnb, (me * width,), (width,)).reshape(1, width)
    g_gm_ws = r_ws.reshape(gm_ws.shape)
    g_gm_bs = r_bs.reshape(gm_bs.shape)
    g_ffn_norm = jnp.stack([r_ffn0, r_ffn1])
    g_final_norm = r_final

    grads = [g_mix_norm, g_mem_norm, g_shards[0], g_shards[1], g_shards[2], g_hg_lb, g_hg_onorm, g_shards[3],
             g_gm_ln_g, g_gm_ln_b, g_gm_ws, g_gm_bs, g_ffn_norm, g_shards[4], g_shards[5], g_final_norm]
    weights = [mix_norm, mem_norm, w_mem_kv, w_out, hg_w_in, hg_lb, hg_onorm, gm_w_in, gm_ln_g, gm_ln_b, gm_ws, gm_bs,
               ffn_norm, w_ffn_in, w_ffn_out, final_norm]
    ms = [m_mix_norm, m_mem_norm, m_w_mem_kv, m_w_out, m_hg_w_in, m_hg_lb, m_hg_onorm, m_gm_w_in, m_gm_ln_g,
          m_gm_ln_b, m_gm_ws, m_gm_bs, m_ffn_norm, m_w_ffn_in, m_w_ffn_out, m_final_norm]
    vs = [v_mix_norm, v_mem_norm, v_w_mem_kv, v_w_out, v_hg_w_in, v_hg_lb, v_hg_onorm, v_gm_w_in, v_gm_ln_g,
          v_gm_ln_b, v_gm_ws, v_gm_bs, v_ffn_norm, v_w_ffn_in, v_w_ffn_out, v_final_norm]
    deltas, new_m, new_v = [], [], []
    for n, (w, g, m, v) in enumerate(zip(weights, grads, ms, vs)):
        if w.ndim == 1:
            d, nm, nv = _adamw(w[None], g.reshape(1, -1), m[None], v[None], name=f"adamw{n}")
            d, nm, nv = d[0], nm[0], nv[0]
        elif n in transposed:
            flip = lambda t: jnp.swapaxes(t, 1, 2)
            d, nm, nv = (flip(t) for t in _adamw(flip(w), g, flip(m), flip(v), name=f"adamw{n}"))
            grads[n] = flip(g)
        else:
            d, nm, nv = _adamw(w, g.reshape(w.shape), m, v, name=f"adamw{n}")
        deltas.append(d)
        new_m.append(nm)
        new_v.append(nv)
    grads = [g.reshape(w.shape) for g, w in zip(grads, weights)]
    return (loss, grad_x[None], *grads, *deltas, *new_m, *new_v)
```

```python
import jax
import jax.numpy as jnp
from jax import lax
from jax.experimental import pallas as pl
from jax.experimental.pallas import tpu as pltpu

F32 = jnp.float32
BF16 = jnp.bfloat16
MXU_DTYPE = jnp.bfloat16
MESH_ID = pl.DeviceIdType.MESH

N_DEV = 8
EPS = 1e-6
D_MODEL = 1024
D_TOK = 768
D_MEM = 256
N_HEADS = 6
HEAD = 128
MEM_HEADS = 4
MEM_HDIM = 64
GM_CHUNK = 128
ATTN_ROWS = 2048
D_FF = 2816
HG_SUB = 16
HG_GROUP = 6
HG_IN = 4 * D_TOK + D_MEM
GM_IN = 2 * D_TOK + D_MEM
LANE = 128
MXU_COLS = 256

ADAM_LR = 0.001
ADAM_B1 = 0.9
ADAM_B2 = 0.999
ADAM_EPS = 1e-08
ADAM_WD = 0.01
ADAM_STEP = 10

VMEM_LIMIT = 48 * 2 ** 20
VMEM_LIMIT_WIDE = 58 * 2 ** 20


def _params(sem=None, limit=VMEM_LIMIT):
    return pltpu.CompilerParams(dimension_semantics=sem, vmem_limit_bytes=limit)


def _tile(n, cap, q=LANE):
    if n <= cap:
        return n
    best = None
    for t in range(q, cap + 1, q):
        if n % t == 0:
            best = t
    assert best is not None, (n, cap, q)
    return best


def _sigmoid(x):
    return 1.0 / (1.0 + jnp.exp(-x))


def _gelu(x, with_grad=False):
    cdf = 0.5 * (1.0 + lax.erf(x * 0.7071067811865476))
    if not with_grad:
        return x * cdf
    return x * cdf, cdf + x * jnp.exp(-0.5 * x * x) * 0.3989422804014327


def _matmul(a, b, *, name, ta=False, tb=False, res=None, out_dtype=F32, a_halves=False, b_halves=False, dep=None,
            norm_gain=None, norm_bwd=None, loss_head=None):
    if a_halves and ta:
        K, M = a.shape[1], 2 * a.shape[2]
    elif a_halves:
        M, K = a.shape[1], 2 * a.shape[2]
    else:
        K, M = a.shape if ta else a.shape[::-1]
    if b_halves:
        assert not tb and b.shape[1] == K
        N = 2 * b.shape[2]
    else:
        N = b.shape[0] if tb else b.shape[1]
        assert (b.shape[1] if tb else b.shape[0]) == K
    tm = _tile(M // 2 if (a_halves and ta) else M, 1664 if ta else 1024)
    tn = _tile(N // 2 if b_halves else N, 1792)
    tk = _tile(K // 2 if (a_halves and not ta) else K, 1024 if ta else 1664)
    nk = K // tk
    dims = (((0 if ta else 1,), (1 if tb else 0,)), ((), ()))

    strips = norm_bwd is not None or loss_head is not None
    fused = norm_gain is not None or strips
    n_in = 2 + (res is not None) + (norm_gain is not None) + 2 * strips + (dep is not None)
    if fused:
        assert tn == N, "the fused norm needs whole rows"
        assert (norm_gain is not None) + (norm_bwd is not None) + (loss_head is not None) == 1
        assert not strips or (res is not None and nk > 1 and tm % LANE == 0)

    def body(*refs):
        a_ref, b_ref = refs[:2]
        r_ref = refs[2] if res is not None else None
        g_ref = refs[2 + (res is not None)] if fused else None
        x_ref = refs[3 + (res is not None)] if strips else None
        o_ref = refs[n_in]
        h_ref = refs[n_in + 1] if fused else None
        l_ref = refs[n_in + 2] if loss_head is not None else None
        acc = None if nk == 1 else refs[-1]
        k = pl.program_id(2)

        def product():
            return lax.dot_general(a_ref[...].astype(MXU_DTYPE), b_ref[...].astype(MXU_DTYPE), dims,
                                   preferred_element_type=F32)

        def finish(r):
            if loss_head is not None:
                @pl.when(pl.program_id(0) == 0)
                def _():
                    h_ref[...] = jnp.zeros_like(h_ref)
                    l_ref[...] = jnp.zeros_like(l_ref)

                acc[...] = r + r_ref[...]
                gv = g_ref[...]

                def strip(s, carry):
                    dg, loss = carry
                    rows = pl.ds(pl.multiple_of(s * LANE, LANE), LANE)
                    xv = acc[rows, :]
                    scale = lax.rsqrt(jnp.mean(xv * xv, axis=-1, keepdims=True) + EPS)
                    xh = xv * scale
                    err = xh * gv - x_ref[rows, :]
                    loss = loss + 0.5 * jnp.sum(jnp.mean(err * err, axis=-1, keepdims=True), axis=0, keepdims=True)
                    dy = err * (1.0 / N)
                    u = dy * gv
                    o_ref[rows, :] = scale * (u - xh * jnp.mean(u * xh, axis=-1, keepdims=True))
                    return dg + jnp.sum(dy * xh, axis=0, keepdims=True), loss

                dg, loss = lax.fori_loop(0, tm // LANE, strip, (jnp.zeros((1, N), F32), jnp.zeros((1, 1), F32)))
                h_ref[...] += dg
                l_ref[...] += jnp.broadcast_to(loss, l_ref.shape)
                return
            if norm_bwd is not None:
                @pl.when(pl.program_id(0) == 0)
                def _():
                    h_ref[...] = jnp.zeros_like(h_ref)

                acc[...] = r
                gv = g_ref[...]

                def strip(s, dg):
                    rows = pl.ds(pl.multiple_of(s * LANE, LANE), LANE)
                    rv = acc[rows, :]
                    xv = x_ref[rows, :]
                    scale = lax.rsqrt(jnp.mean(xv * xv, axis=-1, keepdims=True) + EPS)
                    xh = xv * scale
                    u = rv * gv
                    o_ref[rows, :] = r_ref[rows, :] + scale * (u - xh * jnp.mean(u * xh, axis=-1, keepdims=True))
                    return dg + jnp.sum(rv * xh, axis=0, keepdims=True)

                h_ref[...] += lax.fori_loop(0, tm // LANE, strip, jnp.zeros((1, N), F32))
                return
            if res is not None:
                r = r + r_ref[...].astype(F32)
            o_ref[...] = r.astype(out_dtype)
            if norm_gain is not None:
                scale = lax.rsqrt(jnp.mean(r * r, axis=-1, keepdims=True) + EPS)
                h_ref[...] = (r * scale * g_ref[...]).astype(h_ref.dtype)

        if nk == 1:
            finish(product())
            return

        @pl.when(k == 0)
        def _():
            acc[...] = product()

        @pl.when((k > 0) & (k < nk - 1))
        def _():
            acc[...] += product()

        @pl.when(k == nk - 1)
        def _():
            finish(acc[...] + product())

    if a_halves and ta:
        mh = M // 2 // tm
        a_spec = pl.BlockSpec((None, tk, tm), lambda i, j, k: (i // mh, k, i % mh))
    elif a_halves:
        kh = nk // 2
        a_spec = pl.BlockSpec((None, tm, tk), lambda i, j, k: (k // kh, i, k % kh))
    elif ta:
        a_spec = pl.BlockSpec((tk, tm), lambda i, j, k: (k, i))
    else:
        a_spec = pl.BlockSpec((tm, tk), lambda i, j, k: (i, k))
    if b_halves:
        nh = N // 2 // tn
        b_spec = pl.BlockSpec((None, tk, tn), lambda i, j, k: (j // nh, k, j % nh))
    elif tb:
        b_spec = pl.BlockSpec((tn, tk), lambda i, j, k: (j, k))
    else:
        b_spec = pl.BlockSpec((tk, tn), lambda i, j, k: (k, j))
    o_spec = pl.BlockSpec((tm, tn), lambda i, j, k: (i, j))
    in_specs = [a_spec, b_spec] + ([o_spec] if res is not None else [])
    args = (a, b) + ((res,) if res is not None else ())
    out_specs, out_shape = o_spec, jax.ShapeDtypeStruct((M, N), out_dtype)
    vec = pl.BlockSpec((1, N), lambda i, j, k: (0, 0))
    sem = ("parallel", "parallel", "arbitrary")
    if norm_gain is not None:
        in_specs.append(vec)
        args += (norm_gain,)
        out_specs, out_shape = [o_spec, o_spec], [out_shape, jax.ShapeDtypeStruct((M, N), BF16)]
    if norm_bwd is not None:
        x_in, gain = norm_bwd
        in_specs += [vec, o_spec]
        args += (gain, x_in)
        out_specs, out_shape = [o_spec, vec], [out_shape, jax.ShapeDtypeStruct((1, N), F32)]
        sem = ("arbitrary", "arbitrary", "arbitrary")
    if loss_head is not None:
        gain, target = loss_head
        in_specs += [vec, o_spec]
        args += (gain, target)
        one = pl.BlockSpec((1, LANE), lambda i, j, k: (0, 0))
        out_specs = [o_spec, vec, one]
        out_shape = [out_shape, jax.ShapeDtypeStruct((1, N), F32), jax.ShapeDtypeStruct((1, LANE), F32)]
        sem = ("arbitrary", "arbitrary", "arbitrary")
    if dep is not None:
        in_specs.append(pl.BlockSpec(memory_space=pl.ANY))
        args += (dep,)
    return pl.pallas_call(
        body, name=name, grid=(M // tm, N // tn, nk), in_specs=in_specs, out_specs=out_specs, out_shape=out_shape,
        scratch_shapes=[] if nk == 1 else [pltpu.VMEM((tm, tn), F32)],
        compiler_params=_params(sem, VMEM_LIMIT_WIDE if strips else VMEM_LIMIT))(*args)


def _ffn_in(hf, wt, *, name):
    S, K = hf.shape
    tm = _tile(S, 512)
    tn = _tile(D_FF, 1408)
    nh = D_FF // tn
    nt = (((1,), (1,)), ((), ()))

    def body(a_ref, bg_ref, bu_ref, gu_ref, act_ref):
        av = a_ref[...].astype(MXU_DTYPE)
        for c0 in range(0, tn, MXU_COLS):
            cs = slice(c0, min(c0 + MXU_COLS, tn))
            gate = lax.dot_general(av, bg_ref[cs, :].astype(MXU_DTYPE), nt, preferred_element_type=F32)
            up = lax.dot_general(av, bu_ref[cs, :].astype(MXU_DTYPE), nt, preferred_element_type=F32)
            gu_ref[0, :, cs] = gate.astype(gu_ref.dtype)
            gu_ref[1, :, cs] = up.astype(gu_ref.dtype)
            act_ref[:, cs] = (gate * _sigmoid(gate) * up).astype(act_ref.dtype)

    return pl.pallas_call(
        body, name=name, grid=(nh, S // tm),
        in_specs=[pl.BlockSpec((tm, K), lambda j, i: (i, 0)), pl.BlockSpec((tn, K), lambda j, i: (j, 0)),
                  pl.BlockSpec((tn, K), lambda j, i: (j + nh, 0))],
        out_specs=[pl.BlockSpec((2, tm, tn), lambda j, i: (0, i, j)), pl.BlockSpec((tm, tn), lambda j, i: (i, j))],
        out_shape=[jax.ShapeDtypeStruct((2, S, D_FF), BF16), jax.ShapeDtypeStruct((S, D_FF), BF16)],
        compiler_params=_params(("parallel", "parallel")))(hf, wt, wt)


def _ffn_out_dx(dx, w, gu, dep, *, name):
    S, K = dx.shape
    tm = _tile(S, 1024)
    tn = _tile(D_FF, 1408)

    def body(a_ref, b_ref, gu_ref, dep_ref, o_ref):
        del dep_ref
        av = a_ref[...].astype(MXU_DTYPE)
        for c0 in range(0, tn, MXU_COLS):
            cs = slice(c0, min(c0 + MXU_COLS, tn))
            da = lax.dot_general(av, b_ref[cs, :].astype(MXU_DTYPE), (((1,), (1,)), ((), ())),
                                 preferred_element_type=F32)
            gate = gu_ref[0, :, cs].astype(F32)
            up = gu_ref[1, :, cs].astype(F32)
            sg = _sigmoid(gate)
            o_ref[0, :, cs] = (da * up * sg * (1.0 + gate * (1.0 - sg))).astype(o_ref.dtype)
            o_ref[1, :, cs] = (da * gate * sg).astype(o_ref.dtype)

    halves = pl.BlockSpec((2, tm, tn), lambda i, j: (0, i, j))
    return pl.pallas_call(
        body, name=name, grid=(S // tm, D_FF // tn),
        in_specs=[pl.BlockSpec((tm, K), lambda i, j: (i, 0)), pl.BlockSpec((tn, K), lambda i, j: (j, 0)), halves,
                  pl.BlockSpec(memory_space=pl.ANY)],
        out_specs=halves, out_shape=jax.ShapeDtypeStruct((2, S, D_FF), BF16),
        compiler_params=_params(("parallel", "parallel")))(dx, w, gu, dep)


def _rms_fwd(x, g, *, name, dep=None):
    R, Dm = x.shape
    tr = _tile(R, 512, 8)

    def body(x_ref, g_ref, *rest):
        o_ref = rest[-1]
        xv = x_ref[...]
        r = lax.rsqrt(jnp.mean(xv * xv, axis=-1, keepdims=True) + EPS)
        o_ref[...] = (xv * r * g_ref[...]).astype(o_ref.dtype)

    in_specs = [pl.BlockSpec((tr, Dm), lambda i: (i, 0)), pl.BlockSpec((1, Dm), lambda i: (0, 0))]
    args = (x, g)
    if dep is not None:
        in_specs.append(pl.BlockSpec(memory_space=pl.ANY))
        args += (dep,)
    return pl.pallas_call(
        body, name=name, grid=(R // tr,), in_specs=in_specs,
        out_specs=pl.BlockSpec((tr, Dm), lambda i: (i, 0)), out_shape=jax.ShapeDtypeStruct((R, Dm), BF16),
        compiler_params=_params(("parallel",)))(*args)


def _rms_bwd(x, g, dh, dres, *, name):
    R, Dm = x.shape
    tr = _tile(R, 256, 8)

    def body(x_ref, g_ref, dh_ref, dres_ref, dx_ref, dg_ref):
        @pl.when(pl.program_id(0) == 0)
        def _():
            dg_ref[...] = jnp.zeros_like(dg_ref)

        xv = x_ref[...]
        r = lax.rsqrt(jnp.mean(xv * xv, axis=-1, keepdims=True) + EPS)
        xh = xv * r
        dhv = dh_ref[...].astype(F32)
        dg_ref[...] += jnp.sum(dhv * xh, axis=0, keepdims=True)
        u = dhv * g_ref[...]
        dx = r * (u - xh * jnp.mean(u * xh, axis=-1, keepdims=True))
        dx_ref[...] = dres_ref[...] + dx

    row = pl.BlockSpec((tr, Dm), lambda i: (i, 0))
    vec = pl.BlockSpec((1, Dm), lambda i: (0, 0))
    return pl.pallas_call(
        body, name=name, grid=(R // tr,), in_specs=[row, vec, row, row], out_specs=[row, vec],
        out_shape=[jax.ShapeDtypeStruct((R, Dm), F32), jax.ShapeDtypeStruct((1, Dm), F32)],
        compiler_params=_params(("arbitrary",)))(x, g, dh, dres)


def _head_mask(h):
    lane = lax.broadcasted_iota(jnp.int32, (1, D_MEM), 1)
    return (lane >= h * MEM_HDIM) & (lane < (h + 1) * MEM_HDIM)


def _attn_probs(qv, k_mx, mask):
    s = lax.dot_general(jnp.where(mask, qv, 0.0).astype(MXU_DTYPE), k_mx, (((1,), (1,)), ((), ())),
                        preferred_element_type=F32) * (MEM_HDIM ** -0.5)
    e = jnp.exp(s - jnp.max(s, axis=-1, keepdims=True))
    return e / jnp.sum(e, axis=-1, keepdims=True)


def _attn_fwd(p, qcol, kv, heads, *, name):
    S = p.shape[0]
    M = kv.shape[0]
    ts = _tile(S, ATTN_ROWS, 8)

    def body(q_ref, k_ref, v_ref, heads_in, o_ref):
        del heads_in
        qv = q_ref[...]
        kx = k_ref[...].astype(MXU_DTYPE)
        vv = v_ref[...]
        out = jnp.zeros((ts, D_MEM), F32)
        for h in range(MEM_HEADS):
            mask = _head_mask(h)
            pr = _attn_probs(qv, kx, mask)
            out = out + jnp.dot(pr.astype(MXU_DTYPE), jnp.where(mask, vv, 0.0).astype(MXU_DTYPE),
                                preferred_element_type=F32)
        o_ref[...] = out.astype(o_ref.dtype)

    return pl.pallas_call(
        body, name=name, grid=(S // ts,),
        in_specs=[pl.BlockSpec((ts, D_MEM), lambda i: (i, qcol)), pl.BlockSpec((M, D_MEM), lambda i: (0, 0)),
                  pl.BlockSpec((M, D_MEM), lambda i: (0, 1)), pl.BlockSpec(memory_space=pl.ANY)],
        out_specs=pl.BlockSpec((ts, D_MEM), lambda i: (i, D_TOK // D_MEM)),
        out_shape=jax.ShapeDtypeStruct(heads.shape, heads.dtype), input_output_aliases={3: 0},
        compiler_params=_params(("parallel",)))(p, kv, kv, heads)


def _attn_bwd(p, qcol, kv, dheads, dp, *, name):
    S = p.shape[0]
    M = kv.shape[0]
    ts = _tile(S, ATTN_ROWS, 8)
    scale = MEM_HDIM ** -0.5

    def body(q_ref, k_ref, v_ref, do_ref, dp_in, dq_ref, dk_ref, dv_ref):
        del dp_in

        @pl.when(pl.program_id(0) == 0)
        def _():
            dk_ref[...] = jnp.zeros_like(dk_ref)
            dv_ref[...] = jnp.zeros_like(dv_ref)

        qv = q_ref[...]
        kv_ = k_ref[...]
        kx = kv_.astype(MXU_DTYPE)
        vv = v_ref[...]
        dox = do_ref[...].astype(MXU_DTYPE)
        qx = qv.astype(MXU_DTYPE)
        dq = jnp.zeros((ts, D_MEM), F32)
        for h in range(MEM_HEADS):
            mask = _head_mask(h)
            pr = _attn_probs(qv, kx, mask)
            vh = jnp.where(mask, vv, 0.0).astype(MXU_DTYPE)
            dpr = lax.dot_general(dox, vh, (((1,), (1,)), ((), ())), preferred_element_type=F32)
            ds = (pr * (dpr - jnp.sum(dpr * pr, axis=-1, keepdims=True)) * scale).astype(MXU_DTYPE)
            dq = dq + jnp.dot(ds, jnp.where(mask, kv_, 0.0).astype(MXU_DTYPE), preferred_element_type=F32)
            dk_h = lax.dot_general(ds, qx, (((0,), (0,)), ((), ())), preferred_element_type=F32)
            dv_h = lax.dot_general(pr.astype(MXU_DTYPE), dox, (((0,), (0,)), ((), ())), preferred_element_type=F32)
            dk_ref[...] += jnp.where(mask, dk_h, 0.0)
            dv_ref[...] += jnp.where(mask, dv_h, 0.0)
        dq_ref[...] = dq.astype(dq_ref.dtype)

    return pl.pallas_call(
        body, name=name, grid=(S // ts,),
        in_specs=[pl.BlockSpec((ts, D_MEM), lambda i: (i, qcol)), pl.BlockSpec((M, D_MEM), lambda i: (0, 0)),
                  pl.BlockSpec((M, D_MEM), lambda i: (0, 1)),
                  pl.BlockSpec((ts, D_MEM), lambda i: (i, D_TOK // D_MEM)), pl.BlockSpec(memory_space=pl.ANY)],
        out_specs=[pl.BlockSpec((ts, D_MEM), lambda i: (i, qcol)), pl.BlockSpec((M, D_MEM), lambda i: (0, 0)),
                   pl.BlockSpec((M, D_MEM), lambda i: (0, 0))],
        out_shape=[jax.ShapeDtypeStruct(dp.shape, dp.dtype), jax.ShapeDtypeStruct((M, D_MEM), F32),
                   jax.ShapeDtypeStruct((M, D_MEM), F32)],
        input_output_aliases={4: 0}, compiler_params=_params(("arbitrary",)))(p, kv, kv, dheads, dp)


def _gm_forward_parts(u_ref, v_ref, lng_ref, lnb_ref, w_ref, bsb_ref, with_grad=False):
    if with_grad:
        (zu, du_gelu), (zv, dv_gelu) = _gelu(u_ref[...], True), _gelu(v_ref[...], True)
    else:
        zu, zv, du_gelu, dv_gelu = _gelu(u_ref[...]), _gelu(v_ref[...]), None, None
    mu = jnp.mean(zv, axis=-1, keepdims=True)
    cen = zv - mu
    rs = lax.rsqrt(jnp.mean(cen * cen, axis=-1, keepdims=True) + EPS)
    vh = cen * rs
    vn = vh * lng_ref[...] + lnb_ref[...]
    row = lax.broadcasted_iota(jnp.int32, (GM_CHUNK, GM_CHUNK), 0)
    col = lax.broadcasted_iota(jnp.int32, (GM_CHUNK, GM_CHUNK), 1)
    tril = row >= col
    wm = [jnp.where(tril, w_ref[g], 0.0).astype(MXU_DTYPE) for g in range(N_HEADS)]
    vnx = [vn[:, g * HEAD:(g + 1) * HEAD].astype(MXU_DTYPE) for g in range(N_HEADS)]
    sv = [jnp.dot(wm[g], vnx[g], preferred_element_type=F32) + bsb_ref[g] for g in range(N_HEADS)]
    return zu, vh, rs, wm, vnx, sv, tril, du_gelu, dv_gelu


def _gmlp_fwd(p, lng, lnb, ws, bsb, *, name):
    S = p.shape[0]

    def body(u_ref, v_ref, lng_ref, lnb_ref, w_ref, bsb_ref, o_ref):
        zu, _, _, _, _, sv, _, _, _ = _gm_forward_parts(u_ref, v_ref, lng_ref, lnb_ref, w_ref, bsb_ref)
        for g in range(N_HEADS):
            o_ref[:, g * HEAD:(g + 1) * HEAD] = (zu[:, g * HEAD:(g + 1) * HEAD] * sv[g]).astype(o_ref.dtype)

    blk = lambda c: pl.BlockSpec((GM_CHUNK, D_TOK), lambda i: (i, c))
    vec = pl.BlockSpec((1, D_TOK), lambda i: (0, 0))
    cube = pl.BlockSpec((N_HEADS, GM_CHUNK, GM_CHUNK), lambda i: (0, 0, 0))
    return pl.pallas_call(
        body, name=name, grid=(S // GM_CHUNK,), in_specs=[blk(0), blk(1), vec, vec, cube, cube],
        out_specs=blk(0), out_shape=jax.ShapeDtypeStruct((S, D_MODEL), BF16),
        compiler_params=_params(("parallel",)))(p, p, lng, lnb, ws, bsb)


def _gmlp_bwd(p, lng, lnb, ws, bsb, dheads, *, name):
    S = p.shape[0]

    def body(u_ref, v_ref, lng_ref, lnb_ref, w_ref, bsb_ref, dt_ref, dp_ref, dw_ref, dbs_ref, dlg_ref, dlb_ref):
        @pl.when(pl.program_id(0) == 0)
        def _():
            dw_ref[...] = jnp.zeros_like(dw_ref)
            dbs_ref[...] = jnp.zeros_like(dbs_ref)
            dlg_ref[...] = jnp.zeros_like(dlg_ref)
            dlb_ref[...] = jnp.zeros_like(dlb_ref)

        zu, vh, rs, wm, vnx, sv, tril, du_gelu, dv_gelu = _gm_forward_parts(
            u_ref, v_ref, lng_ref, lnb_ref, w_ref, bsb_ref, with_grad=True)
        dt = dt_ref[...].astype(F32)
        dvn_parts = []
        for g in range(N_HEADS):
            sl = slice(g * HEAD, (g + 1) * HEAD)
            dsv = dt[:, sl] * zu[:, sl]
            dp_ref[:, sl] = (dt[:, sl] * sv[g] * du_gelu[:, sl]).astype(dp_ref.dtype)
            dsx = dsv.astype(MXU_DTYPE)
            dw = lax.dot_general(dsx, vnx[g], (((1,), (1,)), ((), ())), preferred_element_type=F32)
            dw_ref[g] += jnp.where(tril, dw, 0.0)
            dbs_ref[g] += jnp.sum(dsv, axis=-1, keepdims=True)
            dvn_parts.append(lax.dot_general(wm[g], dsx, (((0,), (0,)), ((), ())), preferred_element_type=F32))
        dvn = jnp.concatenate(dvn_parts, axis=-1)
        dlg_ref[...] += jnp.sum(dvn * vh, axis=0, keepdims=True)
        dlb_ref[...] += jnp.sum(dvn, axis=0, keepdims=True)
        dvh = dvn * lng_ref[...]
        dzv = rs * (dvh - jnp.mean(dvh, axis=-1, keepdims=True) - vh * jnp.mean(dvh * vh, axis=-1, keepdims=True))
        dp_ref[:, D_TOK:] = (dzv * dv_gelu).astype(dp_ref.dtype)

    blk = lambda c: pl.BlockSpec((GM_CHUNK, D_TOK), lambda i: (i, c))
    vec = pl.BlockSpec((1, D_TOK), lambda i: (0, 0))
    cube = pl.BlockSpec((N_HEADS, GM_CHUNK, GM_CHUNK), lambda i: (0, 0, 0))
    col = pl.BlockSpec((N_HEADS, GM_CHUNK, 1), lambda i: (0, 0, 0))
    return pl.pallas_call(
        body, name=name, grid=(S // GM_CHUNK,), in_specs=[blk(0), blk(1), vec, vec, cube, cube, blk(0)],
        out_specs=[pl.BlockSpec((GM_CHUNK, 2 * D_TOK), lambda i: (i, 0)), cube, col, vec, vec],
        out_shape=[jax.ShapeDtypeStruct((S, GM_IN), BF16), jax.ShapeDtypeStruct((N_HEADS, GM_CHUNK, GM_CHUNK), F32),
                   jax.ShapeDtypeStruct((N_HEADS, GM_CHUNK, 1), F32), jax.ShapeDtypeStruct((1, D_TOK), F32),
                   jax.ShapeDtypeStruct((1, D_TOK), F32)],
        compiler_params=_params(("arbitrary",)))(p, p, lng, lnb, ws, bsb, dheads)


def _chunk_tri(n, chunk, upper):
    r = lax.broadcasted_iota(jnp.int32, (n, n), 0)
    c = lax.broadcasted_iota(jnp.int32, (n, n), 1)
    same = (r // chunk) == (c // chunk)
    return jnp.where(same & ((r <= c) if upper else (r >= c)), 1.0, 0.0).astype(F32)


def _running_sum(tri, x):
    hi = x.astype(BF16)
    rest = x - hi.astype(F32)
    mid = rest.astype(BF16)
    lo = (rest - mid.astype(F32)).astype(BF16)
    tri = tri.astype(BF16)
    return (jnp.dot(tri, hi, preferred_element_type=F32) + jnp.dot(tri, mid, preferred_element_type=F32)
            + jnp.dot(tri, lo, preferred_element_type=F32))


MASKED = -1e30


def _pair_masks(mask_ref, n, upper):
    row = lax.broadcasted_iota(jnp.int32, (n, HEAD), 0)
    for i in range(n):
        mask_ref[i] = jnp.where((row <= i) if upper else (row >= i), 0.0, MASKED).astype(F32)


def _hg_gates(fz, lb):
    sg = _sigmoid(fz)
    f = lb + (1.0 - lb) * sg
    kk = (1.0 - lb) * (1.0 - sg)
    return sg, f, jnp.log(f), kk


def _hgrn2_fwd(p, lb, onorm, *, name):
    S = p.shape[0]
    C = HG_SUB
    tb = _tile(S, 256, C)
    nsub = tb // C

    def body(q_ref, fz_ref, v_ref, g_ref, lb_ref, on_ref, tok_ref, o_ref, st_ref, state, b_blk, k_blk, bsc, ksc, vsc):
        @pl.when(pl.program_id(0) == 0)
        def _():
            state[...] = jnp.zeros_like(state)

        _, _, lg, kk = _hg_gates(fz_ref[...], lb_ref[...])
        b_blk[...] = _running_sum(_chunk_tri(tb, C, False), lg)
        k_blk[...] = kk
        tt = lax.broadcasted_iota(jnp.int32, (C, HEAD), 0)

        def sub(c, carry):
            rows = pl.ds(pl.multiple_of(c * C, C), C)
            for h in range(N_HEADS):
                cols = slice(h * HEAD, (h + 1) * HEAD)
                qv = q_ref[rows, cols]
                vv = v_ref[rows, cols]
                b = b_blk[rows, cols]
                kk = k_blk[rows, cols]
                st0 = state[h]
                st0x = st0.astype(MXU_DTYPE)
                st_ref[c, h] = st0x.astype(st_ref.dtype)
                inter = lax.dot_general((qv * jnp.exp(b)).astype(MXU_DTYPE), st0x,
                                        (((1,), (1,)), ((), ())), preferred_element_type=F32)
                bsc[h] = b
                ksc[h] = kk
                vsc[h] = vv
                intra = jnp.zeros((C, HEAD), F32)
                for s in range(C):
                    dec = jnp.where(tt >= s, jnp.exp(b - bsc[h, pl.ds(s, 1), :]), 0.0)
                    a_s = jnp.sum(qv * ksc[h, pl.ds(s, 1), :] * dec, axis=-1, keepdims=True)
                    intra = intra + a_s * vsc[h, pl.ds(s, 1), :]
                o_ref[rows, cols] = inter + intra
                b_last = bsc[h, pl.ds(C - 1, 1), :]
                ke = kk * jnp.exp(b_last - b)
                state[h] = st0 * jnp.exp(b_last) + lax.dot_general(
                    vv.astype(MXU_DTYPE), ke.astype(MXU_DTYPE), (((0,), (0,)), ((), ())),
                    preferred_element_type=F32)
            return carry

        lax.fori_loop(0, nsub, sub, 0, unroll=2)

        for h in range(N_HEADS):
            cols = slice(h * HEAD, (h + 1) * HEAD)
            o = o_ref[:, cols]
            gv = g_ref[:, cols]
            n = o * lax.rsqrt(jnp.mean(o * o, axis=-1, keepdims=True) + EPS)
            tok_ref[:, cols] = (n * (gv * _sigmoid(gv)) * on_ref[:, cols]).astype(tok_ref.dtype)

    blk = lambda c: pl.BlockSpec((tb, D_TOK), lambda i, c=c: (i, c))
    vec = pl.BlockSpec((1, D_TOK), lambda i: (0, 0))
    stb = pl.BlockSpec((nsub, N_HEADS, HEAD, HEAD), lambda i: (i, 0, 0, 0))
    return pl.pallas_call(
        body, name=name, grid=(S // tb,), in_specs=[blk(0), blk(1), blk(2), blk(3), vec, vec],
        out_specs=[blk(0), blk(0), stb],
        out_shape=[jax.ShapeDtypeStruct((S, D_MODEL), BF16), jax.ShapeDtypeStruct((S, D_TOK), F32),
                   jax.ShapeDtypeStruct((S // C, N_HEADS, HEAD, HEAD), BF16)],
        scratch_shapes=[pltpu.VMEM((N_HEADS, HEAD, HEAD), F32)] + [pltpu.VMEM((tb, D_TOK), F32)] * 2
        + [pltpu.VMEM((N_HEADS, C, HEAD), F32)] * 3,
        compiler_params=_params(("arbitrary",)))(p, p, p, p, lb, onorm)


def _hgrn2_bwd(p, lb, onorm, o, states, dheads, *, name):
    S = p.shape[0]
    C = HG_SUB
    tb = _tile(S, 256, C)
    nsub = tb // C
    nblk = S // tb

    def body(q_ref, fz_ref, v_ref, g_ref, lb_ref, on_ref, o_ref, st_ref, dt_ref, dp_ref, dlb_ref, don_ref, dstate,
             b_blk, k_blk, do_blk, db_blk, dk_blk, dq_blk, dv_blk, bsc, ksc, vsc, qsc, dosc, causal, anti):
        @pl.when(pl.program_id(0) == 0)
        def _():
            dstate[...] = jnp.zeros_like(dstate)
            dlb_ref[...] = jnp.zeros_like(dlb_ref)
            don_ref[...] = jnp.zeros_like(don_ref)

        for h in range(N_HEADS):
            cols = slice(h * HEAD, (h + 1) * HEAD)
            onv = on_ref[:, cols]
            gv = g_ref[:, cols]
            ov = o_ref[:, cols]
            dt = dt_ref[:, cols].astype(F32)
            sgg = _sigmoid(gv)
            sil = gv * sgg
            rinv = lax.rsqrt(jnp.mean(ov * ov, axis=-1, keepdims=True) + EPS)
            n = ov * rinv
            don_ref[:, cols] += jnp.sum(dt * n * sil, axis=0, keepdims=True)
            dn = dt * sil * onv
            dp_ref[:, 3 * D_TOK + h * HEAD:3 * D_TOK + (h + 1) * HEAD] = (
                dt * n * onv * sgg * (1.0 + gv * (1.0 - sgg))).astype(dp_ref.dtype)
            do_blk[:, cols] = rinv * (dn - n * jnp.mean(dn * n, axis=-1, keepdims=True))
        _, _, lg, kk = _hg_gates(fz_ref[...], lb_ref[...])
        b_blk[...] = _running_sum(_chunk_tri(tb, C, False), lg)
        k_blk[...] = kk
        _pair_masks(causal, C, False)
        _pair_masks(anti, C, True)
        tt = lax.broadcasted_iota(jnp.int32, (C, HEAD), 0)

        def sub(j, heads):
            c = nsub - 1 - j
            rows = pl.ds(pl.multiple_of(c * C, C), C)
            for h in heads:
                cols = slice(h * HEAD, (h + 1) * HEAD)
                qv = q_ref[rows, cols]
                vv = v_ref[rows, cols]
                do = do_blk[rows, cols]
                b = b_blk[rows, cols]
                kk = k_blk[rows, cols]
                bsc[h] = b
                ksc[h] = kk
                vsc[h] = vv
                qsc[h] = qv
                dosc[h] = do
                b_last = bsc[h, pl.ds(C - 1, 1), :]
                eb = jnp.exp(b)
                qe = qv * eb
                ebb = jnp.exp(b_last - b)
                ke = kk * ebb
                e_last = jnp.exp(b_last)
                st0x = st_ref[c, h].astype(MXU_DTYPE)
                st0 = st0x.astype(F32)
                dst1 = dstate[h]
                dst1x = dst1.astype(MXU_DTYPE)
                dox = do.astype(MXU_DTYPE)
                dqe = jnp.dot(dox, st0x, preferred_element_type=F32)
                dke = jnp.dot(vv.astype(MXU_DTYPE), dst1x, preferred_element_type=F32)
                dv = lax.dot_general(ke.astype(MXU_DTYPE), dst1x, (((1,), (1,)), ((), ())),
                                     preferred_element_type=F32)
                db_last = (e_last * jnp.sum(st0 * dst1, axis=0, keepdims=True)
                           + jnp.sum(dke * ke, axis=0, keepdims=True))
                dstate[h] = dst1 * e_last + lax.dot_general(dox, qe.astype(MXU_DTYPE), (((0,), (0,)), ((), ())),
                                                            preferred_element_type=F32)
                dq_pairs = jnp.zeros((C, HEAD), F32)
                for s in range(C):
                    dec = jnp.exp(b - bsc[h, pl.ds(s, 1), :] + causal[s])
                    da_s = jnp.sum(do * vsc[h, pl.ds(s, 1), :], axis=-1, keepdims=True)
                    dq_pairs = dq_pairs + da_s * (ksc[h, pl.ds(s, 1), :] * dec)
                dk_pairs = jnp.zeros((C, HEAD), F32)
                for t in range(C):
                    do_t = dosc[h, pl.ds(t, 1), :]
                    qd = qsc[h, pl.ds(t, 1), :] * jnp.exp(bsc[h, pl.ds(t, 1), :] - b + anti[t])
                    da_t = jnp.sum(vv * do_t, axis=-1, keepdims=True)
                    dk_pairs = dk_pairs + da_t * qd
                    a_t = jnp.sum(qd * kk, axis=-1, keepdims=True)
                    dv = dv + a_t * do_t
                db = dqe * qe - dke * ke + qv * dq_pairs - kk * dk_pairs
                db_blk[rows, cols] = db + jnp.where(tt == C - 1, db_last, 0.0)
                dk_blk[rows, cols] = dke * ebb + dk_pairs
                dq_blk[rows, cols] = dqe * eb + dq_pairs
                dv_blk[rows, cols] = dv

        for first in range(0, N_HEADS, HG_GROUP):
            heads = tuple(range(first, first + HG_GROUP))
            pl.loop(0, nsub)(lambda j, heads=heads: sub(j, heads))

        dlg = _running_sum(_chunk_tri(tb, C, True), db_blk[...])
        lbv = lb_ref[...]
        sg, f, _, _ = _hg_gates(fz_ref[...], lbv)
        w = dlg / f - dk_blk[...]
        dp_ref[:, 0:D_TOK] = dq_blk[...].astype(dp_ref.dtype)
        dp_ref[:, 2 * D_TOK:3 * D_TOK] = dv_blk[...].astype(dp_ref.dtype)
        dp_ref[:, D_TOK:2 * D_TOK] = (w * (1.0 - lbv) * sg * (1.0 - sg)).astype(dp_ref.dtype)
        dlb_ref[...] += jnp.sum(w * (1.0 - sg), axis=0, keepdims=True)

    blk = lambda c: pl.BlockSpec((tb, D_TOK), lambda i, c=c: (nblk - 1 - i, c))
    vec = pl.BlockSpec((1, D_TOK), lambda i: (0, 0))
    stb = pl.BlockSpec((nsub, N_HEADS, HEAD, HEAD), lambda i: (nblk - 1 - i, 0, 0, 0))
    small = jax.ShapeDtypeStruct((1, D_TOK), F32)
    return pl.pallas_call(
        body, name=name, grid=(nblk,), in_specs=[blk(0), blk(1), blk(2), blk(3), vec, vec, blk(0), stb, blk(0)],
        out_specs=[pl.BlockSpec((tb, 4 * D_TOK), lambda i: (nblk - 1 - i, 0)), vec, vec],
        out_shape=[jax.ShapeDtypeStruct((S, HG_IN), BF16), small, small],
        scratch_shapes=[pltpu.VMEM((N_HEADS, HEAD, HEAD), F32)] + [pltpu.VMEM((tb, D_TOK), F32)] * 7
        + [pltpu.VMEM((N_HEADS, C, HEAD), F32)] * 5 + [pltpu.VMEM((C, C, HEAD), F32)] * 2,
        compiler_params=_params(("arbitrary",)))(p, p, p, p, lb, onorm, o, states, dheads)


def _adamw(w, g, m, v, *, name):
    shape = w.shape
    cols = shape[-1]
    w2, g2, m2, v2 = (t.reshape(-1, cols) for t in (w, g, m, v))
    R = w2.shape[0]
    tr = _tile(R, 512, 8)

    def body(w_ref, g_ref, m_ref, v_ref, d_ref, nm_ref, nv_ref):
        gv = g_ref[...]
        nm = ADAM_B1 * m_ref[...] + (1.0 - ADAM_B1) * gv
        nv = ADAM_B2 * v_ref[...] + (1.0 - ADAM_B2) * (gv * gv)
        m_hat = nm / (1.0 - ADAM_B1 ** ADAM_STEP)
        v_hat = nv / (1.0 - ADAM_B2 ** ADAM_STEP)
        d_ref[...] = -ADAM_LR * (m_hat / (jnp.sqrt(v_hat) + ADAM_EPS) + ADAM_WD * w_ref[...])
        nm_ref[...] = nm
        nv_ref[...] = nv

    spec = pl.BlockSpec((tr, cols), lambda i: (i, 0))
    out = jax.ShapeDtypeStruct((R, cols), F32)
    d, nm, nv = pl.pallas_call(body, name=name, grid=(R // tr,), in_specs=[spec] * 4, out_specs=[spec] * 3,
                               out_shape=[out] * 3, compiler_params=_params(("parallel",)))(w2, g2, m2, v2)
    return d.reshape(shape), nm.reshape(shape), nv.reshape(shape)


def _add_received(sent, got, me, *, name):
    n, R, Cc = got.shape
    tr = _tile(R, 256, 16)
    per = R // tr

    def body(me_ref, a_ref, b_ref, o_ref):
        del me_ref
        acc = a_ref[...].astype(F32)
        for k in range(n):
            acc = acc + b_ref[k].astype(F32)
        o_ref[...] = acc

    grid_spec = pltpu.PrefetchScalarGridSpec(
        num_scalar_prefetch=1, grid=(per,),
        in_specs=[pl.BlockSpec((tr, Cc), lambda i, me_ref: (me_ref[0] * per + i, 0)),
                  pl.BlockSpec((n, tr, Cc), lambda i, me_ref: (0, i, 0))],
        out_specs=pl.BlockSpec((tr, Cc), lambda i, me_ref: (i, 0)))
    return pl.pallas_call(body, name=name, grid_spec=grid_spec, out_shape=jax.ShapeDtypeStruct((R, Cc), F32),
                          compiler_params=_params(("parallel",)))(jnp.reshape(me, (1,)).astype(jnp.int32), sent, got)


def _sum_blocks(x, *, name):
    n, R, Cc = x.shape
    tr = _tile(R, 208, 8)

    def body(x_ref, o_ref):
        acc = x_ref[0]
        for k in range(1, n):
            acc = acc + x_ref[k]
        o_ref[...] = acc

    return pl.pallas_call(
        body, name=name, grid=(R // tr,), in_specs=[pl.BlockSpec((n, tr, Cc), lambda i: (0, i, 0))],
        out_specs=pl.BlockSpec((tr, Cc), lambda i: (i, 0)), out_shape=jax.ShapeDtypeStruct((R, Cc), F32),
        compiler_params=_params(("parallel",)))(x)


def _place():
    return lax.axis_index("x"), lax.axis_index("y"), lax.axis_index("c")


def _all_gather(x, *, name, in_vmem, reduce_sum=False, with_token=False):
    R, Cc = x.shape
    space = pltpu.VMEM if in_vmem else pl.ANY

    def body(x_ref, out_ref, *scratch):
        if with_token:
            scratch[0][...] = jnp.zeros_like(scratch[0])
            scratch = scratch[1:]
        if reduce_sum:
            gat_ref, send_sems, recv_sems, local_sem = scratch
        else:
            gat_ref = out_ref
            send_sems, recv_sems, local_sem = scratch
        mx, my, mc = _place()
        me, sibling = (mx, my, mc), (mx, my, 1 - mc)
        chips = [(1 - mx, my), (mx, 1 - my), (1 - mx, 1 - my)]

        def rows(px, py, pc):
            return gat_ref.at[pl.ds((4 * px + 2 * py + pc) * R, R), :]

        def copy(k, block, to, src=None):
            return pltpu.make_async_remote_copy(
                src_ref=rows(*block) if src is None else src, dst_ref=rows(*block), send_sem=send_sems.at[k],
                recv_sem=recv_sems.at[k], device_id=to, device_id_type=MESH_ID)

        mine = pltpu.make_async_copy(x_ref, rows(*me), local_sem)
        mine.start()
        first = [copy(0, me, sibling, src=x_ref)]
        first += [copy(1 + j, me, (*chip, mc), src=x_ref) for j, chip in enumerate(chips)]
        for cp in first:
            cp.start()
        passed = [copy(4 + j, (*chip, mc), sibling) for j, chip in enumerate(chips)]
        for j, chip in enumerate(chips):
            copy(1 + j, (*chip, mc), me).wait_recv()
            passed[j].start()
        copy(0, sibling, me).wait_recv()
        for j, chip in enumerate(chips):
            copy(4 + j, (*chip, 1 - mc), me).wait_recv()
        for cp in first + passed:
            cp.wait_send()
        mine.wait()
        if reduce_sum:
            acc = gat_ref[pl.ds(0, R), :]
            for d in range(1, N_DEV):
                acc = acc + gat_ref[pl.ds(d * R, R), :]
            out_ref[...] = acc

    sems = [pltpu.SemaphoreType.DMA((7,)), pltpu.SemaphoreType.DMA((7,)), pltpu.SemaphoreType.DMA]
    if reduce_sum:
        assert in_vmem
        out_shape = jax.ShapeDtypeStruct((R, Cc), x.dtype)
        scratch = [pltpu.VMEM((N_DEV * R, Cc), x.dtype)] + sems
    else:
        out_shape = jax.ShapeDtypeStruct((N_DEV * R, Cc), x.dtype)
        scratch = sems
    out_specs = pl.BlockSpec(memory_space=space)
    if with_token:
        out_shape = (out_shape, jax.ShapeDtypeStruct((8, LANE), F32))
        out_specs = (out_specs, pl.BlockSpec(memory_space=pltpu.VMEM))
    return pl.pallas_call(
        body, name=name, out_shape=out_shape, in_specs=[pl.BlockSpec(memory_space=space)], out_specs=out_specs,
        scratch_shapes=scratch, compiler_params=pltpu.CompilerParams(vmem_limit_bytes=VMEM_LIMIT))(x)


def _peer(k, mx, my, mc):
    bits = k + 1
    return (1 - mx if bits & 4 else mx, 1 - my if bits & 2 else my, 1 - mc if bits & 1 else mc)


HBM_SPEC = pl.BlockSpec(memory_space=pltpu.HBM)
SEM_SPEC = pl.BlockSpec(memory_space=pltpu.SEMAPHORE)
DATAFLOW = pltpu.SideEffectType.DATAFLOW_SIDE_EFFECTING


def _exchange_copies(x_refs, land_refs, send_sems, recv_sems, scatter):
    mx, my, mc = _place()
    me = 4 * mx + 2 * my + mc
    n = len(x_refs)
    copies = []
    for k in range(N_DEV - 1):
        px, py, pc = _peer(k, mx, my, mc)
        for m, (x_ref, land_ref) in enumerate(zip(x_refs, land_refs)):
            rows = land_ref.shape[1] if scatter else x_ref.shape[0]
            if scatter:
                src = x_ref.at[pl.ds(pl.multiple_of((4 * px + 2 * py + pc) * rows, 16), rows), :]
                dst = land_ref.at[k]
            else:
                src = x_ref
                dst = land_ref.at[pl.ds(pl.multiple_of(me * rows, 16), rows), :]
            copies.append(pltpu.make_async_remote_copy(
                src_ref=src, dst_ref=dst, send_sem=send_sems.at[k * n + m], recv_sem=recv_sems.at[k * n + m],
                device_id=(px, py, pc), device_id_type=MESH_ID))
    return copies


def _land_shape(x, scatter):
    return (N_DEV - 1, x.shape[0] // N_DEV, x.shape[1]) if scatter else (N_DEV * x.shape[0], x.shape[1])


def _own_copies(x_refs, land_refs, local_sems):
    mx, my, mc = _place()
    me = 4 * mx + 2 * my + mc
    return [pltpu.make_async_copy(
        x_ref, land_ref.at[pl.ds(pl.multiple_of(me * x_ref.shape[0], 16), x_ref.shape[0]), :], local_sems.at[m])
        for m, (x_ref, land_ref) in enumerate(zip(x_refs, land_refs))]


def _exchange_start(groups, *, name, scatter):
    sizes = [len(g) for g in groups]
    xs = [x for g in groups for x in g]
    n = len(xs)
    lands = [lax.empty(_land_shape(x, scatter), x.dtype) for x in xs]
    per = 2 if scatter else 3

    def body(*refs):
        sems = refs[2 * n:2 * n + per * len(groups)]
        token = refs[-1]
        off = 0
        for gi, m in enumerate(sizes):
            x_refs, land_refs = refs[off:off + m], refs[n + off:n + off + m]
            for cp in _exchange_copies(x_refs, land_refs, sems[per * gi], sems[per * gi + 1], scatter):
                cp.start()
            if not scatter:
                for cp in _own_copies(x_refs, land_refs, sems[per * gi + 2]):
                    cp.start()
            off += m
        token[...] = jnp.zeros_like(token)

    sem_shapes = []
    for m in sizes:
        sem_shapes += [pltpu.SemaphoreType.DMA(((N_DEV - 1) * m,))] * 2
        if not scatter:
            sem_shapes.append(pltpu.SemaphoreType.DMA((m,)))
    ns = len(sem_shapes)
    out = pl.pallas_call(
        body, name=name,
        out_shape=(*sem_shapes, *[pltpu.HBM(x.shape, x.dtype) for x in xs],
                   *[pltpu.HBM(l.shape, l.dtype) for l in lands], jax.ShapeDtypeStruct((8, LANE), F32)),
        in_specs=(HBM_SPEC,) * (2 * n),
        out_specs=(SEM_SPEC,) * ns + (HBM_SPEC,) * (2 * n) + (pl.BlockSpec(memory_space=pltpu.VMEM),),
        input_output_aliases={i: ns + i for i in range(2 * n)},
        compiler_params=pltpu.CompilerParams(has_side_effects=DATAFLOW))(
            *[pltpu.with_memory_space_constraint(t, pltpu.HBM) for t in xs + lands])
    started, off = [], 0
    for gi, m in enumerate(sizes):
        sems = out[per * gi:per * gi + per]
        started.append((sems[0], sems[1], list(out[ns + off:ns + off + m]),
                        list(out[ns + n + off:ns + n + off + m]), out[-1], None if scatter else sems[2]))
        off += m
    return started


def _exchange_wait(started, after, *, name, scatter):
    send_sems, recv_sems, xs, lands, _, local_sems = started
    n = len(xs)

    def body(*refs):
        x_refs, land_refs = refs[:n], refs[n:2 * n]
        for cp in _exchange_copies(x_refs, land_refs, refs[2 * n], refs[2 * n + 1], scatter):
            cp.wait_send()
            cp.wait_recv()
        if not scatter:
            for cp in _own_copies(x_refs, land_refs, refs[2 * n + 2]):
                cp.wait()

    sems = (send_sems, recv_sems) if scatter else (send_sems, recv_sems, local_sems)
    out = pl.pallas_call(
        body, name=name, out_shape=tuple(pltpu.HBM(t.shape, t.dtype) for t in xs + lands),
        in_specs=(HBM_SPEC,) * (2 * n) + (SEM_SPEC,) * len(sems) + (pl.BlockSpec(memory_space=pl.ANY),),
        out_specs=(HBM_SPEC,) * (2 * n), input_output_aliases={i: i for i in range(2 * n)},
        compiler_params=pltpu.CompilerParams(has_side_effects=DATAFLOW))(*xs, *lands, *sems, after)
    return list(out[:n]), list(out[n:])


def _gather_start(groups, token, *, name):
    first = groups[0]
    groups = [[first[0] + token[0, 0].astype(first[0].dtype)] + list(first[1:])] + [list(g) for g in groups[1:]]
    return _exchange_start(groups, name=name, scatter=False)


def _gather_finish(started, after, *, name):
    return _exchange_wait(started, after, name=name, scatter=False)[1]


def _reduce_start(grads, *, name):
    return _exchange_start([grads], name=name, scatter=True)[0]


def _reduce_finish(started, after, me, *, name):
    sent, gots = _exchange_wait(started, after, name=name + "_wait", scatter=True)
    return [_add_received(g, got, me, name=f"{name}_add{m}") for m, (g, got) in enumerate(zip(sent, gots))]


def _pad_rows(a, mult):
    r = (-a.shape[0]) % mult
    return a if r == 0 else jnp.concatenate([a, jnp.zeros((r,) + a.shape[1:], a.dtype)], axis=0)


def kernel(x, mem, mix_norm, mem_norm, w_mem_kv, w_out, hg_w_in, hg_lb, hg_onorm, gm_w_in, gm_ln_g, gm_ln_b, gm_ws, gm_bs, ffn_norm, w_ffn_in, w_ffn_out, final_norm, loss_target, m_mix_norm, m_mem_norm, m_w_mem_kv, m_w_out, m_hg_w_in, m_hg_lb, m_hg_onorm, m_gm_w_in, m_gm_ln_g, m_gm_ln_b, m_gm_ws, m_gm_bs, m_ffn_norm, m_w_ffn_in, m_w_ffn_out, m_final_norm, v_mix_norm, v_mem_norm, v_w_mem_kv, v_w_out, v_hg_w_in, v_hg_lb, v_hg_onorm, v_gm_w_in, v_gm_ln_g, v_gm_ln_b, v_gm_ws, v_gm_bs, v_ffn_norm, v_w_ffn_in, v_w_ffn_out, v_final_norm):
    mx, my, mc = _place()
    me = 4 * mx + 2 * my + mc
    xs = x[0]
    mems = mem[0]
    tgt = loss_target[0]

    hg_t = hg_w_in[0].T.astype(BF16)
    gm_t = gm_w_in[0].T.astype(BF16)
    fi_t = [w_ffn_in[i].T.astype(BF16) for i in range(2)]
    kv_b = [w_mem_kv[i].astype(BF16) for i in range(2)]
    out_b = [w_out[i].astype(BF16) for i in range(2)]
    fo_b = [w_ffn_out[i].astype(BF16) for i in range(2)]
    ln_local = _pad_rows(jnp.concatenate([gm_ln_g, gm_ln_b], axis=0), 8)
    ln_local = jnp.concatenate([ln_local, jnp.zeros((8, LANE - ln_local.shape[1]), F32)], axis=1)
    ln_all, token = _all_gather(ln_local, name="gather_ln", in_vmem=True, with_token=True)
    ln_all = ln_all.reshape(N_DEV, 8, LANE)
    ln_g = ln_all[:, 0, :D_TOK // N_DEV].reshape(1, D_TOK)
    ln_b = ln_all[:, 1, :D_TOK // N_DEV].reshape(1, D_TOK)
    W_hgT, token = _all_gather(hg_t + token[0, 0].astype(BF16), name="gather_first", in_vmem=False,
                               with_token=True)
    gather_mix, fi0, fo0, gather_gm, fi1, fo1 = _gather_start(
        [kv_b + out_b, [fi_t[0]], [fo_b[0]], [gm_t], [fi_t[1]], [fo_b[1]]], token, name="gather_rest_start")
    gather_fi, gather_fo = [fi0, fi1], [fo0, fo1]

    lb_soft = jax.nn.softmax(hg_lb, axis=0)
    lb0 = lb_soft[0:1]
    bsb = jnp.broadcast_to(gm_bs[0][:, :, None], (N_HEADS, GM_CHUNK, GM_CHUNK))
    ws = gm_ws[0]

    W_fiT, W_fo = [], []

    def ffn_fwd(xin, hf, i, **tail):
        W_fiT.extend(_gather_finish(gather_fi[i], hf, name=f"gather_fi{i}_wait"))
        gu, act = _ffn_in(hf, W_fiT[i], name=f"ffn_in{i}")
        W_fo.extend(_gather_finish(gather_fo[i], act, name=f"gather_fo{i}_wait"))
        return gu, act, _matmul(act, W_fo[i], res=xin, name=f"ffn_out{i}", **tail)

    h0 = _rms_fwd(xs, mix_norm[0:1], name="mix_norm0", dep=gather_mix[4])
    p0 = _matmul(h0, W_hgT, tb=True, name="hg_in")
    heads0, o0, states = _hgrn2_fwd(p0, lb0, hg_onorm, name="hgrn2_fwd")

    kv0, kv1, wo0, wo1 = _gather_finish(gather_mix, o0, name="gather_mix_wait")
    W_kv, W_out = [kv0, kv1], [wo0, wo1]
    mem_n, kv = [], []
    for i in range(2):
        mn = _rms_fwd(mems, mem_norm[i:i + 1], name=f"mem_norm{i}")
        mem_n.append(mn)
        kv.append(_matmul(mn, W_kv[i], name=f"mem_kv{i}"))

    heads0 = _attn_fwd(p0, 4 * D_TOK // D_MEM, kv[0], heads0, name="attn_fwd0")
    x1, hf0 = _matmul(heads0, W_out[0], res=xs, norm_gain=ffn_norm[0:1], name="out_proj0")
    gu0, act0, (x2, h1) = ffn_fwd(x1, hf0, 0, norm_gain=mix_norm[1:2])

    W_gmT, = _gather_finish(gather_gm, h1, name="gather_gm_wait")
    p1 = _matmul(h1, W_gmT, tb=True, name="gm_in")
    heads1 = _gmlp_fwd(p1, ln_g, ln_b, ws, bsb, name="gmlp_fwd")
    heads1 = _attn_fwd(p1, 2 * D_TOK // D_MEM, kv[1], heads1, name="attn_fwd1")
    x3, hf1 = _matmul(heads1, W_out[1], res=x2, norm_gain=ffn_norm[1:2], name="out_proj1")
    gu1, act1, (dx, g_final, loss_part) = ffn_fwd(x3, hf1, 1, loss_head=(final_norm.reshape(1, D_MODEL), tgt))

    def ffn_bwd(dx, xin, hf, gu, act, i, dep):
        dgu = _ffn_out_dx(dx, W_fo[i], gu, dep, name=f"ffn_out_dx{i}")
        g_wfo = _matmul(act, dx, ta=True, out_dtype=BF16, name=f"ffn_out_dw{i}")
        g_wfi_t = _matmul(dgu, hf, ta=True, a_halves=True, out_dtype=BF16, name=f"ffn_in_dw{i}")
        dx, g_norm = _matmul(dgu, W_fiT[i], a_halves=True, res=dx, norm_bwd=(xin, ffn_norm[i:i + 1]),
                             name=f"ffn_in_dx{i}")
        return dx, g_wfi_t, g_wfo, g_norm

    def mem_bwd(dkv, i):
        g_wkv = _matmul(mem_n[i], dkv, ta=True, out_dtype=BF16, name=f"mem_kv_dw{i}")
        dmn = _matmul(dkv, W_kv[i], tb=True, name=f"mem_kv_dx{i}")
        _, g_norm = _rms_bwd(mems, mem_norm[i:i + 1], dmn, jnp.zeros_like(mems), name=f"mem_norm_bwd{i}")
        return g_wkv, g_norm

    dx, g_wfi1_t, g_wfo1, g_ffn1 = ffn_bwd(dx, x3, hf1, gu1, act1, 1, loss_part)
    dheads = _matmul(dx, W_out[1], tb=True, name="out_proj_dx1")
    g_wout1 = _matmul(heads1, dx, ta=True, out_dtype=BF16, name="out_proj_dw1")
    dp, g_ws, g_bs, g_lng, g_lnb = _gmlp_bwd(p1, ln_g, ln_b, ws, bsb, dheads, name="gmlp_bwd")
    dp, dk, dv = _attn_bwd(p1, 2 * D_TOK // D_MEM, kv[1], dheads, dp, name="attn_bwd1")
    g_wkv1, g_mem1 = mem_bwd(jnp.concatenate([dk, dv], axis=1), 1)
    g_wgm_t = _matmul(dp, h1, ta=True, out_dtype=BF16, name="gm_in_dw")
    dx, g_mix1 = _matmul(dp, W_gmT, res=dx, norm_bwd=(x2, mix_norm[1:2]), name="gm_in_dx")
    reduce_l1 = _reduce_start([g_wkv1, g_wout1, g_wgm_t, g_wfi1_t, g_wfo1], name="reduce_l1_start")
    early = [loss_part, g_final, g_ffn1, g_mix1, g_mem1, g_ws.reshape(1, -1), g_bs.reshape(1, -1), g_lng, g_lnb]
    early_rows = _pad_rows(jnp.concatenate(early, axis=1).reshape(-1, LANE), 16) + reduce_l1[4][0, 0]
    small_early = _exchange_start([[early_rows]], name="reduce_small_start", scatter=False)[0]

    dx, g_wfi0_t, g_wfo0, g_ffn0 = ffn_bwd(dx, x1, hf0, gu0, act0, 0, small_early[4])
    reduce_ffn0 = _reduce_start([g_wfi0_t, g_wfo0], name="reduce_ffn0_start")
    dheads = _matmul(dx, W_out[0], tb=True, name="out_proj_dx0", dep=reduce_ffn0[4])
    g_wout0 = _matmul(heads0, dx, ta=True, out_dtype=BF16, name="out_proj_dw0")
    dp, g_lb0, g_onorm = _hgrn2_bwd(p0, lb0, hg_onorm, o0, states, dheads, name="hgrn2_bwd")
    dp, dk, dv = _attn_bwd(p0, 4 * D_TOK // D_MEM, kv[0], dheads, dp, name="attn_bwd0")
    g_wkv0, g_mem0 = mem_bwd(jnp.concatenate([dk, dv], axis=1), 0)
    g_whg_t = _matmul(dp, h0, ta=True, out_dtype=BF16, name="hg_in_dw")
    reduce_mix0 = _reduce_start([g_wkv0, g_wout0, g_whg_t], name="reduce_mix0_start")
    grad_x, g_mix0 = _matmul(dp, W_hgT, res=dx, norm_bwd=(xs, mix_norm[0:1]), name="hg_in_dx", dep=reduce_mix0[4])

    g_kv1, g_out1, g_gm_t, g_fi1_t, g_fo1 = _reduce_finish(reduce_l1, grad_x, me, name="reduce_l1")
    g_fi0_t, g_fo0 = _reduce_finish(reduce_ffn0, g_kv1, me, name="reduce_ffn0")
    g_kv0, g_out0, g_hg_t = _reduce_finish(reduce_mix0, g_fi0_t, me, name="reduce_mix0")
    g_shards = [jnp.stack([g_kv0, g_kv1]), jnp.stack([g_out0, g_out1]), g_hg_t[None], g_gm_t[None],
                jnp.stack([g_fi0_t, g_fi1_t]), jnp.stack([g_fo0, g_fo1])]
    transposed = (4, 7, 13)

    late = [g_ffn0, g_lb0, g_onorm, g_mem0, g_mix0]
    late_rows = _pad_rows(jnp.concatenate(late, axis=1).reshape(-1, LANE), 8)
    red_late = _all_gather(late_rows, name="reduce_small_late", in_vmem=True, reduce_sum=True)
    gathered = _exchange_wait(small_early, red_late, name="reduce_small_wait", scatter=False)[1][0]
    red_early = _sum_blocks(gathered.reshape(N_DEV, -1, LANE), name="reduce_small_sum")

    def split(flat, parts):
        out, off = [], 0
        for t in parts:
            out.append(flat[off:off + t.shape[1]])
            off += t.shape[1]
        return out

    r_loss, r_final, r_ffn1, r_mix1, r_mem1, r_ws, r_bs, r_lng, r_lnb = split(red_early.reshape(-1), early)
    r_ffn0, r_lb0, r_onorm, r_mem0, r_mix0 = split(red_late.reshape(-1), late)
    loss = r_loss[0]
    g_mix_norm = jnp.stack([r_mix0, r_mix1])
    g_mem_norm = jnp.stack([r_mem0, r_mem1])
    g_hg_lb = r_lb0[None, :] * lb0 * (jnp.eye(3, dtype=F32)[:, 0:1] - lb_soft)
    g_hg_onorm = r_onorm.reshape(1, D_TOK)
    width = D_TOK // N_DEV
    g_gm_ln_g = lax.dynamic_slice(r_lng, (me * width,), (width,)).reshape(1, width)
    g_gm_ln_b = lax.dynamic_slice(r_lnb, (me * width,), (width,)).reshape(1, width)
    g_gm_ws = r_ws.reshape(gm_ws.shape)
    g_gm_bs = r_bs.reshape(gm_bs.shape)
    g_ffn_norm = jnp.stack([r_ffn0, r_ffn1])
    g_final_norm = r_final

    grads = [g_mix_norm, g_mem_norm, g_shards[0], g_shards[1], g_shards[2], g_hg_lb, g_hg_onorm, g_shards[3],
             g_gm_ln_g, g_gm_ln_b, g_gm_ws, g_gm_bs, g_ffn_norm, g_shards[4], g_shards[5], g_final_norm]
    weights = [mix_norm, mem_norm, w_mem_kv, w_out, hg_w_in, hg_lb, hg_onorm, gm_w_in, gm_ln_g, gm_ln_b, gm_ws, gm_bs,
               ffn_norm, w_ffn_in, w_ffn_out, final_norm]
    ms = [m_mix_norm, m_mem_norm, m_w_mem_kv, m_w_out, m_hg_w_in, m_hg_lb, m_hg_onorm, m_gm_w_in, m_gm_ln_g,
          m_gm_ln_b, m_gm_ws, m_gm_bs, m_ffn_norm, m_w_ffn_in, m_w_ffn_out, m_final_norm]
    vs = [v_mix_norm, v_mem_norm, v_w_mem_kv, v_w_out, v_hg_w_in, v_hg_lb, v_hg_onorm, v_gm_w_in, v_gm_ln_g,
          v_gm_ln_b, v_gm_ws, v_gm_bs, v_ffn_norm, v_w_ffn_in, v_w_ffn_out, v_final_norm]
    deltas, new_m, new_v = [], [], []
    for n, (w, g, m, v) in enumerate(zip(weights, grads, ms, vs)):
        if w.ndim == 1:
            d, nm, nv = _adamw(w[None], g.reshape(1, -1), m[None], v[None], name=f"adamw{n}")
            d, nm, nv = d[0], nm[0], nv[0]
        elif n in transposed:
            flip = lambda t: jnp.swapaxes(t, 1, 2)
            d, nm, nv = (flip(t) for t in _adamw(flip(w), g, flip(m), flip(v), name=f"adamw{n}"))
            grads[n] = flip(g)
        else:
            d, nm, nv = _adamw(w, g.reshape(w.shape), m, v, name=f"adamw{n}")
        deltas.append(d)
        new_m.append(nm)
        new_v.append(nv)
    grads = [g.reshape(w.shape) for g, w in zip(grads, weights)]
    return (loss, grad_x[None], *grads, *deltas, *new_m, *new_v)
```

```python
import jax
import jax.numpy as jnp
from jax import lax
from jax.experimental import pallas as pl
from jax.experimental.pallas import tpu as pltpu

F32 = jnp.float32
BF16 = jnp.bfloat16
MXU_DTYPE = jnp.bfloat16
MESH_ID = pl.DeviceIdType.MESH

N_DEV = 8
EPS = 1e-6
D_MODEL = 1024
D_TOK = 768
D_MEM = 256
N_HEADS = 6
HEAD = 128
MEM_HEADS = 4
MEM_HDIM = 64
GM_CHUNK = 128
ATTN_ROWS = 2048
D_FF = 2816
HG_SUB = 16
HG_GROUP = 6
HG_IN = 4 * D_TOK + D_MEM
GM_IN = 2 * D_TOK + D_MEM
LANE = 128
MXU_COLS = 256

ADAM_LR = 0.001
ADAM_B1 = 0.9
ADAM_B2 = 0.999
ADAM_EPS = 1e-08
ADAM_WD = 0.01
ADAM_STEP = 10

VMEM_LIMIT = 48 * 2 ** 20
VMEM_LIMIT_WIDE = 58 * 2 ** 20


def _params(sem=None, limit=VMEM_LIMIT):
    return pltpu.CompilerParams(dimension_semantics=sem, vmem_limit_bytes=limit)


def _tile(n, cap, q=LANE):
    if n <= cap:
        return n
    best = None
    for t in range(q, cap + 1, q):
        if n % t == 0:
            best = t
    assert best is not None, (n, cap, q)
    return best


def _sigmoid(x):
    return 1.0 / (1.0 + jnp.exp(-x))


def _gelu(x, with_grad=False):
    cdf = 0.5 * (1.0 + lax.erf(x * 0.7071067811865476))
    if not with_grad:
        return x * cdf
    return x * cdf, cdf + x * jnp.exp(-0.5 * x * x) * 0.3989422804014327


def _matmul(a, b, *, name, ta=False, tb=False, res=None, out_dtype=F32, a_halves=False, b_halves=False, dep=None,
            norm_gain=None, norm_bwd=None, loss_head=None):
    if a_halves and ta:
        K, M = a.shape[1], 2 * a.shape[2]
    elif a_halves:
        M, K = a.shape[1], 2 * a.shape[2]
    else:
        K, M = a.shape if ta else a.shape[::-1]
    if b_halves:
        assert not tb and b.shape[1] == K
        N = 2 * b.shape[2]
    else:
        N = b.shape[0] if tb else b.shape[1]
        assert (b.shape[1] if tb else b.shape[0]) == K
    tm = _tile(M // 2 if (a_halves and ta) else M, 1664 if ta else 1024)
    tn = _tile(N // 2 if b_halves else N, 1792)
    tk = _tile(K // 2 if (a_halves and not ta) else K, 1024 if ta else 1664)
    nk = K // tk
    dims = (((0 if ta else 1,), (1 if tb else 0,)), ((), ()))

    strips = norm_bwd is not None or loss_head is not None
    fused = norm_gain is not None or strips
    n_in = 2 + (res is not None) + (norm_gain is not None) + 2 * strips + (dep is not None)
    if fused:
        assert tn == N, "the fused norm needs whole rows"
        assert (norm_gain is not None) + (norm_bwd is not None) + (loss_head is not None) == 1
        assert not strips or (res is not None and nk > 1 and tm % LANE == 0)

    def body(*refs):
        a_ref, b_ref = refs[:2]
        r_ref = refs[2] if res is not None else None
        g_ref = refs[2 + (res is not None)] if fused else None
        x_ref = refs[3 + (res is not None)] if strips else None
        o_ref = refs[n_in]
        h_ref = refs[n_in + 1] if fused else None
        l_ref = refs[n_in + 2] if loss_head is not None else None
        acc = None if nk == 1 else refs[-1]
        k = pl.program_id(2)

        def product():
            return lax.dot_general(a_ref[...].astype(MXU_DTYPE), b_ref[...].astype(MXU_DTYPE), dims,
                                   preferred_element_type=F32)

        def finish(r):
            if loss_head is not None:
                @pl.when(pl.program_id(0) == 0)
                def _():
                    h_ref[...] = jnp.zeros_like(h_ref)
                    l_ref[...] = jnp.zeros_like(l_ref)

                acc[...] = r + r_ref[...]
                gv = g_ref[...]

                def strip(s, carry):
                    dg, loss = carry
                    rows = pl.ds(pl.multiple_of(s * LANE, LANE), LANE)
                    xv = acc[rows, :]
                    scale = lax.rsqrt(jnp.mean(xv * xv, axis=-1, keepdims=True) + EPS)
                    xh = xv * scale
                    err = xh * gv - x_ref[rows, :]
                    loss = loss + 0.5 * jnp.sum(jnp.mean(err * err, axis=-1, keepdims=True), axis=0, keepdims=True)
                    dy = err * (1.0 / N)
                    u = dy * gv
                    o_ref[rows, :] = scale * (u - xh * jnp.mean(u * xh, axis=-1, keepdims=True))
                    return dg + jnp.sum(dy * xh, axis=0, keepdims=True), loss

                dg, loss = lax.fori_loop(0, tm // LANE, strip, (jnp.zeros((1, N), F32), jnp.zeros((1, 1), F32)))
                h_ref[...] += dg
                l_ref[...] += jnp.broadcast_to(loss, l_ref.shape)
                return
            if norm_bwd is not None:
                @pl.when(pl.program_id(0) == 0)
                def _():
                    h_ref[...] = jnp.zeros_like(h_ref)

                acc[...] = r
                gv = g_ref[...]

                def strip(s, dg):
                    rows = pl.ds(pl.multiple_of(s * LANE, LANE), LANE)
                    rv = acc[rows, :]
                    xv = x_ref[rows, :]
                    scale = lax.rsqrt(jnp.mean(xv * xv, axis=-1, keepdims=True) + EPS)
                    xh = xv * scale
                    u = rv * gv
                    o_ref[rows, :] = r_ref[rows, :] + scale * (u - xh * jnp.mean(u * xh, axis=-1, keepdims=True))
                    return dg + jnp.sum(rv * xh, axis=0, keepdims=True)

                h_ref[...] += lax.fori_loop(0, tm // LANE, strip, jnp.zeros((1, N), F32))
                return
            if res is not None:
                r = r + r_ref[...].astype(F32)
            o_ref[...] = r.astype(out_dtype)
            if norm_gain is not None:
                scale = lax.rsqrt(jnp.mean(r * r, axis=-1, keepdims=True) + EPS)
                h_ref[...] = (r * scale * g_ref[...]).astype(h_ref.dtype)

        if nk == 1:
            finish(product())
            return

        @pl.when(k == 0)
        def _():
            acc[...] = product()

        @pl.when((k > 0) & (k < nk - 1))
        def _():
            acc[...] += product()

        @pl.when(k == nk - 1)
        def _():
            finish(acc[...] + product())

    if a_halves and ta:
        mh = M // 2 // tm
        a_spec = pl.BlockSpec((None, tk, tm), lambda i, j, k: (i // mh, k, i % mh))
    elif a_halves:
        kh = nk // 2
        a_spec = pl.BlockSpec((None, tm, tk), lambda i, j, k: (k // kh, i, k % kh))
    elif ta:
        a_spec = pl.BlockSpec((tk, tm), lambda i, j, k: (k, i))
    else:
        a_spec = pl.BlockSpec((tm, tk), lambda i, j, k: (i, k))
    if b_halves:
        nh = N // 2 // tn
        b_spec = pl.BlockSpec((None, tk, tn), lambda i, j, k: (j // nh, k, j % nh))
    elif tb:
        b_spec = pl.BlockSpec((tn, tk), lambda i, j, k: (j, k))
    else:
        b_spec = pl.BlockSpec((tk, tn), lambda i, j, k: (k, j))
    o_spec = pl.BlockSpec((tm, tn), lambda i, j, k: (i, j))
    in_specs = [a_spec, b_spec] + ([o_spec] if res is not None else [])
    args = (a, b) + ((res,) if res is not None else ())
    out_specs, out_shape = o_spec, jax.ShapeDtypeStruct((M, N), out_dtype)
    vec = pl.BlockSpec((1, N), lambda i, j, k: (0, 0))
    sem = ("parallel", "parallel", "arbitrary")
    if norm_gain is not None:
        in_specs.append(vec)
        args += (norm_gain,)
        out_specs, out_shape = [o_spec, o_spec], [out_shape, jax.ShapeDtypeStruct((M, N), BF16)]
    if norm_bwd is not None:
        x_in, gain = norm_bwd
        in_specs += [vec, o_spec]
        args += (gain, x_in)
        out_specs, out_shape = [o_spec, vec], [out_shape, jax.ShapeDtypeStruct((1, N), F32)]
        sem = ("arbitrary", "arbitrary", "arbitrary")
    if loss_head is not None:
        gain, target = loss_head
        in_specs += [vec, o_spec]
        args += (gain, target)
        one = pl.BlockSpec((1, LANE), lambda i, j, k: (0, 0))
        out_specs = [o_spec, vec, one]
        out_shape = [out_shape, jax.ShapeDtypeStruct((1, N), F32), jax.ShapeDtypeStruct((1, LANE), F32)]
        sem = ("arbitrary", "arbitrary", "arbitrary")
    if dep is not None:
        in_specs.append(pl.BlockSpec(memory_space=pl.ANY))
        args += (dep,)
    return pl.pallas_call(
        body, name=name, grid=(M // tm, N // tn, nk), in_specs=in_specs, out_specs=out_specs, out_shape=out_shape,
        scratch_shapes=[] if nk == 1 else [pltpu.VMEM((tm, tn), F32)],
        compiler_params=_params(sem, VMEM_LIMIT_WIDE if strips else VMEM_LIMIT))(*args)


def _ffn_in(hf, wt, *, name):
    S, K = hf.shape
    tm = _tile(S, 512)
    tn = _tile(D_FF, 1408)
    nh = D_FF // tn
    nt = (((1,), (1,)), ((), ()))

    def body(a_ref, bg_ref, bu_ref, gu_ref, act_ref):
        av = a_ref[...].astype(MXU_DTYPE)
        for c0 in range(0, tn, MXU_COLS):
            cs = slice(c0, min(c0 + MXU_COLS, tn))
            gate = lax.dot_general(av, bg_ref[cs, :].astype(MXU_DTYPE), nt, preferred_element_type=F32)
            up = lax.dot_general(av, bu_ref[cs, :].astype(MXU_DTYPE), nt, preferred_element_type=F32)
            gu_ref[0, :, cs] = gate.astype(gu_ref.dtype)
            gu_ref[1, :, cs] = up.astype(gu_ref.dtype)
            act_ref[:, cs] = (gate * _sigmoid(gate) * up).astype(act_ref.dtype)

    return pl.pallas_call(
        body, name=name, grid=(nh, S // tm),
        in_specs=[pl.BlockSpec((tm, K), lambda j, i: (i, 0)), pl.BlockSpec((tn, K), lambda j, i: (j, 0)),
                  pl.BlockSpec((tn, K), lambda j, i: (j + nh, 0))],
        out_specs=[pl.BlockSpec((2, tm, tn), lambda j, i: (0, i, j)), pl.BlockSpec((tm, tn), lambda j, i: (i, j))],
        out_shape=[jax.ShapeDtypeStruct((2, S, D_FF), BF16), jax.ShapeDtypeStruct((S, D_FF), BF16)],
        compiler_params=_params(("parallel", "parallel")))(hf, wt, wt)


def _ffn_out_dx(dx, w, gu, dep, *, name):
    S, K = dx.shape
    tm = _tile(S, 1024)
    tn = _tile(D_FF, 1408)

    def body(a_ref, b_ref, gu_ref, dep_ref, o_ref):
        del dep_ref
        av = a_ref[...].astype(MXU_DTYPE)
        for c0 in range(0, tn, MXU_COLS):
            cs = slice(c0, min(c0 + MXU_COLS, tn))
            da = lax.dot_general(av, b_ref[cs, :].astype(MXU_DTYPE), (((1,), (1,)), ((), ())),
                                 preferred_element_type=F32)
            gate = gu_ref[0, :, cs].astype(F32)
            up = gu_ref[1, :, cs].astype(F32)
            sg = _sigmoid(gate)
            o_ref[0, :, cs] = (da * up * sg * (1.0 + gate * (1.0 - sg))).astype(o_ref.dtype)
            o_ref[1, :, cs] = (da * gate * sg).astype(o_ref.dtype)

    halves = pl.BlockSpec((2, tm, tn), lambda i, j: (0, i, j))
    return pl.pallas_call(
        body, name=name, grid=(S // tm, D_FF // tn),
        in_specs=[pl.BlockSpec((tm, K), lambda i, j: (i, 0)), pl.BlockSpec((tn, K), lambda i, j: (j, 0)), halves,
                  pl.BlockSpec(memory_space=pl.ANY)],
        out_specs=halves, out_shape=jax.ShapeDtypeStruct((2, S, D_FF), BF16),
        compiler_params=_params(("parallel", "parallel")))(dx, w, gu, dep)


def _rms_fwd(x, g, *, name, dep=None):
    R, Dm = x.shape
    tr = _tile(R, 512, 8)

    def body(x_ref, g_ref, *rest):
        o_ref = rest[-1]
        xv = x_ref[...]
        r = lax.rsqrt(jnp.mean(xv * xv, axis=-1, keepdims=True) + EPS)
        o_ref[...] = (xv * r * g_ref[...]).astype(o_ref.dtype)

    in_specs = [pl.BlockSpec((tr, Dm), lambda i: (i, 0)), pl.BlockSpec((1, Dm), lambda i: (0, 0))]
    args = (x, g)
    if dep is not None:
        in_specs.append(pl.BlockSpec(memory_space=pl.ANY))
        args += (dep,)
    return pl.pallas_call(
        body, name=name, grid=(R // tr,), in_specs=in_specs,
        out_specs=pl.BlockSpec((tr, Dm), lambda i: (i, 0)), out_shape=jax.ShapeDtypeStruct((R, Dm), BF16),
        compiler_params=_params(("parallel",)))(*args)


def _rms_bwd(x, g, dh, dres, *, name):
    R, Dm = x.shape
    tr = _tile(R, 256, 8)

    def body(x_ref, g_ref, dh_ref, dres_ref, dx_ref, dg_ref):
        @pl.when(pl.program_id(0) == 0)
        def _():
            dg_ref[...] = jnp.zeros_like(dg_ref)

        xv = x_ref[...]
        r = lax.rsqrt(jnp.mean(xv * xv, axis=-1, keepdims=True) + EPS)
        xh = xv * r
        dhv = dh_ref[...].astype(F32)
        dg_ref[...] += jnp.sum(dhv * xh, axis=0, keepdims=True)
        u = dhv * g_ref[...]
        dx = r * (u - xh * jnp.mean(u * xh, axis=-1, keepdims=True))
        dx_ref[...] = dres_ref[...] + dx

    row = pl.BlockSpec((tr, Dm), lambda i: (i, 0))
    vec = pl.BlockSpec((1, Dm), lambda i: (0, 0))
    return pl.pallas_call(
        body, name=name, grid=(R // tr,), in_specs=[row, vec, row, row], out_specs=[row, vec],
        out_shape=[jax.ShapeDtypeStruct((R, Dm), F32), jax.ShapeDtypeStruct((1, Dm), F32)],
        compiler_params=_params(("arbitrary",)))(x, g, dh, dres)


def _head_mask(h):
    lane = lax.broadcasted_iota(jnp.int32, (1, D_MEM), 1)
    return (lane >= h * MEM_HDIM) & (lane < (h + 1) * MEM_HDIM)


def _attn_probs(qv, k_mx, mask):
    s = lax.dot_general(jnp.where(mask, qv, 0.0).astype(MXU_DTYPE), k_mx, (((1,), (1,)), ((), ())),
                        preferred_element_type=F32) * (MEM_HDIM ** -0.5)
    e = jnp.exp(s - jnp.max(s, axis=-1, keepdims=True))
    return e / jnp.sum(e, axis=-1, keepdims=True)


def _attn_fwd(p, qcol, kv, heads, *, name):
    S = p.shape[0]
    M = kv.shape[0]
    ts = _tile(S, ATTN_ROWS, 8)

    def body(q_ref, k_ref, v_ref, heads_in, o_ref):
        del heads_in
        qv = q_ref[...]
        kx = k_ref[...].astype(MXU_DTYPE)
        vv = v_ref[...]
        out = jnp.zeros((ts, D_MEM), F32)
        for h in range(MEM_HEADS):
            mask = _head_mask(h)
            pr = _attn_probs(qv, kx, mask)
            out = out + jnp.dot(pr.astype(MXU_DTYPE), jnp.where(mask, vv, 0.0).astype(MXU_DTYPE),
                                preferred_element_type=F32)
        o_ref[...] = out.astype(o_ref.dtype)

    return pl.pallas_call(
        body, name=name, grid=(S // ts,),
        in_specs=[pl.BlockSpec((ts, D_MEM), lambda i: (i, qcol)), pl.BlockSpec((M, D_MEM), lambda i: (0, 0)),
                  pl.BlockSpec((M, D_MEM), lambda i: (0, 1)), pl.BlockSpec(memory_space=pl.ANY)],
        out_specs=pl.BlockSpec((ts, D_MEM), lambda i: (i, D_TOK // D_MEM)),
        out_shape=jax.ShapeDtypeStruct(heads.shape, heads.dtype), input_output_aliases={3: 0},
        compiler_params=_params(("parallel",)))(p, kv, kv, heads)


def _attn_bwd(p, qcol, kv, dheads, dp, *, name):
    S = p.shape[0]
    M = kv.shape[0]
    ts = _tile(S, ATTN_ROWS, 8)
    scale = MEM_HDIM ** -0.5

    def body(q_ref, k_ref, v_ref, do_ref, dp_in, dq_ref, dk_ref, dv_ref):
        del dp_in

        @pl.when(pl.program_id(0) == 0)
        def _():
            dk_ref[...] = jnp.zeros_like(dk_ref)
            dv_ref[...] = jnp.zeros_like(dv_ref)

        qv = q_ref[...]
        kv_ = k_ref[...]
        kx = kv_.astype(MXU_DTYPE)
        vv = v_ref[...]
        dox = do_ref[...].astype(MXU_DTYPE)
        qx = qv.astype(MXU_DTYPE)
        dq = jnp.zeros((ts, D_MEM), F32)
        for h in range(MEM_HEADS):
            mask = _head_mask(h)
            pr = _attn_probs(qv, kx, mask)
            vh = jnp.where(mask, vv, 0.0).astype(MXU_DTYPE)
            dpr = lax.dot_general(dox, vh, (((1,), (1,)), ((), ())), preferred_element_type=F32)
            ds = (pr * (dpr - jnp.sum(dpr * pr, axis=-1, keepdims=True)) * scale).astype(MXU_DTYPE)
            dq = dq + jnp.dot(ds, jnp.where(mask, kv_, 0.0).astype(MXU_DTYPE), preferred_element_type=F32)
            dk_h = lax.dot_general(ds, qx, (((0,), (0,)), ((), ())), preferred_element_type=F32)
            dv_h = lax.dot_general(pr.astype(MXU_DTYPE), dox, (((0,), (0,)), ((), ())), preferred_element_type=F32)
            dk_ref[...] += jnp.where(mask, dk_h, 0.0)
            dv_ref[...] += jnp.where(mask, dv_h, 0.0)
        dq_ref[...] = dq.astype(dq_ref.dtype)

    return pl.pallas_call(
        body, name=name, grid=(S // ts,),
        in_specs=[pl.BlockSpec((ts, D_MEM), lambda i: (i, qcol)), pl.BlockSpec((M, D_MEM), lambda i: (0, 0)),
                  pl.BlockSpec((M, D_MEM), lambda i: (0, 1)),
                  pl.BlockSpec((ts, D_MEM), lambda i: (i, D_TOK // D_MEM)), pl.BlockSpec(memory_space=pl.ANY)],
        out_specs=[pl.BlockSpec((ts, D_MEM), lambda i: (i, qcol)), pl.BlockSpec((M, D_MEM), lambda i: (0, 0)),
                   pl.BlockSpec((M, D_MEM), lambda i: (0, 0))],
        out_shape=[jax.ShapeDtypeStruct(dp.shape, dp.dtype), jax.ShapeDtypeStruct((M, D_MEM), F32),
                   jax.ShapeDtypeStruct((M, D_MEM), F32)],
        input_output_aliases={4: 0}, compiler_params=_params(("arbitrary",)))(p, kv, kv, dheads, dp)


def _gm_forward_parts(u_ref, v_ref, lng_ref, lnb_ref, w_ref, bsb_ref, with_grad=False):
    if with_grad:
        (zu, du_gelu), (zv, dv_gelu) = _gelu(u_ref[...], True), _gelu(v_ref[...], True)
    else:
        zu, zv, du_gelu, dv_gelu = _gelu(u_ref[...]), _gelu(v_ref[...]), None, None
    mu = jnp.mean(zv, axis=-1, keepdims=True)
    cen = zv - mu
    rs = lax.rsqrt(jnp.mean(cen * cen, axis=-1, keepdims=True) + EPS)
    vh = cen * rs
    vn = vh * lng_ref[...] + lnb_ref[...]
    row = lax.broadcasted_iota(jnp.int32, (GM_CHUNK, GM_CHUNK), 0)
    col = lax.broadcasted_iota(jnp.int32, (GM_CHUNK, GM_CHUNK), 1)
    tril = row >= col
    wm = [jnp.where(tril, w_ref[g], 0.0).astype(MXU_DTYPE) for g in range(N_HEADS)]
    vnx = [vn[:, g * HEAD:(g + 1) * HEAD].astype(MXU_DTYPE) for g in range(N_HEADS)]
    sv = [jnp.dot(wm[g], vnx[g], preferred_element_type=F32) + bsb_ref[g] for g in range(N_HEADS)]
    return zu, vh, rs, wm, vnx, sv, tril, du_gelu, dv_gelu


def _gmlp_fwd(p, lng, lnb, ws, bsb, *, name):
    S = p.shape[0]

    def body(u_ref, v_ref, lng_ref, lnb_ref, w_ref, bsb_ref, o_ref):
        zu, _, _, _, _, sv, _, _, _ = _gm_forward_parts(u_ref, v_ref, lng_ref, lnb_ref, w_ref, bsb_ref)
        for g in range(N_HEADS):
            o_ref[:, g * HEAD:(g + 1) * HEAD] = (zu[:, g * HEAD:(g + 1) * HEAD] * sv[g]).astype(o_ref.dtype)

    blk = lambda c: pl.BlockSpec((GM_CHUNK, D_TOK), lambda i: (i, c))
    vec = pl.BlockSpec((1, D_TOK), lambda i: (0, 0))
    cube = pl.BlockSpec((N_HEADS, GM_CHUNK, GM_CHUNK), lambda i: (0, 0, 0))
    return pl.pallas_call(
        body, name=name, grid=(S // GM_CHUNK,), in_specs=[blk(0), blk(1), vec, vec, cube, cube],
        out_specs=blk(0), out_shape=jax.ShapeDtypeStruct((S, D_MODEL), BF16),
        compiler_params=_params(("parallel",)))(p, p, lng, lnb, ws, bsb)


def _gmlp_bwd(p, lng, lnb, ws, bsb, dheads, *, name):
    S = p.shape[0]

    def body(u_ref, v_ref, lng_ref, lnb_ref, w_ref, bsb_ref, dt_ref, dp_ref, dw_ref, dbs_ref, dlg_ref, dlb_ref):
        @pl.when(pl.program_id(0) == 0)
        def _():
            dw_ref[...] = jnp.zeros_like(dw_ref)
            dbs_ref[...] = jnp.zeros_like(dbs_ref)
            dlg_ref[...] = jnp.zeros_like(dlg_ref)
            dlb_ref[...] = jnp.zeros_like(dlb_ref)

        zu, vh, rs, wm, vnx, sv, tril, du_gelu, dv_gelu = _gm_forward_parts(
            u_ref, v_ref, lng_ref, lnb_ref, w_ref, bsb_ref, with_grad=True)
        dt = dt_ref[...].astype(F32)
        dvn_parts = []
        for g in range(N_HEADS):
            sl = slice(g * HEAD, (g + 1) * HEAD)
            dsv = dt[:, sl] * zu[:, sl]
            dp_ref[:, sl] = (dt[:, sl] * sv[g] * du_gelu[:, sl]).astype(dp_ref.dtype)
            dsx = dsv.astype(MXU_DTYPE)
            dw = lax.dot_general(dsx, vnx[g], (((1,), (1,)), ((), ())), preferred_element_type=F32)
            dw_ref[g] += jnp.where(tril, dw, 0.0)
            dbs_ref[g] += jnp.sum(dsv, axis=-1, keepdims=True)
            dvn_parts.append(lax.dot_general(wm[g], dsx, (((0,), (0,)), ((), ())), preferred_element_type=F32))
        dvn = jnp.concatenate(dvn_parts, axis=-1)
        dlg_ref[...] += jnp.sum(dvn * vh, axis=0, keepdims=True)
        dlb_ref[...] += jnp.sum(dvn, axis=0, keepdims=True)
        dvh = dvn * lng_ref[...]
        dzv = rs * (dvh - jnp.mean(dvh, axis=-1, keepdims=True) - vh * jnp.mean(dvh * vh, axis=-1, keepdims=True))
        dp_ref[:, D_TOK:] = (dzv * dv_gelu).astype(dp_ref.dtype)

    blk = lambda c: pl.BlockSpec((GM_CHUNK, D_TOK), lambda i: (i, c))
    vec = pl.BlockSpec((1, D_TOK), lambda i: (0, 0))
    cube = pl.BlockSpec((N_HEADS, GM_CHUNK, GM_CHUNK), lambda i: (0, 0, 0))
    col = pl.BlockSpec((N_HEADS, GM_CHUNK, 1), lambda i: (0, 0, 0))
    return pl.pallas_call(
        body, name=name, grid=(S // GM_CHUNK,), in_specs=[blk(0), blk(1), vec, vec, cube, cube, blk(0)],
        out_specs=[pl.BlockSpec((GM_CHUNK, 2 * D_TOK), lambda i: (i, 0)), cube, col, vec, vec],
        out_shape=[jax.ShapeDtypeStruct((S, GM_IN), BF16), jax.ShapeDtypeStruct((N_HEADS, GM_CHUNK, GM_CHUNK), F32),
                   jax.ShapeDtypeStruct((N_HEADS, GM_CHUNK, 1), F32), jax.ShapeDtypeStruct((1, D_TOK), F32),
                   jax.ShapeDtypeStruct((1, D_TOK), F32)],
        compiler_params=_params(("arbitrary",)))(p, p, lng, lnb, ws, bsb, dheads)


def _chunk_tri(n, chunk, upper):
    r = lax.broadcasted_iota(jnp.int32, (n, n), 0)
    c = lax.broadcasted_iota(jnp.int32, (n, n), 1)
    same = (r // chunk) == (c // chunk)
    return jnp.where(same & ((r <= c) if upper else (r >= c)), 1.0, 0.0).astype(F32)


def _running_sum(tri, x):
    hi = x.astype(BF16)
    rest = x - hi.astype(F32)
    mid = rest.astype(BF16)
    lo = (rest - mid.astype(F32)).astype(BF16)
    tri = tri.astype(BF16)
    return (jnp.dot(tri, hi, preferred_element_type=F32) + jnp.dot(tri, mid, preferred_element_type=F32)
            + jnp.dot(tri, lo, preferred_element_type=F32))


MASKED = -1e30


def _pair_masks(mask_ref, n, upper):
    row = lax.broadcasted_iota(jnp.int32, (n, HEAD), 0)
    for i in range(n):
        mask_ref[i] = jnp.where((row <= i) if upper else (row >= i), 0.0, MASKED).astype(F32)


def _hg_gates(fz, lb):
    sg = _sigmoid(fz)
    f = lb + (1.0 - lb) * sg
    kk = (1.0 - lb) * (1.0 - sg)
    return sg, f, jnp.log(f), kk


def _hgrn2_fwd(p, lb, onorm, *, name):
    S = p.shape[0]
    C = HG_SUB
    tb = _tile(S, 256, C)
    nsub = tb // C

    def body(q_ref, fz_ref, v_ref, g_ref, lb_ref, on_ref, tok_ref, o_ref, st_ref, state, b_blk, k_blk, bsc, ksc, vsc):
        @pl.when(pl.program_id(0) == 0)
        def _():
            state[...] = jnp.zeros_like(state)

        _, _, lg, kk = _hg_gates(fz_ref[...], lb_ref[...])
        b_blk[...] = _running_sum(_chunk_tri(tb, C, False), lg)
        k_blk[...] = kk
        tt = lax.broadcasted_iota(jnp.int32, (C, HEAD), 0)

        def sub(c, carry):
            rows = pl.ds(pl.multiple_of(c * C, C), C)
            for h in range(N_HEADS):
                cols = slice(h * HEAD, (h + 1) * HEAD)
                qv = q_ref[rows, cols]
                vv = v_ref[rows, cols]
                b = b_blk[rows, cols]
                kk = k_blk[rows, cols]
                st0 = state[h]
                st0x = st0.astype(MXU_DTYPE)
                st_ref[c, h] = st0x.astype(st_ref.dtype)
                inter = lax.dot_general((qv * jnp.exp(b)).astype(MXU_DTYPE), st0x,
                                        (((1,), (1,)), ((), ())), preferred_element_type=F32)
                bsc[h] = b
                ksc[h] = kk
                vsc[h] = vv
                intra = jnp.zeros((C, HEAD), F32)
                for s in range(C):
                    dec = jnp.where(tt >= s, jnp.exp(b - bsc[h, pl.ds(s, 1), :]), 0.0)
                    a_s = jnp.sum(qv * ksc[h, pl.ds(s, 1), :] * dec, axis=-1, keepdims=True)
                    intra = intra + a_s * vsc[h, pl.ds(s, 1), :]
                o_ref[rows, cols] = inter + intra
                b_last = bsc[h, pl.ds(C - 1, 1), :]
                ke = kk * jnp.exp(b_last - b)
                state[h] = st0 * jnp.exp(b_last) + lax.dot_general(
                    vv.astype(MXU_DTYPE), ke.astype(MXU_DTYPE), (((0,), (0,)), ((), ())),
                    preferred_element_type=F32)
            return carry

        lax.fori_loop(0, nsub, sub, 0, unroll=2)

        for h in range(N_HEADS):
            cols = slice(h * HEAD, (h + 1) * HEAD)
            o = o_ref[:, cols]
            gv = g_ref[:, cols]
            n = o * lax.rsqrt(jnp.mean(o * o, axis=-1, keepdims=True) + EPS)
            tok_ref[:, cols] = (n * (gv * _sigmoid(gv)) * on_ref[:, cols]).astype(tok_ref.dtype)

    blk = lambda c: pl.BlockSpec((tb, D_TOK), lambda i, c=c: (i, c))
    vec = pl.BlockSpec((1, D_TOK), lambda i: (0, 0))
    stb = pl.BlockSpec((nsub, N_HEADS, HEAD, HEAD), lambda i: (i, 0, 0, 0))
    return pl.pallas_call(
        body, name=name, grid=(S // tb,), in_specs=[blk(0), blk(1), blk(2), blk(3), vec, vec],
        out_specs=[blk(0), blk(0), stb],
        out_shape=[jax.ShapeDtypeStruct((S, D_MODEL), BF16), jax.ShapeDtypeStruct((S, D_TOK), F32),
                   jax.ShapeDtypeStruct((S // C, N_HEADS, HEAD, HEAD), BF16)],
        scratch_shapes=[pltpu.VMEM((N_HEADS, HEAD, HEAD), F32)] + [pltpu.VMEM((tb, D_TOK), F32)] * 2
        + [pltpu.VMEM((N_HEADS, C, HEAD), F32)] * 3,
        compiler_params=_params(("arbitrary",)))(p, p, p, p, lb, onorm)


def _hgrn2_bwd(p, lb, onorm, o, states, dheads, *, name):
    S = p.shape[0]
    C = HG_SUB
    tb = _tile(S, 256, C)
    nsub = tb // C
    nblk = S // tb

    def body(q_ref, fz_ref, v_ref, g_ref, lb_ref, on_ref, o_ref, st_ref, dt_ref, dp_ref, dlb_ref, don_ref, dstate,
             b_blk, k_blk, do_blk, db_blk, dk_blk, dq_blk, dv_blk, bsc, ksc, vsc, qsc, dosc, causal, anti):
        @pl.when(pl.program_id(0) == 0)
        def _():
            dstate[...] = jnp.zeros_like(dstate)
            dlb_ref[...] = jnp.zeros_like(dlb_ref)
            don_ref[...] = jnp.zeros_like(don_ref)

        for h in range(N_HEADS):
            cols = slice(h * HEAD, (h + 1) * HEAD)
            onv = on_ref[:, cols]
            gv = g_ref[:, cols]
            ov = o_ref[:, cols]
            dt = dt_ref[:, cols].astype(F32)
            sgg = _sigmoid(gv)
            sil = gv * sgg
            rinv = lax.rsqrt(jnp.mean(ov * ov, axis=-1, keepdims=True) + EPS)
            n = ov * rinv
            don_ref[:, cols] += jnp.sum(dt * n * sil, axis=0, keepdims=True)
            dn = dt * sil * onv
            dp_ref[:, 3 * D_TOK + h * HEAD:3 * D_TOK + (h + 1) * HEAD] = (
                dt * n * onv * sgg * (1.0 + gv * (1.0 - sgg))).astype(dp_ref.dtype)
            do_blk[:, cols] = rinv * (dn - n * jnp.mean(dn * n, axis=-1, keepdims=True))
        _, _, lg, kk = _hg_gates(fz_ref[...], lb_ref[...])
        b_blk[...] = _running_sum(_chunk_tri(tb, C, False), lg)
        k_blk[...] = kk
        _pair_masks(causal, C, False)
        _pair_masks(anti, C, True)
        tt = lax.broadcasted_iota(jnp.int32, (C, HEAD), 0)

        def sub(j, heads):
            c = nsub - 1 - j
            rows = pl.ds(pl.multiple_of(c * C, C), C)
            for h in heads:
                cols = slice(h * HEAD, (h + 1) * HEAD)
                qv = q_ref[rows, cols]
                vv = v_ref[rows, cols]
                do = do_blk[rows, cols]
                b = b_blk[rows, cols]
                kk = k_blk[rows, cols]
                bsc[h] = b
                ksc[h] = kk
                vsc[h] = vv
                qsc[h] = qv
                dosc[h] = do
                b_last = bsc[h, pl.ds(C - 1, 1), :]
                eb = jnp.exp(b)
                qe = qv * eb
                ebb = jnp.exp(b_last - b)
                ke = kk * ebb
                e_last = jnp.exp(b_last)
                st0x = st_ref[c, h].astype(MXU_DTYPE)
                st0 = st0x.astype(F32)
                dst1 = dstate[h]
                dst1x = dst1.astype(MXU_DTYPE)
                dox = do.astype(MXU_DTYPE)
                dqe = jnp.dot(dox, st0x, preferred_element_type=F32)
                dke = jnp.dot(vv.astype(MXU_DTYPE), dst1x, preferred_element_type=F32)
                dv = lax.dot_general(ke.astype(MXU_DTYPE), dst1x, (((1,), (1,)), ((), ())),
                                     preferred_element_type=F32)
                db_last = (e_last * jnp.sum(st0 * dst1, axis=0, keepdims=True)
                           + jnp.sum(dke * ke, axis=0, keepdims=True))
                dstate[h] = dst1 * e_last + lax.dot_general(dox, qe.astype(MXU_DTYPE), (((0,), (0,)), ((), ())),
                                                            preferred_element_type=F32)
                dq_pairs = jnp.zeros((C, HEAD), F32)
                for s in range(C):
                    dec = jnp.exp(b - bsc[h, pl.ds(s, 1), :] + causal[s])
                    da_s = jnp.sum(do * vsc[h, pl.ds(s, 1), :], axis=-1, keepdims=True)
                    dq_pairs = dq_pairs + da_s * (ksc[h, pl.ds(s, 1), :] * dec)
                dk_pairs = jnp.zeros((C, HEAD), F32)
                for t in range(C):
                    do_t = dosc[h, pl.ds(t, 1), :]
                    qd = qsc[h, pl.ds(t, 1), :] * jnp.exp(bsc[h, pl.ds(t, 1), :] - b + anti[t])
                    da_t = jnp.sum(vv * do_t, axis=-1, keepdims=True)
                    dk_pairs = dk_pairs + da_t * qd
                    a_t = jnp.sum(qd * kk, axis=-1, keepdims=True)
                    dv = dv + a_t * do_t
                db = dqe * qe - dke * ke + qv * dq_pairs - kk * dk_pairs
                db_blk[rows, cols] = db + jnp.where(tt == C - 1, db_last, 0.0)
                dk_blk[rows, cols] = dke * ebb + dk_pairs
                dq_blk[rows, cols] = dqe * eb + dq_pairs
                dv_blk[rows, cols] = dv

        for first in range(0, N_HEADS, HG_GROUP):
            heads = tuple(range(first, first + HG_GROUP))
            pl.loop(0, nsub)(lambda j, heads=heads: sub(j, heads))

        dlg = _running_sum(_chunk_tri(tb, C, True), db_blk[...])
        lbv = lb_ref[...]
        sg, f, _, _ = _hg_gates(fz_ref[...], lbv)
        w = dlg / f - dk_blk[...]
        dp_ref[:, 0:D_TOK] = dq_blk[...].astype(dp_ref.dtype)
        dp_ref[:, 2 * D_TOK:3 * D_TOK] = dv_blk[...].astype(dp_ref.dtype)
        dp_ref[:, D_TOK:2 * D_TOK] = (w * (1.0 - lbv) * sg * (1.0 - sg)).astype(dp_ref.dtype)
        dlb_ref[...] += jnp.sum(w * (1.0 - sg), axis=0, keepdims=True)

    blk = lambda c: pl.BlockSpec((tb, D_TOK), lambda i, c=c: (nblk - 1 - i, c))
    vec = pl.BlockSpec((1, D_TOK), lambda i: (0, 0))
    stb = pl.BlockSpec((nsub, N_HEADS, HEAD, HEAD), lambda i: (nblk - 1 - i, 0, 0, 0))
    small = jax.ShapeDtypeStruct((1, D_TOK), F32)
    return pl.pallas_call(
        body, name=name, grid=(nblk,), in_specs=[blk(0), blk(1), blk(2), blk(3), vec, vec, blk(0), stb, blk(0)],
        out_specs=[pl.BlockSpec((tb, 4 * D_TOK), lambda i: (nblk - 1 - i, 0)), vec, vec],
        out_shape=[jax.ShapeDtypeStruct((S, HG_IN), BF16), small, small],
        scratch_shapes=[pltpu.VMEM((N_HEADS, HEAD, HEAD), F32)] + [pltpu.VMEM((tb, D_TOK), F32)] * 7
        + [pltpu.VMEM((N_HEADS, C, HEAD), F32)] * 5 + [pltpu.VMEM((C, C, HEAD), F32)] * 2,
        compiler_params=_params(("arbitrary",)))(p, p, p, p, lb, onorm, o, states, dheads)


def _adamw(w, g, m, v, *, name):
    shape = w.shape
    cols = shape[-1]
    w2, g2, m2, v2 = (t.reshape(-1, cols) for t in (w, g, m, v))
    R = w2.shape[0]
    tr = _tile(R, 512, 8)

    def body(w_ref, g_ref, m_ref, v_ref, d_ref, nm_ref, nv_ref):
        gv = g_ref[...]
        nm = ADAM_B1 * m_ref[...] + (1.0 - ADAM_B1) * gv
        nv = ADAM_B2 * v_ref[...] + (1.0 - ADAM_B2) * (gv * gv)
        m_hat = nm / (1.0 - ADAM_B1 ** ADAM_STEP)
        v_hat = nv / (1.0 - ADAM_B2 ** ADAM_STEP)
        d_ref[...] = -ADAM_LR * (m_hat / (jnp.sqrt(v_hat) + ADAM_EPS) + ADAM_WD * w_ref[...])
        nm_ref[...] = nm
        nv_ref[...] = nv

    spec = pl.BlockSpec((tr, cols), lambda i: (i, 0))
    out = jax.ShapeDtypeStruct((R, cols), F32)
    d, nm, nv = pl.pallas_call(body, name=name, grid=(R // tr,), in_specs=[spec] * 4, out_specs=[spec] * 3,
                               out_shape=[out] * 3, compiler_params=_params(("parallel",)))(w2, g2, m2, v2)
    return d.reshape(shape), nm.reshape(shape), nv.reshape(shape)


def _add_received(sent, got, me, *, name):
    n, R, Cc = got.shape
    tr = _tile(R, 256, 16)
    per = R // tr

    def body(me_ref, a_ref, b_ref, o_ref):
        del me_ref
        acc = a_ref[...].astype(F32)
        for k in range(n):
            acc = acc + b_ref[k].astype(F32)
        o_ref[...] = acc

    grid_spec = pltpu.PrefetchScalarGridSpec(
        num_scalar_prefetch=1, grid=(per,),
        in_specs=[pl.BlockSpec((tr, Cc), lambda i, me_ref: (me_ref[0] * per + i, 0)),
                  pl.BlockSpec((n, tr, Cc), lambda i, me_ref: (0, i, 0))],
        out_specs=pl.BlockSpec((tr, Cc), lambda i, me_ref: (i, 0)))
    return pl.pallas_call(body, name=name, grid_spec=grid_spec, out_shape=jax.ShapeDtypeStruct((R, Cc), F32),
                          compiler_params=_params(("parallel",)))(jnp.reshape(me, (1,)).astype(jnp.int32), sent, got)


def _sum_blocks(x, *, name):
    n, R, Cc = x.shape
    tr = _tile(R, 208, 8)

    def body(x_ref, o_ref):
        acc = x_ref[0]
        for k in range(1, n):
            acc = acc + x_ref[k]
        o_ref[...] = acc

    return pl.pallas_call(
        body, name=name, grid=(R // tr,), in_specs=[pl.BlockSpec((n, tr, Cc), lambda i: (0, i, 0))],
        out_specs=pl.BlockSpec((tr, Cc), lambda i: (i, 0)), out_shape=jax.ShapeDtypeStruct((R, Cc), F32),
        compiler_params=_params(("parallel",)))(x)


def _place():
    return lax.axis_index("x"), lax.axis_index("y"), lax.axis_index("c")


def _all_gather(x, *, name, in_vmem, reduce_sum=False, with_token=False):
    R, Cc = x.shape
    space = pltpu.VMEM if in_vmem else pl.ANY

    def body(x_ref, out_ref, *scratch):
        if with_token:
            scratch[0][...] = jnp.zeros_like(scratch[0])
            scratch = scratch[1:]
        if reduce_sum:
            gat_ref, send_sems, recv_sems, local_sem = scratch
        else:
            gat_ref = out_ref
            send_sems, recv_sems, local_sem = scratch
        mx, my, mc = _place()
        me, sibling = (mx, my, mc), (mx, my, 1 - mc)
        chips = [(1 - mx, my), (mx, 1 - my), (1 - mx, 1 - my)]

        def rows(px, py, pc):
            return gat_ref.at[pl.ds((4 * px + 2 * py + pc) * R, R), :]

        def copy(k, block, to, src=None):
            return pltpu.make_async_remote_copy(
                src_ref=rows(*block) if src is None else src, dst_ref=rows(*block), send_sem=send_sems.at[k],
                recv_sem=recv_sems.at[k], device_id=to, device_id_type=MESH_ID)

        mine = pltpu.make_async_copy(x_ref, rows(*me), local_sem)
        mine.start()
        first = [copy(0, me, sibling, src=x_ref)]
        first += [copy(1 + j, me, (*chip, mc), src=x_ref) for j, chip in enumerate(chips)]
        for cp in first:
            cp.start()
        passed = [copy(4 + j, (*chip, mc), sibling) for j, chip in enumerate(chips)]
        for j, chip in enumerate(chips):
            copy(1 + j, (*chip, mc), me).wait_recv()
            passed[j].start()
        copy(0, sibling, me).wait_recv()
        for j, chip in enumerate(chips):
            copy(4 + j, (*chip, 1 - mc), me).wait_recv()
        for cp in first + passed:
            cp.wait_send()
        mine.wait()
        if reduce_sum:
            acc = gat_ref[pl.ds(0, R), :]
            for d in range(1, N_DEV):
                acc = acc + gat_ref[pl.ds(d * R, R), :]
            out_ref[...] = acc

    sems = [pltpu.SemaphoreType.DMA((7,)), pltpu.SemaphoreType.DMA((7,)), pltpu.SemaphoreType.DMA]
    if reduce_sum:
        assert in_vmem
        out_shape = jax.ShapeDtypeStruct((R, Cc), x.dtype)
        scratch = [pltpu.VMEM((N_DEV * R, Cc), x.dtype)] + sems
    else:
        out_shape = jax.ShapeDtypeStruct((N_DEV * R, Cc), x.dtype)
        scratch = sems
    out_specs = pl.BlockSpec(memory_space=space)
    if with_token:
        out_shape = (out_shape, jax.ShapeDtypeStruct((8, LANE), F32))
        out_specs = (out_specs, pl.BlockSpec(memory_space=pltpu.VMEM))
    return pl.pallas_call(
        body, name=name, out_shape=out_shape, in_specs=[pl.BlockSpec(memory_space=space)], out_specs=out_specs,
        scratch_shapes=scratch, compiler_params=pltpu.CompilerParams(vmem_limit_bytes=VMEM_LIMIT))(x)


def _peer(k, mx, my, mc):
    bits = k + 1
    return (1 - mx if bits & 4 else mx, 1 - my if bits & 2 else my, 1 - mc if bits & 1 else mc)


HBM_SPEC = pl.BlockSpec(memory_space=pltpu.HBM)
SEM_SPEC = pl.BlockSpec(memory_space=pltpu.SEMAPHORE)
DATAFLOW = pltpu.SideEffectType.DATAFLOW_SIDE_EFFECTING


def _exchange_copies(x_refs, land_refs, send_sems, recv_sems, scatter):
    mx, my, mc = _place()
    me = 4 * mx + 2 * my + mc
    n = len(x_refs)
    copies = []
    for k in range(N_DEV - 1):
        px, py, pc = _peer(k, mx, my, mc)
        for m, (x_ref, land_ref) in enumerate(zip(x_refs, land_refs)):
            rows = land_ref.shape[1] if scatter else x_ref.shape[0]
            if scatter:
                src = x_ref.at[pl.ds(pl.multiple_of((4 * px + 2 * py + pc) * rows, 16), rows), :]
                dst = land_ref.at[k]
            else:
                src = x_ref
                dst = land_ref.at[pl.ds(pl.multiple_of(me * rows, 16), rows), :]
            copies.append(pltpu.make_async_remote_copy(
                src_ref=src, dst_ref=dst, send_sem=send_sems.at[k * n + m], recv_sem=recv_sems.at[k * n + m],
                device_id=(px, py, pc), device_id_type=MESH_ID))
    return copies


def _land_shape(x, scatter):
    return (N_DEV - 1, x.shape[0] // N_DEV, x.shape[1]) if scatter else (N_DEV * x.shape[0], x.shape[1])


def _own_copies(x_refs, land_refs, local_sems):
    mx, my, mc = _place()
    me = 4 * mx + 2 * my + mc
    return [pltpu.make_async_copy(
        x_ref, land_ref.at[pl.ds(pl.multiple_of(me * x_ref.shape[0], 16), x_ref.shape[0]), :], local_sems.at[m])
        for m, (x_ref, land_ref) in enumerate(zip(x_refs, land_refs))]


def _exchange_start(groups, *, name, scatter):
    sizes = [len(g) for g in groups]
    xs = [x for g in groups for x in g]
    n = len(xs)
    lands = [lax.empty(_land_shape(x, scatter), x.dtype) for x in xs]
    per = 2 if scatter else 3

    def body(*refs):
        sems = refs[2 * n:2 * n + per * len(groups)]
        token = refs[-1]
        off = 0
        for gi, m in enumerate(sizes):
            x_refs, land_refs = refs[off:off + m], refs[n + off:n + off + m]
            for cp in _exchange_copies(x_refs, land_refs, sems[per * gi], sems[per * gi + 1], scatter):
                cp.start()
            if not scatter:
                for cp in _own_copies(x_refs, land_refs, sems[per * gi + 2]):
                    cp.start()
            off += m
        token[...] = jnp.zeros_like(token)

    sem_shapes = []
    for m in sizes:
        sem_shapes += [pltpu.SemaphoreType.DMA(((N_DEV - 1) * m,))] * 2
        if not scatter:
            sem_shapes.append(pltpu.SemaphoreType.DMA((m,)))
    ns = len(sem_shapes)
    out = pl.pallas_call(
        body, name=name,
        out_shape=(*sem_shapes, *[pltpu.HBM(x.shape, x.dtype) for x in xs],
                   *[pltpu.HBM(l.shape, l.dtype) for l in lands], jax.ShapeDtypeStruct((8, LANE), F32)),
        in_specs=(HBM_SPEC,) * (2 * n),
        out_specs=(SEM_SPEC,) * ns + (HBM_SPEC,) * (2 * n) + (pl.BlockSpec(memory_space=pltpu.VMEM),),
        input_output_aliases={i: ns + i for i in range(2 * n)},
        compiler_params=pltpu.CompilerParams(has_side_effects=DATAFLOW))(
            *[pltpu.with_memory_space_constraint(t, pltpu.HBM) for t in xs + lands])
    started, off = [], 0
    for gi, m in enumerate(sizes):
        sems = out[per * gi:per * gi + per]
        started.append((sems[0], sems[1], list(out[ns + off:ns + off + m]),
                        list(out[ns + n + off:ns + n + off + m]), out[-1], None if scatter else sems[2]))
        off += m
    return started


def _exchange_wait(started, after, *, name, scatter):
    send_sems, recv_sems, xs, lands, _, local_sems = started
    n = len(xs)

    def body(*refs):
        x_refs, land_refs = refs[:n], refs[n:2 * n]
        for cp in _exchange_copies(x_refs, land_refs, refs[2 * n], refs[2 * n + 1], scatter):
            cp.wait_send()
            cp.wait_recv()
        if not scatter:
            for cp in _own_copies(x_refs, land_refs, refs[2 * n + 2]):
                cp.wait()

    sems = (send_sems, recv_sems) if scatter else (send_sems, recv_sems, local_sems)
    out = pl.pallas_call(
        body, name=name, out_shape=tuple(pltpu.HBM(t.shape, t.dtype) for t in xs + lands),
        in_specs=(HBM_SPEC,) * (2 * n) + (SEM_SPEC,) * len(sems) + (pl.BlockSpec(memory_space=pl.ANY),),
        out_specs=(HBM_SPEC,) * (2 * n), input_output_aliases={i: i for i in range(2 * n)},
        compiler_params=pltpu.CompilerParams(has_side_effects=DATAFLOW))(*xs, *lands, *sems, after)
    return list(out[:n]), list(out[n:])


def _gather_start(groups, token, *, name):
    first = groups[0]
    groups = [[first[0] + token[0, 0].astype(first[0].dtype)] + list(first[1:])] + [list(g) for g in groups[1:]]
    return _exchange_start(groups, name=name, scatter=False)


def _gather_finish(started, after, *, name):
    return _exchange_wait(started, after, name=name, scatter=False)[1]


def _reduce_start(grads, *, name):
    return _exchange_start([grads], name=name, scatter=True)[0]


def _reduce_finish(started, after, me, *, name):
    sent, gots = _exchange_wait(started, after, name=name + "_wait", scatter=True)
    return [_add_received(g, got, me, name=f"{name}_add{m}") for m, (g, got) in enumerate(zip(sent, gots))]


def _pad_rows(a, mult):
    r = (-a.shape[0]) % mult
    return a if r == 0 else jnp.concatenate([a, jnp.zeros((r,) + a.shape[1:], a.dtype)], axis=0)


def kernel(x, mem, mix_norm, mem_norm, w_mem_kv, w_out, hg_w_in, hg_lb, hg_onorm, gm_w_in, gm_ln_g, gm_ln_b, gm_ws, gm_bs, ffn_norm, w_ffn_in, w_ffn_out, final_norm, loss_target, m_mix_norm, m_mem_norm, m_w_mem_kv, m_w_out, m_hg_w_in, m_hg_lb, m_hg_onorm, m_gm_w_in, m_gm_ln_g, m_gm_ln_b, m_gm_ws, m_gm_bs, m_ffn_norm, m_w_ffn_in, m_w_ffn_out, m_final_norm, v_mix_norm, v_mem_norm, v_w_mem_kv, v_w_out, v_hg_w_in, v_hg_lb, v_hg_onorm, v_gm_w_in, v_gm_ln_g, v_gm_ln_b, v_gm_ws, v_gm_bs, v_ffn_norm, v_w_ffn_in, v_w_ffn_out, v_final_norm):
    mx, my, mc = _place()
    me = 4 * mx + 2 * my + mc
    xs = x[0]
    mems = mem[0]
    tgt = loss_target[0]

    hg_t = hg_w_in[0].T.astype(BF16)
    gm_t = gm_w_in[0].T.astype(BF16)
    fi_t = [w_ffn_in[i].T.astype(BF16) for i in range(2)]
    kv_b = [w_mem_kv[i].astype(BF16) for i in range(2)]
    out_b = [w_out[i].astype(BF16) for i in range(2)]
    fo_b = [w_ffn_out[i].astype(BF16) for i in range(2)]
    W_hgT, token = _all_gather(hg_t, name="gather_first", in_vmem=False, with_token=True)
    ln_local = _pad_rows(jnp.concatenate([gm_ln_g, gm_ln_b], axis=0), 16)
    ln_local = jnp.concatenate([ln_local, jnp.zeros((16, LANE - ln_local.shape[1]), F32)], axis=1)
    gather_mix, fi0, fo0, gather_gm, fi1, fo1 = _gather_start(
        [kv_b + out_b, [fi_t[0]], [fo_b[0]], [gm_t, ln_local], [fi_t[1]], [fo_b[1]]], token,
        name="gather_rest_start")
    gather_fi, gather_fo = [fi0, fi1], [fo0, fo1]

    lb_soft = jax.nn.softmax(hg_lb, axis=0)
    lb0 = lb_soft[0:1]
    bsb = jnp.broadcast_to(gm_bs[0][:, :, None], (N_HEADS, GM_CHUNK, GM_CHUNK))
    ws = gm_ws[0]

    W_fiT, W_fo = [], []

    def ffn_fwd(xin, hf, i, **tail):
        W_fiT.extend(_gather_finish(gather_fi[i], hf, name=f"gather_fi{i}_wait"))
        gu, act = _ffn_in(hf, W_fiT[i], name=f"ffn_in{i}")
        W_fo.extend(_gather_finish(gather_fo[i], act, name=f"gather_fo{i}_wait"))
        return gu, act, _matmul(act, W_fo[i], res=xin, name=f"ffn_out{i}", **tail)

    h0 = _rms_fwd(xs, mix_norm[0:1], name="mix_norm0", dep=gather_mix[4])
    p0 = _matmul(h0, W_hgT, tb=True, name="hg_in")
    heads0, o0, states = _hgrn2_fwd(p0, lb0, hg_onorm, name="hgrn2_fwd")

    kv0, kv1, wo0, wo1 = _gather_finish(gather_mix, o0, name="gather_mix_wait")
    W_kv, W_out = [kv0, kv1], [wo0, wo1]
    mem_n, kv = [], []
    for i in range(2):
        mn = _rms_fwd(mems, mem_norm[i:i + 1], name=f"mem_norm{i}")
        mem_n.append(mn)
        kv.append(_matmul(mn, W_kv[i], name=f"mem_kv{i}"))

    heads0 = _attn_fwd(p0, 4 * D_TOK // D_MEM, kv[0], heads0, name="attn_fwd0")
    x1, hf0 = _matmul(heads0, W_out[0], res=xs, norm_gain=ffn_norm[0:1], name="out_proj0")
    gu0, act0, (x2, h1) = ffn_fwd(x1, hf0, 0, norm_gain=mix_norm[1:2])

    W_gmT, ln_all = _gather_finish(gather_gm, h1, name="gather_gm_wait")
    ln_all = ln_all.reshape(N_DEV, 16, LANE)
    ln_g = ln_all[:, 0, :D_TOK // N_DEV].reshape(1, D_TOK)
    ln_b = ln_all[:, 1, :D_TOK // N_DEV].reshape(1, D_TOK)
    p1 = _matmul(h1, W_gmT, tb=True, name="gm_in")
    heads1 = _gmlp_fwd(p1, ln_g, ln_b, ws, bsb, name="gmlp_fwd")
    heads1 = _attn_fwd(p1, 2 * D_TOK // D_MEM, kv[1], heads1, name="attn_fwd1")
    x3, hf1 = _matmul(heads1, W_out[1], res=x2, norm_gain=ffn_norm[1:2], name="out_proj1")
    gu1, act1, (dx, g_final, loss_part) = ffn_fwd(x3, hf1, 1, loss_head=(final_norm.reshape(1, D_MODEL), tgt))

    def ffn_bwd(dx, xin, hf, gu, act, i, dep):
        dgu = _ffn_out_dx(dx, W_fo[i], gu, dep, name=f"ffn_out_dx{i}")
        g_wfo = _matmul(act, dx, ta=True, out_dtype=BF16, name=f"ffn_out_dw{i}")
        g_wfi_t = _matmul(dgu, hf, ta=True, a_halves=True, out_dtype=BF16, name=f"ffn_in_dw{i}")
        dx, g_norm = _matmul(dgu, W_fiT[i], a_halves=True, res=dx, norm_bwd=(xin, ffn_norm[i:i + 1]),
                             name=f"ffn_in_dx{i}")
        return dx, g_wfi_t, g_wfo, g_norm

    def mem_bwd(dkv, i):
        g_wkv = _matmul(mem_n[i], dkv, ta=True, out_dtype=BF16, name=f"mem_kv_dw{i}")
        dmn = _matmul(dkv, W_kv[i], tb=True, name=f"mem_kv_dx{i}")
        _, g_norm = _rms_bwd(mems, mem_norm[i:i + 1], dmn, jnp.zeros_like(mems), name=f"mem_norm_bwd{i}")
        return g_wkv, g_norm

    dx, g_wfi1_t, g_wfo1, g_ffn1 = ffn_bwd(dx, x3, hf1, gu1, act1, 1, loss_part)
    dheads = _matmul(dx, W_out[1], tb=True, name="out_proj_dx1")
    g_wout1 = _matmul(heads1, dx, ta=True, out_dtype=BF16, name="out_proj_dw1")
    dp, g_ws, g_bs, g_lng, g_lnb = _gmlp_bwd(p1, ln_g, ln_b, ws, bsb, dheads, name="gmlp_bwd")
    dp, dk, dv = _attn_bwd(p1, 2 * D_TOK // D_MEM, kv[1], dheads, dp, name="attn_bwd1")
    g_wkv1, g_mem1 = mem_bwd(jnp.concatenate([dk, dv], axis=1), 1)
    g_wgm_t = _matmul(dp, h1, ta=True, out_dtype=BF16, name="gm_in_dw")
    dx, g_mix1 = _matmul(dp, W_gmT, res=dx, norm_bwd=(x2, mix_norm[1:2]), name="gm_in_dx")
    reduce_l1 = _reduce_start([g_wkv1, g_wout1, g_wgm_t, g_wfi1_t, g_wfo1], name="reduce_l1_start")
    early = [loss_part, g_final, g_ffn1, g_mix1, g_mem1, g_ws.reshape(1, -1), g_bs.reshape(1, -1), g_lng, g_lnb]
    early_rows = _pad_rows(jnp.concatenate(early, axis=1).reshape(-1, LANE), 16) + reduce_l1[4][0, 0]
    small_early = _exchange_start([[early_rows]], name="reduce_small_start", scatter=False)[0]

    dx, g_wfi0_t, g_wfo0, g_ffn0 = ffn_bwd(dx, x1, hf0, gu0, act0, 0, small_early[4])
    reduce_ffn0 = _reduce_start([g_wfi0_t, g_wfo0], name="reduce_ffn0_start")
    dheads = _matmul(dx, W_out[0], tb=True, name="out_proj_dx0", dep=reduce_ffn0[4])
    g_wout0 = _matmul(heads0, dx, ta=True, out_dtype=BF16, name="out_proj_dw0")
    dp, g_lb0, g_onorm = _hgrn2_bwd(p0, lb0, hg_onorm, o0, states, dheads, name="hgrn2_bwd")
    dp, dk, dv = _attn_bwd(p0, 4 * D_TOK // D_MEM, kv[0], dheads, dp, name="attn_bwd0")
    g_wkv0, g_mem0 = mem_bwd(jnp.concatenate([dk, dv], axis=1), 0)
    g_whg_t = _matmul(dp, h0, ta=True, out_dtype=BF16, name="hg_in_dw")
    reduce_mix0 = _reduce_start([g_wkv0, g_wout0, g_whg_t], name="reduce_mix0_start")
    grad_x, g_mix0 = _matmul(dp, W_hgT, res=dx, norm_bwd=(xs, mix_norm[0:1]), name="hg_in_dx", dep=reduce_mix0[4])

    g_kv1, g_out1, g_gm_t, g_fi1_t, g_fo1 = _reduce_finish(reduce_l1, grad_x, me, name="reduce_l1")
    g_fi0_t, g_fo0 = _reduce_finish(reduce_ffn0, g_kv1, me, name="reduce_ffn0")
    g_kv0, g_out0, g_hg_t = _reduce_finish(reduce_mix0, g_fi0_t, me, name="reduce_mix0")
    g_shards = [jnp.stack([g_kv0, g_kv1]), jnp.stack([g_out0, g_out1]), g_hg_t[None], g_gm_t[None],
                jnp.stack([g_fi0_t, g_fi1_t]), jnp.stack([g_fo0, g_fo1])]
    transposed = (4, 7, 13)

    late = [g_ffn0, g_lb0, g_onorm, g_mem0, g_mix0]
    late_rows = _pad_rows(jnp.concatenate(late, axis=1).reshape(-1, LANE), 8)
    red_late = _all_gather(late_rows, name="reduce_small_late", in_vmem=True, reduce_sum=True)
    gathered = _exchange_wait(small_early, red_late, name="reduce_small_wait", scatter=False)[1][0]
    red_early = _sum_blocks(gathered.reshape(N_DEV, -1, LANE), name="reduce_small_sum")

    def split(flat, parts):
        out, off = [], 0
        for t in parts:
            out.append(flat[off:off + t.shape[1]])
            off += t.shape[1]
        return out

    r_loss, r_final, r_ffn1, r_mix1, r_mem1, r_ws, r_bs, r_lng, r_lnb = split(red_early.reshape(-1), early)
    r_ffn0, r_lb0, r_onorm, r_mem0, r_mix0 = split(red_late.reshape(-1), late)
    loss = r_loss[0]
    g_mix_norm = jnp.stack([r_mix0, r_mix1])
    g_mem_norm = jnp.stack([r_mem0, r_mem1])
    g_hg_lb = r_lb0[None, :] * lb0 * (jnp.eye(3, dtype=F32)[:, 0:1] - lb_soft)
    g_hg_onorm = r_onorm.reshape(1, D_TOK)
    width = D_TOK // N_DEV
    g_gm_ln_g = lax.dynamic_slice(r_lng, (me * width,), (width,)).reshape(1, width)
    g_gm_ln_b = lax.dynamic_slice(r_lnb, (me * width,), (width,)).reshape(1, width)
    g_gm_ws = r_ws.reshape(gm_ws.shape)
    g_gm_bs = r_bs.reshape(gm_bs.shape)
    g_ffn_norm = jnp.stack([r_ffn0, r_ffn1])
    g_final_norm = r_final

    grads = [g_mix_norm, g_mem_norm, g_shards[0], g_shards[1], g_shards[2], g_hg_lb, g_hg_onorm, g_shards[3],
             g_gm_ln_g, g_gm_ln_b, g_gm_ws, g_gm_bs, g_ffn_norm, g_shards[4], g_shards[5], g_final_norm]
    weights = [mix_norm, mem_norm, w_mem_kv, w_out, hg_w_in, hg_lb, hg_onorm, gm_w_in, gm_ln_g, gm_ln_b, gm_ws, gm_bs,
               ffn_norm, w_ffn_in, w_ffn_out, final_norm]
    ms = [m_mix_norm, m_mem_norm, m_w_mem_kv, m_w_out, m_hg_w_in, m_hg_lb, m_hg_onorm, m_gm_w_in, m_gm_ln_g,
          m_gm_ln_b, m_gm_ws, m_gm_bs, m_ffn_norm, m_w_ffn_in, m_w_ffn_out, m_final_norm]
    vs = [v_mix_norm, v_mem_norm, v_w_mem_kv, v_w_out, v_hg_w_in, v_hg_lb, v_hg_onorm, v_gm_w_in, v_gm_ln_g,
          v_gm_ln_b, v_gm_ws, v_gm_bs, v_ffn_norm, v_w_ffn_in, v_w_ffn_out, v_final_norm]
    deltas, new_m, new_v = [], [], []
    for n, (w, g, m, v) in enumerate(zip(weights, grads, ms, vs)):
        if w.ndim == 1:
            d, nm, nv = _adamw(w[None], g.reshape(1, -1), m[None], v[None], name=f"adamw{n}")
            d, nm, nv = d[0], nm[0], nv[0]
        elif n in transposed:
            flip = lambda t: jnp.swapaxes(t, 1, 2)
            d, nm, nv = (flip(t) for t in _adamw(flip(w), g, flip(m), flip(v), name=f"adamw{n}"))
            grads[n] = flip(g)
        else:
            d, nm, nv = _adamw(w, g.reshape(w.shape), m, v, name=f"adamw{n}")
        deltas.append(d)
        new_m.append(nm)
        new_v.append(nv)
    grads = [g.reshape(w.shape) for g, w in zip(grads, weights)]
    return (loss, grad_x[None], *grads, *deltas, *new_m, *new_v)
```

```python
import jax
import jax.numpy as jnp
from jax import lax
from jax.experimental import pallas as pl
from jax.experimental.pallas import tpu as pltpu

F32 = jnp.float32
BF16 = jnp.bfloat16
MXU_DTYPE = jnp.bfloat16
MESH_ID = pl.DeviceIdType.MESH

N_DEV = 8
EPS = 1e-6
D_MODEL = 1024
D_TOK = 768
D_MEM = 256
N_HEADS = 6
HEAD = 128
MEM_HEADS = 4
MEM_HDIM = 64
GM_CHUNK = 128
ATTN_ROWS = 2048
D_FF = 2816
HG_SUB = 16
HG_GROUP = 6
HG_IN = 4 * D_TOK + D_MEM
GM_IN = 2 * D_TOK + D_MEM
LANE = 128
MXU_COLS = 256

ADAM_LR = 0.001
ADAM_B1 = 0.9
ADAM_B2 = 0.999
ADAM_EPS = 1e-08
ADAM_WD = 0.01
ADAM_STEP = 10

VMEM_LIMIT = 48 * 2 ** 20
VMEM_LIMIT_WIDE = 58 * 2 ** 20


def _params(sem=None, limit=VMEM_LIMIT):
    return pltpu.CompilerParams(dimension_semantics=sem, vmem_limit_bytes=limit)


def _tile(n, cap, q=LANE):
    if n <= cap:
        return n
    best = None
    for t in range(q, cap + 1, q):
        if n % t == 0:
            best = t
    assert best is not None, (n, cap, q)
    return best


def _sigmoid(x):
    return 1.0 / (1.0 + jnp.exp(-x))


def _gelu(x, with_grad=False):
    cdf = 0.5 * (1.0 + lax.erf(x * 0.7071067811865476))
    if not with_grad:
        return x * cdf
    return x * cdf, cdf + x * jnp.exp(-0.5 * x * x) * 0.3989422804014327


def _matmul(a, b, *, name, ta=False, tb=False, res=None, out_dtype=F32, a_halves=False, b_halves=False, dep=None,
            norm_gain=None, norm_bwd=None, loss_head=None):
    if a_halves and ta:
        K, M = a.shape[1], 2 * a.shape[2]
    elif a_halves:
        M, K = a.shape[1], 2 * a.shape[2]
    else:
        K, M = a.shape if ta else a.shape[::-1]
    if b_halves:
        assert not tb and b.shape[1] == K
        N = 2 * b.shape[2]
    else:
        N = b.shape[0] if tb else b.shape[1]
        assert (b.shape[1] if tb else b.shape[0]) == K
    tm = _tile(M // 2 if (a_halves and ta) else M, 1664 if ta else 1024)
    tn = _tile(N // 2 if b_halves else N, 1792)
    tk = _tile(K // 2 if (a_halves and not ta) else K, 1024 if ta else 1664)
    nk = K // tk
    dims = (((0 if ta else 1,), (1 if tb else 0,)), ((), ()))

    strips = norm_bwd is not None or loss_head is not None
    fused = norm_gain is not None or strips
    n_in = 2 + (res is not None) + (norm_gain is not None) + 2 * strips + (dep is not None)
    if fused:
        assert tn == N, "the fused norm needs whole rows"
        assert (norm_gain is not None) + (norm_bwd is not None) + (loss_head is not None) == 1
        assert not strips or (res is not None and nk > 1 and tm % LANE == 0)

    def body(*refs):
        a_ref, b_ref = refs[:2]
        r_ref = refs[2] if res is not None else None
        g_ref = refs[2 + (res is not None)] if fused else None
        x_ref = refs[3 + (res is not None)] if strips else None
        o_ref = refs[n_in]
        h_ref = refs[n_in + 1] if fused else None
        l_ref = refs[n_in + 2] if loss_head is not None else None
        acc = None if nk == 1 else refs[-1]
        k = pl.program_id(2)

        def product():
            return lax.dot_general(a_ref[...].astype(MXU_DTYPE), b_ref[...].astype(MXU_DTYPE), dims,
                                   preferred_element_type=F32)

        def finish(r):
            if loss_head is not None:
                @pl.when(pl.program_id(0) == 0)
                def _():
                    h_ref[...] = jnp.zeros_like(h_ref)
                    l_ref[...] = jnp.zeros_like(l_ref)

                acc[...] = r + r_ref[...]
                gv = g_ref[...]

                def strip(s, carry):
                    dg, loss = carry
                    rows = pl.ds(pl.multiple_of(s * LANE, LANE), LANE)
                    xv = acc[rows, :]
                    scale = lax.rsqrt(jnp.mean(xv * xv, axis=-1, keepdims=True) + EPS)
                    xh = xv * scale
                    err = xh * gv - x_ref[rows, :]
                    loss = loss + 0.5 * jnp.sum(jnp.mean(err * err, axis=-1, keepdims=True), axis=0, keepdims=True)
                    dy = err * (1.0 / N)
                    u = dy * gv
                    o_ref[rows, :] = scale * (u - xh * jnp.mean(u * xh, axis=-1, keepdims=True))
                    return dg + jnp.sum(dy * xh, axis=0, keepdims=True), loss

                dg, loss = lax.fori_loop(0, tm // LANE, strip, (jnp.zeros((1, N), F32), jnp.zeros((1, 1), F32)))
                h_ref[...] += dg
                l_ref[...] += jnp.broadcast_to(loss, l_ref.shape)
                return
            if norm_bwd is not None:
                @pl.when(pl.program_id(0) == 0)
                def _():
                    h_ref[...] = jnp.zeros_like(h_ref)

                acc[...] = r
                gv = g_ref[...]

                def strip(s, dg):
                    rows = pl.ds(pl.multiple_of(s * LANE, LANE), LANE)
                    rv = acc[rows, :]
                    xv = x_ref[rows, :]
                    scale = lax.rsqrt(jnp.mean(xv * xv, axis=-1, keepdims=True) + EPS)
                    xh = xv * scale
                    u = rv * gv
                    o_ref[rows, :] = r_ref[rows, :] + scale * (u - xh * jnp.mean(u * xh, axis=-1, keepdims=True))
                    return dg + jnp.sum(rv * xh, axis=0, keepdims=True)

                h_ref[...] += lax.fori_loop(0, tm // LANE, strip, jnp.zeros((1, N), F32))
                return
            if res is not None:
                r = r + r_ref[...].astype(F32)
            o_ref[...] = r.astype(out_dtype)
            if norm_gain is not None:
                scale = lax.rsqrt(jnp.mean(r * r, axis=-1, keepdims=True) + EPS)
                h_ref[...] = (r * scale * g_ref[...]).astype(h_ref.dtype)

        if nk == 1:
            finish(product())
            return

        @pl.when(k == 0)
        def _():
            acc[...] = product()

        @pl.when((k > 0) & (k < nk - 1))
        def _():
            acc[...] += product()

        @pl.when(k == nk - 1)
        def _():
            finish(acc[...] + product())

    if a_halves and ta:
        mh = M // 2 // tm
        a_spec = pl.BlockSpec((None, tk, tm), lambda i, j, k: (i // mh, k, i % mh))
    elif a_halves:
        kh = nk // 2
        a_spec = pl.BlockSpec((None, tm, tk), lambda i, j, k: (k // kh, i, k % kh))
    elif ta:
        a_spec = pl.BlockSpec((tk, tm), lambda i, j, k: (k, i))
    else:
        a_spec = pl.BlockSpec((tm, tk), lambda i, j, k: (i, k))
    if b_halves:
        nh = N // 2 // tn
        b_spec = pl.BlockSpec((None, tk, tn), lambda i, j, k: (j // nh, k, j % nh))
    elif tb:
        b_spec = pl.BlockSpec((tn, tk), lambda i, j, k: (j, k))
    else:
        b_spec = pl.BlockSpec((tk, tn), lambda i, j, k: (k, j))
    o_spec = pl.BlockSpec((tm, tn), lambda i, j, k: (i, j))
    in_specs = [a_spec, b_spec] + ([o_spec] if res is not None else [])
    args = (a, b) + ((res,) if res is not None else ())
    out_specs, out_shape = o_spec, jax.ShapeDtypeStruct((M, N), out_dtype)
    vec = pl.BlockSpec((1, N), lambda i, j, k: (0, 0))
    sem = ("parallel", "parallel", "arbitrary")
    if norm_gain is not None:
        in_specs.append(vec)
        args += (norm_gain,)
        out_specs, out_shape = [o_spec, o_spec], [out_shape, jax.ShapeDtypeStruct((M, N), BF16)]
    if norm_bwd is not None:
        x_in, gain = norm_bwd
        in_specs += [vec, o_spec]
        args += (gain, x_in)
        out_specs, out_shape = [o_spec, vec], [out_shape, jax.ShapeDtypeStruct((1, N), F32)]
        sem = ("arbitrary", "arbitrary", "arbitrary")
    if loss_head is not None:
        gain, target = loss_head
        in_specs += [vec, o_spec]
        args += (gain, target)
        one = pl.BlockSpec((1, LANE), lambda i, j, k: (0, 0))
        out_specs = [o_spec, vec, one]
        out_shape = [out_shape, jax.ShapeDtypeStruct((1, N), F32), jax.ShapeDtypeStruct((1, LANE), F32)]
        sem = ("arbitrary", "arbitrary", "arbitrary")
    if dep is not None:
        in_specs.append(pl.BlockSpec(memory_space=pl.ANY))
        args += (dep,)
    return pl.pallas_call(
        body, name=name, grid=(M // tm, N // tn, nk), in_specs=in_specs, out_specs=out_specs, out_shape=out_shape,
        scratch_shapes=[] if nk == 1 else [pltpu.VMEM((tm, tn), F32)],
        compiler_params=_params(sem, VMEM_LIMIT_WIDE if strips else VMEM_LIMIT))(*args)


def _ffn_in(hf, wt, *, name):
    S, K = hf.shape
    tm = _tile(S, 512)
    tn = _tile(D_FF, 1408)
    nh = D_FF // tn
    nt = (((1,), (1,)), ((), ()))

    def body(a_ref, bg_ref, bu_ref, gu_ref, act_ref):
        av = a_ref[...].astype(MXU_DTYPE)
        for c0 in range(0, tn, MXU_COLS):
            cs = slice(c0, min(c0 + MXU_COLS, tn))
            gate = lax.dot_general(av, bg_ref[cs, :].astype(MXU_DTYPE), nt, preferred_element_type=F32)
            up = lax.dot_general(av, bu_ref[cs, :].astype(MXU_DTYPE), nt, preferred_element_type=F32)
            gu_ref[0, :, cs] = gate.astype(gu_ref.dtype)
            gu_ref[1, :, cs] = up.astype(gu_ref.dtype)
            act_ref[:, cs] = (gate * _sigmoid(gate) * up).astype(act_ref.dtype)

    return pl.pallas_call(
        body, name=name, grid=(nh, S // tm),
        in_specs=[pl.BlockSpec((tm, K), lambda j, i: (i, 0)), pl.BlockSpec((tn, K), lambda j, i: (j, 0)),
                  pl.BlockSpec((tn, K), lambda j, i: (j + nh, 0))],
        out_specs=[pl.BlockSpec((2, tm, tn), lambda j, i: (0, i, j)), pl.BlockSpec((tm, tn), lambda j, i: (i, j))],
        out_shape=[jax.ShapeDtypeStruct((2, S, D_FF), BF16), jax.ShapeDtypeStruct((S, D_FF), BF16)],
        compiler_params=_params(("parallel", "parallel")))(hf, wt, wt)


def _ffn_out_dx(dx, w, gu, dep, *, name):
    S, K = dx.shape
    tm = _tile(S, 1024)
    tn = _tile(D_FF, 1408)

    def body(a_ref, b_ref, gu_ref, dep_ref, o_ref):
        del dep_ref
        av = a_ref[...].astype(MXU_DTYPE)
        for c0 in range(0, tn, MXU_COLS):
            cs = slice(c0, min(c0 + MXU_COLS, tn))
            da = lax.dot_general(av, b_ref[cs, :].astype(MXU_DTYPE), (((1,), (1,)), ((), ())),
                                 preferred_element_type=F32)
            gate = gu_ref[0, :, cs].astype(F32)
            up = gu_ref[1, :, cs].astype(F32)
            sg = _sigmoid(gate)
            o_ref[0, :, cs] = (da * up * sg * (1.0 + gate * (1.0 - sg))).astype(o_ref.dtype)
            o_ref[1, :, cs] = (da * gate * sg).astype(o_ref.dtype)

    halves = pl.BlockSpec((2, tm, tn), lambda i, j: (0, i, j))
    return pl.pallas_call(
        body, name=name, grid=(S // tm, D_FF // tn),
        in_specs=[pl.BlockSpec((tm, K), lambda i, j: (i, 0)), pl.BlockSpec((tn, K), lambda i, j: (j, 0)), halves,
                  pl.BlockSpec(memory_space=pl.ANY)],
        out_specs=halves, out_shape=jax.ShapeDtypeStruct((2, S, D_FF), BF16),
        compiler_params=_params(("parallel", "parallel")))(dx, w, gu, dep)


def _rms_fwd(x, g, *, name, dep=None):
    R, Dm = x.shape
    tr = _tile(R, 512, 8)

    def body(x_ref, g_ref, *rest):
        o_ref = rest[-1]
        xv = x_ref[...]
        r = lax.rsqrt(jnp.mean(xv * xv, axis=-1, keepdims=True) + EPS)
        o_ref[...] = (xv * r * g_ref[...]).astype(o_ref.dtype)

    in_specs = [pl.BlockSpec((tr, Dm), lambda i: (i, 0)), pl.BlockSpec((1, Dm), lambda i: (0, 0))]
    args = (x, g)
    if dep is not None:
        in_specs.append(pl.BlockSpec(memory_space=pl.ANY))
        args += (dep,)
    return pl.pallas_call(
        body, name=name, grid=(R // tr,), in_specs=in_specs,
        out_specs=pl.BlockSpec((tr, Dm), lambda i: (i, 0)), out_shape=jax.ShapeDtypeStruct((R, Dm), BF16),
        compiler_params=_params(("parallel",)))(*args)


def _rms_bwd(x, g, dh, dres, *, name):
    R, Dm = x.shape
    tr = _tile(R, 256, 8)

    def body(x_ref, g_ref, dh_ref, dres_ref, dx_ref, dg_ref):
        @pl.when(pl.program_id(0) == 0)
        def _():
            dg_ref[...] = jnp.zeros_like(dg_ref)

        xv = x_ref[...]
        r = lax.rsqrt(jnp.mean(xv * xv, axis=-1, keepdims=True) + EPS)
        xh = xv * r
        dhv = dh_ref[...].astype(F32)
        dg_ref[...] += jnp.sum(dhv * xh, axis=0, keepdims=True)
        u = dhv * g_ref[...]
        dx = r * (u - xh * jnp.mean(u * xh, axis=-1, keepdims=True))
        dx_ref[...] = dres_ref[...] + dx

    row = pl.BlockSpec((tr, Dm), lambda i: (i, 0))
    vec = pl.BlockSpec((1, Dm), lambda i: (0, 0))
    return pl.pallas_call(
        body, name=name, grid=(R // tr,), in_specs=[row, vec, row, row], out_specs=[row, vec],
        out_shape=[jax.ShapeDtypeStruct((R, Dm), F32), jax.ShapeDtypeStruct((1, Dm), F32)],
        compiler_params=_params(("arbitrary",)))(x, g, dh, dres)


def _head_mask(h):
    lane = lax.broadcasted_iota(jnp.int32, (1, D_MEM), 1)
    return (lane >= h * MEM_HDIM) & (lane < (h + 1) * MEM_HDIM)


def _attn_probs(qv, k_mx, mask):
    s = lax.dot_general(jnp.where(mask, qv, 0.0).astype(MXU_DTYPE), k_mx, (((1,), (1,)), ((), ())),
                        preferred_element_type=F32) * (MEM_HDIM ** -0.5)
    e = jnp.exp(s - jnp.max(s, axis=-1, keepdims=True))
    return e / jnp.sum(e, axis=-1, keepdims=True)


def _attn_fwd(p, qcol, kv, heads, *, name):
    S = p.shape[0]
    M = kv.shape[0]
    ts = _tile(S, ATTN_ROWS, 8)

    def body(q_ref, k_ref, v_ref, heads_in, o_ref):
        del heads_in
        qv = q_ref[...]
        kx = k_ref[...].astype(MXU_DTYPE)
        vv = v_ref[...]
        out = jnp.zeros((ts, D_MEM), F32)
        for h in range(MEM_HEADS):
            mask = _head_mask(h)
            pr = _attn_probs(qv, kx, mask)
            out = out + jnp.dot(pr.astype(MXU_DTYPE), jnp.where(mask, vv, 0.0).astype(MXU_DTYPE),
                                preferred_element_type=F32)
        o_ref[...] = out.astype(o_ref.dtype)

    return pl.pallas_call(
        body, name=name, grid=(S // ts,),
        in_specs=[pl.BlockSpec((ts, D_MEM), lambda i: (i, qcol)), pl.BlockSpec((M, D_MEM), lambda i: (0, 0)),
                  pl.BlockSpec((M, D_MEM), lambda i: (0, 1)), pl.BlockSpec(memory_space=pl.ANY)],
        out_specs=pl.BlockSpec((ts, D_MEM), lambda i: (i, D_TOK // D_MEM)),
        out_shape=jax.ShapeDtypeStruct(heads.shape, heads.dtype), input_output_aliases={3: 0},
        compiler_params=_params(("parallel",)))(p, kv, kv, heads)


def _attn_bwd(p, qcol, kv, dheads, dp, *, name):
    S = p.shape[0]
    M = kv.shape[0]
    ts = _tile(S, ATTN_ROWS, 8)
    scale = MEM_HDIM ** -0.5

    def body(q_ref, k_ref, v_ref, do_ref, dp_in, dq_ref, dk_ref, dv_ref):
        del dp_in

        @pl.when(pl.program_id(0) == 0)
        def _():
            dk_ref[...] = jnp.zeros_like(dk_ref)
            dv_ref[...] = jnp.zeros_like(dv_ref)

        qv = q_ref[...]
        kv_ = k_ref[...]
        kx = kv_.astype(MXU_DTYPE)
        vv = v_ref[...]
        dox = do_ref[...].astype(MXU_DTYPE)
        qx = qv.astype(MXU_DTYPE)
        dq = jnp.zeros((ts, D_MEM), F32)
        for h in range(MEM_HEADS):
            mask = _head_mask(h)
            pr = _attn_probs(qv, kx, mask)
            vh = jnp.where(mask, vv, 0.0).astype(MXU_DTYPE)
            dpr = lax.dot_general(dox, vh, (((1,), (1,)), ((), ())), preferred_element_type=F32)
            ds = (pr * (dpr - jnp.sum(dpr * pr, axis=-1, keepdims=True)) * scale).astype(MXU_DTYPE)
            dq = dq + jnp.dot(ds, jnp.where(mask, kv_, 0.0).astype(MXU_DTYPE), preferred_element_type=F32)
            dk_h = lax.dot_general(ds, qx, (((0,), (0,)), ((), ())), preferred_element_type=F32)
            dv_h = lax.dot_general(pr.astype(MXU_DTYPE), dox, (((0,), (0,)), ((), ())), preferred_element_type=F32)
            dk_ref[...] += jnp.where(mask, dk_h, 0.0)
            dv_ref[...] += jnp.where(mask, dv_h, 0.0)
        dq_ref[...] = dq.astype(dq_ref.dtype)

    return pl.pallas_call(
        body, name=name, grid=(S // ts,),
        in_specs=[pl.BlockSpec((ts, D_MEM), lambda i: (i, qcol)), pl.BlockSpec((M, D_MEM), lambda i: (0, 0)),
                  pl.BlockSpec((M, D_MEM), lambda i: (0, 1)),
                  pl.BlockSpec((ts, D_MEM), lambda i: (i, D_TOK // D_MEM)), pl.BlockSpec(memory_space=pl.ANY)],
        out_specs=[pl.BlockSpec((ts, D_MEM), lambda i: (i, qcol)), pl.BlockSpec((M, D_MEM), lambda i: (0, 0)),
                   pl.BlockSpec((M, D_MEM), lambda i: (0, 0))],
        out_shape=[jax.ShapeDtypeStruct(dp.shape, dp.dtype), jax.ShapeDtypeStruct((M, D_MEM), F32),
                   jax.ShapeDtypeStruct((M, D_MEM), F32)],
        input_output_aliases={4: 0}, compiler_params=_params(("arbitrary",)))(p, kv, kv, dheads, dp)


def _gm_forward_parts(u_ref, v_ref, lng_ref, lnb_ref, w_ref, bsb_ref, with_grad=False):
    if with_grad:
        (zu, du_gelu), (zv, dv_gelu) = _gelu(u_ref[...], True), _gelu(v_ref[...], True)
    else:
        zu, zv, du_gelu, dv_gelu = _gelu(u_ref[...]), _gelu(v_ref[...]), None, None
    mu = jnp.mean(zv, axis=-1, keepdims=True)
    cen = zv - mu
    rs = lax.rsqrt(jnp.mean(cen * cen, axis=-1, keepdims=True) + EPS)
    vh = cen * rs
    vn = vh * lng_ref[...] + lnb_ref[...]
    row = lax.broadcasted_iota(jnp.int32, (GM_CHUNK, GM_CHUNK), 0)
    col = lax.broadcasted_iota(jnp.int32, (GM_CHUNK, GM_CHUNK), 1)
    tril = row >= col
    wm = [jnp.where(tril, w_ref[g], 0.0).astype(MXU_DTYPE) for g in range(N_HEADS)]
    vnx = [vn[:, g * HEAD:(g + 1) * HEAD].astype(MXU_DTYPE) for g in range(N_HEADS)]
    sv = [jnp.dot(wm[g], vnx[g], preferred_element_type=F32) + bsb_ref[g] for g in range(N_HEADS)]
    return zu, vh, rs, wm, vnx, sv, tril, du_gelu, dv_gelu


def _gmlp_fwd(p, lng, lnb, ws, bsb, *, name):
    S = p.shape[0]

    def body(u_ref, v_ref, lng_ref, lnb_ref, w_ref, bsb_ref, o_ref):
        zu, _, _, _, _, sv, _, _, _ = _gm_forward_parts(u_ref, v_ref, lng_ref, lnb_ref, w_ref, bsb_ref)
        for g in range(N_HEADS):
            o_ref[:, g * HEAD:(g + 1) * HEAD] = (zu[:, g * HEAD:(g + 1) * HEAD] * sv[g]).astype(o_ref.dtype)

    blk = lambda c: pl.BlockSpec((GM_CHUNK, D_TOK), lambda i: (i, c))
    vec = pl.BlockSpec((1, D_TOK), lambda i: (0, 0))
    cube = pl.BlockSpec((N_HEADS, GM_CHUNK, GM_CHUNK), lambda i: (0, 0, 0))
    return pl.pallas_call(
        body, name=name, grid=(S // GM_CHUNK,), in_specs=[blk(0), blk(1), vec, vec, cube, cube],
        out_specs=blk(0), out_shape=jax.ShapeDtypeStruct((S, D_MODEL), BF16),
        compiler_params=_params(("parallel",)))(p, p, lng, lnb, ws, bsb)


def _gmlp_bwd(p, lng, lnb, ws, bsb, dheads, *, name):
    S = p.shape[0]

    def body(u_ref, v_ref, lng_ref, lnb_ref, w_ref, bsb_ref, dt_ref, dp_ref, dw_ref, dbs_ref, dlg_ref, dlb_ref):
        @pl.when(pl.program_id(0) == 0)
        def _():
            dw_ref[...] = jnp.zeros_like(dw_ref)
            dbs_ref[...] = jnp.zeros_like(dbs_ref)
            dlg_ref[...] = jnp.zeros_like(dlg_ref)
            dlb_ref[...] = jnp.zeros_like(dlb_ref)

        zu, vh, rs, wm, vnx, sv, tril, du_gelu, dv_gelu = _gm_forward_parts(
            u_ref, v_ref, lng_ref, lnb_ref, w_ref, bsb_ref, with_grad=True)
        dt = dt_ref[...].astype(F32)
        dvn_parts = []
        for g in range(N_HEADS):
            sl = slice(g * HEAD, (g + 1) * HEAD)
            dsv = dt[:, sl] * zu[:, sl]
            dp_ref[:, sl] = (dt[:, sl] * sv[g] * du_gelu[:, sl]).astype(dp_ref.dtype)
            dsx = dsv.astype(MXU_DTYPE)
            dw = lax.dot_general(dsx, vnx[g], (((1,), (1,)), ((), ())), preferred_element_type=F32)
            dw_ref[g] += jnp.where(tril, dw, 0.0)
            dbs_ref[g] += jnp.sum(dsv, axis=-1, keepdims=True)
            dvn_parts.append(lax.dot_general(wm[g], dsx, (((0,), (0,)), ((), ())), preferred_element_type=F32))
        dvn = jnp.concatenate(dvn_parts, axis=-1)
        dlg_ref[...] += jnp.sum(dvn * vh, axis=0, keepdims=True)
        dlb_ref[...] += jnp.sum(dvn, axis=0, keepdims=True)
        dvh = dvn * lng_ref[...]
        dzv = rs * (dvh - jnp.mean(dvh, axis=-1, keepdims=True) - vh * jnp.mean(dvh * vh, axis=-1, keepdims=True))
        dp_ref[:, D_TOK:] = (dzv * dv_gelu).astype(dp_ref.dtype)

    blk = lambda c: pl.BlockSpec((GM_CHUNK, D_TOK), lambda i: (i, c))
    vec = pl.BlockSpec((1, D_TOK), lambda i: (0, 0))
    cube = pl.BlockSpec((N_HEADS, GM_CHUNK, GM_CHUNK), lambda i: (0, 0, 0))
    col = pl.BlockSpec((N_HEADS, GM_CHUNK, 1), lambda i: (0, 0, 0))
    return pl.pallas_call(
        body, name=name, grid=(S // GM_CHUNK,), in_specs=[blk(0), blk(1), vec, vec, cube, cube, blk(0)],
        out_specs=[pl.BlockSpec((GM_CHUNK, 2 * D_TOK), lambda i: (i, 0)), cube, col, vec, vec],
        out_shape=[jax.ShapeDtypeStruct((S, GM_IN), BF16), jax.ShapeDtypeStruct((N_HEADS, GM_CHUNK, GM_CHUNK), F32),
                   jax.ShapeDtypeStruct((N_HEADS, GM_CHUNK, 1), F32), jax.ShapeDtypeStruct((1, D_TOK), F32),
                   jax.ShapeDtypeStruct((1, D_TOK), F32)],
        compiler_params=_params(("arbitrary",)))(p, p, lng, lnb, ws, bsb, dheads)


def _chunk_tri(n, chunk, upper):
    r = lax.broadcasted_iota(jnp.int32, (n, n), 0)
    c = lax.broadcasted_iota(jnp.int32, (n, n), 1)
    same = (r // chunk) == (c // chunk)
    return jnp.where(same & ((r <= c) if upper else (r >= c)), 1.0, 0.0).astype(F32)


def _running_sum(tri, x):
    hi = x.astype(BF16)
    rest = x - hi.astype(F32)
    mid = rest.astype(BF16)
    lo = (rest - mid.astype(F32)).astype(BF16)
    tri = tri.astype(BF16)
    return (jnp.dot(tri, hi, preferred_element_type=F32) + jnp.dot(tri, mid, preferred_element_type=F32)
            + jnp.dot(tri, lo, preferred_element_type=F32))


MASKED = -1e30


def _pair_masks(mask_ref, n, upper):
    row = lax.broadcasted_iota(jnp.int32, (n, HEAD), 0)
    for i in range(n):
        mask_ref[i] = jnp.where((row <= i) if upper else (row >= i), 0.0, MASKED).astype(F32)


def _hg_gates(fz, lb):
    sg = _sigmoid(fz)
    f = lb + (1.0 - lb) * sg
    kk = (1.0 - lb) * (1.0 - sg)
    return sg, f, jnp.log(f), kk


def _hgrn2_fwd(p, lb, onorm, *, name):
    S = p.shape[0]
    C = HG_SUB
    tb = _tile(S, 256, C)
    nsub = tb // C

    def body(q_ref, fz_ref, v_ref, g_ref, lb_ref, on_ref, tok_ref, o_ref, st_ref, state, b_blk, k_blk, lk_blk, bsc, ksc,
             vsc):
        @pl.when(pl.program_id(0) == 0)
        def _():
            state[...] = jnp.zeros_like(state)

        _, _, lg, kk = _hg_gates(fz_ref[...], lb_ref[...])
        b_blk[...] = _running_sum(_chunk_tri(tb, C, False), lg)
        k_blk[...] = kk
        lk_blk[...] = jnp.log(kk) - b_blk[...]
        tt = lax.broadcasted_iota(jnp.int32, (C, HEAD), 0)

        def sub(c, carry):
            rows = pl.ds(pl.multiple_of(c * C, C), C)
            for h in range(N_HEADS):
                cols = slice(h * HEAD, (h + 1) * HEAD)
                qv = q_ref[rows, cols]
                vv = v_ref[rows, cols]
                b = b_blk[rows, cols]
                kk = k_blk[rows, cols]
                st0 = state[h]
                st0x = st0.astype(MXU_DTYPE)
                st_ref[c, h] = st0x.astype(st_ref.dtype)
                inter = lax.dot_general((qv * jnp.exp(b)).astype(MXU_DTYPE), st0x,
                                        (((1,), (1,)), ((), ())), preferred_element_type=F32)
                bsc[h] = b
                ksc[h] = lk_blk[rows, cols]
                vsc[h] = vv
                intra = jnp.zeros((C, HEAD), F32)
                for s in range(C):
                    kdec = jnp.where(tt >= s, jnp.exp(b + ksc[h, pl.ds(s, 1), :]), 0.0)
                    a_s = jnp.sum(qv * kdec, axis=-1, keepdims=True)
                    intra = intra + a_s * vsc[h, pl.ds(s, 1), :]
                o_ref[rows, cols] = inter + intra
                b_last = bsc[h, pl.ds(C - 1, 1), :]
                ke = kk * jnp.exp(b_last - b)
                state[h] = st0 * jnp.exp(b_last) + lax.dot_general(
                    vv.astype(MXU_DTYPE), ke.astype(MXU_DTYPE), (((0,), (0,)), ((), ())),
                    preferred_element_type=F32)
            return carry

        lax.fori_loop(0, nsub, sub, 0, unroll=2)

        for h in range(N_HEADS):
            cols = slice(h * HEAD, (h + 1) * HEAD)
            o = o_ref[:, cols]
            gv = g_ref[:, cols]
            n = o * lax.rsqrt(jnp.mean(o * o, axis=-1, keepdims=True) + EPS)
            tok_ref[:, cols] = (n * (gv * _sigmoid(gv)) * on_ref[:, cols]).astype(tok_ref.dtype)

    blk = lambda c: pl.BlockSpec((tb, D_TOK), lambda i, c=c: (i, c))
    vec = pl.BlockSpec((1, D_TOK), lambda i: (0, 0))
    stb = pl.BlockSpec((nsub, N_HEADS, HEAD, HEAD), lambda i: (i, 0, 0, 0))
    return pl.pallas_call(
        body, name=name, grid=(S // tb,), in_specs=[blk(0), blk(1), blk(2), blk(3), vec, vec],
        out_specs=[blk(0), blk(0), stb],
        out_shape=[jax.ShapeDtypeStruct((S, D_MODEL), BF16), jax.ShapeDtypeStruct((S, D_TOK), F32),
                   jax.ShapeDtypeStruct((S // C, N_HEADS, HEAD, HEAD), BF16)],
        scratch_shapes=[pltpu.VMEM((N_HEADS, HEAD, HEAD), F32)] + [pltpu.VMEM((tb, D_TOK), F32)] * 3
        + [pltpu.VMEM((N_HEADS, C, HEAD), F32)] * 3,
        compiler_params=_params(("arbitrary",)))(p, p, p, p, lb, onorm)


def _hgrn2_bwd(p, lb, onorm, o, states, dheads, *, name):
    S = p.shape[0]
    C = HG_SUB
    tb = _tile(S, 256, C)
    nsub = tb // C
    nblk = S // tb

    def body(q_ref, fz_ref, v_ref, g_ref, lb_ref, on_ref, o_ref, st_ref, dt_ref, dp_ref, dlb_ref, don_ref, dstate,
             b_blk, k_blk, do_blk, db_blk, dk_blk, dq_blk, dv_blk, bsc, ksc, vsc, qsc, dosc, causal, anti):
        @pl.when(pl.program_id(0) == 0)
        def _():
            dstate[...] = jnp.zeros_like(dstate)
            dlb_ref[...] = jnp.zeros_like(dlb_ref)
            don_ref[...] = jnp.zeros_like(don_ref)

        for h in range(N_HEADS):
            cols = slice(h * HEAD, (h + 1) * HEAD)
            onv = on_ref[:, cols]
            gv = g_ref[:, cols]
            ov = o_ref[:, cols]
            dt = dt_ref[:, cols].astype(F32)
            sgg = _sigmoid(gv)
            sil = gv * sgg
            rinv = lax.rsqrt(jnp.mean(ov * ov, axis=-1, keepdims=True) + EPS)
            n = ov * rinv
            don_ref[:, cols] += jnp.sum(dt * n * sil, axis=0, keepdims=True)
            dn = dt * sil * onv
            dp_ref[:, 3 * D_TOK + h * HEAD:3 * D_TOK + (h + 1) * HEAD] = (
                dt * n * onv * sgg * (1.0 + gv * (1.0 - sgg))).astype(dp_ref.dtype)
            do_blk[:, cols] = rinv * (dn - n * jnp.mean(dn * n, axis=-1, keepdims=True))
        _, _, lg, kk = _hg_gates(fz_ref[...], lb_ref[...])
        b_blk[...] = _running_sum(_chunk_tri(tb, C, False), lg)
        k_blk[...] = kk
        _pair_masks(causal, C, False)
        _pair_masks(anti, C, True)
        tt = lax.broadcasted_iota(jnp.int32, (C, HEAD), 0)

        def sub(j, heads):
            c = nsub - 1 - j
            rows = pl.ds(pl.multiple_of(c * C, C), C)
            for h in heads:
                cols = slice(h * HEAD, (h + 1) * HEAD)
                qv = q_ref[rows, cols]
                vv = v_ref[rows, cols]
                do = do_blk[rows, cols]
                b = b_blk[rows, cols]
                kk = k_blk[rows, cols]
                bsc[h] = b
                ksc[h] = kk
                vsc[h] = vv
                qsc[h] = qv
                dosc[h] = do
                b_last = bsc[h, pl.ds(C - 1, 1), :]
                eb = jnp.exp(b)
                qe = qv * eb
                ebb = jnp.exp(b_last - b)
                ke = kk * ebb
                e_last = jnp.exp(b_last)
                st0x = st_ref[c, h].astype(MXU_DTYPE)
                st0 = st0x.astype(F32)
                dst1 = dstate[h]
                dst1x = dst1.astype(MXU_DTYPE)
                dox = do.astype(MXU_DTYPE)
                dqe = jnp.dot(dox, st0x, preferred_element_type=F32)
                dke = jnp.dot(vv.astype(MXU_DTYPE), dst1x, preferred_element_type=F32)
                dv = lax.dot_general(ke.astype(MXU_DTYPE), dst1x, (((1,), (1,)), ((), ())),
                                     preferred_element_type=F32)
                db_last = (e_last * jnp.sum(st0 * dst1, axis=0, keepdims=True)
                           + jnp.sum(dke * ke, axis=0, keepdims=True))
                dstate[h] = dst1 * e_last + lax.dot_general(dox, qe.astype(MXU_DTYPE), (((0,), (0,)), ((), ())),
                                                            preferred_element_type=F32)
                dq_pairs = jnp.zeros((C, HEAD), F32)
                for s in range(C):
                    dec = jnp.exp(b - bsc[h, pl.ds(s, 1), :] + causal[s])
                    da_s = jnp.sum(do * vsc[h, pl.ds(s, 1), :], axis=-1, keepdims=True)
                    dq_pairs = dq_pairs + da_s * (ksc[h, pl.ds(s, 1), :] * dec)
                dk_pairs = jnp.zeros((C, HEAD), F32)
                for t in range(C):
                    do_t = dosc[h, pl.ds(t, 1), :]
                    qd = qsc[h, pl.ds(t, 1), :] * jnp.exp(bsc[h, pl.ds(t, 1), :] - b + anti[t])
                    da_t = jnp.sum(vv * do_t, axis=-1, keepdims=True)
                    dk_pairs = dk_pairs + da_t * qd
                    a_t = jnp.sum(qd * kk, axis=-1, keepdims=True)
                    dv = dv + a_t * do_t
                db = dqe * qe - dke * ke + qv * dq_pairs - kk * dk_pairs
                db_blk[rows, cols] = db + jnp.where(tt == C - 1, db_last, 0.0)
                dk_blk[rows, cols] = dke * ebb + dk_pairs
                dq_blk[rows, cols] = dqe * eb + dq_pairs
                dv_blk[rows, cols] = dv

        for first in range(0, N_HEADS, HG_GROUP):
            heads = tuple(range(first, first + HG_GROUP))
            pl.loop(0, nsub)(lambda j, heads=heads: sub(j, heads))

        dlg = _running_sum(_chunk_tri(tb, C, True), db_blk[...])
        lbv = lb_ref[...]
        sg, f, _, _ = _hg_gates(fz_ref[...], lbv)
        w = dlg / f - dk_blk[...]
        dp_ref[:, 0:D_TOK] = dq_blk[...].astype(dp_ref.dtype)
        dp_ref[:, 2 * D_TOK:3 * D_TOK] = dv_blk[...].astype(dp_ref.dtype)
        dp_ref[:, D_TOK:2 * D_TOK] = (w * (1.0 - lbv) * sg * (1.0 - sg)).astype(dp_ref.dtype)
        dlb_ref[...] += jnp.sum(w * (1.0 - sg), axis=0, keepdims=True)

    blk = lambda c: pl.BlockSpec((tb, D_TOK), lambda i, c=c: (nblk - 1 - i, c))
    vec = pl.BlockSpec((1, D_TOK), lambda i: (0, 0))
    stb = pl.BlockSpec((nsub, N_HEADS, HEAD, HEAD), lambda i: (nblk - 1 - i, 0, 0, 0))
    small = jax.ShapeDtypeStruct((1, D_TOK), F32)
    return pl.pallas_call(
        body, name=name, grid=(nblk,), in_specs=[blk(0), blk(1), blk(2), blk(3), vec, vec, blk(0), stb, blk(0)],
        out_specs=[pl.BlockSpec((tb, 4 * D_TOK), lambda i: (nblk - 1 - i, 0)), vec, vec],
        out_shape=[jax.ShapeDtypeStruct((S, HG_IN), BF16), small, small],
        scratch_shapes=[pltpu.VMEM((N_HEADS, HEAD, HEAD), F32)] + [pltpu.VMEM((tb, D_TOK), F32)] * 7
        + [pltpu.VMEM((N_HEADS, C, HEAD), F32)] * 5 + [pltpu.VMEM((C, C, HEAD), F32)] * 2,
        compiler_params=_params(("arbitrary",)))(p, p, p, p, lb, onorm, o, states, dheads)


def _adamw(w, g, m, v, *, name):
    shape = w.shape
    cols = shape[-1]
    w2, g2, m2, v2 = (t.reshape(-1, cols) for t in (w, g, m, v))
    R = w2.shape[0]
    tr = _tile(R, 512, 8)

    def body(w_ref, g_ref, m_ref, v_ref, d_ref, nm_ref, nv_ref):
        gv = g_ref[...]
        nm = ADAM_B1 * m_ref[...] + (1.0 - ADAM_B1) * gv
        nv = ADAM_B2 * v_ref[...] + (1.0 - ADAM_B2) * (gv * gv)
        m_hat = nm / (1.0 - ADAM_B1 ** ADAM_STEP)
        v_hat = nv / (1.0 - ADAM_B2 ** ADAM_STEP)
        d_ref[...] = -ADAM_LR * (m_hat / (jnp.sqrt(v_hat) + ADAM_EPS) + ADAM_WD * w_ref[...])
        nm_ref[...] = nm
        nv_ref[...] = nv

    spec = pl.BlockSpec((tr, cols), lambda i: (i, 0))
    out = jax.ShapeDtypeStruct((R, cols), F32)
    d, nm, nv = pl.pallas_call(body, name=name, grid=(R // tr,), in_specs=[spec] * 4, out_specs=[spec] * 3,
                               out_shape=[out] * 3, compiler_params=_params(("parallel",)))(w2, g2, m2, v2)
    return d.reshape(shape), nm.reshape(shape), nv.reshape(shape)


def _add_received(sent, got, me, *, name):
    n, R, Cc = got.shape
    tr = _tile(R, 256, 16)
    per = R // tr

    def body(me_ref, a_ref, b_ref, o_ref):
        del me_ref
        acc = a_ref[...].astype(F32)
        for k in range(n):
            acc = acc + b_ref[k].astype(F32)
        o_ref[...] = acc

    grid_spec = pltpu.PrefetchScalarGridSpec(
        num_scalar_prefetch=1, grid=(per,),
        in_specs=[pl.BlockSpec((tr, Cc), lambda i, me_ref: (me_ref[0] * per + i, 0)),
                  pl.BlockSpec((n, tr, Cc), lambda i, me_ref: (0, i, 0))],
        out_specs=pl.BlockSpec((tr, Cc), lambda i, me_ref: (i, 0)))
    return pl.pallas_call(body, name=name, grid_spec=grid_spec, out_shape=jax.ShapeDtypeStruct((R, Cc), F32),
                          compiler_params=_params(("parallel",)))(jnp.reshape(me, (1,)).astype(jnp.int32), sent, got)


def _sum_blocks(x, *, name):
    n, R, Cc = x.shape
    tr = _tile(R, 208, 8)

    def body(x_ref, o_ref):
        acc = x_ref[0]
        for k in range(1, n):
            acc = acc + x_ref[k]
        o_ref[...] = acc

    return pl.pallas_call(
        body, name=name, grid=(R // tr,), in_specs=[pl.BlockSpec((n, tr, Cc), lambda i: (0, i, 0))],
        out_specs=pl.BlockSpec((tr, Cc), lambda i: (i, 0)), out_shape=jax.ShapeDtypeStruct((R, Cc), F32),
        compiler_params=_params(("parallel",)))(x)


def _place():
    return lax.axis_index("x"), lax.axis_index("y"), lax.axis_index("c")


def _all_gather(x, *, name, in_vmem, reduce_sum=False, with_token=False):
    R, Cc = x.shape
    space = pltpu.VMEM if in_vmem else pl.ANY

    def body(x_ref, out_ref, *scratch):
        if with_token:
            scratch[0][...] = jnp.zeros_like(scratch[0])
            scratch = scratch[1:]
        if reduce_sum:
            gat_ref, send_sems, recv_sems, local_sem = scratch
        else:
            gat_ref = out_ref
            send_sems, recv_sems, local_sem = scratch
        mx, my, mc = _place()
        me, sibling = (mx, my, mc), (mx, my, 1 - mc)
        chips = [(1 - mx, my), (mx, 1 - my), (1 - mx, 1 - my)]

        def rows(px, py, pc):
            return gat_ref.at[pl.ds((4 * px + 2 * py + pc) * R, R), :]

        def copy(k, block, to, src=None):
            return pltpu.make_async_remote_copy(
                src_ref=rows(*block) if src is None else src, dst_ref=rows(*block), send_sem=send_sems.at[k],
                recv_sem=recv_sems.at[k], device_id=to, device_id_type=MESH_ID)

        mine = pltpu.make_async_copy(x_ref, rows(*me), local_sem)
        mine.start()
        first = [copy(0, me, sibling, src=x_ref)]
        first += [copy(1 + j, me, (*chip, mc), src=x_ref) for j, chip in enumerate(chips)]
        for cp in first:
            cp.start()
        passed = [copy(4 + j, (*chip, mc), sibling) for j, chip in enumerate(chips)]
        for j, chip in enumerate(chips):
            copy(1 + j, (*chip, mc), me).wait_recv()
            passed[j].start()
        copy(0, sibling, me).wait_recv()
        for j, chip in enumerate(chips):
            copy(4 + j, (*chip, 1 - mc), me).wait_recv()
        for cp in first + passed:
            cp.wait_send()
        mine.wait()
        if reduce_sum:
            acc = gat_ref[pl.ds(0, R), :]
            for d in range(1, N_DEV):
                acc = acc + gat_ref[pl.ds(d * R, R), :]
            out_ref[...] = acc

    sems = [pltpu.SemaphoreType.DMA((7,)), pltpu.SemaphoreType.DMA((7,)), pltpu.SemaphoreType.DMA]
    if reduce_sum:
        assert in_vmem
        out_shape = jax.ShapeDtypeStruct((R, Cc), x.dtype)
        scratch = [pltpu.VMEM((N_DEV * R, Cc), x.dtype)] + sems
    else:
        out_shape = jax.ShapeDtypeStruct((N_DEV * R, Cc), x.dtype)
        scratch = sems
    out_specs = pl.BlockSpec(memory_space=space)
    if with_token:
        out_shape = (out_shape, jax.ShapeDtypeStruct((8, LANE), F32))
        out_specs = (out_specs, pl.BlockSpec(memory_space=pltpu.VMEM))
    return pl.pallas_call(
        body, name=name, out_shape=out_shape, in_specs=[pl.BlockSpec(memory_space=space)], out_specs=out_specs,
        scratch_shapes=scratch, compiler_params=pltpu.CompilerParams(vmem_limit_bytes=VMEM_LIMIT))(x)


def _peer(k, mx, my, mc):
    bits = k + 1
    return (1 - mx if bits & 4 else mx, 1 - my if bits & 2 else my, 1 - mc if bits & 1 else mc)


HBM_SPEC = pl.BlockSpec(memory_space=pltpu.HBM)
SEM_SPEC = pl.BlockSpec(memory_space=pltpu.SEMAPHORE)
DATAFLOW = pltpu.SideEffectType.DATAFLOW_SIDE_EFFECTING


def _exchange_copies(x_refs, land_refs, send_sems, recv_sems, scatter):
    mx, my, mc = _place()
    me = 4 * mx + 2 * my + mc
    n = len(x_refs)
    copies = []
    for k in range(N_DEV - 1):
        px, py, pc = _peer(k, mx, my, mc)
        for m, (x_ref, land_ref) in enumerate(zip(x_refs, land_refs)):
            rows = land_ref.shape[1] if scatter else x_ref.shape[0]
            if scatter:
                src = x_ref.at[pl.ds(pl.multiple_of((4 * px + 2 * py + pc) * rows, 16), rows), :]
                dst = land_ref.at[k]
            else:
                src = x_ref
                dst = land_ref.at[pl.ds(pl.multiple_of(me * rows, 16), rows), :]
            copies.append(pltpu.make_async_remote_copy(
                src_ref=src, dst_ref=dst, send_sem=send_sems.at[k * n + m], recv_sem=recv_sems.at[k * n + m],
                device_id=(px, py, pc), device_id_type=MESH_ID))
    return copies


def _land_shape(x, scatter):
    return (N_DEV - 1, x.shape[0] // N_DEV, x.shape[1]) if scatter else (N_DEV * x.shape[0], x.shape[1])


def _own_copies(x_refs, land_refs, local_sems):
    mx, my, mc = _place()
    me = 4 * mx + 2 * my + mc
    return [pltpu.make_async_copy(
        x_ref, land_ref.at[pl.ds(pl.multiple_of(me * x_ref.shape[0], 16), x_ref.shape[0]), :], local_sems.at[m])
        for m, (x_ref, land_ref) in enumerate(zip(x_refs, land_refs))]


def _exchange_start(groups, *, name, scatter):
    sizes = [len(g) for g in groups]
    xs = [x for g in groups for x in g]
    n = len(xs)
    lands = [lax.empty(_land_shape(x, scatter), x.dtype) for x in xs]
    per = 2 if scatter else 3

    def body(*refs):
        sems = refs[2 * n:2 * n + per * len(groups)]
        token = refs[-1]
        off = 0
        for gi, m in enumerate(sizes):
            x_refs, land_refs = refs[off:off + m], refs[n + off:n + off + m]
            for cp in _exchange_copies(x_refs, land_refs, sems[per * gi], sems[per * gi + 1], scatter):
                cp.start()
            if not scatter:
                for cp in _own_copies(x_refs, land_refs, sems[per * gi + 2]):
                    cp.start()
            off += m
        token[...] = jnp.zeros_like(token)

    sem_shapes = []
    for m in sizes:
        sem_shapes += [pltpu.SemaphoreType.DMA(((N_DEV - 1) * m,))] * 2
        if not scatter:
            sem_shapes.append(pltpu.SemaphoreType.DMA((m,)))
    ns = len(sem_shapes)
    out = pl.pallas_call(
        body, name=name,
        out_shape=(*sem_shapes, *[pltpu.HBM(x.shape, x.dtype) for x in xs],
                   *[pltpu.HBM(l.shape, l.dtype) for l in lands], jax.ShapeDtypeStruct((8, LANE), F32)),
        in_specs=(HBM_SPEC,) * (2 * n),
        out_specs=(SEM_SPEC,) * ns + (HBM_SPEC,) * (2 * n) + (pl.BlockSpec(memory_space=pltpu.VMEM),),
        input_output_aliases={i: ns + i for i in range(2 * n)},
        compiler_params=pltpu.CompilerParams(has_side_effects=DATAFLOW))(
            *[pltpu.with_memory_space_constraint(t, pltpu.HBM) for t in xs + lands])
    started, off = [], 0
    for gi, m in enumerate(sizes):
        sems = out[per * gi:per * gi + per]
        started.append((sems[0], sems[1], list(out[ns + off:ns + off + m]),
                        list(out[ns + n + off:ns + n + off + m]), out[-1], None if scatter else sems[2]))
        off += m
    return started


def _exchange_wait(started, after, *, name, scatter):
    send_sems, recv_sems, xs, lands, _, local_sems = started
    n = len(xs)

    def body(*refs):
        x_refs, land_refs = refs[:n], refs[n:2 * n]
        for cp in _exchange_copies(x_refs, land_refs, refs[2 * n], refs[2 * n + 1], scatter):
            cp.wait_send()
            cp.wait_recv()
        if not scatter:
            for cp in _own_copies(x_refs, land_refs, refs[2 * n + 2]):
                cp.wait()

    sems = (send_sems, recv_sems) if scatter else (send_sems, recv_sems, local_sems)
    out = pl.pallas_call(
        body, name=name, out_shape=tuple(pltpu.HBM(t.shape, t.dtype) for t in xs + lands),
        in_specs=(HBM_SPEC,) * (2 * n) + (SEM_SPEC,) * len(sems) + (pl.BlockSpec(memory_space=pl.ANY),),
        out_specs=(HBM_SPEC,) * (2 * n), input_output_aliases={i: i for i in range(2 * n)},
        compiler_params=pltpu.CompilerParams(has_side_effects=DATAFLOW))(*xs, *lands, *sems, after)
    return list(out[:n]), list(out[n:])


def _gather_start(groups, token, *, name):
    first = groups[0]
    groups = [[first[0] + token[0, 0].astype(first[0].dtype)] + list(first[1:])] + [list(g) for g in groups[1:]]
    return _exchange_start(groups, name=name, scatter=False)


def _gather_finish(started, after, *, name):
    return _exchange_wait(started, after, name=name, scatter=False)[1]


def _reduce_start(grads, *, name):
    return _exchange_start([grads], name=name, scatter=True)[0]


def _reduce_finish(started, after, me, *, name):
    sent, gots = _exchange_wait(started, after, name=name + "_wait", scatter=True)
    return [_add_received(g, got, me, name=f"{name}_add{m}") for m, (g, got) in enumerate(zip(sent, gots))]


def _pad_rows(a, mult):
    r = (-a.shape[0]) % mult
    return a if r == 0 else jnp.concatenate([a, jnp.zeros((r,) + a.shape[1:], a.dtype)], axis=0)


def kernel(x, mem, mix_norm, mem_norm, w_mem_kv, w_out, hg_w_in, hg_lb, hg_onorm, gm_w_in, gm_ln_g, gm_ln_b, gm_ws, gm_bs, ffn_norm, w_ffn_in, w_ffn_out, final_norm, loss_target, m_mix_norm, m_mem_norm, m_w_mem_kv, m_w_out, m_hg_w_in, m_hg_lb, m_hg_onorm, m_gm_w_in, m_gm_ln_g, m_gm_ln_b, m_gm_ws, m_gm_bs, m_ffn_norm, m_w_ffn_in, m_w_ffn_out, m_final_norm, v_mix_norm, v_mem_norm, v_w_mem_kv, v_w_out, v_hg_w_in, v_hg_lb, v_hg_onorm, v_gm_w_in, v_gm_ln_g, v_gm_ln_b, v_gm_ws, v_gm_bs, v_ffn_norm, v_w_ffn_in, v_w_ffn_out, v_final_norm):
    mx, my, mc = _place()
    me = 4 * mx + 2 * my + mc
    xs = x[0]
    mems = mem[0]
    tgt = loss_target[0]

    hg_t = hg_w_in[0].T.astype(BF16)
    gm_t = gm_w_in[0].T.astype(BF16)
    fi_t = [w_ffn_in[i].T.astype(BF16) for i in range(2)]
    kv_b = [w_mem_kv[i].astype(BF16) for i in range(2)]
    out_b = [w_out[i].astype(BF16) for i in range(2)]
    fo_b = [w_ffn_out[i].astype(BF16) for i in range(2)]
    W_hgT, token = _all_gather(hg_t, name="gather_first", in_vmem=False, with_token=True)
    ln_local = _pad_rows(jnp.concatenate([gm_ln_g, gm_ln_b], axis=0), 16)
    ln_local = jnp.concatenate([ln_local, jnp.zeros((16, LANE - ln_local.shape[1]), F32)], axis=1)
    gather_mix, fi0, fo0, gather_gm, fi1, fo1 = _gather_start(
        [kv_b + out_b, [fi_t[0]], [fo_b[0]], [gm_t, ln_local], [fi_t[1]], [fo_b[1]]], token,
        name="gather_rest_start")
    gather_fi, gather_fo = [fi0, fi1], [fo0, fo1]

    lb_soft = jax.nn.softmax(hg_lb, axis=0)
    lb0 = lb_soft[0:1]
    bsb = jnp.broadcast_to(gm_bs[0][:, :, None], (N_HEADS, GM_CHUNK, GM_CHUNK))
    ws = gm_ws[0]

    W_fiT, W_fo = [], []

    def ffn_fwd(xin, hf, i, **tail):
        W_fiT.extend(_gather_finish(gather_fi[i], hf, name=f"gather_fi{i}_wait"))
        gu, act = _ffn_in(hf, W_fiT[i], name=f"ffn_in{i}")
        W_fo.extend(_gather_finish(gather_fo[i], act, name=f"gather_fo{i}_wait"))
        return gu, act, _matmul(act, W_fo[i], res=xin, name=f"ffn_out{i}", **tail)

    h0 = _rms_fwd(xs, mix_norm[0:1], name="mix_norm0", dep=gather_mix[4])
    p0 = _matmul(h0, W_hgT, tb=True, name="hg_in")
    heads0, o0, states = _hgrn2_fwd(p0, lb0, hg_onorm, name="hgrn2_fwd")

    kv0, kv1, wo0, wo1 = _gather_finish(gather_mix, o0, name="gather_mix_wait")
    W_kv, W_out = [kv0, kv1], [wo0, wo1]
    mem_n, kv = [], []
    for i in range(2):
        mn = _rms_fwd(mems, mem_norm[i:i + 1], name=f"mem_norm{i}")
        mem_n.append(mn)
        kv.append(_matmul(mn, W_kv[i], name=f"mem_kv{i}"))

    heads0 = _attn_fwd(p0, 4 * D_TOK // D_MEM, kv[0], heads0, name="attn_fwd0")
    x1, hf0 = _matmul(heads0, W_out[0], res=xs, norm_gain=ffn_norm[0:1], name="out_proj0")
    gu0, act0, (x2, h1) = ffn_fwd(x1, hf0, 0, norm_gain=mix_norm[1:2])

    W_gmT, ln_all = _gather_finish(gather_gm, h1, name="gather_gm_wait")
    ln_all = ln_all.reshape(N_DEV, 16, LANE)
    ln_g = ln_all[:, 0, :D_TOK // N_DEV].reshape(1, D_TOK)
    ln_b = ln_all[:, 1, :D_TOK // N_DEV].reshape(1, D_TOK)
    p1 = _matmul(h1, W_gmT, tb=True, name="gm_in")
    heads1 = _gmlp_fwd(p1, ln_g, ln_b, ws, bsb, name="gmlp_fwd")
    heads1 = _attn_fwd(p1, 2 * D_TOK // D_MEM, kv[1], heads1, name="attn_fwd1")
    x3, hf1 = _matmul(heads1, W_out[1], res=x2, norm_gain=ffn_norm[1:2], name="out_proj1")
    gu1, act1, (dx, g_final, loss_part) = ffn_fwd(x3, hf1, 1, loss_head=(final_norm.reshape(1, D_MODEL), tgt))

    def ffn_bwd(dx, xin, hf, gu, act, i, dep):
        dgu = _ffn_out_dx(dx, W_fo[i], gu, dep, name=f"ffn_out_dx{i}")
        g_wfo = _matmul(act, dx, ta=True, out_dtype=BF16, name=f"ffn_out_dw{i}")
        g_wfi_t = _matmul(dgu, hf, ta=True, a_halves=True, out_dtype=BF16, name=f"ffn_in_dw{i}")
        dx, g_norm = _matmul(dgu, W_fiT[i], a_halves=True, res=dx, norm_bwd=(xin, ffn_norm[i:i + 1]),
                             name=f"ffn_in_dx{i}")
        return dx, g_wfi_t, g_wfo, g_norm

    def mem_bwd(dkv, i):
        g_wkv = _matmul(mem_n[i], dkv, ta=True, out_dtype=BF16, name=f"mem_kv_dw{i}")
        dmn = _matmul(dkv, W_kv[i], tb=True, name=f"mem_kv_dx{i}")
        _, g_norm = _rms_bwd(mems, mem_norm[i:i + 1], dmn, jnp.zeros_like(mems), name=f"mem_norm_bwd{i}")
        return g_wkv, g_norm

    dx, g_wfi1_t, g_wfo1, g_ffn1 = ffn_bwd(dx, x3, hf1, gu1, act1, 1, loss_part)
    dheads = _matmul(dx, W_out[1], tb=True, name="out_proj_dx1")
    g_wout1 = _matmul(heads1, dx, ta=True, out_dtype=BF16, name="out_proj_dw1")
    dp, g_ws, g_bs, g_lng, g_lnb = _gmlp_bwd(p1, ln_g, ln_b, ws, bsb, dheads, name="gmlp_bwd")
    dp, dk, dv = _attn_bwd(p1, 2 * D_TOK // D_MEM, kv[1], dheads, dp, name="attn_bwd1")
    g_wkv1, g_mem1 = mem_bwd(jnp.concatenate([dk, dv], axis=1), 1)
    g_wgm_t = _matmul(dp, h1, ta=True, out_dtype=BF16, name="gm_in_dw")
    dx, g_mix1 = _matmul(dp, W_gmT, res=dx, norm_bwd=(x2, mix_norm[1:2]), name="gm_in_dx")
    reduce_l1 = _reduce_start([g_wkv1, g_wout1, g_wgm_t, g_wfi1_t, g_wfo1], name="reduce_l1_start")
    early = [loss_part, g_final, g_ffn1, g_mix1, g_mem1, g_ws.reshape(1, -1), g_bs.reshape(1, -1), g_lng, g_lnb]
    early_rows = _pad_rows(jnp.concatenate(early, axis=1).reshape(-1, LANE), 16) + reduce_l1[4][0, 0]
    small_early = _exchange_start([[early_rows]], name="reduce_small_start", scatter=False)[0]

    dx, g_wfi0_t, g_wfo0, g_ffn0 = ffn_bwd(dx, x1, hf0, gu0, act0, 0, small_early[4])
    reduce_ffn0 = _reduce_start([g_wfi0_t, g_wfo0], name="reduce_ffn0_start")
    dheads = _matmul(dx, W_out[0], tb=True, name="out_proj_dx0", dep=reduce_ffn0[4])
    g_wout0 = _matmul(heads0, dx, ta=True, out_dtype=BF16, name="out_proj_dw0")
    dp, g_lb0, g_onorm = _hgrn2_bwd(p0, lb0, hg_onorm, o0, states, dheads, name="hgrn2_bwd")
    dp, dk, dv = _attn_bwd(p0, 4 * D_TOK // D_MEM, kv[0], dheads, dp, name="attn_bwd0")
    g_wkv0, g_mem0 = mem_bwd(jnp.concatenate([dk, dv], axis=1), 0)
    g_whg_t = _matmul(dp, h0, ta=True, out_dtype=BF16, name="hg_in_dw")
    reduce_mix0 = _reduce_start([g_wkv0, g_wout0, g_whg_t], name="reduce_mix0_start")
    grad_x, g_mix0 = _matmul(dp, W_hgT, res=dx, norm_bwd=(xs, mix_norm[0:1]), name="hg_in_dx", dep=reduce_mix0[4])

    g_kv1, g_out1, g_gm_t, g_fi1_t, g_fo1 = _reduce_finish(reduce_l1, grad_x, me, name="reduce_l1")
    g_fi0_t, g_fo0 = _reduce_finish(reduce_ffn0, g_kv1, me, name="reduce_ffn0")
    g_kv0, g_out0, g_hg_t = _reduce_finish(reduce_mix0, g_fi0_t, me, name="reduce_mix0")
    g_shards = [jnp.stack([g_kv0, g_kv1]), jnp.stack([g_out0, g_out1]), g_hg_t[None], g_gm_t[None],
                jnp.stack([g_fi0_t, g_fi1_t]), jnp.stack([g_fo0, g_fo1])]
    transposed = (4, 7, 13)

    late = [g_ffn0, g_lb0, g_onorm, g_mem0, g_mix0]
    late_rows = _pad_rows(jnp.concatenate(late, axis=1).reshape(-1, LANE), 8)
    red_late = _all_gather(late_rows, name="reduce_small_late", in_vmem=True, reduce_sum=True)
    gathered = _exchange_wait(small_early, red_late, name="reduce_small_wait", scatter=False)[1][0]
    red_early = _sum_blocks(gathered.reshape(N_DEV, -1, LANE), name="reduce_small_sum")

    def split(flat, parts):
        out, off = [], 0
        for t in parts:
            out.append(flat[off:off + t.shape[1]])
            off += t.shape[1]
        return out

    r_loss, r_final, r_ffn1, r_mix1, r_mem1, r_ws, r_bs, r_lng, r_lnb = split(red_early.reshape(-1), early)
    r_ffn0, r_lb0, r_onorm, r_mem0, r_mix0 = split(red_late.reshape(-1), late)
    loss = r_loss[0]
    g_mix_norm = jnp.stack([r_mix0, r_mix1])
    g_mem_norm = jnp.stack([r_mem0, r_mem1])
    g_hg_lb = r_lb0[None, :] * lb0 * (jnp.eye(3, dtype=F32)[:, 0:1] - lb_soft)
    g_hg_onorm = r_onorm.reshape(1, D_TOK)
    width = D_TOK // N_DEV
    g_gm_ln_g = lax.dynamic_slice(r_lng, (me * width,), (width,)).reshape(1, width)
    g_gm_ln_b = lax.dynamic_slice(r_lnb, (me * width,), (width,)).reshape(1, width)
    g_gm_ws = r_ws.reshape(gm_ws.shape)
    g_gm_bs = r_bs.reshape(gm_bs.shape)
    g_ffn_norm = jnp.stack([r_ffn0, r_ffn1])
    g_final_norm = r_final

    grads = [g_mix_norm, g_mem_norm, g_shards[0], g_shards[1], g_shards[2], g_hg_lb, g_hg_onorm, g_shards[3],
             g_gm_ln_g, g_gm_ln_b, g_gm_ws, g_gm_bs, g_ffn_norm, g_shards[4], g_shards[5], g_final_norm]
    weights = [mix_norm, mem_norm, w_mem_kv, w_out, hg_w_in, hg_lb, hg_onorm, gm_w_in, gm_ln_g, gm_ln_b, gm_ws, gm_bs,
               ffn_norm, w_ffn_in, w_ffn_out, final_norm]
    ms = [m_mix_norm, m_mem_norm, m_w_mem_kv, m_w_out, m_hg_w_in, m_hg_lb, m_hg_onorm, m_gm_w_in, m_gm_ln_g,
          m_gm_ln_b, m_gm_ws, m_gm_bs, m_ffn_norm, m_w_ffn_in, m_w_ffn_out, m_final_norm]
    vs = [v_mix_norm, v_mem_norm, v_w_mem_kv, v_w_out, v_hg_w_in, v_hg_lb, v_hg_onorm, v_gm_w_in, v_gm_ln_g,
          v_gm_ln_b, v_gm_ws, v_gm_bs, v_ffn_norm, v_w_ffn_in, v_w_ffn_out, v_final_norm]
    deltas, new_m, new_v = [], [], []
    for n, (w, g, m, v) in enumerate(zip(weights, grads, ms, vs)):
        if w.ndim == 1:
            d, nm, nv = _adamw(w[None], g.reshape(1, -1), m[None], v[None], name=f"adamw{n}")
            d, nm, nv = d[0], nm[0], nv[0]
        elif n in transposed:
            flip = lambda t: jnp.swapaxes(t, 1, 2)
            d, nm, nv = (flip(t) for t in _adamw(flip(w), g, flip(m), flip(v), name=f"adamw{n}"))
            grads[n] = flip(g)
        else:
            d, nm, nv = _adamw(w, g.reshape(w.shape), m, v, name=f"adamw{n}")
        deltas.append(d)
        new_m.append(nm)
        new_v.append(nv)
    grads = [g.reshape(w.shape) for g, w in zip(grads, weights)]
    return (loss, grad_x[None], *grads, *deltas, *new_m, *new_v)
```

```python
import jax
import jax.numpy as jnp
from jax import lax
from jax.experimental import pallas as pl
from jax.experimental.pallas import tpu as pltpu

F32 = jnp.float32
BF16 = jnp.bfloat16
MXU_DTYPE = jnp.bfloat16
MESH_ID = pl.DeviceIdType.MESH

N_DEV = 8
EPS = 1e-6
D_MODEL = 1024
D_TOK = 768
D_MEM = 256
N_HEADS = 6
HEAD = 128
MEM_HEADS = 4
MEM_HDIM = 64
GM_CHUNK = 128
ATTN_ROWS = 2048
D_FF = 2816
HG_SUB = 16
HG_GROUP = 6
HG_IN = 4 * D_TOK + D_MEM
GM_IN = 2 * D_TOK + D_MEM
LANE = 128
MXU_COLS = 256

ADAM_LR = 0.001
ADAM_B1 = 0.9
ADAM_B2 = 0.999
ADAM_EPS = 1e-08
ADAM_WD = 0.01
ADAM_STEP = 10

VMEM_LIMIT = 48 * 2 ** 20
VMEM_LIMIT_WIDE = 58 * 2 ** 20


def _params(sem=None, limit=VMEM_LIMIT):
    return pltpu.CompilerParams(dimension_semantics=sem, vmem_limit_bytes=limit)


def _tile(n, cap, q=LANE):
    if n <= cap:
        return n
    best = None
    for t in range(q, cap + 1, q):
        if n % t == 0:
            best = t
    assert best is not None, (n, cap, q)
    return best


def _sigmoid(x):
    return 1.0 / (1.0 + jnp.exp(-x))


def _gelu(x, with_grad=False):
    cdf = 0.5 * (1.0 + lax.erf(x * 0.7071067811865476))
    if not with_grad:
        return x * cdf
    return x * cdf, cdf + x * jnp.exp(-0.5 * x * x) * 0.3989422804014327


def _matmul(a, b, *, name, ta=False, tb=False, res=None, out_dtype=F32, a_halves=False, b_halves=False, dep=None,
            norm_gain=None, norm_bwd=None, loss_head=None):
    if a_halves and ta:
        K, M = a.shape[1], 2 * a.shape[2]
    elif a_halves:
        M, K = a.shape[1], 2 * a.shape[2]
    else:
        K, M = a.shape if ta else a.shape[::-1]
    if b_halves:
        assert not tb and b.shape[1] == K
        N = 2 * b.shape[2]
    else:
        N = b.shape[0] if tb else b.shape[1]
        assert (b.shape[1] if tb else b.shape[0]) == K
    tm = _tile(M // 2 if (a_halves and ta) else M, 1664 if ta else 1024)
    tn = _tile(N // 2 if b_halves else N, 1792)
    tk = _tile(K // 2 if (a_halves and not ta) else K, 1024 if ta else 1664)
    nk = K // tk
    dims = (((0 if ta else 1,), (1 if tb else 0,)), ((), ()))

    strips = norm_bwd is not None or loss_head is not None
    fused = norm_gain is not None or strips
    n_in = 2 + (res is not None) + (norm_gain is not None) + 2 * strips + (dep is not None)
    if fused:
        assert tn == N, "the fused norm needs whole rows"
        assert (norm_gain is not None) + (norm_bwd is not None) + (loss_head is not None) == 1
        assert not strips or (res is not None and nk > 1 and tm % LANE == 0)

    def body(*refs):
        a_ref, b_ref = refs[:2]
        r_ref = refs[2] if res is not None else None
        g_ref = refs[2 + (res is not None)] if fused else None
        x_ref = refs[3 + (res is not None)] if strips else None
        o_ref = refs[n_in]
        h_ref = refs[n_in + 1] if fused else None
        l_ref = refs[n_in + 2] if loss_head is not None else None
        acc = None if nk == 1 else refs[-1]
        k = pl.program_id(2)

        def product():
            return lax.dot_general(a_ref[...].astype(MXU_DTYPE), b_ref[...].astype(MXU_DTYPE), dims,
                                   preferred_element_type=F32)

        def finish(r):
            if loss_head is not None:
                @pl.when(pl.program_id(0) == 0)
                def _():
                    h_ref[...] = jnp.zeros_like(h_ref)
                    l_ref[...] = jnp.zeros_like(l_ref)

                acc[...] = r + r_ref[...]
                gv = g_ref[...]

                def strip(s, carry):
                    dg, loss = carry
                    rows = pl.ds(pl.multiple_of(s * LANE, LANE), LANE)
                    xv = acc[rows, :]
                    scale = lax.rsqrt(jnp.mean(xv * xv, axis=-1, keepdims=True) + EPS)
                    xh = xv * scale
                    err = xh * gv - x_ref[rows, :]
                    loss = loss + 0.5 * jnp.sum(jnp.mean(err * err, axis=-1, keepdims=True), axis=0, keepdims=True)
                    dy = err * (1.0 / N)
                    u = dy * gv
                    o_ref[rows, :] = scale * (u - xh * jnp.mean(u * xh, axis=-1, keepdims=True))
                    return dg + jnp.sum(dy * xh, axis=0, keepdims=True), loss

                dg, loss = lax.fori_loop(0, tm // LANE, strip, (jnp.zeros((1, N), F32), jnp.zeros((1, 1), F32)))
                h_ref[...] += dg
                l_ref[...] += jnp.broadcast_to(loss, l_ref.shape)
                return
            if norm_bwd is not None:
                @pl.when(pl.program_id(0) == 0)
                def _():
                    h_ref[...] = jnp.zeros_like(h_ref)

                acc[...] = r
                gv = g_ref[...]

                def strip(s, dg):
                    rows = pl.ds(pl.multiple_of(s * LANE, LANE), LANE)
                    rv = acc[rows, :]
                    xv = x_ref[rows, :]
                    scale = lax.rsqrt(jnp.mean(xv * xv, axis=-1, keepdims=True) + EPS)
                    xh = xv * scale
                    u = rv * gv
                    o_ref[rows, :] = r_ref[rows, :] + scale * (u - xh * jnp.mean(u * xh, axis=-1, keepdims=True))
                    return dg + jnp.sum(rv * xh, axis=0, keepdims=True)

                h_ref[...] += lax.fori_loop(0, tm // LANE, strip, jnp.zeros((1, N), F32))
                return
            if res is not None:
                r = r + r_ref[...].astype(F32)
            o_ref[...] = r.astype(out_dtype)
            if norm_gain is not None:
                scale = lax.rsqrt(jnp.mean(r * r, axis=-1, keepdims=True) + EPS)
                h_ref[...] = (r * scale * g_ref[...]).astype(h_ref.dtype)

        if nk == 1:
            finish(product())
            return

        @pl.when(k == 0)
        def _():
            acc[...] = product()

        @pl.when((k > 0) & (k < nk - 1))
        def _():
            acc[...] += product()

        @pl.when(k == nk - 1)
        def _():
            finish(acc[...] + product())

    if a_halves and ta:
        mh = M // 2 // tm
        a_spec = pl.BlockSpec((None, tk, tm), lambda i, j, k: (i // mh, k, i % mh))
    elif a_halves:
        kh = nk // 2
        a_spec = pl.BlockSpec((None, tm, tk), lambda i, j, k: (k // kh, i, k % kh))
    elif ta:
        a_spec = pl.BlockSpec((tk, tm), lambda i, j, k: (k, i))
    else:
        a_spec = pl.BlockSpec((tm, tk), lambda i, j, k: (i, k))
    if b_halves:
        nh = N // 2 // tn
        b_spec = pl.BlockSpec((None, tk, tn), lambda i, j, k: (j // nh, k, j % nh))
    elif tb:
        b_spec = pl.BlockSpec((tn, tk), lambda i, j, k: (j, k))
    else:
        b_spec = pl.BlockSpec((tk, tn), lambda i, j, k: (k, j))
    o_spec = pl.BlockSpec((tm, tn), lambda i, j, k: (i, j))
    in_specs = [a_spec, b_spec] + ([o_spec] if res is not None else [])
    args = (a, b) + ((res,) if res is not None else ())
    out_specs, out_shape = o_spec, jax.ShapeDtypeStruct((M, N), out_dtype)
    vec = pl.BlockSpec((1, N), lambda i, j, k: (0, 0))
    sem = ("parallel", "parallel", "arbitrary")
    if norm_gain is not None:
        in_specs.append(vec)
        args += (norm_gain,)
        out_specs, out_shape = [o_spec, o_spec], [out_shape, jax.ShapeDtypeStruct((M, N), BF16)]
    if norm_bwd is not None:
        x_in, gain = norm_bwd
        in_specs += [vec, o_spec]
        args += (gain, x_in)
        out_specs, out_shape = [o_spec, vec], [out_shape, jax.ShapeDtypeStruct((1, N), F32)]
        sem = ("arbitrary", "arbitrary", "arbitrary")
    if loss_head is not None:
        gain, target = loss_head
        in_specs += [vec, o_spec]
        args += (gain, target)
        one = pl.BlockSpec((1, LANE), lambda i, j, k: (0, 0))
        out_specs = [o_spec, vec, one]
        out_shape = [out_shape, jax.ShapeDtypeStruct((1, N), F32), jax.ShapeDtypeStruct((1, LANE), F32)]
        sem = ("arbitrary", "arbitrary", "arbitrary")
    if dep is not None:
        in_specs.append(pl.BlockSpec(memory_space=pl.ANY))
        args += (dep,)
    return pl.pallas_call(
        body, name=name, grid=(M // tm, N // tn, nk), in_specs=in_specs, out_specs=out_specs, out_shape=out_shape,
        scratch_shapes=[] if nk == 1 else [pltpu.VMEM((tm, tn), F32)],
        compiler_params=_params(sem, VMEM_LIMIT_WIDE if strips else VMEM_LIMIT))(*args)


def _ffn_in(hf, wt, *, name):
    S, K = hf.shape
    tm = _tile(S, 512)
    tn = _tile(D_FF, 1408)
    nh = D_FF // tn
    nt = (((1,), (1,)), ((), ()))

    def body(a_ref, bg_ref, bu_ref, gu_ref, act_ref):
        av = a_ref[...].astype(MXU_DTYPE)
        for c0 in range(0, tn, MXU_COLS):
            cs = slice(c0, min(c0 + MXU_COLS, tn))
            gate = lax.dot_general(av, bg_ref[cs, :].astype(MXU_DTYPE), nt, preferred_element_type=F32)
            up = lax.dot_general(av, bu_ref[cs, :].astype(MXU_DTYPE), nt, preferred_element_type=F32)
            gu_ref[0, :, cs] = gate.astype(gu_ref.dtype)
            gu_ref[1, :, cs] = up.astype(gu_ref.dtype)
            act_ref[:, cs] = (gate * _sigmoid(gate) * up).astype(act_ref.dtype)

    return pl.pallas_call(
        body, name=name, grid=(nh, S // tm),
        in_specs=[pl.BlockSpec((tm, K), lambda j, i: (i, 0)), pl.BlockSpec((tn, K), lambda j, i: (j, 0)),
                  pl.BlockSpec((tn, K), lambda j, i: (j + nh, 0))],
        out_specs=[pl.BlockSpec((2, tm, tn), lambda j, i: (0, i, j)), pl.BlockSpec((tm, tn), lambda j, i: (i, j))],
        out_shape=[jax.ShapeDtypeStruct((2, S, D_FF), BF16), jax.ShapeDtypeStruct((S, D_FF), BF16)],
        compiler_params=_params(("parallel", "parallel")))(hf, wt, wt)


def _ffn_out_dx(dx, w, gu, dep, *, name):
    S, K = dx.shape
    tm = _tile(S, 1024)
    tn = _tile(D_FF, 1408)

    def body(a_ref, b_ref, gu_ref, dep_ref, o_ref):
        del dep_ref
        av = a_ref[...].astype(MXU_DTYPE)
        for c0 in range(0, tn, MXU_COLS):
            cs = slice(c0, min(c0 + MXU_COLS, tn))
            da = lax.dot_general(av, b_ref[cs, :].astype(MXU_DTYPE), (((1,), (1,)), ((), ())),
                                 preferred_element_type=F32)
            gate = gu_ref[0, :, cs].astype(F32)
            up = gu_ref[1, :, cs].astype(F32)
            sg = _sigmoid(gate)
            o_ref[0, :, cs] = (da * up * sg * (1.0 + gate * (1.0 - sg))).astype(o_ref.dtype)
            o_ref[1, :, cs] = (da * gate * sg).astype(o_ref.dtype)

    halves = pl.BlockSpec((2, tm, tn), lambda i, j: (0, i, j))
    return pl.pallas_call(
        body, name=name, grid=(S // tm, D_FF // tn),
        in_specs=[pl.BlockSpec((tm, K), lambda i, j: (i, 0)), pl.BlockSpec((tn, K), lambda i, j: (j, 0)), halves,
                  pl.BlockSpec(memory_space=pl.ANY)],
        out_specs=halves, out_shape=jax.ShapeDtypeStruct((2, S, D_FF), BF16),
        compiler_params=_params(("parallel", "parallel")))(dx, w, gu, dep)


def _rms_fwd(x, g, *, name, dep=None):
    R, Dm = x.shape
    tr = _tile(R, 512, 8)

    def body(x_ref, g_ref, *rest):
        o_ref = rest[-1]
        xv = x_ref[...]
        r = lax.rsqrt(jnp.mean(xv * xv, axis=-1, keepdims=True) + EPS)
        o_ref[...] = (xv * r * g_ref[...]).astype(o_ref.dtype)

    in_specs = [pl.BlockSpec((tr, Dm), lambda i: (i, 0)), pl.BlockSpec((1, Dm), lambda i: (0, 0))]
    args = (x, g)
    if dep is not None:
        in_specs.append(pl.BlockSpec(memory_space=pl.ANY))
        args += (dep,)
    return pl.pallas_call(
        body, name=name, grid=(R // tr,), in_specs=in_specs,
        out_specs=pl.BlockSpec((tr, Dm), lambda i: (i, 0)), out_shape=jax.ShapeDtypeStruct((R, Dm), BF16),
        compiler_params=_params(("parallel",)))(*args)


def _rms_bwd(x, g, dh, dres, *, name):
    R, Dm = x.shape
    tr = _tile(R, 256, 8)

    def body(x_ref, g_ref, dh_ref, dres_ref, dx_ref, dg_ref):
        @pl.when(pl.program_id(0) == 0)
        def _():
            dg_ref[...] = jnp.zeros_like(dg_ref)

        xv = x_ref[...]
        r = lax.rsqrt(jnp.mean(xv * xv, axis=-1, keepdims=True) + EPS)
        xh = xv * r
        dhv = dh_ref[...].astype(F32)
        dg_ref[...] += jnp.sum(dhv * xh, axis=0, keepdims=True)
        u = dhv * g_ref[...]
        dx = r * (u - xh * jnp.mean(u * xh, axis=-1, keepdims=True))
        dx_ref[...] = dres_ref[...] + dx

    row = pl.BlockSpec((tr, Dm), lambda i: (i, 0))
    vec = pl.BlockSpec((1, Dm), lambda i: (0, 0))
    return pl.pallas_call(
        body, name=name, grid=(R // tr,), in_specs=[row, vec, row, row], out_specs=[row, vec],
        out_shape=[jax.ShapeDtypeStruct((R, Dm), F32), jax.ShapeDtypeStruct((1, Dm), F32)],
        compiler_params=_params(("arbitrary",)))(x, g, dh, dres)


def _head_mask(h):
    lane = lax.broadcasted_iota(jnp.int32, (1, D_MEM), 1)
    return (lane >= h * MEM_HDIM) & (lane < (h + 1) * MEM_HDIM)


def _attn_probs(qv, k_mx, mask):
    s = lax.dot_general(jnp.where(mask, qv, 0.0).astype(MXU_DTYPE), k_mx, (((1,), (1,)), ((), ())),
                        preferred_element_type=F32) * (MEM_HDIM ** -0.5)
    e = jnp.exp(s - jnp.max(s, axis=-1, keepdims=True))
    return e / jnp.sum(e, axis=-1, keepdims=True)


def _attn_fwd(p, qcol, kv, heads, *, name):
    S = p.shape[0]
    M = kv.shape[0]
    ts = _tile(S, ATTN_ROWS, 8)

    def body(q_ref, k_ref, v_ref, heads_in, o_ref):
        del heads_in
        qv = q_ref[...]
        kx = k_ref[...].astype(MXU_DTYPE)
        vv = v_ref[...]
        out = jnp.zeros((ts, D_MEM), F32)
        for h in range(MEM_HEADS):
            mask = _head_mask(h)
            pr = _attn_probs(qv, kx, mask)
            out = out + jnp.dot(pr.astype(MXU_DTYPE), jnp.where(mask, vv, 0.0).astype(MXU_DTYPE),
                                preferred_element_type=F32)
        o_ref[...] = out.astype(o_ref.dtype)

    return pl.pallas_call(
        body, name=name, grid=(S // ts,),
        in_specs=[pl.BlockSpec((ts, D_MEM), lambda i: (i, qcol)), pl.BlockSpec((M, D_MEM), lambda i: (0, 0)),
                  pl.BlockSpec((M, D_MEM), lambda i: (0, 1)), pl.BlockSpec(memory_space=pl.ANY)],
        out_specs=pl.BlockSpec((ts, D_MEM), lambda i: (i, D_TOK // D_MEM)),
        out_shape=jax.ShapeDtypeStruct(heads.shape, heads.dtype), input_output_aliases={3: 0},
        compiler_params=_params(("parallel",)))(p, kv, kv, heads)


def _attn_bwd(p, qcol, kv, dheads, dp, *, name):
    S = p.shape[0]
    M = kv.shape[0]
    ts = _tile(S, ATTN_ROWS, 8)
    scale = MEM_HDIM ** -0.5

    def body(q_ref, k_ref, v_ref, do_ref, dp_in, dq_ref, dk_ref, dv_ref):
        del dp_in

        @pl.when(pl.program_id(0) == 0)
        def _():
            dk_ref[...] = jnp.zeros_like(dk_ref)
            dv_ref[...] = jnp.zeros_like(dv_ref)

        qv = q_ref[...]
        kv_ = k_ref[...]
        kx = kv_.astype(MXU_DTYPE)
        vv = v_ref[...]
        dox = do_ref[...].astype(MXU_DTYPE)
        qx = qv.astype(MXU_DTYPE)
        dq = jnp.zeros((ts, D_MEM), F32)
        for h in range(MEM_HEADS):
            mask = _head_mask(h)
            pr = _attn_probs(qv, kx, mask)
            vh = jnp.where(mask, vv, 0.0).astype(MXU_DTYPE)
            dpr = lax.dot_general(dox, vh, (((1,), (1,)), ((), ())), preferred_element_type=F32)
            ds = (pr * (dpr - jnp.sum(dpr * pr, axis=-1, keepdims=True)) * scale).astype(MXU_DTYPE)
            dq = dq + jnp.dot(ds, jnp.where(mask, kv_, 0.0).astype(MXU_DTYPE), preferred_element_type=F32)
            dk_h = lax.dot_general(ds, qx, (((0,), (0,)), ((), ())), preferred_element_type=F32)
            dv_h = lax.dot_general(pr.astype(MXU_DTYPE), dox, (((0,), (0,)), ((), ())), preferred_element_type=F32)
            dk_ref[...] += jnp.where(mask, dk_h, 0.0)
            dv_ref[...] += jnp.where(mask, dv_h, 0.0)
        dq_ref[...] = dq.astype(dq_ref.dtype)

    return pl.pallas_call(
        body, name=name, grid=(S // ts,),
        in_specs=[pl.BlockSpec((ts, D_MEM), lambda i: (i, qcol)), pl.BlockSpec((M, D_MEM), lambda i: (0, 0)),
                  pl.BlockSpec((M, D_MEM), lambda i: (0, 1)),
                  pl.BlockSpec((ts, D_MEM), lambda i: (i, D_TOK // D_MEM)), pl.BlockSpec(memory_space=pl.ANY)],
        out_specs=[pl.BlockSpec((ts, D_MEM), lambda i: (i, qcol)), pl.BlockSpec((M, D_MEM), lambda i: (0, 0)),
                   pl.BlockSpec((M, D_MEM), lambda i: (0, 0))],
        out_shape=[jax.ShapeDtypeStruct(dp.shape, dp.dtype), jax.ShapeDtypeStruct((M, D_MEM), F32),
                   jax.ShapeDtypeStruct((M, D_MEM), F32)],
        input_output_aliases={4: 0}, compiler_params=_params(("arbitrary",)))(p, kv, kv, dheads, dp)


def _gm_forward_parts(u_ref, v_ref, lng_ref, lnb_ref, w_ref, bsb_ref, with_grad=False):
    if with_grad:
        (zu, du_gelu), (zv, dv_gelu) = _gelu(u_ref[...], True), _gelu(v_ref[...], True)
    else:
        zu, zv, du_gelu, dv_gelu = _gelu(u_ref[...]), _gelu(v_ref[...]), None, None
    mu = jnp.mean(zv, axis=-1, keepdims=True)
    cen = zv - mu
    rs = lax.rsqrt(jnp.mean(cen * cen, axis=-1, keepdims=True) + EPS)
    vh = cen * rs
    vn = vh * lng_ref[...] + lnb_ref[...]
    row = lax.broadcasted_iota(jnp.int32, (GM_CHUNK, GM_CHUNK), 0)
    col = lax.broadcasted_iota(jnp.int32, (GM_CHUNK, GM_CHUNK), 1)
    tril = row >= col
    wm = [jnp.where(tril, w_ref[g], 0.0).astype(MXU_DTYPE) for g in range(N_HEADS)]
    vnx = [vn[:, g * HEAD:(g + 1) * HEAD].astype(MXU_DTYPE) for g in range(N_HEADS)]
    sv = [jnp.dot(wm[g], vnx[g], preferred_element_type=F32) + bsb_ref[g] for g in range(N_HEADS)]
    return zu, vh, rs, wm, vnx, sv, tril, du_gelu, dv_gelu


def _gmlp_fwd(p, lng, lnb, ws, bsb, *, name):
    S = p.shape[0]

    def body(u_ref, v_ref, lng_ref, lnb_ref, w_ref, bsb_ref, o_ref):
        zu, _, _, _, _, sv, _, _, _ = _gm_forward_parts(u_ref, v_ref, lng_ref, lnb_ref, w_ref, bsb_ref)
        for g in range(N_HEADS):
            o_ref[:, g * HEAD:(g + 1) * HEAD] = (zu[:, g * HEAD:(g + 1) * HEAD] * sv[g]).astype(o_ref.dtype)

    blk = lambda c: pl.BlockSpec((GM_CHUNK, D_TOK), lambda i: (i, c))
    vec = pl.BlockSpec((1, D_TOK), lambda i: (0, 0))
    cube = pl.BlockSpec((N_HEADS, GM_CHUNK, GM_CHUNK), lambda i: (0, 0, 0))
    return pl.pallas_call(
        body, name=name, grid=(S // GM_CHUNK,), in_specs=[blk(0), blk(1), vec, vec, cube, cube],
        out_specs=blk(0), out_shape=jax.ShapeDtypeStruct((S, D_MODEL), BF16),
        compiler_params=_params(("parallel",)))(p, p, lng, lnb, ws, bsb)


def _gmlp_bwd(p, lng, lnb, ws, bsb, dheads, *, name):
    S = p.shape[0]

    def body(u_ref, v_ref, lng_ref, lnb_ref, w_ref, bsb_ref, dt_ref, dp_ref, dw_ref, dbs_ref, dlg_ref, dlb_ref):
        @pl.when(pl.program_id(0) == 0)
        def _():
            dw_ref[...] = jnp.zeros_like(dw_ref)
            dbs_ref[...] = jnp.zeros_like(dbs_ref)
            dlg_ref[...] = jnp.zeros_like(dlg_ref)
            dlb_ref[...] = jnp.zeros_like(dlb_ref)

        zu, vh, rs, wm, vnx, sv, tril, du_gelu, dv_gelu = _gm_forward_parts(
            u_ref, v_ref, lng_ref, lnb_ref, w_ref, bsb_ref, with_grad=True)
        dt = dt_ref[...].astype(F32)
        dvn_parts = []
        for g in range(N_HEADS):
            sl = slice(g * HEAD, (g + 1) * HEAD)
            dsv = dt[:, sl] * zu[:, sl]
            dp_ref[:, sl] = (dt[:, sl] * sv[g] * du_gelu[:, sl]).astype(dp_ref.dtype)
            dsx = dsv.astype(MXU_DTYPE)
            dw = lax.dot_general(dsx, vnx[g], (((1,), (1,)), ((), ())), preferred_element_type=F32)
            dw_ref[g] += jnp.where(tril, dw, 0.0)
            dbs_ref[g] += jnp.sum(dsv, axis=-1, keepdims=True)
            dvn_parts.append(lax.dot_general(wm[g], dsx, (((0,), (0,)), ((), ())), preferred_element_type=F32))
        dvn = jnp.concatenate(dvn_parts, axis=-1)
        dlg_ref[...] += jnp.sum(dvn * vh, axis=0, keepdims=True)
        dlb_ref[...] += jnp.sum(dvn, axis=0, keepdims=True)
        dvh = dvn * lng_ref[...]
        dzv = rs * (dvh - jnp.mean(dvh, axis=-1, keepdims=True) - vh * jnp.mean(dvh * vh, axis=-1, keepdims=True))
        dp_ref[:, D_TOK:] = (dzv * dv_gelu).astype(dp_ref.dtype)

    blk = lambda c: pl.BlockSpec((GM_CHUNK, D_TOK), lambda i: (i, c))
    vec = pl.BlockSpec((1, D_TOK), lambda i: (0, 0))
    cube = pl.BlockSpec((N_HEADS, GM_CHUNK, GM_CHUNK), lambda i: (0, 0, 0))
    col = pl.BlockSpec((N_HEADS, GM_CHUNK, 1), lambda i: (0, 0, 0))
    return pl.pallas_call(
        body, name=name, grid=(S // GM_CHUNK,), in_specs=[blk(0), blk(1), vec, vec, cube, cube, blk(0)],
        out_specs=[pl.BlockSpec((GM_CHUNK, 2 * D_TOK), lambda i: (i, 0)), cube, col, vec, vec],
        out_shape=[jax.ShapeDtypeStruct((S, GM_IN), BF16), jax.ShapeDtypeStruct((N_HEADS, GM_CHUNK, GM_CHUNK), F32),
                   jax.ShapeDtypeStruct((N_HEADS, GM_CHUNK, 1), F32), jax.ShapeDtypeStruct((1, D_TOK), F32),
                   jax.ShapeDtypeStruct((1, D_TOK), F32)],
        compiler_params=_params(("arbitrary",)))(p, p, lng, lnb, ws, bsb, dheads)


def _chunk_tri(n, chunk, upper):
    r = lax.broadcasted_iota(jnp.int32, (n, n), 0)
    c = lax.broadcasted_iota(jnp.int32, (n, n), 1)
    same = (r // chunk) == (c // chunk)
    return jnp.where(same & ((r <= c) if upper else (r >= c)), 1.0, 0.0).astype(F32)


def _running_sum(tri, x):
    hi = x.astype(BF16)
    rest = x - hi.astype(F32)
    mid = rest.astype(BF16)
    lo = (rest - mid.astype(F32)).astype(BF16)
    tri = tri.astype(BF16)
    return (jnp.dot(tri, hi, preferred_element_type=F32) + jnp.dot(tri, mid, preferred_element_type=F32)
            + jnp.dot(tri, lo, preferred_element_type=F32))


MASKED = -1e30


def _pair_masks(mask_ref, n, upper):
    row = lax.broadcasted_iota(jnp.int32, (n, HEAD), 0)
    for i in range(n):
        mask_ref[i] = jnp.where((row <= i) if upper else (row >= i), 0.0, MASKED).astype(F32)


def _hg_gates(fz, lb):
    sg = _sigmoid(fz)
    f = lb + (1.0 - lb) * sg
    kk = (1.0 - lb) * (1.0 - sg)
    return sg, f, jnp.log(f), kk


def _hgrn2_fwd(p, lb, onorm, *, name):
    S = p.shape[0]
    C = HG_SUB
    tb = _tile(S, 256, C)
    nsub = tb // C

    def body(q_ref, fz_ref, v_ref, g_ref, lb_ref, on_ref, tok_ref, o_ref, st_ref, state, b_blk, k_blk, bsc, ksc, vsc,
             xsc):
        @pl.when(pl.program_id(0) == 0)
        def _():
            state[...] = jnp.zeros_like(state)

        _, _, lg, kk = _hg_gates(fz_ref[...], lb_ref[...])
        b_blk[...] = _running_sum(_chunk_tri(tb, C, False), lg)
        k_blk[...] = kk
        tt = lax.broadcasted_iota(jnp.int32, (C, HEAD), 0)
        ones = jnp.ones((HEAD, HEAD), MXU_DTYPE)

        def sub(c, carry):
            rows = pl.ds(pl.multiple_of(c * C, C), C)
            for h in range(N_HEADS):
                cols = slice(h * HEAD, (h + 1) * HEAD)
                qv = q_ref[rows, cols]
                vv = v_ref[rows, cols]
                b = b_blk[rows, cols]
                kk = k_blk[rows, cols]
                st0 = state[h]
                st0x = st0.astype(MXU_DTYPE)
                st_ref[c, h] = st0x.astype(st_ref.dtype)
                inter = lax.dot_general((qv * jnp.exp(b)).astype(MXU_DTYPE), st0x,
                                        (((1,), (1,)), ((), ())), preferred_element_type=F32)
                bsc[h] = b
                ksc[h] = kk
                vsc[h] = vv
                for s in range(C):
                    dec = jnp.where(tt >= s, jnp.exp(b - bsc[h, pl.ds(s, 1), :]), 0.0)
                    xsc[h, s * C:(s + 1) * C, :] = (qv * ksc[h, pl.ds(s, 1), :] * dec).astype(xsc.dtype)
                sums = jnp.dot(xsc[h], ones, preferred_element_type=F32)
                intra = jnp.zeros((C, HEAD), F32)
                for s in range(C):
                    intra = intra + sums[s * C:(s + 1) * C, :] * vsc[h, pl.ds(s, 1), :]
                o_ref[rows, cols] = inter + intra
                b_last = bsc[h, pl.ds(C - 1, 1), :]
                ke = kk * jnp.exp(b_last - b)
                state[h] = st0 * jnp.exp(b_last) + lax.dot_general(
                    vv.astype(MXU_DTYPE), ke.astype(MXU_DTYPE), (((0,), (0,)), ((), ())),
                    preferred_element_type=F32)
            return carry

        lax.fori_loop(0, nsub, sub, 0, unroll=2)

        for h in range(N_HEADS):
            cols = slice(h * HEAD, (h + 1) * HEAD)
            o = o_ref[:, cols]
            gv = g_ref[:, cols]
            n = o * lax.rsqrt(jnp.mean(o * o, axis=-1, keepdims=True) + EPS)
            tok_ref[:, cols] = (n * (gv * _sigmoid(gv)) * on_ref[:, cols]).astype(tok_ref.dtype)

    blk = lambda c: pl.BlockSpec((tb, D_TOK), lambda i, c=c: (i, c))
    vec = pl.BlockSpec((1, D_TOK), lambda i: (0, 0))
    stb = pl.BlockSpec((nsub, N_HEADS, HEAD, HEAD), lambda i: (i, 0, 0, 0))
    return pl.pallas_call(
        body, name=name, grid=(S // tb,), in_specs=[blk(0), blk(1), blk(2), blk(3), vec, vec],
        out_specs=[blk(0), blk(0), stb],
        out_shape=[jax.ShapeDtypeStruct((S, D_MODEL), BF16), jax.ShapeDtypeStruct((S, D_TOK), F32),
                   jax.ShapeDtypeStruct((S // C, N_HEADS, HEAD, HEAD), BF16)],
        scratch_shapes=[pltpu.VMEM((N_HEADS, HEAD, HEAD), F32)] + [pltpu.VMEM((tb, D_TOK), F32)] * 2
        + [pltpu.VMEM((N_HEADS, C, HEAD), F32)] * 3 + [pltpu.VMEM((N_HEADS, C * C, HEAD), MXU_DTYPE)],
        compiler_params=_params(("arbitrary",)))(p, p, p, p, lb, onorm)


def _hgrn2_bwd(p, lb, onorm, o, states, dheads, *, name):
    S = p.shape[0]
    C = HG_SUB
    tb = _tile(S, 256, C)
    nsub = tb // C
    nblk = S // tb

    def body(q_ref, fz_ref, v_ref, g_ref, lb_ref, on_ref, o_ref, st_ref, dt_ref, dp_ref, dlb_ref, don_ref, dstate,
             b_blk, k_blk, do_blk, db_blk, dk_blk, dq_blk, dv_blk, bsc, ksc, vsc, qsc, dosc, causal, anti):
        @pl.when(pl.program_id(0) == 0)
        def _():
            dstate[...] = jnp.zeros_like(dstate)
            dlb_ref[...] = jnp.zeros_like(dlb_ref)
            don_ref[...] = jnp.zeros_like(don_ref)

        for h in range(N_HEADS):
            cols = slice(h * HEAD, (h + 1) * HEAD)
            onv = on_ref[:, cols]
            gv = g_ref[:, cols]
            ov = o_ref[:, cols]
            dt = dt_ref[:, cols].astype(F32)
            sgg = _sigmoid(gv)
            sil = gv * sgg
            rinv = lax.rsqrt(jnp.mean(ov * ov, axis=-1, keepdims=True) + EPS)
            n = ov * rinv
            don_ref[:, cols] += jnp.sum(dt * n * sil, axis=0, keepdims=True)
            dn = dt * sil * onv
            dp_ref[:, 3 * D_TOK + h * HEAD:3 * D_TOK + (h + 1) * HEAD] = (
                dt * n * onv * sgg * (1.0 + gv * (1.0 - sgg))).astype(dp_ref.dtype)
            do_blk[:, cols] = rinv * (dn - n * jnp.mean(dn * n, axis=-1, keepdims=True))
        _, _, lg, kk = _hg_gates(fz_ref[...], lb_ref[...])
        b_blk[...] = _running_sum(_chunk_tri(tb, C, False), lg)
        k_blk[...] = kk
        _pair_masks(causal, C, False)
        _pair_masks(anti, C, True)
        tt = lax.broadcasted_iota(jnp.int32, (C, HEAD), 0)

        def sub(j, heads):
            c = nsub - 1 - j
            rows = pl.ds(pl.multiple_of(c * C, C), C)
            for h in heads:
                cols = slice(h * HEAD, (h + 1) * HEAD)
                qv = q_ref[rows, cols]
                vv = v_ref[rows, cols]
                do = do_blk[rows, cols]
                b = b_blk[rows, cols]
                kk = k_blk[rows, cols]
                bsc[h] = b
                ksc[h] = kk
                vsc[h] = vv
                qsc[h] = qv
                dosc[h] = do
                b_last = bsc[h, pl.ds(C - 1, 1), :]
                eb = jnp.exp(b)
                qe = qv * eb
                ebb = jnp.exp(b_last - b)
                ke = kk * ebb
                e_last = jnp.exp(b_last)
                st0x = st_ref[c, h].astype(MXU_DTYPE)
                st0 = st0x.astype(F32)
                dst1 = dstate[h]
                dst1x = dst1.astype(MXU_DTYPE)
                dox = do.astype(MXU_DTYPE)
                dqe = jnp.dot(dox, st0x, preferred_element_type=F32)
                dke = jnp.dot(vv.astype(MXU_DTYPE), dst1x, preferred_element_type=F32)
                dv = lax.dot_general(ke.astype(MXU_DTYPE), dst1x, (((1,), (1,)), ((), ())),
                                     preferred_element_type=F32)
                db_last = (e_last * jnp.sum(st0 * dst1, axis=0, keepdims=True)
                           + jnp.sum(dke * ke, axis=0, keepdims=True))
                dstate[h] = dst1 * e_last + lax.dot_general(dox, qe.astype(MXU_DTYPE), (((0,), (0,)), ((), ())),
                                                            preferred_element_type=F32)
                dq_pairs = jnp.zeros((C, HEAD), F32)
                for s in range(C):
                    dec = jnp.exp(b - bsc[h, pl.ds(s, 1), :] + causal[s])
                    da_s = jnp.sum(do * vsc[h, pl.ds(s, 1), :], axis=-1, keepdims=True)
                    dq_pairs = dq_pairs + da_s * (ksc[h, pl.ds(s, 1), :] * dec)
                dk_pairs = jnp.zeros((C, HEAD), F32)
                for t in range(C):
                    do_t = dosc[h, pl.ds(t, 1), :]
                    qd = qsc[h, pl.ds(t, 1), :] * jnp.exp(bsc[h, pl.ds(t, 1), :] - b + anti[t])
                    da_t = jnp.sum(vv * do_t, axis=-1, keepdims=True)
                    dk_pairs = dk_pairs + da_t * qd
                    a_t = jnp.sum(qd * kk, axis=-1, keepdims=True)
                    dv = dv + a_t * do_t
                db = dqe * qe - dke * ke + qv * dq_pairs - kk * dk_pairs
                db_blk[rows, cols] = db + jnp.where(tt == C - 1, db_last, 0.0)
                dk_blk[rows, cols] = dke * ebb + dk_pairs
                dq_blk[rows, cols] = dqe * eb + dq_pairs
                dv_blk[rows, cols] = dv

        for first in range(0, N_HEADS, HG_GROUP):
            heads = tuple(range(first, first + HG_GROUP))
            pl.loop(0, nsub)(lambda j, heads=heads: sub(j, heads))

        dlg = _running_sum(_chunk_tri(tb, C, True), db_blk[...])
        lbv = lb_ref[...]
        sg, f, _, _ = _hg_gates(fz_ref[...], lbv)
        w = dlg / f - dk_blk[...]
        dp_ref[:, 0:D_TOK] = dq_blk[...].astype(dp_ref.dtype)
        dp_ref[:, 2 * D_TOK:3 * D_TOK] = dv_blk[...].astype(dp_ref.dtype)
        dp_ref[:, D_TOK:2 * D_TOK] = (w * (1.0 - lbv) * sg * (1.0 - sg)).astype(dp_ref.dtype)
        dlb_ref[...] += jnp.sum(w * (1.0 - sg), axis=0, keepdims=True)

    blk = lambda c: pl.BlockSpec((tb, D_TOK), lambda i, c=c: (nblk - 1 - i, c))
    vec = pl.BlockSpec((1, D_TOK), lambda i: (0, 0))
    stb = pl.BlockSpec((nsub, N_HEADS, HEAD, HEAD), lambda i: (nblk - 1 - i, 0, 0, 0))
    small = jax.ShapeDtypeStruct((1, D_TOK), F32)
    return pl.pallas_call(
        body, name=name, grid=(nblk,), in_specs=[blk(0), blk(1), blk(2), blk(3), vec, vec, blk(0), stb, blk(0)],
        out_specs=[pl.BlockSpec((tb, 4 * D_TOK), lambda i: (nblk - 1 - i, 0)), vec, vec],
        out_shape=[jax.ShapeDtypeStruct((S, HG_IN), BF16), small, small],
        scratch_shapes=[pltpu.VMEM((N_HEADS, HEAD, HEAD), F32)] + [pltpu.VMEM((tb, D_TOK), F32)] * 7
        + [pltpu.VMEM((N_HEADS, C, HEAD), F32)] * 5 + [pltpu.VMEM((C, C, HEAD), F32)] * 2,
        compiler_params=_params(("arbitrary",)))(p, p, p, p, lb, onorm, o, states, dheads)


def _adamw(w, g, m, v, *, name):
    shape = w.shape
    cols = shape[-1]
    w2, g2, m2, v2 = (t.reshape(-1, cols) for t in (w, g, m, v))
    R = w2.shape[0]
    tr = _tile(R, 512, 8)

    def body(w_ref, g_ref, m_ref, v_ref, d_ref, nm_ref, nv_ref):
        gv = g_ref[...]
        nm = ADAM_B1 * m_ref[...] + (1.0 - ADAM_B1) * gv
        nv = ADAM_B2 * v_ref[...] + (1.0 - ADAM_B2) * (gv * gv)
        m_hat = nm / (1.0 - ADAM_B1 ** ADAM_STEP)
        v_hat = nv / (1.0 - ADAM_B2 ** ADAM_STEP)
        d_ref[...] = -ADAM_LR * (m_hat / (jnp.sqrt(v_hat) + ADAM_EPS) + ADAM_WD * w_ref[...])
        nm_ref[...] = nm
        nv_ref[...] = nv

    spec = pl.BlockSpec((tr, cols), lambda i: (i, 0))
    out = jax.ShapeDtypeStruct((R, cols), F32)
    d, nm, nv = pl.pallas_call(body, name=name, grid=(R // tr,), in_specs=[spec] * 4, out_specs=[spec] * 3,
                               out_shape=[out] * 3, compiler_params=_params(("parallel",)))(w2, g2, m2, v2)
    return d.reshape(shape), nm.reshape(shape), nv.reshape(shape)


def _add_received(sent, got, me, *, name):
    n, R, Cc = got.shape
    tr = _tile(R, 256, 16)
    per = R // tr

    def body(me_ref, a_ref, b_ref, o_ref):
        del me_ref
        acc = a_ref[...].astype(F32)
        for k in range(n):
            acc = acc + b_ref[k].astype(F32)
        o_ref[...] = acc

    grid_spec = pltpu.PrefetchScalarGridSpec(
        num_scalar_prefetch=1, grid=(per,),
        in_specs=[pl.BlockSpec((tr, Cc), lambda i, me_ref: (me_ref[0] * per + i, 0)),
                  pl.BlockSpec((n, tr, Cc), lambda i, me_ref: (0, i, 0))],
        out_specs=pl.BlockSpec((tr, Cc), lambda i, me_ref: (i, 0)))
    return pl.pallas_call(body, name=name, grid_spec=grid_spec, out_shape=jax.ShapeDtypeStruct((R, Cc), F32),
                          compiler_params=_params(("parallel",)))(jnp.reshape(me, (1,)).astype(jnp.int32), sent, got)


def _sum_blocks(x, *, name):
    n, R, Cc = x.shape
    tr = _tile(R, 208, 8)

    def body(x_ref, o_ref):
        acc = x_ref[0]
        for k in range(1, n):
            acc = acc + x_ref[k]
        o_ref[...] = acc

    return pl.pallas_call(
        body, name=name, grid=(R // tr,), in_specs=[pl.BlockSpec((n, tr, Cc), lambda i: (0, i, 0))],
        out_specs=pl.BlockSpec((tr, Cc), lambda i: (i, 0)), out_shape=jax.ShapeDtypeStruct((R, Cc), F32),
        compiler_params=_params(("parallel",)))(x)


def _place():
    return lax.axis_index("x"), lax.axis_index("y"), lax.axis_index("c")


def _all_gather(x, *, name, in_vmem, reduce_sum=False, with_token=False):
    R, Cc = x.shape
    space = pltpu.VMEM if in_vmem else pl.ANY

    def body(x_ref, out_ref, *scratch):
        if with_token:
            scratch[0][...] = jnp.zeros_like(scratch[0])
            scratch = scratch[1:]
        if reduce_sum:
            gat_ref, send_sems, recv_sems, local_sem = scratch
        else:
            gat_ref = out_ref
            send_sems, recv_sems, local_sem = scratch
        mx, my, mc = _place()
        me, sibling = (mx, my, mc), (mx, my, 1 - mc)
        chips = [(1 - mx, my), (mx, 1 - my), (1 - mx, 1 - my)]

        def rows(px, py, pc):
            return gat_ref.at[pl.ds((4 * px + 2 * py + pc) * R, R), :]

        def copy(k, block, to, src=None):
            return pltpu.make_async_remote_copy(
                src_ref=rows(*block) if src is None else src, dst_ref=rows(*block), send_sem=send_sems.at[k],
                recv_sem=recv_sems.at[k], device_id=to, device_id_type=MESH_ID)

        mine = pltpu.make_async_copy(x_ref, rows(*me), local_sem)
        mine.start()
        first = [copy(0, me, sibling, src=x_ref)]
        first += [copy(1 + j, me, (*chip, mc), src=x_ref) for j, chip in enumerate(chips)]
        for cp in first:
            cp.start()
        passed = [copy(4 + j, (*chip, mc), sibling) for j, chip in enumerate(chips)]
        for j, chip in enumerate(chips):
            copy(1 + j, (*chip, mc), me).wait_recv()
            passed[j].start()
        copy(0, sibling, me).wait_recv()
        for j, chip in enumerate(chips):
            copy(4 + j, (*chip, 1 - mc), me).wait_recv()
        for cp in first + passed:
            cp.wait_send()
        mine.wait()
        if reduce_sum:
            acc = gat_ref[pl.ds(0, R), :]
            for d in range(1, N_DEV):
                acc = acc + gat_ref[pl.ds(d * R, R), :]
            out_ref[...] = acc

    sems = [pltpu.SemaphoreType.DMA((7,)), pltpu.SemaphoreType.DMA((7,)), pltpu.SemaphoreType.DMA]
    if reduce_sum:
        assert in_vmem
        out_shape = jax.ShapeDtypeStruct((R, Cc), x.dtype)
        scratch = [pltpu.VMEM((N_DEV * R, Cc), x.dtype)] + sems
    else:
        out_shape = jax.ShapeDtypeStruct((N_DEV * R, Cc), x.dtype)
        scratch = sems
    out_specs = pl.BlockSpec(memory_space=space)
    if with_token:
        out_shape = (out_shape, jax.ShapeDtypeStruct((8, LANE), F32))
        out_specs = (out_specs, pl.BlockSpec(memory_space=pltpu.VMEM))
    return pl.pallas_call(
        body, name=name, out_shape=out_shape, in_specs=[pl.BlockSpec(memory_space=space)], out_specs=out_specs,
        scratch_shapes=scratch, compiler_params=pltpu.CompilerParams(vmem_limit_bytes=VMEM_LIMIT))(x)


def _peer(k, mx, my, mc):
    bits = k + 1
    return (1 - mx if bits & 4 else mx, 1 - my if bits & 2 else my, 1 - mc if bits & 1 else mc)


HBM_SPEC = pl.BlockSpec(memory_space=pltpu.HBM)
SEM_SPEC = pl.BlockSpec(memory_space=pltpu.SEMAPHORE)
DATAFLOW = pltpu.SideEffectType.DATAFLOW_SIDE_EFFECTING


def _exchange_copies(x_refs, land_refs, send_sems, recv_sems, scatter):
    mx, my, mc = _place()
    me = 4 * mx + 2 * my + mc
    n = len(x_refs)
    copies = []
    for k in range(N_DEV - 1):
        px, py, pc = _peer(k, mx, my, mc)
        for m, (x_ref, land_ref) in enumerate(zip(x_refs, land_refs)):
            rows = land_ref.shape[1] if scatter else x_ref.shape[0]
            if scatter:
                src = x_ref.at[pl.ds(pl.multiple_of((4 * px + 2 * py + pc) * rows, 16), rows), :]
                dst = land_ref.at[k]
            else:
                src = x_ref
                dst = land_ref.at[pl.ds(pl.multiple_of(me * rows, 16), rows), :]
            copies.append(pltpu.make_async_remote_copy(
                src_ref=src, dst_ref=dst, send_sem=send_sems.at[k * n + m], recv_sem=recv_sems.at[k * n + m],
                device_id=(px, py, pc), device_id_type=MESH_ID))
    return copies


def _land_shape(x, scatter):
    return (N_DEV - 1, x.shape[0] // N_DEV, x.shape[1]) if scatter else (N_DEV * x.shape[0], x.shape[1])


def _own_copies(x_refs, land_refs, local_sems):
    mx, my, mc = _place()
    me = 4 * mx + 2 * my + mc
    return [pltpu.make_async_copy(
        x_ref, land_ref.at[pl.ds(pl.multiple_of(me * x_ref.shape[0], 16), x_ref.shape[0]), :], local_sems.at[m])
        for m, (x_ref, land_ref) in enumerate(zip(x_refs, land_refs))]


def _exchange_start(groups, *, name, scatter):
    sizes = [len(g) for g in groups]
    xs = [x for g in groups for x in g]
    n = len(xs)
    lands = [lax.empty(_land_shape(x, scatter), x.dtype) for x in xs]
    per = 2 if scatter else 3

    def body(*refs):
        sems = refs[2 * n:2 * n + per * len(groups)]
        token = refs[-1]
        off = 0
        for gi, m in enumerate(sizes):
            x_refs, land_refs = refs[off:off + m], refs[n + off:n + off + m]
            for cp in _exchange_copies(x_refs, land_refs, sems[per * gi], sems[per * gi + 1], scatter):
                cp.start()
            if not scatter:
                for cp in _own_copies(x_refs, land_refs, sems[per * gi + 2]):
                    cp.start()
            off += m
        token[...] = jnp.zeros_like(token)

    sem_shapes = []
    for m in sizes:
        sem_shapes += [pltpu.SemaphoreType.DMA(((N_DEV - 1) * m,))] * 2
        if not scatter:
            sem_shapes.append(pltpu.SemaphoreType.DMA((m,)))
    ns = len(sem_shapes)
    out = pl.pallas_call(
        body, name=name,
        out_shape=(*sem_shapes, *[pltpu.HBM(x.shape, x.dtype) for x in xs],
                   *[pltpu.HBM(l.shape, l.dtype) for l in lands], jax.ShapeDtypeStruct((8, LANE), F32)),
        in_specs=(HBM_SPEC,) * (2 * n),
        out_specs=(SEM_SPEC,) * ns + (HBM_SPEC,) * (2 * n) + (pl.BlockSpec(memory_space=pltpu.VMEM),),
        input_output_aliases={i: ns + i for i in range(2 * n)},
        compiler_params=pltpu.CompilerParams(has_side_effects=DATAFLOW))(
            *[pltpu.with_memory_space_constraint(t, pltpu.HBM) for t in xs + lands])
    started, off = [], 0
    for gi, m in enumerate(sizes):
        sems = out[per * gi:per * gi + per]
        started.append((sems[0], sems[1], list(out[ns + off:ns + off + m]),
                        list(out[ns + n + off:ns + n + off + m]), out[-1], None if scatter else sems[2]))
        off += m
    return started


def _exchange_wait(started, after, *, name, scatter):
    send_sems, recv_sems, xs, lands, _, local_sems = started
    n = len(xs)

    def body(*refs):
        x_refs, land_refs = refs[:n], refs[n:2 * n]
        for cp in _exchange_copies(x_refs, land_refs, refs[2 * n], refs[2 * n + 1], scatter):
            cp.wait_send()
            cp.wait_recv()
        if not scatter:
            for cp in _own_copies(x_refs, land_refs, refs[2 * n + 2]):
                cp.wait()

    sems = (send_sems, recv_sems) if scatter else (send_sems, recv_sems, local_sems)
    out = pl.pallas_call(
        body, name=name, out_shape=tuple(pltpu.HBM(t.shape, t.dtype) for t in xs + lands),
        in_specs=(HBM_SPEC,) * (2 * n) + (SEM_SPEC,) * len(sems) + (pl.BlockSpec(memory_space=pl.ANY),),
        out_specs=(HBM_SPEC,) * (2 * n), input_output_aliases={i: i for i in range(2 * n)},
        compiler_params=pltpu.CompilerParams(has_side_effects=DATAFLOW))(*xs, *lands, *sems, after)
    return list(out[:n]), list(out[n:])


def _gather_start(groups, token, *, name):
    first = groups[0]
    groups = [[first[0] + token[0, 0].astype(first[0].dtype)] + list(first[1:])] + [list(g) for g in groups[1:]]
    return _exchange_start(groups, name=name, scatter=False)


def _gather_finish(started, after, *, name):
    return _exchange_wait(started, after, name=name, scatter=False)[1]


def _reduce_start(grads, *, name):
    return _exchange_start([grads], name=name, scatter=True)[0]


def _reduce_finish(started, after, me, *, name):
    sent, gots = _exchange_wait(started, after, name=name + "_wait", scatter=True)
    return [_add_received(g, got, me, name=f"{name}_add{m}") for m, (g, got) in enumerate(zip(sent, gots))]


def _pad_rows(a, mult):
    r = (-a.shape[0]) % mult
    return a if r == 0 else jnp.concatenate([a, jnp.zeros((r,) + a.shape[1:], a.dtype)], axis=0)


def kernel(x, mem, mix_norm, mem_norm, w_mem_kv, w_out, hg_w_in, hg_lb, hg_onorm, gm_w_in, gm_ln_g, gm_ln_b, gm_ws, gm_bs, ffn_norm, w_ffn_in, w_ffn_out, final_norm, loss_target, m_mix_norm, m_mem_norm, m_w_mem_kv, m_w_out, m_hg_w_in, m_hg_lb, m_hg_onorm, m_gm_w_in, m_gm_ln_g, m_gm_ln_b, m_gm_ws, m_gm_bs, m_ffn_norm, m_w_ffn_in, m_w_ffn_out, m_final_norm, v_mix_norm, v_mem_norm, v_w_mem_kv, v_w_out, v_hg_w_in, v_hg_lb, v_hg_onorm, v_gm_w_in, v_gm_ln_g, v_gm_ln_b, v_gm_ws, v_gm_bs, v_ffn_norm, v_w_ffn_in, v_w_ffn_out, v_final_norm):
    mx, my, mc = _place()
    me = 4 * mx + 2 * my + mc
    xs = x[0]
    mems = mem[0]
    tgt = loss_target[0]

    hg_t = hg_w_in[0].T.astype(BF16)
    gm_t = gm_w_in[0].T.astype(BF16)
    fi_t = [w_ffn_in[i].T.astype(BF16) for i in range(2)]
    kv_b = [w_mem_kv[i].astype(BF16) for i in range(2)]
    out_b = [w_out[i].astype(BF16) for i in range(2)]
    fo_b = [w_ffn_out[i].astype(BF16) for i in range(2)]
    W_hgT, token = _all_gather(hg_t, name="gather_first", in_vmem=False, with_token=True)
    ln_local = _pad_rows(jnp.concatenate([gm_ln_g, gm_ln_b], axis=0), 16)
    ln_local = jnp.concatenate([ln_local, jnp.zeros((16, LANE - ln_local.shape[1]), F32)], axis=1)
    gather_mix, fi0, fo0, gather_gm, fi1, fo1 = _gather_start(
        [kv_b + out_b, [fi_t[0]], [fo_b[0]], [gm_t, ln_local], [fi_t[1]], [fo_b[1]]], token,
        name="gather_rest_start")
    gather_fi, gather_fo = [fi0, fi1], [fo0, fo1]

    lb_soft = jax.nn.softmax(hg_lb, axis=0)
    lb0 = lb_soft[0:1]
    bsb = jnp.broadcast_to(gm_bs[0][:, :, None], (N_HEADS, GM_CHUNK, GM_CHUNK))
    ws = gm_ws[0]

    W_fiT, W_fo = [], []

    def ffn_fwd(xin, hf, i, **tail):
        W_fiT.extend(_gather_finish(gather_fi[i], hf, name=f"gather_fi{i}_wait"))
        gu, act = _ffn_in(hf, W_fiT[i], name=f"ffn_in{i}")
        W_fo.extend(_gather_finish(gather_fo[i], act, name=f"gather_fo{i}_wait"))
        return gu, act, _matmul(act, W_fo[i], res=xin, name=f"ffn_out{i}", **tail)

    h0 = _rms_fwd(xs, mix_norm[0:1], name="mix_norm0", dep=gather_mix[4])
    p0 = _matmul(h0, W_hgT, tb=True, name="hg_in")
    heads0, o0, states = _hgrn2_fwd(p0, lb0, hg_onorm, name="hgrn2_fwd")

    kv0, kv1, wo0, wo1 = _gather_finish(gather_mix, o0, name="gather_mix_wait")
    W_kv, W_out = [kv0, kv1], [wo0, wo1]
    mem_n, kv = [], []
    for i in range(2):
        mn = _rms_fwd(mems, mem_norm[i:i + 1], name=f"mem_norm{i}")
        mem_n.append(mn)
        kv.append(_matmul(mn, W_kv[i], name=f"mem_kv{i}"))

    heads0 = _attn_fwd(p0, 4 * D_TOK // D_MEM, kv[0], heads0, name="attn_fwd0")
    x1, hf0 = _matmul(heads0, W_out[0], res=xs, norm_gain=ffn_norm[0:1], name="out_proj0")
    gu0, act0, (x2, h1) = ffn_fwd(x1, hf0, 0, norm_gain=mix_norm[1:2])

    W_gmT, ln_all = _gather_finish(gather_gm, h1, name="gather_gm_wait")
    ln_all = ln_all.reshape(N_DEV, 16, LANE)
    ln_g = ln_all[:, 0, :D_TOK // N_DEV].reshape(1, D_TOK)
    ln_b = ln_all[:, 1, :D_TOK // N_DEV].reshape(1, D_TOK)
    p1 = _matmul(h1, W_gmT, tb=True, name="gm_in")
    heads1 = _gmlp_fwd(p1, ln_g, ln_b, ws, bsb, name="gmlp_fwd")
    heads1 = _attn_fwd(p1, 2 * D_TOK // D_MEM, kv[1], heads1, name="attn_fwd1")
    x3, hf1 = _matmul(heads1, W_out[1], res=x2, norm_gain=ffn_norm[1:2], name="out_proj1")
    gu1, act1, (dx, g_final, loss_part) = ffn_fwd(x3, hf1, 1, loss_head=(final_norm.reshape(1, D_MODEL), tgt))

    def ffn_bwd(dx, xin, hf, gu, act, i, dep):
        dgu = _ffn_out_dx(dx, W_fo[i], gu, dep, name=f"ffn_out_dx{i}")
        g_wfo = _matmul(act, dx, ta=True, out_dtype=BF16, name=f"ffn_out_dw{i}")
        g_wfi_t = _matmul(dgu, hf, ta=True, a_halves=True, out_dtype=BF16, name=f"ffn_in_dw{i}")
        dx, g_norm = _matmul(dgu, W_fiT[i], a_halves=True, res=dx, norm_bwd=(xin, ffn_norm[i:i + 1]),
                             name=f"ffn_in_dx{i}")
        return dx, g_wfi_t, g_wfo, g_norm

    def mem_bwd(dkv, i):
        g_wkv = _matmul(mem_n[i], dkv, ta=True, out_dtype=BF16, name=f"mem_kv_dw{i}")
        dmn = _matmul(dkv, W_kv[i], tb=True, name=f"mem_kv_dx{i}")
        _, g_norm = _rms_bwd(mems, mem_norm[i:i + 1], dmn, jnp.zeros_like(mems), name=f"mem_norm_bwd{i}")
        return g_wkv, g_norm

    dx, g_wfi1_t, g_wfo1, g_ffn1 = ffn_bwd(dx, x3, hf1, gu1, act1, 1, loss_part)
    dheads = _matmul(dx, W_out[1], tb=True, name="out_proj_dx1")
    g_wout1 = _matmul(heads1, dx, ta=True, out_dtype=BF16, name="out_proj_dw1")
    dp, g_ws, g_bs, g_lng, g_lnb = _gmlp_bwd(p1, ln_g, ln_b, ws, bsb, dheads, name="gmlp_bwd")
    dp, dk, dv = _attn_bwd(p1, 2 * D_TOK // D_MEM, kv[1], dheads, dp, name="attn_bwd1")
    g_wkv1, g_mem1 = mem_bwd(jnp.concatenate([dk, dv], axis=1), 1)
    g_wgm_t = _matmul(dp, h1, ta=True, out_dtype=BF16, name="gm_in_dw")
    dx, g_mix1 = _matmul(dp, W_gmT, res=dx, norm_bwd=(x2, mix_norm[1:2]), name="gm_in_dx")
    reduce_l1 = _reduce_start([g_wkv1, g_wout1, g_wgm_t, g_wfi1_t, g_wfo1], name="reduce_l1_start")
    early = [loss_part, g_final, g_ffn1, g_mix1, g_mem1, g_ws.reshape(1, -1), g_bs.reshape(1, -1), g_lng, g_lnb]
    early_rows = _pad_rows(jnp.concatenate(early, axis=1).reshape(-1, LANE), 16) + reduce_l1[4][0, 0]
    small_early = _exchange_start([[early_rows]], name="reduce_small_start", scatter=False)[0]

    dx, g_wfi0_t, g_wfo0, g_ffn0 = ffn_bwd(dx, x1, hf0, gu0, act0, 0, small_early[4])
    reduce_ffn0 = _reduce_start([g_wfi0_t, g_wfo0], name="reduce_ffn0_start")
    dheads = _matmul(dx, W_out[0], tb=True, name="out_proj_dx0", dep=reduce_ffn0[4])
    g_wout0 = _matmul(heads0, dx, ta=True, out_dtype=BF16, name="out_proj_dw0")
    dp, g_lb0, g_onorm = _hgrn2_bwd(p0, lb0, hg_onorm, o0, states, dheads, name="hgrn2_bwd")
    dp, dk, dv = _attn_bwd(p0, 4 * D_TOK // D_MEM, kv[0], dheads, dp, name="attn_bwd0")
    g_wkv0, g_mem0 = mem_bwd(jnp.concatenate([dk, dv], axis=1), 0)
    g_whg_t = _matmul(dp, h0, ta=True, out_dtype=BF16, name="hg_in_dw")
    reduce_mix0 = _reduce_start([g_wkv0, g_wout0, g_whg_t], name="reduce_mix0_start")
    grad_x, g_mix0 = _matmul(dp, W_hgT, res=dx, norm_bwd=(xs, mix_norm[0:1]), name="hg_in_dx", dep=reduce_mix0[4])

    g_kv1, g_out1, g_gm_t, g_fi1_t, g_fo1 = _reduce_finish(reduce_l1, grad_x, me, name="reduce_l1")
    g_fi0_t, g_fo0 = _reduce_finish(reduce_ffn0, g_kv1, me, name="reduce_ffn0")
    g_kv0, g_out0, g_hg_t = _reduce_finish(reduce_mix0, g_fi0_t, me, name="reduce_mix0")
    g_shards = [jnp.stack([g_kv0, g_kv1]), jnp.stack([g_out0, g_out1]), g_hg_t[None], g_gm_t[None],
                jnp.stack([g_fi0_t, g_fi1_t]), jnp.stack([g_fo0, g_fo1])]
    transposed = (4, 7, 13)

    late = [g_ffn0, g_lb0, g_onorm, g_mem0, g_mix0]
    late_rows = _pad_rows(jnp.concatenate(late, axis=1).reshape(-1, LANE), 8)
    red_late = _all_gather(late_rows, name="reduce_small_late", in_vmem=True, reduce_sum=True)
    gathered = _exchange_wait(small_early, red_late, name="reduce_small_wait", scatter=False)[1][0]
    red_early = _sum_blocks(gathered.reshape(N_DEV, -1, LANE), name="reduce_small_sum")

    def split(flat, parts):
        out, off = [], 0
        for t in parts:
            out.append(flat[off:off + t.shape[1]])
            off += t.shape[1]
        return out

    r_loss, r_final, r_ffn1, r_mix1, r_mem1, r_ws, r_bs, r_lng, r_lnb = split(red_early.reshape(-1), early)
    r_ffn0, r_lb0, r_onorm, r_mem0, r_mix0 = split(red_late.reshape(-1), late)
    loss = r_loss[0]
    g_mix_norm = jnp.stack([r_mix0, r_mix1])
    g_mem_norm = jnp.stack([r_mem0, r_mem1])
    g_hg_lb = r_lb0[None, :] * lb0 * (jnp.eye(3, dtype=F32)[:, 0:1] - lb_soft)
    g_hg_onorm = r_onorm.reshape(1, D_TOK)
    width = D_TOK // N_DEV
    g_gm_ln_g = lax.dynamic_slice(r_lng, (me * width,), (width,)).reshape(1, width)
    g_gm_ln_b = lax.dynamic_slice(r_lnb, (me * width,), (width,)).reshape(1, width)
    g_gm_ws = r_ws.reshape(gm_ws.shape)
    g_gm_bs = r_bs.reshape(gm_bs.shape)
    g_ffn_norm = jnp.stack([r_ffn0, r_ffn1])
    g_final_norm = r_final

    grads = [g_mix_norm, g_mem_norm, g_shards[0], g_shards[1], g_shards[2], g_hg_lb, g_hg_onorm, g_shards[3],
             g_gm_ln_g, g_gm_ln_b, g_gm_ws, g_gm_bs, g_ffn_norm, g_shards[4], g_shards[5], g_final_norm]
    weights = [mix_norm, mem_norm, w_mem_kv, w_out, hg_w_in, hg_lb, hg_onorm, gm_w_in, gm_ln_g, gm_ln_b, gm_ws, gm_bs,
               ffn_norm, w_ffn_in, w_ffn_out, final_norm]
    ms = [m_mix_norm, m_mem_norm, m_w_mem_kv, m_w_out, m_hg_w_in, m_hg_lb, m_hg_onorm, m_gm_w_in, m_gm_ln_g,
          m_gm_ln_b, m_gm_ws, m_gm_bs, m_ffn_norm, m_w_ffn_in, m_w_ffn_out, m_final_norm]
    vs = [v_mix_norm, v_mem_norm, v_w_mem_kv, v_w_out, v_hg_w_in, v_hg_lb, v_hg_onorm, v_gm_w_in, v_gm_ln_g,
          v_gm_ln_b, v_gm_ws, v_gm_bs, v_ffn_norm, v_w_ffn_in, v_w_ffn_out, v_final_norm]
    deltas, new_m, new_v = [], [], []
    for n, (w, g, m, v) in enumerate(zip(weights, grads, ms, vs)):
        if w.ndim == 1:
            d, nm, nv = _adamw(w[None], g.reshape(1, -1), m[None], v[None], name=f"adamw{n}")
            d, nm, nv = d[0], nm[0], nv[0]
        elif n in transposed:
            flip = lambda t: jnp.swapaxes(t, 1, 2)
            d, nm, nv = (flip(t) for t in _adamw(flip(w), g, flip(m), flip(v), name=f"adamw{n}"))
            grads[n] = flip(g)
        else:
            d, nm, nv = _adamw(w, g.reshape(w.shape), m, v, name=f"adamw{n}")
        deltas.append(d)
        new_m.append(nm)
        new_v.append(nv)
    grads = [g.reshape(w.shape) for g, w in zip(grads, weights)]
    return (loss, grad_x[None], *grads, *deltas, *new_m, *new_v)
```

```python
import jax
import jax.numpy as jnp
from jax import lax
from jax.experimental import pallas as pl
from jax.experimental.pallas import tpu as pltpu

F32 = jnp.float32
BF16 = jnp.bfloat16
MXU_DTYPE = jnp.bfloat16
MESH_ID = pl.DeviceIdType.MESH

N_DEV = 8
EPS = 1e-6
D_MODEL = 1024
D_TOK = 768
D_MEM = 256
N_HEADS = 6
HEAD = 128
MEM_HEADS = 4
MEM_HDIM = 64
GM_CHUNK = 128
ATTN_ROWS = 2048
D_FF = 2816
HG_SUB = 16
HG_GROUP = 6
HG_IN = 4 * D_TOK + D_MEM
GM_IN = 2 * D_TOK + D_MEM
LANE = 128
MXU_COLS = 256

ADAM_LR = 0.001
ADAM_B1 = 0.9
ADAM_B2 = 0.999
ADAM_EPS = 1e-08
ADAM_WD = 0.01
ADAM_STEP = 10

VMEM_LIMIT = 48 * 2 ** 20
VMEM_LIMIT_WIDE = 58 * 2 ** 20


def _params(sem=None, limit=VMEM_LIMIT):
    return pltpu.CompilerParams(dimension_semantics=sem, vmem_limit_bytes=limit)


def _tile(n, cap, q=LANE):
    if n <= cap:
        return n
    best = None
    for t in range(q, cap + 1, q):
        if n % t == 0:
            best = t
    assert best is not None, (n, cap, q)
    return best


def _sigmoid(x):
    return 1.0 / (1.0 + jnp.exp(-x))


def _gelu(x, with_grad=False):
    cdf = 0.5 * (1.0 + lax.erf(x * 0.7071067811865476))
    if not with_grad:
        return x * cdf
    return x * cdf, cdf + x * jnp.exp(-0.5 * x * x) * 0.3989422804014327


def _matmul(a, b, *, name, ta=False, tb=False, res=None, out_dtype=F32, a_halves=False, b_halves=False, dep=None,
            norm_gain=None, norm_bwd=None, loss_head=None):
    if a_halves and ta:
        K, M = a.shape[1], 2 * a.shape[2]
    elif a_halves:
        M, K = a.shape[1], 2 * a.shape[2]
    else:
        K, M = a.shape if ta else a.shape[::-1]
    if b_halves:
        assert not tb and b.shape[1] == K
        N = 2 * b.shape[2]
    else:
        N = b.shape[0] if tb else b.shape[1]
        assert (b.shape[1] if tb else b.shape[0]) == K
    tm = _tile(M // 2 if (a_halves and ta) else M, 1664 if ta else 1024)
    tn = _tile(N // 2 if b_halves else N, 1792)
    tk = _tile(K // 2 if (a_halves and not ta) else K, 1024 if ta else 1664)
    nk = K // tk
    dims = (((0 if ta else 1,), (1 if tb else 0,)), ((), ()))

    strips = norm_bwd is not None or loss_head is not None
    fused = norm_gain is not None or strips
    n_in = 2 + (res is not None) + (norm_gain is not None) + 2 * strips + (dep is not None)
    if fused:
        assert tn == N, "the fused norm needs whole rows"
        assert (norm_gain is not None) + (norm_bwd is not None) + (loss_head is not None) == 1
        assert not strips or (res is not None and nk > 1 and tm % LANE == 0)

    def body(*refs):
        a_ref, b_ref = refs[:2]
        r_ref = refs[2] if res is not None else None
        g_ref = refs[2 + (res is not None)] if fused else None
        x_ref = refs[3 + (res is not None)] if strips else None
        o_ref = refs[n_in]
        h_ref = refs[n_in + 1] if fused else None
        l_ref = refs[n_in + 2] if loss_head is not None else None
        acc = None if nk == 1 else refs[-1]
        k = pl.program_id(2)

        def product():
            return lax.dot_general(a_ref[...].astype(MXU_DTYPE), b_ref[...].astype(MXU_DTYPE), dims,
                                   preferred_element_type=F32)

        def finish(r):
            if loss_head is not None:
                @pl.when(pl.program_id(0) == 0)
                def _():
                    h_ref[...] = jnp.zeros_like(h_ref)
                    l_ref[...] = jnp.zeros_like(l_ref)

                acc[...] = r + r_ref[...]
                gv = g_ref[...]

                def strip(s, carry):
                    dg, loss = carry
                    rows = pl.ds(pl.multiple_of(s * LANE, LANE), LANE)
                    xv = acc[rows, :]
                    scale = lax.rsqrt(jnp.mean(xv * xv, axis=-1, keepdims=True) + EPS)
                    xh = xv * scale
                    err = xh * gv - x_ref[rows, :]
                    loss = loss + 0.5 * jnp.sum(jnp.mean(err * err, axis=-1, keepdims=True), axis=0, keepdims=True)
                    dy = err * (1.0 / N)
                    u = dy * gv
                    o_ref[rows, :] = scale * (u - xh * jnp.mean(u * xh, axis=-1, keepdims=True))
                    return dg + jnp.sum(dy * xh, axis=0, keepdims=True), loss

                dg, loss = lax.fori_loop(0, tm // LANE, strip, (jnp.zeros((1, N), F32), jnp.zeros((1, 1), F32)))
                h_ref[...] += dg
                l_ref[...] += jnp.broadcast_to(loss, l_ref.shape)
                return
            if norm_bwd is not None:
                @pl.when(pl.program_id(0) == 0)
                def _():
                    h_ref[...] = jnp.zeros_like(h_ref)

                acc[...] = r
                gv = g_ref[...]

                def strip(s, dg):
                    rows = pl.ds(pl.multiple_of(s * LANE, LANE), LANE)
                    rv = acc[rows, :]
                    xv = x_ref[rows, :]
                    scale = lax.rsqrt(jnp.mean(xv * xv, axis=-1, keepdims=True) + EPS)
                    xh = xv * scale
                    u = rv * gv
                    o_ref[rows, :] = r_ref[rows, :] + scale * (u - xh * jnp.mean(u * xh, axis=-1, keepdims=True))
                    return dg + jnp.sum(rv * xh, axis=0, keepdims=True)

                h_ref[...] += lax.fori_loop(0, tm // LANE, strip, jnp.zeros((1, N), F32))
                return
            if res is not None:
                r = r + r_ref[...].astype(F32)
            o_ref[...] = r.astype(out_dtype)
            if norm_gain is not None:
                scale = lax.rsqrt(jnp.mean(r * r, axis=-1, keepdims=True) + EPS)
                h_ref[...] = (r * scale * g_ref[...]).astype(h_ref.dtype)

        if nk == 1:
            finish(product())
            return

        @pl.when(k == 0)
        def _():
            acc[...] = product()

        @pl.when((k > 0) & (k < nk - 1))
        def _():
            acc[...] += product()

        @pl.when(k == nk - 1)
        def _():
            finish(acc[...] + product())

    if a_halves and ta:
        mh = M // 2 // tm
        a_spec = pl.BlockSpec((None, tk, tm), lambda i, j, k: (i // mh, k, i % mh))
    elif a_halves:
        kh = nk // 2
        a_spec = pl.BlockSpec((None, tm, tk), lambda i, j, k: (k // kh, i, k % kh))
    elif ta:
        a_spec = pl.BlockSpec((tk, tm), lambda i, j, k: (k, i))
    else:
        a_spec = pl.BlockSpec((tm, tk), lambda i, j, k: (i, k))
    if b_halves:
        nh = N // 2 // tn
        b_spec = pl.BlockSpec((None, tk, tn), lambda i, j, k: (j // nh, k, j % nh))
    elif tb:
        b_spec = pl.BlockSpec((tn, tk), lambda i, j, k: (j, k))
    else:
        b_spec = pl.BlockSpec((tk, tn), lambda i, j, k: (k, j))
    o_spec = pl.BlockSpec((tm, tn), lambda i, j, k: (i, j))
    in_specs = [a_spec, b_spec] + ([o_spec] if res is not None else [])
    args = (a, b) + ((res,) if res is not None else ())
    out_specs, out_shape = o_spec, jax.ShapeDtypeStruct((M, N), out_dtype)
    vec = pl.BlockSpec((1, N), lambda i, j, k: (0, 0))
    sem = ("parallel", "parallel", "arbitrary")
    if norm_gain is not None:
        in_specs.append(vec)
        args += (norm_gain,)
        out_specs, out_shape = [o_spec, o_spec], [out_shape, jax.ShapeDtypeStruct((M, N), BF16)]
    if norm_bwd is not None:
        x_in, gain = norm_bwd
        in_specs += [vec, o_spec]
        args += (gain, x_in)
        out_specs, out_shape = [o_spec, vec], [out_shape, jax.ShapeDtypeStruct((1, N), F32)]
        sem = ("arbitrary", "arbitrary", "arbitrary")
    if loss_head is not None:
        gain, target = loss_head
        in_specs += [vec, o_spec]
        args += (gain, target)
        one = pl.BlockSpec((1, LANE), lambda i, j, k: (0, 0))
        out_specs = [o_spec, vec, one]
        out_shape = [out_shape, jax.ShapeDtypeStruct((1, N), F32), jax.ShapeDtypeStruct((1, LANE), F32)]
        sem = ("arbitrary", "arbitrary", "arbitrary")
    if dep is not None:
        in_specs.append(pl.BlockSpec(memory_space=pl.ANY))
        args += (dep,)
    return pl.pallas_call(
        body, name=name, grid=(M // tm, N // tn, nk), in_specs=in_specs, out_specs=out_specs, out_shape=out_shape,
        scratch_shapes=[] if nk == 1 else [pltpu.VMEM((tm, tn), F32)],
        compiler_params=_params(sem, VMEM_LIMIT_WIDE if strips else VMEM_LIMIT))(*args)


def _ffn_in(hf, wt, *, name):
    S, K = hf.shape
    tm = _tile(S, 512)
    tn = _tile(D_FF, 1408)
    nh = D_FF // tn
    nt = (((1,), (1,)), ((), ()))

    def body(a_ref, bg_ref, bu_ref, gu_ref, act_ref):
        av = a_ref[...].astype(MXU_DTYPE)
        for c0 in range(0, tn, MXU_COLS):
            cs = slice(c0, min(c0 + MXU_COLS, tn))
            gate = lax.dot_general(av, bg_ref[cs, :].astype(MXU_DTYPE), nt, preferred_element_type=F32)
            up = lax.dot_general(av, bu_ref[cs, :].astype(MXU_DTYPE), nt, preferred_element_type=F32)
            gu_ref[0, :, cs] = gate.astype(gu_ref.dtype)
            gu_ref[1, :, cs] = up.astype(gu_ref.dtype)
            act_ref[:, cs] = (gate * _sigmoid(gate) * up).astype(act_ref.dtype)

    return pl.pallas_call(
        body, name=name, grid=(nh, S // tm),
        in_specs=[pl.BlockSpec((tm, K), lambda j, i: (i, 0)), pl.BlockSpec((tn, K), lambda j, i: (j, 0)),
                  pl.BlockSpec((tn, K), lambda j, i: (j + nh, 0))],
        out_specs=[pl.BlockSpec((2, tm, tn), lambda j, i: (0, i, j)), pl.BlockSpec((tm, tn), lambda j, i: (i, j))],
        out_shape=[jax.ShapeDtypeStruct((2, S, D_FF), BF16), jax.ShapeDtypeStruct((S, D_FF), BF16)],
        compiler_params=_params(("parallel", "parallel")))(hf, wt, wt)


def _ffn_out_dx(dx, w, gu, dep, *, name):
    S, K = dx.shape
    tm = _tile(S, 1024)
    tn = _tile(D_FF, 1408)

    def body(a_ref, b_ref, gu_ref, dep_ref, o_ref):
        del dep_ref
        av = a_ref[...].astype(MXU_DTYPE)
        for c0 in range(0, tn, MXU_COLS):
            cs = slice(c0, min(c0 + MXU_COLS, tn))
            da = lax.dot_general(av, b_ref[cs, :].astype(MXU_DTYPE), (((1,), (1,)), ((), ())),
                                 preferred_element_type=F32)
            gate = gu_ref[0, :, cs].astype(F32)
            up = gu_ref[1, :, cs].astype(F32)
            sg = _sigmoid(gate)
            o_ref[0, :, cs] = (da * up * sg * (1.0 + gate * (1.0 - sg))).astype(o_ref.dtype)
            o_ref[1, :, cs] = (da * gate * sg).astype(o_ref.dtype)

    halves = pl.BlockSpec((2, tm, tn), lambda i, j: (0, i, j))
    return pl.pallas_call(
        body, name=name, grid=(S // tm, D_FF // tn),
        in_specs=[pl.BlockSpec((tm, K), lambda i, j: (i, 0)), pl.BlockSpec((tn, K), lambda i, j: (j, 0)), halves,
                  pl.BlockSpec(memory_space=pl.ANY)],
        out_specs=halves, out_shape=jax.ShapeDtypeStruct((2, S, D_FF), BF16),
        compiler_params=_params(("parallel", "parallel")))(dx, w, gu, dep)


def _rms_fwd(x, g, *, name, dep=None):
    R, Dm = x.shape
    tr = _tile(R, 512, 8)

    def body(x_ref, g_ref, *rest):
        o_ref = rest[-1]
        xv = x_ref[...]
        r = lax.rsqrt(jnp.mean(xv * xv, axis=-1, keepdims=True) + EPS)
        o_ref[...] = (xv * r * g_ref[...]).astype(o_ref.dtype)

    in_specs = [pl.BlockSpec((tr, Dm), lambda i: (i, 0)), pl.BlockSpec((1, Dm), lambda i: (0, 0))]
    args = (x, g)
    if dep is not None:
        in_specs.append(pl.BlockSpec(memory_space=pl.ANY))
        args += (dep,)
    return pl.pallas_call(
        body, name=name, grid=(R // tr,), in_specs=in_specs,
        out_specs=pl.BlockSpec((tr, Dm), lambda i: (i, 0)), out_shape=jax.ShapeDtypeStruct((R, Dm), BF16),
        compiler_params=_params(("parallel",)))(*args)


def _rms_bwd(x, g, dh, dres, *, name):
    R, Dm = x.shape
    tr = _tile(R, 256, 8)

    def body(x_ref, g_ref, dh_ref, dres_ref, dx_ref, dg_ref):
        @pl.when(pl.program_id(0) == 0)
        def _():
            dg_ref[...] = jnp.zeros_like(dg_ref)

        xv = x_ref[...]
        r = lax.rsqrt(jnp.mean(xv * xv, axis=-1, keepdims=True) + EPS)
        xh = xv * r
        dhv = dh_ref[...].astype(F32)
        dg_ref[...] += jnp.sum(dhv * xh, axis=0, keepdims=True)
        u = dhv * g_ref[...]
        dx = r * (u - xh * jnp.mean(u * xh, axis=-1, keepdims=True))
        dx_ref[...] = dres_ref[...] + dx

    row = pl.BlockSpec((tr, Dm), lambda i: (i, 0))
    vec = pl.BlockSpec((1, Dm), lambda i: (0, 0))
    return pl.pallas_call(
        body, name=name, grid=(R // tr,), in_specs=[row, vec, row, row], out_specs=[row, vec],
        out_shape=[jax.ShapeDtypeStruct((R, Dm), F32), jax.ShapeDtypeStruct((1, Dm), F32)],
        compiler_params=_params(("arbitrary",)))(x, g, dh, dres)


def _head_mask(h):
    lane = lax.broadcasted_iota(jnp.int32, (1, D_MEM), 1)
    return (lane >= h * MEM_HDIM) & (lane < (h + 1) * MEM_HDIM)


def _attn_probs(qv, k_mx, mask):
    s = lax.dot_general(jnp.where(mask, qv, 0.0).astype(MXU_DTYPE), k_mx, (((1,), (1,)), ((), ())),
                        preferred_element_type=F32) * (MEM_HDIM ** -0.5)
    e = jnp.exp(s - jnp.max(s, axis=-1, keepdims=True))
    return e / jnp.sum(e, axis=-1, keepdims=True)


def _attn_fwd(p, qcol, kv, heads, *, name):
    S = p.shape[0]
    M = kv.shape[0]
    ts = _tile(S, ATTN_ROWS, 8)

    def body(q_ref, k_ref, v_ref, heads_in, o_ref):
        del heads_in
        qv = q_ref[...]
        kx = k_ref[...].astype(MXU_DTYPE)
        vv = v_ref[...]
        out = jnp.zeros((ts, D_MEM), F32)
        for h in range(MEM_HEADS):
            mask = _head_mask(h)
            pr = _attn_probs(qv, kx, mask)
            out = out + jnp.dot(pr.astype(MXU_DTYPE), jnp.where(mask, vv, 0.0).astype(MXU_DTYPE),
                                preferred_element_type=F32)
        o_ref[...] = out.astype(o_ref.dtype)

    return pl.pallas_call(
        body, name=name, grid=(S // ts,),
        in_specs=[pl.BlockSpec((ts, D_MEM), lambda i: (i, qcol)), pl.BlockSpec((M, D_MEM), lambda i: (0, 0)),
                  pl.BlockSpec((M, D_MEM), lambda i: (0, 1)), pl.BlockSpec(memory_space=pl.ANY)],
        out_specs=pl.BlockSpec((ts, D_MEM), lambda i: (i, D_TOK // D_MEM)),
        out_shape=jax.ShapeDtypeStruct(heads.shape, heads.dtype), input_output_aliases={3: 0},
        compiler_params=_params(("parallel",)))(p, kv, kv, heads)


def _attn_bwd(p, qcol, kv, dheads, dp, *, name):
    S = p.shape[0]
    M = kv.shape[0]
    ts = _tile(S, ATTN_ROWS, 8)
    scale = MEM_HDIM ** -0.5

    def body(q_ref, k_ref, v_ref, do_ref, dp_in, dq_ref, dk_ref, dv_ref):
        del dp_in

        @pl.when(pl.program_id(0) == 0)
        def _():
            dk_ref[...] = jnp.zeros_like(dk_ref)
            dv_ref[...] = jnp.zeros_like(dv_ref)

        qv = q_ref[...]
        kv_ = k_ref[...]
        kx = kv_.astype(MXU_DTYPE)
        vv = v_ref[...]
        dox = do_ref[...].astype(MXU_DTYPE)
        qx = qv.astype(MXU_DTYPE)
        dq = jnp.zeros((ts, D_MEM), F32)
        for h in range(MEM_HEADS):
            mask = _head_mask(h)
            pr = _attn_probs(qv, kx, mask)
            vh = jnp.where(mask, vv, 0.0).astype(MXU_DTYPE)
            dpr = lax.dot_general(dox, vh, (((1,), (1,)), ((), ())), preferred_element_type=F32)
            ds = (pr * (dpr - jnp.sum(dpr * pr, axis=-1, keepdims=True)) * scale).astype(MXU_DTYPE)
            dq = dq + jnp.dot(ds, jnp.where(mask, kv_, 0.0).astype(MXU_DTYPE), preferred_element_type=F32)
            dk_h = lax.dot_general(ds, qx, (((0,), (0,)), ((), ())), preferred_element_type=F32)
            dv_h = lax.dot_general(pr.astype(MXU_DTYPE), dox, (((0,), (0,)), ((), ())), preferred_element_type=F32)
            dk_ref[...] += jnp.where(mask, dk_h, 0.0)
            dv_ref[...] += jnp.where(mask, dv_h, 0.0)
        dq_ref[...] = dq.astype(dq_ref.dtype)

    return pl.pallas_call(
        body, name=name, grid=(S // ts,),
        in_specs=[pl.BlockSpec((ts, D_MEM), lambda i: (i, qcol)), pl.BlockSpec((M, D_MEM), lambda i: (0, 0)),
                  pl.BlockSpec((M, D_MEM), lambda i: (0, 1)),
                  pl.BlockSpec((ts, D_MEM), lambda i: (i, D_TOK // D_MEM)), pl.BlockSpec(memory_space=pl.ANY)],
        out_specs=[pl.BlockSpec((ts, D_MEM), lambda i: (i, qcol)), pl.BlockSpec((M, D_MEM), lambda i: (0, 0)),
                   pl.BlockSpec((M, D_MEM), lambda i: (0, 0))],
        out_shape=[jax.ShapeDtypeStruct(dp.shape, dp.dtype), jax.ShapeDtypeStruct((M, D_MEM), F32),
                   jax.ShapeDtypeStruct((M, D_MEM), F32)],
        input_output_aliases={4: 0}, compiler_params=_params(("arbitrary",)))(p, kv, kv, dheads, dp)


def _gm_forward_parts(u_ref, v_ref, lng_ref, lnb_ref, w_ref, bsb_ref, with_grad=False):
    if with_grad:
        (zu, du_gelu), (zv, dv_gelu) = _gelu(u_ref[...], True), _gelu(v_ref[...], True)
    else:
        zu, zv, du_gelu, dv_gelu = _gelu(u_ref[...]), _gelu(v_ref[...]), None, None
    mu = jnp.mean(zv, axis=-1, keepdims=True)
    cen = zv - mu
    rs = lax.rsqrt(jnp.mean(cen * cen, axis=-1, keepdims=True) + EPS)
    vh = cen * rs
    vn = vh * lng_ref[...] + lnb_ref[...]
    row = lax.broadcasted_iota(jnp.int32, (GM_CHUNK, GM_CHUNK), 0)
    col = lax.broadcasted_iota(jnp.int32, (GM_CHUNK, GM_CHUNK), 1)
    tril = row >= col
    wm = [jnp.where(tril, w_ref[g], 0.0).astype(MXU_DTYPE) for g in range(N_HEADS)]
    vnx = [vn[:, g * HEAD:(g + 1) * HEAD].astype(MXU_DTYPE) for g in range(N_HEADS)]
    sv = [jnp.dot(wm[g], vnx[g], preferred_element_type=F32) + bsb_ref[g] for g in range(N_HEADS)]
    return zu, vh, rs, wm, vnx, sv, tril, du_gelu, dv_gelu


def _gmlp_fwd(p, lng, lnb, ws, bsb, *, name):
    S = p.shape[0]

    def body(u_ref, v_ref, lng_ref, lnb_ref, w_ref, bsb_ref, o_ref):
        zu, _, _, _, _, sv, _, _, _ = _gm_forward_parts(u_ref, v_ref, lng_ref, lnb_ref, w_ref, bsb_ref)
        for g in range(N_HEADS):
            o_ref[:, g * HEAD:(g + 1) * HEAD] = (zu[:, g * HEAD:(g + 1) * HEAD] * sv[g]).astype(o_ref.dtype)

    blk = lambda c: pl.BlockSpec((GM_CHUNK, D_TOK), lambda i: (i, c))
    vec = pl.BlockSpec((1, D_TOK), lambda i: (0, 0))
    cube = pl.BlockSpec((N_HEADS, GM_CHUNK, GM_CHUNK), lambda i: (0, 0, 0))
    return pl.pallas_call(
        body, name=name, grid=(S // GM_CHUNK,), in_specs=[blk(0), blk(1), vec, vec, cube, cube],
        out_specs=blk(0), out_shape=jax.ShapeDtypeStruct((S, D_MODEL), BF16),
        compiler_params=_params(("parallel",)))(p, p, lng, lnb, ws, bsb)


def _gmlp_bwd(p, lng, lnb, ws, bsb, dheads, *, name):
    S = p.shape[0]

    def body(u_ref, v_ref, lng_ref, lnb_ref, w_ref, bsb_ref, dt_ref, dp_ref, dw_ref, dbs_ref, dlg_ref, dlb_ref):
        @pl.when(pl.program_id(0) == 0)
        def _():
            dw_ref[...] = jnp.zeros_like(dw_ref)
            dbs_ref[...] = jnp.zeros_like(dbs_ref)
            dlg_ref[...] = jnp.zeros_like(dlg_ref)
            dlb_ref[...] = jnp.zeros_like(dlb_ref)

        zu, vh, rs, wm, vnx, sv, tril, du_gelu, dv_gelu = _gm_forward_parts(
            u_ref, v_ref, lng_ref, lnb_ref, w_ref, bsb_ref, with_grad=True)
        dt = dt_ref[...].astype(F32)
        dvn_parts = []
        for g in range(N_HEADS):
            sl = slice(g * HEAD, (g + 1) * HEAD)
            dsv = dt[:, sl] * zu[:, sl]
            dp_ref[:, sl] = (dt[:, sl] * sv[g] * du_gelu[:, sl]).astype(dp_ref.dtype)
            dsx = dsv.astype(MXU_DTYPE)
            dw = lax.dot_general(dsx, vnx[g], (((1,), (1,)), ((), ())), preferred_element_type=F32)
            dw_ref[g] += jnp.where(tril, dw, 0.0)
            dbs_ref[g] += jnp.sum(dsv, axis=-1, keepdims=True)
            dvn_parts.append(lax.dot_general(wm[g], dsx, (((0,), (0,)), ((), ())), preferred_element_type=F32))
        dvn = jnp.concatenate(dvn_parts, axis=-1)
        dlg_ref[...] += jnp.sum(dvn * vh, axis=0, keepdims=True)
        dlb_ref[...] += jnp.sum(dvn, axis=0, keepdims=True)
        dvh = dvn * lng_ref[...]
        dzv = rs * (dvh - jnp.mean(dvh, axis=-1, keepdims=True) - vh * jnp.mean(dvh * vh, axis=-1, keepdims=True))
        dp_ref[:, D_TOK:] = (dzv * dv_gelu).astype(dp_ref.dtype)

    blk = lambda c: pl.BlockSpec((GM_CHUNK, D_TOK), lambda i: (i, c))
    vec = pl.BlockSpec((1, D_TOK), lambda i: (0, 0))
    cube = pl.BlockSpec((N_HEADS, GM_CHUNK, GM_CHUNK), lambda i: (0, 0, 0))
    col = pl.BlockSpec((N_HEADS, GM_CHUNK, 1), lambda i: (0, 0, 0))
    return pl.pallas_call(
        body, name=name, grid=(S // GM_CHUNK,), in_specs=[blk(0), blk(1), vec, vec, cube, cube, blk(0)],
        out_specs=[pl.BlockSpec((GM_CHUNK, 2 * D_TOK), lambda i: (i, 0)), cube, col, vec, vec],
        out_shape=[jax.ShapeDtypeStruct((S, GM_IN), BF16), jax.ShapeDtypeStruct((N_HEADS, GM_CHUNK, GM_CHUNK), F32),
                   jax.ShapeDtypeStruct((N_HEADS, GM_CHUNK, 1), F32), jax.ShapeDtypeStruct((1, D_TOK), F32),
                   jax.ShapeDtypeStruct((1, D_TOK), F32)],
        compiler_params=_params(("arbitrary",)))(p, p, lng, lnb, ws, bsb, dheads)


def _chunk_tri(n, chunk, upper):
    r = lax.broadcasted_iota(jnp.int32, (n, n), 0)
    c = lax.broadcasted_iota(jnp.int32, (n, n), 1)
    same = (r // chunk) == (c // chunk)
    return jnp.where(same & ((r <= c) if upper else (r >= c)), 1.0, 0.0).astype(F32)


def _running_sum(tri, x):
    hi = x.astype(BF16)
    rest = x - hi.astype(F32)
    mid = rest.astype(BF16)
    lo = (rest - mid.astype(F32)).astype(BF16)
    tri = tri.astype(BF16)
    return (jnp.dot(tri, hi, preferred_element_type=F32) + jnp.dot(tri, mid, preferred_element_type=F32)
            + jnp.dot(tri, lo, preferred_element_type=F32))


MASKED = -1e30


def _pair_masks(mask_ref, n, upper):
    row = lax.broadcasted_iota(jnp.int32, (n, HEAD), 0)
    for i in range(n):
        mask_ref[i] = jnp.where((row <= i) if upper else (row >= i), 0.0, MASKED).astype(F32)


def _hg_gates(fz, lb):
    sg = _sigmoid(fz)
    f = lb + (1.0 - lb) * sg
    kk = (1.0 - lb) * (1.0 - sg)
    return sg, f, jnp.log(f), kk


def _hgrn2_fwd(p, lb, onorm, *, name):
    S = p.shape[0]
    C = HG_SUB
    tb = _tile(S, 256, C)
    nsub = tb // C

    def body(q_ref, fz_ref, v_ref, g_ref, lb_ref, on_ref, tok_ref, o_ref, st_ref, state, b_blk, k_blk, bsc, ksc, vsc):
        @pl.when(pl.program_id(0) == 0)
        def _():
            state[...] = jnp.zeros_like(state)

        _, _, lg, kk = _hg_gates(fz_ref[...], lb_ref[...])
        b_blk[...] = _running_sum(_chunk_tri(tb, C, False), lg)
        k_blk[...] = kk
        tt = lax.broadcasted_iota(jnp.int32, (C, HEAD), 0)

        def sub(c, carry):
            rows = pl.ds(pl.multiple_of(c * C, C), C)
            for h in range(N_HEADS):
                cols = slice(h * HEAD, (h + 1) * HEAD)
                qv = q_ref[rows, cols]
                vv = v_ref[rows, cols]
                b = b_blk[rows, cols]
                kk = k_blk[rows, cols]
                st0 = state[h]
                st0x = st0.astype(MXU_DTYPE)
                st_ref[c, h] = st0x.astype(st_ref.dtype)
                inter = lax.dot_general((qv * jnp.exp(b)).astype(MXU_DTYPE), st0x,
                                        (((1,), (1,)), ((), ())), preferred_element_type=F32)
                bsc[h] = b
                ksc[h] = kk
                vsc[h] = vv
                intra = jnp.zeros((C, HEAD), F32)
                for s in range(C):
                    dec = jnp.where(tt >= s, jnp.exp(b - bsc[h, pl.ds(s, 1), :]), 0.0)
                    a_s = jnp.sum(qv * ksc[h, pl.ds(s, 1), :] * dec, axis=-1, keepdims=True)
                    intra = intra + a_s * vsc[h, pl.ds(s, 1), :]
                o_ref[rows, cols] = inter + intra
                b_last = bsc[h, pl.ds(C - 1, 1), :]
                ke = kk * jnp.exp(b_last - b)
                state[h] = st0 * jnp.exp(b_last) + lax.dot_general(
                    vv.astype(MXU_DTYPE), ke.astype(MXU_DTYPE), (((0,), (0,)), ((), ())),
                    preferred_element_type=F32)
            return carry

        lax.fori_loop(0, nsub, sub, 0, unroll=2)

        for h in range(N_HEADS):
            cols = slice(h * HEAD, (h + 1) * HEAD)
            o = o_ref[:, cols]
            gv = g_ref[:, cols]
            n = o * lax.rsqrt(jnp.mean(o * o, axis=-1, keepdims=True) + EPS)
            tok_ref[:, cols] = (n * (gv * _sigmoid(gv)) * on_ref[:, cols]).astype(tok_ref.dtype)

    blk = lambda c: pl.BlockSpec((tb, D_TOK), lambda i, c=c: (i, c))
    vec = pl.BlockSpec((1, D_TOK), lambda i: (0, 0))
    stb = pl.BlockSpec((nsub, N_HEADS, HEAD, HEAD), lambda i: (i, 0, 0, 0))
    return pl.pallas_call(
        body, name=name, grid=(S // tb,), in_specs=[blk(0), blk(1), blk(2), blk(3), vec, vec],
        out_specs=[blk(0), blk(0), stb],
        out_shape=[jax.ShapeDtypeStruct((S, D_MODEL), BF16), jax.ShapeDtypeStruct((S, D_TOK), F32),
                   jax.ShapeDtypeStruct((S // C, N_HEADS, HEAD, HEAD), BF16)],
        scratch_shapes=[pltpu.VMEM((N_HEADS, HEAD, HEAD), F32)] + [pltpu.VMEM((tb, D_TOK), F32)] * 2
        + [pltpu.VMEM((N_HEADS, C, HEAD), F32)] * 3,
        compiler_params=_params(("arbitrary",)))(p, p, p, p, lb, onorm)


def _hgrn2_bwd(p, lb, onorm, o, states, dheads, *, name):
    S = p.shape[0]
    C = HG_SUB
    tb = _tile(S, 256, C)
    nsub = tb // C
    nblk = S // tb

    def body(q_ref, fz_ref, v_ref, g_ref, lb_ref, on_ref, o_ref, st_ref, dt_ref, dp_ref, dlb_ref, don_ref, dstate,
             b_blk, k_blk, do_blk, db_blk, dk_blk, dq_blk, dv_blk, bsc, ksc, vsc, qsc, dosc, causal, anti):
        @pl.when(pl.program_id(0) == 0)
        def _():
            dstate[...] = jnp.zeros_like(dstate)
            dlb_ref[...] = jnp.zeros_like(dlb_ref)
            don_ref[...] = jnp.zeros_like(don_ref)

        for h in range(N_HEADS):
            cols = slice(h * HEAD, (h + 1) * HEAD)
            onv = on_ref[:, cols]
            gv = g_ref[:, cols]
            ov = o_ref[:, cols]
            dt = dt_ref[:, cols].astype(F32)
            sgg = _sigmoid(gv)
            sil = gv * sgg
            rinv = lax.rsqrt(jnp.mean(ov * ov, axis=-1, keepdims=True) + EPS)
            n = ov * rinv
            don_ref[:, cols] += jnp.sum(dt * n * sil, axis=0, keepdims=True)
            dn = dt * sil * onv
            dp_ref[:, 3 * D_TOK + h * HEAD:3 * D_TOK + (h + 1) * HEAD] = (
                dt * n * onv * sgg * (1.0 + gv * (1.0 - sgg))).astype(dp_ref.dtype)
            do_blk[:, cols] = rinv * (dn - n * jnp.mean(dn * n, axis=-1, keepdims=True))
        _, _, lg, kk = _hg_gates(fz_ref[...], lb_ref[...])
        b_blk[...] = _running_sum(_chunk_tri(tb, C, False), lg)
        k_blk[...] = kk
        _pair_masks(causal, C, False)
        _pair_masks(anti, C, True)
        tt = lax.broadcasted_iota(jnp.int32, (C, HEAD), 0)

        def sub(j, heads):
            c = nsub - 1 - j
            rows = pl.ds(pl.multiple_of(c * C, C), C)
            for h in heads:
                cols = slice(h * HEAD, (h + 1) * HEAD)
                qv = q_ref[rows, cols]
                vv = v_ref[rows, cols]
                do = do_blk[rows, cols]
                b = b_blk[rows, cols]
                kk = k_blk[rows, cols]
                bsc[h] = b
                ksc[h] = kk
                vsc[h] = vv
                qsc[h] = qv
                dosc[h] = do
                b_last = bsc[h, pl.ds(C - 1, 1), :]
                eb = jnp.exp(b)
                qe = qv * eb
                ebb = jnp.exp(b_last - b)
                ke = kk * ebb
                e_last = jnp.exp(b_last)
                st0x = st_ref[c, h].astype(MXU_DTYPE)
                st0 = st0x.astype(F32)
                dst1 = dstate[h]
                dst1x = dst1.astype(MXU_DTYPE)
                dox = do.astype(MXU_DTYPE)
                dqe = jnp.dot(dox, st0x, preferred_element_type=F32)
                dke = jnp.dot(vv.astype(MXU_DTYPE), dst1x, preferred_element_type=F32)
                dv = lax.dot_general(ke.astype(MXU_DTYPE), dst1x, (((1,), (1,)), ((), ())),
                                     preferred_element_type=F32)
                db_last = (e_last * jnp.sum(st0 * dst1, axis=0, keepdims=True)
                           + jnp.sum(dke * ke, axis=0, keepdims=True))
                dstate[h] = dst1 * e_last + lax.dot_general(dox, qe.astype(MXU_DTYPE), (((0,), (0,)), ((), ())),
                                                            preferred_element_type=F32)
                dq_pairs = jnp.zeros((C, HEAD), F32)
                for s in range(C):
                    dec = jnp.exp(b - bsc[h, pl.ds(s, 1), :] + causal[s])
                    da_s = jnp.sum(do * vsc[h, pl.ds(s, 1), :], axis=-1, keepdims=True)
                    dq_pairs = dq_pairs + da_s * (ksc[h, pl.ds(s, 1), :] * dec)
                dk_pairs = jnp.zeros((C, HEAD), F32)
                for t in range(C):
                    do_t = dosc[h, pl.ds(t, 1), :]
                    qd = qsc[h, pl.ds(t, 1), :] * jnp.exp(bsc[h, pl.ds(t, 1), :] - b + anti[t])
                    da_t = jnp.sum(vv * do_t, axis=-1, keepdims=True)
                    dk_pairs = dk_pairs + da_t * qd
                    a_t = jnp.sum(qd * kk, axis=-1, keepdims=True)
                    dv = dv + a_t * do_t
                db = dqe * qe - dke * ke + qv * dq_pairs - kk * dk_pairs
                db_blk[rows, cols] = db + jnp.where(tt == C - 1, db_last, 0.0)
                dk_blk[rows, cols] = dke * ebb + dk_pairs
                dq_blk[rows, cols] = dqe * eb + dq_pairs
                dv_blk[rows, cols] = dv

        for first in range(0, N_HEADS, HG_GROUP):
            heads = tuple(range(first, first + HG_GROUP))
            pl.loop(0, nsub, unroll=2)(lambda j, heads=heads: sub(j, heads))

        dlg = _running_sum(_chunk_tri(tb, C, True), db_blk[...])
        lbv = lb_ref[...]
        sg, f, _, _ = _hg_gates(fz_ref[...], lbv)
        w = dlg / f - dk_blk[...]
        dp_ref[:, 0:D_TOK] = dq_blk[...].astype(dp_ref.dtype)
        dp_ref[:, 2 * D_TOK:3 * D_TOK] = dv_blk[...].astype(dp_ref.dtype)
        dp_ref[:, D_TOK:2 * D_TOK] = (w * (1.0 - lbv) * sg * (1.0 - sg)).astype(dp_ref.dtype)
        dlb_ref[...] += jnp.sum(w * (1.0 - sg), axis=0, keepdims=True)

    blk = lambda c: pl.BlockSpec((tb, D_TOK), lambda i, c=c: (nblk - 1 - i, c))
    vec = pl.BlockSpec((1, D_TOK), lambda i: (0, 0))
    stb = pl.BlockSpec((nsub, N_HEADS, HEAD, HEAD), lambda i: (nblk - 1 - i, 0, 0, 0))
    small = jax.ShapeDtypeStruct((1, D_TOK), F32)
    return pl.pallas_call(
        body, name=name, grid=(nblk,), in_specs=[blk(0), blk(1), blk(2), blk(3), vec, vec, blk(0), stb, blk(0)],
        out_specs=[pl.BlockSpec((tb, 4 * D_TOK), lambda i: (nblk - 1 - i, 0)), vec, vec],
        out_shape=[jax.ShapeDtypeStruct((S, HG_IN), BF16), small, small],
        scratch_shapes=[pltpu.VMEM((N_HEADS, HEAD, HEAD), F32)] + [pltpu.VMEM((tb, D_TOK), F32)] * 7
        + [pltpu.VMEM((N_HEADS, C, HEAD), F32)] * 5 + [pltpu.VMEM((C, C, HEAD), F32)] * 2,
        compiler_params=_params(("arbitrary",)))(p, p, p, p, lb, onorm, o, states, dheads)


def _adamw(w, g, m, v, *, name):
    shape = w.shape
    cols = shape[-1]
    w2, g2, m2, v2 = (t.reshape(-1, cols) for t in (w, g, m, v))
    R = w2.shape[0]
    tr = _tile(R, 512, 8)

    def body(w_ref, g_ref, m_ref, v_ref, d_ref, nm_ref, nv_ref):
        gv = g_ref[...]
        nm = ADAM_B1 * m_ref[...] + (1.0 - ADAM_B1) * gv
        nv = ADAM_B2 * v_ref[...] + (1.0 - ADAM_B2) * (gv * gv)
        m_hat = nm / (1.0 - ADAM_B1 ** ADAM_STEP)
        v_hat = nv / (1.0 - ADAM_B2 ** ADAM_STEP)
        d_ref[...] = -ADAM_LR * (m_hat / (jnp.sqrt(v_hat) + ADAM_EPS) + ADAM_WD * w_ref[...])
        nm_ref[...] = nm
        nv_ref[...] = nv

    spec = pl.BlockSpec((tr, cols), lambda i: (i, 0))
    out = jax.ShapeDtypeStruct((R, cols), F32)
    d, nm, nv = pl.pallas_call(body, name=name, grid=(R // tr,), in_specs=[spec] * 4, out_specs=[spec] * 3,
                               out_shape=[out] * 3, compiler_params=_params(("parallel",)))(w2, g2, m2, v2)
    return d.reshape(shape), nm.reshape(shape), nv.reshape(shape)


def _add_received(sent, got, me, *, name):
    n, R, Cc = got.shape
    tr = _tile(R, 256, 16)
    per = R // tr

    def body(me_ref, a_ref, b_ref, o_ref):
        del me_ref
        acc = a_ref[...].astype(F32)
        for k in range(n):
            acc = acc + b_ref[k].astype(F32)
        o_ref[...] = acc

    grid_spec = pltpu.PrefetchScalarGridSpec(
        num_scalar_prefetch=1, grid=(per,),
        in_specs=[pl.BlockSpec((tr, Cc), lambda i, me_ref: (me_ref[0] * per + i, 0)),
                  pl.BlockSpec((n, tr, Cc), lambda i, me_ref: (0, i, 0))],
        out_specs=pl.BlockSpec((tr, Cc), lambda i, me_ref: (i, 0)))
    return pl.pallas_call(body, name=name, grid_spec=grid_spec, out_shape=jax.ShapeDtypeStruct((R, Cc), F32),
                          compiler_params=_params(("parallel",)))(jnp.reshape(me, (1,)).astype(jnp.int32), sent, got)


def _sum_blocks(x, *, name):
    n, R, Cc = x.shape
    tr = _tile(R, 208, 8)

    def body(x_ref, o_ref):
        acc = x_ref[0]
        for k in range(1, n):
            acc = acc + x_ref[k]
        o_ref[...] = acc

    return pl.pallas_call(
        body, name=name, grid=(R // tr,), in_specs=[pl.BlockSpec((n, tr, Cc), lambda i: (0, i, 0))],
        out_specs=pl.BlockSpec((tr, Cc), lambda i: (i, 0)), out_shape=jax.ShapeDtypeStruct((R, Cc), F32),
        compiler_params=_params(("parallel",)))(x)


def _place():
    return lax.axis_index("x"), lax.axis_index("y"), lax.axis_index("c")


def _all_gather(x, *, name, in_vmem, reduce_sum=False, with_token=False):
    R, Cc = x.shape
    space = pltpu.VMEM if in_vmem else pl.ANY

    def body(x_ref, out_ref, *scratch):
        if with_token:
            scratch[0][...] = jnp.zeros_like(scratch[0])
            scratch = scratch[1:]
        if reduce_sum:
            gat_ref, send_sems, recv_sems, local_sem = scratch
        else:
            gat_ref = out_ref
            send_sems, recv_sems, local_sem = scratch
        mx, my, mc = _place()
        me, sibling = (mx, my, mc), (mx, my, 1 - mc)
        chips = [(1 - mx, my), (mx, 1 - my), (1 - mx, 1 - my)]

        def rows(px, py, pc):
            return gat_ref.at[pl.ds((4 * px + 2 * py + pc) * R, R), :]

        def copy(k, block, to, src=None):
            return pltpu.make_async_remote_copy(
                src_ref=rows(*block) if src is None else src, dst_ref=rows(*block), send_sem=send_sems.at[k],
                recv_sem=recv_sems.at[k], device_id=to, device_id_type=MESH_ID)

        mine = pltpu.make_async_copy(x_ref, rows(*me), local_sem)
        mine.start()
        first = [copy(0, me, sibling, src=x_ref)]
        first += [copy(1 + j, me, (*chip, mc), src=x_ref) for j, chip in enumerate(chips)]
        for cp in first:
            cp.start()
        passed = [copy(4 + j, (*chip, mc), sibling) for j, chip in enumerate(chips)]
        for j, chip in enumerate(chips):
            copy(1 + j, (*chip, mc), me).wait_recv()
            passed[j].start()
        copy(0, sibling, me).wait_recv()
        for j, chip in enumerate(chips):
            copy(4 + j, (*chip, 1 - mc), me).wait_recv()
        for cp in first + passed:
            cp.wait_send()
        mine.wait()
        if reduce_sum:
            acc = gat_ref[pl.ds(0, R), :]
            for d in range(1, N_DEV):
                acc = acc + gat_ref[pl.ds(d * R, R), :]
            out_ref[...] = acc

    sems = [pltpu.SemaphoreType.DMA((7,)), pltpu.SemaphoreType.DMA((7,)), pltpu.SemaphoreType.DMA]
    if reduce_sum:
        assert in_vmem
        out_shape = jax.ShapeDtypeStruct((R, Cc), x.dtype)
        scratch = [pltpu.VMEM((N_DEV * R, Cc), x.dtype)] + sems
    else:
        out_shape = jax.ShapeDtypeStruct((N_DEV * R, Cc), x.dtype)
        scratch = sems
    out_specs = pl.BlockSpec(memory_space=space)
    if with_token:
        out_shape = (out_shape, jax.ShapeDtypeStruct((8, LANE), F32))
        out_specs = (out_specs, pl.BlockSpec(memory_space=pltpu.VMEM))
    return pl.pallas_call(
        body, name=name, out_shape=out_shape, in_specs=[pl.BlockSpec(memory_space=space)], out_specs=out_specs,
        scratch_shapes=scratch, compiler_params=pltpu.CompilerParams(vmem_limit_bytes=VMEM_LIMIT))(x)


def _peer(k, mx, my, mc):
    bits = k + 1
    return (1 - mx if bits & 4 else mx, 1 - my if bits & 2 else my, 1 - mc if bits & 1 else mc)


HBM_SPEC = pl.BlockSpec(memory_space=pltpu.HBM)
SEM_SPEC = pl.BlockSpec(memory_space=pltpu.SEMAPHORE)
DATAFLOW = pltpu.SideEffectType.DATAFLOW_SIDE_EFFECTING


def _exchange_copies(x_refs, land_refs, send_sems, recv_sems, scatter):
    mx, my, mc = _place()
    me = 4 * mx + 2 * my + mc
    n = len(x_refs)
    copies = []
    for k in range(N_DEV - 1):
        px, py, pc = _peer(k, mx, my, mc)
        for m, (x_ref, land_ref) in enumerate(zip(x_refs, land_refs)):
            rows = land_ref.shape[1] if scatter else x_ref.shape[0]
            if scatter:
                src = x_ref.at[pl.ds(pl.multiple_of((4 * px + 2 * py + pc) * rows, 16), rows), :]
                dst = land_ref.at[k]
            else:
                src = x_ref
                dst = land_ref.at[pl.ds(pl.multiple_of(me * rows, 16), rows), :]
            copies.append(pltpu.make_async_remote_copy(
                src_ref=src, dst_ref=dst, send_sem=send_sems.at[k * n + m], recv_sem=recv_sems.at[k * n + m],
                device_id=(px, py, pc), device_id_type=MESH_ID))
    return copies


def _land_shape(x, scatter):
    return (N_DEV - 1, x.shape[0] // N_DEV, x.shape[1]) if scatter else (N_DEV * x.shape[0], x.shape[1])


def _own_copies(x_refs, land_refs, local_sems):
    mx, my, mc = _place()
    me = 4 * mx + 2 * my + mc
    return [pltpu.make_async_copy(
        x_ref, land_ref.at[pl.ds(pl.multiple_of(me * x_ref.shape[0], 16), x_ref.shape[0]), :], local_sems.at[m])
        for m, (x_ref, land_ref) in enumerate(zip(x_refs, land_refs))]


def _exchange_start(groups, *, name, scatter):
    sizes = [len(g) for g in groups]
    xs = [x for g in groups for x in g]
    n = len(xs)
    lands = [lax.empty(_land_shape(x, scatter), x.dtype) for x in xs]
    per = 2 if scatter else 3

    def body(*refs):
        sems = refs[2 * n:2 * n + per * len(groups)]
        token = refs[-1]
        off = 0
        for gi, m in enumerate(sizes):
            x_refs, land_refs = refs[off:off + m], refs[n + off:n + off + m]
            for cp in _exchange_copies(x_refs, land_refs, sems[per * gi], sems[per * gi + 1], scatter):
                cp.start()
            if not scatter:
                for cp in _own_copies(x_refs, land_refs, sems[per * gi + 2]):
                    cp.start()
            off += m
        token[...] = jnp.zeros_like(token)

    sem_shapes = []
    for m in sizes:
        sem_shapes += [pltpu.SemaphoreType.DMA(((N_DEV - 1) * m,))] * 2
        if not scatter:
            sem_shapes.append(pltpu.SemaphoreType.DMA((m,)))
    ns = len(sem_shapes)
    out = pl.pallas_call(
        body, name=name,
        out_shape=(*sem_shapes, *[pltpu.HBM(x.shape, x.dtype) for x in xs],
                   *[pltpu.HBM(l.shape, l.dtype) for l in lands], jax.ShapeDtypeStruct((8, LANE), F32)),
        in_specs=(HBM_SPEC,) * (2 * n),
        out_specs=(SEM_SPEC,) * ns + (HBM_SPEC,) * (2 * n) + (pl.BlockSpec(memory_space=pltpu.VMEM),),
        input_output_aliases={i: ns + i for i in range(2 * n)},
        compiler_params=pltpu.CompilerParams(has_side_effects=DATAFLOW))(
            *[pltpu.with_memory_space_constraint(t, pltpu.HBM) for t in xs + lands])
    started, off = [], 0
    for gi, m in enumerate(sizes):
        sems = out[per * gi:per * gi + per]
        started.append((sems[0], sems[1], list(out[ns + off:ns + off + m]),
                        list(out[ns + n + off:ns + n + off + m]), out[-1], None if scatter else sems[2]))
        off += m
    return started


def _exchange_wait(started, after, *, name, scatter):
    send_sems, recv_sems, xs, lands, _, local_sems = started
    n = len(xs)

    def body(*refs):
        x_refs, land_refs = refs[:n], refs[n:2 * n]
        for cp in _exchange_copies(x_refs, land_refs, refs[2 * n], refs[2 * n + 1], scatter):
            cp.wait_send()
            cp.wait_recv()
        if not scatter:
            for cp in _own_copies(x_refs, land_refs, refs[2 * n + 2]):
                cp.wait()

    sems = (send_sems, recv_sems) if scatter else (send_sems, recv_sems, local_sems)
    out = pl.pallas_call(
        body, name=name, out_shape=tuple(pltpu.HBM(t.shape, t.dtype) for t in xs + lands),
        in_specs=(HBM_SPEC,) * (2 * n) + (SEM_SPEC,) * len(sems) + (pl.BlockSpec(memory_space=pl.ANY),),
        out_specs=(HBM_SPEC,) * (2 * n), input_output_aliases={i: i for i in range(2 * n)},
        compiler_params=pltpu.CompilerParams(has_side_effects=DATAFLOW))(*xs, *lands, *sems, after)
    return list(out[:n]), list(out[n:])


def _gather_start(groups, token, *, name):
    first = groups[0]
    groups = [[first[0] + token[0, 0].astype(first[0].dtype)] + list(first[1:])] + [list(g) for g in groups[1:]]
    return _exchange_start(groups, name=name, scatter=False)


def _gather_finish(started, after, *, name):
    return _exchange_wait(started, after, name=name, scatter=False)[1]


def _reduce_start(grads, *, name):
    return _exchange_start([grads], name=name, scatter=True)[0]


def _reduce_finish(started, after, me, *, name):
    sent, gots = _exchange_wait(started, after, name=name + "_wait", scatter=True)
    return [_add_received(g, got, me, name=f"{name}_add{m}") for m, (g, got) in enumerate(zip(sent, gots))]


def _pad_rows(a, mult):
    r = (-a.shape[0]) % mult
    return a if r == 0 else jnp.concatenate([a, jnp.zeros((r,) + a.shape[1:], a.dtype)], axis=0)


def kernel(x, mem, mix_norm, mem_norm, w_mem_kv, w_out, hg_w_in, hg_lb, hg_onorm, gm_w_in, gm_ln_g, gm_ln_b, gm_ws, gm_bs, ffn_norm, w_ffn_in, w_ffn_out, final_norm, loss_target, m_mix_norm, m_mem_norm, m_w_mem_kv, m_w_out, m_hg_w_in, m_hg_lb, m_hg_onorm, m_gm_w_in, m_gm_ln_g, m_gm_ln_b, m_gm_ws, m_gm_bs, m_ffn_norm, m_w_ffn_in, m_w_ffn_out, m_final_norm, v_mix_norm, v_mem_norm, v_w_mem_kv, v_w_out, v_hg_w_in, v_hg_lb, v_hg_onorm, v_gm_w_in, v_gm_ln_g, v_gm_ln_b, v_gm_ws, v_gm_bs, v_ffn_norm, v_w_ffn_in, v_w_ffn_out, v_final_norm):
    mx, my, mc = _place()
    me = 4 * mx + 2 * my + mc
    xs = x[0]
    mems = mem[0]
    tgt = loss_target[0]

    hg_t = hg_w_in[0].T.astype(BF16)
    gm_t = gm_w_in[0].T.astype(BF16)
    fi_t = [w_ffn_in[i].T.astype(BF16) for i in range(2)]
    kv_b = [w_mem_kv[i].astype(BF16) for i in range(2)]
    out_b = [w_out[i].astype(BF16) for i in range(2)]
    fo_b = [w_ffn_out[i].astype(BF16) for i in range(2)]
    W_hgT, token = _all_gather(hg_t, name="gather_first", in_vmem=False, with_token=True)
    ln_local = _pad_rows(jnp.concatenate([gm_ln_g, gm_ln_b], axis=0), 16)
    ln_local = jnp.concatenate([ln_local, jnp.zeros((16, LANE - ln_local.shape[1]), F32)], axis=1)
    gather_mix, fi0, fo0, gather_gm, fi1, fo1 = _gather_start(
        [kv_b + out_b, [fi_t[0]], [fo_b[0]], [gm_t, ln_local], [fi_t[1]], [fo_b[1]]], token,
        name="gather_rest_start")
    gather_fi, gather_fo = [fi0, fi1], [fo0, fo1]

    lb_soft = jax.nn.softmax(hg_lb, axis=0)
    lb0 = lb_soft[0:1]
    bsb = jnp.broadcast_to(gm_bs[0][:, :, None], (N_HEADS, GM_CHUNK, GM_CHUNK))
    ws = gm_ws[0]

    W_fiT, W_fo = [], []

    def ffn_fwd(xin, hf, i, **tail):
        W_fiT.extend(_gather_finish(gather_fi[i], hf, name=f"gather_fi{i}_wait"))
        gu, act = _ffn_in(hf, W_fiT[i], name=f"ffn_in{i}")
        W_fo.extend(_gather_finish(gather_fo[i], act, name=f"gather_fo{i}_wait"))
        return gu, act, _matmul(act, W_fo[i], res=xin, name=f"ffn_out{i}", **tail)

    h0 = _rms_fwd(xs, mix_norm[0:1], name="mix_norm0", dep=gather_mix[4])
    p0 = _matmul(h0, W_hgT, tb=True, name="hg_in")
    heads0, o0, states = _hgrn2_fwd(p0, lb0, hg_onorm, name="hgrn2_fwd")

    kv0, kv1, wo0, wo1 = _gather_finish(gather_mix, o0, name="gather_mix_wait")
    W_kv, W_out = [kv0, kv1], [wo0, wo1]
    mem_n, kv = [], []
    for i in range(2):
        mn = _rms_fwd(mems, mem_norm[i:i + 1], name=f"mem_norm{i}")
        mem_n.append(mn)
        kv.append(_matmul(mn, W_kv[i], name=f"mem_kv{i}"))

    heads0 = _attn_fwd(p0, 4 * D_TOK // D_MEM, kv[0], heads0, name="attn_fwd0")
    x1, hf0 = _matmul(heads0, W_out[0], res=xs, norm_gain=ffn_norm[0:1], name="out_proj0")
    gu0, act0, (x2, h1) = ffn_fwd(x1, hf0, 0, norm_gain=mix_norm[1:2])

    W_gmT, ln_all = _gather_finish(gather_gm, h1, name="gather_gm_wait")
    ln_all = ln_all.reshape(N_DEV, 16, LANE)
    ln_g = ln_all[:, 0, :D_TOK // N_DEV].reshape(1, D_TOK)
    ln_b = ln_all[:, 1, :D_TOK // N_DEV].reshape(1, D_TOK)
    p1 = _matmul(h1, W_gmT, tb=True, name="gm_in")
    heads1 = _gmlp_fwd(p1, ln_g, ln_b, ws, bsb, name="gmlp_fwd")
    heads1 = _attn_fwd(p1, 2 * D_TOK // D_MEM, kv[1], heads1, name="attn_fwd1")
    x3, hf1 = _matmul(heads1, W_out[1], res=x2, norm_gain=ffn_norm[1:2], name="out_proj1")
    gu1, act1, (dx, g_final, loss_part) = ffn_fwd(x3, hf1, 1, loss_head=(final_norm.reshape(1, D_MODEL), tgt))

    def ffn_bwd(dx, xin, hf, gu, act, i, dep):
        dgu = _ffn_out_dx(dx, W_fo[i], gu, dep, name=f"ffn_out_dx{i}")
        g_wfo = _matmul(act, dx, ta=True, out_dtype=BF16, name=f"ffn_out_dw{i}")
        g_wfi_t = _matmul(dgu, hf, ta=True, a_halves=True, out_dtype=BF16, name=f"ffn_in_dw{i}")
        dx, g_norm = _matmul(dgu, W_fiT[i], a_halves=True, res=dx, norm_bwd=(xin, ffn_norm[i:i + 1]),
                             name=f"ffn_in_dx{i}")
        return dx, g_wfi_t, g_wfo, g_norm

    def mem_bwd(dkv, i):
        g_wkv = _matmul(mem_n[i], dkv, ta=True, out_dtype=BF16, name=f"mem_kv_dw{i}")
        dmn = _matmul(dkv, W_kv[i], tb=True, name=f"mem_kv_dx{i}")
        _, g_norm = _rms_bwd(mems, mem_norm[i:i + 1], dmn, jnp.zeros_like(mems), name=f"mem_norm_bwd{i}")
        return g_wkv, g_norm

    dx, g_wfi1_t, g_wfo1, g_ffn1 = ffn_bwd(dx, x3, hf1, gu1, act1, 1, loss_part)
    dheads = _matmul(dx, W_out[1], tb=True, name="out_proj_dx1")
    g_wout1 = _matmul(heads1, dx, ta=True, out_dtype=BF16, name="out_proj_dw1")
    dp, g_ws, g_bs, g_lng, g_lnb = _gmlp_bwd(p1, ln_g, ln_b, ws, bsb, dheads, name="gmlp_bwd")
    dp, dk, dv = _attn_bwd(p1, 2 * D_TOK // D_MEM, kv[1], dheads, dp, name="attn_bwd1")
    g_wkv1, g_mem1 = mem_bwd(jnp.concatenate([dk, dv], axis=1), 1)
    g_wgm_t = _matmul(dp, h1, ta=True, out_dtype=BF16, name="gm_in_dw")
    dx, g_mix1 = _matmul(dp, W_gmT, res=dx, norm_bwd=(x2, mix_norm[1:2]), name="gm_in_dx")
    reduce_l1 = _reduce_start([g_wkv1, g_wout1, g_wgm_t, g_wfi1_t, g_wfo1], name="reduce_l1_start")
    early = [loss_part, g_final, g_ffn1, g_mix1, g_mem1, g_ws.reshape(1, -1), g_bs.reshape(1, -1), g_lng, g_lnb]
    early_rows = _pad_rows(jnp.concatenate(early, axis=1).reshape(-1, LANE), 16) + reduce_l1[4][0, 0]
    small_early = _exchange_start([[early_rows]], name="reduce_small_start", scatter=False)[0]

    dx, g_wfi0_t, g_wfo0, g_ffn0 = ffn_bwd(dx, x1, hf0, gu0, act0, 0, small_early[4])
    reduce_ffn0 = _reduce_start([g_wfi0_t, g_wfo0], name="reduce_ffn0_start")
    dheads = _matmul(dx, W_out[0], tb=True, name="out_proj_dx0", dep=reduce_ffn0[4])
    g_wout0 = _matmul(heads0, dx, ta=True, out_dtype=BF16, name="out_proj_dw0")
    dp, g_lb0, g_onorm = _hgrn2_bwd(p0, lb0, hg_onorm, o0, states, dheads, name="hgrn2_bwd")
    dp, dk, dv = _attn_bwd(p0, 4 * D_TOK // D_MEM, kv[0], dheads, dp, name="attn_bwd0")
    g_wkv0, g_mem0 = mem_bwd(jnp.concatenate([dk, dv], axis=1), 0)
    g_whg_t = _matmul(dp, h0, ta=True, out_dtype=BF16, name="hg_in_dw")
    reduce_mix0 = _reduce_start([g_wkv0, g_wout0, g_whg_t], name="reduce_mix0_start")
    grad_x, g_mix0 = _matmul(dp, W_hgT, res=dx, norm_bwd=(xs, mix_norm[0:1]), name="hg_in_dx", dep=reduce_mix0[4])

    g_kv1, g_out1, g_gm_t, g_fi1_t, g_fo1 = _reduce_finish(reduce_l1, grad_x, me, name="reduce_l1")
    g_fi0_t, g_fo0 = _reduce_finish(reduce_ffn0, g_kv1, me, name="reduce_ffn0")
    g_kv0, g_out0, g_hg_t = _reduce_finish(reduce_mix0, g_fi0_t, me, name="reduce_mix0")
    g_shards = [jnp.stack([g_kv0, g_kv1]), jnp.stack([g_out0, g_out1]), g_hg_t[None], g_gm_t[None],
                jnp.stack([g_fi0_t, g_fi1_t]), jnp.stack([g_fo0, g_fo1])]
    transposed = (4, 7, 13)

    late = [g_ffn0, g_lb0, g_onorm, g_mem0, g_mix0]
    late_rows = _pad_rows(jnp.concatenate(late, axis=1).reshape(-1, LANE), 8)
    red_late = _all_gather(late_rows, name="reduce_small_late", in_vmem=True, reduce_sum=True)
    gathered = _exchange_wait(small_early, red_late, name="reduce_small_wait", scatter=False)[1][0]
    red_early = _sum_blocks(gathered.reshape(N_DEV, -1, LANE), name="reduce_small_sum")

    def split(flat, parts):
        out, off = [], 0
        for t in parts:
            out.append(flat[off:off + t.shape[1]])
            off += t.shape[1]
        return out

    r_loss, r_final, r_ffn1, r_mix1, r_mem1, r_ws, r_bs, r_lng, r_lnb = split(red_early.reshape(-1), early)
    r_ffn0, r_lb0, r_onorm, r_mem0, r_mix0 = split(red_late.reshape(-1), late)
    loss = r_loss[0]
    g_mix_norm = jnp.stack([r_mix0, r_mix1])
    g_mem_norm = jnp.stack([r_mem0, r_mem1])
    g_hg_lb = r_lb0[None, :] * lb0 * (jnp.eye(3, dtype=F32)[:, 0:1] - lb_soft)
    g_hg_onorm = r_onorm.reshape(1, D_TOK)
    width = D_TOK // N_DEV
    g_gm_ln_g = lax.dynamic_slice(r_lng, (me * width,), (width,)).reshape(1, width)
    g_gm_ln_b = lax.dynamic_slice(r_lnb, (me * width,), (width,)).reshape(1, width)
    g_gm_ws = r_ws.reshape(gm_ws.shape)
    g_gm_bs = r_bs.reshape(gm_bs.shape)
    g_ffn_norm = jnp.stack([r_ffn0, r_ffn1])
    g_final_norm = r_final

    grads = [g_mix_norm, g_mem_norm, g_shards[0], g_shards[1], g_shards[2], g_hg_lb, g_hg_onorm, g_shards[3],
             g_gm_ln_g, g_gm_ln_b, g_gm_ws, g_gm_bs, g_ffn_norm, g_shards[4], g_shards[5], g_final_norm]
    weights = [mix_norm, mem_norm, w_mem_kv, w_out, hg_w_in, hg_lb, hg_onorm, gm_w_in, gm_ln_g, gm_ln_b, gm_ws, gm_bs,
               ffn_norm, w_ffn_in, w_ffn_out, final_norm]
    ms = [m_mix_norm, m_mem_norm, m_w_mem_kv, m_w_out, m_hg_w_in, m_hg_lb, m_hg_onorm, m_gm_w_in, m_gm_ln_g,
          m_gm_ln_b, m_gm_ws, m_gm_bs, m_ffn_norm, m_w_ffn_in, m_w_ffn_out, m_final_norm]
    vs = [v_mix_norm, v_mem_norm, v_w_mem_kv, v_w_out, v_hg_w_in, v_hg_lb, v_hg_onorm, v_gm_w_in, v_gm_ln_g,
          v_gm_ln_b, v_gm_ws, v_gm_bs, v_ffn_norm, v_w_ffn_in, v_w_ffn_out, v_final_norm]
    deltas, new_m, new_v = [], [], []
    for n, (w, g, m, v) in enumerate(zip(weights, grads, ms, vs)):
        if w.ndim == 1:
            d, nm, nv = _adamw(w[None], g.reshape(1, -1), m[None], v[None], name=f"adamw{n}")
            d, nm, nv = d[0], nm[0], nv[0]
        elif n in transposed:
            flip = lambda t: jnp.swapaxes(t, 1, 2)
            d, nm, nv = (flip(t) for t in _adamw(flip(w), g, flip(m), flip(v), name=f"adamw{n}"))
            grads[n] = flip(g)
        else:
            d, nm, nv = _adamw(w, g.reshape(w.shape), m, v, name=f"adamw{n}")
        deltas.append(d)
        new_m.append(nm)
        new_v.append(nv)
    grads = [g.reshape(w.shape) for g, w in zip(grads, weights)]
    return (loss, grad_x[None], *grads, *deltas, *new_m, *new_v)
```

```python
import jax
import jax.numpy as jnp
from jax import lax
from jax.experimental import pallas as pl
from jax.experimental.pallas import tpu as pltpu

F32 = jnp.float32
BF16 = jnp.bfloat16
MXU_DTYPE = jnp.bfloat16
MESH_ID = pl.DeviceIdType.MESH

N_DEV = 8
EPS = 1e-6
D_MODEL = 1024
D_TOK = 768
D_MEM = 256
N_HEADS = 6
HEAD = 128
MEM_HEADS = 4
MEM_HDIM = 64
GM_CHUNK = 128
GM_STEP = 4
GM_STEP_BWD = 2
ATTN_ROWS = 2048
D_FF = 2816
HG_SUB = 16
HG_GROUP = 6
HG_IN = 4 * D_TOK + D_MEM
GM_IN = 2 * D_TOK + D_MEM
LANE = 128
MXU_COLS = 256

ADAM_LR = 0.001
ADAM_B1 = 0.9
ADAM_B2 = 0.999
ADAM_EPS = 1e-08
ADAM_WD = 0.01
ADAM_STEP = 10

VMEM_LIMIT = 48 * 2 ** 20
VMEM_LIMIT_WIDE = 58 * 2 ** 20


def _params(sem=None, limit=VMEM_LIMIT):
    return pltpu.CompilerParams(dimension_semantics=sem, vmem_limit_bytes=limit)


def _tile(n, cap, q=LANE):
    if n <= cap:
        return n
    best = None
    for t in range(q, cap + 1, q):
        if n % t == 0:
            best = t
    assert best is not None, (n, cap, q)
    return best


def _sigmoid(x):
    return 1.0 / (1.0 + jnp.exp(-x))


def _gelu(x, with_grad=False):
    cdf = 0.5 * (1.0 + lax.erf(x * 0.7071067811865476))
    if not with_grad:
        return x * cdf
    return x * cdf, cdf + x * jnp.exp(-0.5 * x * x) * 0.3989422804014327


def _matmul(a, b, *, name, ta=False, tb=False, res=None, out_dtype=F32, a_halves=False, b_halves=False, dep=None,
            norm_gain=None, norm_bwd=None, loss_head=None):
    if a_halves and ta:
        K, M = a.shape[1], 2 * a.shape[2]
    elif a_halves:
        M, K = a.shape[1], 2 * a.shape[2]
    else:
        K, M = a.shape if ta else a.shape[::-1]
    if b_halves:
        assert not tb and b.shape[1] == K
        N = 2 * b.shape[2]
    else:
        N = b.shape[0] if tb else b.shape[1]
        assert (b.shape[1] if tb else b.shape[0]) == K
    tm = _tile(M // 2 if (a_halves and ta) else M, 1664 if ta else 1024)
    tn = _tile(N // 2 if b_halves else N, 1792)
    tk = _tile(K // 2 if (a_halves and not ta) else K, 1024 if ta else 1664)
    nk = K // tk
    dims = (((0 if ta else 1,), (1 if tb else 0,)), ((), ()))

    strips = norm_bwd is not None or loss_head is not None
    fused = norm_gain is not None or strips
    n_in = 2 + (res is not None) + (norm_gain is not None) + 2 * strips + (dep is not None)
    if fused:
        assert tn == N, "the fused norm needs whole rows"
        assert (norm_gain is not None) + (norm_bwd is not None) + (loss_head is not None) == 1
        assert not strips or (res is not None and nk > 1 and tm % LANE == 0)

    def body(*refs):
        a_ref, b_ref = refs[:2]
        r_ref = refs[2] if res is not None else None
        g_ref = refs[2 + (res is not None)] if fused else None
        x_ref = refs[3 + (res is not None)] if strips else None
        o_ref = refs[n_in]
        h_ref = refs[n_in + 1] if fused else None
        l_ref = refs[n_in + 2] if loss_head is not None else None
        acc = None if nk == 1 else refs[-1]
        k = pl.program_id(2)

        def product():
            return lax.dot_general(a_ref[...].astype(MXU_DTYPE), b_ref[...].astype(MXU_DTYPE), dims,
                                   preferred_element_type=F32)

        def finish(r):
            if loss_head is not None:
                @pl.when(pl.program_id(0) == 0)
                def _():
                    h_ref[...] = jnp.zeros_like(h_ref)
                    l_ref[...] = jnp.zeros_like(l_ref)

                acc[...] = r + r_ref[...]
                gv = g_ref[...]

                def strip(s, carry):
                    dg, loss = carry
                    rows = pl.ds(pl.multiple_of(s * LANE, LANE), LANE)
                    xv = acc[rows, :]
                    scale = lax.rsqrt(jnp.mean(xv * xv, axis=-1, keepdims=True) + EPS)
                    xh = xv * scale
                    err = xh * gv - x_ref[rows, :]
                    loss = loss + 0.5 * jnp.sum(jnp.mean(err * err, axis=-1, keepdims=True), axis=0, keepdims=True)
                    dy = err * (1.0 / N)
                    u = dy * gv
                    o_ref[rows, :] = scale * (u - xh * jnp.mean(u * xh, axis=-1, keepdims=True))
                    return dg + jnp.sum(dy * xh, axis=0, keepdims=True), loss

                dg, loss = lax.fori_loop(0, tm // LANE, strip, (jnp.zeros((1, N), F32), jnp.zeros((1, 1), F32)))
                h_ref[...] += dg
                l_ref[...] += jnp.broadcast_to(loss, l_ref.shape)
                return
            if norm_bwd is not None:
                @pl.when(pl.program_id(0) == 0)
                def _():
                    h_ref[...] = jnp.zeros_like(h_ref)

                acc[...] = r
                gv = g_ref[...]

                def strip(s, dg):
                    rows = pl.ds(pl.multiple_of(s * LANE, LANE), LANE)
                    rv = acc[rows, :]
                    xv = x_ref[rows, :]
                    scale = lax.rsqrt(jnp.mean(xv * xv, axis=-1, keepdims=True) + EPS)
                    xh = xv * scale
                    u = rv * gv
                    o_ref[rows, :] = r_ref[rows, :] + scale * (u - xh * jnp.mean(u * xh, axis=-1, keepdims=True))
                    return dg + jnp.sum(rv * xh, axis=0, keepdims=True)

                h_ref[...] += lax.fori_loop(0, tm // LANE, strip, jnp.zeros((1, N), F32))
                return
            if res is not None:
                r = r + r_ref[...].astype(F32)
            o_ref[...] = r.astype(out_dtype)
            if norm_gain is not None:
                scale = lax.rsqrt(jnp.mean(r * r, axis=-1, keepdims=True) + EPS)
                h_ref[...] = (r * scale * g_ref[...]).astype(h_ref.dtype)

        if nk == 1:
            finish(product())
            return

        @pl.when(k == 0)
        def _():
            acc[...] = product()

        @pl.when((k > 0) & (k < nk - 1))
        def _():
            acc[...] += product()

        @pl.when(k == nk - 1)
        def _():
            finish(acc[...] + product())

    if a_halves and ta:
        mh = M // 2 // tm
        a_spec = pl.BlockSpec((None, tk, tm), lambda i, j, k: (i // mh, k, i % mh))
    elif a_halves:
        kh = nk // 2
        a_spec = pl.BlockSpec((None, tm, tk), lambda i, j, k: (k // kh, i, k % kh))
    elif ta:
        a_spec = pl.BlockSpec((tk, tm), lambda i, j, k: (k, i))
    else:
        a_spec = pl.BlockSpec((tm, tk), lambda i, j, k: (i, k))
    if b_halves:
        nh = N // 2 // tn
        b_spec = pl.BlockSpec((None, tk, tn), lambda i, j, k: (j // nh, k, j % nh))
    elif tb:
        b_spec = pl.BlockSpec((tn, tk), lambda i, j, k: (j, k))
    else:
        b_spec = pl.BlockSpec((tk, tn), lambda i, j, k: (k, j))
    o_spec = pl.BlockSpec((tm, tn), lambda i, j, k: (i, j))
    in_specs = [a_spec, b_spec] + ([o_spec] if res is not None else [])
    args = (a, b) + ((res,) if res is not None else ())
    out_specs, out_shape = o_spec, jax.ShapeDtypeStruct((M, N), out_dtype)
    vec = pl.BlockSpec((1, N), lambda i, j, k: (0, 0))
    sem = ("parallel", "parallel", "arbitrary")
    if norm_gain is not None:
        in_specs.append(vec)
        args += (norm_gain,)
        out_specs, out_shape = [o_spec, o_spec], [out_shape, jax.ShapeDtypeStruct((M, N), BF16)]
    if norm_bwd is not None:
        x_in, gain = norm_bwd
        in_specs += [vec, o_spec]
        args += (gain, x_in)
        out_specs, out_shape = [o_spec, vec], [out_shape, jax.ShapeDtypeStruct((1, N), F32)]
        sem = ("arbitrary", "arbitrary", "arbitrary")
    if loss_head is not None:
        gain, target = loss_head
        in_specs += [vec, o_spec]
        args += (gain, target)
        one = pl.BlockSpec((1, LANE), lambda i, j, k: (0, 0))
        out_specs = [o_spec, vec, one]
        out_shape = [out_shape, jax.ShapeDtypeStruct((1, N), F32), jax.ShapeDtypeStruct((1, LANE), F32)]
        sem = ("arbitrary", "arbitrary", "arbitrary")
    if dep is not None:
        in_specs.append(pl.BlockSpec(memory_space=pl.ANY))
        args += (dep,)
    return pl.pallas_call(
        body, name=name, grid=(M // tm, N // tn, nk), in_specs=in_specs, out_specs=out_specs, out_shape=out_shape,
        scratch_shapes=[] if nk == 1 else [pltpu.VMEM((tm, tn), F32)],
        compiler_params=_params(sem, VMEM_LIMIT_WIDE if strips else VMEM_LIMIT))(*args)


def _ffn_in(hf, wt, *, name):
    S, K = hf.shape
    tm = _tile(S, 512)
    tn = _tile(D_FF, 1408)
    nh = D_FF // tn
    nt = (((1,), (1,)), ((), ()))

    def body(a_ref, bg_ref, bu_ref, gu_ref, act_ref):
        av = a_ref[...].astype(MXU_DTYPE)
        for c0 in range(0, tn, MXU_COLS):
            cs = slice(c0, min(c0 + MXU_COLS, tn))
            gate = lax.dot_general(av, bg_ref[cs, :].astype(MXU_DTYPE), nt, preferred_element_type=F32)
            up = lax.dot_general(av, bu_ref[cs, :].astype(MXU_DTYPE), nt, preferred_element_type=F32)
            gu_ref[0, :, cs] = gate.astype(gu_ref.dtype)
            gu_ref[1, :, cs] = up.astype(gu_ref.dtype)
            act_ref[:, cs] = (gate * _sigmoid(gate) * up).astype(act_ref.dtype)

    return pl.pallas_call(
        body, name=name, grid=(nh, S // tm),
        in_specs=[pl.BlockSpec((tm, K), lambda j, i: (i, 0)), pl.BlockSpec((tn, K), lambda j, i: (j, 0)),
                  pl.BlockSpec((tn, K), lambda j, i: (j + nh, 0))],
        out_specs=[pl.BlockSpec((2, tm, tn), lambda j, i: (0, i, j)), pl.BlockSpec((tm, tn), lambda j, i: (i, j))],
        out_shape=[jax.ShapeDtypeStruct((2, S, D_FF), BF16), jax.ShapeDtypeStruct((S, D_FF), BF16)],
        compiler_params=_params(("parallel", "parallel")))(hf, wt, wt)


def _ffn_out_dx(dx, w, gu, dep, *, name):
    S, K = dx.shape
    tm = _tile(S, 1024)
    tn = _tile(D_FF, 1408)

    def body(a_ref, b_ref, gu_ref, dep_ref, o_ref):
        del dep_ref
        av = a_ref[...].astype(MXU_DTYPE)
        for c0 in range(0, tn, MXU_COLS):
            cs = slice(c0, min(c0 + MXU_COLS, tn))
            da = lax.dot_general(av, b_ref[cs, :].astype(MXU_DTYPE), (((1,), (1,)), ((), ())),
                                 preferred_element_type=F32)
            gate = gu_ref[0, :, cs].astype(F32)
            up = gu_ref[1, :, cs].astype(F32)
            sg = _sigmoid(gate)
            o_ref[0, :, cs] = (da * up * sg * (1.0 + gate * (1.0 - sg))).astype(o_ref.dtype)
            o_ref[1, :, cs] = (da * gate * sg).astype(o_ref.dtype)

    halves = pl.BlockSpec((2, tm, tn), lambda i, j: (0, i, j))
    return pl.pallas_call(
        body, name=name, grid=(S // tm, D_FF // tn),
        in_specs=[pl.BlockSpec((tm, K), lambda i, j: (i, 0)), pl.BlockSpec((tn, K), lambda i, j: (j, 0)), halves,
                  pl.BlockSpec(memory_space=pl.ANY)],
        out_specs=halves, out_shape=jax.ShapeDtypeStruct((2, S, D_FF), BF16),
        compiler_params=_params(("parallel", "parallel")))(dx, w, gu, dep)


def _rms_fwd(x, g, *, name, dep=None):
    R, Dm = x.shape
    tr = _tile(R, 512, 8)

    def body(x_ref, g_ref, *rest):
        o_ref = rest[-1]
        xv = x_ref[...]
        r = lax.rsqrt(jnp.mean(xv * xv, axis=-1, keepdims=True) + EPS)
        o_ref[...] = (xv * r * g_ref[...]).astype(o_ref.dtype)

    in_specs = [pl.BlockSpec((tr, Dm), lambda i: (i, 0)), pl.BlockSpec((1, Dm), lambda i: (0, 0))]
    args = (x, g)
    if dep is not None:
        in_specs.append(pl.BlockSpec(memory_space=pl.ANY))
        args += (dep,)
    return pl.pallas_call(
        body, name=name, grid=(R // tr,), in_specs=in_specs,
        out_specs=pl.BlockSpec((tr, Dm), lambda i: (i, 0)), out_shape=jax.ShapeDtypeStruct((R, Dm), BF16),
        compiler_params=_params(("parallel",)))(*args)


def _rms_bwd(x, g, dh, dres, *, name):
    R, Dm = x.shape
    tr = _tile(R, 256, 8)

    def body(x_ref, g_ref, dh_ref, dres_ref, dx_ref, dg_ref):
        @pl.when(pl.program_id(0) == 0)
        def _():
            dg_ref[...] = jnp.zeros_like(dg_ref)

        xv = x_ref[...]
        r = lax.rsqrt(jnp.mean(xv * xv, axis=-1, keepdims=True) + EPS)
        xh = xv * r
        dhv = dh_ref[...].astype(F32)
        dg_ref[...] += jnp.sum(dhv * xh, axis=0, keepdims=True)
        u = dhv * g_ref[...]
        dx = r * (u - xh * jnp.mean(u * xh, axis=-1, keepdims=True))
        dx_ref[...] = dres_ref[...] + dx

    row = pl.BlockSpec((tr, Dm), lambda i: (i, 0))
    vec = pl.BlockSpec((1, Dm), lambda i: (0, 0))
    return pl.pallas_call(
        body, name=name, grid=(R // tr,), in_specs=[row, vec, row, row], out_specs=[row, vec],
        out_shape=[jax.ShapeDtypeStruct((R, Dm), F32), jax.ShapeDtypeStruct((1, Dm), F32)],
        compiler_params=_params(("arbitrary",)))(x, g, dh, dres)


def _head_mask(h):
    lane = lax.broadcasted_iota(jnp.int32, (1, D_MEM), 1)
    return (lane >= h * MEM_HDIM) & (lane < (h + 1) * MEM_HDIM)


def _attn_probs(qv, k_mx, mask):
    s = lax.dot_general(jnp.where(mask, qv, 0.0).astype(MXU_DTYPE), k_mx, (((1,), (1,)), ((), ())),
                        preferred_element_type=F32) * (MEM_HDIM ** -0.5)
    e = jnp.exp(s - jnp.max(s, axis=-1, keepdims=True))
    return e / jnp.sum(e, axis=-1, keepdims=True)


def _attn_fwd(p, qcol, kv, heads, *, name):
    S = p.shape[0]
    M = kv.shape[0]
    ts = _tile(S, ATTN_ROWS, 8)

    def body(q_ref, k_ref, v_ref, heads_in, o_ref):
        del heads_in
        qv = q_ref[...]
        kx = k_ref[...].astype(MXU_DTYPE)
        vv = v_ref[...]
        out = jnp.zeros((ts, D_MEM), F32)
        for h in range(MEM_HEADS):
            mask = _head_mask(h)
            pr = _attn_probs(qv, kx, mask)
            out = out + jnp.dot(pr.astype(MXU_DTYPE), jnp.where(mask, vv, 0.0).astype(MXU_DTYPE),
                                preferred_element_type=F32)
        o_ref[...] = out.astype(o_ref.dtype)

    return pl.pallas_call(
        body, name=name, grid=(S // ts,),
        in_specs=[pl.BlockSpec((ts, D_MEM), lambda i: (i, qcol)), pl.BlockSpec((M, D_MEM), lambda i: (0, 0)),
                  pl.BlockSpec((M, D_MEM), lambda i: (0, 1)), pl.BlockSpec(memory_space=pl.ANY)],
        out_specs=pl.BlockSpec((ts, D_MEM), lambda i: (i, D_TOK // D_MEM)),
        out_shape=jax.ShapeDtypeStruct(heads.shape, heads.dtype), input_output_aliases={3: 0},
        compiler_params=_params(("parallel",)))(p, kv, kv, heads)


def _attn_bwd(p, qcol, kv, dheads, dp, *, name):
    S = p.shape[0]
    M = kv.shape[0]
    ts = _tile(S, ATTN_ROWS, 8)
    scale = MEM_HDIM ** -0.5

    def body(q_ref, k_ref, v_ref, do_ref, dp_in, dq_ref, dk_ref, dv_ref):
        del dp_in

        @pl.when(pl.program_id(0) == 0)
        def _():
            dk_ref[...] = jnp.zeros_like(dk_ref)
            dv_ref[...] = jnp.zeros_like(dv_ref)

        qv = q_ref[...]
        kv_ = k_ref[...]
        kx = kv_.astype(MXU_DTYPE)
        vv = v_ref[...]
        dox = do_ref[...].astype(MXU_DTYPE)
        qx = qv.astype(MXU_DTYPE)
        dq = jnp.zeros((ts, D_MEM), F32)
        for h in range(MEM_HEADS):
            mask = _head_mask(h)
            pr = _attn_probs(qv, kx, mask)
            vh = jnp.where(mask, vv, 0.0).astype(MXU_DTYPE)
            dpr = lax.dot_general(dox, vh, (((1,), (1,)), ((), ())), preferred_element_type=F32)
            ds = (pr * (dpr - jnp.sum(dpr * pr, axis=-1, keepdims=True)) * scale).astype(MXU_DTYPE)
            dq = dq + jnp.dot(ds, jnp.where(mask, kv_, 0.0).astype(MXU_DTYPE), preferred_element_type=F32)
            dk_h = lax.dot_general(ds, qx, (((0,), (0,)), ((), ())), preferred_element_type=F32)
            dv_h = lax.dot_general(pr.astype(MXU_DTYPE), dox, (((0,), (0,)), ((), ())), preferred_element_type=F32)
            dk_ref[...] += jnp.where(mask, dk_h, 0.0)
            dv_ref[...] += jnp.where(mask, dv_h, 0.0)
        dq_ref[...] = dq.astype(dq_ref.dtype)

    return pl.pallas_call(
        body, name=name, grid=(S // ts,),
        in_specs=[pl.BlockSpec((ts, D_MEM), lambda i: (i, qcol)), pl.BlockSpec((M, D_MEM), lambda i: (0, 0)),
                  pl.BlockSpec((M, D_MEM), lambda i: (0, 1)),
                  pl.BlockSpec((ts, D_MEM), lambda i: (i, D_TOK // D_MEM)), pl.BlockSpec(memory_space=pl.ANY)],
        out_specs=[pl.BlockSpec((ts, D_MEM), lambda i: (i, qcol)), pl.BlockSpec((M, D_MEM), lambda i: (0, 0)),
                   pl.BlockSpec((M, D_MEM), lambda i: (0, 0))],
        out_shape=[jax.ShapeDtypeStruct(dp.shape, dp.dtype), jax.ShapeDtypeStruct((M, D_MEM), F32),
                   jax.ShapeDtypeStruct((M, D_MEM), F32)],
        input_output_aliases={4: 0}, compiler_params=_params(("arbitrary",)))(p, kv, kv, dheads, dp)


def _gm_forward_parts(u_ref, v_ref, lng_ref, lnb_ref, w_ref, bsb_ref, with_grad=False):
    if with_grad:
        (zu, du_gelu), (zv, dv_gelu) = _gelu(u_ref[...], True), _gelu(v_ref[...], True)
    else:
        zu, zv, du_gelu, dv_gelu = _gelu(u_ref[...]), _gelu(v_ref[...]), None, None
    mu = jnp.mean(zv, axis=-1, keepdims=True)
    cen = zv - mu
    rs = lax.rsqrt(jnp.mean(cen * cen, axis=-1, keepdims=True) + EPS)
    vh = cen * rs
    vn = vh * lng_ref[...] + lnb_ref[...]
    row = lax.broadcasted_iota(jnp.int32, (GM_CHUNK, GM_CHUNK), 0)
    col = lax.broadcasted_iota(jnp.int32, (GM_CHUNK, GM_CHUNK), 1)
    tril = row >= col
    wm = [jnp.where(tril, w_ref[g], 0.0).astype(MXU_DTYPE) for g in range(N_HEADS)]
    vnx = [vn[:, g * HEAD:(g + 1) * HEAD].astype(MXU_DTYPE) for g in range(N_HEADS)]
    sv = [jnp.dot(wm[g], vnx[g], preferred_element_type=F32) + bsb_ref[g] for g in range(N_HEADS)]
    return zu, vh, rs, wm, vnx, sv, tril, du_gelu, dv_gelu


def _gmlp_fwd(p, lng, lnb, ws, bsb, *, name):
    S = p.shape[0]

    per_step = GM_STEP if S % (GM_STEP * GM_CHUNK) == 0 else 1

    def body(u_ref, v_ref, lng_ref, lnb_ref, w_ref, bsb_ref, o_ref):
        for c in range(per_step):
            rows = pl.ds(c * GM_CHUNK, GM_CHUNK)
            zu, _, _, _, _, sv, _, _, _ = _gm_forward_parts(u_ref.at[rows], v_ref.at[rows], lng_ref, lnb_ref, w_ref,
                                                            bsb_ref)
            for g in range(N_HEADS):
                o_ref[rows, g * HEAD:(g + 1) * HEAD] = (zu[:, g * HEAD:(g + 1) * HEAD] * sv[g]).astype(o_ref.dtype)

    blk = lambda c: pl.BlockSpec((per_step * GM_CHUNK, D_TOK), lambda i: (i, c))
    vec = pl.BlockSpec((1, D_TOK), lambda i: (0, 0))
    cube = pl.BlockSpec((N_HEADS, GM_CHUNK, GM_CHUNK), lambda i: (0, 0, 0))
    return pl.pallas_call(
        body, name=name, grid=(S // (per_step * GM_CHUNK),), in_specs=[blk(0), blk(1), vec, vec, cube, cube],
        out_specs=blk(0), out_shape=jax.ShapeDtypeStruct((S, D_MODEL), BF16),
        compiler_params=_params(("parallel",)))(p, p, lng, lnb, ws, bsb)


def _gmlp_bwd(p, lng, lnb, ws, bsb, dheads, *, name):
    S = p.shape[0]
    per_step = GM_STEP_BWD if S % (GM_STEP_BWD * GM_CHUNK) == 0 else 1

    def body(u_ref, v_ref, lng_ref, lnb_ref, w_ref, bsb_ref, dt_ref, dp_ref, dw_ref, dbs_ref, dlg_ref, dlb_ref):
        @pl.when(pl.program_id(0) == 0)
        def _():
            dw_ref[...] = jnp.zeros_like(dw_ref)
            dbs_ref[...] = jnp.zeros_like(dbs_ref)
            dlg_ref[...] = jnp.zeros_like(dlg_ref)
            dlb_ref[...] = jnp.zeros_like(dlb_ref)

        for c in range(per_step):
            rows = pl.ds(c * GM_CHUNK, GM_CHUNK)
            zu, vh, rs, wm, vnx, sv, tril, du_gelu, dv_gelu = _gm_forward_parts(
                u_ref.at[rows], v_ref.at[rows], lng_ref, lnb_ref, w_ref, bsb_ref, with_grad=True)
            dt = dt_ref[rows, :].astype(F32)
            dvn_parts = []
            for g in range(N_HEADS):
                sl = slice(g * HEAD, (g + 1) * HEAD)
                dsv = dt[:, sl] * zu[:, sl]
                dp_ref[rows, sl] = (dt[:, sl] * sv[g] * du_gelu[:, sl]).astype(dp_ref.dtype)
                dsx = dsv.astype(MXU_DTYPE)
                dw = lax.dot_general(dsx, vnx[g], (((1,), (1,)), ((), ())), preferred_element_type=F32)
                dw_ref[g] += jnp.where(tril, dw, 0.0)
                dbs_ref[g] += jnp.sum(dsv, axis=-1, keepdims=True)
                dvn_parts.append(lax.dot_general(wm[g], dsx, (((0,), (0,)), ((), ())), preferred_element_type=F32))
            dvn = jnp.concatenate(dvn_parts, axis=-1)
            dlg_ref[...] += jnp.sum(dvn * vh, axis=0, keepdims=True)
            dlb_ref[...] += jnp.sum(dvn, axis=0, keepdims=True)
            dvh = dvn * lng_ref[...]
            dzv = rs * (dvh - jnp.mean(dvh, axis=-1, keepdims=True)
                        - vh * jnp.mean(dvh * vh, axis=-1, keepdims=True))
            dp_ref[rows, D_TOK:] = (dzv * dv_gelu).astype(dp_ref.dtype)

    blk = lambda c: pl.BlockSpec((per_step * GM_CHUNK, D_TOK), lambda i: (i, c))
    vec = pl.BlockSpec((1, D_TOK), lambda i: (0, 0))
    cube = pl.BlockSpec((N_HEADS, GM_CHUNK, GM_CHUNK), lambda i: (0, 0, 0))
    col = pl.BlockSpec((N_HEADS, GM_CHUNK, 1), lambda i: (0, 0, 0))
    return pl.pallas_call(
        body, name=name, grid=(S // (per_step * GM_CHUNK),), in_specs=[blk(0), blk(1), vec, vec, cube, cube, blk(0)],
        out_specs=[pl.BlockSpec((per_step * GM_CHUNK, 2 * D_TOK), lambda i: (i, 0)), cube, col, vec, vec],
        out_shape=[jax.ShapeDtypeStruct((S, GM_IN), BF16), jax.ShapeDtypeStruct((N_HEADS, GM_CHUNK, GM_CHUNK), F32),
                   jax.ShapeDtypeStruct((N_HEADS, GM_CHUNK, 1), F32), jax.ShapeDtypeStruct((1, D_TOK), F32),
                   jax.ShapeDtypeStruct((1, D_TOK), F32)],
        compiler_params=_params(("arbitrary",)))(p, p, lng, lnb, ws, bsb, dheads)


def _chunk_tri(n, chunk, upper):
    r = lax.broadcasted_iota(jnp.int32, (n, n), 0)
    c = lax.broadcasted_iota(jnp.int32, (n, n), 1)
    same = (r // chunk) == (c // chunk)
    return jnp.where(same & ((r <= c) if upper else (r >= c)), 1.0, 0.0).astype(F32)


def _running_sum(tri, x):
    hi = x.astype(BF16)
    rest = x - hi.astype(F32)
    mid = rest.astype(BF16)
    lo = (rest - mid.astype(F32)).astype(BF16)
    tri = tri.astype(BF16)
    return (jnp.dot(tri, hi, preferred_element_type=F32) + jnp.dot(tri, mid, preferred_element_type=F32)
            + jnp.dot(tri, lo, preferred_element_type=F32))


MASKED = -1e30


def _pair_masks(mask_ref, n, upper):
    row = lax.broadcasted_iota(jnp.int32, (n, HEAD), 0)
    for i in range(n):
        mask_ref[i] = jnp.where((row <= i) if upper else (row >= i), 0.0, MASKED).astype(F32)


def _hg_gates(fz, lb):
    sg = _sigmoid(fz)
    f = lb + (1.0 - lb) * sg
    kk = (1.0 - lb) * (1.0 - sg)
    return sg, f, jnp.log(f), kk


def _hgrn2_fwd(p, lb, onorm, *, name):
    S = p.shape[0]
    C = HG_SUB
    tb = _tile(S, 256, C)
    nsub = tb // C

    def body(q_ref, fz_ref, v_ref, g_ref, lb_ref, on_ref, tok_ref, o_ref, st_ref, state, b_blk, k_blk, bsc, ksc, vsc):
        @pl.when(pl.program_id(0) == 0)
        def _():
            state[...] = jnp.zeros_like(state)

        _, _, lg, kk = _hg_gates(fz_ref[...], lb_ref[...])
        b_blk[...] = _running_sum(_chunk_tri(tb, C, False), lg)
        k_blk[...] = kk
        tt = lax.broadcasted_iota(jnp.int32, (C, HEAD), 0)

        def sub(c, carry):
            rows = pl.ds(pl.multiple_of(c * C, C), C)
            for h in range(N_HEADS):
                cols = slice(h * HEAD, (h + 1) * HEAD)
                qv = q_ref[rows, cols]
                vv = v_ref[rows, cols]
                b = b_blk[rows, cols]
                kk = k_blk[rows, cols]
                st0 = state[h]
                st0x = st0.astype(MXU_DTYPE)
                st_ref[c, h] = st0x.astype(st_ref.dtype)
                inter = lax.dot_general((qv * jnp.exp(b)).astype(MXU_DTYPE), st0x,
                                        (((1,), (1,)), ((), ())), preferred_element_type=F32)
                bsc[h] = b
                ksc[h] = kk
                vsc[h] = vv
                intra = jnp.zeros((C, HEAD), F32)
                for s in range(C):
                    dec = jnp.where(tt >= s, jnp.exp(b - bsc[h, pl.ds(s, 1), :]), 0.0)
                    a_s = jnp.sum(qv * ksc[h, pl.ds(s, 1), :] * dec, axis=-1, keepdims=True)
                    intra = intra + a_s * vsc[h, pl.ds(s, 1), :]
                o_ref[rows, cols] = inter + intra
                b_last = bsc[h, pl.ds(C - 1, 1), :]
                ke = kk * jnp.exp(b_last - b)
                state[h] = st0 * jnp.exp(b_last) + lax.dot_general(
                    vv.astype(MXU_DTYPE), ke.astype(MXU_DTYPE), (((0,), (0,)), ((), ())),
                    preferred_element_type=F32)
            return carry

        lax.fori_loop(0, nsub, sub, 0, unroll=2)

        for h in range(N_HEADS):
            cols = slice(h * HEAD, (h + 1) * HEAD)
            o = o_ref[:, cols]
            gv = g_ref[:, cols]
            n = o * lax.rsqrt(jnp.mean(o * o, axis=-1, keepdims=True) + EPS)
            tok_ref[:, cols] = (n * (gv * _sigmoid(gv)) * on_ref[:, cols]).astype(tok_ref.dtype)

    blk = lambda c: pl.BlockSpec((tb, D_TOK), lambda i, c=c: (i, c))
    vec = pl.BlockSpec((1, D_TOK), lambda i: (0, 0))
    stb = pl.BlockSpec((nsub, N_HEADS, HEAD, HEAD), lambda i: (i, 0, 0, 0))
    return pl.pallas_call(
        body, name=name, grid=(S // tb,), in_specs=[blk(0), blk(1), blk(2), blk(3), vec, vec],
        out_specs=[blk(0), blk(0), stb],
        out_shape=[jax.ShapeDtypeStruct((S, D_MODEL), BF16), jax.ShapeDtypeStruct((S, D_TOK), F32),
                   jax.ShapeDtypeStruct((S // C, N_HEADS, HEAD, HEAD), BF16)],
        scratch_shapes=[pltpu.VMEM((N_HEADS, HEAD, HEAD), F32)] + [pltpu.VMEM((tb, D_TOK), F32)] * 2
        + [pltpu.VMEM((N_HEADS, C, HEAD), F32)] * 3,
        compiler_params=_params(("arbitrary",)))(p, p, p, p, lb, onorm)


def _hgrn2_bwd(p, lb, onorm, o, states, dheads, *, name):
    S = p.shape[0]
    C = HG_SUB
    tb = _tile(S, 256, C)
    nsub = tb // C
    nblk = S // tb

    def body(q_ref, fz_ref, v_ref, g_ref, lb_ref, on_ref, o_ref, st_ref, dt_ref, dp_ref, dlb_ref, don_ref, dstate,
             b_blk, k_blk, do_blk, db_blk, dk_blk, dq_blk, dv_blk, bsc, ksc, vsc, qsc, dosc, causal, anti):
        @pl.when(pl.program_id(0) == 0)
        def _():
            dstate[...] = jnp.zeros_like(dstate)
            dlb_ref[...] = jnp.zeros_like(dlb_ref)
            don_ref[...] = jnp.zeros_like(don_ref)

        for h in range(N_HEADS):
            cols = slice(h * HEAD, (h + 1) * HEAD)
            onv = on_ref[:, cols]
            gv = g_ref[:, cols]
            ov = o_ref[:, cols]
            dt = dt_ref[:, cols].astype(F32)
            sgg = _sigmoid(gv)
            sil = gv * sgg
            rinv = lax.rsqrt(jnp.mean(ov * ov, axis=-1, keepdims=True) + EPS)
            n = ov * rinv
            don_ref[:, cols] += jnp.sum(dt * n * sil, axis=0, keepdims=True)
            dn = dt * sil * onv
            dp_ref[:, 3 * D_TOK + h * HEAD:3 * D_TOK + (h + 1) * HEAD] = (
                dt * n * onv * sgg * (1.0 + gv * (1.0 - sgg))).astype(dp_ref.dtype)
            do_blk[:, cols] = rinv * (dn - n * jnp.mean(dn * n, axis=-1, keepdims=True))
        _, _, lg, kk = _hg_gates(fz_ref[...], lb_ref[...])
        b_blk[...] = _running_sum(_chunk_tri(tb, C, False), lg)
        k_blk[...] = kk
        _pair_masks(causal, C, False)
        _pair_masks(anti, C, True)
        tt = lax.broadcasted_iota(jnp.int32, (C, HEAD), 0)

        def sub(j, heads):
            c = nsub - 1 - j
            rows = pl.ds(pl.multiple_of(c * C, C), C)
            for h in heads:
                cols = slice(h * HEAD, (h + 1) * HEAD)
                qv = q_ref[rows, cols]
                vv = v_ref[rows, cols]
                do = do_blk[rows, cols]
                b = b_blk[rows, cols]
                kk = k_blk[rows, cols]
                bsc[h] = b
                ksc[h] = kk
                vsc[h] = vv
                qsc[h] = qv
                dosc[h] = do
                b_last = bsc[h, pl.ds(C - 1, 1), :]
                eb = jnp.exp(b)
                qe = qv * eb
                ebb = jnp.exp(b_last - b)
                ke = kk * ebb
                e_last = jnp.exp(b_last)
                st0x = st_ref[c, h].astype(MXU_DTYPE)
                st0 = st0x.astype(F32)
                dst1 = dstate[h]
                dst1x = dst1.astype(MXU_DTYPE)
                dox = do.astype(MXU_DTYPE)
                dqe = jnp.dot(dox, st0x, preferred_element_type=F32)
                dke = jnp.dot(vv.astype(MXU_DTYPE), dst1x, preferred_element_type=F32)
                dv = lax.dot_general(ke.astype(MXU_DTYPE), dst1x, (((1,), (1,)), ((), ())),
                                     preferred_element_type=F32)
                db_last = (e_last * jnp.sum(st0 * dst1, axis=0, keepdims=True)
                           + jnp.sum(dke * ke, axis=0, keepdims=True))
                dstate[h] = dst1 * e_last + lax.dot_general(dox, qe.astype(MXU_DTYPE), (((0,), (0,)), ((), ())),
                                                            preferred_element_type=F32)
                dq_pairs = jnp.zeros((C, HEAD), F32)
                for s in range(C):
                    dec = jnp.exp(b - bsc[h, pl.ds(s, 1), :] + causal[s])
                    da_s = jnp.sum(do * vsc[h, pl.ds(s, 1), :], axis=-1, keepdims=True)
                    dq_pairs = dq_pairs + da_s * (ksc[h, pl.ds(s, 1), :] * dec)
                dk_pairs = jnp.zeros((C, HEAD), F32)
                for t in range(C):
                    do_t = dosc[h, pl.ds(t, 1), :]
                    qd = qsc[h, pl.ds(t, 1), :] * jnp.exp(bsc[h, pl.ds(t, 1), :] - b + anti[t])
                    da_t = jnp.sum(vv * do_t, axis=-1, keepdims=True)
                    dk_pairs = dk_pairs + da_t * qd
                    a_t = jnp.sum(qd * kk, axis=-1, keepdims=True)
                    dv = dv + a_t * do_t
                db = dqe * qe - dke * ke + qv * dq_pairs - kk * dk_pairs
                db_blk[rows, cols] = db + jnp.where(tt == C - 1, db_last, 0.0)
                dk_blk[rows, cols] = dke * ebb + dk_pairs
                dq_blk[rows, cols] = dqe * eb + dq_pairs
                dv_blk[rows, cols] = dv

        for first in range(0, N_HEADS, HG_GROUP):
            heads = tuple(range(first, first + HG_GROUP))
            pl.loop(0, nsub, unroll=2)(lambda j, heads=heads: sub(j, heads))

        dlg = _running_sum(_chunk_tri(tb, C, True), db_blk[...])
        lbv = lb_ref[...]
        sg, f, _, _ = _hg_gates(fz_ref[...], lbv)
        w = dlg / f - dk_blk[...]
        dp_ref[:, 0:D_TOK] = dq_blk[...].astype(dp_ref.dtype)
        dp_ref[:, 2 * D_TOK:3 * D_TOK] = dv_blk[...].astype(dp_ref.dtype)
        dp_ref[:, D_TOK:2 * D_TOK] = (w * (1.0 - lbv) * sg * (1.0 - sg)).astype(dp_ref.dtype)
        dlb_ref[...] += jnp.sum(w * (1.0 - sg), axis=0, keepdims=True)

    blk = lambda c: pl.BlockSpec((tb, D_TOK), lambda i, c=c: (nblk - 1 - i, c))
    vec = pl.BlockSpec((1, D_TOK), lambda i: (0, 0))
    stb = pl.BlockSpec((nsub, N_HEADS, HEAD, HEAD), lambda i: (nblk - 1 - i, 0, 0, 0))
    small = jax.ShapeDtypeStruct((1, D_TOK), F32)
    return pl.pallas_call(
        body, name=name, grid=(nblk,), in_specs=[blk(0), blk(1), blk(2), blk(3), vec, vec, blk(0), stb, blk(0)],
        out_specs=[pl.BlockSpec((tb, 4 * D_TOK), lambda i: (nblk - 1 - i, 0)), vec, vec],
        out_shape=[jax.ShapeDtypeStruct((S, HG_IN), BF16), small, small],
        scratch_shapes=[pltpu.VMEM((N_HEADS, HEAD, HEAD), F32)] + [pltpu.VMEM((tb, D_TOK), F32)] * 7
        + [pltpu.VMEM((N_HEADS, C, HEAD), F32)] * 5 + [pltpu.VMEM((C, C, HEAD), F32)] * 2,
        compiler_params=_params(("arbitrary",)))(p, p, p, p, lb, onorm, o, states, dheads)


def _adamw(w, g, m, v, *, name):
    shape = w.shape
    cols = shape[-1]
    w2, g2, m2, v2 = (t.reshape(-1, cols) for t in (w, g, m, v))
    R = w2.shape[0]
    tr = _tile(R, 512, 8)

    def body(w_ref, g_ref, m_ref, v_ref, d_ref, nm_ref, nv_ref):
        gv = g_ref[...]
        nm = ADAM_B1 * m_ref[...] + (1.0 - ADAM_B1) * gv
        nv = ADAM_B2 * v_ref[...] + (1.0 - ADAM_B2) * (gv * gv)
        m_hat = nm / (1.0 - ADAM_B1 ** ADAM_STEP)
        v_hat = nv / (1.0 - ADAM_B2 ** ADAM_STEP)
        d_ref[...] = -ADAM_LR * (m_hat / (jnp.sqrt(v_hat) + ADAM_EPS) + ADAM_WD * w_ref[...])
        nm_ref[...] = nm
        nv_ref[...] = nv

    spec = pl.BlockSpec((tr, cols), lambda i: (i, 0))
    out = jax.ShapeDtypeStruct((R, cols), F32)
    d, nm, nv = pl.pallas_call(body, name=name, grid=(R // tr,), in_specs=[spec] * 4, out_specs=[spec] * 3,
                               out_shape=[out] * 3, compiler_params=_params(("parallel",)))(w2, g2, m2, v2)
    return d.reshape(shape), nm.reshape(shape), nv.reshape(shape)


def _add_received(sent, got, me, *, name):
    n, R, Cc = got.shape
    tr = _tile(R, 256, 16)
    per = R // tr

    def body(me_ref, a_ref, b_ref, o_ref):
        del me_ref
        acc = a_ref[...].astype(F32)
        for k in range(n):
            acc = acc + b_ref[k].astype(F32)
        o_ref[...] = acc

    grid_spec = pltpu.PrefetchScalarGridSpec(
        num_scalar_prefetch=1, grid=(per,),
        in_specs=[pl.BlockSpec((tr, Cc), lambda i, me_ref: (me_ref[0] * per + i, 0)),
                  pl.BlockSpec((n, tr, Cc), lambda i, me_ref: (0, i, 0))],
        out_specs=pl.BlockSpec((tr, Cc), lambda i, me_ref: (i, 0)))
    return pl.pallas_call(body, name=name, grid_spec=grid_spec, out_shape=jax.ShapeDtypeStruct((R, Cc), F32),
                          compiler_params=_params(("parallel",)))(jnp.reshape(me, (1,)).astype(jnp.int32), sent, got)


def _sum_blocks(x, *, name):
    n, R, Cc = x.shape
    tr = _tile(R, 208, 8)

    def body(x_ref, o_ref):
        acc = x_ref[0]
        for k in range(1, n):
            acc = acc + x_ref[k]
        o_ref[...] = acc

    return pl.pallas_call(
        body, name=name, grid=(R // tr,), in_specs=[pl.BlockSpec((n, tr, Cc), lambda i: (0, i, 0))],
        out_specs=pl.BlockSpec((tr, Cc), lambda i: (i, 0)), out_shape=jax.ShapeDtypeStruct((R, Cc), F32),
        compiler_params=_params(("parallel",)))(x)


def _place():
    return lax.axis_index("x"), lax.axis_index("y"), lax.axis_index("c")


def _all_gather(x, *, name, in_vmem, reduce_sum=False, with_token=False):
    R, Cc = x.shape
    space = pltpu.VMEM if in_vmem else pl.ANY

    def body(x_ref, out_ref, *scratch):
        if with_token:
            scratch[0][...] = jnp.zeros_like(scratch[0])
            scratch = scratch[1:]
        if reduce_sum:
            gat_ref, send_sems, recv_sems, local_sem = scratch
        else:
            gat_ref = out_ref
            send_sems, recv_sems, local_sem = scratch
        mx, my, mc = _place()
        me, sibling = (mx, my, mc), (mx, my, 1 - mc)
        chips = [(1 - mx, my), (mx, 1 - my), (1 - mx, 1 - my)]

        def rows(px, py, pc):
            return gat_ref.at[pl.ds((4 * px + 2 * py + pc) * R, R), :]

        def copy(k, block, to, src=None):
            return pltpu.make_async_remote_copy(
                src_ref=rows(*block) if src is None else src, dst_ref=rows(*block), send_sem=send_sems.at[k],
                recv_sem=recv_sems.at[k], device_id=to, device_id_type=MESH_ID)

        mine = pltpu.make_async_copy(x_ref, rows(*me), local_sem)
        mine.start()
        first = [copy(0, me, sibling, src=x_ref)]
        first += [copy(1 + j, me, (*chip, mc), src=x_ref) for j, chip in enumerate(chips)]
        for cp in first:
            cp.start()
        passed = [copy(4 + j, (*chip, mc), sibling) for j, chip in enumerate(chips)]
        for j, chip in enumerate(chips):
            copy(1 + j, (*chip, mc), me).wait_recv()
            passed[j].start()
        copy(0, sibling, me).wait_recv()
        for j, chip in enumerate(chips):
            copy(4 + j, (*chip, 1 - mc), me).wait_recv()
        for cp in first + passed:
            cp.wait_send()
        mine.wait()
        if reduce_sum:
            acc = gat_ref[pl.ds(0, R), :]
            for d in range(1, N_DEV):
                acc = acc + gat_ref[pl.ds(d * R, R), :]
            out_ref[...] = acc

    sems = [pltpu.SemaphoreType.DMA((7,)), pltpu.SemaphoreType.DMA((7,)), pltpu.SemaphoreType.DMA]
    if reduce_sum:
        assert in_vmem
        out_shape = jax.ShapeDtypeStruct((R, Cc), x.dtype)
        scratch = [pltpu.VMEM((N_DEV * R, Cc), x.dtype)] + sems
    else:
        out_shape = jax.ShapeDtypeStruct((N_DEV * R, Cc), x.dtype)
        scratch = sems
    out_specs = pl.BlockSpec(memory_space=space)
    if with_token:
        out_shape = (out_shape, jax.ShapeDtypeStruct((8, LANE), F32))
        out_specs = (out_specs, pl.BlockSpec(memory_space=pltpu.VMEM))
    return pl.pallas_call(
        body, name=name, out_shape=out_shape, in_specs=[pl.BlockSpec(memory_space=space)], out_specs=out_specs,
        scratch_shapes=scratch, compiler_params=pltpu.CompilerParams(vmem_limit_bytes=VMEM_LIMIT))(x)


def _peer(k, mx, my, mc):
    bits = k + 1
    return (1 - mx if bits & 4 else mx, 1 - my if bits & 2 else my, 1 - mc if bits & 1 else mc)


HBM_SPEC = pl.BlockSpec(memory_space=pltpu.HBM)
SEM_SPEC = pl.BlockSpec(memory_space=pltpu.SEMAPHORE)
DATAFLOW = pltpu.SideEffectType.DATAFLOW_SIDE_EFFECTING


def _exchange_copies(x_refs, land_refs, send_sems, recv_sems, scatter):
    mx, my, mc = _place()
    me = 4 * mx + 2 * my + mc
    n = len(x_refs)
    copies = []
    for k in range(N_DEV - 1):
        px, py, pc = _peer(k, mx, my, mc)
        for m, (x_ref, land_ref) in enumerate(zip(x_refs, land_refs)):
            rows = land_ref.shape[1] if scatter else x_ref.shape[0]
            if scatter:
                src = x_ref.at[pl.ds(pl.multiple_of((4 * px + 2 * py + pc) * rows, 16), rows), :]
                dst = land_ref.at[k]
            else:
                src = x_ref
                dst = land_ref.at[pl.ds(pl.multiple_of(me * rows, 16), rows), :]
            copies.append(pltpu.make_async_remote_copy(
                src_ref=src, dst_ref=dst, send_sem=send_sems.at[k * n + m], recv_sem=recv_sems.at[k * n + m],
                device_id=(px, py, pc), device_id_type=MESH_ID))
    return copies


def _land_shape(x, scatter):
    return (N_DEV - 1, x.shape[0] // N_DEV, x.shape[1]) if scatter else (N_DEV * x.shape[0], x.shape[1])


def _own_copies(x_refs, land_refs, local_sems):
    mx, my, mc = _place()
    me = 4 * mx + 2 * my + mc
    return [pltpu.make_async_copy(
        x_ref, land_ref.at[pl.ds(pl.multiple_of(me * x_ref.shape[0], 16), x_ref.shape[0]), :], local_sems.at[m])
        for m, (x_ref, land_ref) in enumerate(zip(x_refs, land_refs))]


def _exchange_start(groups, *, name, scatter):
    sizes = [len(g) for g in groups]
    xs = [x for g in groups for x in g]
    n = len(xs)
    lands = [lax.empty(_land_shape(x, scatter), x.dtype) for x in xs]
    per = 2 if scatter else 3

    def body(*refs):
        sems = refs[2 * n:2 * n + per * len(groups)]
        token = refs[-1]
        off = 0
        for gi, m in enumerate(sizes):
            x_refs, land_refs = refs[off:off + m], refs[n + off:n + off + m]
            for cp in _exchange_copies(x_refs, land_refs, sems[per * gi], sems[per * gi + 1], scatter):
                cp.start()
            if not scatter:
                for cp in _own_copies(x_refs, land_refs, sems[per * gi + 2]):
                    cp.start()
            off += m
        token[...] = jnp.zeros_like(token)

    sem_shapes = []
    for m in sizes:
        sem_shapes += [pltpu.SemaphoreType.DMA(((N_DEV - 1) * m,))] * 2
        if not scatter:
            sem_shapes.append(pltpu.SemaphoreType.DMA((m,)))
    ns = len(sem_shapes)
    out = pl.pallas_call(
        body, name=name,
        out_shape=(*sem_shapes, *[pltpu.HBM(x.shape, x.dtype) for x in xs],
                   *[pltpu.HBM(l.shape, l.dtype) for l in lands], jax.ShapeDtypeStruct((8, LANE), F32)),
        in_specs=(HBM_SPEC,) * (2 * n),
        out_specs=(SEM_SPEC,) * ns + (HBM_SPEC,) * (2 * n) + (pl.BlockSpec(memory_space=pltpu.VMEM),),
        input_output_aliases={i: ns + i for i in range(2 * n)},
        compiler_params=pltpu.CompilerParams(has_side_effects=DATAFLOW))(
            *[pltpu.with_memory_space_constraint(t, pltpu.HBM) for t in xs + lands])
    started, off = [], 0
    for gi, m in enumerate(sizes):
        sems = out[per * gi:per * gi + per]
        started.append((sems[0], sems[1], list(out[ns + off:ns + off + m]),
                        list(out[ns + n + off:ns + n + off + m]), out[-1], None if scatter else sems[2]))
        off += m
    return started


def _exchange_wait(started, after, *, name, scatter):
    send_sems, recv_sems, xs, lands, _, local_sems = started
    n = len(xs)

    def body(*refs):
        x_refs, land_refs = refs[:n], refs[n:2 * n]
        for cp in _exchange_copies(x_refs, land_refs, refs[2 * n], refs[2 * n + 1], scatter):
            cp.wait_send()
            cp.wait_recv()
        if not scatter:
            for cp in _own_copies(x_refs, land_refs, refs[2 * n + 2]):
                cp.wait()

    sems = (send_sems, recv_sems) if scatter else (send_sems, recv_sems, local_sems)
    out = pl.pallas_call(
        body, name=name, out_shape=tuple(pltpu.HBM(t.shape, t.dtype) for t in xs + lands),
        in_specs=(HBM_SPEC,) * (2 * n) + (SEM_SPEC,) * len(sems) + (pl.BlockSpec(memory_space=pl.ANY),),
        out_specs=(HBM_SPEC,) * (2 * n), input_output_aliases={i: i for i in range(2 * n)},
        compiler_params=pltpu.CompilerParams(has_side_effects=DATAFLOW))(*xs, *lands, *sems, after)
    return list(out[:n]), list(out[n:])


def _gather_start(groups, token, *, name):
    first = groups[0]
    groups = [[first[0] + token[0, 0].astype(first[0].dtype)] + list(first[1:])] + [list(g) for g in groups[1:]]
    return _exchange_start(groups, name=name, scatter=False)


def _gather_finish(started, after, *, name):
    return _exchange_wait(started, after, name=name, scatter=False)[1]


def _reduce_start(grads, *, name):
    return _exchange_start([grads], name=name, scatter=True)[0]


def _reduce_finish(started, after, me, *, name):
    sent, gots = _exchange_wait(started, after, name=name + "_wait", scatter=True)
    return [_add_received(g, got, me, name=f"{name}_add{m}") for m, (g, got) in enumerate(zip(sent, gots))]


def _pad_rows(a, mult):
    r = (-a.shape[0]) % mult
    return a if r == 0 else jnp.concatenate([a, jnp.zeros((r,) + a.shape[1:], a.dtype)], axis=0)


def kernel(x, mem, mix_norm, mem_norm, w_mem_kv, w_out, hg_w_in, hg_lb, hg_onorm, gm_w_in, gm_ln_g, gm_ln_b, gm_ws, gm_bs, ffn_norm, w_ffn_in, w_ffn_out, final_norm, loss_target, m_mix_norm, m_mem_norm, m_w_mem_kv, m_w_out, m_hg_w_in, m_hg_lb, m_hg_onorm, m_gm_w_in, m_gm_ln_g, m_gm_ln_b, m_gm_ws, m_gm_bs, m_ffn_norm, m_w_ffn_in, m_w_ffn_out, m_final_norm, v_mix_norm, v_mem_norm, v_w_mem_kv, v_w_out, v_hg_w_in, v_hg_lb, v_hg_onorm, v_gm_w_in, v_gm_ln_g, v_gm_ln_b, v_gm_ws, v_gm_bs, v_ffn_norm, v_w_ffn_in, v_w_ffn_out, v_final_norm):
    mx, my, mc = _place()
    me = 4 * mx + 2 * my + mc
    xs = x[0]
    mems = mem[0]
    tgt = loss_target[0]

    hg_t = hg_w_in[0].T.astype(BF16)
    gm_t = gm_w_in[0].T.astype(BF16)
    fi_t = [w_ffn_in[i].T.astype(BF16) for i in range(2)]
    kv_b = [w_mem_kv[i].astype(BF16) for i in range(2)]
    out_b = [w_out[i].astype(BF16) for i in range(2)]
    fo_b = [w_ffn_out[i].astype(BF16) for i in range(2)]
    W_hgT, token = _all_gather(hg_t, name="gather_first", in_vmem=False, with_token=True)
    ln_local = _pad_rows(jnp.concatenate([gm_ln_g, gm_ln_b], axis=0), 16)
    ln_local = jnp.concatenate([ln_local, jnp.zeros((16, LANE - ln_local.shape[1]), F32)], axis=1)
    gather_mix, fi0, fo0, gather_gm, fi1, fo1 = _gather_start(
        [kv_b + out_b, [fi_t[0]], [fo_b[0]], [gm_t, ln_local], [fi_t[1]], [fo_b[1]]], token,
        name="gather_rest_start")
    gather_fi, gather_fo = [fi0, fi1], [fo0, fo1]

    lb_soft = jax.nn.softmax(hg_lb, axis=0)
    lb0 = lb_soft[0:1]
    bsb = jnp.broadcast_to(gm_bs[0][:, :, None], (N_HEADS, GM_CHUNK, GM_CHUNK))
    ws = gm_ws[0]

    W_fiT, W_fo = [], []

    def ffn_fwd(xin, hf, i, **tail):
        W_fiT.extend(_gather_finish(gather_fi[i], hf, name=f"gather_fi{i}_wait"))
        gu, act = _ffn_in(hf, W_fiT[i], name=f"ffn_in{i}")
        W_fo.extend(_gather_finish(gather_fo[i], act, name=f"gather_fo{i}_wait"))
        return gu, act, _matmul(act, W_fo[i], res=xin, name=f"ffn_out{i}", **tail)

    h0 = _rms_fwd(xs, mix_norm[0:1], name="mix_norm0", dep=gather_mix[4])
    p0 = _matmul(h0, W_hgT, tb=True, name="hg_in")
    heads0, o0, states = _hgrn2_fwd(p0, lb0, hg_onorm, name="hgrn2_fwd")

    kv0, kv1, wo0, wo1 = _gather_finish(gather_mix, o0, name="gather_mix_wait")
    W_kv, W_out = [kv0, kv1], [wo0, wo1]
    mem_n, kv = [], []
    for i in range(2):
        mn = _rms_fwd(mems, mem_norm[i:i + 1], name=f"mem_norm{i}")
        mem_n.append(mn)
        kv.append(_matmul(mn, W_kv[i], name=f"mem_kv{i}"))

    heads0 = _attn_fwd(p0, 4 * D_TOK // D_MEM, kv[0], heads0, name="attn_fwd0")
    x1, hf0 = _matmul(heads0, W_out[0], res=xs, norm_gain=ffn_norm[0:1], name="out_proj0")
    gu0, act0, (x2, h1) = ffn_fwd(x1, hf0, 0, norm_gain=mix_norm[1:2])

    W_gmT, ln_all = _gather_finish(gather_gm, h1, name="gather_gm_wait")
    ln_all = ln_all.reshape(N_DEV, 16, LANE)
    ln_g = ln_all[:, 0, :D_TOK // N_DEV].reshape(1, D_TOK)
    ln_b = ln_all[:, 1, :D_TOK // N_DEV].reshape(1, D_TOK)
    p1 = _matmul(h1, W_gmT, tb=True, name="gm_in")
    heads1 = _gmlp_fwd(p1, ln_g, ln_b, ws, bsb, name="gmlp_fwd")
    heads1 = _attn_fwd(p1, 2 * D_TOK // D_MEM, kv[1], heads1, name="attn_fwd1")
    x3, hf1 = _matmul(heads1, W_out[1], res=x2, norm_gain=ffn_norm[1:2], name="out_proj1")
    gu1, act1, (dx, g_final, loss_part) = ffn_fwd(x3, hf1, 1, loss_head=(final_norm.reshape(1, D_MODEL), tgt))

    def ffn_bwd(dx, xin, hf, gu, act, i, dep):
        dgu = _ffn_out_dx(dx, W_fo[i], gu, dep, name=f"ffn_out_dx{i}")
        g_wfo = _matmul(act, dx, ta=True, out_dtype=BF16, name=f"ffn_out_dw{i}")
        g_wfi_t = _matmul(dgu, hf, ta=True, a_halves=True, out_dtype=BF16, name=f"ffn_in_dw{i}")
        dx, g_norm = _matmul(dgu, W_fiT[i], a_halves=True, res=dx, norm_bwd=(xin, ffn_norm[i:i + 1]),
                             name=f"ffn_in_dx{i}")
        return dx, g_wfi_t, g_wfo, g_norm

    def mem_bwd(dkv, i):
        g_wkv = _matmul(mem_n[i], dkv, ta=True, out_dtype=BF16, name=f"mem_kv_dw{i}")
        dmn = _matmul(dkv, W_kv[i], tb=True, name=f"mem_kv_dx{i}")
        _, g_norm = _rms_bwd(mems, mem_norm[i:i + 1], dmn, jnp.zeros_like(mems), name=f"mem_norm_bwd{i}")
        return g_wkv, g_norm

    dx, g_wfi1_t, g_wfo1, g_ffn1 = ffn_bwd(dx, x3, hf1, gu1, act1, 1, loss_part)
    dheads = _matmul(dx, W_out[1], tb=True, name="out_proj_dx1")
    g_wout1 = _matmul(heads1, dx, ta=True, out_dtype=BF16, name="out_proj_dw1")
    dp, g_ws, g_bs, g_lng, g_lnb = _gmlp_bwd(p1, ln_g, ln_b, ws, bsb, dheads, name="gmlp_bwd")
    dp, dk, dv = _attn_bwd(p1, 2 * D_TOK // D_MEM, kv[1], dheads, dp, name="attn_bwd1")
    g_wkv1, g_mem1 = mem_bwd(jnp.concatenate([dk, dv], axis=1), 1)
    g_wgm_t = _matmul(dp, h1, ta=True, out_dtype=BF16, name="gm_in_dw")
    dx, g_mix1 = _matmul(dp, W_gmT, res=dx, norm_bwd=(x2, mix_norm[1:2]), name="gm_in_dx")
    reduce_l1 = _reduce_start([g_wkv1, g_wout1, g_wgm_t, g_wfi1_t, g_wfo1], name="reduce_l1_start")
    early = [loss_part, g_final, g_ffn1, g_mix1, g_mem1, g_ws.reshape(1, -1), g_bs.reshape(1, -1), g_lng, g_lnb]
    early_rows = _pad_rows(jnp.concatenate(early, axis=1).reshape(-1, LANE), 16) + reduce_l1[4][0, 0]
    small_early = _exchange_start([[early_rows]], name="reduce_small_start", scatter=False)[0]

    dx, g_wfi0_t, g_wfo0, g_ffn0 = ffn_bwd(dx, x1, hf0, gu0, act0, 0, small_early[4])
    reduce_ffn0 = _reduce_start([g_wfi0_t, g_wfo0], name="reduce_ffn0_start")
    dheads = _matmul(dx, W_out[0], tb=True, name="out_proj_dx0", dep=reduce_ffn0[4])
    g_wout0 = _matmul(heads0, dx, ta=True, out_dtype=BF16, name="out_proj_dw0")
    dp, g_lb0, g_onorm = _hgrn2_bwd(p0, lb0, hg_onorm, o0, states, dheads, name="hgrn2_bwd")
    dp, dk, dv = _attn_bwd(p0, 4 * D_TOK // D_MEM, kv[0], dheads, dp, name="attn_bwd0")
    g_wkv0, g_mem0 = mem_bwd(jnp.concatenate([dk, dv], axis=1), 0)
    g_whg_t = _matmul(dp, h0, ta=True, out_dtype=BF16, name="hg_in_dw")
    reduce_mix0 = _reduce_start([g_wkv0, g_wout0, g_whg_t], name="reduce_mix0_start")
    grad_x, g_mix0 = _matmul(dp, W_hgT, res=dx, norm_bwd=(xs, mix_norm[0:1]), name="hg_in_dx", dep=reduce_mix0[4])

    g_kv1, g_out1, g_gm_t, g_fi1_t, g_fo1 = _reduce_finish(reduce_l1, grad_x, me, name="reduce_l1")
    g_fi0_t, g_fo0 = _reduce_finish(reduce_ffn0, g_kv1, me, name="reduce_ffn0")
    g_kv0, g_out0, g_hg_t = _reduce_finish(reduce_mix0, g_fi0_t, me, name="reduce_mix0")
    g_shards = [jnp.stack([g_kv0, g_kv1]), jnp.stack([g_out0, g_out1]), g_hg_t[None], g_gm_t[None],
                jnp.stack([g_fi0_t, g_fi1_t]), jnp.stack([g_fo0, g_fo1])]
    transposed = (4, 7, 13)

    late = [g_ffn0, g_lb0, g_onorm, g_mem0, g_mix0]
    late_rows = _pad_rows(jnp.concatenate(late, axis=1).reshape(-1, LANE), 8)
    red_late = _all_gather(late_rows, name="reduce_small_late", in_vmem=True, reduce_sum=True)
    gathered = _exchange_wait(small_early, red_late, name="reduce_small_wait", scatter=False)[1][0]
    red_early = _sum_blocks(gathered.reshape(N_DEV, -1, LANE), name="reduce_small_sum")

    def split(flat, parts):
        out, off = [], 0
        for t in parts:
            out.append(flat[off:off + t.shape[1]])
            off += t.shape[1]
        return out

    r_loss, r_final, r_ffn1, r_mix1, r_mem1, r_ws, r_bs, r_lng, r_lnb = split(red_early.reshape(-1), early)
    r_ffn0, r_lb0, r_onorm, r_mem0, r_mix0 = split(red_late.reshape(-1), late)
    loss = r_loss[0]
    g_mix_norm = jnp.stack([r_mix0, r_mix1])
    g_mem_norm = jnp.stack([r_mem0, r_mem1])
    g_hg_lb = r_lb0[None, :] * lb0 * (jnp.eye(3, dtype=F32)[:, 0:1] - lb_soft)
    g_hg_onorm = r_onorm.reshape(1, D_TOK)
    width = D_TOK // N_DEV
    g_gm_ln_g = lax.dynamic_slice(r_lng, (me * width,), (width,)).reshape(1, width)
    g_gm_ln_b = lax.dynamic_slice(r_lnb, (me * width,), (width,)).reshape(1, width)
    g_gm_ws = r_ws.reshape(gm_ws.shape)
    g_gm_bs = r_bs.reshape(gm_bs.shape)
    g_ffn_norm = jnp.stack([r_ffn0, r_ffn1])
    g_final_norm = r_final

    grads = [g_mix_norm, g_mem_norm, g_shards[0], g_shards[1], g_shards[2], g_hg_lb, g_hg_onorm, g_shards[3],
             g_gm_ln_g, g_gm_ln_b, g_gm_ws, g_gm_bs, g_ffn_norm, g_shards[4], g_shards[5], g_final_norm]
    weights = [mix_norm, mem_norm, w_mem_kv, w_out, hg_w_in, hg_lb, hg_onorm, gm_w_in, gm_ln_g, gm_ln_b, gm_ws, gm_bs,
               ffn_norm, w_ffn_in, w_ffn_out, final_norm]
    ms = [m_mix_norm, m_mem_norm, m_w_mem_kv, m_w_out, m_hg_w_in, m_hg_lb, m_hg_onorm, m_gm_w_in, m_gm_ln_g,
          m_gm_ln_b, m_gm_ws, m_gm_bs, m_ffn_norm, m_w_ffn_in, m_w_ffn_out, m_final_norm]
    vs = [v_mix_norm, v_mem_norm, v_w_mem_kv, v_w_out, v_hg_w_in, v_hg_lb, v_hg_onorm, v_gm_w_in, v_gm_ln_g,
          v_gm_ln_b, v_gm_ws, v_gm_bs, v_ffn_norm, v_w_ffn_in, v_w_ffn_out, v_final_norm]
    deltas, new_m, new_v = [], [], []
    for n, (w, g, m, v) in enumerate(zip(weights, grads, ms, vs)):
        if w.ndim == 1:
            d, nm, nv = _adamw(w[None], g.reshape(1, -1), m[None], v[None], name=f"adamw{n}")
            d, nm, nv = d[0], nm[0], nv[0]
        elif n in transposed:
            flip = lambda t: jnp.swapaxes(t, 1, 2)
            d, nm, nv = (flip(t) for t in _adamw(flip(w), g, flip(m), flip(v), name=f"adamw{n}"))
            grads[n] = flip(g)
        else:
            d, nm, nv = _adamw(w, g.reshape(w.shape), m, v, name=f"adamw{n}")
        deltas.append(d)
        new_m.append(nm)
        new_v.append(nv)
    grads = [g.reshape(w.shape) for g, w in zip(grads, weights)]
    return (loss, grad_x[None], *grads, *deltas, *new_m, *new_v)
```

```python
import jax
import jax.numpy as jnp
from jax import lax
from jax.experimental import pallas as pl
from jax.experimental.pallas import tpu as pltpu

F32 = jnp.float32
BF16 = jnp.bfloat16
MXU_DTYPE = jnp.bfloat16
MESH_ID = pl.DeviceIdType.MESH

N_DEV = 8
EPS = 1e-6
D_MODEL = 1024
D_TOK = 768
D_MEM = 256
N_HEADS = 6
HEAD = 128
MEM_HEADS = 4
MEM_HDIM = 64
GM_CHUNK = 128
GM_STEP = 4
GM_STEP_BWD = 2
ATTN_ROWS = 2048
D_FF = 2816
HG_SUB = 16
HG_GROUP = 6
HG_IN = 4 * D_TOK + D_MEM
GM_IN = 2 * D_TOK + D_MEM
LANE = 128
STREAM_BUFFERS = 3
MXU_COLS = 256

ADAM_LR = 0.001
ADAM_B1 = 0.9
ADAM_B2 = 0.999
ADAM_EPS = 1e-08
ADAM_WD = 0.01
ADAM_STEP = 10

VMEM_LIMIT = 48 * 2 ** 20
VMEM_LIMIT_WIDE = 58 * 2 ** 20


def _params(sem=None, limit=VMEM_LIMIT):
    return pltpu.CompilerParams(dimension_semantics=sem, vmem_limit_bytes=limit)


def _tile(n, cap, q=LANE):
    if n <= cap:
        return n
    best = None
    for t in range(q, cap + 1, q):
        if n % t == 0:
            best = t
    assert best is not None, (n, cap, q)
    return best


def _sigmoid(x):
    return 1.0 / (1.0 + jnp.exp(-x))


def _gelu(x, with_grad=False):
    cdf = 0.5 * (1.0 + lax.erf(x * 0.7071067811865476))
    if not with_grad:
        return x * cdf
    return x * cdf, cdf + x * jnp.exp(-0.5 * x * x) * 0.3989422804014327


def _matmul(a, b, *, name, ta=False, tb=False, res=None, out_dtype=F32, a_halves=False, b_halves=False, dep=None,
            norm_gain=None, norm_bwd=None, loss_head=None):
    if a_halves and ta:
        K, M = a.shape[1], 2 * a.shape[2]
    elif a_halves:
        M, K = a.shape[1], 2 * a.shape[2]
    else:
        K, M = a.shape if ta else a.shape[::-1]
    if b_halves:
        assert not tb and b.shape[1] == K
        N = 2 * b.shape[2]
    else:
        N = b.shape[0] if tb else b.shape[1]
        assert (b.shape[1] if tb else b.shape[0]) == K
    tm = _tile(M // 2 if (a_halves and ta) else M, 1664 if ta else 1024)
    tn = _tile(N // 2 if b_halves else N, 1792)
    tk = _tile(K // 2 if (a_halves and not ta) else K, 1024 if ta else 1664)
    nk = K // tk
    dims = (((0 if ta else 1,), (1 if tb else 0,)), ((), ()))

    strips = norm_bwd is not None or loss_head is not None
    fused = norm_gain is not None or strips
    n_in = 2 + (res is not None) + (norm_gain is not None) + 2 * strips + (dep is not None)
    if fused:
        assert tn == N, "the fused norm needs whole rows"
        assert (norm_gain is not None) + (norm_bwd is not None) + (loss_head is not None) == 1
        assert not strips or (res is not None and nk > 1 and tm % LANE == 0)

    def body(*refs):
        a_ref, b_ref = refs[:2]
        r_ref = refs[2] if res is not None else None
        g_ref = refs[2 + (res is not None)] if fused else None
        x_ref = refs[3 + (res is not None)] if strips else None
        o_ref = refs[n_in]
        h_ref = refs[n_in + 1] if fused else None
        l_ref = refs[n_in + 2] if loss_head is not None else None
        acc = None if nk == 1 else refs[-1]
        k = pl.program_id(2)

        def product():
            return lax.dot_general(a_ref[...].astype(MXU_DTYPE), b_ref[...].astype(MXU_DTYPE), dims,
                                   preferred_element_type=F32)

        def finish(r):
            if loss_head is not None:
                @pl.when(pl.program_id(0) == 0)
                def _():
                    h_ref[...] = jnp.zeros_like(h_ref)
                    l_ref[...] = jnp.zeros_like(l_ref)

                acc[...] = r + r_ref[...]
                gv = g_ref[...]

                def strip(s, carry):
                    dg, loss = carry
                    rows = pl.ds(pl.multiple_of(s * LANE, LANE), LANE)
                    xv = acc[rows, :]
                    scale = lax.rsqrt(jnp.mean(xv * xv, axis=-1, keepdims=True) + EPS)
                    xh = xv * scale
                    err = xh * gv - x_ref[rows, :]
                    loss = loss + 0.5 * jnp.sum(jnp.mean(err * err, axis=-1, keepdims=True), axis=0, keepdims=True)
                    dy = err * (1.0 / N)
                    u = dy * gv
                    o_ref[rows, :] = scale * (u - xh * jnp.mean(u * xh, axis=-1, keepdims=True))
                    return dg + jnp.sum(dy * xh, axis=0, keepdims=True), loss

                dg, loss = lax.fori_loop(0, tm // LANE, strip, (jnp.zeros((1, N), F32), jnp.zeros((1, 1), F32)))
                h_ref[...] += dg
                l_ref[...] += jnp.broadcast_to(loss, l_ref.shape)
                return
            if norm_bwd is not None:
                @pl.when(pl.program_id(0) == 0)
                def _():
                    h_ref[...] = jnp.zeros_like(h_ref)

                acc[...] = r
                gv = g_ref[...]

                def strip(s, dg):
                    rows = pl.ds(pl.multiple_of(s * LANE, LANE), LANE)
                    rv = acc[rows, :]
                    xv = x_ref[rows, :]
                    scale = lax.rsqrt(jnp.mean(xv * xv, axis=-1, keepdims=True) + EPS)
                    xh = xv * scale
                    u = rv * gv
                    o_ref[rows, :] = r_ref[rows, :] + scale * (u - xh * jnp.mean(u * xh, axis=-1, keepdims=True))
                    return dg + jnp.sum(rv * xh, axis=0, keepdims=True)

                h_ref[...] += lax.fori_loop(0, tm // LANE, strip, jnp.zeros((1, N), F32))
                return
            if res is not None:
                r = r + r_ref[...].astype(F32)
            o_ref[...] = r.astype(out_dtype)
            if norm_gain is not None:
                scale = lax.rsqrt(jnp.mean(r * r, axis=-1, keepdims=True) + EPS)
                h_ref[...] = (r * scale * g_ref[...]).astype(h_ref.dtype)

        if nk == 1:
            finish(product())
            return

        @pl.when(k == 0)
        def _():
            acc[...] = product()

        @pl.when((k > 0) & (k < nk - 1))
        def _():
            acc[...] += product()

        @pl.when(k == nk - 1)
        def _():
            finish(acc[...] + product())

    if a_halves and ta:
        mh = M // 2 // tm
        a_spec = pl.BlockSpec((None, tk, tm), lambda i, j, k: (i // mh, k, i % mh))
    elif a_halves:
        kh = nk // 2
        a_spec = pl.BlockSpec((None, tm, tk), lambda i, j, k: (k // kh, i, k % kh))
    elif ta:
        a_spec = pl.BlockSpec((tk, tm), lambda i, j, k: (k, i))
    else:
        a_spec = pl.BlockSpec((tm, tk), lambda i, j, k: (i, k))
    if b_halves:
        nh = N // 2 // tn
        b_spec = pl.BlockSpec((None, tk, tn), lambda i, j, k: (j // nh, k, j % nh))
    elif tb:
        b_spec = pl.BlockSpec((tn, tk), lambda i, j, k: (j, k))
    else:
        b_spec = pl.BlockSpec((tk, tn), lambda i, j, k: (k, j))
    o_spec = pl.BlockSpec((tm, tn), lambda i, j, k: (i, j))
    in_specs = [a_spec, b_spec] + ([o_spec] if res is not None else [])
    args = (a, b) + ((res,) if res is not None else ())
    out_specs, out_shape = o_spec, jax.ShapeDtypeStruct((M, N), out_dtype)
    vec = pl.BlockSpec((1, N), lambda i, j, k: (0, 0))
    sem = ("parallel", "parallel", "arbitrary")
    if norm_gain is not None:
        in_specs.append(vec)
        args += (norm_gain,)
        out_specs, out_shape = [o_spec, o_spec], [out_shape, jax.ShapeDtypeStruct((M, N), BF16)]
    if norm_bwd is not None:
        x_in, gain = norm_bwd
        in_specs += [vec, o_spec]
        args += (gain, x_in)
        out_specs, out_shape = [o_spec, vec], [out_shape, jax.ShapeDtypeStruct((1, N), F32)]
        sem = ("arbitrary", "arbitrary", "arbitrary")
    if loss_head is not None:
        gain, target = loss_head
        in_specs += [vec, o_spec]
        args += (gain, target)
        one = pl.BlockSpec((1, LANE), lambda i, j, k: (0, 0))
        out_specs = [o_spec, vec, one]
        out_shape = [out_shape, jax.ShapeDtypeStruct((1, N), F32), jax.ShapeDtypeStruct((1, LANE), F32)]
        sem = ("arbitrary", "arbitrary", "arbitrary")
    if dep is not None:
        in_specs.append(pl.BlockSpec(memory_space=pl.ANY))
        args += (dep,)
    return pl.pallas_call(
        body, name=name, grid=(M // tm, N // tn, nk), in_specs=in_specs, out_specs=out_specs, out_shape=out_shape,
        scratch_shapes=[] if nk == 1 else [pltpu.VMEM((tm, tn), F32)],
        compiler_params=_params(sem, VMEM_LIMIT_WIDE if strips else VMEM_LIMIT))(*args)


def _ffn_in(hf, wt, *, name):
    S, K = hf.shape
    tm = _tile(S, 512)
    tn = _tile(D_FF, 1408)
    nh = D_FF // tn
    nt = (((1,), (1,)), ((), ()))

    def body(a_ref, bg_ref, bu_ref, gu_ref, act_ref):
        av = a_ref[...].astype(MXU_DTYPE)
        for c0 in range(0, tn, MXU_COLS):
            cs = slice(c0, min(c0 + MXU_COLS, tn))
            gate = lax.dot_general(av, bg_ref[cs, :].astype(MXU_DTYPE), nt, preferred_element_type=F32)
            up = lax.dot_general(av, bu_ref[cs, :].astype(MXU_DTYPE), nt, preferred_element_type=F32)
            gu_ref[0, :, cs] = gate.astype(gu_ref.dtype)
            gu_ref[1, :, cs] = up.astype(gu_ref.dtype)
            act_ref[:, cs] = (gate * _sigmoid(gate) * up).astype(act_ref.dtype)

    return pl.pallas_call(
        body, name=name, grid=(nh, S // tm),
        in_specs=[pl.BlockSpec((tm, K), lambda j, i: (i, 0)), pl.BlockSpec((tn, K), lambda j, i: (j, 0)),
                  pl.BlockSpec((tn, K), lambda j, i: (j + nh, 0))],
        out_specs=[pl.BlockSpec((2, tm, tn), lambda j, i: (0, i, j)), pl.BlockSpec((tm, tn), lambda j, i: (i, j))],
        out_shape=[jax.ShapeDtypeStruct((2, S, D_FF), BF16), jax.ShapeDtypeStruct((S, D_FF), BF16)],
        compiler_params=_params(("parallel", "parallel")))(hf, wt, wt)


def _ffn_out_dx(dx, w, gu, dep, *, name):
    S, K = dx.shape
    tm = _tile(S, 1024)
    tn = _tile(D_FF, 1408)

    def body(a_ref, b_ref, gu_ref, dep_ref, o_ref):
        del dep_ref
        av = a_ref[...].astype(MXU_DTYPE)
        for c0 in range(0, tn, MXU_COLS):
            cs = slice(c0, min(c0 + MXU_COLS, tn))
            da = lax.dot_general(av, b_ref[cs, :].astype(MXU_DTYPE), (((1,), (1,)), ((), ())),
                                 preferred_element_type=F32)
            gate = gu_ref[0, :, cs].astype(F32)
            up = gu_ref[1, :, cs].astype(F32)
            sg = _sigmoid(gate)
            o_ref[0, :, cs] = (da * up * sg * (1.0 + gate * (1.0 - sg))).astype(o_ref.dtype)
            o_ref[1, :, cs] = (da * gate * sg).astype(o_ref.dtype)

    halves = pl.BlockSpec((2, tm, tn), lambda i, j: (0, i, j))
    return pl.pallas_call(
        body, name=name, grid=(S // tm, D_FF // tn),
        in_specs=[pl.BlockSpec((tm, K), lambda i, j: (i, 0)), pl.BlockSpec((tn, K), lambda i, j: (j, 0)), halves,
                  pl.BlockSpec(memory_space=pl.ANY)],
        out_specs=halves, out_shape=jax.ShapeDtypeStruct((2, S, D_FF), BF16),
        compiler_params=_params(("parallel", "parallel")))(dx, w, gu, dep)


def _rms_fwd_stream(x, g, tr, *, name, dep):
    R, Dm = x.shape
    n = R // tr
    nb = STREAM_BUFFERS

    def body(x_hbm, g_ref, *rest):
        o_hbm, xbuf, obuf, in_sems, out_sems = rest[-5:]

        def read(i):
            return pltpu.make_async_copy(x_hbm.at[pl.ds(i * tr, tr), :], xbuf.at[i % nb], in_sems.at[i % nb])

        def write(i):
            return pltpu.make_async_copy(obuf.at[i % 2], o_hbm.at[pl.ds(i * tr, tr), :], out_sems.at[i % 2])

        for i in range(nb):
            read(i).start()
        gv = g_ref[...]
        for i in range(n):
            read(i).wait()
            if i >= 2:
                write(i - 2).wait()
            xv = xbuf[i % nb]
            scale = lax.rsqrt(jnp.mean(xv * xv, axis=-1, keepdims=True) + EPS)
            obuf[i % 2] = (xv * scale * gv).astype(obuf.dtype)
            write(i).start()
            if i + nb < n:
                read(i + nb).start()
        write(n - 2).wait()
        write(n - 1).wait()

    hbm = pl.BlockSpec(memory_space=pl.ANY)
    in_specs = [hbm, pl.BlockSpec(memory_space=pltpu.VMEM)]
    args = (x, g)
    if dep is not None:
        in_specs.append(hbm)
        args += (dep,)
    return pl.pallas_call(
        body, name=name, in_specs=in_specs, out_specs=hbm, out_shape=jax.ShapeDtypeStruct((R, Dm), BF16),
        scratch_shapes=[pltpu.VMEM((nb, tr, Dm), F32), pltpu.VMEM((2, tr, Dm), BF16),
                        pltpu.SemaphoreType.DMA((nb,)), pltpu.SemaphoreType.DMA((2,))],
        compiler_params=_params())(*args)


def _rms_fwd(x, g, *, name, dep=None):
    R, Dm = x.shape
    tr = _tile(R, 512, 8)
    if R // tr >= 2 * STREAM_BUFFERS:
        return _rms_fwd_stream(x, g, tr, name=name, dep=dep)

    def body(x_ref, g_ref, *rest):
        o_ref = rest[-1]
        xv = x_ref[...]
        r = lax.rsqrt(jnp.mean(xv * xv, axis=-1, keepdims=True) + EPS)
        o_ref[...] = (xv * r * g_ref[...]).astype(o_ref.dtype)

    in_specs = [pl.BlockSpec((tr, Dm), lambda i: (i, 0)), pl.BlockSpec((1, Dm), lambda i: (0, 0))]
    args = (x, g)
    if dep is not None:
        in_specs.append(pl.BlockSpec(memory_space=pl.ANY))
        args += (dep,)
    return pl.pallas_call(
        body, name=name, grid=(R // tr,), in_specs=in_specs,
        out_specs=pl.BlockSpec((tr, Dm), lambda i: (i, 0)), out_shape=jax.ShapeDtypeStruct((R, Dm), BF16),
        compiler_params=_params(("parallel",)))(*args)


def _rms_bwd(x, g, dh, dres, *, name):
    R, Dm = x.shape
    tr = _tile(R, 256, 8)

    def body(x_ref, g_ref, dh_ref, dres_ref, dx_ref, dg_ref):
        @pl.when(pl.program_id(0) == 0)
        def _():
            dg_ref[...] = jnp.zeros_like(dg_ref)

        xv = x_ref[...]
        r = lax.rsqrt(jnp.mean(xv * xv, axis=-1, keepdims=True) + EPS)
        xh = xv * r
        dhv = dh_ref[...].astype(F32)
        dg_ref[...] += jnp.sum(dhv * xh, axis=0, keepdims=True)
        u = dhv * g_ref[...]
        dx = r * (u - xh * jnp.mean(u * xh, axis=-1, keepdims=True))
        dx_ref[...] = dres_ref[...] + dx

    row = pl.BlockSpec((tr, Dm), lambda i: (i, 0))
    vec = pl.BlockSpec((1, Dm), lambda i: (0, 0))
    return pl.pallas_call(
        body, name=name, grid=(R // tr,), in_specs=[row, vec, row, row], out_specs=[row, vec],
        out_shape=[jax.ShapeDtypeStruct((R, Dm), F32), jax.ShapeDtypeStruct((1, Dm), F32)],
        compiler_params=_params(("arbitrary",)))(x, g, dh, dres)


def _head_mask(h):
    lane = lax.broadcasted_iota(jnp.int32, (1, D_MEM), 1)
    return (lane >= h * MEM_HDIM) & (lane < (h + 1) * MEM_HDIM)


def _attn_probs(qv, k_mx, mask):
    s = lax.dot_general(jnp.where(mask, qv, 0.0).astype(MXU_DTYPE), k_mx, (((1,), (1,)), ((), ())),
                        preferred_element_type=F32) * (MEM_HDIM ** -0.5)
    e = jnp.exp(s - jnp.max(s, axis=-1, keepdims=True))
    return e / jnp.sum(e, axis=-1, keepdims=True)


def _attn_fwd(p, qcol, kv, heads, *, name):
    S = p.shape[0]
    M = kv.shape[0]
    ts = _tile(S, ATTN_ROWS, 8)

    def body(q_ref, k_ref, v_ref, heads_in, o_ref):
        del heads_in
        qv = q_ref[...]
        kx = k_ref[...].astype(MXU_DTYPE)
        vv = v_ref[...]
        out = jnp.zeros((ts, D_MEM), F32)
        for h in range(MEM_HEADS):
            mask = _head_mask(h)
            pr = _attn_probs(qv, kx, mask)
            out = out + jnp.dot(pr.astype(MXU_DTYPE), jnp.where(mask, vv, 0.0).astype(MXU_DTYPE),
                                preferred_element_type=F32)
        o_ref[...] = out.astype(o_ref.dtype)

    return pl.pallas_call(
        body, name=name, grid=(S // ts,),
        in_specs=[pl.BlockSpec((ts, D_MEM), lambda i: (i, qcol)), pl.BlockSpec((M, D_MEM), lambda i: (0, 0)),
                  pl.BlockSpec((M, D_MEM), lambda i: (0, 1)), pl.BlockSpec(memory_space=pl.ANY)],
        out_specs=pl.BlockSpec((ts, D_MEM), lambda i: (i, D_TOK // D_MEM)),
        out_shape=jax.ShapeDtypeStruct(heads.shape, heads.dtype), input_output_aliases={3: 0},
        compiler_params=_params(("parallel",)))(p, kv, kv, heads)


def _attn_bwd(p, qcol, kv, dheads, dp, *, name):
    S = p.shape[0]
    M = kv.shape[0]
    ts = _tile(S, ATTN_ROWS, 8)
    scale = MEM_HDIM ** -0.5

    def body(q_ref, k_ref, v_ref, do_ref, dp_in, dq_ref, dk_ref, dv_ref):
        del dp_in

        @pl.when(pl.program_id(0) == 0)
        def _():
            dk_ref[...] = jnp.zeros_like(dk_ref)
            dv_ref[...] = jnp.zeros_like(dv_ref)

        qv = q_ref[...]
        kv_ = k_ref[...]
        kx = kv_.astype(MXU_DTYPE)
        vv = v_ref[...]
        dox = do_ref[...].astype(MXU_DTYPE)
        qx = qv.astype(MXU_DTYPE)
        dq = jnp.zeros((ts, D_MEM), F32)
        for h in range(MEM_HEADS):
            mask = _head_mask(h)
            pr = _attn_probs(qv, kx, mask)
            vh = jnp.where(mask, vv, 0.0).astype(MXU_DTYPE)
            dpr = lax.dot_general(dox, vh, (((1,), (1,)), ((), ())), preferred_element_type=F32)
            ds = (pr * (dpr - jnp.sum(dpr * pr, axis=-1, keepdims=True)) * scale).astype(MXU_DTYPE)
            dq = dq + jnp.dot(ds, jnp.where(mask, kv_, 0.0).astype(MXU_DTYPE), preferred_element_type=F32)
            dk_h = lax.dot_general(ds, qx, (((0,), (0,)), ((), ())), preferred_element_type=F32)
            dv_h = lax.dot_general(pr.astype(MXU_DTYPE), dox, (((0,), (0,)), ((), ())), preferred_element_type=F32)
            dk_ref[...] += jnp.where(mask, dk_h, 0.0)
            dv_ref[...] += jnp.where(mask, dv_h, 0.0)
        dq_ref[...] = dq.astype(dq_ref.dtype)

    return pl.pallas_call(
        body, name=name, grid=(S // ts,),
        in_specs=[pl.BlockSpec((ts, D_MEM), lambda i: (i, qcol)), pl.BlockSpec((M, D_MEM), lambda i: (0, 0)),
                  pl.BlockSpec((M, D_MEM), lambda i: (0, 1)),
                  pl.BlockSpec((ts, D_MEM), lambda i: (i, D_TOK // D_MEM)), pl.BlockSpec(memory_space=pl.ANY)],
        out_specs=[pl.BlockSpec((ts, D_MEM), lambda i: (i, qcol)), pl.BlockSpec((M, D_MEM), lambda i: (0, 0)),
                   pl.BlockSpec((M, D_MEM), lambda i: (0, 0))],
        out_shape=[jax.ShapeDtypeStruct(dp.shape, dp.dtype), jax.ShapeDtypeStruct((M, D_MEM), F32),
                   jax.ShapeDtypeStruct((M, D_MEM), F32)],
        input_output_aliases={4: 0}, compiler_params=_params(("arbitrary",)))(p, kv, kv, dheads, dp)


def _gm_forward_parts(u_ref, v_ref, lng_ref, lnb_ref, w_ref, bsb_ref, with_grad=False):
    if with_grad:
        (zu, du_gelu), (zv, dv_gelu) = _gelu(u_ref[...], True), _gelu(v_ref[...], True)
    else:
        zu, zv, du_gelu, dv_gelu = _gelu(u_ref[...]), _gelu(v_ref[...]), None, None
    mu = jnp.mean(zv, axis=-1, keepdims=True)
    cen = zv - mu
    rs = lax.rsqrt(jnp.mean(cen * cen, axis=-1, keepdims=True) + EPS)
    vh = cen * rs
    vn = vh * lng_ref[...] + lnb_ref[...]
    row = lax.broadcasted_iota(jnp.int32, (GM_CHUNK, GM_CHUNK), 0)
    col = lax.broadcasted_iota(jnp.int32, (GM_CHUNK, GM_CHUNK), 1)
    tril = row >= col
    wm = [jnp.where(tril, w_ref[g], 0.0).astype(MXU_DTYPE) for g in range(N_HEADS)]
    vnx = [vn[:, g * HEAD:(g + 1) * HEAD].astype(MXU_DTYPE) for g in range(N_HEADS)]
    sv = [jnp.dot(wm[g], vnx[g], preferred_element_type=F32) + bsb_ref[g] for g in range(N_HEADS)]
    return zu, vh, rs, wm, vnx, sv, tril, du_gelu, dv_gelu


def _gmlp_fwd(p, lng, lnb, ws, bsb, *, name):
    S = p.shape[0]

    per_step = GM_STEP if S % (GM_STEP * GM_CHUNK) == 0 else 1

    def body(u_ref, v_ref, lng_ref, lnb_ref, w_ref, bsb_ref, o_ref):
        for c in range(per_step):
            rows = pl.ds(c * GM_CHUNK, GM_CHUNK)
            zu, _, _, _, _, sv, _, _, _ = _gm_forward_parts(u_ref.at[rows], v_ref.at[rows], lng_ref, lnb_ref, w_ref,
                                                            bsb_ref)
            for g in range(N_HEADS):
                o_ref[rows, g * HEAD:(g + 1) * HEAD] = (zu[:, g * HEAD:(g + 1) * HEAD] * sv[g]).astype(o_ref.dtype)

    blk = lambda c: pl.BlockSpec((per_step * GM_CHUNK, D_TOK), lambda i: (i, c))
    vec = pl.BlockSpec((1, D_TOK), lambda i: (0, 0))
    cube = pl.BlockSpec((N_HEADS, GM_CHUNK, GM_CHUNK), lambda i: (0, 0, 0))
    return pl.pallas_call(
        body, name=name, grid=(S // (per_step * GM_CHUNK),), in_specs=[blk(0), blk(1), vec, vec, cube, cube],
        out_specs=blk(0), out_shape=jax.ShapeDtypeStruct((S, D_MODEL), BF16),
        compiler_params=_params(("parallel",)))(p, p, lng, lnb, ws, bsb)


def _gmlp_bwd(p, lng, lnb, ws, bsb, dheads, *, name):
    S = p.shape[0]
    per_step = GM_STEP_BWD if S % (GM_STEP_BWD * GM_CHUNK) == 0 else 1

    def body(u_ref, v_ref, lng_ref, lnb_ref, w_ref, bsb_ref, dt_ref, dp_ref, dw_ref, dbs_ref, dlg_ref, dlb_ref):
        @pl.when(pl.program_id(0) == 0)
        def _():
            dw_ref[...] = jnp.zeros_like(dw_ref)
            dbs_ref[...] = jnp.zeros_like(dbs_ref)
            dlg_ref[...] = jnp.zeros_like(dlg_ref)
            dlb_ref[...] = jnp.zeros_like(dlb_ref)

        for c in range(per_step):
            rows = pl.ds(c * GM_CHUNK, GM_CHUNK)
            zu, vh, rs, wm, vnx, sv, tril, du_gelu, dv_gelu = _gm_forward_parts(
                u_ref.at[rows], v_ref.at[rows], lng_ref, lnb_ref, w_ref, bsb_ref, with_grad=True)
            dt = dt_ref[rows, :].astype(F32)
            dvn_parts = []
            for g in range(N_HEADS):
                sl = slice(g * HEAD, (g + 1) * HEAD)
                dsv = dt[:, sl] * zu[:, sl]
                dp_ref[rows, sl] = (dt[:, sl] * sv[g] * du_gelu[:, sl]).astype(dp_ref.dtype)
                dsx = dsv.astype(MXU_DTYPE)
                dw = lax.dot_general(dsx, vnx[g], (((1,), (1,)), ((), ())), preferred_element_type=F32)
                dw_ref[g] += jnp.where(tril, dw, 0.0)
                dbs_ref[g] += jnp.sum(dsv, axis=-1, keepdims=True)
                dvn_parts.append(lax.dot_general(wm[g], dsx, (((0,), (0,)), ((), ())), preferred_element_type=F32))
            dvn = jnp.concatenate(dvn_parts, axis=-1)
            dlg_ref[...] += jnp.sum(dvn * vh, axis=0, keepdims=True)
            dlb_ref[...] += jnp.sum(dvn, axis=0, keepdims=True)
            dvh = dvn * lng_ref[...]
            dzv = rs * (dvh - jnp.mean(dvh, axis=-1, keepdims=True)
                        - vh * jnp.mean(dvh * vh, axis=-1, keepdims=True))
            dp_ref[rows, D_TOK:] = (dzv * dv_gelu).astype(dp_ref.dtype)

    blk = lambda c: pl.BlockSpec((per_step * GM_CHUNK, D_TOK), lambda i: (i, c))
    vec = pl.BlockSpec((1, D_TOK), lambda i: (0, 0))
    cube = pl.BlockSpec((N_HEADS, GM_CHUNK, GM_CHUNK), lambda i: (0, 0, 0))
    col = pl.BlockSpec((N_HEADS, GM_CHUNK, 1), lambda i: (0, 0, 0))
    return pl.pallas_call(
        body, name=name, grid=(S // (per_step * GM_CHUNK),), in_specs=[blk(0), blk(1), vec, vec, cube, cube, blk(0)],
        out_specs=[pl.BlockSpec((per_step * GM_CHUNK, 2 * D_TOK), lambda i: (i, 0)), cube, col, vec, vec],
        out_shape=[jax.ShapeDtypeStruct((S, GM_IN), BF16), jax.ShapeDtypeStruct((N_HEADS, GM_CHUNK, GM_CHUNK), F32),
                   jax.ShapeDtypeStruct((N_HEADS, GM_CHUNK, 1), F32), jax.ShapeDtypeStruct((1, D_TOK), F32),
                   jax.ShapeDtypeStruct((1, D_TOK), F32)],
        compiler_params=_params(("arbitrary",)))(p, p, lng, lnb, ws, bsb, dheads)


def _chunk_tri(n, chunk, upper):
    r = lax.broadcasted_iota(jnp.int32, (n, n), 0)
    c = lax.broadcasted_iota(jnp.int32, (n, n), 1)
    same = (r // chunk) == (c // chunk)
    return jnp.where(same & ((r <= c) if upper else (r >= c)), 1.0, 0.0).astype(F32)


def _running_sum(tri, x):
    hi = x.astype(BF16)
    rest = x - hi.astype(F32)
    mid = rest.astype(BF16)
    lo = (rest - mid.astype(F32)).astype(BF16)
    tri = tri.astype(BF16)
    return (jnp.dot(tri, hi, preferred_element_type=F32) + jnp.dot(tri, mid, preferred_element_type=F32)
            + jnp.dot(tri, lo, preferred_element_type=F32))


MASKED = -1e30


def _pair_masks(mask_ref, n, upper):
    row = lax.broadcasted_iota(jnp.int32, (n, HEAD), 0)
    for i in range(n):
        mask_ref[i] = jnp.where((row <= i) if upper else (row >= i), 0.0, MASKED).astype(F32)


def _hg_gates(fz, lb):
    sg = _sigmoid(fz)
    f = lb + (1.0 - lb) * sg
    kk = (1.0 - lb) * (1.0 - sg)
    return sg, f, jnp.log(f), kk


def _hgrn2_fwd(p, lb, onorm, *, name):
    S = p.shape[0]
    C = HG_SUB
    tb = _tile(S, 256, C)
    nsub = tb // C

    def body(q_ref, fz_ref, v_ref, g_ref, lb_ref, on_ref, tok_ref, o_ref, st_ref, state, b_blk, k_blk, bsc, ksc, vsc):
        @pl.when(pl.program_id(0) == 0)
        def _():
            state[...] = jnp.zeros_like(state)

        _, _, lg, kk = _hg_gates(fz_ref[...], lb_ref[...])
        b_blk[...] = _running_sum(_chunk_tri(tb, C, False), lg)
        k_blk[...] = kk
        tt = lax.broadcasted_iota(jnp.int32, (C, HEAD), 0)

        def sub(c, carry):
            rows = pl.ds(pl.multiple_of(c * C, C), C)
            for h in range(N_HEADS):
                cols = slice(h * HEAD, (h + 1) * HEAD)
                qv = q_ref[rows, cols]
                vv = v_ref[rows, cols]
                b = b_blk[rows, cols]
                kk = k_blk[rows, cols]
                st0 = state[h]
                st0x = st0.astype(MXU_DTYPE)
                st_ref[c, h] = st0x.astype(st_ref.dtype)
                inter = lax.dot_general((qv * jnp.exp(b)).astype(MXU_DTYPE), st0x,
                                        (((1,), (1,)), ((), ())), preferred_element_type=F32)
                bsc[h] = b
                ksc[h] = kk
                vsc[h] = vv
                intra = jnp.zeros((C, HEAD), F32)
                for s in range(C):
                    dec = jnp.where(tt >= s, jnp.exp(b - bsc[h, pl.ds(s, 1), :]), 0.0)
                    a_s = jnp.sum(qv * ksc[h, pl.ds(s, 1), :] * dec, axis=-1, keepdims=True)
                    intra = intra + a_s * vsc[h, pl.ds(s, 1), :]
                o_ref[rows, cols] = inter + intra
                b_last = bsc[h, pl.ds(C - 1, 1), :]
                ke = kk * jnp.exp(b_last - b)
                state[h] = st0 * jnp.exp(b_last) + lax.dot_general(
                    vv.astype(MXU_DTYPE), ke.astype(MXU_DTYPE), (((0,), (0,)), ((), ())),
                    preferred_element_type=F32)
            return carry

        lax.fori_loop(0, nsub, sub, 0, unroll=2)

        for h in range(N_HEADS):
            cols = slice(h * HEAD, (h + 1) * HEAD)
            o = o_ref[:, cols]
            gv = g_ref[:, cols]
            n = o * lax.rsqrt(jnp.mean(o * o, axis=-1, keepdims=True) + EPS)
            tok_ref[:, cols] = (n * (gv * _sigmoid(gv)) * on_ref[:, cols]).astype(tok_ref.dtype)

    blk = lambda c: pl.BlockSpec((tb, D_TOK), lambda i, c=c: (i, c))
    vec = pl.BlockSpec((1, D_TOK), lambda i: (0, 0))
    stb = pl.BlockSpec((nsub, N_HEADS, HEAD, HEAD), lambda i: (i, 0, 0, 0))
    return pl.pallas_call(
        body, name=name, grid=(S // tb,), in_specs=[blk(0), blk(1), blk(2), blk(3), vec, vec],
        out_specs=[blk(0), blk(0), stb],
        out_shape=[jax.ShapeDtypeStruct((S, D_MODEL), BF16), jax.ShapeDtypeStruct((S, D_TOK), F32),
                   jax.ShapeDtypeStruct((S // C, N_HEADS, HEAD, HEAD), BF16)],
        scratch_shapes=[pltpu.VMEM((N_HEADS, HEAD, HEAD), F32)] + [pltpu.VMEM((tb, D_TOK), F32)] * 2
        + [pltpu.VMEM((N_HEADS, C, HEAD), F32)] * 3,
        compiler_params=_params(("arbitrary",)))(p, p, p, p, lb, onorm)


def _hgrn2_bwd(p, lb, onorm, o, states, dheads, *, name):
    S = p.shape[0]
    C = HG_SUB
    tb = _tile(S, 256, C)
    nsub = tb // C
    nblk = S // tb

    def body(q_ref, fz_ref, v_ref, g_ref, lb_ref, on_ref, o_ref, st_ref, dt_ref, dp_ref, dlb_ref, don_ref, dstate,
             b_blk, k_blk, do_blk, db_blk, dk_blk, dq_blk, dv_blk, bsc, ksc, vsc, qsc, dosc, causal, anti):
        @pl.when(pl.program_id(0) == 0)
        def _():
            dstate[...] = jnp.zeros_like(dstate)
            dlb_ref[...] = jnp.zeros_like(dlb_ref)
            don_ref[...] = jnp.zeros_like(don_ref)

        for h in range(N_HEADS):
            cols = slice(h * HEAD, (h + 1) * HEAD)
            onv = on_ref[:, cols]
            gv = g_ref[:, cols]
            ov = o_ref[:, cols]
            dt = dt_ref[:, cols].astype(F32)
            sgg = _sigmoid(gv)
            sil = gv * sgg
            rinv = lax.rsqrt(jnp.mean(ov * ov, axis=-1, keepdims=True) + EPS)
            n = ov * rinv
            don_ref[:, cols] += jnp.sum(dt * n * sil, axis=0, keepdims=True)
            dn = dt * sil * onv
            dp_ref[:, 3 * D_TOK + h * HEAD:3 * D_TOK + (h + 1) * HEAD] = (
                dt * n * onv * sgg * (1.0 + gv * (1.0 - sgg))).astype(dp_ref.dtype)
            do_blk[:, cols] = rinv * (dn - n * jnp.mean(dn * n, axis=-1, keepdims=True))
        _, _, lg, kk = _hg_gates(fz_ref[...], lb_ref[...])
        b_blk[...] = _running_sum(_chunk_tri(tb, C, False), lg)
        k_blk[...] = kk
        _pair_masks(causal, C, False)
        _pair_masks(anti, C, True)
        tt = lax.broadcasted_iota(jnp.int32, (C, HEAD), 0)

        def sub(j, heads):
            c = nsub - 1 - j
            rows = pl.ds(pl.multiple_of(c * C, C), C)
            for h in heads:
                cols = slice(h * HEAD, (h + 1) * HEAD)
                qv = q_ref[rows, cols]
                vv = v_ref[rows, cols]
                do = do_blk[rows, cols]
                b = b_blk[rows, cols]
                kk = k_blk[rows, cols]
                bsc[h] = b
                ksc[h] = kk
                vsc[h] = vv
                qsc[h] = qv
                dosc[h] = do
                b_last = bsc[h, pl.ds(C - 1, 1), :]
                eb = jnp.exp(b)
                qe = qv * eb
                ebb = jnp.exp(b_last - b)
                ke = kk * ebb
                e_last = jnp.exp(b_last)
                st0x = st_ref[c, h].astype(MXU_DTYPE)
                st0 = st0x.astype(F32)
                dst1 = dstate[h]
                dst1x = dst1.astype(MXU_DTYPE)
                dox = do.astype(MXU_DTYPE)
                dqe = jnp.dot(dox, st0x, preferred_element_type=F32)
                dke = jnp.dot(vv.astype(MXU_DTYPE), dst1x, preferred_element_type=F32)
                dv = lax.dot_general(ke.astype(MXU_DTYPE), dst1x, (((1,), (1,)), ((), ())),
                                     preferred_element_type=F32)
                db_last = (e_last * jnp.sum(st0 * dst1, axis=0, keepdims=True)
                           + jnp.sum(dke * ke, axis=0, keepdims=True))
                dstate[h] = dst1 * e_last + lax.dot_general(dox, qe.astype(MXU_DTYPE), (((0,), (0,)), ((), ())),
                                                            preferred_element_type=F32)
                dq_pairs = jnp.zeros((C, HEAD), F32)
                for s in range(C):
                    dec = jnp.exp(b - bsc[h, pl.ds(s, 1), :] + causal[s])
                    da_s = jnp.sum(do * vsc[h, pl.ds(s, 1), :], axis=-1, keepdims=True)
                    dq_pairs = dq_pairs + da_s * (ksc[h, pl.ds(s, 1), :] * dec)
                dk_pairs = jnp.zeros((C, HEAD), F32)
                for t in range(C):
                    do_t = dosc[h, pl.ds(t, 1), :]
                    qd = qsc[h, pl.ds(t, 1), :] * jnp.exp(bsc[h, pl.ds(t, 1), :] - b + anti[t])
                    da_t = jnp.sum(vv * do_t, axis=-1, keepdims=True)
                    dk_pairs = dk_pairs + da_t * qd
                    a_t = jnp.sum(qd * kk, axis=-1, keepdims=True)
                    dv = dv + a_t * do_t
                db = dqe * qe - dke * ke + qv * dq_pairs - kk * dk_pairs
                db_blk[rows, cols] = db + jnp.where(tt == C - 1, db_last, 0.0)
                dk_blk[rows, cols] = dke * ebb + dk_pairs
                dq_blk[rows, cols] = dqe * eb + dq_pairs
                dv_blk[rows, cols] = dv

        for first in range(0, N_HEADS, HG_GROUP):
            heads = tuple(range(first, first + HG_GROUP))
            pl.loop(0, nsub, unroll=2)(lambda j, heads=heads: sub(j, heads))

        dlg = _running_sum(_chunk_tri(tb, C, True), db_blk[...])
        lbv = lb_ref[...]
        sg, f, _, _ = _hg_gates(fz_ref[...], lbv)
        w = dlg / f - dk_blk[...]
        dp_ref[:, 0:D_TOK] = dq_blk[...].astype(dp_ref.dtype)
        dp_ref[:, 2 * D_TOK:3 * D_TOK] = dv_blk[...].astype(dp_ref.dtype)
        dp_ref[:, D_TOK:2 * D_TOK] = (w * (1.0 - lbv) * sg * (1.0 - sg)).astype(dp_ref.dtype)
        dlb_ref[...] += jnp.sum(w * (1.0 - sg), axis=0, keepdims=True)

    blk = lambda c: pl.BlockSpec((tb, D_TOK), lambda i, c=c: (nblk - 1 - i, c))
    vec = pl.BlockSpec((1, D_TOK), lambda i: (0, 0))
    stb = pl.BlockSpec((nsub, N_HEADS, HEAD, HEAD), lambda i: (nblk - 1 - i, 0, 0, 0))
    small = jax.ShapeDtypeStruct((1, D_TOK), F32)
    return pl.pallas_call(
        body, name=name, grid=(nblk,), in_specs=[blk(0), blk(1), blk(2), blk(3), vec, vec, blk(0), stb, blk(0)],
        out_specs=[pl.BlockSpec((tb, 4 * D_TOK), lambda i: (nblk - 1 - i, 0)), vec, vec],
        out_shape=[jax.ShapeDtypeStruct((S, HG_IN), BF16), small, small],
        scratch_shapes=[pltpu.VMEM((N_HEADS, HEAD, HEAD), F32)] + [pltpu.VMEM((tb, D_TOK), F32)] * 7
        + [pltpu.VMEM((N_HEADS, C, HEAD), F32)] * 5 + [pltpu.VMEM((C, C, HEAD), F32)] * 2,
        compiler_params=_params(("arbitrary",)))(p, p, p, p, lb, onorm, o, states, dheads)


def _adamw(w, g, m, v, *, name):
    shape = w.shape
    cols = shape[-1]
    w2, g2, m2, v2 = (t.reshape(-1, cols) for t in (w, g, m, v))
    R = w2.shape[0]
    tr = _tile(R, 512, 8)

    def body(w_ref, g_ref, m_ref, v_ref, d_ref, nm_ref, nv_ref):
        gv = g_ref[...]
        nm = ADAM_B1 * m_ref[...] + (1.0 - ADAM_B1) * gv
        nv = ADAM_B2 * v_ref[...] + (1.0 - ADAM_B2) * (gv * gv)
        m_hat = nm / (1.0 - ADAM_B1 ** ADAM_STEP)
        v_hat = nv / (1.0 - ADAM_B2 ** ADAM_STEP)
        d_ref[...] = -ADAM_LR * (m_hat / (jnp.sqrt(v_hat) + ADAM_EPS) + ADAM_WD * w_ref[...])
        nm_ref[...] = nm
        nv_ref[...] = nv

    spec = pl.BlockSpec((tr, cols), lambda i: (i, 0))
    out = jax.ShapeDtypeStruct((R, cols), F32)
    d, nm, nv = pl.pallas_call(body, name=name, grid=(R // tr,), in_specs=[spec] * 4, out_specs=[spec] * 3,
                               out_shape=[out] * 3, compiler_params=_params(("parallel",)))(w2, g2, m2, v2)
    return d.reshape(shape), nm.reshape(shape), nv.reshape(shape)


def _add_received(sent, got, me, *, name):
    n, R, Cc = got.shape
    tr = _tile(R, 256, 16)
    per = R // tr

    def body(me_ref, a_ref, b_ref, o_ref):
        del me_ref
        acc = a_ref[...].astype(F32)
        for k in range(n):
            acc = acc + b_ref[k].astype(F32)
        o_ref[...] = acc

    grid_spec = pltpu.PrefetchScalarGridSpec(
        num_scalar_prefetch=1, grid=(per,),
        in_specs=[pl.BlockSpec((tr, Cc), lambda i, me_ref: (me_ref[0] * per + i, 0)),
                  pl.BlockSpec((n, tr, Cc), lambda i, me_ref: (0, i, 0))],
        out_specs=pl.BlockSpec((tr, Cc), lambda i, me_ref: (i, 0)))
    return pl.pallas_call(body, name=name, grid_spec=grid_spec, out_shape=jax.ShapeDtypeStruct((R, Cc), F32),
                          compiler_params=_params(("parallel",)))(jnp.reshape(me, (1,)).astype(jnp.int32), sent, got)


def _sum_blocks(x, *, name):
    n, R, Cc = x.shape
    tr = _tile(R, 208, 8)

    def body(x_ref, o_ref):
        acc = x_ref[0]
        for k in range(1, n):
            acc = acc + x_ref[k]
        o_ref[...] = acc

    return pl.pallas_call(
        body, name=name, grid=(R // tr,), in_specs=[pl.BlockSpec((n, tr, Cc), lambda i: (0, i, 0))],
        out_specs=pl.BlockSpec((tr, Cc), lambda i: (i, 0)), out_shape=jax.ShapeDtypeStruct((R, Cc), F32),
        compiler_params=_params(("parallel",)))(x)


def _place():
    return lax.axis_index("x"), lax.axis_index("y"), lax.axis_index("c")


def _all_gather(x, *, name, in_vmem, reduce_sum=False, with_token=False):
    R, Cc = x.shape
    space = pltpu.VMEM if in_vmem else pl.ANY

    def body(x_ref, out_ref, *scratch):
        if with_token:
            scratch[0][...] = jnp.zeros_like(scratch[0])
            scratch = scratch[1:]
        if reduce_sum:
            gat_ref, send_sems, recv_sems, local_sem = scratch
        else:
            gat_ref = out_ref
            send_sems, recv_sems, local_sem = scratch
        mx, my, mc = _place()
        me, sibling = (mx, my, mc), (mx, my, 1 - mc)
        chips = [(1 - mx, my), (mx, 1 - my), (1 - mx, 1 - my)]

        def rows(px, py, pc):
            return gat_ref.at[pl.ds((4 * px + 2 * py + pc) * R, R), :]

        def copy(k, block, to, src=None):
            return pltpu.make_async_remote_copy(
                src_ref=rows(*block) if src is None else src, dst_ref=rows(*block), send_sem=send_sems.at[k],
                recv_sem=recv_sems.at[k], device_id=to, device_id_type=MESH_ID)

        mine = pltpu.make_async_copy(x_ref, rows(*me), local_sem)
        mine.start()
        first = [copy(0, me, sibling, src=x_ref)]
        first += [copy(1 + j, me, (*chip, mc), src=x_ref) for j, chip in enumerate(chips)]
        for cp in first:
            cp.start()
        passed = [copy(4 + j, (*chip, mc), sibling) for j, chip in enumerate(chips)]
        for j, chip in enumerate(chips):
            copy(1 + j, (*chip, mc), me).wait_recv()
            passed[j].start()
        copy(0, sibling, me).wait_recv()
        for j, chip in enumerate(chips):
            copy(4 + j, (*chip, 1 - mc), me).wait_recv()
        for cp in first + passed:
            cp.wait_send()
        mine.wait()
        if reduce_sum:
            acc = gat_ref[pl.ds(0, R), :]
            for d in range(1, N_DEV):
                acc = acc + gat_ref[pl.ds(d * R, R), :]
            out_ref[...] = acc

    sems = [pltpu.SemaphoreType.DMA((7,)), pltpu.SemaphoreType.DMA((7,)), pltpu.SemaphoreType.DMA]
    if reduce_sum:
        assert in_vmem
        out_shape = jax.ShapeDtypeStruct((R, Cc), x.dtype)
        scratch = [pltpu.VMEM((N_DEV * R, Cc), x.dtype)] + sems
    else:
        out_shape = jax.ShapeDtypeStruct((N_DEV * R, Cc), x.dtype)
        scratch = sems
    out_specs = pl.BlockSpec(memory_space=space)
    if with_token:
        out_shape = (out_shape, jax.ShapeDtypeStruct((8, LANE), F32))
        out_specs = (out_specs, pl.BlockSpec(memory_space=pltpu.VMEM))
    return pl.pallas_call(
        body, name=name, out_shape=out_shape, in_specs=[pl.BlockSpec(memory_space=space)], out_specs=out_specs,
        scratch_shapes=scratch, compiler_params=pltpu.CompilerParams(vmem_limit_bytes=VMEM_LIMIT))(x)


def _peer(k, mx, my, mc):
    bits = k + 1
    return (1 - mx if bits & 4 else mx, 1 - my if bits & 2 else my, 1 - mc if bits & 1 else mc)


HBM_SPEC = pl.BlockSpec(memory_space=pltpu.HBM)
SEM_SPEC = pl.BlockSpec(memory_space=pltpu.SEMAPHORE)
DATAFLOW = pltpu.SideEffectType.DATAFLOW_SIDE_EFFECTING


def _exchange_copies(x_refs, land_refs, send_sems, recv_sems, scatter):
    mx, my, mc = _place()
    me = 4 * mx + 2 * my + mc
    n = len(x_refs)
    copies = []
    for k in range(N_DEV - 1):
        px, py, pc = _peer(k, mx, my, mc)
        for m, (x_ref, land_ref) in enumerate(zip(x_refs, land_refs)):
            rows = land_ref.shape[1] if scatter else x_ref.shape[0]
            if scatter:
                src = x_ref.at[pl.ds(pl.multiple_of((4 * px + 2 * py + pc) * rows, 16), rows), :]
                dst = land_ref.at[k]
            else:
                src = x_ref
                dst = land_ref.at[pl.ds(pl.multiple_of(me * rows, 16), rows), :]
            copies.append(pltpu.make_async_remote_copy(
                src_ref=src, dst_ref=dst, send_sem=send_sems.at[k * n + m], recv_sem=recv_sems.at[k * n + m],
                device_id=(px, py, pc), device_id_type=MESH_ID))
    return copies


def _land_shape(x, scatter):
    return (N_DEV - 1, x.shape[0] // N_DEV, x.shape[1]) if scatter else (N_DEV * x.shape[0], x.shape[1])


def _own_copies(x_refs, land_refs, local_sems):
    mx, my, mc = _place()
    me = 4 * mx + 2 * my + mc
    return [pltpu.make_async_copy(
        x_ref, land_ref.at[pl.ds(pl.multiple_of(me * x_ref.shape[0], 16), x_ref.shape[0]), :], local_sems.at[m])
        for m, (x_ref, land_ref) in enumerate(zip(x_refs, land_refs))]


def _exchange_start(groups, *, name, scatter):
    sizes = [len(g) for g in groups]
    xs = [x for g in groups for x in g]
    n = len(xs)
    lands = [lax.empty(_land_shape(x, scatter), x.dtype) for x in xs]
    per = 2 if scatter else 3

    def body(*refs):
        sems = refs[2 * n:2 * n + per * len(groups)]
        token = refs[-1]
        off = 0
        for gi, m in enumerate(sizes):
            x_refs, land_refs = refs[off:off + m], refs[n + off:n + off + m]
            for cp in _exchange_copies(x_refs, land_refs, sems[per * gi], sems[per * gi + 1], scatter):
                cp.start()
            if not scatter:
                for cp in _own_copies(x_refs, land_refs, sems[per * gi + 2]):
                    cp.start()
            off += m
        token[...] = jnp.zeros_like(token)

    sem_shapes = []
    for m in sizes:
        sem_shapes += [pltpu.SemaphoreType.DMA(((N_DEV - 1) * m,))] * 2
        if not scatter:
            sem_shapes.append(pltpu.SemaphoreType.DMA((m,)))
    ns = len(sem_shapes)
    out = pl.pallas_call(
        body, name=name,
        out_shape=(*sem_shapes, *[pltpu.HBM(x.shape, x.dtype) for x in xs],
                   *[pltpu.HBM(l.shape, l.dtype) for l in lands], jax.ShapeDtypeStruct((8, LANE), F32)),
        in_specs=(HBM_SPEC,) * (2 * n),
        out_specs=(SEM_SPEC,) * ns + (HBM_SPEC,) * (2 * n) + (pl.BlockSpec(memory_space=pltpu.VMEM),),
        input_output_aliases={i: ns + i for i in range(2 * n)},
        compiler_params=pltpu.CompilerParams(has_side_effects=DATAFLOW))(
            *[pltpu.with_memory_space_constraint(t, pltpu.HBM) for t in xs + lands])
    started, off = [], 0
    for gi, m in enumerate(sizes):
        sems = out[per * gi:per * gi + per]
        started.append((sems[0], sems[1], list(out[ns + off:ns + off + m]),
                        list(out[ns + n + off:ns + n + off + m]), out[-1], None if scatter else sems[2]))
        off += m
    return started


def _exchange_wait(started, after, *, name, scatter):
    send_sems, recv_sems, xs, lands, _, local_sems = started
    n = len(xs)

    def body(*refs):
        x_refs, land_refs = refs[:n], refs[n:2 * n]
        for cp in _exchange_copies(x_refs, land_refs, refs[2 * n], refs[2 * n + 1], scatter):
            cp.wait_send()
            cp.wait_recv()
        if not scatter:
            for cp in _own_copies(x_refs, land_refs, refs[2 * n + 2]):
                cp.wait()

    sems = (send_sems, recv_sems) if scatter else (send_sems, recv_sems, local_sems)
    out = pl.pallas_call(
        body, name=name, out_shape=tuple(pltpu.HBM(t.shape, t.dtype) for t in xs + lands),
        in_specs=(HBM_SPEC,) * (2 * n) + (SEM_SPEC,) * len(sems) + (pl.BlockSpec(memory_space=pl.ANY),),
        out_specs=(HBM_SPEC,) * (2 * n), input_output_aliases={i: i for i in range(2 * n)},
        compiler_params=pltpu.CompilerParams(has_side_effects=DATAFLOW))(*xs, *lands, *sems, after)
    return list(out[:n]), list(out[n:])


def _gather_start(groups, token, *, name):
    first = groups[0]
    groups = [[first[0] + token[0, 0].astype(first[0].dtype)] + list(first[1:])] + [list(g) for g in groups[1:]]
    return _exchange_start(groups, name=name, scatter=False)


def _gather_finish(started, after, *, name):
    return _exchange_wait(started, after, name=name, scatter=False)[1]


def _reduce_start(grads, *, name):
    return _exchange_start([grads], name=name, scatter=True)[0]


def _reduce_finish(started, after, me, *, name):
    sent, gots = _exchange_wait(started, after, name=name + "_wait", scatter=True)
    return [_add_received(g, got, me, name=f"{name}_add{m}") for m, (g, got) in enumerate(zip(sent, gots))]


def _pad_rows(a, mult):
    r = (-a.shape[0]) % mult
    return a if r == 0 else jnp.concatenate([a, jnp.zeros((r,) + a.shape[1:], a.dtype)], axis=0)


def kernel(x, mem, mix_norm, mem_norm, w_mem_kv, w_out, hg_w_in, hg_lb, hg_onorm, gm_w_in, gm_ln_g, gm_ln_b, gm_ws, gm_bs, ffn_norm, w_ffn_in, w_ffn_out, final_norm, loss_target, m_mix_norm, m_mem_norm, m_w_mem_kv, m_w_out, m_hg_w_in, m_hg_lb, m_hg_onorm, m_gm_w_in, m_gm_ln_g, m_gm_ln_b, m_gm_ws, m_gm_bs, m_ffn_norm, m_w_ffn_in, m_w_ffn_out, m_final_norm, v_mix_norm, v_mem_norm, v_w_mem_kv, v_w_out, v_hg_w_in, v_hg_lb, v_hg_onorm, v_gm_w_in, v_gm_ln_g, v_gm_ln_b, v_gm_ws, v_gm_bs, v_ffn_norm, v_w_ffn_in, v_w_ffn_out, v_final_norm):
    mx, my, mc = _place()
    me = 4 * mx + 2 * my + mc
    xs = x[0]
    mems = mem[0]
    tgt = loss_target[0]

    hg_t = hg_w_in[0].T.astype(BF16)
    gm_t = gm_w_in[0].T.astype(BF16)
    fi_t = [w_ffn_in[i].T.astype(BF16) for i in range(2)]
    kv_b = [w_mem_kv[i].astype(BF16) for i in range(2)]
    out_b = [w_out[i].astype(BF16) for i in range(2)]
    fo_b = [w_ffn_out[i].astype(BF16) for i in range(2)]
    W_hgT, token = _all_gather(hg_t, name="gather_first", in_vmem=False, with_token=True)
    ln_local = _pad_rows(jnp.concatenate([gm_ln_g, gm_ln_b], axis=0), 16)
    ln_local = jnp.concatenate([ln_local, jnp.zeros((16, LANE - ln_local.shape[1]), F32)], axis=1)
    gather_mix, fi0, fo0, gather_gm, fi1, fo1 = _gather_start(
        [kv_b + out_b, [fi_t[0]], [fo_b[0]], [gm_t, ln_local], [fi_t[1]], [fo_b[1]]], token,
        name="gather_rest_start")
    gather_fi, gather_fo = [fi0, fi1], [fo0, fo1]

    lb_soft = jax.nn.softmax(hg_lb, axis=0)
    lb0 = lb_soft[0:1]
    bsb = jnp.broadcast_to(gm_bs[0][:, :, None], (N_HEADS, GM_CHUNK, GM_CHUNK))
    ws = gm_ws[0]

    W_fiT, W_fo = [], []

    def ffn_fwd(xin, hf, i, **tail):
        W_fiT.extend(_gather_finish(gather_fi[i], hf, name=f"gather_fi{i}_wait"))
        gu, act = _ffn_in(hf, W_fiT[i], name=f"ffn_in{i}")
        W_fo.extend(_gather_finish(gather_fo[i], act, name=f"gather_fo{i}_wait"))
        return gu, act, _matmul(act, W_fo[i], res=xin, name=f"ffn_out{i}", **tail)

    h0 = _rms_fwd(xs, mix_norm[0:1], name="mix_norm0", dep=gather_mix[4])
    p0 = _matmul(h0, W_hgT, tb=True, name="hg_in")
    heads0, o0, states = _hgrn2_fwd(p0, lb0, hg_onorm, name="hgrn2_fwd")

    kv0, kv1, wo0, wo1 = _gather_finish(gather_mix, o0, name="gather_mix_wait")
    W_kv, W_out = [kv0, kv1], [wo0, wo1]
    mem_n, kv = [], []
    for i in range(2):
        mn = _rms_fwd(mems, mem_norm[i:i + 1], name=f"mem_norm{i}")
        mem_n.append(mn)
        kv.append(_matmul(mn, W_kv[i], name=f"mem_kv{i}"))

    heads0 = _attn_fwd(p0, 4 * D_TOK // D_MEM, kv[0], heads0, name="attn_fwd0")
    x1, hf0 = _matmul(heads0, W_out[0], res=xs, norm_gain=ffn_norm[0:1], name="out_proj0")
    gu0, act0, (x2, h1) = ffn_fwd(x1, hf0, 0, norm_gain=mix_norm[1:2])

    W_gmT, ln_all = _gather_finish(gather_gm, h1, name="gather_gm_wait")
    ln_all = ln_all.reshape(N_DEV, 16, LANE)
    ln_g = ln_all[:, 0, :D_TOK // N_DEV].reshape(1, D_TOK)
    ln_b = ln_all[:, 1, :D_TOK // N_DEV].reshape(1, D_TOK)
    p1 = _matmul(h1, W_gmT, tb=True, name="gm_in")
    heads1 = _gmlp_fwd(p1, ln_g, ln_b, ws, bsb, name="gmlp_fwd")
    heads1 = _attn_fwd(p1, 2 * D_TOK // D_MEM, kv[1], heads1, name="attn_fwd1")
    x3, hf1 = _matmul(heads1, W_out[1], res=x2, norm_gain=ffn_norm[1:2], name="out_proj1")
    gu1, act1, (dx, g_final, loss_part) = ffn_fwd(x3, hf1, 1, loss_head=(final_norm.reshape(1, D_MODEL), tgt))

    def ffn_bwd(dx, xin, hf, gu, act, i, dep):
        dgu = _ffn_out_dx(dx, W_fo[i], gu, dep, name=f"ffn_out_dx{i}")
        g_wfo = _matmul(act, dx, ta=True, out_dtype=BF16, name=f"ffn_out_dw{i}")
        g_wfi_t = _matmul(dgu, hf, ta=True, a_halves=True, out_dtype=BF16, name=f"ffn_in_dw{i}")
        dx, g_norm = _matmul(dgu, W_fiT[i], a_halves=True, res=dx, norm_bwd=(xin, ffn_norm[i:i + 1]),
                             name=f"ffn_in_dx{i}")
        return dx, g_wfi_t, g_wfo, g_norm

    def mem_bwd(dkv, i):
        g_wkv = _matmul(mem_n[i], dkv, ta=True, out_dtype=BF16, name=f"mem_kv_dw{i}")
        dmn = _matmul(dkv, W_kv[i], tb=True, name=f"mem_kv_dx{i}")
        _, g_norm = _rms_bwd(mems, mem_norm[i:i + 1], dmn, jnp.zeros_like(mems), name=f"mem_norm_bwd{i}")
        return g_wkv, g_norm

    dx, g_wfi1_t, g_wfo1, g_ffn1 = ffn_bwd(dx, x3, hf1, gu1, act1, 1, loss_part)
    dheads = _matmul(dx, W_out[1], tb=True, name="out_proj_dx1")
    g_wout1 = _matmul(heads1, dx, ta=True, out_dtype=BF16, name="out_proj_dw1")
    dp, g_ws, g_bs, g_lng, g_lnb = _gmlp_bwd(p1, ln_g, ln_b, ws, bsb, dheads, name="gmlp_bwd")
    dp, dk, dv = _attn_bwd(p1, 2 * D_TOK // D_MEM, kv[1], dheads, dp, name="attn_bwd1")
    g_wkv1, g_mem1 = mem_bwd(jnp.concatenate([dk, dv], axis=1), 1)
    g_wgm_t = _matmul(dp, h1, ta=True, out_dtype=BF16, name="gm_in_dw")
    dx, g_mix1 = _matmul(dp, W_gmT, res=dx, norm_bwd=(x2, mix_norm[1:2]), name="gm_in_dx")
    reduce_l1 = _reduce_start([g_wkv1, g_wout1, g_wgm_t, g_wfi1_t, g_wfo1], name="reduce_l1_start")
    early = [loss_part, g_final, g_ffn1, g_mix1, g_mem1, g_ws.reshape(1, -1), g_bs.reshape(1, -1), g_lng, g_lnb]
    early_rows = _pad_rows(jnp.concatenate(early, axis=1).reshape(-1, LANE), 16) + reduce_l1[4][0, 0]
    small_early = _exchange_start([[early_rows]], name="reduce_small_start", scatter=False)[0]

    dx, g_wfi0_t, g_wfo0, g_ffn0 = ffn_bwd(dx, x1, hf0, gu0, act0, 0, small_early[4])
    reduce_ffn0 = _reduce_start([g_wfi0_t, g_wfo0], name="reduce_ffn0_start")
    dheads = _matmul(dx, W_out[0], tb=True, name="out_proj_dx0", dep=reduce_ffn0[4])
    g_wout0 = _matmul(heads0, dx, ta=True, out_dtype=BF16, name="out_proj_dw0")
    dp, g_lb0, g_onorm = _hgrn2_bwd(p0, lb0, hg_onorm, o0, states, dheads, name="hgrn2_bwd")
    dp, dk, dv = _attn_bwd(p0, 4 * D_TOK // D_MEM, kv[0], dheads, dp, name="attn_bwd0")
    g_wkv0, g_mem0 = mem_bwd(jnp.concatenate([dk, dv], axis=1), 0)
    g_whg_t = _matmul(dp, h0, ta=True, out_dtype=BF16, name="hg_in_dw")
    reduce_mix0 = _reduce_start([g_wkv0, g_wout0, g_whg_t], name="reduce_mix0_start")
    grad_x, g_mix0 = _matmul(dp, W_hgT, res=dx, norm_bwd=(xs, mix_norm[0:1]), name="hg_in_dx", dep=reduce_mix0[4])

    g_kv1, g_out1, g_gm_t, g_fi1_t, g_fo1 = _reduce_finish(reduce_l1, grad_x, me, name="reduce_l1")
    g_fi0_t, g_fo0 = _reduce_finish(reduce_ffn0, g_kv1, me, name="reduce_ffn0")
    g_kv0, g_out0, g_hg_t = _reduce_finish(reduce_mix0, g_fi0_t, me, name="reduce_mix0")
    g_shards = [jnp.stack([g_kv0, g_kv1]), jnp.stack([g_out0, g_out1]), g_hg_t[None], g_gm_t[None],
                jnp.stack([g_fi0_t, g_fi1_t]), jnp.stack([g_fo0, g_fo1])]
    transposed = (4, 7, 13)

    late = [g_ffn0, g_lb0, g_onorm, g_mem0, g_mix0]
    late_rows = _pad_rows(jnp.concatenate(late, axis=1).reshape(-1, LANE), 8)
    red_late = _all_gather(late_rows, name="reduce_small_late", in_vmem=True, reduce_sum=True)
    gathered = _exchange_wait(small_early, red_late, name="reduce_small_wait", scatter=False)[1][0]
    red_early = _sum_blocks(gathered.reshape(N_DEV, -1, LANE), name="reduce_small_sum")

    def split(flat, parts):
        out, off = [], 0
        for t in parts:
            out.append(flat[off:off + t.shape[1]])
            off += t.shape[1]
        return out

    r_loss, r_final, r_ffn1, r_mix1, r_mem1, r_ws, r_bs, r_lng, r_lnb = split(red_early.reshape(-1), early)
    r_ffn0, r_lb0, r_onorm, r_mem0, r_mix0 = split(red_late.reshape(-1), late)
    loss = r_loss[0]
    g_mix_norm = jnp.stack([r_mix0, r_mix1])
    g_mem_norm = jnp.stack([r_mem0, r_mem1])
    g_hg_lb = r_lb0[None, :] * lb0 * (jnp.eye(3, dtype=F32)[:, 0:1] - lb_soft)
    g_hg_onorm = r_onorm.reshape(1, D_TOK)
    width = D_TOK // N_DEV
    g_gm_ln_g = lax.dynamic_slice(r_lng, (me * width,), (width,)).reshape(1, width)
    g_gm_ln_b = lax.dynamic_slice(r_lnb, (me * width,), (width,)).reshape(1, width)
    g_gm_ws = r_ws.reshape(gm_ws.shape)
    g_gm_bs = r_bs.reshape(gm_bs.shape)
    g_ffn_norm = jnp.stack([r_ffn0, r_ffn1])
    g_final_norm = r_final

    grads = [g_mix_norm, g_mem_norm, g_shards[0], g_shards[1], g_shards[2], g_hg_lb, g_hg_onorm, g_shards[3],
             g_gm_ln_g, g_gm_ln_b, g_gm_ws, g_gm_bs, g_ffn_norm, g_shards[4], g_shards[5], g_final_norm]
    weights = [mix_norm, mem_norm, w_mem_kv, w_out, hg_w_in, hg_lb, hg_onorm, gm_w_in, gm_ln_g, gm_ln_b, gm_ws, gm_bs,
               ffn_norm, w_ffn_in, w_ffn_out, final_norm]
    ms = [m_mix_norm, m_mem_norm, m_w_mem_kv, m_w_out, m_hg_w_in, m_hg_lb, m_hg_onorm, m_gm_w_in, m_gm_ln_g,
          m_gm_ln_b, m_gm_ws, m_gm_bs, m_ffn_norm, m_w_ffn_in, m_w_ffn_out, m_final_norm]
    vs = [v_mix_norm, v_mem_norm, v_w_mem_kv, v_w_out, v_hg_w_in, v_hg_lb, v_hg_onorm, v_gm_w_in, v_gm_ln_g,
          v_gm_ln_b, v_gm_ws, v_gm_bs, v_ffn_norm, v_w_ffn_in, v_w_ffn_out, v_final_norm]
    deltas, new_m, new_v = [], [], []
    for n, (w, g, m, v) in enumerate(zip(weights, grads, ms, vs)):
        if w.ndim == 1:
            d, nm, nv = _adamw(w[None], g.reshape(1, -1), m[None], v[None], name=f"adamw{n}")
            d, nm, nv = d[0], nm[0], nv[0]
        elif n in transposed:
            flip = lambda t: jnp.swapaxes(t, 1, 2)
            d, nm, nv = (flip(t) for t in _adamw(flip(w), g, flip(m), flip(v), name=f"adamw{n}"))
            grads[n] = flip(g)
        else:
            d, nm, nv = _adamw(w, g.reshape(w.shape), m, v, name=f"adamw{n}")
        deltas.append(d)
        new_m.append(nm)
        new_v.append(nv)
    grads = [g.reshape(w.shape) for g, w in zip(grads, weights)]
    return (loss, grad_x[None], *grads, *deltas, *new_m, *new_v)
```
